```python
import jax, jax.numpy as jnp
from jax import lax
import numpy as np

D_MODEL = 1024
BATCH = 8
SEQ = 2048
DEPTH = 1
DEC_BATCH = 32
DEC_SEQ = 64
PAST_LEN = 2048

CHUNK = 64
Q_BLOCK = 128
MIX_WIDTH = D_MODEL // 2
H_A = 4
DV_A = MIX_WIDTH // H_A
DK_A = DV_A // 2
QK_A = H_A * DK_A
GATE_RANK = 16
GATE_TAU = 16.0
H_B = 8
DH_B = MIX_WIDTH // H_B
N_BRANCH = 2
N_GROUPS = 4
EXPERTS_PER_GROUP = 8
N_EXPERTS = N_GROUPS * EXPERTS_PER_GROUP
TOP_K = 2
D_EXPERT = D_MODEL // 2
MOE_BLOCK = 128
RMS_EPS = 1e-6
SPLIT_POINTS = (QK_A, 2 * QK_A, 2 * QK_A + MIX_WIDTH, 2 * QK_A + 2 * MIX_WIDTH,
                2 * QK_A + 2 * MIX_WIDTH + GATE_RANK, 2 * QK_A + 3 * MIX_WIDTH + GATE_RANK,
                2 * QK_A + 4 * MIX_WIDTH + GATE_RANK, 2 * QK_A + 5 * MIX_WIDTH + GATE_RANK)
IN_WIDTH = 2 * QK_A + 5 * MIX_WIDTH + GATE_RANK + N_BRANCH * D_MODEL

kernel_name = 'gla_stickbreak_hmoe_stream_step'


def rms_norm(x, gain):
    xf = x.astype(jnp.float32)
    xf = xf * lax.rsqrt(jnp.mean(xf * xf, axis=-1, keepdims=True) + RMS_EPS)
    return (xf * gain.astype(jnp.float32)).astype(x.dtype)


def gla_recurrence(q, k, v, log_a, s0):
    B, S, H, DK = q.shape
    DV = v.shape[-1]
    C = min(CHUNK, S)
    N = S // C
    f32 = jnp.float32

    def chunks(t):
        return jnp.moveaxis(t.astype(f32).reshape(B, N, C, *t.shape[2:]), 1, 0)

    causal = jnp.tril(jnp.ones((C, C), dtype=bool))[None, :, :, None, None]

    def step(state, inp):
        qc, kc, vc, gc = inp
        b = jnp.cumsum(gc, axis=1)
        b_last = b[:, -1]
        inter = jnp.einsum('bthk,bhkv->bthv', qc * jnp.exp(b), state)
        decay = jnp.exp(jnp.where(causal, b[:, :, None] - b[:, None, :], -jnp.inf))
        att = jnp.einsum('bthk,bshk,btshk->bhts', qc, kc, decay)
        intra = jnp.einsum('bhts,bshv->bthv', att, vc)
        state = state * jnp.exp(b_last)[..., None] + jnp.einsum(
            'bshk,bshv->bhkv', kc * jnp.exp(b_last[:, None] - b), vc)
        return state, inter + intra

    s_fin, out = lax.scan(step, s0.astype(f32), (chunks(q), chunks(k), chunks(v), chunks(log_a)))
    out = jnp.moveaxis(out, 0, 1).reshape(B, S, H, DV)
    return out, s_fin


def stick_breaking(q, k, v, q_pos, k_pos):
    B, Sq, H, D = q.shape
    blk = min(Q_BLOCK, Sq)
    nb = Sq // blk
    qb = jnp.moveaxis(q.reshape(B, nb, blk, H, D), 1, 0)
    pb = q_pos.reshape(nb, blk)
    scale = D ** -0.5

    def one_block(args):
        qc, pc = args
        z = jnp.einsum('bqhd,bkhd->bhqk', qc, k).astype(jnp.float32) * scale
        mask = (k_pos[None, :] < pc[:, None])[None, None]
        log_beta = jax.nn.log_sigmoid(z)
        log_keep = jnp.where(mask, jax.nn.log_sigmoid(-z), 0.0)
        between = lax.cumsum(log_keep, axis=3, reverse=True) - log_keep
        w = jnp.where(mask, jnp.exp(log_beta + between), 0.0)
        return jnp.einsum('bhqk,bkhd->bqhd', w.astype(v.dtype), v)

    out = lax.map(one_block, (qb, pb))
    return jnp.moveaxis(out, 0, 1).reshape(B, Sq, H, D)


def routed_experts(h, expert_idx, expert_w, w_gate, w_up, w_down):
    T = h.shape[0]
    A = T * TOP_K
    n_blocks = -(-(A + N_EXPERTS * (MOE_BLOCK - 1)) // MOE_BLOCK)
    P = n_blocks * MOE_BLOCK
    flat_e = expert_idx.reshape(A)
    order = jnp.argsort(flat_e)
    sorted_e = flat_e[order]
    counts = jnp.bincount(flat_e, length=N_EXPERTS)
    padded = (counts + MOE_BLOCK - 1) // MOE_BLOCK * MOE_BLOCK
    pad_end = jnp.cumsum(padded)
    pad_start = pad_end - padded
    start = jnp.cumsum(counts) - counts
    dest = pad_start[sorted_e] + jnp.arange(A) - start[sorted_e]
    tok_sorted = order // TOP_K
    slot_tok = jnp.zeros((P,), jnp.int32).at[dest].set(tok_sorted.astype(jnp.int32))
    block_e = jnp.minimum(jnp.searchsorted(pad_end, jnp.arange(n_blocks) * MOE_BLOCK, side='right'),
                          N_EXPERTS - 1)
    xb = h[slot_tok].reshape(n_blocks, MOE_BLOCK, h.shape[1])

    def run_block(args):
        xblk, e = args
        return (jax.nn.silu(xblk @ w_gate[e]) * (xblk @ w_up[e])) @ w_down[e]

    yb = lax.map(run_block, (xb, block_e)).reshape(P, h.shape[1])
    y_sorted = yb[dest] * expert_w.reshape(A)[order][:, None]
    return jax.ops.segment_sum(y_sorted, tok_sorted, num_segments=T).astype(h.dtype)


def hier_moe(h, w_router_group, b_router_group, w_router_expert, b_router_expert, w_exp_gate, w_exp_up, w_exp_down):
    T = h.shape[0]
    g_prob = jax.nn.softmax((h @ w_router_group).astype(jnp.float32) + b_router_group, axis=-1)
    g_p, g_idx = lax.top_k(g_prob, 1)
    e_logits = ((h @ w_router_expert).astype(jnp.float32) + b_router_expert).reshape(T, N_GROUPS, EXPERTS_PER_GROUP)
    e_logits = jnp.take_along_axis(e_logits, g_idx[:, :, None], axis=1)[:, 0]
    e_p, e_idx = lax.top_k(jax.nn.softmax(e_logits, axis=-1), TOP_K)
    weights = g_p * (e_p / jnp.sum(e_p, axis=-1, keepdims=True))
    experts = g_idx * EXPERTS_PER_GROUP + e_idx
    return routed_experts(h, experts, weights, w_exp_gate, w_exp_up, w_exp_down)


def trunk_layer(x, gla_s0, k_past, v_past, norm_mix_gain, w_in, w_gla_gate_up, b_gla_gate, gla_norm_gain,
                w_branch, w_out, norm_ffn_gain, w_router_group, b_router_group, w_router_expert,
                b_router_expert, w_exp_gate, w_exp_up, w_exp_down):
    B, S, _ = x.shape
    past = 0 if k_past is None else k_past.shape[1]
    h = rms_norm(x, norm_mix_gain)
    q_a, k_a, v_a, r_a, lr_a, q_b, k_b, v_b, g_br = jnp.split(h @ w_in, SPLIT_POINTS, axis=-1)

    log_a = jax.nn.log_sigmoid((lr_a @ w_gla_gate_up + b_gla_gate).astype(jnp.float32)) / GATE_TAU
    o_a, s_new = gla_recurrence(q_a.reshape(B, S, H_A, DK_A) * (DK_A ** -0.5), k_a.reshape(B, S, H_A, DK_A),
                                v_a.reshape(B, S, H_A, DV_A), log_a.reshape(B, S, H_A, DK_A), gla_s0)
    o_a = rms_norm(o_a, gla_norm_gain).astype(x.dtype) * jax.nn.silu(r_a.reshape(B, S, H_A, DV_A))

    k_new = k_b.reshape(B, S, H_B, DH_B)
    v_new = v_b.reshape(B, S, H_B, DH_B)
    if k_past is None:
        k_all, v_all = k_new, v_new
    else:
        k_all = jnp.concatenate([k_past.astype(k_new.dtype), k_new], axis=1)
        v_all = jnp.concatenate([v_past.astype(v_new.dtype), v_new], axis=1)
    o_b = stick_breaking(q_b.reshape(B, S, H_B, DH_B), k_all, v_all,
                         past + jnp.arange(S, dtype=jnp.int32), jnp.arange(past + S, dtype=jnp.int32))

    y_a = o_a.reshape(B, S, MIX_WIDTH) @ w_branch[0]
    y_b = o_b.reshape(B, S, MIX_WIDTH) @ w_branch[1]
    gates = jax.nn.sigmoid(g_br.reshape(B, S, N_BRANCH, D_MODEL))
    x = x + ((gates[:, :, 0] * y_a + gates[:, :, 1] * y_b) @ w_out).astype(x.dtype)

    h2 = rms_norm(x, norm_ffn_gain).reshape(B * S, D_MODEL)
    x = x + hier_moe(h2, w_router_group, b_router_group, w_router_expert, b_router_expert,
                     w_exp_gate, w_exp_up, w_exp_down).reshape(B, S, D_MODEL).astype(x.dtype)
    return x, s_new.astype(gla_s0.dtype), k_new, v_new


def setup_inputs(seed: int = 0) -> dict:
    key = jax.random.key(seed)
    ks = jax.random.split(key, 24)
    f32 = jnp.float32

    def nrm(k, shape, scale):
        return jax.random.normal(k, shape, f32) * scale

    return {
        'x_prompt': nrm(ks[0], (BATCH, SEQ, D_MODEL), 1.0),
        'x_sample': nrm(ks[1], (DEC_BATCH, DEC_SEQ, D_MODEL), 1.0),
        'state_gla': nrm(ks[2], (DEPTH, DEC_BATCH, H_A, DK_A, DV_A), 0.5),
        'cache_sb_k': nrm(ks[3], (DEPTH, DEC_BATCH, PAST_LEN, H_B, DH_B), 1.0),
        'cache_sb_v': nrm(ks[4], (DEPTH, DEC_BATCH, PAST_LEN, H_B, DH_B), 1.0),
        'norm_mix_gain': 1.0 + nrm(ks[5], (DEPTH, D_MODEL), 0.02),
        'w_in': nrm(ks[6], (DEPTH, D_MODEL, IN_WIDTH), D_MODEL ** -0.5),
        'w_gla_gate_up': nrm(ks[7], (DEPTH, GATE_RANK, QK_A), GATE_RANK ** -0.5),
        'b_gla_gate': nrm(ks[8], (DEPTH, QK_A), 0.02),
        'gla_norm_gain': 1.0 + nrm(ks[9], (DEPTH, DV_A), 0.02),
        'w_branch': nrm(ks[10], (DEPTH, N_BRANCH, MIX_WIDTH, D_MODEL), MIX_WIDTH ** -0.5),
        'w_out': nrm(ks[11], (DEPTH, D_MODEL, D_MODEL), D_MODEL ** -0.5),
        'norm_ffn_gain': 1.0 + nrm(ks[12], (DEPTH, D_MODEL), 0.02),
        'w_router_group': nrm(ks[13], (DEPTH, D_MODEL, N_GROUPS), D_MODEL ** -0.5),
        'b_router_group': nrm(ks[14], (DEPTH, N_GROUPS), 0.01),
        'w_router_expert': nrm(ks[15], (DEPTH, D_MODEL, N_EXPERTS), D_MODEL ** -0.5),
        'b_router_expert': nrm(ks[16], (DEPTH, N_EXPERTS), 0.01),
        'w_exp_gate': nrm(ks[17], (DEPTH, N_EXPERTS, D_MODEL, D_EXPERT), D_MODEL ** -0.5),
        'w_exp_up': nrm(ks[18], (DEPTH, N_EXPERTS, D_MODEL, D_EXPERT), D_MODEL ** -0.5),
        'w_exp_down': nrm(ks[19], (DEPTH, N_EXPERTS, D_EXPERT, D_MODEL), D_EXPERT ** -0.5),
        'norm_final_gain': 1.0 + nrm(ks[20], (D_MODEL,), 0.02),
    }


def reference(x_prompt, x_sample, state_gla, cache_sb_k, cache_sb_v, norm_mix_gain, w_in, w_gla_gate_up,
              b_gla_gate, gla_norm_gain, w_branch, w_out, norm_ffn_gain, w_router_group, b_router_group,
              w_router_expert, b_router_expert, w_exp_gate, w_exp_up, w_exp_down, norm_final_gain):
    yp, ys = x_prompt, x_sample
    gla_p, k_p, v_p, gla_s, k_s, v_s = [], [], [], [], [], []
    for l in range(DEPTH):
        lw = (norm_mix_gain[l], w_in[l], w_gla_gate_up[l], b_gla_gate[l], gla_norm_gain[l], w_branch[l],
              w_out[l], norm_ffn_gain[l], w_router_group[l], b_router_group[l], w_router_expert[l],
              b_router_expert[l], w_exp_gate[l], w_exp_up[l], w_exp_down[l])
        s0 = jnp.zeros((yp.shape[0], H_A, DK_A, DV_A), yp.dtype)
        yp, sp, kp, vp = trunk_layer(yp, s0, None, None, *lw)
        ys, ss, kn, vn = trunk_layer(ys, state_gla[l], cache_sb_k[l], cache_sb_v[l], *lw)
        gla_p.append(sp); k_p.append(kp); v_p.append(vp)
        gla_s.append(ss); k_s.append(kn); v_s.append(vn)
    y_prompt = rms_norm(yp, norm_final_gain)
    y_sample = rms_norm(ys, norm_final_gain)
    return (y_prompt, y_sample, jnp.stack(gla_p), jnp.stack(k_p), jnp.stack(v_p),
            jnp.stack(gla_s), jnp.stack(k_s), jnp.stack(v_s))
```

```python
import functools

import jax
import jax.numpy as jnp
from jax import lax
from jax.experimental import pallas as pl
from jax.experimental.pallas import tpu as pltpu

F32 = jnp.float32
BF16 = jnp.bfloat16
I32 = jnp.int32

LANES = 128
RMS_EPS = 1e-6
GATE_TAU = 16.0
H_A = 4
DK_A = 64
DV_A = 128
GATE_RANK = 16
H_B = 8
DH_B = 64
N_GROUPS = 4
EXPERTS_PER_GROUP = 8
N_EXPERTS = N_GROUPS * EXPERTS_PER_GROUP
TOP_K = 2
GLA_CHUNK = 64
GLA_SUB = 16
GLA_EXP_CLAMP = 80.0
ROW_TILE = 256
SB_TQ = 128
SB_TK = 256
MOE_TILE = 256
VMEM_LIMIT = 56 * 1024 * 1024


def _cparams(sem):
    return pltpu.CompilerParams(dimension_semantics=sem, vmem_limit_bytes=VMEM_LIMIT)


def _dot(a, b):
    return jnp.dot(a, b, preferred_element_type=F32)


def _dot_nt(a, b):
    return lax.dot_general(a, b, (((1,), (1,)), ((), ())), preferred_element_type=F32)


def _dot_tn(a, b):
    return lax.dot_general(a, b, (((0,), (0,)), ((), ())), preferred_element_type=F32)


def _split_bf16(x):
    hi = x.astype(BF16)
    lo = (x - hi.astype(F32)).astype(BF16)
    return hi, lo


def _log_sigmoid(x):
    return jnp.minimum(x, 0.0) - jnp.log(1.0 + jnp.exp(-jnp.abs(x)))


def _sigmoid(x):
    return 1.0 / (1.0 + jnp.exp(-x))


def _rms_norm(x, gain):
    return x * lax.rsqrt(jnp.mean(x * x, axis=-1, keepdims=True) + RMS_EPS) * gain


def _inproj_kernel(x_ref, gain_ref, wa_ref, wb_ref, wg_ref, wgu_ref, bgu_ref,
                   qa_ref, ka_ref, va_ref, ra_ref, la_ref, qb_ref, kb_ref, vb_ref,
                   kb16_ref, vb16_ref, gbr_ref):
    h = _rms_norm(x_ref[...], gain_ref[...]).astype(BF16)
    pa = H_A * LANES
    mw = va_ref.shape[-1]
    qa_ref[...] = _dot(h, wa_ref[:, 0:pa])
    ka_ref[...] = _dot(h, wa_ref[:, pa:2 * pa])
    va_ref[...] = _dot(h, wa_ref[:, 2 * pa:2 * pa + mw])
    ra_ref[...] = _dot(h, wa_ref[:, 2 * pa + mw:2 * pa + 2 * mw])
    lr = _dot(h, wa_ref[:, 2 * pa + 2 * mw:2 * pa + 2 * mw + LANES])
    gl = _dot(lr.astype(BF16), wgu_ref[...]) + bgu_ref[...]
    la_ref[...] = _log_sigmoid(gl) / GATE_TAU
    qb_ref[...] = _dot(h, wb_ref[:, 0:mw]).astype(BF16)
    kb = _dot(h, wb_ref[:, mw:2 * mw])
    vb = _dot(h, wb_ref[:, 2 * mw:3 * mw])
    kb_ref[...] = kb
    vb_ref[...] = vb
    kb16_ref[...] = kb.astype(BF16)
    vb16_ref[...] = vb.astype(BF16)
    gbr_ref[...] = _dot(h, wg_ref[...])


def _in_projection(x, gain, wa, wb, wg, wgu, bgu):
    t, d = x.shape
    pa = H_A * LANES
    mw = wb.shape[1] // 3
    tm = ROW_TILE
    row = lambda w: pl.BlockSpec((tm, w), lambda i: (i, 0))
    full = lambda a: pl.BlockSpec(a.shape, lambda i: (0,) * a.ndim)
    out_shapes = [
        jax.ShapeDtypeStruct((t, pa), F32), jax.ShapeDtypeStruct((t, pa), F32),
        jax.ShapeDtypeStruct((t, mw), F32), jax.ShapeDtypeStruct((t, mw), F32),
        jax.ShapeDtypeStruct((t, pa), F32),
        jax.ShapeDtypeStruct((t, mw), BF16),
        jax.ShapeDtypeStruct((t, mw), F32), jax.ShapeDtypeStruct((t, mw), F32),
        jax.ShapeDtypeStruct((t, mw), BF16), jax.ShapeDtypeStruct((t, mw), BF16),
        jax.ShapeDtypeStruct((t, wg.shape[1]), F32),
    ]
    return pl.pallas_call(
        _inproj_kernel,
        grid=(t // tm,),
        in_specs=[row(d), full(gain), full(wa), full(wb), full(wg), full(wgu), full(bgu)],
        out_specs=[row(s.shape[1]) for s in out_shapes],
        out_shape=out_shapes,
        compiler_params=_cparams(("arbitrary",)),
        name="in_projection",
    )(x, gain, wa, wb, wg, wgu, bgu)


def _gla_chunk(q, k, v, la, st, tril_incl):
    c = q.shape[0]
    la_hi, la_lo = _split_bf16(la)
    b = _dot(tril_incl, la_hi) + _dot(tril_incl, la_lo)
    b_last = b[c - 1:c, :]
    rows = lax.broadcasted_iota(I32, (c, LANES), 0)
    nsub = c // GLA_SUB
    refs = [jnp.zeros((1, LANES), F32)] + [b[i * GLA_SUB - 1:i * GLA_SUB, :] for i in range(1, nsub)]
    ref_rows = refs[0]
    for i in range(1, nsub):
        ref_rows = jnp.where(rows >= i * GLA_SUB, refs[i], ref_rows)
    q_rel = q * jnp.exp(b - ref_rows)
    lhs = jnp.concatenate(
        [jnp.where((rows >= i * GLA_SUB) & (rows < (i + 1) * GLA_SUB), q_rel, 0.0) for i in range(nsub)],
        axis=1).astype(BF16)
    rhs = jnp.concatenate(
        [jnp.where(rows < (i + 1) * GLA_SUB, k * jnp.exp(jnp.minimum(refs[i] - b, GLA_EXP_CLAMP)), 0.0)
         for i in range(nsub)], axis=1).astype(BF16)
    att = _dot_nt(lhs, rhs)
    tt = lax.broadcasted_iota(I32, (c, c), 0)
    ss = lax.broadcasted_iota(I32, (c, c), 1)
    att = jnp.where(ss <= tt, att, 0.0)
    v16 = v.astype(BF16)
    inter = _dot_nt((q * jnp.exp(b)).astype(BF16), st.astype(BF16))
    intra = _dot(att.astype(BF16), v16)
    kd = (k * jnp.exp(b_last - b)).astype(BF16)
    st_new = st * jnp.exp(b_last) + _dot_tn(v16, kd)
    return inter + intra, st_new


def _gla_kernel(qa_ref, ka_ref, va_ref, ra_ref, la_ref, s0_ref, gain_ref, o_ref, sfin_ref, st_ref):
    j = pl.program_id(1)
    nj = pl.num_programs(1)
    rows_per_step = qa_ref.shape[1]
    c = GLA_CHUNK
    zpad = jnp.zeros((LANES - DK_A, DV_A), F32)

    @pl.when(j == 0)
    def _():
        for h in range(H_A):
            st_ref[h] = jnp.concatenate([s0_ref[0, h], zpad], axis=0).T

    ti = lax.broadcasted_iota(I32, (c, c), 0)
    si = lax.broadcasted_iota(I32, (c, c), 1)
    tril_incl = jnp.where(si <= ti, 1.0, 0.0).astype(BF16)
    gain = gain_ref[...]
    for ci in range(rows_per_step // c):
        r0 = ci * c
        for h in range(H_A):
            hp = slice(h * LANES, (h + 1) * LANES)
            hv = slice(h * DV_A, (h + 1) * DV_A)
            q = qa_ref[0, r0:r0 + c, hp] * (DK_A ** -0.5)
            o, st_new = _gla_chunk(q, ka_ref[0, r0:r0 + c, hp], va_ref[0, r0:r0 + c, hv],
                                   la_ref[0, r0:r0 + c, hp], st_ref[h], tril_incl)
            st_ref[h] = st_new
            r = ra_ref[0, r0:r0 + c, hv]
            o = _rms_norm(o, gain) * (r * _sigmoid(r))
            o_ref[0, r0:r0 + c, hv] = o.astype(o_ref.dtype)

    @pl.when(j == nj - 1)
    def _():
        for h in range(H_A):
            sfin_ref[0, h] = st_ref[h].T[0:DK_A, :]


def _gla(qa, ka, va, ra, la, s0, gain, rows_per_step):
    b, s, pa = qa.shape
    mw = va.shape[-1]
    seq = lambda w: pl.BlockSpec((1, rows_per_step, w), lambda i, j: (i, j, 0))
    state = pl.BlockSpec((1, H_A, DK_A, DV_A), lambda i, j: (i, 0, 0, 0))
    return pl.pallas_call(
        _gla_kernel,
        grid=(b, s // rows_per_step),
        in_specs=[seq(pa), seq(pa), seq(mw), seq(mw), seq(pa), state,
                  pl.BlockSpec(gain.shape, lambda i, j: (0, 0))],
        out_specs=[seq(mw), state],
        out_shape=[jax.ShapeDtypeStruct((b, s, mw), BF16),
                   jax.ShapeDtypeStruct((b, H_A, DK_A, DV_A), F32)],
        scratch_shapes=[pltpu.VMEM((H_A, LANES, LANES), F32)],
        compiler_params=_cparams(("arbitrary", "arbitrary")),
        name="gla",
    )(qa, ka, va, ra, la, s0, gain)


def _sb_tile(qm, kblk, vblk, carry, acc, tri, mask):
    z = _dot_nt(qm, kblk) * (DH_B ** -0.5)
    log_beta = _log_sigmoid(z)
    log_keep = log_beta - z
    if mask is not None:
        log_keep = jnp.where(mask, log_keep, 0.0)
    hi, lo = _split_bf16(log_keep)
    suffix = _dot(hi, tri) + _dot(lo, tri)
    w = jnp.exp(log_beta + (suffix - log_keep) + carry)
    if mask is not None:
        w = jnp.where(mask, w, 0.0)
    acc = acc + _dot(w.astype(BF16), vblk)
    carry = carry + suffix[:, 0:1]
    return carry, acc


def _sb_tri(tk):
    ji = lax.broadcasted_iota(I32, (tk, tk), 0)
    si = lax.broadcasted_iota(I32, (tk, tk), 1)
    return jnp.where(ji >= si, 1.0, 0.0).astype(BF16)


def _head_lane_masks():
    lane = lax.broadcasted_iota(I32, (1, LANES), 1)
    return lane < DH_B, lane >= DH_B


def _sb_prompt_kernel(q_ref, k_ref, v_ref, o_ref):
    qi = pl.program_id(2)
    tq, tk = SB_TQ, SB_TK
    tri = _sb_tri(tk)
    q = q_ref[0]
    m0, m1 = _head_lane_masks()
    zero = jnp.zeros_like(q)
    qms = (jnp.where(m0, q, zero), jnp.where(m1, q, zero))
    jd = (qi * tq) // tk
    qpos = qi * tq + lax.broadcasted_iota(I32, (tq, tk), 0)
    kpos = jd * tk + lax.broadcasted_iota(I32, (tq, tk), 1)
    mask = kpos < qpos

    def load(jb):
        start = pl.multiple_of(jb * tk, tk)
        return k_ref[0, pl.ds(start, tk), :], v_ref[0, pl.ds(start, tk), :]

    kd, vd = load(jd)
    state = []
    for qm in qms:
        state.extend(_sb_tile(qm, kd, vd, jnp.zeros((tq, 1), F32), jnp.zeros((tq, LANES), F32), tri, mask))

    def body(i, st):
        kb, vb = load(jd - 1 - i)
        out = []
        for hh, qm in enumerate(qms):
            out.extend(_sb_tile(qm, kb, vb, st[2 * hh], st[2 * hh + 1], tri, None))
        return tuple(out)

    st = lax.fori_loop(0, jd, body, tuple(state))
    o_ref[0] = jnp.where(m0, st[1], st[3]).astype(o_ref.dtype)


def _sb_prompt(q, k, v):
    b, s, w = q.shape
    npair = w // LANES
    qspec = pl.BlockSpec((1, SB_TQ, LANES), lambda i, p, j: (i, j, p))
    kvspec = pl.BlockSpec((1, s, LANES), lambda i, p, j: (i, 0, p))
    return pl.pallas_call(
        _sb_prompt_kernel,
        grid=(b, npair, s // SB_TQ),
        in_specs=[qspec, kvspec, kvspec],
        out_specs=qspec,
        out_shape=jax.ShapeDtypeStruct((b, s, w), BF16),
        compiler_params=_cparams(("arbitrary", "arbitrary", "arbitrary")),
        name="sb_prompt",
    )(q, k, v)


def _sb_sample_kernel(q_ref, kn_ref, vn_ref, kp_ref, vp_ref, o_ref):
    sq = q_ref.shape[1]
    past = kp_ref.shape[1]
    tk = SB_TK
    tri_new = _sb_tri(sq)
    tri = _sb_tri(tk)
    m0, m1 = _head_lane_masks()
    tpos = lax.broadcasted_iota(I32, (sq, sq), 0)
    spos = lax.broadcasted_iota(I32, (sq, sq), 1)
    mask_new = spos < tpos
    npair = q_ref.shape[2] // LANES
    for p in range(npair):
        lanes = slice(p * LANES, (p + 1) * LANES)
        q = q_ref[0, :, lanes]
        zero = jnp.zeros_like(q)
        qms = (jnp.where(m0, q, zero), jnp.where(m1, q, zero))
        kn = kn_ref[0, :, lanes]
        vn = vn_ref[0, :, lanes]
        state = []
        for qm in qms:
            state.extend(_sb_tile(qm, kn, vn, jnp.zeros((sq, 1), F32), jnp.zeros((sq, LANES), F32),
                                  tri_new, mask_new))

        def body(i, st, lanes=lanes, qms=qms):
            start = pl.multiple_of(past - (i + 1) * tk, tk)
            kb = kp_ref[0, pl.ds(start, tk), lanes].astype(BF16)
            vb = vp_ref[0, pl.ds(start, tk), lanes].astype(BF16)
            out = []
            for hh, qm in enumerate(qms):
                out.extend(_sb_tile(qm, kb, vb, st[2 * hh], st[2 * hh + 1], tri, None))
            return tuple(out)

        st = lax.fori_loop(0, past // tk, body, tuple(state))
        o_ref[0, :, lanes] = jnp.where(m0, st[1], st[3]).astype(o_ref.dtype)


def _sb_sample(q, k_new, v_new, k_past, v_past):
    b, sq, w = q.shape
    past = k_past.shape[1]
    new = pl.BlockSpec((1, sq, w), lambda i: (i, 0, 0))
    old = pl.BlockSpec((1, past, w), lambda i: (i, 0, 0))
    return pl.pallas_call(
        _sb_sample_kernel,
        grid=(b,),
        in_specs=[new, new, new, old, old],
        out_specs=new,
        out_shape=jax.ShapeDtypeStruct((b, sq, w), BF16),
        compiler_params=_cparams(("arbitrary",)),
        name="sb_sample",
    )(q, k_new, v_new, k_past, v_past)


def _first_argmax(vals, nrows):
    idx = lax.broadcasted_iota(I32, vals.shape, 0)
    top = jnp.max(vals, axis=0, keepdims=True)
    first = jnp.min(jnp.where(vals == top, idx, nrows), axis=0, keepdims=True)
    return top, first, idx


def _merge_kernel(oa_ref, ob_ref, g_ref, x_ref, wb0_ref, wb1_ref, wo_ref, gain_ref, wr_ref, br_ref,
                  x1_ref, h2_ref, eid_ref, wcol_ref):
    d = x_ref.shape[1]
    ya = _dot(oa_ref[...], wb0_ref[...])
    yb = _dot(ob_ref[...], wb1_ref[...])
    m = _sigmoid(g_ref[:, 0:d]) * ya + _sigmoid(g_ref[:, d:2 * d]) * yb
    x1 = x_ref[...] + _dot(m.astype(BF16), wo_ref[...])
    x1_ref[...] = x1
    h2 = _rms_norm(x1, gain_ref[...])
    h2_ref[...] = h2

    h_hi, h_lo = _split_bf16(h2)
    w_hi, w_lo = _split_bf16(wr_ref[...])
    lt = _dot_nt(w_hi, h_hi) + _dot_nt(w_hi, h_lo) + _dot_nt(w_lo, h_hi) + br_ref[:, 0:1]
    gl = lt[0:N_GROUPS, :]
    g_top, g_idx, _ = _first_argmax(gl, N_GROUPS)
    g_e = jnp.exp(gl - g_top)
    g_p = jnp.max(g_e / jnp.sum(g_e, axis=0, keepdims=True), axis=0, keepdims=True)
    el = jnp.zeros((EXPERTS_PER_GROUP, lt.shape[1]), F32)
    for g in range(N_GROUPS):
        r0 = 8 + g * EXPERTS_PER_GROUP
        el = jnp.where(g_idx == g, lt[r0:r0 + EXPERTS_PER_GROUP, :], el)
    e_top, i1, eidx = _first_argmax(el, EXPERTS_PER_GROUP)
    e_e = jnp.exp(el - e_top)
    e_p = e_e / jnp.sum(e_e, axis=0, keepdims=True)
    p1 = jnp.max(e_p, axis=0, keepdims=True)
    rest = jnp.where(eidx == i1, -1.0, e_p)
    p2, i2, _ = _first_argmax(rest, EXPERTS_PER_GROUP)
    norm = p1 + p2
    w1 = g_p * (p1 / norm)
    w2 = g_p * (p2 / norm)
    eid_ref[...] = jnp.concatenate([g_idx * EXPERTS_PER_GROUP + i1, g_idx * EXPERTS_PER_GROUP + i2], axis=0)
    rows = lax.broadcasted_iota(I32, (LANES, lt.shape[1]), 0)
    wrows = jnp.where(rows == 0, w1, jnp.where(rows == 1, w2, 0.0))
    wcol_ref[...] = wrows.T


def _merge(oa, ob, gbr, x, wb0, wb1, wo, gain, wr, br):
    t, d = x.shape
    tm = ROW_TILE
    row = lambda w: pl.BlockSpec((tm, w), lambda i: (i, 0))
    full = lambda a: pl.BlockSpec(a.shape, lambda i: (0,) * a.ndim)
    return pl.pallas_call(
        _merge_kernel,
        grid=(t // tm,),
        in_specs=[row(oa.shape[1]), row(ob.shape[1]), row(gbr.shape[1]), row(d),
                  full(wb0), full(wb1), full(wo), full(gain), full(wr), full(br)],
        out_specs=[row(d), row(d), pl.BlockSpec((TOP_K, tm), lambda i: (0, i)), row(LANES)],
        out_shape=[jax.ShapeDtypeStruct((t, d), F32), jax.ShapeDtypeStruct((t, d), F32),
                   jax.ShapeDtypeStruct((TOP_K, t), I32), jax.ShapeDtypeStruct((t, LANES), F32)],
        compiler_params=_cparams(("arbitrary",)),
        name="merge_router",
    )(oa, ob, gbr, x, wb0, wb1, wo, gain, wr, br)


def _positions_kernel(eid_ref, dest_ref, counts_ref, rank_ref):
    nblk, width = eid_ref.shape
    ji = lax.broadcasted_iota(I32, (width, width), 0)
    si = lax.broadcasted_iota(I32, (width, width), 1)
    prefix = jnp.where(ji <= si, 1.0, 0.0).astype(BF16)
    expert = lax.broadcasted_iota(I32, (N_EXPERTS, width), 0)

    def onehot(i):
        return expert == eid_ref[pl.ds(i, 1), :]

    def rank_body(i, run):
        oh = onehot(i)
        cum = _dot(jnp.where(oh, 1.0, 0.0).astype(BF16), prefix) + run
        rank_ref[pl.ds(i, 1), :] = jnp.sum(jnp.where(oh, cum, 0.0), axis=0, keepdims=True) - 1.0
        return cum[:, width - 1:width]

    counts = lax.fori_loop(0, nblk, rank_body, jnp.zeros((N_EXPERTS, 1), F32))
    counts_ref[...] = jnp.broadcast_to(counts, counts_ref.shape).astype(I32)
    c_hi = jnp.floor(counts * (1.0 / 256.0))
    c_lo = counts - 256.0 * c_hi
    ei = lax.broadcasted_iota(I32, (N_EXPERTS, N_EXPERTS), 0)
    ej = lax.broadcasted_iota(I32, (N_EXPERTS, N_EXPERTS), 1)
    strict = jnp.where(ej < ei, 1.0, 0.0).astype(BF16)
    digits = jnp.concatenate([jnp.broadcast_to(c_hi, (N_EXPERTS, LANES)),
                              jnp.broadcast_to(c_lo, (N_EXPERTS, LANES))], axis=1).astype(BF16)
    sums = _dot(strict, digits)
    start = 256.0 * sums[:, 0:1] + sums[:, LANES:LANES + 1]

    def dest_body(i, carry):
        off = jnp.sum(jnp.where(onehot(i), start, 0.0), axis=0, keepdims=True)
        dest_ref[pl.ds(i, 1), :] = (rank_ref[pl.ds(i, 1), :] + off).astype(I32)
        return carry

    lax.fori_loop(0, nblk, dest_body, 0)


def _positions(eid_blocks):
    nblk, width = eid_blocks.shape
    vm = lambda shape: pl.BlockSpec(shape, lambda: (0,) * len(shape))
    return pl.pallas_call(
        _positions_kernel,
        in_specs=[vm((nblk, width))],
        out_specs=[vm((nblk, width)), vm((N_EXPERTS, LANES))],
        out_shape=[jax.ShapeDtypeStruct((nblk, width), I32), jax.ShapeDtypeStruct((N_EXPERTS, LANES), I32)],
        scratch_shapes=[pltpu.VMEM((nblk, width), F32)],
        name="positions",
    )(eid_blocks)


def _dispatch_kernel(dest_ref, hp_ref, hs_ref, xs_ref, sem):
    i = pl.program_id(0)
    tm = dest_ref.shape[1]
    n_prompt_tiles = hp_ref.shape[0] // tm

    def scatter(src_ref, base):
        def copies(r):
            return [pltpu.make_async_copy(src_ref.at[pl.ds(base + r, 1)],
                                          xs_ref.at[pl.ds(dest_ref[k, r], 1)], sem) for k in range(TOP_K)]

        def start(r, c):
            for cp in copies(r):
                cp.start()
            return c

        def wait(r, c):
            for cp in copies(r):
                cp.wait()
            return c

        lax.fori_loop(0, tm, start, 0)
        lax.fori_loop(0, tm, wait, 0)

    @pl.when(i < n_prompt_tiles)
    def _():
        scatter(hp_ref, i * tm)

    @pl.when(i >= n_prompt_tiles)
    def _():
        scatter(hs_ref, (i - n_prompt_tiles) * tm)


def _dispatch(dest, h_prompt, h_sample):
    t = dest.shape[1]
    d = h_prompt.shape[1]
    tm = ROW_TILE
    return pl.pallas_call(
        _dispatch_kernel,
        grid=(t // tm,),
        in_specs=[pl.BlockSpec((TOP_K, tm), lambda i: (0, i), memory_space=pltpu.SMEM),
                  pl.BlockSpec(memory_space=pl.ANY), pl.BlockSpec(memory_space=pl.ANY)],
        out_specs=pl.BlockSpec(memory_space=pl.ANY),
        out_shape=jax.ShapeDtypeStruct((TOP_K * t, d), F32),
        scratch_shapes=[pltpu.SemaphoreType.DMA(())],
        compiler_params=_cparams(("arbitrary",)),
        name="dispatch",
    )(dest, h_prompt, h_sample)


def _experts_kernel(vblk_ref, vexp_ref, vlo_ref, vhi_ref, xs_ref, wg_ref, wu_ref, wd_ref, ys_ref):
    v = pl.program_id(0)
    lo = vlo_ref[v]
    hi = vhi_ref[v]
    first = jnp.logical_or(v == 0, vblk_ref[v] != vblk_ref[jnp.maximum(v - 1, 0)])

    @pl.when(hi > lo)
    def _():
        x = xs_ref[...].astype(BF16)
        gate = _dot(x, wg_ref[0].astype(BF16))
        up = _dot(x, wu_ref[0].astype(BF16))
        hid = (gate * _sigmoid(gate) * up).astype(BF16)
        y = _dot(hid, wd_ref[0].astype(BF16))
        rows = lax.broadcasted_iota(I32, y.shape, 0)
        mine = (rows >= lo) & (rows < hi)

        @pl.when(first)
        def _():
            ys_ref[...] = jnp.where(mine, y, 0.0)

        @pl.when(jnp.logical_not(first))
        def _():
            ys_ref[...] = jnp.where(mine, y, ys_ref[...])


def _experts(vblk, vexp, vlo, vhi, xs, wg, wu, wd):
    a, d = xs.shape
    de = wg.shape[2]
    tm = MOE_TILE
    grid_spec = pltpu.PrefetchScalarGridSpec(
        num_scalar_prefetch=4,
        grid=(vblk.shape[0],),
        in_specs=[pl.BlockSpec((tm, d), lambda v, b, e, lo, hi: (b[v], 0)),
                  pl.BlockSpec((1, d, de), lambda v, b, e, lo, hi: (e[v], 0, 0)),
                  pl.BlockSpec((1, d, de), lambda v, b, e, lo, hi: (e[v], 0, 0)),
                  pl.BlockSpec((1, de, d), lambda v, b, e, lo, hi: (e[v], 0, 0))],
        out_specs=pl.BlockSpec((tm, d), lambda v, b, e, lo, hi: (b[v], 0)),
    )
    return pl.pallas_call(
        _experts_kernel,
        grid_spec=grid_spec,
        out_shape=jax.ShapeDtypeStruct((a, d), F32),
        compiler_params=_cparams(("arbitrary",)),
        name="experts",
    )(vblk, vexp, vlo, vhi, xs, wg, wu, wd)


def _visit_plan(counts, n_rows):
    tm = MOE_TILE
    nblk = n_rows // tm
    n_visits = nblk + N_EXPERTS - 1
    ends = jnp.cumsum(counts)
    starts = ends - counts
    first_blk = starts // tm
    nvis = jnp.where(counts > 0, (ends + tm - 1) // tm - first_blk, 0)
    vis_end = jnp.cumsum(nvis)
    vis_start = vis_end - nvis
    v = jnp.arange(n_visits, dtype=I32)
    e = jnp.minimum(jnp.searchsorted(vis_end, v, side="right"), N_EXPERTS - 1).astype(I32)
    valid = v < vis_end[-1]
    blk = first_blk[e] + (v - vis_start[e])
    lo = jnp.clip(starts[e] - blk * tm, 0, tm)
    hi = jnp.clip(ends[e] - blk * tm, 0, tm)
    last_e = jnp.max(jnp.where(counts > 0, jnp.arange(N_EXPERTS, dtype=I32), 0))
    blk = jnp.where(valid, blk, nblk - 1).astype(I32)
    e = jnp.where(valid, e, last_e).astype(I32)
    lo = jnp.where(valid, lo, 0).astype(I32)
    hi = jnp.where(valid, hi, 0).astype(I32)
    return blk, e, lo, hi


def _combine_kernel(dest_ref, ys_ref, x1_ref, wcol_ref, gain_ref, out_ref, buf_ref, sem):
    tm = x1_ref.shape[0]

    def copies(r):
        return [pltpu.make_async_copy(ys_ref.at[pl.ds(dest_ref[k, r], 1)],
                                      buf_ref.at[k, pl.ds(r, 1)], sem) for k in range(TOP_K)]

    def start(r, c):
        for cp in copies(r):
            cp.start()
        return c

    def wait(r, c):
        for cp in copies(r):
            cp.wait()
        return c

    lax.fori_loop(0, tm, start, 0)
    lax.fori_loop(0, tm, wait, 0)
    y = wcol_ref[:, 0:1] * buf_ref[0] + wcol_ref[:, 1:2] * buf_ref[1]
    out_ref[...] = _rms_norm(x1_ref[...] + y, gain_ref[...])


def _combine(dest, ys, x1, wcol, gain):
    t, d = x1.shape
    tm = ROW_TILE
    row = lambda w: pl.BlockSpec((tm, w), lambda i: (i, 0))
    return pl.pallas_call(
        _combine_kernel,
        grid=(t // tm,),
        in_specs=[pl.BlockSpec((TOP_K, tm), lambda i: (0, i), memory_space=pltpu.SMEM),
                  pl.BlockSpec(memory_space=pl.ANY), row(d), row(LANES),
                  pl.BlockSpec(gain.shape, lambda i: (0, 0))],
        out_specs=row(d),
        out_shape=jax.ShapeDtypeStruct((t, d), F32),
        scratch_shapes=[pltpu.VMEM((TOP_K, tm, d), F32), pltpu.SemaphoreType.DMA(())],
        compiler_params=_cparams(("arbitrary",)),
        name="combine",
    )(dest, ys, x1, wcol, gain)


def _pad_heads(w):
    r = w.shape[0]
    return jnp.pad(w.reshape(r, H_A, DK_A), ((0, 0), (0, 0), (0, LANES - DK_A))).reshape(r, H_A * LANES)


def _prepare_weights(w_in, w_gla_gate_up, b_gla_gate, w_branch, w_out, w_router_group, b_router_group,
                     w_router_expert, b_router_expert):
    d = w_in.shape[0]
    qk = H_A * DK_A
    mw = H_A * DV_A
    c = 0
    w_qa, c = w_in[:, c:c + qk], c + qk
    w_ka, c = w_in[:, c:c + qk], c + qk
    w_va, c = w_in[:, c:c + mw], c + mw
    w_ra, c = w_in[:, c:c + mw], c + mw
    w_lr, c = w_in[:, c:c + GATE_RANK], c + GATE_RANK
    w_b, c = w_in[:, c:c + 3 * mw], c + 3 * mw
    w_g = w_in[:, c:]
    wa = jnp.concatenate([_pad_heads(w_qa), _pad_heads(w_ka), w_va, w_ra,
                          jnp.pad(w_lr, ((0, 0), (0, LANES - GATE_RANK)))], axis=1).astype(BF16)
    wgu = jnp.pad(_pad_heads(w_gla_gate_up), ((0, LANES - GATE_RANK), (0, 0))).astype(BF16)
    bgu = _pad_heads(b_gla_gate[None, :])
    wr = jnp.zeros((LANES, d), F32)
    wr = wr.at[0:N_GROUPS].set(w_router_group.T).at[8:8 + N_EXPERTS].set(w_router_expert.T)
    br = jnp.zeros((LANES,), F32).at[0:N_GROUPS].set(b_router_group).at[8:8 + N_EXPERTS].set(b_router_expert)
    br = jnp.broadcast_to(br[:, None], (LANES, LANES))
    return dict(wa=wa, wb=w_b.astype(BF16), wg=w_g.astype(BF16), wgu=wgu, bgu=bgu,
                wb0=w_branch[0].astype(BF16), wb1=w_branch[1].astype(BF16), wo=w_out.astype(BF16),
                wr=wr, br=br)


def _mixers(x, s0, k_past, v_past, w, norm_mix_gain, gla_norm_gain, norm_ffn_gain):
    b, s, d = x.shape
    xf = x.reshape(b * s, d)
    qa, ka, va, ra, la, qb, kb, vb, kb16, vb16, gbr = _in_projection(
        xf, norm_mix_gain[None, :], w["wa"], w["wb"], w["wg"], w["wgu"], w["bgu"])
    seq = lambda a: a.reshape(b, s, a.shape[-1])
    oa, s_new = _gla(seq(qa), seq(ka), seq(va), seq(ra), seq(la), s0, gla_norm_gain[None, :],
                     min(s, ROW_TILE))
    if k_past is None:
        ob = _sb_prompt(seq(qb), seq(kb16), seq(vb16))
    else:
        past = k_past.shape[1]
        ob = _sb_sample(seq(qb), seq(kb16), seq(vb16), k_past.reshape(b, past, -1), v_past.reshape(b, past, -1))
    x1, h2, eid, wcol = _merge(oa.reshape(b * s, -1), ob.reshape(b * s, -1), gbr, xf, w["wb0"], w["wb1"],
                               w["wo"], norm_ffn_gain[None, :], w["wr"], w["br"])
    return x1, h2, eid, wcol, s_new, kb.reshape(b, s, H_B, DH_B), vb.reshape(b, s, H_B, DH_B)


def kernel(x_prompt, x_sample, state_gla, cache_sb_k, cache_sb_v, norm_mix_gain, w_in, w_gla_gate_up, b_gla_gate, gla_norm_gain, w_branch, w_out, norm_ffn_gain, w_router_group, b_router_group, w_router_expert, b_router_expert, w_exp_gate, w_exp_up, w_exp_down, norm_final_gain):
    depth = w_in.shape[0]
    assert depth == 1, "one trunk layer per step"
    l = 0
    w = _prepare_weights(w_in[l], w_gla_gate_up[l], b_gla_gate[l], w_branch[l], w_out[l], w_router_group[l],
                         b_router_group[l], w_router_expert[l], b_router_expert[l])
    bp, sp, d = x_prompt.shape
    bs, ss, _ = x_sample.shape
    s0 = jnp.zeros((bp, H_A, DK_A, DV_A), x_prompt.dtype)
    x1p, h2p, eidp, wcolp, gla_p, k_p, v_p = _mixers(
        x_prompt, s0, None, None, w, norm_mix_gain[l], gla_norm_gain[l], norm_ffn_gain[l])
    x1s, h2s, eids, wcols, gla_s, k_s, v_s = _mixers(
        x_sample, state_gla[l], cache_sb_k[l], cache_sb_v[l], w, norm_mix_gain[l], gla_norm_gain[l],
        norm_ffn_gain[l])

    tp, ts = bp * sp, bs * ss
    eid = jnp.concatenate([eidp, eids], axis=1)
    dest_blocks, counts = _positions(eid.reshape(-1, MOE_TILE))
    dest = dest_blocks.reshape(TOP_K, tp + ts)
    xs = _dispatch(dest, h2p, h2s)
    vblk, vexp, vlo, vhi = _visit_plan(counts[:, 0], TOP_K * (tp + ts))
    ys = _experts(vblk, vexp, vlo, vhi, xs, w_exp_gate[l], w_exp_up[l], w_exp_down[l])
    gf = norm_final_gain[None, :]
    y_prompt = _combine(dest[:, :tp], ys, x1p, wcolp, gf).reshape(bp, sp, d)
    y_sample = _combine(dest[:, tp:], ys, x1s, wcols, gf).reshape(bs, ss, d)
    return (y_prompt, y_sample, gla_p[None], k_p[None], v_p[None], gla_s[None], k_s[None], v_s[None])
```

```python
import functools

import jax
import jax.numpy as jnp
from jax import lax
from jax.experimental import pallas as pl
from jax.experimental.pallas import tpu as pltpu

F32 = jnp.float32
BF16 = jnp.bfloat16
I32 = jnp.int32

LANES = 128
LOG2_E = 1.4426950408889634
RMS_EPS = 1e-6
GATE_TAU = 16.0
H_A = 4
DK_A = 64
DV_A = 128
GATE_RANK = 16
H_B = 8
DH_B = 64
N_GROUPS = 4
EXPERTS_PER_GROUP = 8
N_EXPERTS = N_GROUPS * EXPERTS_PER_GROUP
TOP_K = 2
GLA_CHUNK = 64
GLA_SUB = 16
GLA_EXP_CLAMP = 80.0
ROW_TILE = 256
SB_TILE = 256
MOE_TILE = 256
VMEM_LIMIT = 56 * 1024 * 1024


def _cparams(sem):
    return pltpu.CompilerParams(dimension_semantics=sem, vmem_limit_bytes=VMEM_LIMIT)


def _dot(a, b):
    return jnp.dot(a, b, preferred_element_type=F32)


def _dot_nt(a, b):
    return lax.dot_general(a, b, (((1,), (1,)), ((), ())), preferred_element_type=F32)


def _dot_tn(a, b):
    return lax.dot_general(a, b, (((0,), (0,)), ((), ())), preferred_element_type=F32)


def _split_bf16(x):
    hi = x.astype(BF16)
    lo = (x - hi.astype(F32)).astype(BF16)
    return hi, lo


def _log_sigmoid(x):
    return jnp.minimum(x, 0.0) - jnp.log(1.0 + jnp.exp(-jnp.abs(x)))


def _sigmoid(x):
    return 1.0 / (1.0 + jnp.exp(-x))


def _rms_norm(x, gain):
    return x * lax.rsqrt(jnp.mean(x * x, axis=-1, keepdims=True) + RMS_EPS) * gain


def _inproj_kernel(x_ref, gain_ref, wa_ref, wb_ref, wg_ref, wgu_ref, bgu_ref,
                   qa_ref, ka_ref, va_ref, ra_ref, la_ref, qb_ref, kb_ref, vb_ref,
                   kb16_ref, vb16_ref, gbr_ref):
    h = _rms_norm(x_ref[...], gain_ref[...]).astype(BF16)
    pa = H_A * LANES
    mw = va_ref.shape[-1]
    qa_ref[...] = _dot(h, wa_ref[:, 0:pa])
    ka_ref[...] = _dot(h, wa_ref[:, pa:2 * pa])
    va_ref[...] = _dot(h, wa_ref[:, 2 * pa:2 * pa + mw])
    ra_ref[...] = _dot(h, wa_ref[:, 2 * pa + mw:2 * pa + 2 * mw])
    lr = _dot(h, wa_ref[:, 2 * pa + 2 * mw:2 * pa + 2 * mw + LANES])
    gl = _dot(lr.astype(BF16), wgu_ref[...]) + bgu_ref[...]
    la_ref[...] = _log_sigmoid(gl) / GATE_TAU
    qb_ref[...] = _dot(h, wb_ref[:, 0:mw]).astype(BF16)
    kb = _dot(h, wb_ref[:, mw:2 * mw])
    vb = _dot(h, wb_ref[:, 2 * mw:3 * mw])
    kb_ref[...] = kb
    vb_ref[...] = vb
    kb16_ref[...] = kb.astype(BF16)
    vb16_ref[...] = vb.astype(BF16)
    gbr_ref[...] = _dot(h, wg_ref[...])


def _in_projection(x, gain, wa, wb, wg, wgu, bgu):
    t, d = x.shape
    pa = H_A * LANES
    mw = wb.shape[1] // 3
    tm = ROW_TILE
    row = lambda w: pl.BlockSpec((tm, w), lambda i: (i, 0))
    full = lambda a: pl.BlockSpec(a.shape, lambda i: (0,) * a.ndim)
    out_shapes = [
        jax.ShapeDtypeStruct((t, pa), F32), jax.ShapeDtypeStruct((t, pa), F32),
        jax.ShapeDtypeStruct((t, mw), F32), jax.ShapeDtypeStruct((t, mw), F32),
        jax.ShapeDtypeStruct((t, pa), F32),
        jax.ShapeDtypeStruct((t, mw), BF16),
        jax.ShapeDtypeStruct((t, mw), F32), jax.ShapeDtypeStruct((t, mw), F32),
        jax.ShapeDtypeStruct((t, mw), BF16), jax.ShapeDtypeStruct((t, mw), BF16),
        jax.ShapeDtypeStruct((t, wg.shape[1]), F32),
    ]
    return pl.pallas_call(
        _inproj_kernel,
        grid=(t // tm,),
        in_specs=[row(d), full(gain), full(wa), full(wb), full(wg), full(wgu), full(bgu)],
        out_specs=[row(s.shape[1]) for s in out_shapes],
        out_shape=out_shapes,
        compiler_params=_cparams(("arbitrary",)),
        name="in_projection",
    )(x, gain, wa, wb, wg, wgu, bgu)


def _gla_chunk(q, k, v, la, st, tril_incl):
    c = q.shape[0]
    la_hi, la_lo = _split_bf16(la)
    b = _dot(tril_incl, la_hi) + _dot(tril_incl, la_lo)
    b_last = b[c - 1:c, :]
    rows = lax.broadcasted_iota(I32, (c, LANES), 0)
    nsub = c // GLA_SUB
    refs = [jnp.zeros((1, LANES), F32)] + [b[i * GLA_SUB - 1:i * GLA_SUB, :] for i in range(1, nsub)]
    ref_rows = refs[0]
    for i in range(1, nsub):
        ref_rows = jnp.where(rows >= i * GLA_SUB, refs[i], ref_rows)
    q_rel = q * jnp.exp(b - ref_rows)
    lhs = jnp.concatenate(
        [jnp.where((rows >= i * GLA_SUB) & (rows < (i + 1) * GLA_SUB), q_rel, 0.0) for i in range(nsub)],
        axis=1).astype(BF16)
    rhs = jnp.concatenate(
        [jnp.where(rows < (i + 1) * GLA_SUB, k * jnp.exp(jnp.minimum(refs[i] - b, GLA_EXP_CLAMP)), 0.0)
         for i in range(nsub)], axis=1).astype(BF16)
    att = _dot_nt(lhs, rhs)
    tt = lax.broadcasted_iota(I32, (c, c), 0)
    ss = lax.broadcasted_iota(I32, (c, c), 1)
    att = jnp.where(ss <= tt, att, 0.0)
    v16 = v.astype(BF16)
    inter = _dot_nt((q * jnp.exp(b)).astype(BF16), st.astype(BF16))
    intra = _dot(att.astype(BF16), v16)
    kd = (k * jnp.exp(b_last - b)).astype(BF16)
    st_new = st * jnp.exp(b_last) + _dot_tn(v16, kd)
    return inter + intra, st_new


def _gla_kernel(qa_ref, ka_ref, va_ref, ra_ref, la_ref, s0_ref, gain_ref, o_ref, sfin_ref, st_ref):
    j = pl.program_id(1)
    nj = pl.num_programs(1)
    rows_per_step = qa_ref.shape[1]
    c = GLA_CHUNK
    zpad = jnp.zeros((LANES - DK_A, DV_A), F32)

    @pl.when(j == 0)
    def _():
        for h in range(H_A):
            st_ref[h] = jnp.concatenate([s0_ref[0, h], zpad], axis=0).T

    ti = lax.broadcasted_iota(I32, (c, c), 0)
    si = lax.broadcasted_iota(I32, (c, c), 1)
    tril_incl = jnp.where(si <= ti, 1.0, 0.0).astype(BF16)
    gain = gain_ref[...]
    for ci in range(rows_per_step // c):
        r0 = ci * c
        for h in range(H_A):
            hp = slice(h * LANES, (h + 1) * LANES)
            hv = slice(h * DV_A, (h + 1) * DV_A)
            q = qa_ref[0, r0:r0 + c, hp] * (DK_A ** -0.5)
            o, st_new = _gla_chunk(q, ka_ref[0, r0:r0 + c, hp], va_ref[0, r0:r0 + c, hv],
                                   la_ref[0, r0:r0 + c, hp], st_ref[h], tril_incl)
            st_ref[h] = st_new
            r = ra_ref[0, r0:r0 + c, hv]
            o = _rms_norm(o, gain) * (r * _sigmoid(r))
            o_ref[0, r0:r0 + c, hv] = o.astype(o_ref.dtype)

    @pl.when(j == nj - 1)
    def _():
        for h in range(H_A):
            sfin_ref[0, h] = st_ref[h].T[0:DK_A, :]


def _gla(qa, ka, va, ra, la, s0, gain, rows_per_step):
    b, s, pa = qa.shape
    mw = va.shape[-1]
    seq = lambda w: pl.BlockSpec((1, rows_per_step, w), lambda i, j: (i, j, 0))
    state = pl.BlockSpec((1, H_A, DK_A, DV_A), lambda i, j: (i, 0, 0, 0))
    return pl.pallas_call(
        _gla_kernel,
        grid=(b, s // rows_per_step),
        in_specs=[seq(pa), seq(pa), seq(mw), seq(mw), seq(pa), state,
                  pl.BlockSpec(gain.shape, lambda i, j: (0, 0))],
        out_specs=[seq(mw), state],
        out_shape=[jax.ShapeDtypeStruct((b, s, mw), BF16),
                   jax.ShapeDtypeStruct((b, H_A, DK_A, DV_A), F32)],
        scratch_shapes=[pltpu.VMEM((H_A, LANES, LANES), F32)],
        compiler_params=_cparams(("arbitrary", "arbitrary")),
        name="gla",
    )(qa, ka, va, ra, la, s0, gain)


def _head_lane_masks():
    lane = lax.broadcasted_iota(I32, (1, LANES), 1)
    return lane < DH_B, lane >= DH_B


def _sb_neg_tri2(tk):
    ji = lax.broadcasted_iota(I32, (2 * tk, tk), 0)
    si = lax.broadcasted_iota(I32, (2 * tk, tk), 1)
    ji = jnp.where(ji >= tk, ji - tk, ji)
    return jnp.where(ji >= si, -1.0, 0.0).astype(BF16)


def _sb_stack_queries(q, qs_ref):
    m0, m1 = _head_lane_masks()
    for p in range(qs_ref.shape[0]):
        qp = q[:, p * LANES:(p + 1) * LANES] * (DH_B ** -0.5)
        zero = jnp.zeros_like(qp)
        qs_ref[p] = jnp.concatenate([jnp.where(m0, qp, zero), jnp.where(m1, qp, zero)], axis=0)


def _lane_fit(x, width):
    if width >= LANES:
        return jnp.concatenate([x] * (width // LANES), axis=1)
    return x[:, 0:width]


def _sb_tile_step(qs_ref, acc_ref, carry_ref, k_tile, v_tile, ntri2, diagonal):
    npair, rows, _ = qs_ref.shape
    tq = rows // 2
    tk = ntri2.shape[1]
    m0, _ = _head_lane_masks()
    if diagonal:
        t = lax.broadcasted_iota(I32, (rows, tk), 0)
        t = jnp.where(t >= tq, t - tq, t)
        visible = lax.broadcasted_iota(I32, (rows, tk), 1) < t
    for p in range(npair):
        lanes = slice(p * LANES, (p + 1) * LANES)
        z = _dot_nt(qs_ref[p], k_tile(lanes)) * LOG2_E
        sp = jnp.maximum(z, 0.0) + jnp.log2(1.0 + jnp.exp2(-jnp.abs(z)))
        if diagonal:
            sp = jnp.where(visible, sp, 0.0)
        hi, lo = _split_bf16(sp)
        suffix = _dot(jnp.concatenate([hi, lo], axis=1), ntri2)
        carry = carry_ref[p]
        w = jnp.exp2(z + suffix + _lane_fit(carry, tk))
        if diagonal:
            w = jnp.where(visible, w, 0.0)
        pv = _dot(w.astype(BF16), v_tile(lanes))
        acc_ref[p] += jnp.where(m0, pv[0:tq], pv[tq:rows])
        carry_ref[p] = carry + jnp.broadcast_to(suffix[:, 0:1], carry.shape)


def _sb_prompt_kernel(q_ref, k_ref, v_ref, o_ref, qs_ref, acc_ref, carry_ref):
    qi = pl.program_id(1)
    tk = SB_TILE
    _sb_stack_queries(q_ref[0], qs_ref)
    acc_ref[...] = jnp.zeros_like(acc_ref)
    carry_ref[...] = jnp.zeros_like(carry_ref)
    ntri2 = _sb_neg_tri2(tk)

    def step(jb, diagonal):
        start = pl.multiple_of(jb * tk, tk)
        _sb_tile_step(qs_ref, acc_ref, carry_ref,
                      lambda lanes: k_ref[0, pl.ds(start, tk), lanes],
                      lambda lanes: v_ref[0, pl.ds(start, tk), lanes], ntri2, diagonal)

    step(qi, True)

    def body(i, c):
        step(qi - 1 - i, False)
        return c

    lax.fori_loop(0, qi, body, 0)
    for p in range(acc_ref.shape[0]):
        o_ref[0, :, p * LANES:(p + 1) * LANES] = acc_ref[p].astype(o_ref.dtype)


def _sb_scratch(tq, npair):
    return [pltpu.VMEM((npair, 2 * tq, LANES), BF16), pltpu.VMEM((npair, tq, LANES), F32),
            pltpu.VMEM((npair, 2 * tq, LANES), F32)]


def _sb_prompt(q, k, v):
    b, s, w = q.shape
    tq = SB_TILE
    qspec = pl.BlockSpec((1, tq, w), lambda i, j: (i, j, 0))
    kvspec = pl.BlockSpec((1, s, w), lambda i, j: (i, 0, 0))
    return pl.pallas_call(
        _sb_prompt_kernel,
        grid=(b, s // tq),
        in_specs=[qspec, kvspec, kvspec],
        out_specs=qspec,
        out_shape=jax.ShapeDtypeStruct((b, s, w), BF16),
        scratch_shapes=_sb_scratch(tq, w // LANES),
        compiler_params=_cparams(("arbitrary", "arbitrary")),
        name="sb_prompt",
    )(q, k, v)


def _sb_sample_kernel(q_ref, kn_ref, vn_ref, kp_ref, vp_ref, o_ref, qs_ref, acc_ref, carry_ref):
    sq = q_ref.shape[1]
    past = kp_ref.shape[1]
    tk = SB_TILE
    _sb_stack_queries(q_ref[0], qs_ref)
    acc_ref[...] = jnp.zeros_like(acc_ref)
    carry_ref[...] = jnp.zeros_like(carry_ref)
    _sb_tile_step(qs_ref, acc_ref, carry_ref, lambda lanes: kn_ref[0, :, lanes],
                  lambda lanes: vn_ref[0, :, lanes], _sb_neg_tri2(sq), True)
    ntri2 = _sb_neg_tri2(tk)

    def body(i, c):
        start = pl.multiple_of(past - (i + 1) * tk, tk)
        _sb_tile_step(qs_ref, acc_ref, carry_ref,
                      lambda lanes: kp_ref[0, pl.ds(start, tk), lanes].astype(BF16),
                      lambda lanes: vp_ref[0, pl.ds(start, tk), lanes].astype(BF16), ntri2, False)
        return c

    lax.fori_loop(0, past // tk, body, 0)
    for p in range(acc_ref.shape[0]):
        o_ref[0, :, p * LANES:(p + 1) * LANES] = acc_ref[p].astype(o_ref.dtype)


def _sb_sample(q, k_new, v_new, k_past, v_past):
    b, sq, w = q.shape
    past = k_past.shape[1]
    new = pl.BlockSpec((1, sq, w), lambda i: (i, 0, 0))
    old = pl.BlockSpec((1, past, w), lambda i: (i, 0, 0))
    return pl.pallas_call(
        _sb_sample_kernel,
        grid=(b,),
        in_specs=[new, new, new, old, old],
        out_specs=new,
        out_shape=jax.ShapeDtypeStruct((b, sq, w), BF16),
        scratch_shapes=_sb_scratch(sq, w // LANES),
        compiler_params=_cparams(("arbitrary",)),
        name="sb_sample",
    )(q, k_new, v_new, k_past, v_past)


def _first_argmax(vals, nrows):
    idx = lax.broadcasted_iota(I32, vals.shape, 0)
    top = jnp.max(vals, axis=0, keepdims=True)
    first = jnp.min(jnp.where(vals == top, idx, nrows), axis=0, keepdims=True)
    return top, first, idx


def _merge_kernel(oa_ref, ob_ref, g_ref, x_ref, wb0_ref, wb1_ref, wo_ref, gain_ref, wr_ref, br_ref,
                  x1_ref, h2_ref, eid_ref, wcol_ref):
    d = x_ref.shape[1]
    ya = _dot(oa_ref[...], wb0_ref[...])
    yb = _dot(ob_ref[...], wb1_ref[...])
    m = _sigmoid(g_ref[:, 0:d]) * ya + _sigmoid(g_ref[:, d:2 * d]) * yb
    x1 = x_ref[...] + _dot(m.astype(BF16), wo_ref[...])
    x1_ref[...] = x1
    h2 = _rms_norm(x1, gain_ref[...])
    h2_ref[...] = h2

    h_hi, h_lo = _split_bf16(h2)
    w_hi, w_lo = _split_bf16(wr_ref[...])
    lt = _dot_nt(w_hi, h_hi) + _dot_nt(w_hi, h_lo) + _dot_nt(w_lo, h_hi) + br_ref[:, 0:1]
    gl = lt[0:N_GROUPS, :]
    g_top, g_idx, _ = _first_argmax(gl, N_GROUPS)
    g_e = jnp.exp(gl - g_top)
    g_p = jnp.max(g_e / jnp.sum(g_e, axis=0, keepdims=True), axis=0, keepdims=True)
    el = jnp.zeros((EXPERTS_PER_GROUP, lt.shape[1]), F32)
    for g in range(N_GROUPS):
        r0 = 8 + g * EXPERTS_PER_GROUP
        el = jnp.where(g_idx == g, lt[r0:r0 + EXPERTS_PER_GROUP, :], el)
    e_top, i1, eidx = _first_argmax(el, EXPERTS_PER_GROUP)
    e_e = jnp.exp(el - e_top)
    e_p = e_e / jnp.sum(e_e, axis=0, keepdims=True)
    p1 = jnp.max(e_p, axis=0, keepdims=True)
    rest = jnp.where(eidx == i1, -1.0, e_p)
    p2, i2, _ = _first_argmax(rest, EXPERTS_PER_GROUP)
    norm = p1 + p2
    w1 = g_p * (p1 / norm)
    w2 = g_p * (p2 / norm)
    eid_ref[...] = jnp.concatenate([g_idx * EXPERTS_PER_GROUP + i1, g_idx * EXPERTS_PER_GROUP + i2], axis=0)
    rows = lax.broadcasted_iota(I32, (LANES, lt.shape[1]), 0)
    wrows = jnp.where(rows == 0, w1, jnp.where(rows == 1, w2, 0.0))
    wcol_ref[...] = wrows.T


def _merge(oa, ob, gbr, x, wb0, wb1, wo, gain, wr, br):
    t, d = x.shape
    tm = ROW_TILE
    row = lambda w: pl.BlockSpec((tm, w), lambda i: (i, 0))
    full = lambda a: pl.BlockSpec(a.shape, lambda i: (0,) * a.ndim)
    return pl.pallas_call(
        _merge_kernel,
        grid=(t // tm,),
        in_specs=[row(oa.shape[1]), row(ob.shape[1]), row(gbr.shape[1]), row(d),
                  full(wb0), full(wb1), full(wo), full(gain), full(wr), full(br)],
        out_specs=[row(d), row(d), pl.BlockSpec((TOP_K, tm), lambda i: (0, i)), row(LANES)],
        out_shape=[jax.ShapeDtypeStruct((t, d), F32), jax.ShapeDtypeStruct((t, d), F32),
                   jax.ShapeDtypeStruct((TOP_K, t), I32), jax.ShapeDtypeStruct((t, LANES), F32)],
        compiler_params=_cparams(("arbitrary",)),
        name="merge_router",
    )(oa, ob, gbr, x, wb0, wb1, wo, gain, wr, br)


def _positions_kernel(eid_ref, dest_ref, counts_ref, rank_ref):
    nblk, width = eid_ref.shape
    ji = lax.broadcasted_iota(I32, (width, width), 0)
    si = lax.broadcasted_iota(I32, (width, width), 1)
    prefix = jnp.where(ji <= si, 1.0, 0.0).astype(BF16)
    expert = lax.broadcasted_iota(I32, (N_EXPERTS, width), 0)

    def onehot(i):
        return expert == eid_ref[pl.ds(i, 1), :]

    def rank_body(i, run):
        oh = onehot(i)
        cum = _dot(jnp.where(oh, 1.0, 0.0).astype(BF16), prefix) + run
        rank_ref[pl.ds(i, 1), :] = jnp.sum(jnp.where(oh, cum, 0.0), axis=0, keepdims=True) - 1.0
        return cum[:, width - 1:width]

    counts = lax.fori_loop(0, nblk, rank_body, jnp.zeros((N_EXPERTS, 1), F32))
    counts_ref[...] = jnp.broadcast_to(counts, counts_ref.shape).astype(I32)
    c_hi = jnp.floor(counts * (1.0 / 256.0))
    c_lo = counts - 256.0 * c_hi
    ei = lax.broadcasted_iota(I32, (N_EXPERTS, N_EXPERTS), 0)
    ej = lax.broadcasted_iota(I32, (N_EXPERTS, N_EXPERTS), 1)
    strict = jnp.where(ej < ei, 1.0, 0.0).astype(BF16)
    digits = jnp.concatenate([jnp.broadcast_to(c_hi, (N_EXPERTS, LANES)),
                              jnp.broadcast_to(c_lo, (N_EXPERTS, LANES))], axis=1).astype(BF16)
    sums = _dot(strict, digits)
    start = 256.0 * sums[:, 0:1] + sums[:, LANES:LANES + 1]

    def dest_body(i, carry):
        off = jnp.sum(jnp.where(onehot(i), start, 0.0), axis=0, keepdims=True)
        dest_ref[pl.ds(i, 1), :] = (rank_ref[pl.ds(i, 1), :] + off).astype(I32)
        return carry

    lax.fori_loop(0, nblk, dest_body, 0)


def _positions(eid_blocks):
    nblk, width = eid_blocks.shape
    vm = lambda shape: pl.BlockSpec(shape, lambda: (0,) * len(shape))
    return pl.pallas_call(
        _positions_kernel,
        in_specs=[vm((nblk, width))],
        out_specs=[vm((nblk, width)), vm((N_EXPERTS, LANES))],
        out_shape=[jax.ShapeDtypeStruct((nblk, width), I32), jax.ShapeDtypeStruct((N_EXPERTS, LANES), I32)],
        scratch_shapes=[pltpu.VMEM((nblk, width), F32)],
        name="positions",
    )(eid_blocks)


def _dispatch_kernel(n_prompt_tiles, dest_ref, hp_ref, hs_ref, xs_ref, sem):
    i = pl.program_id(0)
    tm = dest_ref.shape[1]

    def scatter(src_ref):
        def copies(r):
            return [pltpu.make_async_copy(src_ref.at[pl.ds(r, 1)],
                                          xs_ref.at[pl.ds(dest_ref[k, r], 1)], sem) for k in range(TOP_K)]

        def start(r, c):
            for cp in copies(r):
                cp.start()
            return c

        def wait(r, c):
            for cp in copies(r):
                cp.wait()
            return c

        lax.fori_loop(0, tm, start, 0)
        lax.fori_loop(0, tm, wait, 0)

    @pl.when(i < n_prompt_tiles)
    def _():
        scatter(hp_ref)

    @pl.when(i >= n_prompt_tiles)
    def _():
        scatter(hs_ref)


def _dispatch(dest, h_prompt, h_sample):
    t = dest.shape[1]
    d = h_prompt.shape[1]
    tm = ROW_TILE
    npt = h_prompt.shape[0] // tm
    return pl.pallas_call(
        functools.partial(_dispatch_kernel, npt),
        grid=(t // tm,),
        in_specs=[pl.BlockSpec((TOP_K, tm), lambda i: (0, i), memory_space=pltpu.SMEM),
                  pl.BlockSpec((tm, d), lambda i: (jnp.minimum(i, npt - 1), 0)),
                  pl.BlockSpec((tm, d), lambda i: (jnp.maximum(i - npt, 0), 0))],
        out_specs=pl.BlockSpec(memory_space=pl.ANY),
        out_shape=jax.ShapeDtypeStruct((TOP_K * t, d), F32),
        scratch_shapes=[pltpu.SemaphoreType.DMA(())],
        compiler_params=_cparams(("arbitrary",)),
        name="dispatch",
    )(dest, h_prompt, h_sample)


def _experts_kernel(vblk_ref, vexp_ref, vlo_ref, vhi_ref, xs_ref, wg_ref, wu_ref, wd_ref, ys_ref):
    v = pl.program_id(0)
    lo = vlo_ref[v]
    hi = vhi_ref[v]
    first = jnp.logical_or(v == 0, vblk_ref[v] != vblk_ref[jnp.maximum(v - 1, 0)])

    @pl.when(hi > lo)
    def _():
        x = xs_ref[...].astype(BF16)
        gate = _dot(x, wg_ref[0].astype(BF16))
        up = _dot(x, wu_ref[0].astype(BF16))
        hid = (gate * _sigmoid(gate) * up).astype(BF16)
        y = _dot(hid, wd_ref[0].astype(BF16))
        rows = lax.broadcasted_iota(I32, y.shape, 0)
        mine = (rows >= lo) & (rows < hi)

        @pl.when(first)
        def _():
            ys_ref[...] = jnp.where(mine, y, 0.0)

        @pl.when(jnp.logical_not(first))
        def _():
            ys_ref[...] = jnp.where(mine, y, ys_ref[...])


def _experts(vblk, vexp, vlo, vhi, xs, wg, wu, wd):
    a, d = xs.shape
    de = wg.shape[2]
    tm = MOE_TILE
    grid_spec = pltpu.PrefetchScalarGridSpec(
        num_scalar_prefetch=4,
        grid=(vblk.shape[0],),
        in_specs=[pl.BlockSpec((tm, d), lambda v, b, e, lo, hi: (b[v], 0)),
                  pl.BlockSpec((1, d, de), lambda v, b, e, lo, hi: (e[v], 0, 0)),
                  pl.BlockSpec((1, d, de), lambda v, b, e, lo, hi: (e[v], 0, 0)),
                  pl.BlockSpec((1, de, d), lambda v, b, e, lo, hi: (e[v], 0, 0))],
        out_specs=pl.BlockSpec((tm, d), lambda v, b, e, lo, hi: (b[v], 0)),
    )
    return pl.pallas_call(
        _experts_kernel,
        grid_spec=grid_spec,
        out_shape=jax.ShapeDtypeStruct((a, d), F32),
        compiler_params=_cparams(("arbitrary",)),
        name="experts",
    )(vblk, vexp, vlo, vhi, xs, wg, wu, wd)


def _visit_plan(counts, n_rows):
    tm = MOE_TILE
    nblk = n_rows // tm
    n_visits = nblk + N_EXPERTS - 1
    ends = jnp.cumsum(counts)
    starts = ends - counts
    first_blk = starts // tm
    nvis = jnp.where(counts > 0, (ends + tm - 1) // tm - first_blk, 0)
    vis_end = jnp.cumsum(nvis)
    vis_start = vis_end - nvis
    v = jnp.arange(n_visits, dtype=I32)
    e = jnp.minimum(jnp.sum((vis_end[None, :] <= v[:, None]).astype(I32), axis=1), N_EXPERTS - 1)
    valid = v < vis_end[-1]
    blk = first_blk[e] + (v - vis_start[e])
    lo = jnp.clip(starts[e] - blk * tm, 0, tm)
    hi = jnp.clip(ends[e] - blk * tm, 0, tm)
    last_e = jnp.max(jnp.where(counts > 0, jnp.arange(N_EXPERTS, dtype=I32), 0))
    blk = jnp.where(valid, blk, nblk - 1).astype(I32)
    e = jnp.where(valid, e, last_e).astype(I32)
    lo = jnp.where(valid, lo, 0).astype(I32)
    hi = jnp.where(valid, hi, 0).astype(I32)
    return blk, e, lo, hi


def _combine_kernel(dest_ref, ys_ref, x1_ref, wcol_ref, gain_ref, out_ref, buf_ref, sem):
    tm = x1_ref.shape[0]

    def copies(r):
        return [pltpu.make_async_copy(ys_ref.at[pl.ds(dest_ref[k, r], 1)],
                                      buf_ref.at[k, pl.ds(r, 1)], sem) for k in range(TOP_K)]

    def start(r, c):
        for cp in copies(r):
            cp.start()
        return c

    def wait(r, c):
        for cp in copies(r):
            cp.wait()
        return c

    lax.fori_loop(0, tm, start, 0)
    lax.fori_loop(0, tm, wait, 0)
    y = wcol_ref[:, 0:1] * buf_ref[0] + wcol_ref[:, 1:2] * buf_ref[1]
    out_ref[...] = _rms_norm(x1_ref[...] + y, gain_ref[...])


def _combine(dest, ys, x1, wcol, gain):
    t, d = x1.shape
    tm = ROW_TILE
    row = lambda w: pl.BlockSpec((tm, w), lambda i: (i, 0))
    return pl.pallas_call(
        _combine_kernel,
        grid=(t // tm,),
        in_specs=[pl.BlockSpec((TOP_K, tm), lambda i: (0, i), memory_space=pltpu.SMEM),
                  pl.BlockSpec(memory_space=pl.ANY), row(d), row(LANES),
                  pl.BlockSpec(gain.shape, lambda i: (0, 0))],
        out_specs=row(d),
        out_shape=jax.ShapeDtypeStruct((t, d), F32),
        scratch_shapes=[pltpu.VMEM((TOP_K, tm, d), F32), pltpu.SemaphoreType.DMA(())],
        compiler_params=_cparams(("arbitrary",)),
        name="combine",
    )(dest, ys, x1, wcol, gain)


def _pad_heads(w):
    r = w.shape[0]
    return jnp.pad(w.reshape(r, H_A, DK_A), ((0, 0), (0, 0), (0, LANES - DK_A))).reshape(r, H_A * LANES)


def _prepare_weights(w_in, w_gla_gate_up, b_gla_gate, w_branch, w_out, w_router_group, b_router_group,
                     w_router_expert, b_router_expert):
    d = w_in.shape[0]
    qk = H_A * DK_A
    mw = H_A * DV_A
    c = 0
    w_qa, c = w_in[:, c:c + qk], c + qk
    w_ka, c = w_in[:, c:c + qk], c + qk
    w_va, c = w_in[:, c:c + mw], c + mw
    w_ra, c = w_in[:, c:c + mw], c + mw
    w_lr, c = w_in[:, c:c + GATE_RANK], c + GATE_RANK
    w_b, c = w_in[:, c:c + 3 * mw], c + 3 * mw
    w_g = w_in[:, c:]
    wa = jnp.concatenate([_pad_heads(w_qa), _pad_heads(w_ka), w_va, w_ra,
                          jnp.pad(w_lr, ((0, 0), (0, LANES - GATE_RANK)))], axis=1).astype(BF16)
    wgu = jnp.pad(_pad_heads(w_gla_gate_up), ((0, LANES - GATE_RANK), (0, 0))).astype(BF16)
    bgu = _pad_heads(b_gla_gate[None, :])
    wr = jnp.zeros((LANES, d), F32)
    wr = wr.at[0:N_GROUPS].set(w_router_group.T).at[8:8 + N_EXPERTS].set(w_router_expert.T)
    br = jnp.zeros((LANES,), F32).at[0:N_GROUPS].set(b_router_group).at[8:8 + N_EXPERTS].set(b_router_expert)
    br = jnp.broadcast_to(br[:, None], (LANES, LANES))
    return dict(wa=wa, wb=w_b.astype(BF16), wg=w_g.astype(BF16), wgu=wgu, bgu=bgu,
                wb0=w_branch[0].astype(BF16), wb1=w_branch[1].astype(BF16), wo=w_out.astype(BF16),
                wr=wr, br=br)


def _mixers(x, s0, k_past, v_past, w, norm_mix_gain, gla_norm_gain, norm_ffn_gain):
    b, s, d = x.shape
    xf = x.reshape(b * s, d)
    qa, ka, va, ra, la, qb, kb, vb, kb16, vb16, gbr = _in_projection(
        xf, norm_mix_gain[None, :], w["wa"], w["wb"], w["wg"], w["wgu"], w["bgu"])
    seq = lambda a: a.reshape(b, s, a.shape[-1])
    oa, s_new = _gla(seq(qa), seq(ka), seq(va), seq(ra), seq(la), s0, gla_norm_gain[None, :],
                     min(s, ROW_TILE))
    if k_past is None:
        ob = _sb_prompt(seq(qb), seq(kb16), seq(vb16))
    else:
        past = k_past.shape[1]
        ob = _sb_sample(seq(qb), seq(kb16), seq(vb16), k_past.reshape(b, past, -1), v_past.reshape(b, past, -1))
    x1, h2, eid, wcol = _merge(oa.reshape(b * s, -1), ob.reshape(b * s, -1), gbr, xf, w["wb0"], w["wb1"],
                               w["wo"], norm_ffn_gain[None, :], w["wr"], w["br"])
    return x1, h2, eid, wcol, s_new, kb.reshape(b, s, H_B, DH_B), vb.reshape(b, s, H_B, DH_B)


def kernel(x_prompt, x_sample, state_gla, cache_sb_k, cache_sb_v, norm_mix_gain, w_in, w_gla_gate_up, b_gla_gate, gla_norm_gain, w_branch, w_out, norm_ffn_gain, w_router_group, b_router_group, w_router_expert, b_router_expert, w_exp_gate, w_exp_up, w_exp_down, norm_final_gain):
    depth = w_in.shape[0]
    assert depth == 1, "one trunk layer per step"
    l = 0
    w = _prepare_weights(w_in[l], w_gla_gate_up[l], b_gla_gate[l], w_branch[l], w_out[l], w_router_group[l],
                         b_router_group[l], w_router_expert[l], b_router_expert[l])
    bp, sp, d = x_prompt.shape
    bs, ss, _ = x_sample.shape
    s0 = jnp.zeros((bp, H_A, DK_A, DV_A), x_prompt.dtype)
    x1p, h2p, eidp, wcolp, gla_p, k_p, v_p = _mixers(
        x_prompt, s0, None, None, w, norm_mix_gain[l], gla_norm_gain[l], norm_ffn_gain[l])
    x1s, h2s, eids, wcols, gla_s, k_s, v_s = _mixers(
        x_sample, state_gla[l], cache_sb_k[l], cache_sb_v[l], w, norm_mix_gain[l], gla_norm_gain[l],
        norm_ffn_gain[l])

    tp, ts = bp * sp, bs * ss
    eid = jnp.concatenate([eidp, eids], axis=1)
    dest_blocks, counts = _positions(eid.reshape(-1, MOE_TILE))
    dest = dest_blocks.reshape(TOP_K, tp + ts)
    xs = _dispatch(dest, h2p, h2s)
    vblk, vexp, vlo, vhi = _visit_plan(counts[:, 0], TOP_K * (tp + ts))
    ys = _experts(vblk, vexp, vlo, vhi, xs, w_exp_gate[l], w_exp_up[l], w_exp_down[l])
    gf = norm_final_gain[None, :]
    y_prompt = _combine(dest[:, :tp], ys, x1p, wcolp, gf).reshape(bp, sp, d)
    y_sample = _combine(dest[:, tp:], ys, x1s, wcols, gf).reshape(bs, ss, d)
    return (y_prompt, y_sample, gla_p[None], k_p[None], v_p[None], gla_s[None], k_s[None], v_s[None])
```

```python
import functools

import jax
import jax.numpy as jnp
from jax import lax
from jax.experimental import pallas as pl
from jax.experimental.pallas import tpu as pltpu

F32 = jnp.float32
BF16 = jnp.bfloat16
I32 = jnp.int32

LANES = 128
LOG2_E = 1.4426950408889634
RMS_EPS = 1e-6
GATE_TAU = 16.0
H_A = 4
DK_A = 64
DV_A = 128
GATE_RANK = 16
H_B = 8
DH_B = 64
N_GROUPS = 4
EXPERTS_PER_GROUP = 8
N_EXPERTS = N_GROUPS * EXPERTS_PER_GROUP
TOP_K = 2
GLA_CHUNK = 64
GLA_SUB = 16
GLA_EXP_CLAMP = 80.0
ROW_TILE = 256
SB_TILE = 256
MOE_TILE = 256
DMA_UNROLL = 8
VMEM_LIMIT = 56 * 1024 * 1024


def _cparams(sem):
    return pltpu.CompilerParams(dimension_semantics=sem, vmem_limit_bytes=VMEM_LIMIT)


def _dot(a, b):
    return jnp.dot(a, b, preferred_element_type=F32)


def _dot_nt(a, b):
    return lax.dot_general(a, b, (((1,), (1,)), ((), ())), preferred_element_type=F32)


def _dot_tn(a, b):
    return lax.dot_general(a, b, (((0,), (0,)), ((), ())), preferred_element_type=F32)


def _split_bf16(x):
    hi = x.astype(BF16)
    lo = (x - hi.astype(F32)).astype(BF16)
    return hi, lo


def _log_sigmoid(x):
    return jnp.minimum(x, 0.0) - jnp.log(1.0 + jnp.exp(-jnp.abs(x)))


def _sigmoid(x):
    return 1.0 / (1.0 + jnp.exp(-x))


def _rms_norm(x, gain):
    return x * lax.rsqrt(jnp.mean(x * x, axis=-1, keepdims=True) + RMS_EPS) * gain


def _inproj_kernel(x_ref, gain_ref, wa_ref, wb_ref, wg_ref, wgu_ref, bgu_ref,
                   qa_ref, ka_ref, va_ref, ra_ref, la_ref, qb_ref, kb_ref, vb_ref,
                   kb16_ref, vb16_ref, gbr_ref):
    h = _rms_norm(x_ref[...], gain_ref[...]).astype(BF16)
    pa = H_A * LANES
    mw = va_ref.shape[-1]
    qa_ref[...] = _dot(h, wa_ref[:, 0:pa])
    ka_ref[...] = _dot(h, wa_ref[:, pa:2 * pa])
    va_ref[...] = _dot(h, wa_ref[:, 2 * pa:2 * pa + mw])
    ra_ref[...] = _dot(h, wa_ref[:, 2 * pa + mw:2 * pa + 2 * mw])
    lr = _dot(h, wa_ref[:, 2 * pa + 2 * mw:2 * pa + 2 * mw + LANES])
    gl = _dot(lr.astype(BF16), wgu_ref[...]) + bgu_ref[...]
    la_ref[...] = _log_sigmoid(gl) / GATE_TAU
    qb_ref[...] = _dot(h, wb_ref[:, 0:mw]).astype(BF16)
    kb = _dot(h, wb_ref[:, mw:2 * mw])
    vb = _dot(h, wb_ref[:, 2 * mw:3 * mw])
    for hb in range(H_B):
        kb_ref[:, hb, :] = kb[:, hb * DH_B:(hb + 1) * DH_B]
        vb_ref[:, hb, :] = vb[:, hb * DH_B:(hb + 1) * DH_B]
    kb16_ref[...] = kb.astype(BF16)
    vb16_ref[...] = vb.astype(BF16)
    gbr_ref[...] = _dot(h, wg_ref[...])


def _in_projection(x, gain, wa, wb, wg, wgu, bgu):
    t, d = x.shape
    pa = H_A * LANES
    mw = wb.shape[1] // 3
    tm = ROW_TILE
    row = lambda w: pl.BlockSpec((tm, w), lambda i: (i, 0))
    full = lambda a: pl.BlockSpec(a.shape, lambda i: (0,) * a.ndim)
    out_shapes = [
        jax.ShapeDtypeStruct((t, pa), F32), jax.ShapeDtypeStruct((t, pa), F32),
        jax.ShapeDtypeStruct((t, mw), F32), jax.ShapeDtypeStruct((t, mw), F32),
        jax.ShapeDtypeStruct((t, pa), F32),
        jax.ShapeDtypeStruct((t, mw), BF16),
        jax.ShapeDtypeStruct((t, H_B, DH_B), F32), jax.ShapeDtypeStruct((t, H_B, DH_B), F32),
        jax.ShapeDtypeStruct((t, mw), BF16), jax.ShapeDtypeStruct((t, mw), BF16),
        jax.ShapeDtypeStruct((t, wg.shape[1]), F32),
    ]
    spec = lambda s: pl.BlockSpec((tm,) + s.shape[1:], lambda i: (i,) + (0,) * (len(s.shape) - 1))
    return pl.pallas_call(
        _inproj_kernel,
        grid=(t // tm,),
        in_specs=[row(d), full(gain), full(wa), full(wb), full(wg), full(wgu), full(bgu)],
        out_specs=[spec(s) for s in out_shapes],
        out_shape=out_shapes,
        compiler_params=_cparams(("arbitrary",)),
        name="in_projection",
    )(x, gain, wa, wb, wg, wgu, bgu)


def _gla_chunk(q, k, v, la, st, tril_incl):
    c = q.shape[0]
    la_hi, la_lo = _split_bf16(la)
    b = _dot(tril_incl, la_hi) + _dot(tril_incl, la_lo)
    b_last = b[c - 1:c, :]
    rows = lax.broadcasted_iota(I32, (c, LANES), 0)
    nsub = c // GLA_SUB
    refs = [jnp.zeros((1, LANES), F32)] + [b[i * GLA_SUB - 1:i * GLA_SUB, :] for i in range(1, nsub)]
    ref_rows = refs[0]
    for i in range(1, nsub):
        ref_rows = jnp.where(rows >= i * GLA_SUB, refs[i], ref_rows)
    q_rel = q * jnp.exp(b - ref_rows)
    lhs = jnp.concatenate(
        [jnp.where((rows >= i * GLA_SUB) & (rows < (i + 1) * GLA_SUB), q_rel, 0.0) for i in range(nsub)],
        axis=1).astype(BF16)
    rhs = jnp.concatenate(
        [jnp.where(rows < (i + 1) * GLA_SUB, k * jnp.exp(jnp.minimum(refs[i] - b, GLA_EXP_CLAMP)), 0.0)
         for i in range(nsub)], axis=1).astype(BF16)
    att = _dot_nt(lhs, rhs)
    tt = lax.broadcasted_iota(I32, (c, c), 0)
    ss = lax.broadcasted_iota(I32, (c, c), 1)
    att = jnp.where(ss <= tt, att, 0.0)
    v16 = v.astype(BF16)
    inter = _dot_nt((q * jnp.exp(b)).astype(BF16), st.astype(BF16))
    intra = _dot(att.astype(BF16), v16)
    kd = (k * jnp.exp(b_last - b)).astype(BF16)
    st_new = st * jnp.exp(b_last) + _dot_tn(v16, kd)
    return inter + intra, st_new


def _gla_kernel(qa_ref, ka_ref, va_ref, ra_ref, la_ref, s0_ref, gain_ref, o_ref, sfin_ref, st_ref):
    j = pl.program_id(1)
    nj = pl.num_programs(1)
    rows_per_step = qa_ref.shape[1]
    c = GLA_CHUNK
    zpad = jnp.zeros((LANES - DK_A, DV_A), F32)

    @pl.when(j == 0)
    def _():
        for h in range(H_A):
            st_ref[h] = jnp.concatenate([s0_ref[0, h], zpad], axis=0).T

    ti = lax.broadcasted_iota(I32, (c, c), 0)
    si = lax.broadcasted_iota(I32, (c, c), 1)
    tril_incl = jnp.where(si <= ti, 1.0, 0.0).astype(BF16)
    gain = gain_ref[...]
    for ci in range(rows_per_step // c):
        r0 = ci * c
        for h in range(H_A):
            hp = slice(h * LANES, (h + 1) * LANES)
            hv = slice(h * DV_A, (h + 1) * DV_A)
            q = qa_ref[0, r0:r0 + c, hp] * (DK_A ** -0.5)
            o, st_new = _gla_chunk(q, ka_ref[0, r0:r0 + c, hp], va_ref[0, r0:r0 + c, hv],
                                   la_ref[0, r0:r0 + c, hp], st_ref[h], tril_incl)
            st_ref[h] = st_new
            r = ra_ref[0, r0:r0 + c, hv]
            o = _rms_norm(o, gain) * (r * _sigmoid(r))
            o_ref[0, r0:r0 + c, hv] = o.astype(o_ref.dtype)

    @pl.when(j == nj - 1)
    def _():
        for h in range(H_A):
            sfin_ref[0, h] = st_ref[h].T[0:DK_A, :]


def _gla(qa, ka, va, ra, la, s0, gain, rows_per_step):
    b, s, pa = qa.shape
    mw = va.shape[-1]
    seq = lambda w: pl.BlockSpec((1, rows_per_step, w), lambda i, j: (i, j, 0))
    state = pl.BlockSpec((1, H_A, DK_A, DV_A), lambda i, j: (i, 0, 0, 0))
    return pl.pallas_call(
        _gla_kernel,
        grid=(b, s // rows_per_step),
        in_specs=[seq(pa), seq(pa), seq(mw), seq(mw), seq(pa), state,
                  pl.BlockSpec(gain.shape, lambda i, j: (0, 0))],
        out_specs=[seq(mw), state],
        out_shape=[jax.ShapeDtypeStruct((b, s, mw), BF16),
                   jax.ShapeDtypeStruct((b, H_A, DK_A, DV_A), F32)],
        scratch_shapes=[pltpu.VMEM((H_A, LANES, LANES), F32)],
        compiler_params=_cparams(("arbitrary", "arbitrary")),
        name="gla",
    )(qa, ka, va, ra, la, s0, gain)


def _head_lane_masks():
    lane = lax.broadcasted_iota(I32, (1, LANES), 1)
    return lane < DH_B, lane >= DH_B


def _sb_neg_tri(tk):
    ji = lax.broadcasted_iota(I32, (tk, tk), 0)
    si = lax.broadcasted_iota(I32, (tk, tk), 1)
    return jnp.where(ji >= si, -1.0, 0.0).astype(BF16)


def _sb_stack_queries(q, qs_ref):
    m0, m1 = _head_lane_masks()
    for p in range(qs_ref.shape[0]):
        qp = q[:, p * LANES:(p + 1) * LANES] * (DH_B ** -0.5)
        zero = jnp.zeros_like(qp)
        qs_ref[p] = jnp.concatenate([jnp.where(m0, qp, zero), jnp.where(m1, qp, zero)], axis=0)


def _lane_fit(x, width):
    if width >= LANES:
        return jnp.concatenate([x] * (width // LANES), axis=1)
    return x[:, 0:width]


def _sb_tile_step(qs_ref, acc_ref, carry_ref, k_tile, v_tile, ntri, diagonal):
    npair, rows, _ = qs_ref.shape
    tq = rows // 2
    tk = ntri.shape[1]
    m0, _ = _head_lane_masks()
    if diagonal:
        t = lax.broadcasted_iota(I32, (rows, tk), 0)
        t = jnp.where(t >= tq, t - tq, t)
        visible = lax.broadcasted_iota(I32, (rows, tk), 1) < t
    for p in range(npair):
        lanes = slice(p * LANES, (p + 1) * LANES)
        z = _dot_nt(qs_ref[p], k_tile(lanes)) * LOG2_E
        sp = jnp.maximum(z, 0.0) + jnp.log2(1.0 + jnp.exp2(-jnp.abs(z)))
        if diagonal:
            sp = jnp.where(visible, sp, 0.0)
        suffix = _dot(sp.astype(BF16), ntri)
        carry = carry_ref[p]
        w = jnp.exp2(z + suffix + _lane_fit(carry, tk))
        if diagonal:
            w = jnp.where(visible, w, 0.0)
        pv = _dot(w.astype(BF16), v_tile(lanes))
        acc_ref[p] += jnp.where(m0, pv[0:tq], pv[tq:rows])
        carry_ref[p] = carry + jnp.broadcast_to(suffix[:, 0:1], carry.shape)


def _sb_prompt_kernel(q_ref, k_ref, v_ref, o_ref, qs_ref, acc_ref, carry_ref):
    qi = pl.program_id(1)
    tk = SB_TILE
    _sb_stack_queries(q_ref[0], qs_ref)
    acc_ref[...] = jnp.zeros_like(acc_ref)
    carry_ref[...] = jnp.zeros_like(carry_ref)
    ntri = _sb_neg_tri(tk)

    def step(jb, diagonal):
        start = pl.multiple_of(jb * tk, tk)
        _sb_tile_step(qs_ref, acc_ref, carry_ref,
                      lambda lanes: k_ref[0, pl.ds(start, tk), lanes],
                      lambda lanes: v_ref[0, pl.ds(start, tk), lanes], ntri, diagonal)

    step(qi, True)

    def body(i, c):
        step(qi - 1 - i, False)
        return c

    lax.fori_loop(0, qi, body, 0)
    for p in range(acc_ref.shape[0]):
        o_ref[0, :, p * LANES:(p + 1) * LANES] = acc_ref[p].astype(o_ref.dtype)


def _sb_scratch(tq, npair):
    return [pltpu.VMEM((npair, 2 * tq, LANES), BF16), pltpu.VMEM((npair, tq, LANES), F32),
            pltpu.VMEM((npair, 2 * tq, LANES), F32)]


def _sb_prompt(q, k, v):
    b, s, w = q.shape
    tq = SB_TILE
    qspec = pl.BlockSpec((1, tq, w), lambda i, j: (i, j, 0))
    kvspec = pl.BlockSpec((1, s, w), lambda i, j: (i, 0, 0))
    return pl.pallas_call(
        _sb_prompt_kernel,
        grid=(b, s // tq),
        in_specs=[qspec, kvspec, kvspec],
        out_specs=qspec,
        out_shape=jax.ShapeDtypeStruct((b, s, w), BF16),
        scratch_shapes=_sb_scratch(tq, w // LANES),
        compiler_params=_cparams(("arbitrary", "arbitrary")),
        name="sb_prompt",
    )(q, k, v)


def _sb_sample_kernel(q_ref, kn_ref, vn_ref, kp_ref, vp_ref, o_ref, qs_ref, acc_ref, carry_ref):
    sq = q_ref.shape[1]
    past = kp_ref.shape[1]
    tk = SB_TILE
    _sb_stack_queries(q_ref[0], qs_ref)
    acc_ref[...] = jnp.zeros_like(acc_ref)
    carry_ref[...] = jnp.zeros_like(carry_ref)
    _sb_tile_step(qs_ref, acc_ref, carry_ref, lambda lanes: kn_ref[0, :, lanes],
                  lambda lanes: vn_ref[0, :, lanes], _sb_neg_tri(sq), True)
    ntri = _sb_neg_tri(tk)

    def body(i, c):
        start = pl.multiple_of(past - (i + 1) * tk, tk)
        _sb_tile_step(qs_ref, acc_ref, carry_ref,
                      lambda lanes: kp_ref[0, pl.ds(start, tk), lanes].astype(BF16),
                      lambda lanes: vp_ref[0, pl.ds(start, tk), lanes].astype(BF16), ntri, False)
        return c

    lax.fori_loop(0, past // tk, body, 0)
    for p in range(acc_ref.shape[0]):
        o_ref[0, :, p * LANES:(p + 1) * LANES] = acc_ref[p].astype(o_ref.dtype)


def _sb_sample(q, k_new, v_new, k_past, v_past):
    b, sq, w = q.shape
    past = k_past.shape[1]
    new = pl.BlockSpec((1, sq, w), lambda i: (i, 0, 0))
    old = pl.BlockSpec((1, past, w), lambda i: (i, 0, 0))
    return pl.pallas_call(
        _sb_sample_kernel,
        grid=(b,),
        in_specs=[new, new, new, old, old],
        out_specs=new,
        out_shape=jax.ShapeDtypeStruct((b, sq, w), BF16),
        scratch_shapes=_sb_scratch(sq, w // LANES),
        compiler_params=_cparams(("arbitrary",)),
        name="sb_sample",
    )(q, k_new, v_new, k_past, v_past)


def _first_argmax(vals, nrows):
    idx = lax.broadcasted_iota(I32, vals.shape, 0)
    top = jnp.max(vals, axis=0, keepdims=True)
    first = jnp.min(jnp.where(vals == top, idx, nrows), axis=0, keepdims=True)
    return top, first, idx


def _merge_kernel(oa_ref, ob_ref, g_ref, x_ref, wb0_ref, wb1_ref, wo_ref, gain_ref, wr_ref, br_ref,
                  x1_ref, h2_ref, eid_ref, wcol_ref):
    d = x_ref.shape[1]
    ya = _dot(oa_ref[...], wb0_ref[...])
    yb = _dot(ob_ref[...], wb1_ref[...])
    m = _sigmoid(g_ref[:, 0:d]) * ya + _sigmoid(g_ref[:, d:2 * d]) * yb
    x1 = x_ref[...] + _dot(m.astype(BF16), wo_ref[...])
    x1_ref[...] = x1
    h2 = _rms_norm(x1, gain_ref[...])
    h2_ref[...] = h2

    h_hi, h_lo = _split_bf16(h2)
    w_hi, w_lo = _split_bf16(wr_ref[...])
    lt = _dot_nt(w_hi, h_hi) + _dot_nt(w_hi, h_lo) + _dot_nt(w_lo, h_hi) + br_ref[:, 0:1]
    gl = lt[0:N_GROUPS, :]
    g_top, g_idx, _ = _first_argmax(gl, N_GROUPS)
    g_e = jnp.exp(gl - g_top)
    g_p = jnp.max(g_e / jnp.sum(g_e, axis=0, keepdims=True), axis=0, keepdims=True)
    el = jnp.zeros((EXPERTS_PER_GROUP, lt.shape[1]), F32)
    for g in range(N_GROUPS):
        r0 = 8 + g * EXPERTS_PER_GROUP
        el = jnp.where(g_idx == g, lt[r0:r0 + EXPERTS_PER_GROUP, :], el)
    e_top, i1, eidx = _first_argmax(el, EXPERTS_PER_GROUP)
    e_e = jnp.exp(el - e_top)
    e_p = e_e / jnp.sum(e_e, axis=0, keepdims=True)
    p1 = jnp.max(e_p, axis=0, keepdims=True)
    rest = jnp.where(eidx == i1, -1.0, e_p)
    p2, i2, _ = _first_argmax(rest, EXPERTS_PER_GROUP)
    norm = p1 + p2
    w1 = g_p * (p1 / norm)
    w2 = g_p * (p2 / norm)
    eid_ref[...] = jnp.concatenate([g_idx * EXPERTS_PER_GROUP + i1, g_idx * EXPERTS_PER_GROUP + i2], axis=0)
    rows = lax.broadcasted_iota(I32, (LANES, lt.shape[1]), 0)
    wrows = jnp.where(rows == 0, w1, jnp.where(rows == 1, w2, 0.0))
    wcol_ref[...] = wrows.T


def _merge(oa, ob, gbr, x, wb0, wb1, wo, gain, wr, br):
    t, d = x.shape
    tm = ROW_TILE
    row = lambda w: pl.BlockSpec((tm, w), lambda i: (i, 0))
    full = lambda a: pl.BlockSpec(a.shape, lambda i: (0,) * a.ndim)
    return pl.pallas_call(
        _merge_kernel,
        grid=(t // tm,),
        in_specs=[row(oa.shape[1]), row(ob.shape[1]), row(gbr.shape[1]), row(d),
                  full(wb0), full(wb1), full(wo), full(gain), full(wr), full(br)],
        out_specs=[row(d), row(d), pl.BlockSpec((TOP_K, tm), lambda i: (0, i)), row(LANES)],
        out_shape=[jax.ShapeDtypeStruct((t, d), F32), jax.ShapeDtypeStruct((t, d), F32),
                   jax.ShapeDtypeStruct((TOP_K, t), I32), jax.ShapeDtypeStruct((t, LANES), F32)],
        compiler_params=_cparams(("arbitrary",)),
        name="merge_router",
    )(oa, ob, gbr, x, wb0, wb1, wo, gain, wr, br)


def _positions_kernel(eid_ref, dest_ref, counts_ref, rank_ref):
    nblk, width = eid_ref.shape
    ji = lax.broadcasted_iota(I32, (width, width), 0)
    si = lax.broadcasted_iota(I32, (width, width), 1)
    prefix = jnp.where(ji <= si, 1.0, 0.0).astype(BF16)
    expert = lax.broadcasted_iota(I32, (N_EXPERTS, width), 0)

    def onehot(i):
        return expert == eid_ref[pl.ds(i, 1), :]

    def rank_body(i, run):
        oh = onehot(i)
        cum = _dot(jnp.where(oh, 1.0, 0.0).astype(BF16), prefix) + run
        rank_ref[pl.ds(i, 1), :] = jnp.sum(jnp.where(oh, cum, 0.0), axis=0, keepdims=True) - 1.0
        return cum[:, width - 1:width]

    counts = lax.fori_loop(0, nblk, rank_body, jnp.zeros((N_EXPERTS, 1), F32))
    counts_ref[...] = jnp.broadcast_to(counts, counts_ref.shape).astype(I32)
    c_hi = jnp.floor(counts * (1.0 / 256.0))
    c_lo = counts - 256.0 * c_hi
    ei = lax.broadcasted_iota(I32, (N_EXPERTS, N_EXPERTS), 0)
    ej = lax.broadcasted_iota(I32, (N_EXPERTS, N_EXPERTS), 1)
    strict = jnp.where(ej < ei, 1.0, 0.0).astype(BF16)
    digits = jnp.concatenate([jnp.broadcast_to(c_hi, (N_EXPERTS, LANES)),
                              jnp.broadcast_to(c_lo, (N_EXPERTS, LANES))], axis=1).astype(BF16)
    sums = _dot(strict, digits)
    start = 256.0 * sums[:, 0:1] + sums[:, LANES:LANES + 1]

    def dest_body(i, carry):
        off = jnp.sum(jnp.where(onehot(i), start, 0.0), axis=0, keepdims=True)
        dest_ref[pl.ds(i, 1), :] = (rank_ref[pl.ds(i, 1), :] + off).astype(I32)
        return carry

    lax.fori_loop(0, nblk, dest_body, 0)


def _positions(eid_blocks):
    nblk, width = eid_blocks.shape
    vm = lambda shape: pl.BlockSpec(shape, lambda: (0,) * len(shape))
    return pl.pallas_call(
        _positions_kernel,
        in_specs=[vm((nblk, width))],
        out_specs=[vm((nblk, width)), vm((N_EXPERTS, LANES))],
        out_shape=[jax.ShapeDtypeStruct((nblk, width), I32), jax.ShapeDtypeStruct((N_EXPERTS, LANES), I32)],
        scratch_shapes=[pltpu.VMEM((nblk, width), F32)],
        name="positions",
    )(eid_blocks)


def _dispatch_kernel(n_prompt_tiles, dest_ref, hp_ref, hs_ref, xs_ref, sem):
    i = pl.program_id(0)
    tm = dest_ref.shape[1]

    def scatter(src_ref):
        def start(r, c):
            for k in range(TOP_K):
                pltpu.make_async_copy(src_ref.at[pl.ds(r, 1)], xs_ref.at[pl.ds(dest_ref[k, r], 1)], sem).start()
            return c

        lax.fori_loop(0, tm, start, 0, unroll=DMA_UNROLL)
        for k in range(TOP_K):
            pltpu.make_async_copy(src_ref, xs_ref.at[pl.ds(0, tm)], sem).wait()

    @pl.when(i < n_prompt_tiles)
    def _():
        scatter(hp_ref)

    @pl.when(i >= n_prompt_tiles)
    def _():
        scatter(hs_ref)


def _dispatch(dest, h_prompt, h_sample):
    t = dest.shape[1]
    d = h_prompt.shape[1]
    tm = ROW_TILE
    npt = h_prompt.shape[0] // tm
    return pl.pallas_call(
        functools.partial(_dispatch_kernel, npt),
        grid=(t // tm,),
        in_specs=[pl.BlockSpec((TOP_K, tm), lambda i: (0, i), memory_space=pltpu.SMEM),
                  pl.BlockSpec((tm, d), lambda i: (jnp.minimum(i, npt - 1), 0)),
                  pl.BlockSpec((tm, d), lambda i: (jnp.maximum(i - npt, 0), 0))],
        out_specs=pl.BlockSpec(memory_space=pl.ANY),
        out_shape=jax.ShapeDtypeStruct((TOP_K * t, d), F32),
        scratch_shapes=[pltpu.SemaphoreType.DMA(())],
        compiler_params=_cparams(("arbitrary",)),
        name="dispatch",
    )(dest, h_prompt, h_sample)


def _experts_kernel(vblk_ref, vexp_ref, vlo_ref, vhi_ref, xs_ref, wg_ref, wu_ref, wd_ref, ys_ref,
                    wg16_ref, wu16_ref, wd16_ref):
    v = pl.program_id(0)
    lo = vlo_ref[v]
    hi = vhi_ref[v]
    prev = jnp.maximum(v - 1, 0)
    first = jnp.logical_or(v == 0, vblk_ref[v] != vblk_ref[prev])
    new_expert = jnp.logical_or(v == 0, vexp_ref[v] != vexp_ref[prev])

    @pl.when(new_expert)
    def _():
        wg16_ref[...] = wg_ref[0].astype(BF16)
        wu16_ref[...] = wu_ref[0].astype(BF16)
        wd16_ref[...] = wd_ref[0].astype(BF16)

    @pl.when(hi > lo)
    def _():
        x = xs_ref[...].astype(BF16)
        gate = _dot(x, wg16_ref[...])
        up = _dot(x, wu16_ref[...])
        hid = (gate * _sigmoid(gate) * up).astype(BF16)
        y = _dot(hid, wd16_ref[...])
        rows = lax.broadcasted_iota(I32, y.shape, 0)
        mine = (rows >= lo) & (rows < hi)

        @pl.when(first)
        def _():
            ys_ref[...] = jnp.where(mine, y, 0.0)

        @pl.when(jnp.logical_not(first))
        def _():
            ys_ref[...] = jnp.where(mine, y, ys_ref[...])


def _experts(vblk, vexp, vlo, vhi, xs, wg, wu, wd):
    a, d = xs.shape
    de = wg.shape[2]
    tm = MOE_TILE
    grid_spec = pltpu.PrefetchScalarGridSpec(
        num_scalar_prefetch=4,
        grid=(vblk.shape[0],),
        in_specs=[pl.BlockSpec((tm, d), lambda v, b, e, lo, hi: (b[v], 0)),
                  pl.BlockSpec((1, d, de), lambda v, b, e, lo, hi: (e[v], 0, 0)),
                  pl.BlockSpec((1, d, de), lambda v, b, e, lo, hi: (e[v], 0, 0)),
                  pl.BlockSpec((1, de, d), lambda v, b, e, lo, hi: (e[v], 0, 0))],
        out_specs=pl.BlockSpec((tm, d), lambda v, b, e, lo, hi: (b[v], 0)),
        scratch_shapes=[pltpu.VMEM((d, de), BF16), pltpu.VMEM((d, de), BF16), pltpu.VMEM((de, d), BF16)],
    )
    return pl.pallas_call(
        _experts_kernel,
        grid_spec=grid_spec,
        out_shape=jax.ShapeDtypeStruct((a, d), F32),
        compiler_params=_cparams(("arbitrary",)),
        name="experts",
    )(vblk, vexp, vlo, vhi, xs, wg, wu, wd)


def _visit_plan(counts, n_rows):
    tm = MOE_TILE
    nblk = n_rows // tm
    n_visits = nblk + N_EXPERTS - 1
    ends = jnp.cumsum(counts)
    starts = ends - counts
    first_blk = starts // tm
    nvis = jnp.where(counts > 0, (ends + tm - 1) // tm - first_blk, 0)
    vis_end = jnp.cumsum(nvis)
    vis_start = vis_end - nvis
    v = jnp.arange(n_visits, dtype=I32)
    e = jnp.minimum(jnp.sum((vis_end[None, :] <= v[:, None]).astype(I32), axis=1), N_EXPERTS - 1)
    valid = v < vis_end[-1]
    blk = first_blk[e] + (v - vis_start[e])
    lo = jnp.clip(starts[e] - blk * tm, 0, tm)
    hi = jnp.clip(ends[e] - blk * tm, 0, tm)
    last_e = jnp.max(jnp.where(counts > 0, jnp.arange(N_EXPERTS, dtype=I32), 0))
    blk = jnp.where(valid, blk, nblk - 1).astype(I32)
    e = jnp.where(valid, e, last_e).astype(I32)
    lo = jnp.where(valid, lo, 0).astype(I32)
    hi = jnp.where(valid, hi, 0).astype(I32)
    return blk, e, lo, hi


def _combine_kernel(dest_ref, ys_ref, x1_ref, wcol_ref, gain_ref, out_ref, buf_ref, sem):
    tm = x1_ref.shape[0]

    def start(r, c):
        for k in range(TOP_K):
            pltpu.make_async_copy(ys_ref.at[pl.ds(dest_ref[k, r], 1)], buf_ref.at[k, pl.ds(r, 1)], sem).start()
        return c

    lax.fori_loop(0, tm, start, 0, unroll=DMA_UNROLL)
    for k in range(TOP_K):
        pltpu.make_async_copy(ys_ref.at[pl.ds(0, tm)], buf_ref.at[k], sem).wait()
    y = wcol_ref[:, 0:1] * buf_ref[0] + wcol_ref[:, 1:2] * buf_ref[1]
    out_ref[...] = _rms_norm(x1_ref[...] + y, gain_ref[...])


def _combine(dest, ys, x1, wcol, gain):
    t, d = x1.shape
    tm = ROW_TILE
    row = lambda w: pl.BlockSpec((tm, w), lambda i: (i, 0))
    return pl.pallas_call(
        _combine_kernel,
        grid=(t // tm,),
        in_specs=[pl.BlockSpec((TOP_K, tm), lambda i: (0, i), memory_space=pltpu.SMEM),
                  pl.BlockSpec(memory_space=pl.ANY), row(d), row(LANES),
                  pl.BlockSpec(gain.shape, lambda i: (0, 0))],
        out_specs=row(d),
        out_shape=jax.ShapeDtypeStruct((t, d), F32),
        scratch_shapes=[pltpu.VMEM((TOP_K, tm, d), F32), pltpu.SemaphoreType.DMA(())],
        compiler_params=_cparams(("arbitrary",)),
        name="combine",
    )(dest, ys, x1, wcol, gain)


def _pad_heads(w):
    r = w.shape[0]
    return jnp.pad(w.reshape(r, H_A, DK_A), ((0, 0), (0, 0), (0, LANES - DK_A))).reshape(r, H_A * LANES)


def _prepare_weights(w_in, w_gla_gate_up, b_gla_gate, w_branch, w_out, w_router_group, b_router_group,
                     w_router_expert, b_router_expert):
    d = w_in.shape[0]
    qk = H_A * DK_A
    mw = H_A * DV_A
    c = 0
    w_qa, c = w_in[:, c:c + qk], c + qk
    w_ka, c = w_in[:, c:c + qk], c + qk
    w_va, c = w_in[:, c:c + mw], c + mw
    w_ra, c = w_in[:, c:c + mw], c + mw
    w_lr, c = w_in[:, c:c + GATE_RANK], c + GATE_RANK
    w_b, c = w_in[:, c:c + 3 * mw], c + 3 * mw
    w_g = w_in[:, c:]
    wa = jnp.concatenate([_pad_heads(w_qa), _pad_heads(w_ka), w_va, w_ra,
                          jnp.pad(w_lr, ((0, 0), (0, LANES - GATE_RANK)))], axis=1).astype(BF16)
    wgu = jnp.pad(_pad_heads(w_gla_gate_up), ((0, LANES - GATE_RANK), (0, 0))).astype(BF16)
    bgu = _pad_heads(b_gla_gate[None, :])
    wr = jnp.zeros((LANES, d), F32)
    wr = wr.at[0:N_GROUPS].set(w_router_group.T).at[8:8 + N_EXPERTS].set(w_router_expert.T)
    br = jnp.zeros((LANES,), F32).at[0:N_GROUPS].set(b_router_group).at[8:8 + N_EXPERTS].set(b_router_expert)
    br = jnp.broadcast_to(br[:, None], (LANES, LANES))
    return dict(wa=wa, wb=w_b.astype(BF16), wg=w_g.astype(BF16), wgu=wgu, bgu=bgu,
                wb0=w_branch[0].astype(BF16), wb1=w_branch[1].astype(BF16), wo=w_out.astype(BF16),
                wr=wr, br=br)


def _mixers(x, s0, k_past, v_past, w, norm_mix_gain, gla_norm_gain, norm_ffn_gain):
    b, s, d = x.shape
    xf = x.reshape(b * s, d)
    qa, ka, va, ra, la, qb, kb, vb, kb16, vb16, gbr = _in_projection(
        xf, norm_mix_gain[None, :], w["wa"], w["wb"], w["wg"], w["wgu"], w["bgu"])
    seq = lambda a: a.reshape(b, s, a.shape[-1])
    oa, s_new = _gla(seq(qa), seq(ka), seq(va), seq(ra), seq(la), s0, gla_norm_gain[None, :],
                     min(s, ROW_TILE))
    if k_past is None:
        ob = _sb_prompt(seq(qb), seq(kb16), seq(vb16))
    else:
        past = k_past.shape[1]
        ob = _sb_sample(seq(qb), seq(kb16), seq(vb16), k_past.reshape(b, past, -1), v_past.reshape(b, past, -1))
    x1, h2, eid, wcol = _merge(oa.reshape(b * s, -1), ob.reshape(b * s, -1), gbr, xf, w["wb0"], w["wb1"],
                               w["wo"], norm_ffn_gain[None, :], w["wr"], w["br"])
    return x1, h2, eid, wcol, s_new, kb.reshape(b, s, H_B, DH_B), vb.reshape(b, s, H_B, DH_B)


def kernel(x_prompt, x_sample, state_gla, cache_sb_k, cache_sb_v, norm_mix_gain, w_in, w_gla_gate_up, b_gla_gate, gla_norm_gain, w_branch, w_out, norm_ffn_gain, w_router_group, b_router_group, w_router_expert, b_router_expert, w_exp_gate, w_exp_up, w_exp_down, norm_final_gain):
    depth = w_in.shape[0]
    assert depth == 1, "one trunk layer per step"
    l = 0
    w = _prepare_weights(w_in[l], w_gla_gate_up[l], b_gla_gate[l], w_branch[l], w_out[l], w_router_group[l],
                         b_router_group[l], w_router_expert[l], b_router_expert[l])
    bp, sp, d = x_prompt.shape
    bs, ss, _ = x_sample.shape
    s0 = jnp.zeros((bp, H_A, DK_A, DV_A), x_prompt.dtype)
    x1p, h2p, eidp, wcolp, gla_p, k_p, v_p = _mixers(
        x_prompt, s0, None, None, w, norm_mix_gain[l], gla_norm_gain[l], norm_ffn_gain[l])
    x1s, h2s, eids, wcols, gla_s, k_s, v_s = _mixers(
        x_sample, state_gla[l], cache_sb_k[l], cache_sb_v[l], w, norm_mix_gain[l], gla_norm_gain[l],
        norm_ffn_gain[l])

    tp, ts = bp * sp, bs * ss
    eid = jnp.concatenate([eidp, eids], axis=1)
    dest_blocks, counts = _positions(eid.reshape(-1, MOE_TILE))
    dest = dest_blocks.reshape(TOP_K, tp + ts)
    xs = _dispatch(dest, h2p, h2s)
    vblk, vexp, vlo, vhi = _visit_plan(counts[:, 0], TOP_K * (tp + ts))
    ys = _experts(vblk, vexp, vlo, vhi, xs, w_exp_gate[l], w_exp_up[l], w_exp_down[l])
    gf = norm_final_gain[None, :]
    y_prompt = _combine(dest[:, :tp], ys, x1p, wcolp, gf).reshape(bp, sp, d)
    y_sample = _combine(dest[:, tp:], ys, x1s, wcols, gf).reshape(bs, ss, d)
    return (y_prompt, y_sample, gla_p[None], k_p[None], v_p[None], gla_s[None], k_s[None], v_s[None])
```

```python
import functools

import jax
import jax.numpy as jnp
from jax import lax
from jax.experimental import pallas as pl
from jax.experimental.pallas import tpu as pltpu

F32 = jnp.float32
BF16 = jnp.bfloat16
I32 = jnp.int32

LANES = 128
LOG2_E = 1.4426950408889634
RMS_EPS = 1e-6
GATE_TAU = 16.0
H_A = 4
DK_A = 64
DV_A = 128
GATE_RANK = 16
H_B = 8
DH_B = 64
N_GROUPS = 4
EXPERTS_PER_GROUP = 8
N_EXPERTS = N_GROUPS * EXPERTS_PER_GROUP
TOP_K = 2
GLA_CHUNK = 64
GLA_SUB = 16
GLA_EXP_CLAMP = 80.0
ROW_TILE = 256
SB_TILE = 256
MOE_TILE = 512
SORT_WIDTH = 256
DMA_UNROLL = 8
VMEM_LIMIT = 56 * 1024 * 1024


def _cparams(sem):
    return pltpu.CompilerParams(dimension_semantics=sem, vmem_limit_bytes=VMEM_LIMIT)


def _dot(a, b):
    return jnp.dot(a, b, preferred_element_type=F32)


def _dot_nt(a, b):
    return lax.dot_general(a, b, (((1,), (1,)), ((), ())), preferred_element_type=F32)


def _dot_tn(a, b):
    return lax.dot_general(a, b, (((0,), (0,)), ((), ())), preferred_element_type=F32)


def _split_bf16(x):
    hi = x.astype(BF16)
    lo = (x - hi.astype(F32)).astype(BF16)
    return hi, lo


def _log_sigmoid(x):
    return jnp.minimum(x, 0.0) - jnp.log(1.0 + jnp.exp(-jnp.abs(x)))


def _sigmoid(x):
    return 1.0 / (1.0 + jnp.exp(-x))


def _rms_norm(x, gain):
    return x * lax.rsqrt(jnp.mean(x * x, axis=-1, keepdims=True) + RMS_EPS) * gain


def _inproj_kernel(x_ref, gain_ref, wa_ref, wb_ref, wg_ref, wgu_ref, bgu_ref,
                   qa_ref, ka_ref, va_ref, ra_ref, la_ref, qb_ref, kb_ref, vb_ref,
                   kb16_ref, vb16_ref, gbr_ref):
    h = _rms_norm(x_ref[...], gain_ref[...]).astype(BF16)
    pa = H_A * LANES
    mw = va_ref.shape[-1]
    kb = _dot(h, wb_ref[:, mw:2 * mw])
    vb = _dot(h, wb_ref[:, 2 * mw:3 * mw])
    for hb in range(H_B):
        kb_ref[:, hb, :] = kb[:, hb * DH_B:(hb + 1) * DH_B]
        vb_ref[:, hb, :] = vb[:, hb * DH_B:(hb + 1) * DH_B]
    kb16_ref[...] = kb.astype(BF16)
    vb16_ref[...] = vb.astype(BF16)
    qb_ref[...] = _dot(h, wb_ref[:, 0:mw]).astype(BF16)
    qa_ref[...] = _dot(h, wa_ref[:, 0:pa])
    ka_ref[...] = _dot(h, wa_ref[:, pa:2 * pa])
    va_ref[...] = _dot(h, wa_ref[:, 2 * pa:2 * pa + mw])
    ra_ref[...] = _dot(h, wa_ref[:, 2 * pa + mw:2 * pa + 2 * mw])
    lr = _dot(h, wa_ref[:, 2 * pa + 2 * mw:2 * pa + 2 * mw + LANES])
    gl = _dot(lr.astype(BF16), wgu_ref[...]) + bgu_ref[...]
    la_ref[...] = _log_sigmoid(gl) / GATE_TAU
    gbr_ref[...] = _dot(h, wg_ref[...])


def _in_projection(x, gain, wa, wb, wg, wgu, bgu):
    t, d = x.shape
    pa = H_A * LANES
    mw = wb.shape[1] // 3
    tm = ROW_TILE
    row = lambda w: pl.BlockSpec((tm, w), lambda i: (i, 0))
    full = lambda a: pl.BlockSpec(a.shape, lambda i: (0,) * a.ndim)
    out_shapes = [
        jax.ShapeDtypeStruct((t, pa), F32), jax.ShapeDtypeStruct((t, pa), F32),
        jax.ShapeDtypeStruct((t, mw), F32), jax.ShapeDtypeStruct((t, mw), F32),
        jax.ShapeDtypeStruct((t, pa), F32),
        jax.ShapeDtypeStruct((t, mw), BF16),
        jax.ShapeDtypeStruct((t, H_B, DH_B), F32), jax.ShapeDtypeStruct((t, H_B, DH_B), F32),
        jax.ShapeDtypeStruct((t, mw), BF16), jax.ShapeDtypeStruct((t, mw), BF16),
        jax.ShapeDtypeStruct((t, wg.shape[1]), F32),
    ]
    spec = lambda s: pl.BlockSpec((tm,) + s.shape[1:], lambda i: (i,) + (0,) * (len(s.shape) - 1))
    return pl.pallas_call(
        _inproj_kernel,
        grid=(t // tm,),
        in_specs=[row(d), full(gain), full(wa), full(wb), full(wg), full(wgu), full(bgu)],
        out_specs=[spec(s) for s in out_shapes],
        out_shape=out_shapes,
        compiler_params=_cparams(("arbitrary",)),
        name="in_projection",
    )(x, gain, wa, wb, wg, wgu, bgu)


def _gla_chunk(q, k, v, b, st):
    c = q.shape[0]
    b_last = b[c - 1:c, :]
    rows = lax.broadcasted_iota(I32, (c, LANES), 0)
    nsub = c // GLA_SUB
    refs = [jnp.zeros((1, LANES), F32)] + [b[i * GLA_SUB - 1:i * GLA_SUB, :] for i in range(1, nsub)]
    ref_rows = refs[0]
    for i in range(1, nsub):
        ref_rows = jnp.where(rows >= i * GLA_SUB, refs[i], ref_rows)
    q_rel = q * jnp.exp(b - ref_rows)
    lhs = jnp.concatenate(
        [jnp.where((rows >= i * GLA_SUB) & (rows < (i + 1) * GLA_SUB), q_rel, 0.0) for i in range(nsub)],
        axis=1).astype(BF16)
    rhs = jnp.concatenate(
        [jnp.where(rows < (i + 1) * GLA_SUB, k * jnp.exp(jnp.minimum(refs[i] - b, GLA_EXP_CLAMP)), 0.0)
         for i in range(nsub)], axis=1).astype(BF16)
    att = _dot_nt(lhs, rhs)
    tt = lax.broadcasted_iota(I32, (c, c), 0)
    ss = lax.broadcasted_iota(I32, (c, c), 1)
    att = jnp.where(ss <= tt, att, 0.0)
    v16 = v.astype(BF16)
    inter = _dot_nt((q * jnp.exp(b)).astype(BF16), st.astype(BF16))
    intra = _dot(att.astype(BF16), v16)
    kd = (k * jnp.exp(b_last - b)).astype(BF16)
    st_new = st * jnp.exp(b_last) + _dot_tn(v16, kd)
    return inter + intra, st_new


def _gla_kernel(qa_ref, ka_ref, va_ref, ra_ref, la_ref, s0_ref, gain_ref, o_ref, sfin_ref, st_ref):
    j = pl.program_id(1)
    nj = pl.num_programs(1)
    rows_per_step = qa_ref.shape[1]
    c = GLA_CHUNK
    zpad = jnp.zeros((LANES - DK_A, DV_A), F32)

    @pl.when(j == 0)
    def _():
        for h in range(H_A):
            st_ref[h] = jnp.concatenate([s0_ref[0, h], zpad], axis=0).T

    ti = lax.broadcasted_iota(I32, (rows_per_step, rows_per_step), 0)
    si = lax.broadcasted_iota(I32, (rows_per_step, rows_per_step), 1)
    chunk_shift = c.bit_length() - 1
    same_chunk = (ti >> chunk_shift) == (si >> chunk_shift)
    tril_blocks = jnp.where(same_chunk & (si <= ti), 1.0, 0.0).astype(BF16)
    la_hi, la_lo = _split_bf16(la_ref[0])
    b_all = _dot(tril_blocks, la_hi) + _dot(tril_blocks, la_lo)
    gain = gain_ref[...]
    for h in range(H_A):
        hp = slice(h * LANES, (h + 1) * LANES)
        hv = slice(h * DV_A, (h + 1) * DV_A)
        st = st_ref[h]
        for ci in range(rows_per_step // c):
            r0 = ci * c
            q = qa_ref[0, r0:r0 + c, hp] * (DK_A ** -0.5)
            o, st = _gla_chunk(q, ka_ref[0, r0:r0 + c, hp], va_ref[0, r0:r0 + c, hv], b_all[r0:r0 + c, hp], st)
            r = ra_ref[0, r0:r0 + c, hv]
            o = _rms_norm(o, gain) * (r * _sigmoid(r))
            o_ref[0, r0:r0 + c, hv] = o.astype(o_ref.dtype)
        st_ref[h] = st

    @pl.when(j == nj - 1)
    def _():
        for h in range(H_A):
            sfin_ref[0, h] = st_ref[h].T[0:DK_A, :]


def _gla(qa, ka, va, ra, la, s0, gain, rows_per_step):
    b, s, pa = qa.shape
    mw = va.shape[-1]
    seq = lambda w: pl.BlockSpec((1, rows_per_step, w), lambda i, j: (i, j, 0))
    state = pl.BlockSpec((1, H_A, DK_A, DV_A), lambda i, j: (i, 0, 0, 0))
    return pl.pallas_call(
        _gla_kernel,
        grid=(b, s // rows_per_step),
        in_specs=[seq(pa), seq(pa), seq(mw), seq(mw), seq(pa), state,
                  pl.BlockSpec(gain.shape, lambda i, j: (0, 0))],
        out_specs=[seq(mw), state],
        out_shape=[jax.ShapeDtypeStruct((b, s, mw), BF16),
                   jax.ShapeDtypeStruct((b, H_A, DK_A, DV_A), F32)],
        scratch_shapes=[pltpu.VMEM((H_A, LANES, LANES), F32)],
        compiler_params=_cparams(("arbitrary", "arbitrary")),
        name="gla",
    )(qa, ka, va, ra, la, s0, gain)


def _head_lane_masks():
    lane = lax.broadcasted_iota(I32, (1, LANES), 1)
    return lane < DH_B, lane >= DH_B


def _sb_neg_tri(tk):
    ji = lax.broadcasted_iota(I32, (tk, tk), 0)
    si = lax.broadcasted_iota(I32, (tk, tk), 1)
    return jnp.where(ji >= si, -1.0, 0.0).astype(BF16)


def _sb_stack_queries(q, qs_ref):
    m0, m1 = _head_lane_masks()
    for p in range(qs_ref.shape[0]):
        qp = q[:, p * LANES:(p + 1) * LANES] * (DH_B ** -0.5)
        zero = jnp.zeros_like(qp)
        qs_ref[p] = jnp.concatenate([jnp.where(m0, qp, zero), jnp.where(m1, qp, zero)], axis=0)


def _pair_lanes(p):
    return slice(p * LANES, (p + 1) * LANES)


def _lane_fit(x, width):
    if width >= LANES:
        return jnp.concatenate([x] * (width // LANES), axis=1)
    return x[:, 0:width]


def _sb_tile_step(qs_ref, acc_ref, carry_ref, k_tile, v_tile, ntri, diagonal):
    npair, rows, _ = qs_ref.shape
    tq = rows // 2
    tk = ntri.shape[1]
    m0, _ = _head_lane_masks()
    if diagonal:
        t = lax.broadcasted_iota(I32, (rows, tk), 0)
        t = jnp.where(t >= tq, t - tq, t)
        visible = lax.broadcasted_iota(I32, (rows, tk), 1) < t
    for p in range(npair):
        z = _dot_nt(qs_ref[p], k_tile(p)) * LOG2_E
        sp = jnp.maximum(z, 0.0) + jnp.log2(1.0 + jnp.exp2(-jnp.abs(z)))
        if diagonal:
            sp = jnp.where(visible, sp, 0.0)
        suffix = _dot(sp.astype(BF16), ntri)
        carry = carry_ref[p]
        w = jnp.exp2(z + suffix + _lane_fit(carry, tk))
        if diagonal:
            w = jnp.where(visible, w, 0.0)
        pv = _dot(w.astype(BF16), v_tile(p))
        acc_ref[p] += jnp.where(m0, pv[0:tq], pv[tq:rows])
        carry_ref[p] = carry + jnp.broadcast_to(suffix[:, 0:1], carry.shape)


def _sb_prompt_kernel(q_ref, k_ref, v_ref, o_ref, qs_ref, acc_ref, carry_ref):
    qi = pl.program_id(1)
    tk = SB_TILE
    _sb_stack_queries(q_ref[0], qs_ref)
    acc_ref[...] = jnp.zeros_like(acc_ref)
    carry_ref[...] = jnp.zeros_like(carry_ref)
    ntri = _sb_neg_tri(tk)

    def step(jb, diagonal):
        start = pl.multiple_of(jb * tk, tk)
        _sb_tile_step(qs_ref, acc_ref, carry_ref,
                      lambda p: k_ref[0, pl.ds(start, tk), _pair_lanes(p)],
                      lambda p: v_ref[0, pl.ds(start, tk), _pair_lanes(p)], ntri, diagonal)

    step(qi, True)

    def body(i, c):
        step(qi - 1 - i, False)
        return c

    lax.fori_loop(0, qi, body, 0)
    for p in range(acc_ref.shape[0]):
        o_ref[0, :, p * LANES:(p + 1) * LANES] = acc_ref[p].astype(o_ref.dtype)


def _sb_scratch(tq, npair):
    return [pltpu.VMEM((npair, 2 * tq, LANES), BF16), pltpu.VMEM((npair, tq, LANES), F32),
            pltpu.VMEM((npair, 2 * tq, LANES), F32)]


def _sb_prompt(q, k, v):
    b, s, w = q.shape
    tq = SB_TILE
    qspec = pl.BlockSpec((1, tq, w), lambda i, j: (i, j, 0))
    kvspec = pl.BlockSpec((1, s, w), lambda i, j: (i, 0, 0))
    return pl.pallas_call(
        _sb_prompt_kernel,
        grid=(b, s // tq),
        in_specs=[qspec, kvspec, kvspec],
        out_specs=qspec,
        out_shape=jax.ShapeDtypeStruct((b, s, w), BF16),
        scratch_shapes=_sb_scratch(tq, w // LANES),
        compiler_params=_cparams(("arbitrary", "arbitrary")),
        name="sb_prompt",
    )(q, k, v)


def _sb_sample_kernel(q_ref, kn_ref, vn_ref, kp_ref, vp_ref, o_ref, qs_ref, acc_ref, carry_ref,
                      kbuf_ref, vbuf_ref, sems):
    b = pl.program_id(0)
    nb = pl.num_programs(0)
    sq = q_ref.shape[1]
    past = kp_ref.shape[1]
    tk = SB_TILE
    nt = past // tk

    def tile_copies(batch, tile, slot):
        rows = pl.ds(past - (tile + 1) * tk, tk)
        cps = []
        for hb in range(H_B):
            cps.append(pltpu.make_async_copy(kp_ref.at[batch, rows, hb, :], kbuf_ref.at[slot, hb], sems.at[slot]))
            cps.append(pltpu.make_async_copy(vp_ref.at[batch, rows, hb, :], vbuf_ref.at[slot, hb], sems.at[slot]))
        return cps

    @pl.when(b == 0)
    def _():
        for cp in tile_copies(0, 0, 0):
            cp.start()

    _sb_stack_queries(q_ref[0], qs_ref)
    acc_ref[...] = jnp.zeros_like(acc_ref)
    carry_ref[...] = jnp.zeros_like(carry_ref)
    _sb_tile_step(qs_ref, acc_ref, carry_ref, lambda p: kn_ref[0, :, _pair_lanes(p)],
                  lambda p: vn_ref[0, :, _pair_lanes(p)], _sb_neg_tri(sq), True)
    ntri = _sb_neg_tri(tk)

    def pair(buf_ref, slot, p):
        return jnp.concatenate([buf_ref[slot, 2 * p], buf_ref[slot, 2 * p + 1]], axis=1).astype(BF16)

    def body(i2, c):
        for slot in range(2):
            tile = 2 * i2 + slot
            nxt = tile + 1

            @pl.when(nxt < nt)
            def _():
                for cp in tile_copies(b, nxt, 1 - slot):
                    cp.start()

            @pl.when(jnp.logical_and(nxt == nt, b + 1 < nb))
            def _():
                for cp in tile_copies(b + 1, 0, 1 - slot):
                    cp.start()

            for cp in tile_copies(b, tile, slot):
                cp.wait()
            _sb_tile_step(qs_ref, acc_ref, carry_ref, lambda p: pair(kbuf_ref, slot, p),
                          lambda p: pair(vbuf_ref, slot, p), ntri, False)
        return c

    lax.fori_loop(0, nt // 2, body, 0)
    for p in range(acc_ref.shape[0]):
        o_ref[0, :, p * LANES:(p + 1) * LANES] = acc_ref[p].astype(o_ref.dtype)


def _sb_sample(q, k_new, v_new, k_past, v_past):
    b, sq, w = q.shape
    past = k_past.shape[1]
    assert (past // SB_TILE) % 2 == 0 and past % SB_TILE == 0
    new = pl.BlockSpec((1, sq, w), lambda i: (i, 0, 0))
    old = pl.BlockSpec(memory_space=pl.ANY)
    ring = pltpu.VMEM((2, H_B, SB_TILE, DH_B), F32)
    return pl.pallas_call(
        _sb_sample_kernel,
        grid=(b,),
        in_specs=[new, new, new, old, old],
        out_specs=new,
        out_shape=jax.ShapeDtypeStruct((b, sq, w), BF16),
        scratch_shapes=_sb_scratch(sq, w // LANES) + [ring, ring, pltpu.SemaphoreType.DMA((2,))],
        compiler_params=_cparams(("arbitrary",)),
        name="sb_sample",
    )(q, k_new, v_new, k_past, v_past)


def _first_argmax(vals, nrows):
    idx = lax.broadcasted_iota(I32, vals.shape, 0)
    top = jnp.max(vals, axis=0, keepdims=True)
    first = jnp.min(jnp.where(vals == top, idx, nrows), axis=0, keepdims=True)
    return top, first, idx


def _merge_kernel(oa_ref, ob_ref, g_ref, x_ref, wb0_ref, wb1_ref, wo_ref, gain_ref, wr_ref, br_ref,
                  x1_ref, h2_ref, eid_ref, wcol_ref):
    d = x_ref.shape[1]
    ya = _dot(oa_ref[...], wb0_ref[...])
    yb = _dot(ob_ref[...], wb1_ref[...])
    m = _sigmoid(g_ref[:, 0:d]) * ya + _sigmoid(g_ref[:, d:2 * d]) * yb
    x1 = x_ref[...] + _dot(m.astype(BF16), wo_ref[...])
    x1_ref[...] = x1
    h2 = _rms_norm(x1, gain_ref[...])
    h2_ref[...] = h2

    h_hi, h_lo = _split_bf16(h2)
    w_hi, w_lo = _split_bf16(wr_ref[...])
    lt = _dot_nt(w_hi, h_hi) + _dot_nt(w_hi, h_lo) + _dot_nt(w_lo, h_hi) + br_ref[:, 0:1]
    gl = lt[0:N_GROUPS, :]
    g_top, g_idx, _ = _first_argmax(gl, N_GROUPS)
    g_e = jnp.exp(gl - g_top)
    g_p = jnp.max(g_e / jnp.sum(g_e, axis=0, keepdims=True), axis=0, keepdims=True)
    el = jnp.zeros((EXPERTS_PER_GROUP, lt.shape[1]), F32)
    for g in range(N_GROUPS):
        r0 = 8 + g * EXPERTS_PER_GROUP
        el = jnp.where(g_idx == g, lt[r0:r0 + EXPERTS_PER_GROUP, :], el)
    e_top, i1, eidx = _first_argmax(el, EXPERTS_PER_GROUP)
    e_e = jnp.exp(el - e_top)
    e_p = e_e / jnp.sum(e_e, axis=0, keepdims=True)
    p1 = jnp.max(e_p, axis=0, keepdims=True)
    rest = jnp.where(eidx == i1, -1.0, e_p)
    p2, i2, _ = _first_argmax(rest, EXPERTS_PER_GROUP)
    norm = p1 + p2
    w1 = g_p * (p1 / norm)
    w2 = g_p * (p2 / norm)
    eid_ref[...] = jnp.concatenate([g_idx * EXPERTS_PER_GROUP + i1, g_idx * EXPERTS_PER_GROUP + i2], axis=0)
    rows = lax.broadcasted_iota(I32, (LANES, lt.shape[1]), 0)
    wrows = jnp.where(rows == 0, w1, jnp.where(rows == 1, w2, 0.0))
    wcol_ref[...] = wrows.T


def _merge(oa, ob, gbr, x, wb0, wb1, wo, gain, wr, br):
    t, d = x.shape
    tm = ROW_TILE
    row = lambda w: pl.BlockSpec((tm, w), lambda i: (i, 0))
    full = lambda a: pl.BlockSpec(a.shape, lambda i: (0,) * a.ndim)
    return pl.pallas_call(
        _merge_kernel,
        grid=(t // tm,),
        in_specs=[row(oa.shape[1]), row(ob.shape[1]), row(gbr.shape[1]), row(d),
                  full(wb0), full(wb1), full(wo), full(gain), full(wr), full(br)],
        out_specs=[row(d), row(d), pl.BlockSpec((TOP_K, tm), lambda i: (0, i)), row(LANES)],
        out_shape=[jax.ShapeDtypeStruct((t, d), F32), jax.ShapeDtypeStruct((t, d), F32),
                   jax.ShapeDtypeStruct((TOP_K, t), I32), jax.ShapeDtypeStruct((t, LANES), F32)],
        compiler_params=_cparams(("arbitrary",)),
        name="merge_router",
    )(oa, ob, gbr, x, wb0, wb1, wo, gain, wr, br)


def _positions_kernel(eid_ref, dest_ref, counts_ref, rank_ref):
    nblk, width = eid_ref.shape
    ji = lax.broadcasted_iota(I32, (width, width), 0)
    si = lax.broadcasted_iota(I32, (width, width), 1)
    prefix = jnp.where(ji <= si, 1.0, 0.0).astype(BF16)
    expert = lax.broadcasted_iota(I32, (N_EXPERTS, width), 0)

    def onehot(i):
        return expert == eid_ref[pl.ds(i, 1), :]

    def rank_body(i, run):
        oh = onehot(i)
        cum = _dot(jnp.where(oh, 1.0, 0.0).astype(BF16), prefix) + run
        rank_ref[pl.ds(i, 1), :] = jnp.sum(jnp.where(oh, cum, 0.0), axis=0, keepdims=True) - 1.0
        return cum[:, width - 1:width]

    counts = lax.fori_loop(0, nblk, rank_body, jnp.zeros((N_EXPERTS, 1), F32))
    counts_ref[...] = jnp.broadcast_to(counts, counts_ref.shape).astype(I32)
    c_hi = jnp.floor(counts * (1.0 / 256.0))
    c_lo = counts - 256.0 * c_hi
    ei = lax.broadcasted_iota(I32, (N_EXPERTS, N_EXPERTS), 0)
    ej = lax.broadcasted_iota(I32, (N_EXPERTS, N_EXPERTS), 1)
    strict = jnp.where(ej < ei, 1.0, 0.0).astype(BF16)
    digits = jnp.concatenate([jnp.broadcast_to(c_hi, (N_EXPERTS, LANES)),
                              jnp.broadcast_to(c_lo, (N_EXPERTS, LANES))], axis=1).astype(BF16)
    sums = _dot(strict, digits)
    start = 256.0 * sums[:, 0:1] + sums[:, LANES:LANES + 1]

    def dest_body(i, carry):
        off = jnp.sum(jnp.where(onehot(i), start, 0.0), axis=0, keepdims=True)
        dest_ref[pl.ds(i, 1), :] = (rank_ref[pl.ds(i, 1), :] + off).astype(I32)
        return carry

    lax.fori_loop(0, nblk, dest_body, 0)


def _positions(eid_blocks):
    nblk, width = eid_blocks.shape
    vm = lambda shape: pl.BlockSpec(shape, lambda: (0,) * len(shape))
    return pl.pallas_call(
        _positions_kernel,
        in_specs=[vm((nblk, width))],
        out_specs=[vm((nblk, width)), vm((N_EXPERTS, LANES))],
        out_shape=[jax.ShapeDtypeStruct((nblk, width), I32), jax.ShapeDtypeStruct((N_EXPERTS, LANES), I32)],
        scratch_shapes=[pltpu.VMEM((nblk, width), F32)],
        name="positions",
    )(eid_blocks)


def _dispatch_kernel(n_prompt_tiles, dest_ref, hp_ref, hs_ref, xs_ref, sem):
    i = pl.program_id(0)
    tm = dest_ref.shape[1]

    def scatter(src_ref):
        def start(r, c):
            for k in range(TOP_K):
                pltpu.make_async_copy(src_ref.at[pl.ds(r, 1)], xs_ref.at[pl.ds(dest_ref[k, r], 1)], sem).start()
            return c

        lax.fori_loop(0, tm, start, 0, unroll=DMA_UNROLL)
        for k in range(TOP_K):
            pltpu.make_async_copy(src_ref, xs_ref.at[pl.ds(0, tm)], sem).wait()

    @pl.when(i < n_prompt_tiles)
    def _():
        scatter(hp_ref)

    @pl.when(i >= n_prompt_tiles)
    def _():
        scatter(hs_ref)


def _dispatch(dest, h_prompt, h_sample):
    t = dest.shape[1]
    d = h_prompt.shape[1]
    tm = ROW_TILE
    npt = h_prompt.shape[0] // tm
    return pl.pallas_call(
        functools.partial(_dispatch_kernel, npt),
        grid=(t // tm,),
        in_specs=[pl.BlockSpec((TOP_K, tm), lambda i: (0, i), memory_space=pltpu.SMEM),
                  pl.BlockSpec((tm, d), lambda i: (jnp.minimum(i, npt - 1), 0)),
                  pl.BlockSpec((tm, d), lambda i: (jnp.maximum(i - npt, 0), 0))],
        out_specs=pl.BlockSpec(memory_space=pl.ANY),
        out_shape=jax.ShapeDtypeStruct((TOP_K * t, d), F32),
        scratch_shapes=[pltpu.SemaphoreType.DMA(())],
        compiler_params=_cparams(("arbitrary",)),
        name="dispatch",
    )(dest, h_prompt, h_sample)


def _experts_kernel(vblk_ref, vexp_ref, vlo_ref, vhi_ref, xs_ref, wg_ref, wu_ref, wd_ref, ys_ref,
                    wg16_ref, wu16_ref, wd16_ref):
    v = pl.program_id(0)
    lo = vlo_ref[v]
    hi = vhi_ref[v]
    prev = jnp.maximum(v - 1, 0)
    first = jnp.logical_or(v == 0, vblk_ref[v] != vblk_ref[prev])
    new_expert = jnp.logical_or(v == 0, vexp_ref[v] != vexp_ref[prev])

    @pl.when(new_expert)
    def _():
        wg16_ref[...] = wg_ref[0].astype(BF16)
        wu16_ref[...] = wu_ref[0].astype(BF16)
        wd16_ref[...] = wd_ref[0].astype(BF16)

    @pl.when(hi > lo)
    def _():
        x = xs_ref[...].astype(BF16)
        gate = _dot(x, wg16_ref[...])
        up = _dot(x, wu16_ref[...])
        hid = (gate * _sigmoid(gate) * up).astype(BF16)
        y = _dot(hid, wd16_ref[...])
        rows = lax.broadcasted_iota(I32, y.shape, 0)
        mine = (rows >= lo) & (rows < hi)

        @pl.when(first)
        def _():
            ys_ref[...] = jnp.where(mine, y, 0.0)

        @pl.when(jnp.logical_not(first))
        def _():
            ys_ref[...] = jnp.where(mine, y, ys_ref[...])


def _experts(vblk, vexp, vlo, vhi, xs, wg, wu, wd):
    a, d = xs.shape
    de = wg.shape[2]
    tm = MOE_TILE
    grid_spec = pltpu.PrefetchScalarGridSpec(
        num_scalar_prefetch=4,
        grid=(vblk.shape[0],),
        in_specs=[pl.BlockSpec((tm, d), lambda v, b, e, lo, hi: (b[v], 0)),
                  pl.BlockSpec((1, d, de), lambda v, b, e, lo, hi: (e[v], 0, 0)),
                  pl.BlockSpec((1, d, de), lambda v, b, e, lo, hi: (e[v], 0, 0)),
                  pl.BlockSpec((1, de, d), lambda v, b, e, lo, hi: (e[v], 0, 0))],
        out_specs=pl.BlockSpec((tm, d), lambda v, b, e, lo, hi: (b[v], 0)),
        scratch_shapes=[pltpu.VMEM((d, de), BF16), pltpu.VMEM((d, de), BF16), pltpu.VMEM((de, d), BF16)],
    )
    return pl.pallas_call(
        _experts_kernel,
        grid_spec=grid_spec,
        out_shape=jax.ShapeDtypeStruct((a, d), F32),
        compiler_params=_cparams(("arbitrary",)),
        name="experts",
    )(vblk, vexp, vlo, vhi, xs, wg, wu, wd)


def _visit_plan(counts, n_rows):
    tm = MOE_TILE
    nblk = n_rows // tm
    n_visits = nblk + N_EXPERTS - 1
    ends = jnp.cumsum(counts)
    starts = ends - counts
    first_blk = starts // tm
    nvis = jnp.where(counts > 0, (ends + tm - 1) // tm - first_blk, 0)
    vis_end = jnp.cumsum(nvis)
    vis_start = vis_end - nvis
    v = jnp.arange(n_visits, dtype=I32)
    e = jnp.minimum(jnp.sum((vis_end[None, :] <= v[:, None]).astype(I32), axis=1), N_EXPERTS - 1)
    valid = v < vis_end[-1]
    blk = first_blk[e] + (v - vis_start[e])
    lo = jnp.clip(starts[e] - blk * tm, 0, tm)
    hi = jnp.clip(ends[e] - blk * tm, 0, tm)
    last_e = jnp.max(jnp.where(counts > 0, jnp.arange(N_EXPERTS, dtype=I32), 0))
    blk = jnp.where(valid, blk, nblk - 1).astype(I32)
    e = jnp.where(valid, e, last_e).astype(I32)
    lo = jnp.where(valid, lo, 0).astype(I32)
    hi = jnp.where(valid, hi, 0).astype(I32)
    return blk, e, lo, hi


def _combine_kernel(dest_ref, ys_ref, x1_ref, wcol_ref, gain_ref, out_ref, buf_ref, sem):
    tm = x1_ref.shape[0]

    def start(r, c):
        for k in range(TOP_K):
            pltpu.make_async_copy(ys_ref.at[pl.ds(dest_ref[k, r], 1)], buf_ref.at[k, pl.ds(r, 1)], sem).start()
        return c

    lax.fori_loop(0, tm, start, 0, unroll=DMA_UNROLL)
    for k in range(TOP_K):
        pltpu.make_async_copy(ys_ref.at[pl.ds(0, tm)], buf_ref.at[k], sem).wait()
    y = wcol_ref[:, 0:1] * buf_ref[0] + wcol_ref[:, 1:2] * buf_ref[1]
    out_ref[...] = _rms_norm(x1_ref[...] + y, gain_ref[...])


def _combine(dest, ys, x1, wcol, gain):
    t, d = x1.shape
    tm = ROW_TILE
    row = lambda w: pl.BlockSpec((tm, w), lambda i: (i, 0))
    return pl.pallas_call(
        _combine_kernel,
        grid=(t // tm,),
        in_specs=[pl.BlockSpec((TOP_K, tm), lambda i: (0, i), memory_space=pltpu.SMEM),
                  pl.BlockSpec(memory_space=pl.ANY), row(d), row(LANES),
                  pl.BlockSpec(gain.shape, lambda i: (0, 0))],
        out_specs=row(d),
        out_shape=jax.ShapeDtypeStruct((t, d), F32),
        scratch_shapes=[pltpu.VMEM((TOP_K, tm, d), F32), pltpu.SemaphoreType.DMA(())],
        compiler_params=_cparams(("arbitrary",)),
        name="combine",
    )(dest, ys, x1, wcol, gain)


def _pad_heads(w):
    r = w.shape[0]
    return jnp.pad(w.reshape(r, H_A, DK_A), ((0, 0), (0, 0), (0, LANES - DK_A))).reshape(r, H_A * LANES)


def _prepare_weights(w_in, w_gla_gate_up, b_gla_gate, w_branch, w_out, w_router_group, b_router_group,
                     w_router_expert, b_router_expert):
    d = w_in.shape[0]
    qk = H_A * DK_A
    mw = H_A * DV_A
    c = 0
    w_qa, c = w_in[:, c:c + qk], c + qk
    w_ka, c = w_in[:, c:c + qk], c + qk
    w_va, c = w_in[:, c:c + mw], c + mw
    w_ra, c = w_in[:, c:c + mw], c + mw
    w_lr, c = w_in[:, c:c + GATE_RANK], c + GATE_RANK
    w_b, c = w_in[:, c:c + 3 * mw], c + 3 * mw
    w_g = w_in[:, c:]
    wa = jnp.concatenate([_pad_heads(w_qa), _pad_heads(w_ka), w_va, w_ra,
                          jnp.pad(w_lr, ((0, 0), (0, LANES - GATE_RANK)))], axis=1).astype(BF16)
    wgu = jnp.pad(_pad_heads(w_gla_gate_up), ((0, LANES - GATE_RANK), (0, 0))).astype(BF16)
    bgu = _pad_heads(b_gla_gate[None, :])
    wr = jnp.zeros((LANES, d), F32)
    wr = wr.at[0:N_GROUPS].set(w_router_group.T).at[8:8 + N_EXPERTS].set(w_router_expert.T)
    br = jnp.zeros((LANES,), F32).at[0:N_GROUPS].set(b_router_group).at[8:8 + N_EXPERTS].set(b_router_expert)
    br = jnp.broadcast_to(br[:, None], (LANES, LANES))
    return dict(wa=wa, wb=w_b.astype(BF16), wg=w_g.astype(BF16), wgu=wgu, bgu=bgu,
                wb0=w_branch[0].astype(BF16), wb1=w_branch[1].astype(BF16), wo=w_out.astype(BF16),
                wr=wr, br=br)


def _mixers(x, s0, k_past, v_past, w, norm_mix_gain, gla_norm_gain, norm_ffn_gain):
    b, s, d = x.shape
    xf = x.reshape(b * s, d)
    qa, ka, va, ra, la, qb, kb, vb, kb16, vb16, gbr = _in_projection(
        xf, norm_mix_gain[None, :], w["wa"], w["wb"], w["wg"], w["wgu"], w["bgu"])
    seq = lambda a: a.reshape(b, s, a.shape[-1])
    oa, s_new = _gla(seq(qa), seq(ka), seq(va), seq(ra), seq(la), s0, gla_norm_gain[None, :],
                     min(s, ROW_TILE))
    if k_past is None:
        ob = _sb_prompt(seq(qb), seq(kb16), seq(vb16))
    else:
        past = k_past.shape[1]
        ob = _sb_sample(seq(qb), seq(kb16), seq(vb16), k_past, v_past)
    x1, h2, eid, wcol = _merge(oa.reshape(b * s, -1), ob.reshape(b * s, -1), gbr, xf, w["wb0"], w["wb1"],
                               w["wo"], norm_ffn_gain[None, :], w["wr"], w["br"])
    return x1, h2, eid, wcol, s_new, kb.reshape(b, s, H_B, DH_B), vb.reshape(b, s, H_B, DH_B)


def kernel(x_prompt, x_sample, state_gla, cache_sb_k, cache_sb_v, norm_mix_gain, w_in, w_gla_gate_up, b_gla_gate, gla_norm_gain, w_branch, w_out, norm_ffn_gain, w_router_group, b_router_group, w_router_expert, b_router_expert, w_exp_gate, w_exp_up, w_exp_down, norm_final_gain):
    depth = w_in.shape[0]
    assert depth == 1, "one trunk layer per step"
    l = 0
    w = _prepare_weights(w_in[l], w_gla_gate_up[l], b_gla_gate[l], w_branch[l], w_out[l], w_router_group[l],
                         b_router_group[l], w_router_expert[l], b_router_expert[l])
    bp, sp, d = x_prompt.shape
    bs, ss, _ = x_sample.shape
    s0 = jnp.zeros((bp, H_A, DK_A, DV_A), x_prompt.dtype)
    x1p, h2p, eidp, wcolp, gla_p, k_p, v_p = _mixers(
        x_prompt, s0, None, None, w, norm_mix_gain[l], gla_norm_gain[l], norm_ffn_gain[l])
    x1s, h2s, eids, wcols, gla_s, k_s, v_s = _mixers(
        x_sample, state_gla[l], cache_sb_k[l], cache_sb_v[l], w, norm_mix_gain[l], gla_norm_gain[l],
        norm_ffn_gain[l])

    tp, ts = bp * sp, bs * ss
    eid = jnp.concatenate([eidp, eids], axis=1)
    dest_blocks, counts = _positions(eid.reshape(-1, SORT_WIDTH))
    dest = dest_blocks.reshape(TOP_K, tp + ts)
    xs = _dispatch(dest, h2p, h2s)
    vblk, vexp, vlo, vhi = _visit_plan(counts[:, 0], TOP_K * (tp + ts))
    ys = _experts(vblk, vexp, vlo, vhi, xs, w_exp_gate[l], w_exp_up[l], w_exp_down[l])
    gf = norm_final_gain[None, :]
    y_prompt = _combine(dest[:, :tp], ys, x1p, wcolp, gf).reshape(bp, sp, d)
    y_sample = _combine(dest[:, tp:], ys, x1s, wcols, gf).reshape(bs, ss, d)
    return (y_prompt, y_sample, gla_p[None], k_p[None], v_p[None], gla_s[None], k_s[None], v_s[None])
```

```python
import functools

import jax
import jax.numpy as jnp
from jax import lax
from jax.experimental import pallas as pl
from jax.experimental.pallas import tpu as pltpu

F32 = jnp.float32
BF16 = jnp.bfloat16
I32 = jnp.int32

LANES = 128
LOG2_E = 1.4426950408889634
RMS_EPS = 1e-6
GATE_TAU = 16.0
H_A = 4
DK_A = 64
DV_A = 128
GATE_RANK = 16
H_B = 8
DH_B = 64
N_GROUPS = 4
EXPERTS_PER_GROUP = 8
N_EXPERTS = N_GROUPS * EXPERTS_PER_GROUP
TOP_K = 2
GLA_CHUNK = 64
GLA_SUB = 16
GLA_EXP_CLAMP = 80.0
ROW_TILE = 256
SB_TILE = 256
MOE_TILE = 512
SORT_WIDTH = 256
DMA_UNROLL = 8
VMEM_LIMIT = 56 * 1024 * 1024


def _cparams(sem):
    return pltpu.CompilerParams(dimension_semantics=sem, vmem_limit_bytes=VMEM_LIMIT)


def _dot(a, b):
    return jnp.dot(a, b, preferred_element_type=F32)


def _dot_nt(a, b):
    return lax.dot_general(a, b, (((1,), (1,)), ((), ())), preferred_element_type=F32)


def _dot_tn(a, b):
    return lax.dot_general(a, b, (((0,), (0,)), ((), ())), preferred_element_type=F32)


def _split_bf16(x):
    hi = x.astype(BF16)
    lo = (x - hi.astype(F32)).astype(BF16)
    return hi, lo


def _log_sigmoid(x):
    return jnp.minimum(x, 0.0) - jnp.log(1.0 + jnp.exp(-jnp.abs(x)))


def _sigmoid(x):
    return 1.0 / (1.0 + jnp.exp(-x))


def _rms_norm(x, gain):
    return x * lax.rsqrt(jnp.mean(x * x, axis=-1, keepdims=True) + RMS_EPS) * gain


def _inproj_kernel(x_ref, gain_ref, wa_ref, wqb_ref, wkvt_ref, wg_ref, wgu_ref, bgu_ref,
                   qa_ref, ka_ref, va_ref, ra_ref, la_ref, qb_ref, kt_ref, vt_ref,
                   kt16_ref, vt16_ref, gbr_ref):
    h = _rms_norm(x_ref[...], gain_ref[...]).astype(BF16)
    pa = H_A * LANES
    mw = va_ref.shape[-1]
    kvt = _dot_nt(wkvt_ref[...], h)
    nseq, _, s = kt_ref.shape
    for i in range(nseq):
        cols = slice(i * s, (i + 1) * s)
        kt_ref[i] = kvt[0:mw, cols]
        vt_ref[i] = kvt[mw:2 * mw, cols]
        kt16_ref[i, 0] = kvt[0:mw, cols].astype(BF16)
        vt16_ref[i, 0] = kvt[mw:2 * mw, cols].astype(BF16)
    qb_ref[...] = _dot(h, wqb_ref[...]).astype(BF16)
    qa_ref[...] = _dot(h, wa_ref[:, 0:pa])
    ka_ref[...] = _dot(h, wa_ref[:, pa:2 * pa])
    va_ref[...] = _dot(h, wa_ref[:, 2 * pa:2 * pa + mw])
    ra_ref[...] = _dot(h, wa_ref[:, 2 * pa + mw:2 * pa + 2 * mw])
    lr = _dot(h, wa_ref[:, 2 * pa + 2 * mw:2 * pa + 2 * mw + LANES])
    gl = _dot(lr.astype(BF16), wgu_ref[...]) + bgu_ref[...]
    la_ref[...] = _log_sigmoid(gl) / GATE_TAU
    gbr_ref[...] = _dot(h, wg_ref[...])


def _in_projection(x, seq_len, gain, wa, wqb, wkvt, wg, wgu, bgu):
    t, d = x.shape
    nb = t // seq_len
    pa = H_A * LANES
    mw = wqb.shape[1]
    tm = ROW_TILE
    row = lambda w: pl.BlockSpec((tm, w), lambda i: (i, 0))
    full = lambda a: pl.BlockSpec(a.shape, lambda i: (0,) * a.ndim)
    if seq_len >= tm:
        per_seq = seq_len // tm
        assert tm == SB_TILE and seq_len % tm == 0
        kt_spec = pl.BlockSpec((1, mw, tm), lambda i: (i // per_seq, 0, i % per_seq))
        kt16_spec = pl.BlockSpec((1, 1, mw, tm), lambda i: (i // per_seq, i % per_seq, 0, 0))
        kt16_shape = (nb, per_seq, mw, tm)
    else:
        nseq = tm // seq_len
        assert tm % seq_len == 0
        kt_spec = pl.BlockSpec((nseq, mw, seq_len), lambda i: (i, 0, 0))
        kt16_spec = pl.BlockSpec((nseq, 1, mw, seq_len), lambda i: (i, 0, 0, 0))
        kt16_shape = (nb, 1, mw, seq_len)
    outs = [
        (jax.ShapeDtypeStruct((t, pa), F32), row(pa)), (jax.ShapeDtypeStruct((t, pa), F32), row(pa)),
        (jax.ShapeDtypeStruct((t, mw), F32), row(mw)), (jax.ShapeDtypeStruct((t, mw), F32), row(mw)),
        (jax.ShapeDtypeStruct((t, pa), F32), row(pa)),
        (jax.ShapeDtypeStruct((t, mw), BF16), row(mw)),
        (jax.ShapeDtypeStruct((nb, mw, seq_len), F32), kt_spec), (jax.ShapeDtypeStruct((nb, mw, seq_len), F32), kt_spec),
        (jax.ShapeDtypeStruct(kt16_shape, BF16), kt16_spec), (jax.ShapeDtypeStruct(kt16_shape, BF16), kt16_spec),
        (jax.ShapeDtypeStruct((t, wg.shape[1]), F32), row(wg.shape[1])),
    ]
    return pl.pallas_call(
        _inproj_kernel,
        grid=(t // tm,),
        in_specs=[row(d), full(gain), full(wa), full(wqb), full(wkvt), full(wg), full(wgu), full(bgu)],
        out_specs=[spec for _, spec in outs],
        out_shape=[shape for shape, _ in outs],
        compiler_params=_cparams(("arbitrary",)),
        name="in_projection",
    )(x, gain, wa, wqb, wkvt, wg, wgu, bgu)


def _gla_chunk(q, k, v, b, st):
    c = q.shape[0]
    b_last = b[c - 1:c, :]
    rows = lax.broadcasted_iota(I32, (c, LANES), 0)
    nsub = c // GLA_SUB
    refs = [jnp.zeros((1, LANES), F32)] + [b[i * GLA_SUB - 1:i * GLA_SUB, :] for i in range(1, nsub)]
    ref_rows = refs[0]
    for i in range(1, nsub):
        ref_rows = jnp.where(rows >= i * GLA_SUB, refs[i], ref_rows)
    q_rel = q * jnp.exp(b - ref_rows)
    lhs = jnp.concatenate(
        [jnp.where((rows >= i * GLA_SUB) & (rows < (i + 1) * GLA_SUB), q_rel, 0.0) for i in range(nsub)],
        axis=1).astype(BF16)
    rhs = jnp.concatenate(
        [jnp.where(rows < (i + 1) * GLA_SUB, k * jnp.exp(jnp.minimum(refs[i] - b, GLA_EXP_CLAMP)), 0.0)
         for i in range(nsub)], axis=1).astype(BF16)
    att = _dot_nt(lhs, rhs)
    tt = lax.broadcasted_iota(I32, (c, c), 0)
    ss = lax.broadcasted_iota(I32, (c, c), 1)
    att = jnp.where(ss <= tt, att, 0.0)
    v16 = v.astype(BF16)
    inter = _dot_nt((q * jnp.exp(b)).astype(BF16), st.astype(BF16))
    intra = _dot(att.astype(BF16), v16)
    kd = (k * jnp.exp(b_last - b)).astype(BF16)
    st_new = st * jnp.exp(b_last) + _dot_tn(v16, kd)
    return inter + intra, st_new


def _gla_kernel(qa_ref, ka_ref, va_ref, ra_ref, la_ref, s0_ref, gain_ref, o_ref, sfin_ref, st_ref):
    j = pl.program_id(1)
    nj = pl.num_programs(1)
    rows_per_step = qa_ref.shape[1]
    c = GLA_CHUNK
    zpad = jnp.zeros((LANES - DK_A, DV_A), F32)

    @pl.when(j == 0)
    def _():
        for h in range(H_A):
            st_ref[h] = jnp.concatenate([s0_ref[0, h], zpad], axis=0).T

    ti = lax.broadcasted_iota(I32, (rows_per_step, rows_per_step), 0)
    si = lax.broadcasted_iota(I32, (rows_per_step, rows_per_step), 1)
    chunk_shift = c.bit_length() - 1
    same_chunk = (ti >> chunk_shift) == (si >> chunk_shift)
    tril_blocks = jnp.where(same_chunk & (si <= ti), 1.0, 0.0).astype(BF16)
    la_hi, la_lo = _split_bf16(la_ref[0])
    b_all = _dot(tril_blocks, la_hi) + _dot(tril_blocks, la_lo)
    gain = gain_ref[...]
    for h in range(H_A):
        hp = slice(h * LANES, (h + 1) * LANES)
        hv = slice(h * DV_A, (h + 1) * DV_A)
        st = st_ref[h]
        for ci in range(rows_per_step // c):
            r0 = ci * c
            q = qa_ref[0, r0:r0 + c, hp] * (DK_A ** -0.5)
            o, st = _gla_chunk(q, ka_ref[0, r0:r0 + c, hp], va_ref[0, r0:r0 + c, hv], b_all[r0:r0 + c, hp], st)
            r = ra_ref[0, r0:r0 + c, hv]
            o = _rms_norm(o, gain) * (r * _sigmoid(r))
            o_ref[0, r0:r0 + c, hv] = o.astype(o_ref.dtype)
        st_ref[h] = st

    @pl.when(j == nj - 1)
    def _():
        for h in range(H_A):
            sfin_ref[0, h] = st_ref[h].T[0:DK_A, :]


def _gla(qa, ka, va, ra, la, s0, gain, rows_per_step):
    b, s, pa = qa.shape
    mw = va.shape[-1]
    seq = lambda w: pl.BlockSpec((1, rows_per_step, w), lambda i, j: (i, j, 0))
    state = pl.BlockSpec((1, H_A, DK_A, DV_A), lambda i, j: (i, 0, 0, 0))
    return pl.pallas_call(
        _gla_kernel,
        grid=(b, s // rows_per_step),
        in_specs=[seq(pa), seq(pa), seq(mw), seq(mw), seq(pa), state,
                  pl.BlockSpec(gain.shape, lambda i, j: (0, 0))],
        out_specs=[seq(mw), state],
        out_shape=[jax.ShapeDtypeStruct((b, s, mw), BF16),
                   jax.ShapeDtypeStruct((b, H_A, DK_A, DV_A), F32)],
        scratch_shapes=[pltpu.VMEM((H_A, LANES, LANES), F32)],
        compiler_params=_cparams(("arbitrary", "arbitrary")),
        name="gla",
    )(qa, ka, va, ra, la, s0, gain)


def _head_lane_masks():
    lane = lax.broadcasted_iota(I32, (1, LANES), 1)
    return lane < DH_B, lane >= DH_B


def _sb_neg_tri(tk):
    ji = lax.broadcasted_iota(I32, (tk, tk), 0)
    si = lax.broadcasted_iota(I32, (tk, tk), 1)
    return jnp.where(ji >= si, -1.0, 0.0).astype(BF16)


def _sb_stack_queries(q, qs_ref):
    m0, m1 = _head_lane_masks()
    for p in range(qs_ref.shape[0]):
        qp = q[:, p * LANES:(p + 1) * LANES] * (DH_B ** -0.5)
        zero = jnp.zeros_like(qp)
        qs_ref[p] = jnp.concatenate([jnp.where(m0, qp, zero), jnp.where(m1, qp, zero)], axis=0)


def _pair_lanes(p):
    return slice(p * LANES, (p + 1) * LANES)


def _lane_fit(x, width):
    if width >= LANES:
        return jnp.concatenate([x] * (width // LANES), axis=1)
    return x[:, 0:width]


def _sb_tile_step(qs_ref, acc_ref, carry_ref, k_tile, v_tile, ntri, diagonal):
    npair, rows, _ = qs_ref.shape
    tq = rows // 2
    tk = ntri.shape[1]
    m0, _ = _head_lane_masks()
    if diagonal:
        t = lax.broadcasted_iota(I32, (rows, tk), 0)
        t = jnp.where(t >= tq, t - tq, t)
        visible = lax.broadcasted_iota(I32, (rows, tk), 1) < t
    for p in range(npair):
        z = _dot(qs_ref[p], k_tile(p)) * LOG2_E
        sp = jnp.maximum(z, 0.0) + jnp.log2(1.0 + jnp.exp2(-jnp.abs(z)))
        if diagonal:
            sp = jnp.where(visible, sp, 0.0)
        suffix = _dot(sp.astype(BF16), ntri)
        carry = carry_ref[p]
        w = jnp.exp2(z + suffix + _lane_fit(carry, tk))
        if diagonal:
            w = jnp.where(visible, w, 0.0)
        pv = _dot_nt(w.astype(BF16), v_tile(p))
        acc_ref[p] += jnp.where(m0, pv[0:tq], pv[tq:rows])
        carry_ref[p] = carry + jnp.broadcast_to(suffix[:, 0:1], carry.shape)


def _sb_prompt_kernel(q_ref, k_ref, v_ref, o_ref, qs_ref, acc_ref, carry_ref):
    qi = pl.program_id(1)
    tk = SB_TILE
    _sb_stack_queries(q_ref[0], qs_ref)
    acc_ref[...] = jnp.zeros_like(acc_ref)
    carry_ref[...] = jnp.zeros_like(carry_ref)
    ntri = _sb_neg_tri(tk)

    def step(jb, diagonal):
        _sb_tile_step(qs_ref, acc_ref, carry_ref, lambda p: k_ref[0, jb, _pair_lanes(p), :],
                      lambda p: v_ref[0, jb, _pair_lanes(p), :], ntri, diagonal)

    step(qi, True)

    def body(i, c):
        step(qi - 1 - i, False)
        return c

    lax.fori_loop(0, qi, body, 0)
    for p in range(acc_ref.shape[0]):
        o_ref[0, :, p * LANES:(p + 1) * LANES] = acc_ref[p].astype(o_ref.dtype)


def _sb_scratch(tq, npair):
    return [pltpu.VMEM((npair, 2 * tq, LANES), BF16), pltpu.VMEM((npair, tq, LANES), F32),
            pltpu.VMEM((npair, 2 * tq, LANES), F32)]


def _sb_prompt(q, kt, vt):
    b, s, w = q.shape
    tq = SB_TILE
    assert kt.shape == (b, s // tq, w, tq)
    qspec = pl.BlockSpec((1, tq, w), lambda i, j: (i, j, 0))
    kvspec = pl.BlockSpec((1,) + kt.shape[1:], lambda i, j: (i, 0, 0, 0))
    return pl.pallas_call(
        _sb_prompt_kernel,
        grid=(b, s // tq),
        in_specs=[qspec, kvspec, kvspec],
        out_specs=qspec,
        out_shape=jax.ShapeDtypeStruct((b, s, w), BF16),
        scratch_shapes=_sb_scratch(tq, w // LANES),
        compiler_params=_cparams(("arbitrary", "arbitrary")),
        name="sb_prompt",
    )(q, kt, vt)


def _sb_sample_kernel(q_ref, kn_ref, vn_ref, kp_ref, vp_ref, o_ref, qs_ref, acc_ref, carry_ref):
    past = kp_ref.shape[3]
    sq = q_ref.shape[1]
    tk = SB_TILE
    _sb_stack_queries(q_ref[0], qs_ref)
    acc_ref[...] = jnp.zeros_like(acc_ref)
    carry_ref[...] = jnp.zeros_like(carry_ref)
    _sb_tile_step(qs_ref, acc_ref, carry_ref, lambda p: kn_ref[0, 0, _pair_lanes(p), :],
                  lambda p: vn_ref[0, 0, _pair_lanes(p), :], _sb_neg_tri(sq), True)
    ntri = _sb_neg_tri(tk)

    def body(i, c):
        cols = pl.ds(pl.multiple_of(past - (i + 1) * tk, tk), tk)

        def pair(ref, p):
            return ref[0, 2 * p:2 * p + 2, :, cols].reshape(LANES, tk).astype(BF16)

        _sb_tile_step(qs_ref, acc_ref, carry_ref, lambda p: pair(kp_ref, p), lambda p: pair(vp_ref, p), ntri, False)
        return c

    lax.fori_loop(0, past // tk, body, 0)
    for p in range(acc_ref.shape[0]):
        o_ref[0, :, p * LANES:(p + 1) * LANES] = acc_ref[p].astype(o_ref.dtype)


def _sb_sample(q, kt_new, vt_new, kt_past, vt_past):
    b, sq, w = q.shape
    past = kt_past.shape[3]
    assert past % SB_TILE == 0 and 2 * DH_B == LANES
    qspec = pl.BlockSpec((1, sq, w), lambda i: (i, 0, 0))
    new = pl.BlockSpec((1, 1, w, sq), lambda i: (i, 0, 0, 0))
    old = pl.BlockSpec((1, H_B, DH_B, past), lambda i: (i, 0, 0, 0))
    return pl.pallas_call(
        _sb_sample_kernel,
        grid=(b,),
        in_specs=[qspec, new, new, old, old],
        out_specs=qspec,
        out_shape=jax.ShapeDtypeStruct((b, sq, w), BF16),
        scratch_shapes=_sb_scratch(sq, w // LANES),
        compiler_params=_cparams(("arbitrary",)),
        name="sb_sample",
    )(q, kt_new, vt_new, kt_past, vt_past)


def _first_argmax(vals, nrows):
    idx = lax.broadcasted_iota(I32, vals.shape, 0)
    top = jnp.max(vals, axis=0, keepdims=True)
    first = jnp.min(jnp.where(vals == top, idx, nrows), axis=0, keepdims=True)
    return top, first, idx


def _merge_kernel(oa_ref, ob_ref, g_ref, x_ref, wb0_ref, wb1_ref, wo_ref, gain_ref, wr_ref, br_ref,
                  x1_ref, h2_ref, eid_ref, wcol_ref):
    d = x_ref.shape[1]
    ya = _dot(oa_ref[...], wb0_ref[...])
    yb = _dot(ob_ref[...], wb1_ref[...])
    m = _sigmoid(g_ref[:, 0:d]) * ya + _sigmoid(g_ref[:, d:2 * d]) * yb
    x1 = x_ref[...] + _dot(m.astype(BF16), wo_ref[...])
    x1_ref[...] = x1
    h2 = _rms_norm(x1, gain_ref[...])
    h2_ref[...] = h2

    h_hi, h_lo = _split_bf16(h2)
    w_hi, w_lo = _split_bf16(wr_ref[...])
    lt = _dot_nt(w_hi, h_hi) + _dot_nt(w_hi, h_lo) + _dot_nt(w_lo, h_hi) + br_ref[:, 0:1]
    gl = lt[0:N_GROUPS, :]
    g_top, g_idx, _ = _first_argmax(gl, N_GROUPS)
    g_e = jnp.exp(gl - g_top)
    g_p = jnp.max(g_e / jnp.sum(g_e, axis=0, keepdims=True), axis=0, keepdims=True)
    el = jnp.zeros((EXPERTS_PER_GROUP, lt.shape[1]), F32)
    for g in range(N_GROUPS):
        r0 = 8 + g * EXPERTS_PER_GROUP
        el = jnp.where(g_idx == g, lt[r0:r0 + EXPERTS_PER_GROUP, :], el)
    e_top, i1, eidx = _first_argmax(el, EXPERTS_PER_GROUP)
    e_e = jnp.exp(el - e_top)
    e_p = e_e / jnp.sum(e_e, axis=0, keepdims=True)
    p1 = jnp.max(e_p, axis=0, keepdims=True)
    rest = jnp.where(eidx == i1, -1.0, e_p)
    p2, i2, _ = _first_argmax(rest, EXPERTS_PER_GROUP)
    norm = p1 + p2
    w1 = g_p * (p1 / norm)
    w2 = g_p * (p2 / norm)
    eid_ref[...] = jnp.concatenate([g_idx * EXPERTS_PER_GROUP + i1, g_idx * EXPERTS_PER_GROUP + i2], axis=0)
    rows = lax.broadcasted_iota(I32, (LANES, lt.shape[1]), 0)
    wrows = jnp.where(rows == 0, w1, jnp.where(rows == 1, w2, 0.0))
    wcol_ref[...] = wrows.T


def _merge(oa, ob, gbr, x, wb0, wb1, wo, gain, wr, br):
    t, d = x.shape
    tm = ROW_TILE
    row = lambda w: pl.BlockSpec((tm, w), lambda i: (i, 0))
    full = lambda a: pl.BlockSpec(a.shape, lambda i: (0,) * a.ndim)
    return pl.pallas_call(
        _merge_kernel,
        grid=(t // tm,),
        in_specs=[row(oa.shape[1]), row(ob.shape[1]), row(gbr.shape[1]), row(d),
                  full(wb0), full(wb1), full(wo), full(gain), full(wr), full(br)],
        out_specs=[row(d), row(d), pl.BlockSpec((TOP_K, tm), lambda i: (0, i)), row(LANES)],
        out_shape=[jax.ShapeDtypeStruct((t, d), F32), jax.ShapeDtypeStruct((t, d), F32),
                   jax.ShapeDtypeStruct((TOP_K, t), I32), jax.ShapeDtypeStruct((t, LANES), F32)],
        compiler_params=_cparams(("arbitrary",)),
        name="merge_router",
    )(oa, ob, gbr, x, wb0, wb1, wo, gain, wr, br)


def _positions_kernel(eid_ref, dest_ref, counts_ref, rank_ref):
    nblk, width = eid_ref.shape
    ji = lax.broadcasted_iota(I32, (width, width), 0)
    si = lax.broadcasted_iota(I32, (width, width), 1)
    prefix = jnp.where(ji <= si, 1.0, 0.0).astype(BF16)
    expert = lax.broadcasted_iota(I32, (N_EXPERTS, width), 0)

    def onehot(i):
        return expert == eid_ref[pl.ds(i, 1), :]

    def rank_body(i, run):
        oh = onehot(i)
        cum = _dot(jnp.where(oh, 1.0, 0.0).astype(BF16), prefix) + run
        rank_ref[pl.ds(i, 1), :] = jnp.sum(jnp.where(oh, cum, 0.0), axis=0, keepdims=True) - 1.0
        return cum[:, width - 1:width]

    counts = lax.fori_loop(0, nblk, rank_body, jnp.zeros((N_EXPERTS, 1), F32))
    counts_ref[...] = jnp.broadcast_to(counts, counts_ref.shape).astype(I32)
    c_hi = jnp.floor(counts * (1.0 / 256.0))
    c_lo = counts - 256.0 * c_hi
    ei = lax.broadcasted_iota(I32, (N_EXPERTS, N_EXPERTS), 0)
    ej = lax.broadcasted_iota(I32, (N_EXPERTS, N_EXPERTS), 1)
    strict = jnp.where(ej < ei, 1.0, 0.0).astype(BF16)
    digits = jnp.concatenate([jnp.broadcast_to(c_hi, (N_EXPERTS, LANES)),
                              jnp.broadcast_to(c_lo, (N_EXPERTS, LANES))], axis=1).astype(BF16)
    sums = _dot(strict, digits)
    start = 256.0 * sums[:, 0:1] + sums[:, LANES:LANES + 1]

    def dest_body(i, carry):
        off = jnp.sum(jnp.where(onehot(i), start, 0.0), axis=0, keepdims=True)
        dest_ref[pl.ds(i, 1), :] = (rank_ref[pl.ds(i, 1), :] + off).astype(I32)
        return carry

    lax.fori_loop(0, nblk, dest_body, 0)


def _positions(eid_blocks):
    nblk, width = eid_blocks.shape
    vm = lambda shape: pl.BlockSpec(shape, lambda: (0,) * len(shape))
    return pl.pallas_call(
        _positions_kernel,
        in_specs=[vm((nblk, width))],
        out_specs=[vm((nblk, width)), vm((N_EXPERTS, LANES))],
        out_shape=[jax.ShapeDtypeStruct((nblk, width), I32), jax.ShapeDtypeStruct((N_EXPERTS, LANES), I32)],
        scratch_shapes=[pltpu.VMEM((nblk, width), F32)],
        name="positions",
    )(eid_blocks)


def _dispatch_kernel(n_prompt_tiles, dest_ref, hp_ref, hs_ref, xs_ref, sem):
    i = pl.program_id(0)
    tm = dest_ref.shape[1]

    def scatter(src_ref):
        def start(r, c):
            for k in range(TOP_K):
                pltpu.make_async_copy(src_ref.at[pl.ds(r, 1)], xs_ref.at[pl.ds(dest_ref[k, r], 1)], sem).start()
            return c

        lax.fori_loop(0, tm, start, 0, unroll=DMA_UNROLL)
        for k in range(TOP_K):
            pltpu.make_async_copy(src_ref, xs_ref.at[pl.ds(0, tm)], sem).wait()

    @pl.when(i < n_prompt_tiles)
    def _():
        scatter(hp_ref)

    @pl.when(i >= n_prompt_tiles)
    def _():
        scatter(hs_ref)


def _dispatch(dest, h_prompt, h_sample):
    t = dest.shape[1]
    d = h_prompt.shape[1]
    tm = ROW_TILE
    npt = h_prompt.shape[0] // tm
    return pl.pallas_call(
        functools.partial(_dispatch_kernel, npt),
        grid=(t // tm,),
        in_specs=[pl.BlockSpec((TOP_K, tm), lambda i: (0, i), memory_space=pltpu.SMEM),
                  pl.BlockSpec((tm, d), lambda i: (jnp.minimum(i, npt - 1), 0)),
                  pl.BlockSpec((tm, d), lambda i: (jnp.maximum(i - npt, 0), 0))],
        out_specs=pl.BlockSpec(memory_space=pl.ANY),
        out_shape=jax.ShapeDtypeStruct((TOP_K * t, d), F32),
        scratch_shapes=[pltpu.SemaphoreType.DMA(())],
        compiler_params=_cparams(("arbitrary",)),
        name="dispatch",
    )(dest, h_prompt, h_sample)


def _experts_kernel(vblk_ref, vexp_ref, vlo_ref, vhi_ref, xs_ref, wg_ref, wu_ref, wd_ref, ys_ref,
                    wg16_ref, wu16_ref, wd16_ref):
    v = pl.program_id(0)
    lo = vlo_ref[v]
    hi = vhi_ref[v]
    prev = jnp.maximum(v - 1, 0)
    first = jnp.logical_or(v == 0, vblk_ref[v] != vblk_ref[prev])
    new_expert = jnp.logical_or(v == 0, vexp_ref[v] != vexp_ref[prev])

    @pl.when(new_expert)
    def _():
        wg16_ref[...] = wg_ref[0].astype(BF16)
        wu16_ref[...] = wu_ref[0].astype(BF16)
        wd16_ref[...] = wd_ref[0].astype(BF16)

    @pl.when(hi > lo)
    def _():
        x = xs_ref[...].astype(BF16)
        gate = _dot(x, wg16_ref[...])
        up = _dot(x, wu16_ref[...])
        hid = (gate * _sigmoid(gate) * up).astype(BF16)
        y = _dot(hid, wd16_ref[...])
        rows = lax.broadcasted_iota(I32, y.shape, 0)
        mine = (rows >= lo) & (rows < hi)

        @pl.when(first)
        def _():
            ys_ref[...] = jnp.where(mine, y, 0.0)

        @pl.when(jnp.logical_not(first))
        def _():
            ys_ref[...] = jnp.where(mine, y, ys_ref[...])


def _experts(vblk, vexp, vlo, vhi, xs, wg, wu, wd):
    a, d = xs.shape
    de = wg.shape[2]
    tm = MOE_TILE
    grid_spec = pltpu.PrefetchScalarGridSpec(
        num_scalar_prefetch=4,
        grid=(vblk.shape[0],),
        in_specs=[pl.BlockSpec((tm, d), lambda v, b, e, lo, hi: (b[v], 0)),
                  pl.BlockSpec((1, d, de), lambda v, b, e, lo, hi: (e[v], 0, 0)),
                  pl.BlockSpec((1, d, de), lambda v, b, e, lo, hi: (e[v], 0, 0)),
                  pl.BlockSpec((1, de, d), lambda v, b, e, lo, hi: (e[v], 0, 0))],
        out_specs=pl.BlockSpec((tm, d), lambda v, b, e, lo, hi: (b[v], 0)),
        scratch_shapes=[pltpu.VMEM((d, de), BF16), pltpu.VMEM((d, de), BF16), pltpu.VMEM((de, d), BF16)],
    )
    return pl.pallas_call(
        _experts_kernel,
        grid_spec=grid_spec,
        out_shape=jax.ShapeDtypeStruct((a, d), F32),
        compiler_params=_cparams(("arbitrary",)),
        name="experts",
    )(vblk, vexp, vlo, vhi, xs, wg, wu, wd)


def _visit_plan(counts, n_rows):
    tm = MOE_TILE
    nblk = n_rows // tm
    n_visits = nblk + N_EXPERTS - 1
    ends = jnp.cumsum(counts)
    starts = ends - counts
    first_blk = starts // tm
    nvis = jnp.where(counts > 0, (ends + tm - 1) // tm - first_blk, 0)
    vis_end = jnp.cumsum(nvis)
    vis_start = vis_end - nvis
    v = jnp.arange(n_visits, dtype=I32)
    e = jnp.minimum(jnp.sum((vis_end[None, :] <= v[:, None]).astype(I32), axis=1), N_EXPERTS - 1)
    valid = v < vis_end[-1]
    blk = first_blk[e] + (v - vis_start[e])
    lo = jnp.clip(starts[e] - blk * tm, 0, tm)
    hi = jnp.clip(ends[e] - blk * tm, 0, tm)
    last_e = jnp.max(jnp.where(counts > 0, jnp.arange(N_EXPERTS, dtype=I32), 0))
    blk = jnp.where(valid, blk, nblk - 1).astype(I32)
    e = jnp.where(valid, e, last_e).astype(I32)
    lo = jnp.where(valid, lo, 0).astype(I32)
    hi = jnp.where(valid, hi, 0).astype(I32)
    return blk, e, lo, hi


def _combine_kernel(dest_ref, ys_ref, x1_ref, wcol_ref, gain_ref, out_ref, buf_ref, sem):
    tm = x1_ref.shape[0]

    def start(r, c):
        for k in range(TOP_K):
            pltpu.make_async_copy(ys_ref.at[pl.ds(dest_ref[k, r], 1)], buf_ref.at[k, pl.ds(r, 1)], sem).start()
        return c

    lax.fori_loop(0, tm, start, 0, unroll=DMA_UNROLL)
    for k in range(TOP_K):
        pltpu.make_async_copy(ys_ref.at[pl.ds(0, tm)], buf_ref.at[k], sem).wait()
    y = wcol_ref[:, 0:1] * buf_ref[0] + wcol_ref[:, 1:2] * buf_ref[1]
    out_ref[...] = _rms_norm(x1_ref[...] + y, gain_ref[...])


def _combine(dest, ys, x1, wcol, gain):
    t, d = x1.shape
    tm = ROW_TILE
    row = lambda w: pl.BlockSpec((tm, w), lambda i: (i, 0))
    return pl.pallas_call(
        _combine_kernel,
        grid=(t // tm,),
        in_specs=[pl.BlockSpec((TOP_K, tm), lambda i: (0, i), memory_space=pltpu.SMEM),
                  pl.BlockSpec(memory_space=pl.ANY), row(d), row(LANES),
                  pl.BlockSpec(gain.shape, lambda i: (0, 0))],
        out_specs=row(d),
        out_shape=jax.ShapeDtypeStruct((t, d), F32),
        scratch_shapes=[pltpu.VMEM((TOP_K, tm, d), F32), pltpu.SemaphoreType.DMA(())],
        compiler_params=_cparams(("arbitrary",)),
        name="combine",
    )(dest, ys, x1, wcol, gain)


def _pad_heads(w):
    r = w.shape[0]
    return jnp.pad(w.reshape(r, H_A, DK_A), ((0, 0), (0, 0), (0, LANES - DK_A))).reshape(r, H_A * LANES)


def _prepare_weights(w_in, w_gla_gate_up, b_gla_gate, w_branch, w_out, w_router_group, b_router_group,
                     w_router_expert, b_router_expert):
    d = w_in.shape[0]
    qk = H_A * DK_A
    mw = H_A * DV_A
    c = 0
    w_qa, c = w_in[:, c:c + qk], c + qk
    w_ka, c = w_in[:, c:c + qk], c + qk
    w_va, c = w_in[:, c:c + mw], c + mw
    w_ra, c = w_in[:, c:c + mw], c + mw
    w_lr, c = w_in[:, c:c + GATE_RANK], c + GATE_RANK
    w_b, c = w_in[:, c:c + 3 * mw], c + 3 * mw
    w_g = w_in[:, c:]
    wa = jnp.concatenate([_pad_heads(w_qa), _pad_heads(w_ka), w_va, w_ra,
                          jnp.pad(w_lr, ((0, 0), (0, LANES - GATE_RANK)))], axis=1).astype(BF16)
    wgu = jnp.pad(_pad_heads(w_gla_gate_up), ((0, LANES - GATE_RANK), (0, 0))).astype(BF16)
    bgu = _pad_heads(b_gla_gate[None, :])
    wr = jnp.zeros((LANES, d), F32)
    wr = wr.at[0:N_GROUPS].set(w_router_group.T).at[8:8 + N_EXPERTS].set(w_router_expert.T)
    br = jnp.zeros((LANES,), F32).at[0:N_GROUPS].set(b_router_group).at[8:8 + N_EXPERTS].set(b_router_expert)
    br = jnp.broadcast_to(br[:, None], (LANES, LANES))
    return dict(wa=wa, wqb=w_b[:, 0:mw].astype(BF16), wkvt=w_b[:, mw:3 * mw].T.astype(BF16),
                wg=w_g.astype(BF16), wgu=wgu, bgu=bgu,
                wb0=w_branch[0].astype(BF16), wb1=w_branch[1].astype(BF16), wo=w_out.astype(BF16),
                wr=wr, br=br)


def _mixers(x, s0, k_past, v_past, w, norm_mix_gain, gla_norm_gain, norm_ffn_gain):
    b, s, d = x.shape
    xf = x.reshape(b * s, d)
    qa, ka, va, ra, la, qb, kt, vt, kt16, vt16, gbr = _in_projection(
        xf, s, norm_mix_gain[None, :], w["wa"], w["wqb"], w["wkvt"], w["wg"], w["wgu"], w["bgu"])
    seq = lambda a: a.reshape(b, s, a.shape[-1])
    oa, s_new = _gla(seq(qa), seq(ka), seq(va), seq(ra), seq(la), s0, gla_norm_gain[None, :],
                     min(s, ROW_TILE))
    to_channel_major = lambda a: jnp.transpose(a, (0, 2, 3, 1))
    if k_past is None:
        ob = _sb_prompt(seq(qb), kt16, vt16)
    else:
        ob = _sb_sample(seq(qb), kt16, vt16, to_channel_major(k_past), to_channel_major(v_past))
    x1, h2, eid, wcol = _merge(oa.reshape(b * s, -1), ob.reshape(b * s, -1), gbr, xf, w["wb0"], w["wb1"],
                               w["wo"], norm_ffn_gain[None, :], w["wr"], w["br"])
    from_channel_major = lambda a: jnp.transpose(a.reshape(b, H_B, DH_B, s), (0, 3, 1, 2))
    return x1, h2, eid, wcol, s_new, from_channel_major(kt), from_channel_major(vt)


def kernel(x_prompt, x_sample, state_gla, cache_sb_k, cache_sb_v, norm_mix_gain, w_in, w_gla_gate_up, b_gla_gate, gla_norm_gain, w_branch, w_out, norm_ffn_gain, w_router_group, b_router_group, w_router_expert, b_router_expert, w_exp_gate, w_exp_up, w_exp_down, norm_final_gain):
    depth = w_in.shape[0]
    assert depth == 1, "one trunk layer per step"
    l = 0
    w = _prepare_weights(w_in[l], w_gla_gate_up[l], b_gla_gate[l], w_branch[l], w_out[l], w_router_group[l],
                         b_router_group[l], w_router_expert[l], b_router_expert[l])
    bp, sp, d = x_prompt.shape
    bs, ss, _ = x_sample.shape
    s0 = jnp.zeros((bp, H_A, DK_A, DV_A), x_prompt.dtype)
    x1p, h2p, eidp, wcolp, gla_p, k_p, v_p = _mixers(
        x_prompt, s0, None, None, w, norm_mix_gain[l], gla_norm_gain[l], norm_ffn_gain[l])
    x1s, h2s, eids, wcols, gla_s, k_s, v_s = _mixers(
        x_sample, state_gla[l], cache_sb_k[l], cache_sb_v[l], w, norm_mix_gain[l], gla_norm_gain[l],
        norm_ffn_gain[l])

    tp, ts = bp * sp, bs * ss
    eid = jnp.concatenate([eidp, eids], axis=1)
    dest_blocks, counts = _positions(eid.reshape(-1, SORT_WIDTH))
    dest = dest_blocks.reshape(TOP_K, tp + ts)
    xs = _dispatch(dest, h2p, h2s)
    vblk, vexp, vlo, vhi = _visit_plan(counts[:, 0], TOP_K * (tp + ts))
    ys = _experts(vblk, vexp, vlo, vhi, xs, w_exp_gate[l], w_exp_up[l], w_exp_down[l])
    gf = norm_final_gain[None, :]
    y_prompt = _combine(dest[:, :tp], ys, x1p, wcolp, gf).reshape(bp, sp, d)
    y_sample = _combine(dest[:, tp:], ys, x1s, wcols, gf).reshape(bs, ss, d)
    return (y_prompt, y_sample, gla_p[None], k_p[None], v_p[None], gla_s[None], k_s[None], v_s[None])
```

```python
import functools

import jax
import jax.numpy as jnp
from jax import lax
from jax.experimental import pallas as pl
from jax.experimental.pallas import tpu as pltpu

F32 = jnp.float32
BF16 = jnp.bfloat16
I32 = jnp.int32

LANES = 128
LOG2_E = 1.4426950408889634
RMS_EPS = 1e-6
GATE_TAU = 16.0
H_A = 4
DK_A = 64
DV_A = 128
GATE_RANK = 16
H_B = 8
DH_B = 64
N_GROUPS = 4
EXPERTS_PER_GROUP = 8
N_EXPERTS = N_GROUPS * EXPERTS_PER_GROUP
TOP_K = 2
GLA_CHUNK = 64
GLA_SUB = 16
GLA_EXP_CLAMP = 80.0
ROW_TILE = 256
SB_TILE = 256
MOE_TILE = 512
SORT_WIDTH = 256
DMA_UNROLL = 8
VMEM_LIMIT = 56 * 1024 * 1024


def _cparams(sem):
    return pltpu.CompilerParams(dimension_semantics=sem, vmem_limit_bytes=VMEM_LIMIT)


def _dot(a, b):
    return jnp.dot(a, b, preferred_element_type=F32)


def _dot_nt(a, b):
    return lax.dot_general(a, b, (((1,), (1,)), ((), ())), preferred_element_type=F32)


def _dot_tn(a, b):
    return lax.dot_general(a, b, (((0,), (0,)), ((), ())), preferred_element_type=F32)


def _split_bf16(x):
    hi = x.astype(BF16)
    lo = (x - hi.astype(F32)).astype(BF16)
    return hi, lo


def _log_sigmoid(x):
    return jnp.minimum(x, 0.0) - jnp.log(1.0 + jnp.exp(-jnp.abs(x)))


def _sigmoid(x):
    return 1.0 / (1.0 + jnp.exp(-x))


def _rms_norm(x, gain):
    return x * lax.rsqrt(jnp.mean(x * x, axis=-1, keepdims=True) + RMS_EPS) * gain


def _inproj_kernel(x_ref, gain_ref, wa_ref, wqb_ref, wkvt_ref, wg_ref, wgu_ref, bgu_ref,
                   qa_ref, ka_ref, va_ref, ra_ref, la_ref, qb_ref, kt_ref, vt_ref,
                   kt16_ref, vt16_ref, gbr_ref):
    h = _rms_norm(x_ref[...], gain_ref[...]).astype(BF16)
    pa = H_A * LANES
    mw = va_ref.shape[-1]
    kvt = _dot_nt(wkvt_ref[...], h)
    nseq, _, s = kt_ref.shape
    for i in range(nseq):
        cols = slice(i * s, (i + 1) * s)
        kt_ref[i] = kvt[0:mw, cols]
        vt_ref[i] = kvt[mw:2 * mw, cols]
        kt16_ref[i, 0] = kvt[0:mw, cols].astype(BF16)
        vt16_ref[i, 0] = kvt[mw:2 * mw, cols].astype(BF16)
    qb_ref[...] = _dot(h, wqb_ref[...]).astype(BF16)
    qa_ref[...] = _dot(h, wa_ref[:, 0:pa])
    ka_ref[...] = _dot(h, wa_ref[:, pa:2 * pa])
    va_ref[...] = _dot(h, wa_ref[:, 2 * pa:2 * pa + mw])
    ra_ref[...] = _dot(h, wa_ref[:, 2 * pa + mw:2 * pa + 2 * mw])
    lr = _dot(h, wa_ref[:, 2 * pa + 2 * mw:2 * pa + 2 * mw + LANES])
    gl = _dot(lr.astype(BF16), wgu_ref[...]) + bgu_ref[...]
    la_ref[...] = _log_sigmoid(gl) / GATE_TAU
    gbr_ref[...] = _dot(h, wg_ref[...])


def _in_projection(x, seq_len, gain, wa, wqb, wkvt, wg, wgu, bgu):
    t, d = x.shape
    nb = t // seq_len
    pa = H_A * LANES
    mw = wqb.shape[1]
    tm = ROW_TILE
    row = lambda w: pl.BlockSpec((tm, w), lambda i: (i, 0))
    full = lambda a: pl.BlockSpec(a.shape, lambda i: (0,) * a.ndim)
    if seq_len >= tm:
        per_seq = seq_len // tm
        assert tm == SB_TILE and seq_len % tm == 0
        kt_spec = pl.BlockSpec((1, mw, tm), lambda i: (i // per_seq, 0, i % per_seq))
        kt16_spec = pl.BlockSpec((1, 1, mw, tm), lambda i: (i // per_seq, i % per_seq, 0, 0))
        kt16_shape = (nb, per_seq, mw, tm)
    else:
        nseq = tm // seq_len
        assert tm % seq_len == 0
        kt_spec = pl.BlockSpec((nseq, mw, seq_len), lambda i: (i, 0, 0))
        kt16_spec = pl.BlockSpec((nseq, 1, mw, seq_len), lambda i: (i, 0, 0, 0))
        kt16_shape = (nb, 1, mw, seq_len)
    outs = [
        (jax.ShapeDtypeStruct((t, pa), F32), row(pa)), (jax.ShapeDtypeStruct((t, pa), F32), row(pa)),
        (jax.ShapeDtypeStruct((t, mw), F32), row(mw)), (jax.ShapeDtypeStruct((t, mw), F32), row(mw)),
        (jax.ShapeDtypeStruct((t, pa), F32), row(pa)),
        (jax.ShapeDtypeStruct((t, mw), BF16), row(mw)),
        (jax.ShapeDtypeStruct((nb, mw, seq_len), F32), kt_spec), (jax.ShapeDtypeStruct((nb, mw, seq_len), F32), kt_spec),
        (jax.ShapeDtypeStruct(kt16_shape, BF16), kt16_spec), (jax.ShapeDtypeStruct(kt16_shape, BF16), kt16_spec),
        (jax.ShapeDtypeStruct((t, wg.shape[1]), F32), row(wg.shape[1])),
    ]
    return pl.pallas_call(
        _inproj_kernel,
        grid=(t // tm,),
        in_specs=[row(d), full(gain), full(wa), full(wqb), full(wkvt), full(wg), full(wgu), full(bgu)],
        out_specs=[spec for _, spec in outs],
        out_shape=[shape for shape, _ in outs],
        compiler_params=_cparams(("arbitrary",)),
        name="in_projection",
    )(x, gain, wa, wqb, wkvt, wg, wgu, bgu)


def _gla_chunk(q, k, v, b, st):
    c = q.shape[0]
    b_last = b[c - 1:c, :]
    rows = lax.broadcasted_iota(I32, (c, LANES), 0)
    nsub = c // GLA_SUB
    refs = [jnp.zeros((1, LANES), F32)] + [b[i * GLA_SUB - 1:i * GLA_SUB, :] for i in range(1, nsub)]
    ref_rows = refs[0]
    for i in range(1, nsub):
        ref_rows = jnp.where(rows >= i * GLA_SUB, refs[i], ref_rows)
    q_rel = q * jnp.exp(b - ref_rows)
    lhs = jnp.concatenate(
        [jnp.where((rows >= i * GLA_SUB) & (rows < (i + 1) * GLA_SUB), q_rel, 0.0) for i in range(nsub)],
        axis=1).astype(BF16)
    rhs = jnp.concatenate(
        [jnp.where(rows < (i + 1) * GLA_SUB, k * jnp.exp(jnp.minimum(refs[i] - b, GLA_EXP_CLAMP)), 0.0)
         for i in range(nsub)], axis=1).astype(BF16)
    att = _dot_nt(lhs, rhs)
    tt = lax.broadcasted_iota(I32, (c, c), 0)
    ss = lax.broadcasted_iota(I32, (c, c), 1)
    att = jnp.where(ss <= tt, att, 0.0)
    v16 = v.astype(BF16)
    inter = _dot_nt((q * jnp.exp(b)).astype(BF16), st.astype(BF16))
    intra = _dot(att.astype(BF16), v16)
    kd = (k * jnp.exp(b_last - b)).astype(BF16)
    st_new = st * jnp.exp(b_last) + _dot_tn(v16, kd)
    return inter + intra, st_new


def _gla_kernel(qa_ref, ka_ref, va_ref, ra_ref, la_ref, s0_ref, gain_ref, o_ref, sfin_ref, st_ref):
    j = pl.program_id(1)
    nj = pl.num_programs(1)
    rows_per_step = qa_ref.shape[1]
    c = GLA_CHUNK
    zpad = jnp.zeros((LANES - DK_A, DV_A), F32)

    @pl.when(j == 0)
    def _():
        for h in range(H_A):
            st_ref[h] = jnp.concatenate([s0_ref[0, h], zpad], axis=0).T

    ti = lax.broadcasted_iota(I32, (rows_per_step, rows_per_step), 0)
    si = lax.broadcasted_iota(I32, (rows_per_step, rows_per_step), 1)
    chunk_shift = c.bit_length() - 1
    same_chunk = (ti >> chunk_shift) == (si >> chunk_shift)
    tril_blocks = jnp.where(same_chunk & (si <= ti), 1.0, 0.0).astype(BF16)
    la_hi, la_lo = _split_bf16(la_ref[0])
    b_all = _dot(tril_blocks, la_hi) + _dot(tril_blocks, la_lo)
    gain = gain_ref[...]
    for h in range(H_A):
        hp = slice(h * LANES, (h + 1) * LANES)
        hv = slice(h * DV_A, (h + 1) * DV_A)
        st = st_ref[h]
        for ci in range(rows_per_step // c):
            r0 = ci * c
            q = qa_ref[0, r0:r0 + c, hp] * (DK_A ** -0.5)
            o, st = _gla_chunk(q, ka_ref[0, r0:r0 + c, hp], va_ref[0, r0:r0 + c, hv], b_all[r0:r0 + c, hp], st)
            r = ra_ref[0, r0:r0 + c, hv]
            o = _rms_norm(o, gain) * (r * _sigmoid(r))
            o_ref[0, r0:r0 + c, hv] = o.astype(o_ref.dtype)
        st_ref[h] = st

    @pl.when(j == nj - 1)
    def _():
        for h in range(H_A):
            sfin_ref[0, h] = st_ref[h].T[0:DK_A, :]


def _gla(qa, ka, va, ra, la, s0, gain, rows_per_step):
    b, s, pa = qa.shape
    mw = va.shape[-1]
    seq = lambda w: pl.BlockSpec((1, rows_per_step, w), lambda i, j: (i, j, 0))
    state = pl.BlockSpec((1, H_A, DK_A, DV_A), lambda i, j: (i, 0, 0, 0))
    return pl.pallas_call(
        _gla_kernel,
        grid=(b, s // rows_per_step),
        in_specs=[seq(pa), seq(pa), seq(mw), seq(mw), seq(pa), state,
                  pl.BlockSpec(gain.shape, lambda i, j: (0, 0))],
        out_specs=[seq(mw), state],
        out_shape=[jax.ShapeDtypeStruct((b, s, mw), BF16),
                   jax.ShapeDtypeStruct((b, H_A, DK_A, DV_A), F32)],
        scratch_shapes=[pltpu.VMEM((H_A, LANES, LANES), F32)],
        compiler_params=_cparams(("arbitrary", "arbitrary")),
        name="gla",
    )(qa, ka, va, ra, la, s0, gain)


def _head_lane_masks():
    lane = lax.broadcasted_iota(I32, (1, LANES), 1)
    return lane < DH_B, lane >= DH_B


def _sb_neg_tri(tk):
    ji = lax.broadcasted_iota(I32, (tk, tk), 0)
    si = lax.broadcasted_iota(I32, (tk, tk), 1)
    return jnp.where(ji >= si, -1.0, 0.0).astype(BF16)


def _sb_stack_queries(q, qs_ref):
    m0, m1 = _head_lane_masks()
    for p in range(qs_ref.shape[0]):
        qp = q[:, p * LANES:(p + 1) * LANES] * (DH_B ** -0.5)
        zero = jnp.zeros_like(qp)
        qs_ref[p] = jnp.concatenate([jnp.where(m0, qp, zero), jnp.where(m1, qp, zero)], axis=0)


def _pair_lanes(p):
    return slice(p * LANES, (p + 1) * LANES)


def _lane_fit(x, width):
    if width >= LANES:
        return jnp.concatenate([x] * (width // LANES), axis=1)
    return x[:, 0:width]


def _sb_tile_step(qs_ref, acc_ref, carry_ref, k_tile, v_tile, ntri, diagonal):
    npair, rows, _ = qs_ref.shape
    tq = rows // 2
    tk = ntri.shape[1]
    m0, _ = _head_lane_masks()
    if diagonal:
        t = lax.broadcasted_iota(I32, (rows, tk), 0)
        t = jnp.where(t >= tq, t - tq, t)
        visible = lax.broadcasted_iota(I32, (rows, tk), 1) < t
    for p in range(npair):
        z = _dot(qs_ref[p], k_tile(p)) * LOG2_E
        sp = jnp.maximum(z, 0.0) + jnp.log2(1.0 + jnp.exp2(-jnp.abs(z)))
        if diagonal:
            sp = jnp.where(visible, sp, 0.0)
        suffix = _dot(sp.astype(BF16), ntri)
        carry = carry_ref[p]
        w = jnp.exp2(z + suffix + _lane_fit(carry, tk))
        if diagonal:
            w = jnp.where(visible, w, 0.0)
        pv = _dot_nt(w.astype(BF16), v_tile(p))
        acc_ref[p] += jnp.where(m0, pv[0:tq], pv[tq:rows])
        carry_ref[p] = carry + jnp.broadcast_to(suffix[:, 0:1], carry.shape)


def _sb_prompt_kernel(q_ref, k_ref, v_ref, o_ref, qs_ref, acc_ref, carry_ref):
    qi = pl.program_id(1)
    tk = SB_TILE
    _sb_stack_queries(q_ref[0], qs_ref)
    acc_ref[...] = jnp.zeros_like(acc_ref)
    carry_ref[...] = jnp.zeros_like(carry_ref)
    ntri = _sb_neg_tri(tk)

    def step(jb, diagonal):
        _sb_tile_step(qs_ref, acc_ref, carry_ref, lambda p: k_ref[0, jb, _pair_lanes(p), :],
                      lambda p: v_ref[0, jb, _pair_lanes(p), :], ntri, diagonal)

    step(qi, True)

    def body(i, c):
        step(qi - 1 - i, False)
        return c

    lax.fori_loop(0, qi, body, 0)
    for p in range(acc_ref.shape[0]):
        o_ref[0, :, p * LANES:(p + 1) * LANES] = acc_ref[p].astype(o_ref.dtype)


def _sb_scratch(tq, npair):
    return [pltpu.VMEM((npair, 2 * tq, LANES), BF16), pltpu.VMEM((npair, tq, LANES), F32),
            pltpu.VMEM((npair, 2 * tq, LANES), F32)]


def _sb_prompt(q, kt, vt):
    b, s, w = q.shape
    tq = SB_TILE
    assert kt.shape == (b, s // tq, w, tq)
    qspec = pl.BlockSpec((1, tq, w), lambda i, j: (i, j, 0))
    kvspec = pl.BlockSpec((1,) + kt.shape[1:], lambda i, j: (i, 0, 0, 0))
    return pl.pallas_call(
        _sb_prompt_kernel,
        grid=(b, s // tq),
        in_specs=[qspec, kvspec, kvspec],
        out_specs=qspec,
        out_shape=jax.ShapeDtypeStruct((b, s, w), BF16),
        scratch_shapes=_sb_scratch(tq, w // LANES),
        compiler_params=_cparams(("arbitrary", "arbitrary")),
        name="sb_prompt",
    )(q, kt, vt)


def _sb_sample_kernel(q_ref, kn_ref, vn_ref, kp_ref, vp_ref, o_ref, qs_ref, acc_ref, carry_ref):
    past = kp_ref.shape[3]
    sq = q_ref.shape[1]
    tk = SB_TILE
    _sb_stack_queries(q_ref[0], qs_ref)
    acc_ref[...] = jnp.zeros_like(acc_ref)
    carry_ref[...] = jnp.zeros_like(carry_ref)
    _sb_tile_step(qs_ref, acc_ref, carry_ref, lambda p: kn_ref[0, 0, _pair_lanes(p), :],
                  lambda p: vn_ref[0, 0, _pair_lanes(p), :], _sb_neg_tri(sq), True)
    ntri = _sb_neg_tri(tk)

    def body(i, c):
        cols = pl.ds(pl.multiple_of(past - (i + 1) * tk, tk), tk)

        def pair(ref, p):
            return ref[0, 2 * p:2 * p + 2, :, cols].reshape(LANES, tk).astype(BF16)

        _sb_tile_step(qs_ref, acc_ref, carry_ref, lambda p: pair(kp_ref, p), lambda p: pair(vp_ref, p), ntri, False)
        return c

    lax.fori_loop(0, past // tk, body, 0)
    for p in range(acc_ref.shape[0]):
        o_ref[0, :, p * LANES:(p + 1) * LANES] = acc_ref[p].astype(o_ref.dtype)


def _sb_sample(q, kt_new, vt_new, kt_past, vt_past):
    b, sq, w = q.shape
    past = kt_past.shape[3]
    assert past % SB_TILE == 0 and 2 * DH_B == LANES
    qspec = pl.BlockSpec((1, sq, w), lambda i: (i, 0, 0))
    new = pl.BlockSpec((1, 1, w, sq), lambda i: (i, 0, 0, 0))
    old = pl.BlockSpec((1, H_B, DH_B, past), lambda i: (i, 0, 0, 0))
    return pl.pallas_call(
        _sb_sample_kernel,
        grid=(b,),
        in_specs=[qspec, new, new, old, old],
        out_specs=qspec,
        out_shape=jax.ShapeDtypeStruct((b, sq, w), BF16),
        scratch_shapes=_sb_scratch(sq, w // LANES),
        compiler_params=_cparams(("arbitrary",)),
        name="sb_sample",
    )(q, kt_new, vt_new, kt_past, vt_past)


def _first_argmax(vals, nrows):
    idx = lax.broadcasted_iota(I32, vals.shape, 0)
    top = jnp.max(vals, axis=0, keepdims=True)
    first = jnp.min(jnp.where(vals == top, idx, nrows), axis=0, keepdims=True)
    return top, first, idx


def _merge_kernel(oa_ref, ob_ref, g_ref, x_ref, wb0_ref, wb1_ref, wo_ref, gain_ref, wr_ref, br_ref,
                  x1_ref, h2_ref, eid_ref, wcol_ref):
    d = x_ref.shape[1]
    ya = _dot(oa_ref[...], wb0_ref[...])
    yb = _dot(ob_ref[...], wb1_ref[...])
    m = _sigmoid(g_ref[:, 0:d]) * ya + _sigmoid(g_ref[:, d:2 * d]) * yb
    x1 = x_ref[...] + _dot(m.astype(BF16), wo_ref[...])
    x1_ref[...] = x1
    h2 = _rms_norm(x1, gain_ref[...])
    h2_ref[...] = h2.reshape(h2_ref.shape)

    h_hi, h_lo = _split_bf16(h2)
    w_hi, w_lo = _split_bf16(wr_ref[...])
    lt = _dot_nt(w_hi, h_hi) + _dot_nt(w_hi, h_lo) + _dot_nt(w_lo, h_hi) + br_ref[:, 0:1]
    gl = lt[0:N_GROUPS, :]
    g_top, g_idx, _ = _first_argmax(gl, N_GROUPS)
    g_e = jnp.exp(gl - g_top)
    g_p = jnp.max(g_e / jnp.sum(g_e, axis=0, keepdims=True), axis=0, keepdims=True)
    el = jnp.zeros((EXPERTS_PER_GROUP, lt.shape[1]), F32)
    for g in range(N_GROUPS):
        r0 = 8 + g * EXPERTS_PER_GROUP
        el = jnp.where(g_idx == g, lt[r0:r0 + EXPERTS_PER_GROUP, :], el)
    e_top, i1, eidx = _first_argmax(el, EXPERTS_PER_GROUP)
    e_e = jnp.exp(el - e_top)
    e_p = e_e / jnp.sum(e_e, axis=0, keepdims=True)
    p1 = jnp.max(e_p, axis=0, keepdims=True)
    rest = jnp.where(eidx == i1, -1.0, e_p)
    p2, i2, _ = _first_argmax(rest, EXPERTS_PER_GROUP)
    norm = p1 + p2
    w1 = g_p * (p1 / norm)
    w2 = g_p * (p2 / norm)
    eid_ref[...] = jnp.concatenate([g_idx * EXPERTS_PER_GROUP + i1, g_idx * EXPERTS_PER_GROUP + i2], axis=0)
    rows = lax.broadcasted_iota(I32, (LANES, lt.shape[1]), 0)
    wrows = jnp.where(rows == 0, w1, jnp.where(rows == 1, w2, 0.0))
    wcol_ref[...] = wrows.T


def _merge(oa, ob, gbr, x, wb0, wb1, wo, gain, wr, br):
    t, d = x.shape
    tm = ROW_TILE
    row = lambda w: pl.BlockSpec((tm, w), lambda i: (i, 0))
    full = lambda a: pl.BlockSpec(a.shape, lambda i: (0,) * a.ndim)
    return pl.pallas_call(
        _merge_kernel,
        grid=(t // tm,),
        in_specs=[row(oa.shape[1]), row(ob.shape[1]), row(gbr.shape[1]), row(d),
                  full(wb0), full(wb1), full(wo), full(gain), full(wr), full(br)],
        out_specs=[row(d), pl.BlockSpec((tm, d // LANES, LANES), lambda i: (i, 0, 0)),
                   pl.BlockSpec((TOP_K, tm), lambda i: (0, i)), row(LANES)],
        out_shape=[jax.ShapeDtypeStruct((t, d), F32), jax.ShapeDtypeStruct((t, d // LANES, LANES), F32),
                   jax.ShapeDtypeStruct((TOP_K, t), I32), jax.ShapeDtypeStruct((t, LANES), F32)],
        compiler_params=_cparams(("arbitrary",)),
        name="merge_router",
    )(oa, ob, gbr, x, wb0, wb1, wo, gain, wr, br)


def _positions_kernel(eid_ref, dest_ref, counts_ref, rank_ref):
    nblk, width = eid_ref.shape
    ji = lax.broadcasted_iota(I32, (width, width), 0)
    si = lax.broadcasted_iota(I32, (width, width), 1)
    prefix = jnp.where(ji <= si, 1.0, 0.0).astype(BF16)
    expert = lax.broadcasted_iota(I32, (N_EXPERTS, width), 0)

    def onehot(i):
        return expert == eid_ref[pl.ds(i, 1), :]

    def rank_body(i, run):
        oh = onehot(i)
        cum = _dot(jnp.where(oh, 1.0, 0.0).astype(BF16), prefix) + run
        rank_ref[pl.ds(i, 1), :] = jnp.sum(jnp.where(oh, cum, 0.0), axis=0, keepdims=True) - 1.0
        return cum[:, width - 1:width]

    counts = lax.fori_loop(0, nblk, rank_body, jnp.zeros((N_EXPERTS, 1), F32))
    counts_ref[...] = jnp.broadcast_to(counts, counts_ref.shape).astype(I32)
    c_hi = jnp.floor(counts * (1.0 / 256.0))
    c_lo = counts - 256.0 * c_hi
    ei = lax.broadcasted_iota(I32, (N_EXPERTS, N_EXPERTS), 0)
    ej = lax.broadcasted_iota(I32, (N_EXPERTS, N_EXPERTS), 1)
    strict = jnp.where(ej < ei, 1.0, 0.0).astype(BF16)
    digits = jnp.concatenate([jnp.broadcast_to(c_hi, (N_EXPERTS, LANES)),
                              jnp.broadcast_to(c_lo, (N_EXPERTS, LANES))], axis=1).astype(BF16)
    sums = _dot(strict, digits)
    start = 256.0 * sums[:, 0:1] + sums[:, LANES:LANES + 1]

    def dest_body(i, carry):
        off = jnp.sum(jnp.where(onehot(i), start, 0.0), axis=0, keepdims=True)
        dest_ref[pl.ds(i, 1), :] = (rank_ref[pl.ds(i, 1), :] + off).astype(I32)
        return carry

    lax.fori_loop(0, nblk, dest_body, 0)


def _positions(eid_blocks):
    nblk, width = eid_blocks.shape
    vm = lambda shape: pl.BlockSpec(shape, lambda: (0,) * len(shape))
    return pl.pallas_call(
        _positions_kernel,
        in_specs=[vm((nblk, width))],
        out_specs=[vm((nblk, width)), vm((N_EXPERTS, LANES))],
        out_shape=[jax.ShapeDtypeStruct((nblk, width), I32), jax.ShapeDtypeStruct((N_EXPERTS, LANES), I32)],
        scratch_shapes=[pltpu.VMEM((nblk, width), F32)],
        name="positions",
    )(eid_blocks)


def _dispatch_kernel(n_prompt_tiles, dest_ref, hp_ref, hs_ref, xs_ref, sem):
    i = pl.program_id(0)
    tm = dest_ref.shape[1]

    def scatter(src_ref):
        def start(r, c):
            for k in range(TOP_K):
                pltpu.make_async_copy(src_ref.at[r], xs_ref.at[dest_ref[k, r]], sem).start(priority=k)
            return c

        lax.fori_loop(0, tm, start, 0, unroll=DMA_UNROLL)
        for k in range(TOP_K):
            pltpu.make_async_copy(src_ref, xs_ref.at[pl.ds(0, tm)], sem).wait()

    @pl.when(i < n_prompt_tiles)
    def _():
        scatter(hp_ref)

    @pl.when(i >= n_prompt_tiles)
    def _():
        scatter(hs_ref)


def _dispatch(dest, h_prompt, h_sample):
    t = dest.shape[1]
    slab = h_prompt.shape[1:]
    tm = ROW_TILE
    npt = h_prompt.shape[0] // tm
    return pl.pallas_call(
        functools.partial(_dispatch_kernel, npt),
        grid=(t // tm,),
        in_specs=[pl.BlockSpec((TOP_K, tm), lambda i: (0, i), memory_space=pltpu.SMEM),
                  pl.BlockSpec((tm,) + slab, lambda i: (jnp.minimum(i, npt - 1), 0, 0)),
                  pl.BlockSpec((tm,) + slab, lambda i: (jnp.maximum(i - npt, 0), 0, 0))],
        out_specs=pl.BlockSpec(memory_space=pl.ANY),
        out_shape=jax.ShapeDtypeStruct((TOP_K * t,) + slab, F32),
        scratch_shapes=[pltpu.SemaphoreType.DMA(())],
        compiler_params=_cparams(("arbitrary",)),
        name="dispatch",
    )(dest, h_prompt, h_sample)


def _experts_kernel(vblk_ref, vexp_ref, vlo_ref, vhi_ref, xs_ref, wg_ref, wu_ref, wd_ref, ys_ref,
                    wg16_ref, wu16_ref, wd16_ref):
    v = pl.program_id(0)
    lo = vlo_ref[v]
    hi = vhi_ref[v]
    prev = jnp.maximum(v - 1, 0)
    first = jnp.logical_or(v == 0, vblk_ref[v] != vblk_ref[prev])
    new_expert = jnp.logical_or(v == 0, vexp_ref[v] != vexp_ref[prev])

    @pl.when(new_expert)
    def _():
        wg16_ref[...] = wg_ref[0].astype(BF16)
        wu16_ref[...] = wu_ref[0].astype(BF16)
        wd16_ref[...] = wd_ref[0].astype(BF16)

    @pl.when(hi > lo)
    def _():
        tm = xs_ref.shape[0]
        d = wg_ref.shape[1]
        x = xs_ref[...].reshape(tm, d).astype(BF16)
        gate = _dot(x, wg16_ref[...])
        up = _dot(x, wu16_ref[...])
        hid = (gate * _sigmoid(gate) * up).astype(BF16)
        y = _dot(hid, wd16_ref[...])
        rows = lax.broadcasted_iota(I32, y.shape, 0)
        mine = (rows >= lo) & (rows < hi)

        @pl.when(first)
        def _():
            ys_ref[...] = jnp.where(mine, y, 0.0).reshape(ys_ref.shape)

        @pl.when(jnp.logical_not(first))
        def _():
            ys_ref[...] = jnp.where(mine, y, ys_ref[...].reshape(tm, d)).reshape(ys_ref.shape)


def _experts(vblk, vexp, vlo, vhi, xs, wg, wu, wd):
    a = xs.shape[0]
    slab = xs.shape[1:]
    d, de = wg.shape[1:]
    tm = MOE_TILE
    grid_spec = pltpu.PrefetchScalarGridSpec(
        num_scalar_prefetch=4,
        grid=(vblk.shape[0],),
        in_specs=[pl.BlockSpec((tm,) + slab, lambda v, b, e, lo, hi: (b[v], 0, 0)),
                  pl.BlockSpec((1, d, de), lambda v, b, e, lo, hi: (e[v], 0, 0)),
                  pl.BlockSpec((1, d, de), lambda v, b, e, lo, hi: (e[v], 0, 0)),
                  pl.BlockSpec((1, de, d), lambda v, b, e, lo, hi: (e[v], 0, 0))],
        out_specs=pl.BlockSpec((tm,) + slab, lambda v, b, e, lo, hi: (b[v], 0, 0)),
        scratch_shapes=[pltpu.VMEM((d, de), BF16), pltpu.VMEM((d, de), BF16), pltpu.VMEM((de, d), BF16)],
    )
    return pl.pallas_call(
        _experts_kernel,
        grid_spec=grid_spec,
        out_shape=jax.ShapeDtypeStruct((a,) + slab, F32),
        compiler_params=_cparams(("arbitrary",)),
        name="experts",
    )(vblk, vexp, vlo, vhi, xs, wg, wu, wd)


def _visit_plan(counts, n_rows):
    tm = MOE_TILE
    nblk = n_rows // tm
    n_visits = nblk + N_EXPERTS - 1
    ends = jnp.cumsum(counts)
    starts = ends - counts
    first_blk = starts // tm
    nvis = jnp.where(counts > 0, (ends + tm - 1) // tm - first_blk, 0)
    vis_end = jnp.cumsum(nvis)
    vis_start = vis_end - nvis
    v = jnp.arange(n_visits, dtype=I32)
    e = jnp.minimum(jnp.sum((vis_end[None, :] <= v[:, None]).astype(I32), axis=1), N_EXPERTS - 1)
    valid = v < vis_end[-1]
    blk = first_blk[e] + (v - vis_start[e])
    lo = jnp.clip(starts[e] - blk * tm, 0, tm)
    hi = jnp.clip(ends[e] - blk * tm, 0, tm)
    last_e = jnp.max(jnp.where(counts > 0, jnp.arange(N_EXPERTS, dtype=I32), 0))
    blk = jnp.where(valid, blk, nblk - 1).astype(I32)
    e = jnp.where(valid, e, last_e).astype(I32)
    lo = jnp.where(valid, lo, 0).astype(I32)
    hi = jnp.where(valid, hi, 0).astype(I32)
    return blk, e, lo, hi


def _combine_kernel(dest_ref, ys_ref, x1_ref, wcol_ref, gain_ref, out_ref, buf_ref, sem):
    tm = x1_ref.shape[0]

    def start(r, c):
        for k in range(TOP_K):
            pltpu.make_async_copy(ys_ref.at[dest_ref[k, r]], buf_ref.at[k, r], sem).start(priority=k)
        return c

    lax.fori_loop(0, tm, start, 0, unroll=DMA_UNROLL)
    for k in range(TOP_K):
        pltpu.make_async_copy(ys_ref.at[pl.ds(0, tm)], buf_ref.at[k], sem).wait()
    rows = x1_ref.shape
    y = wcol_ref[:, 0:1] * buf_ref[0].reshape(rows) + wcol_ref[:, 1:2] * buf_ref[1].reshape(rows)
    out_ref[...] = _rms_norm(x1_ref[...] + y, gain_ref[...])


def _combine(dest, ys, x1, wcol, gain):
    t, d = x1.shape
    tm = ROW_TILE
    row = lambda w: pl.BlockSpec((tm, w), lambda i: (i, 0))
    return pl.pallas_call(
        _combine_kernel,
        grid=(t // tm,),
        in_specs=[pl.BlockSpec((TOP_K, tm), lambda i: (0, i), memory_space=pltpu.SMEM),
                  pl.BlockSpec(memory_space=pl.ANY), row(d), row(LANES),
                  pl.BlockSpec(gain.shape, lambda i: (0, 0))],
        out_specs=row(d),
        out_shape=jax.ShapeDtypeStruct((t, d), F32),
        scratch_shapes=[pltpu.VMEM((TOP_K, tm) + ys.shape[1:], F32), pltpu.SemaphoreType.DMA(())],
        compiler_params=_cparams(("arbitrary",)),
        name="combine",
    )(dest, ys, x1, wcol, gain)


def _pad_heads(w):
    r = w.shape[0]
    return jnp.pad(w.reshape(r, H_A, DK_A), ((0, 0), (0, 0), (0, LANES - DK_A))).reshape(r, H_A * LANES)


def _prepare_weights(w_in, w_gla_gate_up, b_gla_gate, w_branch, w_out, w_router_group, b_router_group,
                     w_router_expert, b_router_expert):
    d = w_in.shape[0]
    qk = H_A * DK_A
    mw = H_A * DV_A
    c = 0
    w_qa, c = w_in[:, c:c + qk], c + qk
    w_ka, c = w_in[:, c:c + qk], c + qk
    w_va, c = w_in[:, c:c + mw], c + mw
    w_ra, c = w_in[:, c:c + mw], c + mw
    w_lr, c = w_in[:, c:c + GATE_RANK], c + GATE_RANK
    w_b, c = w_in[:, c:c + 3 * mw], c + 3 * mw
    w_g = w_in[:, c:]
    wa = jnp.concatenate([_pad_heads(w_qa), _pad_heads(w_ka), w_va, w_ra,
                          jnp.pad(w_lr, ((0, 0), (0, LANES - GATE_RANK)))], axis=1).astype(BF16)
    wgu = jnp.pad(_pad_heads(w_gla_gate_up), ((0, LANES - GATE_RANK), (0, 0))).astype(BF16)
    bgu = _pad_heads(b_gla_gate[None, :])
    wr = jnp.zeros((LANES, d), F32)
    wr = wr.at[0:N_GROUPS].set(w_router_group.T).at[8:8 + N_EXPERTS].set(w_router_expert.T)
    br = jnp.zeros((LANES,), F32).at[0:N_GROUPS].set(b_router_group).at[8:8 + N_EXPERTS].set(b_router_expert)
    br = jnp.broadcast_to(br[:, None], (LANES, LANES))
    return dict(wa=wa, wqb=w_b[:, 0:mw].astype(BF16), wkvt=w_b[:, mw:3 * mw].T.astype(BF16),
                wg=w_g.astype(BF16), wgu=wgu, bgu=bgu,
                wb0=w_branch[0].astype(BF16), wb1=w_branch[1].astype(BF16), wo=w_out.astype(BF16),
                wr=wr, br=br)


def _mixers(x, s0, k_past, v_past, w, norm_mix_gain, gla_norm_gain, norm_ffn_gain):
    b, s, d = x.shape
    xf = x.reshape(b * s, d)
    qa, ka, va, ra, la, qb, kt, vt, kt16, vt16, gbr = _in_projection(
        xf, s, norm_mix_gain[None, :], w["wa"], w["wqb"], w["wkvt"], w["wg"], w["wgu"], w["bgu"])
    seq = lambda a: a.reshape(b, s, a.shape[-1])
    oa, s_new = _gla(seq(qa), seq(ka), seq(va), seq(ra), seq(la), s0, gla_norm_gain[None, :],
                     min(s, ROW_TILE))
    to_channel_major = lambda a: jnp.transpose(a, (0, 2, 3, 1))
    if k_past is None:
        ob = _sb_prompt(seq(qb), kt16, vt16)
    else:
        ob = _sb_sample(seq(qb), kt16, vt16, to_channel_major(k_past), to_channel_major(v_past))
    x1, h2, eid, wcol = _merge(oa.reshape(b * s, -1), ob.reshape(b * s, -1), gbr, xf, w["wb0"], w["wb1"],
                               w["wo"], norm_ffn_gain[None, :], w["wr"], w["br"])
    from_channel_major = lambda a: jnp.transpose(a.reshape(b, H_B, DH_B, s), (0, 3, 1, 2))
    return x1, h2, eid, wcol, s_new, from_channel_major(kt), from_channel_major(vt)


def kernel(x_prompt, x_sample, state_gla, cache_sb_k, cache_sb_v, norm_mix_gain, w_in, w_gla_gate_up, b_gla_gate, gla_norm_gain, w_branch, w_out, norm_ffn_gain, w_router_group, b_router_group, w_router_expert, b_router_expert, w_exp_gate, w_exp_up, w_exp_down, norm_final_gain):
    depth = w_in.shape[0]
    assert depth == 1, "one trunk layer per step"
    l = 0
    w = _prepare_weights(w_in[l], w_gla_gate_up[l], b_gla_gate[l], w_branch[l], w_out[l], w_router_group[l],
                         b_router_group[l], w_router_expert[l], b_router_expert[l])
    bp, sp, d = x_prompt.shape
    bs, ss, _ = x_sample.shape
    s0 = jnp.zeros((bp, H_A, DK_A, DV_A), x_prompt.dtype)
    x1p, h2p, eidp, wcolp, gla_p, k_p, v_p = _mixers(
        x_prompt, s0, None, None, w, norm_mix_gain[l], gla_norm_gain[l], norm_ffn_gain[l])
    x1s, h2s, eids, wcols, gla_s, k_s, v_s = _mixers(
        x_sample, state_gla[l], cache_sb_k[l], cache_sb_v[l], w, norm_mix_gain[l], gla_norm_gain[l],
        norm_ffn_gain[l])

    tp, ts = bp * sp, bs * ss
    eid = jnp.concatenate([eidp, eids], axis=1)
    dest_blocks, counts = _positions(eid.reshape(-1, SORT_WIDTH))
    dest = dest_blocks.reshape(TOP_K, tp + ts)
    xs = _dispatch(dest, h2p, h2s)
    vblk, vexp, vlo, vhi = _visit_plan(counts[:, 0], TOP_K * (tp + ts))
    ys = _experts(vblk, vexp, vlo, vhi, xs, w_exp_gate[l], w_exp_up[l], w_exp_down[l])
    gf = norm_final_gain[None, :]
    y_prompt = _combine(dest[:, :tp], ys, x1p, wcolp, gf).reshape(bp, sp, d)
    y_sample = _combine(dest[:, tp:], ys, x1s, wcols, gf).reshape(bs, ss, d)
    return (y_prompt, y_sample, gla_p[None], k_p[None], v_p[None], gla_s[None], k_s[None], v_s[None])
```

```python
import functools

import jax
import jax.numpy as jnp
from jax import lax
from jax.experimental import pallas as pl
from jax.experimental.pallas import tpu as pltpu

F32 = jnp.float32
BF16 = jnp.bfloat16
I32 = jnp.int32

LANES = 128
LOG2_E = 1.4426950408889634
RMS_EPS = 1e-6
GATE_TAU = 16.0
H_A = 4
DK_A = 64
DV_A = 128
GATE_RANK = 16
H_B = 8
DH_B = 64
N_GROUPS = 4
EXPERTS_PER_GROUP = 8
N_EXPERTS = N_GROUPS * EXPERTS_PER_GROUP
TOP_K = 2
GLA_CHUNK = 64
GLA_SUB = 16
GLA_EXP_CLAMP = 80.0
GLA_SEQS = 2
ROW_TILE = 256
SB_TILE = 256
SB_SAMPLE_SEQS = 2
MOE_TILE = 512
SORT_WIDTH = 256
DISPATCH_TILE = 512
COMBINE_PARTS = 4
DMA_UNROLL = 8
VMEM_LIMIT = 56 * 1024 * 1024


def _cparams(sem):
    return pltpu.CompilerParams(dimension_semantics=sem, vmem_limit_bytes=VMEM_LIMIT)


def _dot(a, b):
    return jnp.dot(a, b, preferred_element_type=F32)


def _dot_nt(a, b):
    return lax.dot_general(a, b, (((1,), (1,)), ((), ())), preferred_element_type=F32)


def _dot_tn(a, b):
    return lax.dot_general(a, b, (((0,), (0,)), ((), ())), preferred_element_type=F32)


def _split_bf16(x):
    hi = x.astype(BF16)
    lo = (x - hi.astype(F32)).astype(BF16)
    return hi, lo


def _log_sigmoid(x):
    return jnp.minimum(x, 0.0) - jnp.log(1.0 + jnp.exp(-jnp.abs(x)))


def _sigmoid(x):
    return 1.0 / (1.0 + jnp.exp(-x))


def _rms_norm(x, gain):
    return x * lax.rsqrt(jnp.mean(x * x, axis=-1, keepdims=True) + RMS_EPS) * gain


def _inproj_kernel(x_ref, gain_ref, wa_ref, wqb_ref, wkvt_ref, wg_ref, wgu_ref, bgu_ref,
                   qa_ref, ka_ref, va_ref, ra_ref, la_ref, qb_ref, kt_ref, vt_ref,
                   kt16_ref, vt16_ref, gbr_ref):
    h = _rms_norm(x_ref[...], gain_ref[...]).astype(BF16)
    pa = H_A * LANES
    mw = va_ref.shape[-1]
    kvt = _dot_nt(wkvt_ref[...], h)
    nseq, _, s = kt_ref.shape
    for i in range(nseq):
        cols = slice(i * s, (i + 1) * s)
        kt_ref[i] = kvt[0:mw, cols]
        vt_ref[i] = kvt[mw:2 * mw, cols]
        kt16_ref[i, 0] = kvt[0:mw, cols].astype(BF16)
        vt16_ref[i, 0] = kvt[mw:2 * mw, cols].astype(BF16)
    qb_ref[...] = _dot(h, wqb_ref[...]).astype(BF16)
    qa_ref[...] = _dot(h, wa_ref[:, 0:pa])
    ka_ref[...] = _dot(h, wa_ref[:, pa:2 * pa])
    va_ref[...] = _dot(h, wa_ref[:, 2 * pa:2 * pa + mw])
    ra_ref[...] = _dot(h, wa_ref[:, 2 * pa + mw:2 * pa + 2 * mw])
    lr = _dot(h, wa_ref[:, 2 * pa + 2 * mw:2 * pa + 2 * mw + LANES])
    gl = _dot(lr.astype(BF16), wgu_ref[...]) + bgu_ref[...]
    la_ref[...] = _log_sigmoid(gl) / GATE_TAU
    gbr_ref[...] = _dot(h, wg_ref[...]).astype(gbr_ref.dtype)


def _in_projection(x, seq_len, gain, wa, wqb, wkvt, wg, wgu, bgu):
    t, d = x.shape
    nb = t // seq_len
    pa = H_A * LANES
    mw = wqb.shape[1]
    tm = ROW_TILE
    row = lambda w: pl.BlockSpec((tm, w), lambda i: (i, 0))
    full = lambda a: pl.BlockSpec(a.shape, lambda i: (0,) * a.ndim)
    if seq_len >= tm:
        per_seq = seq_len // tm
        assert tm == SB_TILE and seq_len % tm == 0
        kt_spec = pl.BlockSpec((1, mw, tm), lambda i: (i // per_seq, 0, i % per_seq))
        kt16_spec = pl.BlockSpec((1, 1, mw, tm), lambda i: (i // per_seq, i % per_seq, 0, 0))
        kt16_shape = (nb, per_seq, mw, tm)
    else:
        nseq = tm // seq_len
        assert tm % seq_len == 0
        kt_spec = pl.BlockSpec((nseq, mw, seq_len), lambda i: (i, 0, 0))
        kt16_spec = pl.BlockSpec((nseq, 1, mw, seq_len), lambda i: (i, 0, 0, 0))
        kt16_shape = (nb, 1, mw, seq_len)
    outs = [
        (jax.ShapeDtypeStruct((t, pa), F32), row(pa)), (jax.ShapeDtypeStruct((t, pa), F32), row(pa)),
        (jax.ShapeDtypeStruct((t, mw), F32), row(mw)), (jax.ShapeDtypeStruct((t, mw), F32), row(mw)),
        (jax.ShapeDtypeStruct((t, pa), F32), row(pa)),
        (jax.ShapeDtypeStruct((t, mw), BF16), row(mw)),
        (jax.ShapeDtypeStruct((nb, mw, seq_len), F32), kt_spec), (jax.ShapeDtypeStruct((nb, mw, seq_len), F32), kt_spec),
        (jax.ShapeDtypeStruct(kt16_shape, BF16), kt16_spec), (jax.ShapeDtypeStruct(kt16_shape, BF16), kt16_spec),
        (jax.ShapeDtypeStruct((t, wg.shape[1]), BF16), row(wg.shape[1])),
    ]
    return pl.pallas_call(
        _inproj_kernel,
        grid=(t // tm,),
        in_specs=[row(d), full(gain), full(wa), full(wqb), full(wkvt), full(wg), full(wgu), full(bgu)],
        out_specs=[spec for _, spec in outs],
        out_shape=[shape for shape, _ in outs],
        compiler_params=_cparams(("arbitrary",)),
        name="in_projection",
    )(x, gain, wa, wqb, wkvt, wg, wgu, bgu)


def _gla_chunk(q, k, v, b, st):
    c = q.shape[0]
    b_last = b[c - 1:c, :]
    rows = lax.broadcasted_iota(I32, (c, LANES), 0)
    nsub = c // GLA_SUB
    refs = [jnp.zeros((1, LANES), F32)] + [b[i * GLA_SUB - 1:i * GLA_SUB, :] for i in range(1, nsub)]
    ref_rows = refs[0]
    for i in range(1, nsub):
        ref_rows = jnp.where(rows >= i * GLA_SUB, refs[i], ref_rows)
    q_rel = q * jnp.exp(b - ref_rows)
    lhs = jnp.concatenate(
        [jnp.where((rows >= i * GLA_SUB) & (rows < (i + 1) * GLA_SUB), q_rel, 0.0) for i in range(nsub)],
        axis=1).astype(BF16)
    rhs = jnp.concatenate(
        [jnp.where(rows < (i + 1) * GLA_SUB, k * jnp.exp(jnp.minimum(refs[i] - b, GLA_EXP_CLAMP)), 0.0)
         for i in range(nsub)], axis=1).astype(BF16)
    att = _dot_nt(lhs, rhs)
    tt = lax.broadcasted_iota(I32, (c, c), 0)
    ss = lax.broadcasted_iota(I32, (c, c), 1)
    att = jnp.where(ss <= tt, att, 0.0)
    v16 = v.astype(BF16)
    inter = _dot_nt((q * jnp.exp(b)).astype(BF16), st.astype(BF16))
    intra = _dot(att.astype(BF16), v16)
    kd = (k * jnp.exp(b_last - b)).astype(BF16)
    st_new = st * jnp.exp(b_last) + _dot_tn(v16, kd)
    return inter + intra, st_new


def _gla_kernel(qa_ref, ka_ref, va_ref, ra_ref, la_ref, s0_ref, gain_ref, o_ref, sfin_ref, st_ref):
    j = pl.program_id(1)
    nj = pl.num_programs(1)
    nseq, rows_per_step, _ = qa_ref.shape
    c = GLA_CHUNK
    zpad = jnp.zeros((LANES - DK_A, DV_A), F32)

    @pl.when(j == 0)
    def _():
        for si in range(nseq):
            for h in range(H_A):
                st_ref[si * H_A + h] = jnp.concatenate([s0_ref[si, h], zpad], axis=0).T

    ti = lax.broadcasted_iota(I32, (rows_per_step, rows_per_step), 0)
    si = lax.broadcasted_iota(I32, (rows_per_step, rows_per_step), 1)
    chunk_shift = c.bit_length() - 1
    same_chunk = (ti >> chunk_shift) == (si >> chunk_shift)
    tril_blocks = jnp.where(same_chunk & (si <= ti), 1.0, 0.0).astype(BF16)
    gain = gain_ref[...]
    for si in range(nseq):
        la_hi, la_lo = _split_bf16(la_ref[si])
        b_all = _dot(tril_blocks, la_hi) + _dot(tril_blocks, la_lo)
        for h in range(H_A):
            hp = slice(h * LANES, (h + 1) * LANES)
            hv = slice(h * DV_A, (h + 1) * DV_A)
            st = st_ref[si * H_A + h]
            for ci in range(rows_per_step // c):
                r0 = ci * c
                q = qa_ref[si, r0:r0 + c, hp] * (DK_A ** -0.5)
                o, st = _gla_chunk(q, ka_ref[si, r0:r0 + c, hp], va_ref[si, r0:r0 + c, hv],
                                   b_all[r0:r0 + c, hp], st)
                r = ra_ref[si, r0:r0 + c, hv]
                o = _rms_norm(o, gain) * (r * _sigmoid(r))
                o_ref[si, r0:r0 + c, hv] = o.astype(o_ref.dtype)
            st_ref[si * H_A + h] = st

    @pl.when(j == nj - 1)
    def _():
        for si in range(nseq):
            for h in range(H_A):
                sfin_ref[si, h] = st_ref[si * H_A + h].T[0:DK_A, :]


def _gla(qa, ka, va, ra, la, s0, gain, rows_per_step):
    b, s, pa = qa.shape
    mw = va.shape[-1]
    ns = GLA_SEQS
    assert b % ns == 0
    seq = lambda w: pl.BlockSpec((ns, rows_per_step, w), lambda i, j: (i, j, 0))
    state = pl.BlockSpec((ns, H_A, DK_A, DV_A), lambda i, j: (i, 0, 0, 0))
    return pl.pallas_call(
        _gla_kernel,
        grid=(b // ns, s // rows_per_step),
        in_specs=[seq(pa), seq(pa), seq(mw), seq(mw), seq(pa), state,
                  pl.BlockSpec(gain.shape, lambda i, j: (0, 0))],
        out_specs=[seq(mw), state],
        out_shape=[jax.ShapeDtypeStruct((b, s, mw), BF16),
                   jax.ShapeDtypeStruct((b, H_A, DK_A, DV_A), F32)],
        scratch_shapes=[pltpu.VMEM((ns * H_A, LANES, LANES), F32)],
        compiler_params=_cparams(("arbitrary", "arbitrary")),
        name="gla",
    )(qa, ka, va, ra, la, s0, gain)


def _head_lane_masks():
    lane = lax.broadcasted_iota(I32, (1, LANES), 1)
    return lane < DH_B, lane >= DH_B


def _sb_neg_tri(tk):
    ji = lax.broadcasted_iota(I32, (tk, tk), 0)
    si = lax.broadcasted_iota(I32, (tk, tk), 1)
    return jnp.where(ji >= si, -1.0, 0.0).astype(BF16)


def _sb_stack_queries(q, qs_ref, base=0):
    m0, m1 = _head_lane_masks()
    for p in range(q.shape[1] // LANES):
        qp = q[:, p * LANES:(p + 1) * LANES] * (DH_B ** -0.5)
        zero = jnp.zeros_like(qp)
        qs_ref[base + p] = jnp.concatenate([jnp.where(m0, qp, zero), jnp.where(m1, qp, zero)], axis=0)


def _pair_lanes(p):
    return slice(p * LANES, (p + 1) * LANES)


def _lane_fit(x, width):
    if width >= LANES:
        return jnp.concatenate([x] * (width // LANES), axis=1)
    return x[:, 0:width]


def _sb_tile_step(qs_ref, acc_ref, carry_ref, k_tile, v_tile, ntri, diagonal):
    npair, rows, _ = qs_ref.shape
    tq = rows // 2
    tk = ntri.shape[1]
    m0, _ = _head_lane_masks()
    if diagonal:
        t = lax.broadcasted_iota(I32, (rows, tk), 0)
        t = jnp.where(t >= tq, t - tq, t)
        visible = lax.broadcasted_iota(I32, (rows, tk), 1) < t
    for p in range(npair):
        z = _dot(qs_ref[p], k_tile(p)) * LOG2_E
        sp = jnp.maximum(z, 0.0) + jnp.log2(1.0 + jnp.exp2(-jnp.abs(z)))
        if diagonal:
            sp = jnp.where(visible, sp, 0.0)
        suffix = _dot(sp.astype(BF16), ntri)
        carry = carry_ref[p]
        w = jnp.exp2(z + suffix + _lane_fit(carry, tk))
        if diagonal:
            w = jnp.where(visible, w, 0.0)
        pv = _dot_nt(w.astype(BF16), v_tile(p))
        acc_ref[p] += jnp.where(m0, pv[0:tq], pv[tq:rows])
        carry_ref[p] = carry + jnp.broadcast_to(suffix[:, 0:1], carry.shape)


def _sb_prompt_kernel(q_ref, k_ref, v_ref, o_ref, qs_ref, acc_ref, carry_ref):
    qi = pl.program_id(1)
    tk = SB_TILE
    _sb_stack_queries(q_ref[0], qs_ref)
    acc_ref[...] = jnp.zeros_like(acc_ref)
    carry_ref[...] = jnp.zeros_like(carry_ref)
    ntri = _sb_neg_tri(tk)

    def step(jb, diagonal):
        _sb_tile_step(qs_ref, acc_ref, carry_ref, lambda p: k_ref[0, jb, _pair_lanes(p), :],
                      lambda p: v_ref[0, jb, _pair_lanes(p), :], ntri, diagonal)

    step(qi, True)

    def body(i, c):
        step(qi - 1 - i, False)
        return c

    lax.fori_loop(0, qi, body, 0)
    for p in range(acc_ref.shape[0]):
        o_ref[0, :, p * LANES:(p + 1) * LANES] = acc_ref[p].astype(o_ref.dtype)


def _sb_scratch(tq, npair):
    return [pltpu.VMEM((npair, 2 * tq, LANES), BF16), pltpu.VMEM((npair, tq, LANES), F32),
            pltpu.VMEM((npair, 2 * tq, LANES), F32)]


def _sb_prompt(q, kt, vt):
    b, s, w = q.shape
    tq = SB_TILE
    assert kt.shape == (b, s // tq, w, tq)
    qspec = pl.BlockSpec((1, tq, w), lambda i, j: (i, j, 0))
    kvspec = pl.BlockSpec((1,) + kt.shape[1:], lambda i, j: (i, 0, 0, 0))
    return pl.pallas_call(
        _sb_prompt_kernel,
        grid=(b, s // tq),
        in_specs=[qspec, kvspec, kvspec],
        out_specs=qspec,
        out_shape=jax.ShapeDtypeStruct((b, s, w), BF16),
        scratch_shapes=_sb_scratch(tq, w // LANES),
        compiler_params=_cparams(("arbitrary", "arbitrary")),
        name="sb_prompt",
    )(q, kt, vt)


def _sb_sample_kernel(q_ref, kn_ref, vn_ref, kp_ref, vp_ref, o_ref, qs_ref, acc_ref, carry_ref):
    nseq, sq, w = q_ref.shape
    past = kp_ref.shape[3]
    npair = w // LANES
    tk = SB_TILE
    for si in range(nseq):
        _sb_stack_queries(q_ref[si], qs_ref, si * npair)
    acc_ref[...] = jnp.zeros_like(acc_ref)
    carry_ref[...] = jnp.zeros_like(carry_ref)
    _sb_tile_step(qs_ref, acc_ref, carry_ref, lambda e: kn_ref[e // npair, 0, _pair_lanes(e % npair), :],
                  lambda e: vn_ref[e // npair, 0, _pair_lanes(e % npair), :], _sb_neg_tri(sq), True)
    ntri = _sb_neg_tri(tk)

    def body(i, c):
        cols = pl.ds(pl.multiple_of(past - (i + 1) * tk, tk), tk)

        def pair(ref, e):
            p = e % npair
            return ref[e // npair, 2 * p:2 * p + 2, :, cols].reshape(LANES, tk).astype(BF16)

        _sb_tile_step(qs_ref, acc_ref, carry_ref, lambda e: pair(kp_ref, e), lambda e: pair(vp_ref, e), ntri, False)
        return c

    lax.fori_loop(0, past // tk, body, 0)
    for e in range(acc_ref.shape[0]):
        o_ref[e // npair, :, _pair_lanes(e % npair)] = acc_ref[e].astype(o_ref.dtype)


def _sb_sample(q, kt_new, vt_new, kt_past, vt_past):
    b, sq, w = q.shape
    past = kt_past.shape[3]
    ns = SB_SAMPLE_SEQS
    assert past % SB_TILE == 0 and 2 * DH_B == LANES and b % ns == 0
    qspec = pl.BlockSpec((ns, sq, w), lambda i: (i, 0, 0))
    new = pl.BlockSpec((ns, 1, w, sq), lambda i: (i, 0, 0, 0))
    old = pl.BlockSpec((ns, H_B, DH_B, past), lambda i: (i, 0, 0, 0))
    return pl.pallas_call(
        _sb_sample_kernel,
        grid=(b // ns,),
        in_specs=[qspec, new, new, old, old],
        out_specs=qspec,
        out_shape=jax.ShapeDtypeStruct((b, sq, w), BF16),
        scratch_shapes=_sb_scratch(sq, ns * (w // LANES)),
        compiler_params=_cparams(("arbitrary",)),
        name="sb_sample",
    )(q, kt_new, vt_new, kt_past, vt_past)


def _first_argmax(vals, nrows):
    idx = lax.broadcasted_iota(I32, vals.shape, 0)
    top = jnp.max(vals, axis=0, keepdims=True)
    first = jnp.min(jnp.where(vals == top, idx, nrows), axis=0, keepdims=True)
    return top, first, idx


def _merge_kernel(oa_ref, ob_ref, g_ref, x_ref, wb0_ref, wb1_ref, wo_ref, gain_ref, wr_ref, br_ref,
                  x1_ref, h2_ref, eid_ref, wcol_ref):
    d = x_ref.shape[1]
    ya = _dot(oa_ref[...], wb0_ref[...])
    yb = _dot(ob_ref[...], wb1_ref[...])
    g = g_ref[...].astype(F32)
    m = _sigmoid(g[:, 0:d]) * ya + _sigmoid(g[:, d:2 * d]) * yb
    x1 = x_ref[...] + _dot(m.astype(BF16), wo_ref[...])
    x1_ref[...] = x1
    h2 = _rms_norm(x1, gain_ref[...])
    h2_ref[...] = h2.reshape(h2_ref.shape)

    h_hi, h_lo = _split_bf16(h2)
    w_hi, w_lo = _split_bf16(wr_ref[...])
    lt = _dot_nt(w_hi, h_hi) + _dot_nt(w_hi, h_lo) + _dot_nt(w_lo, h_hi) + br_ref[:, 0:1]
    gl = lt[0:N_GROUPS, :]
    g_top, g_idx, _ = _first_argmax(gl, N_GROUPS)
    g_e = jnp.exp(gl - g_top)
    g_p = jnp.max(g_e / jnp.sum(g_e, axis=0, keepdims=True), axis=0, keepdims=True)
    el = jnp.zeros((EXPERTS_PER_GROUP, lt.shape[1]), F32)
    for g in range(N_GROUPS):
        r0 = 8 + g * EXPERTS_PER_GROUP
        el = jnp.where(g_idx == g, lt[r0:r0 + EXPERTS_PER_GROUP, :], el)
    e_top, i1, eidx = _first_argmax(el, EXPERTS_PER_GROUP)
    e_e = jnp.exp(el - e_top)
    e_p = e_e / jnp.sum(e_e, axis=0, keepdims=True)
    p1 = jnp.max(e_p, axis=0, keepdims=True)
    rest = jnp.where(eidx == i1, -1.0, e_p)
    p2, i2, _ = _first_argmax(rest, EXPERTS_PER_GROUP)
    norm = p1 + p2
    w1 = g_p * (p1 / norm)
    w2 = g_p * (p2 / norm)
    eid_ref[...] = jnp.concatenate([g_idx * EXPERTS_PER_GROUP + i1, g_idx * EXPERTS_PER_GROUP + i2], axis=0)
    rows = lax.broadcasted_iota(I32, (LANES, lt.shape[1]), 0)
    wrows = jnp.where(rows == 0, w1, jnp.where(rows == 1, w2, 0.0))
    wcol_ref[...] = wrows.T


def _merge(oa, ob, gbr, x, wb0, wb1, wo, gain, wr, br):
    t, d = x.shape
    tm = ROW_TILE
    row = lambda w: pl.BlockSpec((tm, w), lambda i: (i, 0))
    full = lambda a: pl.BlockSpec(a.shape, lambda i: (0,) * a.ndim)
    return pl.pallas_call(
        _merge_kernel,
        grid=(t // tm,),
        in_specs=[row(oa.shape[1]), row(ob.shape[1]), row(gbr.shape[1]), row(d),
                  full(wb0), full(wb1), full(wo), full(gain), full(wr), full(br)],
        out_specs=[row(d), pl.BlockSpec((tm, d // LANES, LANES), lambda i: (i, 0, 0)),
                   pl.BlockSpec((TOP_K, tm), lambda i: (0, i)), row(LANES)],
        out_shape=[jax.ShapeDtypeStruct((t, d), F32), jax.ShapeDtypeStruct((t, d // LANES, LANES), F32),
                   jax.ShapeDtypeStruct((TOP_K, t), I32), jax.ShapeDtypeStruct((t, LANES), F32)],
        compiler_params=_cparams(("arbitrary",)),
        name="merge_router",
    )(oa, ob, gbr, x, wb0, wb1, wo, gain, wr, br)


def _positions_kernel(eid_ref, dest_ref, counts_ref, rank_ref):
    nblk, width = eid_ref.shape
    ji = lax.broadcasted_iota(I32, (width, width), 0)
    si = lax.broadcasted_iota(I32, (width, width), 1)
    prefix = jnp.where(ji <= si, 1.0, 0.0).astype(BF16)
    expert = lax.broadcasted_iota(I32, (N_EXPERTS, width), 0)

    def onehot(i):
        return expert == eid_ref[pl.ds(i, 1), :]

    def rank_body(i, run):
        oh = onehot(i)
        cum = _dot(jnp.where(oh, 1.0, 0.0).astype(BF16), prefix) + run
        rank_ref[pl.ds(i, 1), :] = jnp.sum(jnp.where(oh, cum, 0.0), axis=0, keepdims=True) - 1.0
        return cum[:, width - 1:width]

    counts = lax.fori_loop(0, nblk, rank_body, jnp.zeros((N_EXPERTS, 1), F32))
    counts_ref[...] = jnp.broadcast_to(counts, counts_ref.shape).astype(I32)
    c_hi = jnp.floor(counts * (1.0 / 256.0))
    c_lo = counts - 256.0 * c_hi
    ei = lax.broadcasted_iota(I32, (N_EXPERTS, N_EXPERTS), 0)
    ej = lax.broadcasted_iota(I32, (N_EXPERTS, N_EXPERTS), 1)
    strict = jnp.where(ej < ei, 1.0, 0.0).astype(BF16)
    digits = jnp.concatenate([jnp.broadcast_to(c_hi, (N_EXPERTS, LANES)),
                              jnp.broadcast_to(c_lo, (N_EXPERTS, LANES))], axis=1).astype(BF16)
    sums = _dot(strict, digits)
    start = 256.0 * sums[:, 0:1] + sums[:, LANES:LANES + 1]

    def dest_body(i, carry):
        off = jnp.sum(jnp.where(onehot(i), start, 0.0), axis=0, keepdims=True)
        dest_ref[pl.ds(i, 1), :] = (rank_ref[pl.ds(i, 1), :] + off).astype(I32)
        return carry

    lax.fori_loop(0, nblk, dest_body, 0)


def _positions(eid_blocks):
    nblk, width = eid_blocks.shape
    vm = lambda shape: pl.BlockSpec(shape, lambda: (0,) * len(shape))
    return pl.pallas_call(
        _positions_kernel,
        in_specs=[vm((nblk, width))],
        out_specs=[vm((nblk, width)), vm((N_EXPERTS, LANES))],
        out_shape=[jax.ShapeDtypeStruct((nblk, width), I32), jax.ShapeDtypeStruct((N_EXPERTS, LANES), I32)],
        scratch_shapes=[pltpu.VMEM((nblk, width), F32)],
        name="positions",
    )(eid_blocks)


def _dispatch_kernel(n_prompt_tiles, dest_ref, hp_ref, hs_ref, xs_ref, sem):
    i = pl.program_id(0)
    tm = dest_ref.shape[1]

    def scatter(src_ref):
        def start(r, c):
            for k in range(TOP_K):
                pltpu.make_async_copy(src_ref.at[r], xs_ref.at[dest_ref[k, r]], sem).start(priority=k)
            return c

        lax.fori_loop(0, tm, start, 0, unroll=DMA_UNROLL)
        for k in range(TOP_K):
            pltpu.make_async_copy(src_ref, xs_ref.at[pl.ds(0, tm)], sem).wait()

    @pl.when(i < n_prompt_tiles)
    def _():
        scatter(hp_ref)

    @pl.when(i >= n_prompt_tiles)
    def _():
        scatter(hs_ref)


def _dispatch(dest, h_prompt, h_sample):
    t = dest.shape[1]
    slab = h_prompt.shape[1:]
    tm = DISPATCH_TILE
    assert h_prompt.shape[0] % tm == 0 and h_sample.shape[0] % tm == 0
    npt = h_prompt.shape[0] // tm
    return pl.pallas_call(
        functools.partial(_dispatch_kernel, npt),
        grid=(t // tm,),
        in_specs=[pl.BlockSpec((TOP_K, tm), lambda i: (0, i), memory_space=pltpu.SMEM),
                  pl.BlockSpec((tm,) + slab, lambda i: (jnp.minimum(i, npt - 1), 0, 0)),
                  pl.BlockSpec((tm,) + slab, lambda i: (jnp.maximum(i - npt, 0), 0, 0))],
        out_specs=pl.BlockSpec(memory_space=pl.ANY),
        out_shape=jax.ShapeDtypeStruct((TOP_K * t,) + slab, F32),
        scratch_shapes=[pltpu.SemaphoreType.DMA(())],
        compiler_params=_cparams(("arbitrary",)),
        name="dispatch",
    )(dest, h_prompt, h_sample)


def _experts_kernel(vblk_ref, vexp_ref, vlo_ref, vhi_ref, xs_ref, wg_ref, wu_ref, wd_ref, ys_ref,
                    wg16_ref, wu16_ref, wd16_ref):
    v = pl.program_id(0)
    lo = vlo_ref[v]
    hi = vhi_ref[v]
    prev = jnp.maximum(v - 1, 0)
    first = jnp.logical_or(v == 0, vblk_ref[v] != vblk_ref[prev])
    new_expert = jnp.logical_or(v == 0, vexp_ref[v] != vexp_ref[prev])

    @pl.when(new_expert)
    def _():
        wg16_ref[...] = wg_ref[0].astype(BF16)
        wu16_ref[...] = wu_ref[0].astype(BF16)
        wd16_ref[...] = wd_ref[0].astype(BF16)

    @pl.when(hi > lo)
    def _():
        tm = xs_ref.shape[0]
        d = wg_ref.shape[1]
        x = xs_ref[...].reshape(tm, d).astype(BF16)
        gate = _dot(x, wg16_ref[...])
        up = _dot(x, wu16_ref[...])
        hid = (gate * _sigmoid(gate) * up).astype(BF16)
        y = _dot(hid, wd16_ref[...])
        rows = lax.broadcasted_iota(I32, y.shape, 0)
        mine = (rows >= lo) & (rows < hi)

        @pl.when(first)
        def _():
            ys_ref[...] = jnp.where(mine, y, 0.0).reshape(ys_ref.shape)

        @pl.when(jnp.logical_not(first))
        def _():
            ys_ref[...] = jnp.where(mine, y, ys_ref[...].reshape(tm, d)).reshape(ys_ref.shape)


def _experts(vblk, vexp, vlo, vhi, xs, wg, wu, wd):
    a = xs.shape[0]
    slab = xs.shape[1:]
    d, de = wg.shape[1:]
    tm = MOE_TILE
    grid_spec = pltpu.PrefetchScalarGridSpec(
        num_scalar_prefetch=4,
        grid=(vblk.shape[0],),
        in_specs=[pl.BlockSpec((tm,) + slab, lambda v, b, e, lo, hi: (b[v], 0, 0)),
                  pl.BlockSpec((1, d, de), lambda v, b, e, lo, hi: (e[v], 0, 0)),
                  pl.BlockSpec((1, d, de), lambda v, b, e, lo, hi: (e[v], 0, 0)),
                  pl.BlockSpec((1, de, d), lambda v, b, e, lo, hi: (e[v], 0, 0))],
        out_specs=pl.BlockSpec((tm,) + slab, lambda v, b, e, lo, hi: (b[v], 0, 0)),
        scratch_shapes=[pltpu.VMEM((d, de), BF16), pltpu.VMEM((d, de), BF16), pltpu.VMEM((de, d), BF16)],
    )
    return pl.pallas_call(
        _experts_kernel,
        grid_spec=grid_spec,
        out_shape=jax.ShapeDtypeStruct((a,) + slab, F32),
        compiler_params=_cparams(("arbitrary",)),
        name="experts",
    )(vblk, vexp, vlo, vhi, xs, wg, wu, wd)


def _visit_plan(counts, n_rows):
    tm = MOE_TILE
    nblk = n_rows // tm
    n_visits = nblk + N_EXPERTS - 1
    ends = jnp.cumsum(counts)
    starts = ends - counts
    first_blk = starts // tm
    nvis = jnp.where(counts > 0, (ends + tm - 1) // tm - first_blk, 0)
    vis_end = jnp.cumsum(nvis)
    vis_start = vis_end - nvis
    v = jnp.arange(n_visits, dtype=I32)
    e = jnp.minimum(jnp.sum((vis_end[None, :] <= v[:, None]).astype(I32), axis=1), N_EXPERTS - 1)
    valid = v < vis_end[-1]
    blk = first_blk[e] + (v - vis_start[e])
    lo = jnp.clip(starts[e] - blk * tm, 0, tm)
    hi = jnp.clip(ends[e] - blk * tm, 0, tm)
    last_e = jnp.max(jnp.where(counts > 0, jnp.arange(N_EXPERTS, dtype=I32), 0))
    blk = jnp.where(valid, blk, nblk - 1).astype(I32)
    e = jnp.where(valid, e, last_e).astype(I32)
    lo = jnp.where(valid, lo, 0).astype(I32)
    hi = jnp.where(valid, hi, 0).astype(I32)
    return blk, e, lo, hi


def _combine_kernel(dest_ref, ys_ref, x1_ref, wcol_ref, gain_ref, out_ref, buf_ref, sems):
    tm, d = x1_ref.shape
    part = tm // COMBINE_PARTS

    def start(r, c, sem):
        for k in range(TOP_K):
            pltpu.make_async_copy(ys_ref.at[dest_ref[k, r]], buf_ref.at[k, r], sem).start(priority=k)
        return c

    for h in range(COMBINE_PARTS):
        lax.fori_loop(h * part, (h + 1) * part, functools.partial(start, sem=sems.at[h]), 0, unroll=DMA_UNROLL)
    for h in range(COMBINE_PARTS):
        rows = pl.ds(h * part, part)
        for k in range(TOP_K):
            pltpu.make_async_copy(ys_ref.at[rows], buf_ref.at[k, rows], sems.at[h]).wait()
        y = (wcol_ref[rows, 0:1] * buf_ref[0, rows].reshape(part, d)
             + wcol_ref[rows, 1:2] * buf_ref[1, rows].reshape(part, d))
        out_ref[rows, :] = _rms_norm(x1_ref[rows, :] + y, gain_ref[...])


def _combine(dest, ys, x1, wcol, gain):
    t, d = x1.shape
    tm = ROW_TILE
    row = lambda w: pl.BlockSpec((tm, w), lambda i: (i, 0))
    return pl.pallas_call(
        _combine_kernel,
        grid=(t // tm,),
        in_specs=[pl.BlockSpec((TOP_K, tm), lambda i: (0, i), memory_space=pltpu.SMEM),
                  pl.BlockSpec(memory_space=pl.ANY), row(d), row(LANES),
                  pl.BlockSpec(gain.shape, lambda i: (0, 0))],
        out_specs=row(d),
        out_shape=jax.ShapeDtypeStruct((t, d), F32),
        scratch_shapes=[pltpu.VMEM((TOP_K, tm) + ys.shape[1:], F32), pltpu.SemaphoreType.DMA((COMBINE_PARTS,))],
        compiler_params=_cparams(("arbitrary",)),
        name="combine",
    )(dest, ys, x1, wcol, gain)


def _pad_heads(w):
    r = w.shape[0]
    return jnp.pad(w.reshape(r, H_A, DK_A), ((0, 0), (0, 0), (0, LANES - DK_A))).reshape(r, H_A * LANES)


def _prepare_weights(w_in, w_gla_gate_up, b_gla_gate, w_branch, w_out, w_router_group, b_router_group,
                     w_router_expert, b_router_expert):
    d = w_in.shape[0]
    qk = H_A * DK_A
    mw = H_A * DV_A
    c = 0
    w_qa, c = w_in[:, c:c + qk], c + qk
    w_ka, c = w_in[:, c:c + qk], c + qk
    w_va, c = w_in[:, c:c + mw], c + mw
    w_ra, c = w_in[:, c:c + mw], c + mw
    w_lr, c = w_in[:, c:c + GATE_RANK], c + GATE_RANK
    w_b, c = w_in[:, c:c + 3 * mw], c + 3 * mw
    w_g = w_in[:, c:]
    wa = jnp.concatenate([_pad_heads(w_qa), _pad_heads(w_ka), w_va, w_ra,
                          jnp.pad(w_lr, ((0, 0), (0, LANES - GATE_RANK)))], axis=1).astype(BF16)
    wgu = jnp.pad(_pad_heads(w_gla_gate_up), ((0, LANES - GATE_RANK), (0, 0))).astype(BF16)
    bgu = _pad_heads(b_gla_gate[None, :])
    wr = jnp.zeros((LANES, d), F32)
    wr = wr.at[0:N_GROUPS].set(w_router_group.T).at[8:8 + N_EXPERTS].set(w_router_expert.T)
    br = jnp.zeros((LANES,), F32).at[0:N_GROUPS].set(b_router_group).at[8:8 + N_EXPERTS].set(b_router_expert)
    br = jnp.broadcast_to(br[:, None], (LANES, LANES))
    return dict(wa=wa, wqb=w_b[:, 0:mw].astype(BF16), wkvt=w_b[:, mw:3 * mw].T.astype(BF16),
                wg=w_g.astype(BF16), wgu=wgu, bgu=bgu,
                wb0=w_branch[0].astype(BF16), wb1=w_branch[1].astype(BF16), wo=w_out.astype(BF16),
                wr=wr, br=br)


def _mixers(x, s0, k_past, v_past, w, norm_mix_gain, gla_norm_gain, norm_ffn_gain):
    b, s, d = x.shape
    xf = x.reshape(b * s, d)
    qa, ka, va, ra, la, qb, kt, vt, kt16, vt16, gbr = _in_projection(
        xf, s, norm_mix_gain[None, :], w["wa"], w["wqb"], w["wkvt"], w["wg"], w["wgu"], w["bgu"])
    seq = lambda a: a.reshape(b, s, a.shape[-1])
    oa, s_new = _gla(seq(qa), seq(ka), seq(va), seq(ra), seq(la), s0, gla_norm_gain[None, :],
                     min(s, ROW_TILE))
    to_channel_major = lambda a: jnp.transpose(a, (0, 2, 3, 1))
    if k_past is None:
        ob = _sb_prompt(seq(qb), kt16, vt16)
    else:
        ob = _sb_sample(seq(qb), kt16, vt16, to_channel_major(k_past), to_channel_major(v_past))
    x1, h2, eid, wcol = _merge(oa.reshape(b * s, -1), ob.reshape(b * s, -1), gbr, xf, w["wb0"], w["wb1"],
                               w["wo"], norm_ffn_gain[None, :], w["wr"], w["br"])
    from_channel_major = lambda a: jnp.transpose(a.reshape(b, H_B, DH_B, s), (0, 3, 1, 2))
    return x1, h2, eid, wcol, s_new, from_channel_major(kt), from_channel_major(vt)


def kernel(x_prompt, x_sample, state_gla, cache_sb_k, cache_sb_v, norm_mix_gain, w_in, w_gla_gate_up, b_gla_gate, gla_norm_gain, w_branch, w_out, norm_ffn_gain, w_router_group, b_router_group, w_router_expert, b_router_expert, w_exp_gate, w_exp_up, w_exp_down, norm_final_gain):
    depth = w_in.shape[0]
    assert depth == 1, "one trunk layer per step"
    l = 0
    w = _prepare_weights(w_in[l], w_gla_gate_up[l], b_gla_gate[l], w_branch[l], w_out[l], w_router_group[l],
                         b_router_group[l], w_router_expert[l], b_router_expert[l])
    bp, sp, d = x_prompt.shape
    bs, ss, _ = x_sample.shape
    s0 = jnp.zeros((bp, H_A, DK_A, DV_A), x_prompt.dtype)
    x1p, h2p, eidp, wcolp, gla_p, k_p, v_p = _mixers(
        x_prompt, s0, None, None, w, norm_mix_gain[l], gla_norm_gain[l], norm_ffn_gain[l])
    x1s, h2s, eids, wcols, gla_s, k_s, v_s = _mixers(
        x_sample, state_gla[l], cache_sb_k[l], cache_sb_v[l], w, norm_mix_gain[l], gla_norm_gain[l],
        norm_ffn_gain[l])

    tp, ts = bp * sp, bs * ss
    eid = jnp.concatenate([eidp, eids], axis=1)
    dest_blocks, counts = _positions(eid.reshape(-1, SORT_WIDTH))
    dest = dest_blocks.reshape(TOP_K, tp + ts)
    xs = _dispatch(dest, h2p, h2s)
    vblk, vexp, vlo, vhi = _visit_plan(counts[:, 0], TOP_K * (tp + ts))
    ys = _experts(vblk, vexp, vlo, vhi, xs, w_exp_gate[l], w_exp_up[l], w_exp_down[l])
    gf = norm_final_gain[None, :]
    y_prompt = _combine(dest[:, :tp], ys, x1p, wcolp, gf).reshape(bp, sp, d)
    y_sample = _combine(dest[:, tp:], ys, x1s, wcols, gf).reshape(bs, ss, d)
    return (y_prompt, y_sample, gla_p[None], k_p[None], v_p[None], gla_s[None], k_s[None], v_s[None])
```

```python
import functools

import jax
import jax.numpy as jnp
from jax import lax
from jax.experimental import pallas as pl
from jax.experimental.pallas import tpu as pltpu

F32 = jnp.float32
BF16 = jnp.bfloat16
I32 = jnp.int32

LANES = 128
MXU_DEPTH = 256
LOG2_E = 1.4426950408889634
RMS_EPS = 1e-6
GATE_TAU = 16.0
H_A = 4
DK_A = 64
DV_A = 128
GATE_RANK = 16
H_B = 8
DH_B = 64
N_GROUPS = 4
EXPERTS_PER_GROUP = 8
N_EXPERTS = N_GROUPS * EXPERTS_PER_GROUP
TOP_K = 2
GLA_CHUNK = 64
GLA_SUB = 16
GLA_EXP_CLAMP = 80.0
GLA_SEQS = 2
ROW_TILE = 256
SB_TILE = 256
SB_SAMPLE_SEQS = 2
MOE_TILE = 512
SORT_WIDTH = 256
MERGE_TILE = 512
DISPATCH_TILE = 512
COMBINE_PARTS = 4
DMA_UNROLL = 8
VMEM_LIMIT = 56 * 1024 * 1024


def _cparams(sem):
    return pltpu.CompilerParams(dimension_semantics=sem, vmem_limit_bytes=VMEM_LIMIT)


def _dot(a, b):
    return jnp.dot(a, b, preferred_element_type=F32)


def _dot_nt(a, b):
    return lax.dot_general(a, b, (((1,), (1,)), ((), ())), preferred_element_type=F32)


def _dot_tn(a, b):
    return lax.dot_general(a, b, (((0,), (0,)), ((), ())), preferred_element_type=F32)


def _split_bf16(x):
    hi = x.astype(BF16)
    lo = (x - hi.astype(F32)).astype(BF16)
    return hi, lo


def _log_sigmoid(x):
    return jnp.minimum(x, 0.0) - jnp.log(1.0 + jnp.exp(-jnp.abs(x)))


def _sigmoid(x):
    return 1.0 / (1.0 + jnp.exp(-x))


def _rms_norm(x, gain):
    return x * lax.rsqrt(jnp.mean(x * x, axis=-1, keepdims=True) + RMS_EPS) * gain


def _inproj_kernel(x_ref, gain_ref, wa_ref, wqb_ref, wkvt_ref, wg_ref, wgu_ref, bgu_ref,
                   qa_ref, ka_ref, va_ref, ra_ref, la_ref, qb_ref, kt_ref, vt_ref,
                   kt16_ref, vt16_ref, gbr_ref):
    h = _rms_norm(x_ref[...], gain_ref[...]).astype(BF16)
    pa = H_A * LANES
    mw = va_ref.shape[-1]
    kvt = _dot_nt(wkvt_ref[...], h)
    nseq, _, s = kt_ref.shape
    for i in range(nseq):
        cols = slice(i * s, (i + 1) * s)
        kt_ref[i] = kvt[0:mw, cols]
        vt_ref[i] = kvt[mw:2 * mw, cols]
        kt16_ref[i, 0] = kvt[0:mw, cols].astype(BF16)
        vt16_ref[i, 0] = kvt[mw:2 * mw, cols].astype(BF16)
    qb_ref[...] = _dot(h, wqb_ref[...]).astype(BF16)
    qa_ref[...] = _dot(h, wa_ref[:, 0:pa])
    ka_ref[...] = _dot(h, wa_ref[:, pa:2 * pa])
    va_ref[...] = _dot(h, wa_ref[:, 2 * pa:2 * pa + mw])
    ra_ref[...] = _dot(h, wa_ref[:, 2 * pa + mw:2 * pa + 2 * mw])
    lr = _dot(h, wa_ref[:, 2 * pa + 2 * mw:2 * pa + 2 * mw + LANES])
    gl = _dot(lr.astype(BF16), wgu_ref[...]) + bgu_ref[...]
    la_ref[...] = _log_sigmoid(gl) / GATE_TAU
    gbr_ref[...] = _dot(h, wg_ref[...]).astype(gbr_ref.dtype)


def _in_projection(x, seq_len, gain, wa, wqb, wkvt, wg, wgu, bgu):
    t, d = x.shape
    nb = t // seq_len
    pa = H_A * LANES
    mw = wqb.shape[1]
    tm = ROW_TILE
    row = lambda w: pl.BlockSpec((tm, w), lambda i: (i, 0))
    full = lambda a: pl.BlockSpec(a.shape, lambda i: (0,) * a.ndim)
    if seq_len >= tm:
        per_seq = seq_len // tm
        assert tm == SB_TILE and seq_len % tm == 0
        kt_spec = pl.BlockSpec((1, mw, tm), lambda i: (i // per_seq, 0, i % per_seq))
        kt16_spec = pl.BlockSpec((1, 1, mw, tm), lambda i: (i // per_seq, i % per_seq, 0, 0))
        kt16_shape = (nb, per_seq, mw, tm)
    else:
        nseq = tm // seq_len
        assert tm % seq_len == 0
        kt_spec = pl.BlockSpec((nseq, mw, seq_len), lambda i: (i, 0, 0))
        kt16_spec = pl.BlockSpec((nseq, 1, mw, seq_len), lambda i: (i, 0, 0, 0))
        kt16_shape = (nb, 1, mw, seq_len)
    outs = [
        (jax.ShapeDtypeStruct((t, pa), F32), row(pa)), (jax.ShapeDtypeStruct((t, pa), F32), row(pa)),
        (jax.ShapeDtypeStruct((t, mw), F32), row(mw)), (jax.ShapeDtypeStruct((t, mw), F32), row(mw)),
        (jax.ShapeDtypeStruct((t, pa), F32), row(pa)),
        (jax.ShapeDtypeStruct((t, mw), BF16), row(mw)),
        (jax.ShapeDtypeStruct((nb, mw, seq_len), F32), kt_spec), (jax.ShapeDtypeStruct((nb, mw, seq_len), F32), kt_spec),
        (jax.ShapeDtypeStruct(kt16_shape, BF16), kt16_spec), (jax.ShapeDtypeStruct(kt16_shape, BF16), kt16_spec),
        (jax.ShapeDtypeStruct((t, wg.shape[1]), BF16), row(wg.shape[1])),
    ]
    return pl.pallas_call(
        _inproj_kernel,
        grid=(t // tm,),
        in_specs=[row(d), full(gain), full(wa), full(wqb), full(wkvt), full(wg), full(wgu), full(bgu)],
        out_specs=[spec for _, spec in outs],
        out_shape=[shape for shape, _ in outs],
        compiler_params=_cparams(("arbitrary",)),
        name="in_projection",
    )(x, gain, wa, wqb, wkvt, wg, wgu, bgu)


def _gla_chunk(q, k, v, b, st):
    c = q.shape[0]
    b_last = b[c - 1:c, :]
    rows = lax.broadcasted_iota(I32, (c, LANES), 0)
    nsub = c // GLA_SUB
    refs = [jnp.zeros((1, LANES), F32)] + [b[i * GLA_SUB - 1:i * GLA_SUB, :] for i in range(1, nsub)]
    ref_rows = refs[0]
    for i in range(1, nsub):
        ref_rows = jnp.where(rows >= i * GLA_SUB, refs[i], ref_rows)
    q_rel = q * jnp.exp(b - ref_rows)
    lhs = jnp.concatenate(
        [jnp.where((rows >= i * GLA_SUB) & (rows < (i + 1) * GLA_SUB), q_rel, 0.0) for i in range(nsub)],
        axis=1).astype(BF16)
    rhs = jnp.concatenate(
        [jnp.where(rows < (i + 1) * GLA_SUB, k * jnp.exp(jnp.minimum(refs[i] - b, GLA_EXP_CLAMP)), 0.0)
         for i in range(nsub)], axis=1).astype(BF16)
    att = _dot_nt(lhs, rhs)
    tt = lax.broadcasted_iota(I32, (c, c), 0)
    ss = lax.broadcasted_iota(I32, (c, c), 1)
    att = jnp.where(ss <= tt, att, 0.0)
    v16 = v.astype(BF16)
    inter = _dot_nt((q * jnp.exp(b)).astype(BF16), st.astype(BF16))
    intra = _dot(att.astype(BF16), v16)
    kd = (k * jnp.exp(b_last - b)).astype(BF16)
    st_new = st * jnp.exp(b_last) + _dot_tn(v16, kd)
    return inter + intra, st_new


def _gla_kernel(qa_ref, ka_ref, va_ref, ra_ref, la_ref, s0_ref, gain_ref, o_ref, sfin_ref, st_ref):
    j = pl.program_id(1)
    nj = pl.num_programs(1)
    nseq, rows_per_step, _ = qa_ref.shape
    c = GLA_CHUNK
    zpad = jnp.zeros((LANES - DK_A, DV_A), F32)

    @pl.when(j == 0)
    def _():
        for si in range(nseq):
            for h in range(H_A):
                st_ref[si * H_A + h] = jnp.concatenate([s0_ref[si, h], zpad], axis=0).T

    ti = lax.broadcasted_iota(I32, (rows_per_step, rows_per_step), 0)
    si = lax.broadcasted_iota(I32, (rows_per_step, rows_per_step), 1)
    chunk_shift = c.bit_length() - 1
    same_chunk = (ti >> chunk_shift) == (si >> chunk_shift)
    tril_blocks = jnp.where(same_chunk & (si <= ti), 1.0, 0.0).astype(BF16)
    gain = gain_ref[...]
    for si in range(nseq):
        la_hi, la_lo = _split_bf16(la_ref[si])
        b_all = _dot(tril_blocks, la_hi) + _dot(tril_blocks, la_lo)
        for h in range(H_A):
            hp = slice(h * LANES, (h + 1) * LANES)
            hv = slice(h * DV_A, (h + 1) * DV_A)
            st = st_ref[si * H_A + h]
            for ci in range(rows_per_step // c):
                r0 = ci * c
                q = qa_ref[si, r0:r0 + c, hp] * (DK_A ** -0.5)
                o, st = _gla_chunk(q, ka_ref[si, r0:r0 + c, hp], va_ref[si, r0:r0 + c, hv],
                                   b_all[r0:r0 + c, hp], st)
                r = ra_ref[si, r0:r0 + c, hv]
                o = _rms_norm(o, gain) * (r * _sigmoid(r))
                o_ref[si, r0:r0 + c, hv] = o.astype(o_ref.dtype)
            st_ref[si * H_A + h] = st

    @pl.when(j == nj - 1)
    def _():
        for si in range(nseq):
            for h in range(H_A):
                sfin_ref[si, h] = st_ref[si * H_A + h].T[0:DK_A, :]


def _gla(qa, ka, va, ra, la, s0, gain, rows_per_step):
    b, s, pa = qa.shape
    mw = va.shape[-1]
    ns = GLA_SEQS
    assert b % ns == 0
    seq = lambda w: pl.BlockSpec((ns, rows_per_step, w), lambda i, j: (i, j, 0))
    state = pl.BlockSpec((ns, H_A, DK_A, DV_A), lambda i, j: (i, 0, 0, 0))
    return pl.pallas_call(
        _gla_kernel,
        grid=(b // ns, s // rows_per_step),
        in_specs=[seq(pa), seq(pa), seq(mw), seq(mw), seq(pa), state,
                  pl.BlockSpec(gain.shape, lambda i, j: (0, 0))],
        out_specs=[seq(mw), state],
        out_shape=[jax.ShapeDtypeStruct((b, s, mw), BF16),
                   jax.ShapeDtypeStruct((b, H_A, DK_A, DV_A), F32)],
        scratch_shapes=[pltpu.VMEM((ns * H_A, LANES, LANES), F32)],
        compiler_params=_cparams(("arbitrary", "arbitrary")),
        name="gla",
    )(qa, ka, va, ra, la, s0, gain)


def _head_lane_masks():
    lane = lax.broadcasted_iota(I32, (1, LANES), 1)
    return lane < DH_B, lane >= DH_B


def _sb_neg_tri(tk):
    ji = lax.broadcasted_iota(I32, (tk, tk), 0)
    si = lax.broadcasted_iota(I32, (tk, tk), 1)
    return jnp.where(ji >= si, -1.0, 0.0).astype(BF16)


def _sb_stack_queries(q, qs_ref, base=0):
    m0, m1 = _head_lane_masks()
    for p in range(q.shape[1] // LANES):
        qp = q[:, p * LANES:(p + 1) * LANES] * (DH_B ** -0.5)
        zero = jnp.zeros_like(qp)
        qs_ref[base + p] = jnp.concatenate([jnp.where(m0, qp, zero), jnp.where(m1, qp, zero)], axis=0)


def _pair_lanes(p):
    return slice(p * LANES, (p + 1) * LANES)


def _lane_fit(x, width):
    if width >= LANES:
        return jnp.concatenate([x] * (width // LANES), axis=1)
    return x[:, 0:width]


def _sb_tile_step(qs_ref, acc_ref, carry_ref, k_tile, v_tile, ntri, diagonal):
    npair, rows, _ = qs_ref.shape
    tq = rows // 2
    tk = ntri.shape[1]
    m0, _ = _head_lane_masks()
    if diagonal:
        t = lax.broadcasted_iota(I32, (rows, tk), 0)
        t = jnp.where(t >= tq, t - tq, t)
        visible = lax.broadcasted_iota(I32, (rows, tk), 1) < t
    for p in range(npair):
        z = _dot(qs_ref[p], k_tile(p)) * LOG2_E
        sp = jnp.maximum(z, 0.0) + jnp.log2(1.0 + jnp.exp2(-jnp.abs(z)))
        if diagonal:
            sp = jnp.where(visible, sp, 0.0)
        suffix = _dot(sp.astype(BF16), ntri)
        carry = carry_ref[p]
        w = jnp.exp2(z + suffix + _lane_fit(carry, tk))
        if diagonal:
            w = jnp.where(visible, w, 0.0)
        pv = _dot_nt(w.astype(BF16), v_tile(p))
        acc_ref[p] += jnp.where(m0, pv[0:tq], pv[tq:rows])
        carry_ref[p] = carry + jnp.broadcast_to(suffix[:, 0:1], carry.shape)


def _sb_prompt_kernel(q_ref, k_ref, v_ref, o_ref, qs_ref, acc_ref, carry_ref):
    qi = pl.program_id(1)
    tk = SB_TILE
    _sb_stack_queries(q_ref[0], qs_ref)
    acc_ref[...] = jnp.zeros_like(acc_ref)
    carry_ref[...] = jnp.zeros_like(carry_ref)
    ntri = _sb_neg_tri(tk)

    def step(jb, diagonal):
        _sb_tile_step(qs_ref, acc_ref, carry_ref, lambda p: k_ref[0, jb, _pair_lanes(p), :],
                      lambda p: v_ref[0, jb, _pair_lanes(p), :], ntri, diagonal)

    step(qi, True)

    def body(i, c):
        step(qi - 1 - i, False)
        return c

    lax.fori_loop(0, qi, body, 0)
    for p in range(acc_ref.shape[0]):
        o_ref[0, :, p * LANES:(p + 1) * LANES] = acc_ref[p].astype(o_ref.dtype)


def _sb_scratch(tq, npair):
    return [pltpu.VMEM((npair, 2 * tq, LANES), BF16), pltpu.VMEM((npair, tq, LANES), F32),
            pltpu.VMEM((npair, 2 * tq, LANES), F32)]


def _sb_prompt(q, kt, vt):
    b, s, w = q.shape
    tq = SB_TILE
    assert kt.shape == (b, s // tq, w, tq)
    qspec = pl.BlockSpec((1, tq, w), lambda i, j: (i, j, 0))
    kvspec = pl.BlockSpec((1,) + kt.shape[1:], lambda i, j: (i, 0, 0, 0))
    return pl.pallas_call(
        _sb_prompt_kernel,
        grid=(b, s // tq),
        in_specs=[qspec, kvspec, kvspec],
        out_specs=qspec,
        out_shape=jax.ShapeDtypeStruct((b, s, w), BF16),
        scratch_shapes=_sb_scratch(tq, w // LANES),
        compiler_params=_cparams(("arbitrary", "arbitrary")),
        name="sb_prompt",
    )(q, kt, vt)


def _sb_sample_kernel(q_ref, kn_ref, vn_ref, kp_ref, vp_ref, o_ref, qs_ref, acc_ref, carry_ref):
    nseq, sq, w = q_ref.shape
    past = kp_ref.shape[3]
    npair = w // LANES
    tk = SB_TILE
    for si in range(nseq):
        _sb_stack_queries(q_ref[si], qs_ref, si * npair)
    acc_ref[...] = jnp.zeros_like(acc_ref)
    carry_ref[...] = jnp.zeros_like(carry_ref)
    _sb_tile_step(qs_ref, acc_ref, carry_ref, lambda e: kn_ref[e // npair, 0, _pair_lanes(e % npair), :],
                  lambda e: vn_ref[e // npair, 0, _pair_lanes(e % npair), :], _sb_neg_tri(sq), True)
    ntri = _sb_neg_tri(tk)

    def body(i, c):
        cols = pl.ds(pl.multiple_of(past - (i + 1) * tk, tk), tk)

        def pair(ref, e):
            p = e % npair
            return ref[e // npair, 2 * p:2 * p + 2, :, cols].reshape(LANES, tk).astype(BF16)

        _sb_tile_step(qs_ref, acc_ref, carry_ref, lambda e: pair(kp_ref, e), lambda e: pair(vp_ref, e), ntri, False)
        return c

    lax.fori_loop(0, past // tk, body, 0)
    for e in range(acc_ref.shape[0]):
        o_ref[e // npair, :, _pair_lanes(e % npair)] = acc_ref[e].astype(o_ref.dtype)


def _sb_sample(q, kt_new, vt_new, kt_past, vt_past):
    b, sq, w = q.shape
    past = kt_past.shape[3]
    ns = SB_SAMPLE_SEQS
    assert past % SB_TILE == 0 and 2 * DH_B == LANES and b % ns == 0
    qspec = pl.BlockSpec((ns, sq, w), lambda i: (i, 0, 0))
    new = pl.BlockSpec((ns, 1, w, sq), lambda i: (i, 0, 0, 0))
    old = pl.BlockSpec((ns, H_B, DH_B, past), lambda i: (i, 0, 0, 0))
    return pl.pallas_call(
        _sb_sample_kernel,
        grid=(b // ns,),
        in_specs=[qspec, new, new, old, old],
        out_specs=qspec,
        out_shape=jax.ShapeDtypeStruct((b, sq, w), BF16),
        scratch_shapes=_sb_scratch(sq, ns * (w // LANES)),
        compiler_params=_cparams(("arbitrary",)),
        name="sb_sample",
    )(q, kt_new, vt_new, kt_past, vt_past)


def _first_argmax(vals, nrows):
    idx = lax.broadcasted_iota(I32, vals.shape, 0)
    top = jnp.max(vals, axis=0, keepdims=True)
    first = jnp.min(jnp.where(vals == top, idx, nrows), axis=0, keepdims=True)
    return top, first, idx


def _merge_kernel(oa_ref, ob_ref, g_ref, x_ref, wb0_ref, wb1_ref, wo_ref, gain_ref, wr_ref, br_ref,
                  x1_ref, h2_ref, eid_ref, wcol_ref):
    d = x_ref.shape[1]
    ya = _dot(oa_ref[...], wb0_ref[...])
    yb = _dot(ob_ref[...], wb1_ref[...])
    g = g_ref[...].astype(F32)
    m = _sigmoid(g[:, 0:d]) * ya + _sigmoid(g[:, d:2 * d]) * yb
    x1 = x_ref[...] + _dot(m.astype(BF16), wo_ref[...])
    x1_ref[...] = x1
    h2 = _rms_norm(x1, gain_ref[...])
    h2_ref[...] = h2.reshape(h2_ref.shape)

    h_hi, h_lo = _split_bf16(h2)
    w_hi, w_lo = _split_bf16(wr_ref[...])
    lt = _dot_nt(w_hi, h_hi) + _dot_nt(w_hi, h_lo) + _dot_nt(w_lo, h_hi) + br_ref[:, 0:1]
    gl = lt[0:N_GROUPS, :]
    g_top, g_idx, _ = _first_argmax(gl, N_GROUPS)
    g_e = jnp.exp(gl - g_top)
    g_p = jnp.max(g_e / jnp.sum(g_e, axis=0, keepdims=True), axis=0, keepdims=True)
    el = jnp.zeros((EXPERTS_PER_GROUP, lt.shape[1]), F32)
    for g in range(N_GROUPS):
        r0 = 8 + g * EXPERTS_PER_GROUP
        el = jnp.where(g_idx == g, lt[r0:r0 + EXPERTS_PER_GROUP, :], el)
    e_top, i1, eidx = _first_argmax(el, EXPERTS_PER_GROUP)
    e_e = jnp.exp(el - e_top)
    e_p = e_e / jnp.sum(e_e, axis=0, keepdims=True)
    p1 = jnp.max(e_p, axis=0, keepdims=True)
    rest = jnp.where(eidx == i1, -1.0, e_p)
    p2, i2, _ = _first_argmax(rest, EXPERTS_PER_GROUP)
    norm = p1 + p2
    w1 = g_p * (p1 / norm)
    w2 = g_p * (p2 / norm)
    eid_ref[...] = jnp.concatenate([g_idx * EXPERTS_PER_GROUP + i1, g_idx * EXPERTS_PER_GROUP + i2], axis=0)
    rows = lax.broadcasted_iota(I32, (LANES, lt.shape[1]), 0)
    wrows = jnp.where(rows == 0, w1, jnp.where(rows == 1, w2, 0.0))
    wcol_ref[...] = wrows.T


def _merge(oa, ob, gbr, x, wb0, wb1, wo, gain, wr, br):
    t, d = x.shape
    tm = MERGE_TILE
    assert t % tm == 0
    row = lambda w: pl.BlockSpec((tm, w), lambda i: (i, 0))
    full = lambda a: pl.BlockSpec(a.shape, lambda i: (0,) * a.ndim)
    return pl.pallas_call(
        _merge_kernel,
        grid=(t // tm,),
        in_specs=[row(oa.shape[1]), row(ob.shape[1]), row(gbr.shape[1]), row(d),
                  full(wb0), full(wb1), full(wo), full(gain), full(wr), full(br)],
        out_specs=[row(d), pl.BlockSpec((tm, d // LANES, LANES), lambda i: (i, 0, 0)),
                   pl.BlockSpec((TOP_K, tm), lambda i: (0, i)), row(LANES)],
        out_shape=[jax.ShapeDtypeStruct((t, d), F32), jax.ShapeDtypeStruct((t, d // LANES, LANES), F32),
                   jax.ShapeDtypeStruct((TOP_K, t), I32), jax.ShapeDtypeStruct((t, LANES), F32)],
        compiler_params=_cparams(("arbitrary",)),
        name="merge_router",
    )(oa, ob, gbr, x, wb0, wb1, wo, gain, wr, br)


def _positions_kernel(eid_ref, dest_ref, counts_ref, rank_ref):
    nblk, width = eid_ref.shape
    ji = lax.broadcasted_iota(I32, (width, width), 0)
    si = lax.broadcasted_iota(I32, (width, width), 1)
    prefix = jnp.where(ji <= si, 1.0, 0.0).astype(BF16)
    expert = lax.broadcasted_iota(I32, (N_EXPERTS, width), 0)
    group = max(g for g in (8, 4, 2, 1) if nblk % g == 0)

    def onehot(i):
        return expert == eid_ref[pl.ds(i, 1), :]

    def rank_body(ig, run):
        first = pl.multiple_of(ig * group, group)
        ohs = [onehot(first + j) for j in range(group)]
        stacked = jnp.concatenate([jnp.where(oh, 1.0, 0.0) for oh in ohs], axis=0).astype(BF16)
        cum = _dot(stacked, prefix)
        ranks = []
        for j, oh in enumerate(ohs):
            cum_j = cum[j * N_EXPERTS:(j + 1) * N_EXPERTS, :] + run
            ranks.append(jnp.sum(jnp.where(oh, cum_j, 0.0), axis=0, keepdims=True) - 1.0)
            run = cum_j[:, width - 1:width]
        rank_ref[pl.ds(first, group), :] = jnp.concatenate(ranks, axis=0)
        return run

    counts = lax.fori_loop(0, nblk // group, rank_body, jnp.zeros((N_EXPERTS, 1), F32))
    counts_ref[...] = jnp.broadcast_to(counts, counts_ref.shape).astype(I32)
    c_hi = jnp.floor(counts * (1.0 / 256.0))
    c_lo = counts - 256.0 * c_hi
    ei = lax.broadcasted_iota(I32, (N_EXPERTS, N_EXPERTS), 0)
    ej = lax.broadcasted_iota(I32, (N_EXPERTS, N_EXPERTS), 1)
    strict = jnp.where(ej < ei, 1.0, 0.0).astype(BF16)
    digits = jnp.concatenate([jnp.broadcast_to(c_hi, (N_EXPERTS, LANES)),
                              jnp.broadcast_to(c_lo, (N_EXPERTS, LANES))], axis=1).astype(BF16)
    sums = _dot(strict, digits)
    start = 256.0 * sums[:, 0:1] + sums[:, LANES:LANES + 1]

    def dest_body(ig, carry):
        first = pl.multiple_of(ig * group, group)
        offs = [jnp.sum(jnp.where(onehot(first + j), start, 0.0), axis=0, keepdims=True) for j in range(group)]
        rows = pl.ds(first, group)
        dest_ref[rows, :] = (rank_ref[rows, :] + jnp.concatenate(offs, axis=0)).astype(I32)
        return carry

    lax.fori_loop(0, nblk // group, dest_body, 0)


def _positions(eid_blocks):
    nblk, width = eid_blocks.shape
    vm = lambda shape: pl.BlockSpec(shape, lambda: (0,) * len(shape))
    return pl.pallas_call(
        _positions_kernel,
        in_specs=[vm((nblk, width))],
        out_specs=[vm((nblk, width)), vm((N_EXPERTS, LANES))],
        out_shape=[jax.ShapeDtypeStruct((nblk, width), I32), jax.ShapeDtypeStruct((N_EXPERTS, LANES), I32)],
        scratch_shapes=[pltpu.VMEM((nblk, width), F32)],
        name="positions",
    )(eid_blocks)


def _dispatch_kernel(n_prompt_tiles, dest_ref, hp_ref, hs_ref, xs_ref, sem):
    i = pl.program_id(0)
    tm = dest_ref.shape[1]

    def scatter(src_ref):
        def start(r, c):
            for k in range(TOP_K):
                pltpu.make_async_copy(src_ref.at[r], xs_ref.at[dest_ref[k, r]], sem).start(priority=k)
            return c

        lax.fori_loop(0, tm, start, 0, unroll=DMA_UNROLL)
        for k in range(TOP_K):
            pltpu.make_async_copy(src_ref, xs_ref.at[pl.ds(0, tm)], sem).wait()

    @pl.when(i < n_prompt_tiles)
    def _():
        scatter(hp_ref)

    @pl.when(i >= n_prompt_tiles)
    def _():
        scatter(hs_ref)


def _dispatch(dest, h_prompt, h_sample):
    t = dest.shape[1]
    slab = h_prompt.shape[1:]
    tm = DISPATCH_TILE
    assert h_prompt.shape[0] % tm == 0 and h_sample.shape[0] % tm == 0
    npt = h_prompt.shape[0] // tm
    return pl.pallas_call(
        functools.partial(_dispatch_kernel, npt),
        grid=(t // tm,),
        in_specs=[pl.BlockSpec((TOP_K, tm), lambda i: (0, i), memory_space=pltpu.SMEM),
                  pl.BlockSpec((tm,) + slab, lambda i: (jnp.minimum(i, npt - 1), 0, 0)),
                  pl.BlockSpec((tm,) + slab, lambda i: (jnp.maximum(i - npt, 0), 0, 0))],
        out_specs=pl.BlockSpec(memory_space=pl.ANY),
        out_shape=jax.ShapeDtypeStruct((TOP_K * t,) + slab, F32),
        scratch_shapes=[pltpu.SemaphoreType.DMA(())],
        compiler_params=_cparams(("arbitrary",)),
        name="dispatch",
    )(dest, h_prompt, h_sample)


def _experts_kernel(vblk_ref, vexp_ref, vlo_ref, vhi_ref, xs_ref, wg_ref, wu_ref, wd_ref, ys_ref,
                    wg16_ref, wu16_ref, wd16_ref):
    v = pl.program_id(0)
    lo = vlo_ref[v]
    hi = vhi_ref[v]
    prev = jnp.maximum(v - 1, 0)
    first = jnp.logical_or(v == 0, vblk_ref[v] != vblk_ref[prev])
    new_expert = jnp.logical_or(v == 0, vexp_ref[v] != vexp_ref[prev])

    @pl.when(new_expert)
    def _():
        wg16_ref[...] = wg_ref[0].astype(BF16)
        wu16_ref[...] = wu_ref[0].astype(BF16)
        wd16_ref[...] = wd_ref[0].astype(BF16)

    @pl.when(first)
    def _():
        ys_ref[...] = jnp.zeros_like(ys_ref)

    @pl.when(hi > lo)
    def _():
        tm, nsl, _ = xs_ref.shape
        cs = MXU_DEPTH // LANES
        ck = MXU_DEPTH
        gate = None
        up = None
        for c in range(nsl // cs):
            xc = xs_ref[:, c * cs:(c + 1) * cs, :].reshape(tm, ck).astype(BF16)
            g = _dot(xc, wg16_ref[c * ck:(c + 1) * ck, :])
            u = _dot(xc, wu16_ref[c * ck:(c + 1) * ck, :])
            gate = g if gate is None else gate + g
            up = u if up is None else up + u
        hid = (gate * _sigmoid(gate) * up).astype(BF16)
        rows = lax.broadcasted_iota(I32, (tm, ck), 0)
        mine = (rows >= lo) & (rows < hi)
        for c in range(nsl // cs):
            yc = _dot(hid, wd16_ref[:, c * ck:(c + 1) * ck])
            out_c = ys_ref.at[:, c * cs:(c + 1) * cs, :]
            out_c[...] = jnp.where(mine, yc, out_c[...].reshape(tm, ck)).reshape(out_c.shape)


def _experts(vblk, vexp, vlo, vhi, xs, wg, wu, wd):
    a = xs.shape[0]
    slab = xs.shape[1:]
    d, de = wg.shape[1:]
    tm = MOE_TILE
    grid_spec = pltpu.PrefetchScalarGridSpec(
        num_scalar_prefetch=4,
        grid=(vblk.shape[0],),
        in_specs=[pl.BlockSpec((tm,) + slab, lambda v, b, e, lo, hi: (b[v], 0, 0)),
                  pl.BlockSpec((1, d, de), lambda v, b, e, lo, hi: (e[v], 0, 0)),
                  pl.BlockSpec((1, d, de), lambda v, b, e, lo, hi: (e[v], 0, 0)),
                  pl.BlockSpec((1, de, d), lambda v, b, e, lo, hi: (e[v], 0, 0))],
        out_specs=pl.BlockSpec((tm,) + slab, lambda v, b, e, lo, hi: (b[v], 0, 0)),
        scratch_shapes=[pltpu.VMEM((d, de), BF16), pltpu.VMEM((d, de), BF16), pltpu.VMEM((de, d), BF16)],
    )
    return pl.pallas_call(
        _experts_kernel,
        grid_spec=grid_spec,
        out_shape=jax.ShapeDtypeStruct((a,) + slab, F32),
        compiler_params=_cparams(("arbitrary",)),
        name="experts",
    )(vblk, vexp, vlo, vhi, xs, wg, wu, wd)


def _visit_plan(counts, n_rows):
    tm = MOE_TILE
    nblk = n_rows // tm
    n_visits = nblk + N_EXPERTS - 1
    ends = jnp.cumsum(counts)
    starts = ends - counts
    first_blk = starts // tm
    nvis = jnp.where(counts > 0, (ends + tm - 1) // tm - first_blk, 0)
    vis_end = jnp.cumsum(nvis)
    vis_start = vis_end - nvis
    v = jnp.arange(n_visits, dtype=I32)
    e = jnp.minimum(jnp.sum((vis_end[None, :] <= v[:, None]).astype(I32), axis=1), N_EXPERTS - 1)
    valid = v < vis_end[-1]
    blk = first_blk[e] + (v - vis_start[e])
    lo = jnp.clip(starts[e] - blk * tm, 0, tm)
    hi = jnp.clip(ends[e] - blk * tm, 0, tm)
    last_e = jnp.max(jnp.where(counts > 0, jnp.arange(N_EXPERTS, dtype=I32), 0))
    blk = jnp.where(valid, blk, nblk - 1).astype(I32)
    e = jnp.where(valid, e, last_e).astype(I32)
    lo = jnp.where(valid, lo, 0).astype(I32)
    hi = jnp.where(valid, hi, 0).astype(I32)
    return blk, e, lo, hi


def _combine_kernel(dest_ref, ys_ref, x1_ref, wcol_ref, gain_ref, out_ref, buf_ref, sems):
    tm, d = x1_ref.shape
    part = tm // COMBINE_PARTS

    def start(r, c, sem):
        for k in range(TOP_K):
            pltpu.make_async_copy(ys_ref.at[dest_ref[k, r]], buf_ref.at[k, r], sem).start(priority=k)
        return c

    for h in range(COMBINE_PARTS):
        lax.fori_loop(h * part, (h + 1) * part, functools.partial(start, sem=sems.at[h]), 0, unroll=DMA_UNROLL)
    for h in range(COMBINE_PARTS):
        rows = pl.ds(h * part, part)
        for k in range(TOP_K):
            pltpu.make_async_copy(ys_ref.at[rows], buf_ref.at[k, rows], sems.at[h]).wait()
        y = (wcol_ref[rows, 0:1] * buf_ref[0, rows].reshape(part, d)
             + wcol_ref[rows, 1:2] * buf_ref[1, rows].reshape(part, d))
        out_ref[rows, :] = _rms_norm(x1_ref[rows, :] + y, gain_ref[...])


def _combine(dest, ys, x1, wcol, gain):
    t, d = x1.shape
    tm = ROW_TILE
    row = lambda w: pl.BlockSpec((tm, w), lambda i: (i, 0))
    return pl.pallas_call(
        _combine_kernel,
        grid=(t // tm,),
        in_specs=[pl.BlockSpec((TOP_K, tm), lambda i: (0, i), memory_space=pltpu.SMEM),
                  pl.BlockSpec(memory_space=pl.ANY), row(d), row(LANES),
                  pl.BlockSpec(gain.shape, lambda i: (0, 0))],
        out_specs=row(d),
        out_shape=jax.ShapeDtypeStruct((t, d), F32),
        scratch_shapes=[pltpu.VMEM((TOP_K, tm) + ys.shape[1:], F32), pltpu.SemaphoreType.DMA((COMBINE_PARTS,))],
        compiler_params=_cparams(("arbitrary",)),
        name="combine",
    )(dest, ys, x1, wcol, gain)


def _pad_heads(w):
    r = w.shape[0]
    return jnp.pad(w.reshape(r, H_A, DK_A), ((0, 0), (0, 0), (0, LANES - DK_A))).reshape(r, H_A * LANES)


def _prepare_weights(w_in, w_gla_gate_up, b_gla_gate, w_branch, w_out, w_router_group, b_router_group,
                     w_router_expert, b_router_expert):
    d = w_in.shape[0]
    qk = H_A * DK_A
    mw = H_A * DV_A
    c = 0
    w_qa, c = w_in[:, c:c + qk], c + qk
    w_ka, c = w_in[:, c:c + qk], c + qk
    w_va, c = w_in[:, c:c + mw], c + mw
    w_ra, c = w_in[:, c:c + mw], c + mw
    w_lr, c = w_in[:, c:c + GATE_RANK], c + GATE_RANK
    w_b, c = w_in[:, c:c + 3 * mw], c + 3 * mw
    w_g = w_in[:, c:]
    wa = jnp.concatenate([_pad_heads(w_qa), _pad_heads(w_ka), w_va, w_ra,
                          jnp.pad(w_lr, ((0, 0), (0, LANES - GATE_RANK)))], axis=1).astype(BF16)
    wgu = jnp.pad(_pad_heads(w_gla_gate_up), ((0, LANES - GATE_RANK), (0, 0))).astype(BF16)
    bgu = _pad_heads(b_gla_gate[None, :])
    wr = jnp.zeros((LANES, d), F32)
    wr = wr.at[0:N_GROUPS].set(w_router_group.T).at[8:8 + N_EXPERTS].set(w_router_expert.T)
    br = jnp.zeros((LANES,), F32).at[0:N_GROUPS].set(b_router_group).at[8:8 + N_EXPERTS].set(b_router_expert)
    br = jnp.broadcast_to(br[:, None], (LANES, LANES))
    return dict(wa=wa, wqb=w_b[:, 0:mw].astype(BF16), wkvt=w_b[:, mw:3 * mw].T.astype(BF16),
                wg=w_g.astype(BF16), wgu=wgu, bgu=bgu,
                wb0=w_branch[0].astype(BF16), wb1=w_branch[1].astype(BF16), wo=w_out.astype(BF16),
                wr=wr, br=br)


def _mixers(x, s0, k_past, v_past, w, norm_mix_gain, gla_norm_gain, norm_ffn_gain):
    b, s, d = x.shape
    xf = x.reshape(b * s, d)
    qa, ka, va, ra, la, qb, kt, vt, kt16, vt16, gbr = _in_projection(
        xf, s, norm_mix_gain[None, :], w["wa"], w["wqb"], w["wkvt"], w["wg"], w["wgu"], w["bgu"])
    seq = lambda a: a.reshape(b, s, a.shape[-1])
    oa, s_new = _gla(seq(qa), seq(ka), seq(va), seq(ra), seq(la), s0, gla_norm_gain[None, :],
                     min(s, ROW_TILE))
    to_channel_major = lambda a: jnp.transpose(a, (0, 2, 3, 1))
    if k_past is None:
        ob = _sb_prompt(seq(qb), kt16, vt16)
    else:
        ob = _sb_sample(seq(qb), kt16, vt16, to_channel_major(k_past), to_channel_major(v_past))
    x1, h2, eid, wcol = _merge(oa.reshape(b * s, -1), ob.reshape(b * s, -1), gbr, xf, w["wb0"], w["wb1"],
                               w["wo"], norm_ffn_gain[None, :], w["wr"], w["br"])
    from_channel_major = lambda a: jnp.transpose(a.reshape(b, H_B, DH_B, s), (0, 3, 1, 2))
    return x1, h2, eid, wcol, s_new, from_channel_major(kt), from_channel_major(vt)


def kernel(x_prompt, x_sample, state_gla, cache_sb_k, cache_sb_v, norm_mix_gain, w_in, w_gla_gate_up, b_gla_gate, gla_norm_gain, w_branch, w_out, norm_ffn_gain, w_router_group, b_router_group, w_router_expert, b_router_expert, w_exp_gate, w_exp_up, w_exp_down, norm_final_gain):
    depth = w_in.shape[0]
    assert depth == 1, "one trunk layer per step"
    l = 0
    w = _prepare_weights(w_in[l], w_gla_gate_up[l], b_gla_gate[l], w_branch[l], w_out[l], w_router_group[l],
                         b_router_group[l], w_router_expert[l], b_router_expert[l])
    bp, sp, d = x_prompt.shape
    bs, ss, _ = x_sample.shape
    s0 = jnp.zeros((bp, H_A, DK_A, DV_A), x_prompt.dtype)
    x1p, h2p, eidp, wcolp, gla_p, k_p, v_p = _mixers(
        x_prompt, s0, None, None, w, norm_mix_gain[l], gla_norm_gain[l], norm_ffn_gain[l])
    x1s, h2s, eids, wcols, gla_s, k_s, v_s = _mixers(
        x_sample, state_gla[l], cache_sb_k[l], cache_sb_v[l], w, norm_mix_gain[l], gla_norm_gain[l],
        norm_ffn_gain[l])

    tp, ts = bp * sp, bs * ss
    eid = jnp.concatenate([eidp, eids], axis=1)
    dest_blocks, counts = _positions(eid.reshape(-1, SORT_WIDTH))
    dest = dest_blocks.reshape(TOP_K, tp + ts)
    xs = _dispatch(dest, h2p, h2s)
    vblk, vexp, vlo, vhi = _visit_plan(counts[:, 0], TOP_K * (tp + ts))
    ys = _experts(vblk, vexp, vlo, vhi, xs, w_exp_gate[l], w_exp_up[l], w_exp_down[l])
    gf = norm_final_gain[None, :]
    y_prompt = _combine(dest[:, :tp], ys, x1p, wcolp, gf).reshape(bp, sp, d)
    y_sample = _combine(dest[:, tp:], ys, x1s, wcols, gf).reshape(bs, ss, d)
    return (y_prompt, y_sample, gla_p[None], k_p[None], v_p[None], gla_s[None], k_s[None], v_s[None])
```

```python
import functools

import jax
import jax.numpy as jnp
from jax import lax
from jax.experimental import pallas as pl
from jax.experimental.pallas import tpu as pltpu

F32 = jnp.float32
BF16 = jnp.bfloat16
I32 = jnp.int32

LANES = 128
LOG2_E = 1.4426950408889634
RMS_EPS = 1e-6
GATE_TAU = 16.0
H_A = 4
DK_A = 64
DV_A = 128
GATE_RANK = 16
H_B = 8
DH_B = 64
N_GROUPS = 4
EXPERTS_PER_GROUP = 8
N_EXPERTS = N_GROUPS * EXPERTS_PER_GROUP
TOP_K = 2
GLA_CHUNK = 64
GLA_SUB = 16
GLA_EXP_CLAMP = 80.0
GLA_SEQS = 2
ROW_TILE = 256
SB_TILE = 256
SB_SAMPLE_SEQS = 2
MOE_TILE = 512
SORT_WIDTH = 256
MERGE_TILE = 512
DISPATCH_TILE = 512
COMBINE_PARTS = 4
DMA_UNROLL = 8
VMEM_LIMIT = 56 * 1024 * 1024


def _cparams(sem):
    return pltpu.CompilerParams(dimension_semantics=sem, vmem_limit_bytes=VMEM_LIMIT)


def _dot(a, b):
    return jnp.dot(a, b, preferred_element_type=F32)


def _dot_nt(a, b):
    return lax.dot_general(a, b, (((1,), (1,)), ((), ())), preferred_element_type=F32)


def _dot_tn(a, b):
    return lax.dot_general(a, b, (((0,), (0,)), ((), ())), preferred_element_type=F32)


def _split_bf16(x):
    hi = x.astype(BF16)
    lo = (x - hi.astype(F32)).astype(BF16)
    return hi, lo


def _log_sigmoid(x):
    return jnp.minimum(x, 0.0) - jnp.log(1.0 + jnp.exp(-jnp.abs(x)))


def _sigmoid(x):
    return 1.0 / (1.0 + jnp.exp(-x))


def _rms_norm(x, gain):
    return x * lax.rsqrt(jnp.mean(x * x, axis=-1, keepdims=True) + RMS_EPS) * gain


def _inproj_kernel(x_ref, gain_ref, wa_ref, wqb_ref, wkvt_ref, wg_ref, wgu_ref, bgu_ref,
                   qa_ref, ka_ref, va_ref, ra_ref, la_ref, qb_ref, kt_ref, vt_ref,
                   kt16_ref, vt16_ref, gbr_ref):
    h = _rms_norm(x_ref[...], gain_ref[...]).astype(BF16)
    pa = H_A * LANES
    mw = va_ref.shape[-1]
    kvt = _dot_nt(wkvt_ref[...], h)
    nseq, _, s = kt_ref.shape
    for i in range(nseq):
        cols = slice(i * s, (i + 1) * s)
        kt_ref[i] = kvt[0:mw, cols]
        vt_ref[i] = kvt[mw:2 * mw, cols]
        kt16_ref[i, 0] = kvt[0:mw, cols].astype(BF16)
        vt16_ref[i, 0] = kvt[mw:2 * mw, cols].astype(BF16)
    qb_ref[...] = _dot(h, wqb_ref[...]).astype(BF16)
    qa_ref[...] = _dot(h, wa_ref[:, 0:pa])
    ka_ref[...] = _dot(h, wa_ref[:, pa:2 * pa])
    va_ref[...] = _dot(h, wa_ref[:, 2 * pa:2 * pa + mw])
    ra_ref[...] = _dot(h, wa_ref[:, 2 * pa + mw:2 * pa + 2 * mw])
    lr = _dot(h, wa_ref[:, 2 * pa + 2 * mw:2 * pa + 2 * mw + LANES])
    gl = _dot(lr.astype(BF16), wgu_ref[...]) + bgu_ref[...]
    la_ref[...] = _log_sigmoid(gl) / GATE_TAU
    gbr_ref[...] = _dot(h, wg_ref[...]).astype(gbr_ref.dtype)


def _in_projection(x, seq_len, gain, wa, wqb, wkvt, wg, wgu, bgu):
    t, d = x.shape
    nb = t // seq_len
    pa = H_A * LANES
    mw = wqb.shape[1]
    tm = ROW_TILE
    row = lambda w: pl.BlockSpec((tm, w), lambda i: (i, 0))
    full = lambda a: pl.BlockSpec(a.shape, lambda i: (0,) * a.ndim)
    if seq_len >= tm:
        per_seq = seq_len // tm
        assert tm == SB_TILE and seq_len % tm == 0
        kt_spec = pl.BlockSpec((1, mw, tm), lambda i: (i // per_seq, 0, i % per_seq))
        kt16_spec = pl.BlockSpec((1, 1, mw, tm), lambda i: (i // per_seq, i % per_seq, 0, 0))
        kt16_shape = (nb, per_seq, mw, tm)
    else:
        nseq = tm // seq_len
        assert tm % seq_len == 0
        kt_spec = pl.BlockSpec((nseq, mw, seq_len), lambda i: (i, 0, 0))
        kt16_spec = pl.BlockSpec((nseq, 1, mw, seq_len), lambda i: (i, 0, 0, 0))
        kt16_shape = (nb, 1, mw, seq_len)
    outs = [
        (jax.ShapeDtypeStruct((t, pa), F32), row(pa)), (jax.ShapeDtypeStruct((t, pa), F32), row(pa)),
        (jax.ShapeDtypeStruct((t, mw), F32), row(mw)), (jax.ShapeDtypeStruct((t, mw), F32), row(mw)),
        (jax.ShapeDtypeStruct((t, pa), F32), row(pa)),
        (jax.ShapeDtypeStruct((t, mw), BF16), row(mw)),
        (jax.ShapeDtypeStruct((nb, mw, seq_len), F32), kt_spec), (jax.ShapeDtypeStruct((nb, mw, seq_len), F32), kt_spec),
        (jax.ShapeDtypeStruct(kt16_shape, BF16), kt16_spec), (jax.ShapeDtypeStruct(kt16_shape, BF16), kt16_spec),
        (jax.ShapeDtypeStruct((t, wg.shape[1]), BF16), row(wg.shape[1])),
    ]
    return pl.pallas_call(
        _inproj_kernel,
        grid=(t // tm,),
        in_specs=[row(d), full(gain), full(wa), full(wqb), full(wkvt), full(wg), full(wgu), full(bgu)],
        out_specs=[spec for _, spec in outs],
        out_shape=[shape for shape, _ in outs],
        compiler_params=_cparams(("arbitrary",)),
        name="in_projection",
    )(x, gain, wa, wqb, wkvt, wg, wgu, bgu)


def _gla_chunk(q, k, v, b, st):
    c = q.shape[0]
    b_last = b[c - 1:c, :]
    rows = lax.broadcasted_iota(I32, (c, LANES), 0)
    nsub = c // GLA_SUB
    refs = [jnp.zeros((1, LANES), F32)] + [b[i * GLA_SUB - 1:i * GLA_SUB, :] for i in range(1, nsub)]
    ref_rows = refs[0]
    for i in range(1, nsub):
        ref_rows = jnp.where(rows >= i * GLA_SUB, refs[i], ref_rows)
    q_rel = q * jnp.exp(b - ref_rows)
    lhs = jnp.concatenate(
        [jnp.where((rows >= i * GLA_SUB) & (rows < (i + 1) * GLA_SUB), q_rel, 0.0) for i in range(nsub)],
        axis=1).astype(BF16)
    rhs = jnp.concatenate(
        [jnp.where(rows < (i + 1) * GLA_SUB, k * jnp.exp(jnp.minimum(refs[i] - b, GLA_EXP_CLAMP)), 0.0)
         for i in range(nsub)], axis=1).astype(BF16)
    att = _dot_nt(lhs, rhs)
    tt = lax.broadcasted_iota(I32, (c, c), 0)
    ss = lax.broadcasted_iota(I32, (c, c), 1)
    att = jnp.where(ss <= tt, att, 0.0)
    v16 = v.astype(BF16)
    inter = _dot_nt((q * jnp.exp(b)).astype(BF16), st.astype(BF16))
    intra = _dot(att.astype(BF16), v16)
    kd = (k * jnp.exp(b_last - b)).astype(BF16)
    st_new = st * jnp.exp(b_last) + _dot_tn(v16, kd)
    return inter + intra, st_new


def _gla_kernel(qa_ref, ka_ref, va_ref, ra_ref, la_ref, s0_ref, gain_ref, o_ref, sfin_ref, st_ref):
    j = pl.program_id(1)
    nj = pl.num_programs(1)
    nseq, rows_per_step, _ = qa_ref.shape
    c = GLA_CHUNK
    zpad = jnp.zeros((LANES - DK_A, DV_A), F32)

    @pl.when(j == 0)
    def _():
        for si in range(nseq):
            for h in range(H_A):
                st_ref[si * H_A + h] = jnp.concatenate([s0_ref[si, h], zpad], axis=0).T

    ti = lax.broadcasted_iota(I32, (rows_per_step, rows_per_step), 0)
    si = lax.broadcasted_iota(I32, (rows_per_step, rows_per_step), 1)
    chunk_shift = c.bit_length() - 1
    same_chunk = (ti >> chunk_shift) == (si >> chunk_shift)
    tril_blocks = jnp.where(same_chunk & (si <= ti), 1.0, 0.0).astype(BF16)
    gain = gain_ref[...]
    for si in range(nseq):
        la_hi, la_lo = _split_bf16(la_ref[si])
        b_all = _dot(tril_blocks, la_hi) + _dot(tril_blocks, la_lo)
        for h in range(H_A):
            hp = slice(h * LANES, (h + 1) * LANES)
            hv = slice(h * DV_A, (h + 1) * DV_A)
            st = st_ref[si * H_A + h]
            for ci in range(rows_per_step // c):
                r0 = ci * c
                q = qa_ref[si, r0:r0 + c, hp] * (DK_A ** -0.5)
                o, st = _gla_chunk(q, ka_ref[si, r0:r0 + c, hp], va_ref[si, r0:r0 + c, hv],
                                   b_all[r0:r0 + c, hp], st)
                r = ra_ref[si, r0:r0 + c, hv]
                o = _rms_norm(o, gain) * (r * _sigmoid(r))
                o_ref[si, r0:r0 + c, hv] = o.astype(o_ref.dtype)
            st_ref[si * H_A + h] = st

    @pl.when(j == nj - 1)
    def _():
        for si in range(nseq):
            for h in range(H_A):
                sfin_ref[si, h] = st_ref[si * H_A + h].T[0:DK_A, :]


def _gla(qa, ka, va, ra, la, s0, gain, rows_per_step):
    b, s, pa = qa.shape
    mw = va.shape[-1]
    ns = GLA_SEQS
    assert b % ns == 0
    seq = lambda w: pl.BlockSpec((ns, rows_per_step, w), lambda i, j: (i, j, 0))
    state = pl.BlockSpec((ns, H_A, DK_A, DV_A), lambda i, j: (i, 0, 0, 0))
    return pl.pallas_call(
        _gla_kernel,
        grid=(b // ns, s // rows_per_step),
        in_specs=[seq(pa), seq(pa), seq(mw), seq(mw), seq(pa), state,
                  pl.BlockSpec(gain.shape, lambda i, j: (0, 0))],
        out_specs=[seq(mw), state],
        out_shape=[jax.ShapeDtypeStruct((b, s, mw), BF16),
                   jax.ShapeDtypeStruct((b, H_A, DK_A, DV_A), F32)],
        scratch_shapes=[pltpu.VMEM((ns * H_A, LANES, LANES), F32)],
        compiler_params=_cparams(("arbitrary", "arbitrary")),
        name="gla",
    )(qa, ka, va, ra, la, s0, gain)


def _head_lane_masks():
    lane = lax.broadcasted_iota(I32, (1, LANES), 1)
    return lane < DH_B, lane >= DH_B


def _sb_neg_tri(tk):
    ji = lax.broadcasted_iota(I32, (tk, tk), 0)
    si = lax.broadcasted_iota(I32, (tk, tk), 1)
    return jnp.where(ji >= si, -1.0, 0.0).astype(BF16)


def _sb_stack_queries(q, qs_ref, base=0):
    m0, m1 = _head_lane_masks()
    for p in range(q.shape[1] // LANES):
        qp = (q[:, p * LANES:(p + 1) * LANES].astype(F32) * (DH_B ** -0.5 * LOG2_E)).astype(BF16)
        zero = jnp.zeros_like(qp)
        qs_ref[base + p] = jnp.concatenate([jnp.where(m0, qp, zero), jnp.where(m1, qp, zero)], axis=0)


def _pair_lanes(p):
    return slice(p * LANES, (p + 1) * LANES)


def _lane_fit(x, width):
    if width >= LANES:
        return jnp.concatenate([x] * (width // LANES), axis=1)
    return x[:, 0:width]


def _sb_tile_step(qs_ref, acc_ref, carry_ref, k_tile, v_tile, ntri, diagonal, one_suffix_matmul=False):
    npair, rows, _ = qs_ref.shape
    tq = rows // 2
    tk = ntri.shape[1]
    m0, _ = _head_lane_masks()
    if diagonal:
        t = lax.broadcasted_iota(I32, (rows, tk), 0)
        t = jnp.where(t >= tq, t - tq, t)
        visible = lax.broadcasted_iota(I32, (rows, tk), 1) < t
    def scores(p):
        z = _dot(qs_ref[p], k_tile(p))
        sp = jnp.maximum(z, 0.0) + jnp.log2(1.0 + jnp.exp2(-jnp.abs(z)))
        if diagonal:
            sp = jnp.where(visible, sp, 0.0)
        return z, sp.astype(BF16)

    if one_suffix_matmul:
        zs, sps = zip(*[scores(p) for p in range(npair)])
        stacked = _dot(jnp.concatenate(sps, axis=0), ntri)
        suffixes = [stacked[p * rows:(p + 1) * rows] for p in range(npair)]
    for p in range(npair):
        if one_suffix_matmul:
            z, suffix = zs[p], suffixes[p]
        else:
            z, sp = scores(p)
            suffix = _dot(sp, ntri)
        carry = carry_ref[p]
        w = jnp.exp2(z + suffix + _lane_fit(carry, tk))
        if diagonal:
            w = jnp.where(visible, w, 0.0)
        pv = _dot_nt(w.astype(BF16), v_tile(p))
        acc_ref[p] += jnp.where(m0, pv[0:tq], pv[tq:rows])
        carry_ref[p] = carry + jnp.broadcast_to(suffix[:, 0:1], carry.shape)


def _sb_prompt_kernel(q_ref, k_ref, v_ref, o_ref, qs_ref, acc_ref, carry_ref):
    qi = pl.program_id(1)
    tk = SB_TILE
    _sb_stack_queries(q_ref[0], qs_ref)
    acc_ref[...] = jnp.zeros_like(acc_ref)
    carry_ref[...] = jnp.zeros_like(carry_ref)
    ntri = _sb_neg_tri(tk)

    def step(jb, diagonal):
        _sb_tile_step(qs_ref, acc_ref, carry_ref, lambda p: k_ref[0, jb, _pair_lanes(p), :],
                      lambda p: v_ref[0, jb, _pair_lanes(p), :], ntri, diagonal, True)

    step(qi, True)

    def body(i, c):
        step(qi - 1 - 2 * i, False)
        step(qi - 2 - 2 * i, False)
        return c

    lax.fori_loop(0, qi // 2, body, 0)

    @pl.when(qi % 2 == 1)
    def _():
        step(0, False)

    for p in range(acc_ref.shape[0]):
        o_ref[0, :, p * LANES:(p + 1) * LANES] = acc_ref[p].astype(o_ref.dtype)


def _sb_scratch(tq, npair):
    return [pltpu.VMEM((npair, 2 * tq, LANES), BF16), pltpu.VMEM((npair, tq, LANES), F32),
            pltpu.VMEM((npair, 2 * tq, LANES), F32)]


def _sb_prompt(q, kt, vt):
    b, s, w = q.shape
    tq = SB_TILE
    assert kt.shape == (b, s // tq, w, tq)
    qspec = pl.BlockSpec((1, tq, w), lambda i, j: (i, j, 0))
    kvspec = pl.BlockSpec((1,) + kt.shape[1:], lambda i, j: (i, 0, 0, 0))
    return pl.pallas_call(
        _sb_prompt_kernel,
        grid=(b, s // tq),
        in_specs=[qspec, kvspec, kvspec],
        out_specs=qspec,
        out_shape=jax.ShapeDtypeStruct((b, s, w), BF16),
        scratch_shapes=_sb_scratch(tq, w // LANES),
        compiler_params=_cparams(("arbitrary", "arbitrary")),
        name="sb_prompt",
    )(q, kt, vt)


def _sb_sample_kernel(q_ref, kn_ref, vn_ref, kp_ref, vp_ref, o_ref, qs_ref, acc_ref, carry_ref):
    nseq, sq, w = q_ref.shape
    past = kp_ref.shape[3]
    npair = w // LANES
    tk = SB_TILE
    for si in range(nseq):
        _sb_stack_queries(q_ref[si], qs_ref, si * npair)
    acc_ref[...] = jnp.zeros_like(acc_ref)
    carry_ref[...] = jnp.zeros_like(carry_ref)
    _sb_tile_step(qs_ref, acc_ref, carry_ref, lambda e: kn_ref[e // npair, 0, _pair_lanes(e % npair), :],
                  lambda e: vn_ref[e // npair, 0, _pair_lanes(e % npair), :], _sb_neg_tri(sq), True, True)
    ntri = _sb_neg_tri(tk)

    def body(i, c):
        cols = pl.ds(pl.multiple_of(past - (i + 1) * tk, tk), tk)

        def pair(ref, e):
            p = e % npair
            return ref[e // npair, 2 * p:2 * p + 2, :, cols].reshape(LANES, tk).astype(BF16)

        _sb_tile_step(qs_ref, acc_ref, carry_ref, lambda e: pair(kp_ref, e), lambda e: pair(vp_ref, e), ntri, False,
                      True)
        return c

    lax.fori_loop(0, past // tk, body, 0)
    for e in range(acc_ref.shape[0]):
        o_ref[e // npair, :, _pair_lanes(e % npair)] = acc_ref[e].astype(o_ref.dtype)


def _sb_sample(q, kt_new, vt_new, kt_past, vt_past):
    b, sq, w = q.shape
    past = kt_past.shape[3]
    ns = SB_SAMPLE_SEQS
    assert past % SB_TILE == 0 and 2 * DH_B == LANES and b % ns == 0
    qspec = pl.BlockSpec((ns, sq, w), lambda i: (i, 0, 0))
    new = pl.BlockSpec((ns, 1, w, sq), lambda i: (i, 0, 0, 0))
    old = pl.BlockSpec((ns, H_B, DH_B, past), lambda i: (i, 0, 0, 0))
    return pl.pallas_call(
        _sb_sample_kernel,
        grid=(b // ns,),
        in_specs=[qspec, new, new, old, old],
        out_specs=qspec,
        out_shape=jax.ShapeDtypeStruct((b, sq, w), BF16),
        scratch_shapes=_sb_scratch(sq, ns * (w // LANES)),
        compiler_params=_cparams(("arbitrary",)),
        name="sb_sample",
    )(q, kt_new, vt_new, kt_past, vt_past)


def _first_argmax(vals, nrows):
    idx = lax.broadcasted_iota(I32, vals.shape, 0)
    top = jnp.max(vals, axis=0, keepdims=True)
    first = jnp.min(jnp.where(vals == top, idx, nrows), axis=0, keepdims=True)
    return top, first, idx


def _merge_kernel(oa_ref, ob_ref, g_ref, x_ref, wb0_ref, wb1_ref, wo_ref, gain_ref, wr_ref, br_ref,
                  x1_ref, h2_ref, eid_ref, wcol_ref):
    d = x_ref.shape[1]
    ya = _dot(oa_ref[...], wb0_ref[...])
    yb = _dot(ob_ref[...], wb1_ref[...])
    g = g_ref[...].astype(F32)
    m = _sigmoid(g[:, 0:d]) * ya + _sigmoid(g[:, d:2 * d]) * yb
    x1 = x_ref[...] + _dot(m.astype(BF16), wo_ref[...])
    x1_ref[...] = x1
    h2 = _rms_norm(x1, gain_ref[...])
    h2_ref[...] = h2.reshape(h2_ref.shape)

    h_hi, h_lo = _split_bf16(h2)
    w_hi, w_lo = _split_bf16(wr_ref[...])
    lt = _dot_nt(w_hi, h_hi) + _dot_nt(w_hi, h_lo) + _dot_nt(w_lo, h_hi) + br_ref[:, 0:1]
    gl = lt[0:N_GROUPS, :]
    g_top, g_idx, _ = _first_argmax(gl, N_GROUPS)
    g_e = jnp.exp(gl - g_top)
    g_p = jnp.max(g_e / jnp.sum(g_e, axis=0, keepdims=True), axis=0, keepdims=True)
    el = jnp.zeros((EXPERTS_PER_GROUP, lt.shape[1]), F32)
    for g in range(N_GROUPS):
        r0 = 8 + g * EXPERTS_PER_GROUP
        el = jnp.where(g_idx == g, lt[r0:r0 + EXPERTS_PER_GROUP, :], el)
    e_top, i1, eidx = _first_argmax(el, EXPERTS_PER_GROUP)
    e_e = jnp.exp(el - e_top)
    e_p = e_e / jnp.sum(e_e, axis=0, keepdims=True)
    p1 = jnp.max(e_p, axis=0, keepdims=True)
    rest = jnp.where(eidx == i1, -1.0, e_p)
    p2, i2, _ = _first_argmax(rest, EXPERTS_PER_GROUP)
    norm = p1 + p2
    w1 = g_p * (p1 / norm)
    w2 = g_p * (p2 / norm)
    eid_ref[...] = jnp.concatenate([g_idx * EXPERTS_PER_GROUP + i1, g_idx * EXPERTS_PER_GROUP + i2], axis=0)
    rows = lax.broadcasted_iota(I32, (LANES, lt.shape[1]), 0)
    wrows = jnp.where(rows == 0, w1, jnp.where(rows == 1, w2, 0.0))
    wcol_ref[...] = wrows.T


def _merge(oa, ob, gbr, x, wb0, wb1, wo, gain, wr, br):
    t, d = x.shape
    tm = MERGE_TILE
    assert t % tm == 0
    row = lambda w: pl.BlockSpec((tm, w), lambda i: (i, 0))
    full = lambda a: pl.BlockSpec(a.shape, lambda i: (0,) * a.ndim)
    return pl.pallas_call(
        _merge_kernel,
        grid=(t // tm,),
        in_specs=[row(oa.shape[1]), row(ob.shape[1]), row(gbr.shape[1]), row(d),
                  full(wb0), full(wb1), full(wo), full(gain), full(wr), full(br)],
        out_specs=[row(d), pl.BlockSpec((tm, d // LANES, LANES), lambda i: (i, 0, 0)),
                   pl.BlockSpec((TOP_K, tm), lambda i: (0, i)), row(LANES)],
        out_shape=[jax.ShapeDtypeStruct((t, d), F32), jax.ShapeDtypeStruct((t, d // LANES, LANES), F32),
                   jax.ShapeDtypeStruct((TOP_K, t), I32), jax.ShapeDtypeStruct((t, LANES), F32)],
        compiler_params=_cparams(("arbitrary",)),
        name="merge_router",
    )(oa, ob, gbr, x, wb0, wb1, wo, gain, wr, br)


def _positions_kernel(eid_ref, dest_ref, counts_ref, rank_ref):
    nblk, width = eid_ref.shape
    ji = lax.broadcasted_iota(I32, (width, width), 0)
    si = lax.broadcasted_iota(I32, (width, width), 1)
    prefix = jnp.where(ji <= si, 1.0, 0.0).astype(BF16)
    expert = lax.broadcasted_iota(I32, (N_EXPERTS, width), 0)
    group = max(g for g in (8, 4, 2, 1) if nblk % g == 0)

    def onehot(i):
        return expert == eid_ref[pl.ds(i, 1), :]

    def rank_body(ig, run):
        first = pl.multiple_of(ig * group, group)
        ohs = [onehot(first + j) for j in range(group)]
        stacked = jnp.concatenate([jnp.where(oh, 1.0, 0.0) for oh in ohs], axis=0).astype(BF16)
        cum = _dot(stacked, prefix)
        ranks = []
        for j, oh in enumerate(ohs):
            cum_j = cum[j * N_EXPERTS:(j + 1) * N_EXPERTS, :] + run
            ranks.append(jnp.sum(jnp.where(oh, cum_j, 0.0), axis=0, keepdims=True) - 1.0)
            run = cum_j[:, width - 1:width]
        rank_ref[pl.ds(first, group), :] = jnp.concatenate(ranks, axis=0)
        return run

    counts = lax.fori_loop(0, nblk // group, rank_body, jnp.zeros((N_EXPERTS, 1), F32))
    counts_ref[...] = jnp.broadcast_to(counts, counts_ref.shape).astype(I32)
    c_hi = jnp.floor(counts * (1.0 / 256.0))
    c_lo = counts - 256.0 * c_hi
    ei = lax.broadcasted_iota(I32, (N_EXPERTS, N_EXPERTS), 0)
    ej = lax.broadcasted_iota(I32, (N_EXPERTS, N_EXPERTS), 1)
    strict = jnp.where(ej < ei, 1.0, 0.0).astype(BF16)
    digits = jnp.concatenate([jnp.broadcast_to(c_hi, (N_EXPERTS, LANES)),
                              jnp.broadcast_to(c_lo, (N_EXPERTS, LANES))], axis=1).astype(BF16)
    sums = _dot(strict, digits)
    start = 256.0 * sums[:, 0:1] + sums[:, LANES:LANES + 1]

    def dest_body(ig, carry):
        first = pl.multiple_of(ig * group, group)
        offs = [jnp.sum(jnp.where(onehot(first + j), start, 0.0), axis=0, keepdims=True) for j in range(group)]
        rows = pl.ds(first, group)
        dest_ref[rows, :] = (rank_ref[rows, :] + jnp.concatenate(offs, axis=0)).astype(I32)
        return carry

    lax.fori_loop(0, nblk // group, dest_body, 0)


def _positions(eid_blocks):
    nblk, width = eid_blocks.shape
    vm = lambda shape: pl.BlockSpec(shape, lambda: (0,) * len(shape))
    return pl.pallas_call(
        _positions_kernel,
        in_specs=[vm((nblk, width))],
        out_specs=[vm((nblk, width)), vm((N_EXPERTS, LANES))],
        out_shape=[jax.ShapeDtypeStruct((nblk, width), I32), jax.ShapeDtypeStruct((N_EXPERTS, LANES), I32)],
        scratch_shapes=[pltpu.VMEM((nblk, width), F32)],
        name="positions",
    )(eid_blocks)


def _dispatch_kernel(n_prompt_tiles, dest_ref, hp_ref, hs_ref, xs_ref, sem):
    i = pl.program_id(0)
    tm = dest_ref.shape[1]

    def scatter(src_ref):
        def start(r, c):
            for k in range(TOP_K):
                pltpu.make_async_copy(src_ref.at[r], xs_ref.at[dest_ref[k, r]], sem).start(priority=k)
            return c

        lax.fori_loop(0, tm, start, 0, unroll=DMA_UNROLL)
        for k in range(TOP_K):
            pltpu.make_async_copy(src_ref, xs_ref.at[pl.ds(0, tm)], sem).wait()

    @pl.when(i < n_prompt_tiles)
    def _():
        scatter(hp_ref)

    @pl.when(i >= n_prompt_tiles)
    def _():
        scatter(hs_ref)


def _dispatch(dest, h_prompt, h_sample):
    t = dest.shape[1]
    slab = h_prompt.shape[1:]
    tm = DISPATCH_TILE
    assert h_prompt.shape[0] % tm == 0 and h_sample.shape[0] % tm == 0
    npt = h_prompt.shape[0] // tm
    return pl.pallas_call(
        functools.partial(_dispatch_kernel, npt),
        grid=(t // tm,),
        in_specs=[pl.BlockSpec((TOP_K, tm), lambda i: (0, i), memory_space=pltpu.SMEM),
                  pl.BlockSpec((tm,) + slab, lambda i: (jnp.minimum(i, npt - 1), 0, 0)),
                  pl.BlockSpec((tm,) + slab, lambda i: (jnp.maximum(i - npt, 0), 0, 0))],
        out_specs=pl.BlockSpec(memory_space=pl.ANY),
        out_shape=jax.ShapeDtypeStruct((TOP_K * t,) + slab, F32),
        scratch_shapes=[pltpu.SemaphoreType.DMA(())],
        compiler_params=_cparams(("arbitrary",)),
        name="dispatch",
    )(dest, h_prompt, h_sample)


def _experts_kernel(vblk_ref, vexp_ref, vlo_ref, vhi_ref, xs_ref, wg_ref, wu_ref, wd_ref, ys_ref,
                    wg16_ref, wu16_ref, wd16_ref):
    v = pl.program_id(0)
    lo = vlo_ref[v]
    hi = vhi_ref[v]
    prev = jnp.maximum(v - 1, 0)
    first = jnp.logical_or(v == 0, vblk_ref[v] != vblk_ref[prev])
    new_expert = jnp.logical_or(v == 0, vexp_ref[v] != vexp_ref[prev])

    @pl.when(new_expert)
    def _():
        wg16_ref[...] = wg_ref[0].astype(BF16)
        wu16_ref[...] = wu_ref[0].astype(BF16)
        wd16_ref[...] = wd_ref[0].astype(BF16)

    @pl.when(hi > lo)
    def _():
        tm = xs_ref.shape[0]
        d = wg_ref.shape[1]
        x = xs_ref[...].reshape(tm, d).astype(BF16)
        gate = _dot(x, wg16_ref[...])
        up = _dot(x, wu16_ref[...])
        hid = (gate * _sigmoid(gate) * up).astype(BF16)
        y = _dot(hid, wd16_ref[...])
        whole = jnp.logical_and(lo == 0, hi == tm)

        @pl.when(whole)
        def _():
            ys_ref[...] = y.reshape(ys_ref.shape)

        @pl.when(jnp.logical_not(whole))
        def _():
            rows = lax.broadcasted_iota(I32, y.shape, 0)
            mine = (rows >= lo) & (rows < hi)

            @pl.when(first)
            def _():
                ys_ref[...] = jnp.where(mine, y, 0.0).reshape(ys_ref.shape)

            @pl.when(jnp.logical_not(first))
            def _():
                ys_ref[...] = jnp.where(mine, y, ys_ref[...].reshape(tm, d)).reshape(ys_ref.shape)


def _experts(vblk, vexp, vlo, vhi, xs, wg, wu, wd):
    a = xs.shape[0]
    slab = xs.shape[1:]
    d, de = wg.shape[1:]
    tm = MOE_TILE
    grid_spec = pltpu.PrefetchScalarGridSpec(
        num_scalar_prefetch=4,
        grid=(vblk.shape[0],),
        in_specs=[pl.BlockSpec((tm,) + slab, lambda v, b, e, lo, hi: (b[v], 0, 0)),
                  pl.BlockSpec((1, d, de), lambda v, b, e, lo, hi: (e[v], 0, 0)),
                  pl.BlockSpec((1, d, de), lambda v, b, e, lo, hi: (e[v], 0, 0)),
                  pl.BlockSpec((1, de, d), lambda v, b, e, lo, hi: (e[v], 0, 0))],
        out_specs=pl.BlockSpec((tm,) + slab, lambda v, b, e, lo, hi: (b[v], 0, 0)),
        scratch_shapes=[pltpu.VMEM((d, de), BF16), pltpu.VMEM((d, de), BF16), pltpu.VMEM((de, d), BF16)],
    )
    return pl.pallas_call(
        _experts_kernel,
        grid_spec=grid_spec,
        out_shape=jax.ShapeDtypeStruct((a,) + slab, F32),
        compiler_params=_cparams(("arbitrary",)),
        name="experts",
    )(vblk, vexp, vlo, vhi, xs, wg, wu, wd)


def _visit_plan(counts, n_rows):
    tm = MOE_TILE
    nblk = n_rows // tm
    n_visits = nblk + N_EXPERTS - 1
    ends = jnp.cumsum(counts)
    starts = ends - counts
    first_blk = starts // tm
    nvis = jnp.where(counts > 0, (ends + tm - 1) // tm - first_blk, 0)
    vis_end = jnp.cumsum(nvis)
    vis_start = vis_end - nvis
    v = jnp.arange(n_visits, dtype=I32)
    e = jnp.minimum(jnp.sum((vis_end[None, :] <= v[:, None]).astype(I32), axis=1), N_EXPERTS - 1)
    valid = v < vis_end[-1]
    blk = first_blk[e] + (v - vis_start[e])
    lo = jnp.clip(starts[e] - blk * tm, 0, tm)
    hi = jnp.clip(ends[e] - blk * tm, 0, tm)
    last_e = jnp.max(jnp.where(counts > 0, jnp.arange(N_EXPERTS, dtype=I32), 0))
    blk = jnp.where(valid, blk, nblk - 1).astype(I32)
    e = jnp.where(valid, e, last_e).astype(I32)
    lo = jnp.where(valid, lo, 0).astype(I32)
    hi = jnp.where(valid, hi, 0).astype(I32)
    return blk, e, lo, hi


def _combine_kernel(dest_ref, ys_ref, x1_ref, wcol_ref, gain_ref, out_ref, buf_ref, sems):
    tm, d = x1_ref.shape
    part = tm // COMBINE_PARTS

    def start(r, c, sem):
        for k in range(TOP_K):
            pltpu.make_async_copy(ys_ref.at[dest_ref[k, r]], buf_ref.at[k, r], sem).start(priority=k)
        return c

    for h in range(COMBINE_PARTS):
        lax.fori_loop(h * part, (h + 1) * part, functools.partial(start, sem=sems.at[h]), 0, unroll=DMA_UNROLL)
    for h in range(COMBINE_PARTS):
        rows = pl.ds(h * part, part)
        for k in range(TOP_K):
            pltpu.make_async_copy(ys_ref.at[rows], buf_ref.at[k, rows], sems.at[h]).wait()
        y = (wcol_ref[rows, 0:1] * buf_ref[0, rows].reshape(part, d)
             + wcol_ref[rows, 1:2] * buf_ref[1, rows].reshape(part, d))
        out_ref[rows, :] = _rms_norm(x1_ref[rows, :] + y, gain_ref[...])


def _combine(dest, ys, x1, wcol, gain):
    t, d = x1.shape
    tm = ROW_TILE
    row = lambda w: pl.BlockSpec((tm, w), lambda i: (i, 0))
    return pl.pallas_call(
        _combine_kernel,
        grid=(t // tm,),
        in_specs=[pl.BlockSpec((TOP_K, tm), lambda i: (0, i), memory_space=pltpu.SMEM),
                  pl.BlockSpec(memory_space=pl.ANY), row(d), row(LANES),
                  pl.BlockSpec(gain.shape, lambda i: (0, 0))],
        out_specs=row(d),
        out_shape=jax.ShapeDtypeStruct((t, d), F32),
        scratch_shapes=[pltpu.VMEM((TOP_K, tm) + ys.shape[1:], F32), pltpu.SemaphoreType.DMA((COMBINE_PARTS,))],
        compiler_params=_cparams(("arbitrary",)),
        name="combine",
    )(dest, ys, x1, wcol, gain)


def _pad_heads(w):
    r = w.shape[0]
    return jnp.pad(w.reshape(r, H_A, DK_A), ((0, 0), (0, 0), (0, LANES - DK_A))).reshape(r, H_A * LANES)


def _prepare_weights(w_in, w_gla_gate_up, b_gla_gate, w_branch, w_out, w_router_group, b_router_group,
                     w_router_expert, b_router_expert):
    d = w_in.shape[0]
    qk = H_A * DK_A
    mw = H_A * DV_A
    c = 0
    w_qa, c = w_in[:, c:c + qk], c + qk
    w_ka, c = w_in[:, c:c + qk], c + qk
    w_va, c = w_in[:, c:c + mw], c + mw
    w_ra, c = w_in[:, c:c + mw], c + mw
    w_lr, c = w_in[:, c:c + GATE_RANK], c + GATE_RANK
    w_b, c = w_in[:, c:c + 3 * mw], c + 3 * mw
    w_g = w_in[:, c:]
    wa = jnp.concatenate([_pad_heads(w_qa), _pad_heads(w_ka), w_va, w_ra,
                          jnp.pad(w_lr, ((0, 0), (0, LANES - GATE_RANK)))], axis=1).astype(BF16)
    wgu = jnp.pad(_pad_heads(w_gla_gate_up), ((0, LANES - GATE_RANK), (0, 0))).astype(BF16)
    bgu = _pad_heads(b_gla_gate[None, :])
    wr = jnp.zeros((LANES, d), F32)
    wr = wr.at[0:N_GROUPS].set(w_router_group.T).at[8:8 + N_EXPERTS].set(w_router_expert.T)
    br = jnp.zeros((LANES,), F32).at[0:N_GROUPS].set(b_router_group).at[8:8 + N_EXPERTS].set(b_router_expert)
    br = jnp.broadcast_to(br[:, None], (LANES, LANES))
    return dict(wa=wa, wqb=w_b[:, 0:mw].astype(BF16), wkvt=w_b[:, mw:3 * mw].T.astype(BF16),
                wg=w_g.astype(BF16), wgu=wgu, bgu=bgu,
                wb0=w_branch[0].astype(BF16), wb1=w_branch[1].astype(BF16), wo=w_out.astype(BF16),
                wr=wr, br=br)


def _mixers(x, s0, k_past, v_past, w, norm_mix_gain, gla_norm_gain, norm_ffn_gain):
    b, s, d = x.shape
    xf = x.reshape(b * s, d)
    qa, ka, va, ra, la, qb, kt, vt, kt16, vt16, gbr = _in_projection(
        xf, s, norm_mix_gain[None, :], w["wa"], w["wqb"], w["wkvt"], w["wg"], w["wgu"], w["bgu"])
    seq = lambda a: a.reshape(b, s, a.shape[-1])
    oa, s_new = _gla(seq(qa), seq(ka), seq(va), seq(ra), seq(la), s0, gla_norm_gain[None, :],
                     min(s, ROW_TILE))
    to_channel_major = lambda a: jnp.transpose(a, (0, 2, 3, 1))
    if k_past is None:
        ob = _sb_prompt(seq(qb), kt16, vt16)
    else:
        ob = _sb_sample(seq(qb), kt16, vt16, to_channel_major(k_past), to_channel_major(v_past))
    x1, h2, eid, wcol = _merge(oa.reshape(b * s, -1), ob.reshape(b * s, -1), gbr, xf, w["wb0"], w["wb1"],
                               w["wo"], norm_ffn_gain[None, :], w["wr"], w["br"])
    from_channel_major = lambda a: jnp.transpose(a.reshape(b, H_B, DH_B, s), (0, 3, 1, 2))
    return x1, h2, eid, wcol, s_new, from_channel_major(kt), from_channel_major(vt)


def kernel(x_prompt, x_sample, state_gla, cache_sb_k, cache_sb_v, norm_mix_gain, w_in, w_gla_gate_up, b_gla_gate, gla_norm_gain, w_branch, w_out, norm_ffn_gain, w_router_group, b_router_group, w_router_expert, b_router_expert, w_exp_gate, w_exp_up, w_exp_down, norm_final_gain):
    depth = w_in.shape[0]
    assert depth == 1, "one trunk layer per step"
    l = 0
    w = _prepare_weights(w_in[l], w_gla_gate_up[l], b_gla_gate[l], w_branch[l], w_out[l], w_router_group[l],
                         b_router_group[l], w_router_expert[l], b_router_expert[l])
    bp, sp, d = x_prompt.shape
    bs, ss, _ = x_sample.shape
    s0 = jnp.zeros((bp, H_A, DK_A, DV_A), x_prompt.dtype)
    x1p, h2p, eidp, wcolp, gla_p, k_p, v_p = _mixers(
        x_prompt, s0, None, None, w, norm_mix_gain[l], gla_norm_gain[l], norm_ffn_gain[l])
    x1s, h2s, eids, wcols, gla_s, k_s, v_s = _mixers(
        x_sample, state_gla[l], cache_sb_k[l], cache_sb_v[l], w, norm_mix_gain[l], gla_norm_gain[l],
        norm_ffn_gain[l])

    tp, ts = bp * sp, bs * ss
    eid = jnp.concatenate([eidp, eids], axis=1)
    dest_blocks, counts = _positions(eid.reshape(-1, SORT_WIDTH))
    dest = dest_blocks.reshape(TOP_K, tp + ts)
    xs = _dispatch(dest, h2p, h2s)
    vblk, vexp, vlo, vhi = _visit_plan(counts[:, 0], TOP_K * (tp + ts))
    ys = _experts(vblk, vexp, vlo, vhi, xs, w_exp_gate[l], w_exp_up[l], w_exp_down[l])
    gf = norm_final_gain[None, :]
    y_prompt = _combine(dest[:, :tp], ys, x1p, wcolp, gf).reshape(bp, sp, d)
    y_sample = _combine(dest[:, tp:], ys, x1s, wcols, gf).reshape(bs, ss, d)
    return (y_prompt, y_sample, gla_p[None], k_p[None], v_p[None], gla_s[None], k_s[None], v_s[None])
```

```python
import functools

import jax
import jax.numpy as jnp
from jax import lax
from jax.experimental import pallas as pl
from jax.experimental.pallas import tpu as pltpu

F32 = jnp.float32
BF16 = jnp.bfloat16
I32 = jnp.int32

LANES = 128
LOG2_E = 1.4426950408889634
RMS_EPS = 1e-6
GATE_TAU = 16.0
H_A = 4
DK_A = 64
DV_A = 128
GATE_RANK = 16
H_B = 8
DH_B = 64
N_GROUPS = 4
EXPERTS_PER_GROUP = 8
N_EXPERTS = N_GROUPS * EXPERTS_PER_GROUP
TOP_K = 2
GLA_CHUNK = 64
GLA_SUB = 16
GLA_EXP_CLAMP = 80.0
GLA_SEQS = 2
ROW_TILE = 256
SB_TILE = 256
SB_SAMPLE_SEQS = 2
MOE_TILE = 512
SORT_WIDTH = 256
INPROJ_TILE = 512
MERGE_TILE = 512
DISPATCH_TILE = 512
COMBINE_TILE = 512
COMBINE_PARTS = 4
DMA_UNROLL = 8
VMEM_LIMIT = 56 * 1024 * 1024


def _cparams(sem):
    return pltpu.CompilerParams(dimension_semantics=sem, vmem_limit_bytes=VMEM_LIMIT)


def _dot(a, b):
    return jnp.dot(a, b, preferred_element_type=F32)


def _dot_nt(a, b):
    return lax.dot_general(a, b, (((1,), (1,)), ((), ())), preferred_element_type=F32)


def _dot_tn(a, b):
    return lax.dot_general(a, b, (((0,), (0,)), ((), ())), preferred_element_type=F32)


def _split_bf16(x):
    hi = x.astype(BF16)
    lo = (x - hi.astype(F32)).astype(BF16)
    return hi, lo


def _log_sigmoid(x):
    return jnp.minimum(x, 0.0) - jnp.log(1.0 + jnp.exp(-jnp.abs(x)))


def _sigmoid(x):
    return 1.0 / (1.0 + jnp.exp(-x))


def _rms_norm(x, gain):
    return x * lax.rsqrt(jnp.mean(x * x, axis=-1, keepdims=True) + RMS_EPS) * gain


def _inproj_kernel(x_ref, gain_ref, wa_ref, wqb_ref, wkvt_ref, wg_ref, wgu_ref, bgu_ref,
                   qa_ref, ka_ref, va_ref, ra_ref, la_ref, qb_ref, kt_ref, vt_ref,
                   kt16_ref, vt16_ref, gbr_ref):
    h = _rms_norm(x_ref[...], gain_ref[...]).astype(BF16)
    pa = H_A * LANES
    mw = va_ref.shape[-1]
    kvt = _dot_nt(wkvt_ref[...], h)
    nseq, _, s = kt_ref.shape
    ntile, tile = kt16_ref.shape[1], kt16_ref.shape[3]
    for i in range(nseq):
        kt_ref[i] = kvt[0:mw, i * s:(i + 1) * s]
        vt_ref[i] = kvt[mw:2 * mw, i * s:(i + 1) * s]
        for j in range(ntile):
            cols = slice(i * s + j * tile, i * s + (j + 1) * tile)
            kt16_ref[i, j] = kvt[0:mw, cols].astype(BF16)
            vt16_ref[i, j] = kvt[mw:2 * mw, cols].astype(BF16)
    qb_ref[...] = _dot(h, wqb_ref[...]).astype(BF16)
    qa_ref[...] = _dot(h, wa_ref[:, 0:pa])
    ka_ref[...] = _dot(h, wa_ref[:, pa:2 * pa])
    va_ref[...] = _dot(h, wa_ref[:, 2 * pa:2 * pa + mw])
    ra_ref[...] = _dot(h, wa_ref[:, 2 * pa + mw:2 * pa + 2 * mw])
    lr = _dot(h, wa_ref[:, 2 * pa + 2 * mw:2 * pa + 2 * mw + LANES])
    gl = _dot(lr.astype(BF16), wgu_ref[...]) + bgu_ref[...]
    la_ref[...] = _log_sigmoid(gl) / GATE_TAU
    gbr_ref[...] = _dot(h, wg_ref[...]).astype(gbr_ref.dtype)


def _in_projection(x, seq_len, gain, wa, wqb, wkvt, wg, wgu, bgu):
    t, d = x.shape
    nb = t // seq_len
    pa = H_A * LANES
    mw = wqb.shape[1]
    tm = INPROJ_TILE
    assert t % tm == 0
    row = lambda w: pl.BlockSpec((tm, w), lambda i: (i, 0))
    full = lambda a: pl.BlockSpec(a.shape, lambda i: (0,) * a.ndim, pipeline_mode=pl.Buffered(1))
    if seq_len >= tm:
        per_seq = seq_len // tm
        ntile = tm // SB_TILE
        assert tm % SB_TILE == 0 and seq_len % tm == 0
        kt_spec = pl.BlockSpec((1, mw, tm), lambda i: (i // per_seq, 0, i % per_seq))
        kt16_spec = pl.BlockSpec((1, ntile, mw, SB_TILE), lambda i: (i // per_seq, i % per_seq, 0, 0))
        kt16_shape = (nb, seq_len // SB_TILE, mw, SB_TILE)
    else:
        nseq = tm // seq_len
        assert tm % seq_len == 0
        kt_spec = pl.BlockSpec((nseq, mw, seq_len), lambda i: (i, 0, 0))
        kt16_spec = pl.BlockSpec((nseq, 1, mw, seq_len), lambda i: (i, 0, 0, 0))
        kt16_shape = (nb, 1, mw, seq_len)
    outs = [
        (jax.ShapeDtypeStruct((t, pa), F32), row(pa)), (jax.ShapeDtypeStruct((t, pa), F32), row(pa)),
        (jax.ShapeDtypeStruct((t, mw), F32), row(mw)), (jax.ShapeDtypeStruct((t, mw), F32), row(mw)),
        (jax.ShapeDtypeStruct((t, pa), F32), row(pa)),
        (jax.ShapeDtypeStruct((t, mw), BF16), row(mw)),
        (jax.ShapeDtypeStruct((nb, mw, seq_len), F32), kt_spec), (jax.ShapeDtypeStruct((nb, mw, seq_len), F32), kt_spec),
        (jax.ShapeDtypeStruct(kt16_shape, BF16), kt16_spec), (jax.ShapeDtypeStruct(kt16_shape, BF16), kt16_spec),
        (jax.ShapeDtypeStruct((t, wg.shape[1]), BF16), row(wg.shape[1])),
    ]
    return pl.pallas_call(
        _inproj_kernel,
        grid=(t // tm,),
        in_specs=[row(d), full(gain), full(wa), full(wqb), full(wkvt), full(wg), full(wgu), full(bgu)],
        out_specs=[spec for _, spec in outs],
        out_shape=[shape for shape, _ in outs],
        compiler_params=_cparams(("arbitrary",)),
        name="in_projection",
    )(x, gain, wa, wqb, wkvt, wg, wgu, bgu)


def _gla_chunk(q, k, v, b, st):
    c = q.shape[0]
    b_last = b[c - 1:c, :]
    rows = lax.broadcasted_iota(I32, (c, LANES), 0)
    nsub = c // GLA_SUB
    refs = [jnp.zeros((1, LANES), F32)] + [b[i * GLA_SUB - 1:i * GLA_SUB, :] for i in range(1, nsub)]
    ref_rows = refs[0]
    for i in range(1, nsub):
        ref_rows = jnp.where(rows >= i * GLA_SUB, refs[i], ref_rows)
    q_rel = q * jnp.exp(b - ref_rows)
    lhs = jnp.concatenate(
        [jnp.where((rows >= i * GLA_SUB) & (rows < (i + 1) * GLA_SUB), q_rel, 0.0) for i in range(nsub)],
        axis=1).astype(BF16)
    rhs = jnp.concatenate(
        [jnp.where(rows < (i + 1) * GLA_SUB, k * jnp.exp(jnp.minimum(refs[i] - b, GLA_EXP_CLAMP)), 0.0)
         for i in range(nsub)], axis=1).astype(BF16)
    att = _dot_nt(lhs, rhs)
    tt = lax.broadcasted_iota(I32, (c, c), 0)
    ss = lax.broadcasted_iota(I32, (c, c), 1)
    att = jnp.where(ss <= tt, att, 0.0)
    v16 = v.astype(BF16)
    inter = _dot_nt((q * jnp.exp(b)).astype(BF16), st.astype(BF16))
    intra = _dot(att.astype(BF16), v16)
    kd = (k * jnp.exp(b_last - b)).astype(BF16)
    st_new = st * jnp.exp(b_last) + _dot_tn(v16, kd)
    return inter + intra, st_new


def _gla_kernel(qa_ref, ka_ref, va_ref, ra_ref, la_ref, s0_ref, gain_ref, o_ref, sfin_ref, st_ref):
    j = pl.program_id(1)
    nj = pl.num_programs(1)
    nseq, rows_per_step, _ = qa_ref.shape
    c = GLA_CHUNK
    zpad = jnp.zeros((LANES - DK_A, DV_A), F32)

    @pl.when(j == 0)
    def _():
        for si in range(nseq):
            for h in range(H_A):
                st_ref[si * H_A + h] = jnp.concatenate([s0_ref[si, h], zpad], axis=0).T

    ti = lax.broadcasted_iota(I32, (rows_per_step, rows_per_step), 0)
    si = lax.broadcasted_iota(I32, (rows_per_step, rows_per_step), 1)
    chunk_shift = c.bit_length() - 1
    same_chunk = (ti >> chunk_shift) == (si >> chunk_shift)
    tril_blocks = jnp.where(same_chunk & (si <= ti), 1.0, 0.0).astype(BF16)
    gain = gain_ref[...]
    for si in range(nseq):
        la_hi, la_lo = _split_bf16(la_ref[si])
        b_all = _dot(tril_blocks, la_hi) + _dot(tril_blocks, la_lo)
        for h in range(H_A):
            hp = slice(h * LANES, (h + 1) * LANES)
            hv = slice(h * DV_A, (h + 1) * DV_A)
            st = st_ref[si * H_A + h]
            for ci in range(rows_per_step // c):
                r0 = ci * c
                q = qa_ref[si, r0:r0 + c, hp] * (DK_A ** -0.5)
                o, st = _gla_chunk(q, ka_ref[si, r0:r0 + c, hp], va_ref[si, r0:r0 + c, hv],
                                   b_all[r0:r0 + c, hp], st)
                r = ra_ref[si, r0:r0 + c, hv]
                o = _rms_norm(o, gain) * (r * _sigmoid(r))
                o_ref[si, r0:r0 + c, hv] = o.astype(o_ref.dtype)
            st_ref[si * H_A + h] = st

    @pl.when(j == nj - 1)
    def _():
        for si in range(nseq):
            for h in range(H_A):
                sfin_ref[si, h] = st_ref[si * H_A + h].T[0:DK_A, :]


def _gla(qa, ka, va, ra, la, s0, gain, rows_per_step):
    b, s, pa = qa.shape
    mw = va.shape[-1]
    ns = GLA_SEQS
    assert b % ns == 0
    seq = lambda w: pl.BlockSpec((ns, rows_per_step, w), lambda i, j: (i, j, 0))
    state = pl.BlockSpec((ns, H_A, DK_A, DV_A), lambda i, j: (i, 0, 0, 0))
    return pl.pallas_call(
        _gla_kernel,
        grid=(b // ns, s // rows_per_step),
        in_specs=[seq(pa), seq(pa), seq(mw), seq(mw), seq(pa), state,
                  pl.BlockSpec(gain.shape, lambda i, j: (0, 0))],
        out_specs=[seq(mw), state],
        out_shape=[jax.ShapeDtypeStruct((b, s, mw), BF16),
                   jax.ShapeDtypeStruct((b, H_A, DK_A, DV_A), F32)],
        scratch_shapes=[pltpu.VMEM((ns * H_A, LANES, LANES), F32)],
        compiler_params=_cparams(("arbitrary", "arbitrary")),
        name="gla",
    )(qa, ka, va, ra, la, s0, gain)


def _head_lane_masks():
    lane = lax.broadcasted_iota(I32, (1, LANES), 1)
    return lane < DH_B, lane >= DH_B


def _sb_neg_tri(tk):
    ji = lax.broadcasted_iota(I32, (tk, tk), 0)
    si = lax.broadcasted_iota(I32, (tk, tk), 1)
    return jnp.where(ji >= si, -1.0, 0.0).astype(BF16)


def _sb_stack_queries(q, qs_ref, base=0):
    m0, m1 = _head_lane_masks()
    for p in range(q.shape[1] // LANES):
        qp = (q[:, p * LANES:(p + 1) * LANES].astype(F32) * (DH_B ** -0.5 * LOG2_E)).astype(BF16)
        zero = jnp.zeros_like(qp)
        qs_ref[base + p] = jnp.concatenate([jnp.where(m0, qp, zero), jnp.where(m1, qp, zero)], axis=0)


def _pair_lanes(p):
    return slice(p * LANES, (p + 1) * LANES)


def _lane_fit(x, width):
    if width >= LANES:
        return jnp.concatenate([x] * (width // LANES), axis=1)
    return x[:, 0:width]


def _sb_tile_step(qs_ref, acc_ref, carry_ref, k_tile, v_tile, ntri, diagonal, one_suffix_matmul=False):
    npair, rows, _ = qs_ref.shape
    tq = rows // 2
    tk = ntri.shape[1]
    m0, _ = _head_lane_masks()
    if diagonal:
        t = lax.broadcasted_iota(I32, (rows, tk), 0)
        t = jnp.where(t >= tq, t - tq, t)
        visible = lax.broadcasted_iota(I32, (rows, tk), 1) < t
    def scores(p):
        z = _dot(qs_ref[p], k_tile(p))
        sp = jnp.maximum(z, 0.0) + jnp.log2(1.0 + jnp.exp2(-jnp.abs(z)))
        if diagonal:
            sp = jnp.where(visible, sp, 0.0)
        return z, sp.astype(BF16)

    if one_suffix_matmul:
        zs, sps = zip(*[scores(p) for p in range(npair)])
        stacked = _dot(jnp.concatenate(sps, axis=0), ntri)
        suffixes = [stacked[p * rows:(p + 1) * rows] for p in range(npair)]
    for p in range(npair):
        if one_suffix_matmul:
            z, suffix = zs[p], suffixes[p]
        else:
            z, sp = scores(p)
            suffix = _dot(sp, ntri)
        carry = carry_ref[p]
        w = jnp.exp2(z + suffix + _lane_fit(carry, tk))
        if diagonal:
            w = jnp.where(visible, w, 0.0)
        pv = _dot_nt(w.astype(BF16), v_tile(p))
        acc_ref[p] += jnp.where(m0, pv[0:tq], pv[tq:rows])
        carry_ref[p] = carry + jnp.broadcast_to(suffix[:, 0:1], carry.shape)


def _sb_prompt_kernel(q_ref, k_ref, v_ref, o_ref, qs_ref, acc_ref, carry_ref):
    qi = pl.program_id(1)
    tk = SB_TILE
    _sb_stack_queries(q_ref[0], qs_ref)
    acc_ref[...] = jnp.zeros_like(acc_ref)
    carry_ref[...] = jnp.zeros_like(carry_ref)
    ntri = _sb_neg_tri(tk)

    def step(jb, diagonal):
        _sb_tile_step(qs_ref, acc_ref, carry_ref, lambda p: k_ref[0, jb, _pair_lanes(p), :],
                      lambda p: v_ref[0, jb, _pair_lanes(p), :], ntri, diagonal, True)

    step(qi, True)

    def body(i, c):
        step(qi - 1 - 2 * i, False)
        step(qi - 2 - 2 * i, False)
        return c

    lax.fori_loop(0, qi // 2, body, 0)

    @pl.when(qi % 2 == 1)
    def _():
        step(0, False)

    for p in range(acc_ref.shape[0]):
        o_ref[0, :, p * LANES:(p + 1) * LANES] = acc_ref[p].astype(o_ref.dtype)


def _sb_scratch(tq, npair):
    return [pltpu.VMEM((npair, 2 * tq, LANES), BF16), pltpu.VMEM((npair, tq, LANES), F32),
            pltpu.VMEM((npair, 2 * tq, LANES), F32)]


def _sb_prompt(q, kt, vt):
    b, s, w = q.shape
    tq = SB_TILE
    assert kt.shape == (b, s // tq, w, tq)
    qspec = pl.BlockSpec((1, tq, w), lambda i, j: (i, j, 0))
    kvspec = pl.BlockSpec((1,) + kt.shape[1:], lambda i, j: (i, 0, 0, 0))
    return pl.pallas_call(
        _sb_prompt_kernel,
        grid=(b, s // tq),
        in_specs=[qspec, kvspec, kvspec],
        out_specs=qspec,
        out_shape=jax.ShapeDtypeStruct((b, s, w), BF16),
        scratch_shapes=_sb_scratch(tq, w // LANES),
        compiler_params=_cparams(("arbitrary", "arbitrary")),
        name="sb_prompt",
    )(q, kt, vt)


def _sb_sample_kernel(q_ref, kn_ref, vn_ref, kp_ref, vp_ref, o_ref, qs_ref, acc_ref, carry_ref):
    nseq, sq, w = q_ref.shape
    past = kp_ref.shape[3]
    npair = w // LANES
    tk = SB_TILE
    for si in range(nseq):
        _sb_stack_queries(q_ref[si], qs_ref, si * npair)
    acc_ref[...] = jnp.zeros_like(acc_ref)
    carry_ref[...] = jnp.zeros_like(carry_ref)
    _sb_tile_step(qs_ref, acc_ref, carry_ref, lambda e: kn_ref[e // npair, 0, _pair_lanes(e % npair), :],
                  lambda e: vn_ref[e // npair, 0, _pair_lanes(e % npair), :], _sb_neg_tri(sq), True, True)
    ntri = _sb_neg_tri(tk)

    def body(i, c):
        cols = pl.ds(pl.multiple_of(past - (i + 1) * tk, tk), tk)

        def pair(ref, e):
            p = e % npair
            return ref[e // npair, 2 * p:2 * p + 2, :, cols].reshape(LANES, tk).astype(BF16)

        _sb_tile_step(qs_ref, acc_ref, carry_ref, lambda e: pair(kp_ref, e), lambda e: pair(vp_ref, e), ntri, False,
                      True)
        return c

    lax.fori_loop(0, past // tk, body, 0)
    for e in range(acc_ref.shape[0]):
        o_ref[e // npair, :, _pair_lanes(e % npair)] = acc_ref[e].astype(o_ref.dtype)


def _sb_sample(q, kt_new, vt_new, kt_past, vt_past):
    b, sq, w = q.shape
    past = kt_past.shape[3]
    ns = SB_SAMPLE_SEQS
    assert past % SB_TILE == 0 and 2 * DH_B == LANES and b % ns == 0
    qspec = pl.BlockSpec((ns, sq, w), lambda i: (i, 0, 0))
    new = pl.BlockSpec((ns, 1, w, sq), lambda i: (i, 0, 0, 0))
    old = pl.BlockSpec((ns, H_B, DH_B, past), lambda i: (i, 0, 0, 0))
    return pl.pallas_call(
        _sb_sample_kernel,
        grid=(b // ns,),
        in_specs=[qspec, new, new, old, old],
        out_specs=qspec,
        out_shape=jax.ShapeDtypeStruct((b, sq, w), BF16),
        scratch_shapes=_sb_scratch(sq, ns * (w // LANES)),
        compiler_params=_cparams(("arbitrary",)),
        name="sb_sample",
    )(q, kt_new, vt_new, kt_past, vt_past)


def _first_argmax(vals, nrows):
    idx = lax.broadcasted_iota(I32, vals.shape, 0)
    top = jnp.max(vals, axis=0, keepdims=True)
    first = jnp.min(jnp.where(vals == top, idx, nrows), axis=0, keepdims=True)
    return top, first, idx


def _merge_kernel(oa_ref, ob_ref, g_ref, x_ref, wb0_ref, wb1_ref, wo_ref, gain_ref, wr_ref, br_ref,
                  x1_ref, h2_ref, eid_ref, wcol_ref):
    d = x_ref.shape[1]
    ya = _dot(oa_ref[...], wb0_ref[...])
    yb = _dot(ob_ref[...], wb1_ref[...])
    g = g_ref[...].astype(F32)
    m = _sigmoid(g[:, 0:d]) * ya + _sigmoid(g[:, d:2 * d]) * yb
    x1 = x_ref[...] + _dot(m.astype(BF16), wo_ref[...])
    x1_ref[...] = x1
    h2 = _rms_norm(x1, gain_ref[...])
    h2_ref[...] = h2.reshape(h2_ref.shape)

    h_hi, h_lo = _split_bf16(h2)
    w_hi, w_lo = _split_bf16(wr_ref[...])
    lt = _dot_nt(w_hi, h_hi) + _dot_nt(w_hi, h_lo) + _dot_nt(w_lo, h_hi) + br_ref[:, 0:1]
    gl = lt[0:N_GROUPS, :]
    g_top, g_idx, _ = _first_argmax(gl, N_GROUPS)
    g_e = jnp.exp(gl - g_top)
    g_p = jnp.max(g_e / jnp.sum(g_e, axis=0, keepdims=True), axis=0, keepdims=True)
    el = jnp.zeros((EXPERTS_PER_GROUP, lt.shape[1]), F32)
    for g in range(N_GROUPS):
        r0 = 8 + g * EXPERTS_PER_GROUP
        el = jnp.where(g_idx == g, lt[r0:r0 + EXPERTS_PER_GROUP, :], el)
    e_top, i1, eidx = _first_argmax(el, EXPERTS_PER_GROUP)
    e_e = jnp.exp(el - e_top)
    e_p = e_e / jnp.sum(e_e, axis=0, keepdims=True)
    p1 = jnp.max(e_p, axis=0, keepdims=True)
    rest = jnp.where(eidx == i1, -1.0, e_p)
    p2, i2, _ = _first_argmax(rest, EXPERTS_PER_GROUP)
    norm = p1 + p2
    w1 = g_p * (p1 / norm)
    w2 = g_p * (p2 / norm)
    eid_ref[...] = jnp.concatenate([g_idx * EXPERTS_PER_GROUP + i1, g_idx * EXPERTS_PER_GROUP + i2], axis=0)
    rows = lax.broadcasted_iota(I32, (LANES, lt.shape[1]), 0)
    wrows = jnp.where(rows == 0, w1, jnp.where(rows == 1, w2, 0.0))
    wcol_ref[...] = wrows.T


def _merge(oa, ob, gbr, x, wb0, wb1, wo, gain, wr, br):
    t, d = x.shape
    tm = MERGE_TILE
    assert t % tm == 0
    row = lambda w: pl.BlockSpec((tm, w), lambda i: (i, 0))
    full = lambda a: pl.BlockSpec(a.shape, lambda i: (0,) * a.ndim)
    return pl.pallas_call(
        _merge_kernel,
        grid=(t // tm,),
        in_specs=[row(oa.shape[1]), row(ob.shape[1]), row(gbr.shape[1]), row(d),
                  full(wb0), full(wb1), full(wo), full(gain), full(wr), full(br)],
        out_specs=[row(d), pl.BlockSpec((tm, d // LANES, LANES), lambda i: (i, 0, 0)),
                   pl.BlockSpec((TOP_K, tm), lambda i: (0, i)), row(LANES)],
        out_shape=[jax.ShapeDtypeStruct((t, d), F32), jax.ShapeDtypeStruct((t, d // LANES, LANES), F32),
                   jax.ShapeDtypeStruct((TOP_K, t), I32), jax.ShapeDtypeStruct((t, LANES), F32)],
        compiler_params=_cparams(("arbitrary",)),
        name="merge_router",
    )(oa, ob, gbr, x, wb0, wb1, wo, gain, wr, br)


def _positions_kernel(eid_ref, dest_ref, counts_ref, rank_ref):
    nblk, width = eid_ref.shape
    ji = lax.broadcasted_iota(I32, (width, width), 0)
    si = lax.broadcasted_iota(I32, (width, width), 1)
    prefix = jnp.where(ji <= si, 1.0, 0.0).astype(BF16)
    expert = lax.broadcasted_iota(I32, (N_EXPERTS, width), 0)
    group = max(g for g in (8, 4, 2, 1) if nblk % g == 0)

    def onehot(i):
        return expert == eid_ref[pl.ds(i, 1), :]

    def rank_body(ig, run):
        first = pl.multiple_of(ig * group, group)
        ohs = [onehot(first + j) for j in range(group)]
        stacked = jnp.concatenate([jnp.where(oh, 1.0, 0.0) for oh in ohs], axis=0).astype(BF16)
        cum = _dot(stacked, prefix)
        ranks = []
        for j, oh in enumerate(ohs):
            cum_j = cum[j * N_EXPERTS:(j + 1) * N_EXPERTS, :] + run
            ranks.append(jnp.sum(jnp.where(oh, cum_j, 0.0), axis=0, keepdims=True) - 1.0)
            run = cum_j[:, width - 1:width]
        rank_ref[pl.ds(first, group), :] = jnp.concatenate(ranks, axis=0)
        return run

    counts = lax.fori_loop(0, nblk // group, rank_body, jnp.zeros((N_EXPERTS, 1), F32))
    counts_ref[...] = jnp.broadcast_to(counts, counts_ref.shape).astype(I32)
    c_hi = jnp.floor(counts * (1.0 / 256.0))
    c_lo = counts - 256.0 * c_hi
    ei = lax.broadcasted_iota(I32, (N_EXPERTS, N_EXPERTS), 0)
    ej = lax.broadcasted_iota(I32, (N_EXPERTS, N_EXPERTS), 1)
    strict = jnp.where(ej < ei, 1.0, 0.0).astype(BF16)
    digits = jnp.concatenate([jnp.broadcast_to(c_hi, (N_EXPERTS, LANES)),
                              jnp.broadcast_to(c_lo, (N_EXPERTS, LANES))], axis=1).astype(BF16)
    sums = _dot(strict, digits)
    start = 256.0 * sums[:, 0:1] + sums[:, LANES:LANES + 1]

    def dest_body(ig, carry):
        first = pl.multiple_of(ig * group, group)
        offs = [jnp.sum(jnp.where(onehot(first + j), start, 0.0), axis=0, keepdims=True) for j in range(group)]
        rows = pl.ds(first, group)
        dest_ref[rows, :] = (rank_ref[rows, :] + jnp.concatenate(offs, axis=0)).astype(I32)
        return carry

    lax.fori_loop(0, nblk // group, dest_body, 0)


def _positions(eid_blocks):
    nblk, width = eid_blocks.shape
    vm = lambda shape: pl.BlockSpec(shape, lambda: (0,) * len(shape))
    return pl.pallas_call(
        _positions_kernel,
        in_specs=[vm((nblk, width))],
        out_specs=[vm((nblk, width)), vm((N_EXPERTS, LANES))],
        out_shape=[jax.ShapeDtypeStruct((nblk, width), I32), jax.ShapeDtypeStruct((N_EXPERTS, LANES), I32)],
        scratch_shapes=[pltpu.VMEM((nblk, width), F32)],
        name="positions",
    )(eid_blocks)


def _dispatch_kernel(n_prompt_tiles, dest_ref, hp_ref, hs_ref, xs_ref, sem):
    i = pl.program_id(0)
    tm = dest_ref.shape[1]

    def scatter(src_ref):
        def start(r, c):
            for k in range(TOP_K):
                pltpu.make_async_copy(src_ref.at[r], xs_ref.at[dest_ref[k, r]], sem).start(priority=k)
            return c

        lax.fori_loop(0, tm, start, 0, unroll=DMA_UNROLL)
        for k in range(TOP_K):
            pltpu.make_async_copy(src_ref, xs_ref.at[pl.ds(0, tm)], sem).wait()

    @pl.when(i < n_prompt_tiles)
    def _():
        scatter(hp_ref)

    @pl.when(i >= n_prompt_tiles)
    def _():
        scatter(hs_ref)


def _dispatch(dest, h_prompt, h_sample):
    t = dest.shape[1]
    slab = h_prompt.shape[1:]
    tm = DISPATCH_TILE
    assert h_prompt.shape[0] % tm == 0 and h_sample.shape[0] % tm == 0
    npt = h_prompt.shape[0] // tm
    return pl.pallas_call(
        functools.partial(_dispatch_kernel, npt),
        grid=(t // tm,),
        in_specs=[pl.BlockSpec((TOP_K, tm), lambda i: (0, i), memory_space=pltpu.SMEM),
                  pl.BlockSpec((tm,) + slab, lambda i: (jnp.minimum(i, npt - 1), 0, 0)),
                  pl.BlockSpec((tm,) + slab, lambda i: (jnp.maximum(i - npt, 0), 0, 0))],
        out_specs=pl.BlockSpec(memory_space=pl.ANY),
        out_shape=jax.ShapeDtypeStruct((TOP_K * t,) + slab, F32),
        scratch_shapes=[pltpu.SemaphoreType.DMA(())],
        compiler_params=_cparams(("arbitrary",)),
        name="dispatch",
    )(dest, h_prompt, h_sample)


def _experts_kernel(vblk_ref, vexp_ref, vlo_ref, vhi_ref, xs_ref, wg_ref, wu_ref, wd_ref, ys_ref,
                    wg16_ref, wu16_ref, wd16_ref):
    v = pl.program_id(0)
    lo = vlo_ref[v]
    hi = vhi_ref[v]
    prev = jnp.maximum(v - 1, 0)
    first = jnp.logical_or(v == 0, vblk_ref[v] != vblk_ref[prev])
    new_expert = jnp.logical_or(v == 0, vexp_ref[v] != vexp_ref[prev])

    @pl.when(new_expert)
    def _():
        wg16_ref[...] = wg_ref[0].astype(BF16)
        wu16_ref[...] = wu_ref[0].astype(BF16)
        wd16_ref[...] = wd_ref[0].astype(BF16)

    @pl.when(hi > lo)
    def _():
        tm = xs_ref.shape[0]
        d = wg_ref.shape[1]
        x = xs_ref[...].reshape(tm, d).astype(BF16)
        gate = _dot(x, wg16_ref[...])
        up = _dot(x, wu16_ref[...])
        hid = (gate * _sigmoid(gate) * up).astype(BF16)
        y = _dot(hid, wd16_ref[...])
        whole = jnp.logical_and(lo == 0, hi == tm)

        @pl.when(whole)
        def _():
            ys_ref[...] = y.reshape(ys_ref.shape)

        @pl.when(jnp.logical_not(whole))
        def _():
            rows = lax.broadcasted_iota(I32, y.shape, 0)
            mine = (rows >= lo) & (rows < hi)

            @pl.when(first)
            def _():
                ys_ref[...] = jnp.where(mine, y, 0.0).reshape(ys_ref.shape)

            @pl.when(jnp.logical_not(first))
            def _():
                ys_ref[...] = jnp.where(mine, y, ys_ref[...].reshape(tm, d)).reshape(ys_ref.shape)


def _experts(vblk, vexp, vlo, vhi, xs, wg, wu, wd):
    a = xs.shape[0]
    slab = xs.shape[1:]
    d, de = wg.shape[1:]
    tm = MOE_TILE
    grid_spec = pltpu.PrefetchScalarGridSpec(
        num_scalar_prefetch=4,
        grid=(vblk.shape[0],),
        in_specs=[pl.BlockSpec((tm,) + slab, lambda v, b, e, lo, hi: (b[v], 0, 0)),
                  pl.BlockSpec((1, d, de), lambda v, b, e, lo, hi: (e[v], 0, 0)),
                  pl.BlockSpec((1, d, de), lambda v, b, e, lo, hi: (e[v], 0, 0)),
                  pl.BlockSpec((1, de, d), lambda v, b, e, lo, hi: (e[v], 0, 0))],
        out_specs=pl.BlockSpec((tm,) + slab, lambda v, b, e, lo, hi: (b[v], 0, 0)),
        scratch_shapes=[pltpu.VMEM((d, de), BF16), pltpu.VMEM((d, de), BF16), pltpu.VMEM((de, d), BF16)],
    )
    return pl.pallas_call(
        _experts_kernel,
        grid_spec=grid_spec,
        out_shape=jax.ShapeDtypeStruct((a,) + slab, F32),
        compiler_params=_cparams(("arbitrary",)),
        name="experts",
    )(vblk, vexp, vlo, vhi, xs, wg, wu, wd)


def _visit_plan(counts, n_rows):
    tm = MOE_TILE
    nblk = n_rows // tm
    n_visits = nblk + N_EXPERTS - 1
    ends = jnp.cumsum(counts)
    starts = ends - counts
    first_blk = starts // tm
    nvis = jnp.where(counts > 0, (ends + tm - 1) // tm - first_blk, 0)
    vis_end = jnp.cumsum(nvis)
    vis_start = vis_end - nvis
    v = jnp.arange(n_visits, dtype=I32)
    e = jnp.minimum(jnp.sum((vis_end[None, :] <= v[:, None]).astype(I32), axis=1), N_EXPERTS - 1)
    valid = v < vis_end[-1]
    blk = first_blk[e] + (v - vis_start[e])
    lo = jnp.clip(starts[e] - blk * tm, 0, tm)
    hi = jnp.clip(ends[e] - blk * tm, 0, tm)
    last_e = jnp.max(jnp.where(counts > 0, jnp.arange(N_EXPERTS, dtype=I32), 0))
    blk = jnp.where(valid, blk, nblk - 1).astype(I32)
    e = jnp.where(valid, e, last_e).astype(I32)
    lo = jnp.where(valid, lo, 0).astype(I32)
    hi = jnp.where(valid, hi, 0).astype(I32)
    return blk, e, lo, hi


def _combine_kernel(dest_ref, ys_ref, x1_ref, wcol_ref, gain_ref, out_ref, buf_ref, sems):
    tm, d = x1_ref.shape
    part = tm // COMBINE_PARTS

    def start(r, c, sem):
        for k in range(TOP_K):
            pltpu.make_async_copy(ys_ref.at[dest_ref[k, r]], buf_ref.at[k, r], sem).start(priority=k)
        return c

    for h in range(COMBINE_PARTS):
        lax.fori_loop(h * part, (h + 1) * part, functools.partial(start, sem=sems.at[h]), 0, unroll=DMA_UNROLL)
    for h in range(COMBINE_PARTS):
        rows = pl.ds(h * part, part)
        for k in range(TOP_K):
            pltpu.make_async_copy(ys_ref.at[rows], buf_ref.at[k, rows], sems.at[h]).wait()
        y = (wcol_ref[rows, 0:1] * buf_ref[0, rows].reshape(part, d)
             + wcol_ref[rows, 1:2] * buf_ref[1, rows].reshape(part, d))
        out_ref[rows, :] = _rms_norm(x1_ref[rows, :] + y, gain_ref[...])


def _combine(dest, first_token, ys, x1, wcol, gain):
    t, d = x1.shape
    tm = COMBINE_TILE
    assert t % tm == 0 and first_token % tm == 0
    off = first_token // tm
    row = lambda w: pl.BlockSpec((tm, w), lambda i: (i, 0))
    return pl.pallas_call(
        _combine_kernel,
        grid=(t // tm,),
        in_specs=[pl.BlockSpec((TOP_K, tm), lambda i: (0, i + off), memory_space=pltpu.SMEM),
                  pl.BlockSpec(memory_space=pl.ANY), row(d), row(LANES),
                  pl.BlockSpec(gain.shape, lambda i: (0, 0))],
        out_specs=row(d),
        out_shape=jax.ShapeDtypeStruct((t, d), F32),
        scratch_shapes=[pltpu.VMEM((TOP_K, tm) + ys.shape[1:], F32), pltpu.SemaphoreType.DMA((COMBINE_PARTS,))],
        compiler_params=_cparams(("arbitrary",)),
        name="combine",
    )(dest, ys, x1, wcol, gain)


def _pad_heads(w):
    r = w.shape[0]
    return jnp.pad(w.reshape(r, H_A, DK_A), ((0, 0), (0, 0), (0, LANES - DK_A))).reshape(r, H_A * LANES)


def _prepare_weights(w_in, w_gla_gate_up, b_gla_gate, w_branch, w_out, w_router_group, b_router_group,
                     w_router_expert, b_router_expert):
    d = w_in.shape[0]
    qk = H_A * DK_A
    mw = H_A * DV_A
    c = 0
    w_qa, c = w_in[:, c:c + qk], c + qk
    w_ka, c = w_in[:, c:c + qk], c + qk
    w_va, c = w_in[:, c:c + mw], c + mw
    w_ra, c = w_in[:, c:c + mw], c + mw
    w_lr, c = w_in[:, c:c + GATE_RANK], c + GATE_RANK
    w_b, c = w_in[:, c:c + 3 * mw], c + 3 * mw
    w_g = w_in[:, c:]
    wa = jnp.concatenate([_pad_heads(w_qa), _pad_heads(w_ka), w_va, w_ra,
                          jnp.pad(w_lr, ((0, 0), (0, LANES - GATE_RANK)))], axis=1).astype(BF16)
    wgu = jnp.pad(_pad_heads(w_gla_gate_up), ((0, LANES - GATE_RANK), (0, 0))).astype(BF16)
    bgu = _pad_heads(b_gla_gate[None, :])
    wr = jnp.zeros((LANES, d), F32)
    wr = wr.at[0:N_GROUPS].set(w_router_group.T).at[8:8 + N_EXPERTS].set(w_router_expert.T)
    br = jnp.zeros((LANES,), F32).at[0:N_GROUPS].set(b_router_group).at[8:8 + N_EXPERTS].set(b_router_expert)
    br = jnp.broadcast_to(br[:, None], (LANES, LANES))
    return dict(wa=wa, wqb=w_b[:, 0:mw].astype(BF16), wkvt=w_b[:, mw:3 * mw].T.astype(BF16),
                wg=w_g.astype(BF16), wgu=wgu, bgu=bgu,
                wb0=w_branch[0].astype(BF16), wb1=w_branch[1].astype(BF16), wo=w_out.astype(BF16),
                wr=wr, br=br)


def _mixers(x, s0, k_past, v_past, w, norm_mix_gain, gla_norm_gain, norm_ffn_gain):
    b, s, d = x.shape
    xf = x.reshape(b * s, d)
    qa, ka, va, ra, la, qb, kt, vt, kt16, vt16, gbr = _in_projection(
        xf, s, norm_mix_gain[None, :], w["wa"], w["wqb"], w["wkvt"], w["wg"], w["wgu"], w["bgu"])
    seq = lambda a: a.reshape(b, s, a.shape[-1])
    oa, s_new = _gla(seq(qa), seq(ka), seq(va), seq(ra), seq(la), s0, gla_norm_gain[None, :],
                     min(s, ROW_TILE))
    to_channel_major = lambda a: jnp.transpose(a, (0, 2, 3, 1))
    if k_past is None:
        ob = _sb_prompt(seq(qb), kt16, vt16)
    else:
        ob = _sb_sample(seq(qb), kt16, vt16, to_channel_major(k_past), to_channel_major(v_past))
    x1, h2, eid, wcol = _merge(oa.reshape(b * s, -1), ob.reshape(b * s, -1), gbr, xf, w["wb0"], w["wb1"],
                               w["wo"], norm_ffn_gain[None, :], w["wr"], w["br"])
    from_channel_major = lambda a: jnp.transpose(a.reshape(b, H_B, DH_B, s), (0, 3, 1, 2))
    return x1, h2, eid, wcol, s_new, from_channel_major(kt), from_channel_major(vt)


def kernel(x_prompt, x_sample, state_gla, cache_sb_k, cache_sb_v, norm_mix_gain, w_in, w_gla_gate_up, b_gla_gate, gla_norm_gain, w_branch, w_out, norm_ffn_gain, w_router_group, b_router_group, w_router_expert, b_router_expert, w_exp_gate, w_exp_up, w_exp_down, norm_final_gain):
    depth = w_in.shape[0]
    assert depth == 1, "one trunk layer per step"
    l = 0
    w = _prepare_weights(w_in[l], w_gla_gate_up[l], b_gla_gate[l], w_branch[l], w_out[l], w_router_group[l],
                         b_router_group[l], w_router_expert[l], b_router_expert[l])
    bp, sp, d = x_prompt.shape
    bs, ss, _ = x_sample.shape
    s0 = jnp.zeros((bp, H_A, DK_A, DV_A), x_prompt.dtype)
    x1p, h2p, eidp, wcolp, gla_p, k_p, v_p = _mixers(
        x_prompt, s0, None, None, w, norm_mix_gain[l], gla_norm_gain[l], norm_ffn_gain[l])
    x1s, h2s, eids, wcols, gla_s, k_s, v_s = _mixers(
        x_sample, state_gla[l], cache_sb_k[l], cache_sb_v[l], w, norm_mix_gain[l], gla_norm_gain[l],
        norm_ffn_gain[l])

    tp, ts = bp * sp, bs * ss
    eid = jnp.concatenate([eidp, eids], axis=1)
    dest_blocks, counts = _positions(eid.reshape(-1, SORT_WIDTH))
    dest = dest_blocks.reshape(TOP_K, tp + ts)
    xs = _dispatch(dest, h2p, h2s)
    vblk, vexp, vlo, vhi = _visit_plan(counts[:, 0], TOP_K * (tp + ts))
    ys = _experts(vblk, vexp, vlo, vhi, xs, w_exp_gate[l], w_exp_up[l], w_exp_down[l])
    gf = norm_final_gain[None, :]
    y_prompt = _combine(dest, 0, ys, x1p, wcolp, gf).reshape(bp, sp, d)
    y_sample = _combine(dest, tp, ys, x1s, wcols, gf).reshape(bs, ss, d)
    return (y_prompt, y_sample, gla_p[None], k_p[None], v_p[None], gla_s[None], k_s[None], v_s[None])
```

```python
import functools

import jax
import jax.numpy as jnp
from jax import lax
from jax.experimental import pallas as pl
from jax.experimental.pallas import tpu as pltpu

F32 = jnp.float32
BF16 = jnp.bfloat16
MOE_OUT_DTYPE = jnp.bfloat16
I32 = jnp.int32

LANES = 128
LOG2_E = 1.4426950408889634
RMS_EPS = 1e-6
GATE_TAU = 16.0
H_A = 4
DK_A = 64
DV_A = 128
GATE_RANK = 16
H_B = 8
DH_B = 64
N_GROUPS = 4
EXPERTS_PER_GROUP = 8
N_EXPERTS = N_GROUPS * EXPERTS_PER_GROUP
TOP_K = 2
GLA_CHUNK = 64
GLA_SUB = 16
GLA_EXP_CLAMP = 80.0
GLA_SEQS = 4
ROW_TILE = 256
SB_TILE = 256
SB_SAMPLE_SEQS = 2
MOE_TILE = 512
SORT_WIDTH = 256
INPROJ_TILE = 512
MERGE_TILE = 512
DISPATCH_TILE = 1024
COMBINE_TILE = 1024
COMBINE_PARTS = 4
DMA_UNROLL = 8
VMEM_LIMIT = 56 * 1024 * 1024


def _cparams(sem):
    return pltpu.CompilerParams(dimension_semantics=sem, vmem_limit_bytes=VMEM_LIMIT)


def _dot(a, b):
    return jnp.dot(a, b, preferred_element_type=F32)


def _dot_nt(a, b):
    return lax.dot_general(a, b, (((1,), (1,)), ((), ())), preferred_element_type=F32)


def _dot_tn(a, b):
    return lax.dot_general(a, b, (((0,), (0,)), ((), ())), preferred_element_type=F32)


def _split_bf16(x):
    hi = x.astype(BF16)
    lo = (x - hi.astype(F32)).astype(BF16)
    return hi, lo


def _log_sigmoid(x):
    return jnp.minimum(x, 0.0) - jnp.log(1.0 + jnp.exp(-jnp.abs(x)))


def _sigmoid(x):
    return 1.0 / (1.0 + jnp.exp(-x))


def _rms_norm(x, gain):
    return x * lax.rsqrt(jnp.mean(x * x, axis=-1, keepdims=True) + RMS_EPS) * gain


def _inproj_kernel(x_ref, gain_ref, wa_ref, wqb_ref, wkvt_ref, wg_ref, wgu_ref, bgu_ref,
                   qa_ref, ka_ref, va_ref, ra_ref, la_ref, qb_ref, kt_ref, vt_ref,
                   kt16_ref, vt16_ref, gbr_ref):
    h = _rms_norm(x_ref[...], gain_ref[...]).astype(BF16)
    pa = H_A * LANES
    mw = va_ref.shape[-1]
    kvt = _dot_nt(wkvt_ref[...], h)
    nseq, _, s = kt_ref.shape
    ntile, tile = kt16_ref.shape[1], kt16_ref.shape[3]
    for i in range(nseq):
        kt_ref[i] = kvt[0:mw, i * s:(i + 1) * s]
        vt_ref[i] = kvt[mw:2 * mw, i * s:(i + 1) * s]
        for j in range(ntile):
            cols = slice(i * s + j * tile, i * s + (j + 1) * tile)
            kt16_ref[i, j] = kvt[0:mw, cols].astype(BF16)
            vt16_ref[i, j] = kvt[mw:2 * mw, cols].astype(BF16)
    qb_ref[...] = _dot(h, wqb_ref[...]).astype(BF16)
    qa_ref[...] = _dot(h, wa_ref[:, 0:pa])
    ka_ref[...] = _dot(h, wa_ref[:, pa:2 * pa])
    va_ref[...] = _dot(h, wa_ref[:, 2 * pa:2 * pa + mw])
    ra_ref[...] = _dot(h, wa_ref[:, 2 * pa + mw:2 * pa + 2 * mw])
    lr = _dot(h, wa_ref[:, 2 * pa + 2 * mw:2 * pa + 2 * mw + LANES])
    gl = _dot(lr.astype(BF16), wgu_ref[...]) + bgu_ref[...]
    la_ref[...] = _log_sigmoid(gl) / GATE_TAU
    gbr_ref[...] = _dot(h, wg_ref[...]).astype(gbr_ref.dtype)


def _in_projection(x, seq_len, gain, wa, wqb, wkvt, wg, wgu, bgu):
    t, d = x.shape
    nb = t // seq_len
    pa = H_A * LANES
    mw = wqb.shape[1]
    tm = INPROJ_TILE
    assert t % tm == 0
    row = lambda w: pl.BlockSpec((tm, w), lambda i: (i, 0))
    full = lambda a: pl.BlockSpec(a.shape, lambda i: (0,) * a.ndim, pipeline_mode=pl.Buffered(1))
    if seq_len >= tm:
        per_seq = seq_len // tm
        ntile = tm // SB_TILE
        assert tm % SB_TILE == 0 and seq_len % tm == 0
        kt_spec = pl.BlockSpec((1, mw, tm), lambda i: (i // per_seq, 0, i % per_seq))
        kt16_spec = pl.BlockSpec((1, ntile, mw, SB_TILE), lambda i: (i // per_seq, i % per_seq, 0, 0))
        kt16_shape = (nb, seq_len // SB_TILE, mw, SB_TILE)
    else:
        nseq = tm // seq_len
        assert tm % seq_len == 0
        kt_spec = pl.BlockSpec((nseq, mw, seq_len), lambda i: (i, 0, 0))
        kt16_spec = pl.BlockSpec((nseq, 1, mw, seq_len), lambda i: (i, 0, 0, 0))
        kt16_shape = (nb, 1, mw, seq_len)
    outs = [
        (jax.ShapeDtypeStruct((t, pa), F32), row(pa)), (jax.ShapeDtypeStruct((t, pa), F32), row(pa)),
        (jax.ShapeDtypeStruct((t, mw), F32), row(mw)), (jax.ShapeDtypeStruct((t, mw), F32), row(mw)),
        (jax.ShapeDtypeStruct((t, pa), F32), row(pa)),
        (jax.ShapeDtypeStruct((t, mw), BF16), row(mw)),
        (jax.ShapeDtypeStruct((nb, mw, seq_len), F32), kt_spec), (jax.ShapeDtypeStruct((nb, mw, seq_len), F32), kt_spec),
        (jax.ShapeDtypeStruct(kt16_shape, BF16), kt16_spec), (jax.ShapeDtypeStruct(kt16_shape, BF16), kt16_spec),
        (jax.ShapeDtypeStruct((t, wg.shape[1]), BF16), row(wg.shape[1])),
    ]
    return pl.pallas_call(
        _inproj_kernel,
        grid=(t // tm,),
        in_specs=[row(d), full(gain), full(wa), full(wqb), full(wkvt), full(wg), full(wgu), full(bgu)],
        out_specs=[spec for _, spec in outs],
        out_shape=[shape for shape, _ in outs],
        compiler_params=_cparams(("arbitrary",)),
        name="in_projection",
    )(x, gain, wa, wqb, wkvt, wg, wgu, bgu)


def _gla_chunk(q, k, v, b, st):
    c = q.shape[0]
    b_last = b[c - 1:c, :]
    rows = lax.broadcasted_iota(I32, (c, LANES), 0)
    nsub = c // GLA_SUB
    refs = [jnp.zeros((1, LANES), F32)] + [b[i * GLA_SUB - 1:i * GLA_SUB, :] for i in range(1, nsub)]
    ref_rows = refs[0]
    for i in range(1, nsub):
        ref_rows = jnp.where(rows >= i * GLA_SUB, refs[i], ref_rows)
    q_rel = q * jnp.exp(b - ref_rows)
    lhs = jnp.concatenate(
        [jnp.where((rows >= i * GLA_SUB) & (rows < (i + 1) * GLA_SUB), q_rel, 0.0) for i in range(nsub)],
        axis=1).astype(BF16)
    rhs = jnp.concatenate(
        [jnp.where(rows < (i + 1) * GLA_SUB, k * jnp.exp(jnp.minimum(refs[i] - b, GLA_EXP_CLAMP)), 0.0)
         for i in range(nsub)], axis=1).astype(BF16)
    att = _dot_nt(lhs, rhs)
    tt = lax.broadcasted_iota(I32, (c, c), 0)
    ss = lax.broadcasted_iota(I32, (c, c), 1)
    att = jnp.where(ss <= tt, att, 0.0)
    v16 = v.astype(BF16)
    inter = _dot_nt((q * jnp.exp(b)).astype(BF16), st.astype(BF16))
    intra = _dot(att.astype(BF16), v16)
    kd = (k * jnp.exp(b_last - b)).astype(BF16)
    st_new = st * jnp.exp(b_last) + _dot_tn(v16, kd)
    return inter + intra, st_new


def _gla_kernel(qa_ref, ka_ref, va_ref, ra_ref, la_ref, s0_ref, gain_ref, o_ref, sfin_ref, st_ref):
    j = pl.program_id(1)
    nj = pl.num_programs(1)
    nseq, rows_per_step, _ = qa_ref.shape
    c = GLA_CHUNK
    zpad = jnp.zeros((LANES - DK_A, DV_A), F32)

    @pl.when(j == 0)
    def _():
        for si in range(nseq):
            for h in range(H_A):
                st_ref[si * H_A + h] = jnp.concatenate([s0_ref[si, h], zpad], axis=0).T

    ti = lax.broadcasted_iota(I32, (rows_per_step, rows_per_step), 0)
    si = lax.broadcasted_iota(I32, (rows_per_step, rows_per_step), 1)
    chunk_shift = c.bit_length() - 1
    same_chunk = (ti >> chunk_shift) == (si >> chunk_shift)
    tril_blocks = jnp.where(same_chunk & (si <= ti), 1.0, 0.0).astype(BF16)
    gain = gain_ref[...]
    for si in range(nseq):
        la_hi, la_lo = _split_bf16(la_ref[si])
        b_all = _dot(tril_blocks, la_hi) + _dot(tril_blocks, la_lo)
        for h in range(H_A):
            hp = slice(h * LANES, (h + 1) * LANES)
            hv = slice(h * DV_A, (h + 1) * DV_A)
            st = st_ref[si * H_A + h]
            for ci in range(rows_per_step // c):
                r0 = ci * c
                q = qa_ref[si, r0:r0 + c, hp] * (DK_A ** -0.5)
                o, st = _gla_chunk(q, ka_ref[si, r0:r0 + c, hp], va_ref[si, r0:r0 + c, hv],
                                   b_all[r0:r0 + c, hp], st)
                r = ra_ref[si, r0:r0 + c, hv]
                o = _rms_norm(o, gain) * (r * _sigmoid(r))
                o_ref[si, r0:r0 + c, hv] = o.astype(o_ref.dtype)
            st_ref[si * H_A + h] = st

    @pl.when(j == nj - 1)
    def _():
        for si in range(nseq):
            for h in range(H_A):
                sfin_ref[si, h] = st_ref[si * H_A + h].T[0:DK_A, :]


def _gla(qa, ka, va, ra, la, s0, gain, rows_per_step):
    b, s, pa = qa.shape
    mw = va.shape[-1]
    ns = GLA_SEQS
    assert b % ns == 0
    seq = lambda w: pl.BlockSpec((ns, rows_per_step, w), lambda i, j: (i, j, 0))
    state = pl.BlockSpec((ns, H_A, DK_A, DV_A), lambda i, j: (i, 0, 0, 0))
    return pl.pallas_call(
        _gla_kernel,
        grid=(b // ns, s // rows_per_step),
        in_specs=[seq(pa), seq(pa), seq(mw), seq(mw), seq(pa), state,
                  pl.BlockSpec(gain.shape, lambda i, j: (0, 0))],
        out_specs=[seq(mw), state],
        out_shape=[jax.ShapeDtypeStruct((b, s, mw), BF16),
                   jax.ShapeDtypeStruct((b, H_A, DK_A, DV_A), F32)],
        scratch_shapes=[pltpu.VMEM((ns * H_A, LANES, LANES), F32)],
        compiler_params=_cparams(("arbitrary", "arbitrary")),
        name="gla",
    )(qa, ka, va, ra, la, s0, gain)


def _head_lane_masks():
    lane = lax.broadcasted_iota(I32, (1, LANES), 1)
    return lane < DH_B, lane >= DH_B


def _sb_neg_tri(tk):
    ji = lax.broadcasted_iota(I32, (tk, tk), 0)
    si = lax.broadcasted_iota(I32, (tk, tk), 1)
    return jnp.where(ji >= si, -1.0, 0.0).astype(BF16)


def _sb_stack_queries(q, qs_ref, base=0):
    m0, m1 = _head_lane_masks()
    for p in range(q.shape[1] // LANES):
        qp = (q[:, p * LANES:(p + 1) * LANES].astype(F32) * (DH_B ** -0.5 * LOG2_E)).astype(BF16)
        zero = jnp.zeros_like(qp)
        qs_ref[base + p] = jnp.concatenate([jnp.where(m0, qp, zero), jnp.where(m1, qp, zero)], axis=0)


def _pair_lanes(p):
    return slice(p * LANES, (p + 1) * LANES)


def _lane_fit(x, width):
    if width >= LANES:
        return jnp.concatenate([x] * (width // LANES), axis=1)
    return x[:, 0:width]


def _sb_tile_step(qs_ref, acc_ref, carry_ref, k_tile, v_tile, ntri, diagonal, one_suffix_matmul=False):
    npair, rows, _ = qs_ref.shape
    tq = rows // 2
    tk = ntri.shape[1]
    m0, _ = _head_lane_masks()
    if diagonal:
        t = lax.broadcasted_iota(I32, (rows, tk), 0)
        t = jnp.where(t >= tq, t - tq, t)
        visible = lax.broadcasted_iota(I32, (rows, tk), 1) < t
    def scores(p):
        z = _dot(qs_ref[p], k_tile(p))
        sp = jnp.maximum(z, 0.0) + jnp.log2(1.0 + jnp.exp2(-jnp.abs(z)))
        if diagonal:
            sp = jnp.where(visible, sp, 0.0)
        return z, sp.astype(BF16)

    if one_suffix_matmul:
        zs, sps = zip(*[scores(p) for p in range(npair)])
        stacked = _dot(jnp.concatenate(sps, axis=0), ntri)
        suffixes = [stacked[p * rows:(p + 1) * rows] for p in range(npair)]
    for p in range(npair):
        if one_suffix_matmul:
            z, suffix = zs[p], suffixes[p]
        else:
            z, sp = scores(p)
            suffix = _dot(sp, ntri)
        carry = carry_ref[p]
        w = jnp.exp2(z + suffix + _lane_fit(carry, tk))
        if diagonal:
            w = jnp.where(visible, w, 0.0)
        pv = _dot_nt(w.astype(BF16), v_tile(p))
        acc_ref[p] += jnp.where(m0, pv[0:tq], pv[tq:rows])
        carry_ref[p] = carry + jnp.broadcast_to(suffix[:, 0:1], carry.shape)


def _sb_prompt_kernel(q_ref, k_ref, v_ref, o_ref, qs_ref, acc_ref, carry_ref):
    qi = pl.program_id(1)
    tk = SB_TILE
    _sb_stack_queries(q_ref[0], qs_ref)
    acc_ref[...] = jnp.zeros_like(acc_ref)
    carry_ref[...] = jnp.zeros_like(carry_ref)
    ntri = _sb_neg_tri(tk)

    def step(jb, diagonal):
        _sb_tile_step(qs_ref, acc_ref, carry_ref, lambda p: k_ref[0, jb, _pair_lanes(p), :],
                      lambda p: v_ref[0, jb, _pair_lanes(p), :], ntri, diagonal, True)

    step(qi, True)

    def body(i, c):
        step(qi - 1 - 2 * i, False)
        step(qi - 2 - 2 * i, False)
        return c

    lax.fori_loop(0, qi // 2, body, 0)

    @pl.when(qi % 2 == 1)
    def _():
        step(0, False)

    for p in range(acc_ref.shape[0]):
        o_ref[0, :, p * LANES:(p + 1) * LANES] = acc_ref[p].astype(o_ref.dtype)


def _sb_scratch(tq, npair):
    return [pltpu.VMEM((npair, 2 * tq, LANES), BF16), pltpu.VMEM((npair, tq, LANES), F32),
            pltpu.VMEM((npair, 2 * tq, LANES), F32)]


def _sb_prompt(q, kt, vt):
    b, s, w = q.shape
    tq = SB_TILE
    assert kt.shape == (b, s // tq, w, tq)
    qspec = pl.BlockSpec((1, tq, w), lambda i, j: (i, j, 0))
    kvspec = pl.BlockSpec((1,) + kt.shape[1:], lambda i, j: (i, 0, 0, 0))
    return pl.pallas_call(
        _sb_prompt_kernel,
        grid=(b, s // tq),
        in_specs=[qspec, kvspec, kvspec],
        out_specs=qspec,
        out_shape=jax.ShapeDtypeStruct((b, s, w), BF16),
        scratch_shapes=_sb_scratch(tq, w // LANES),
        compiler_params=_cparams(("arbitrary", "arbitrary")),
        name="sb_prompt",
    )(q, kt, vt)


def _sb_sample_kernel(q_ref, kn_ref, vn_ref, kp_ref, vp_ref, o_ref, qs_ref, acc_ref, carry_ref):
    nseq, sq, w = q_ref.shape
    past = kp_ref.shape[3]
    npair = w // LANES
    tk = SB_TILE
    for si in range(nseq):
        _sb_stack_queries(q_ref[si], qs_ref, si * npair)
    acc_ref[...] = jnp.zeros_like(acc_ref)
    carry_ref[...] = jnp.zeros_like(carry_ref)
    _sb_tile_step(qs_ref, acc_ref, carry_ref, lambda e: kn_ref[e // npair, 0, _pair_lanes(e % npair), :],
                  lambda e: vn_ref[e // npair, 0, _pair_lanes(e % npair), :], _sb_neg_tri(sq), True, True)
    ntri = _sb_neg_tri(tk)

    def body(i, c):
        cols = pl.ds(pl.multiple_of(past - (i + 1) * tk, tk), tk)

        def pair(ref, e):
            p = e % npair
            return ref[e // npair, 2 * p:2 * p + 2, :, cols].reshape(LANES, tk).astype(BF16)

        _sb_tile_step(qs_ref, acc_ref, carry_ref, lambda e: pair(kp_ref, e), lambda e: pair(vp_ref, e), ntri, False,
                      True)
        return c

    lax.fori_loop(0, past // tk, body, 0)
    for e in range(acc_ref.shape[0]):
        o_ref[e // npair, :, _pair_lanes(e % npair)] = acc_ref[e].astype(o_ref.dtype)


def _sb_sample(q, kt_new, vt_new, kt_past, vt_past):
    b, sq, w = q.shape
    past = kt_past.shape[3]
    ns = SB_SAMPLE_SEQS
    assert past % SB_TILE == 0 and 2 * DH_B == LANES and b % ns == 0
    qspec = pl.BlockSpec((ns, sq, w), lambda i: (i, 0, 0))
    new = pl.BlockSpec((ns, 1, w, sq), lambda i: (i, 0, 0, 0))
    old = pl.BlockSpec((ns, H_B, DH_B, past), lambda i: (i, 0, 0, 0))
    return pl.pallas_call(
        _sb_sample_kernel,
        grid=(b // ns,),
        in_specs=[qspec, new, new, old, old],
        out_specs=qspec,
        out_shape=jax.ShapeDtypeStruct((b, sq, w), BF16),
        scratch_shapes=_sb_scratch(sq, ns * (w // LANES)),
        compiler_params=_cparams(("arbitrary",)),
        name="sb_sample",
    )(q, kt_new, vt_new, kt_past, vt_past)


def _first_argmax(vals, nrows):
    idx = lax.broadcasted_iota(I32, vals.shape, 0)
    top = jnp.max(vals, axis=0, keepdims=True)
    first = jnp.min(jnp.where(vals == top, idx, nrows), axis=0, keepdims=True)
    return top, first, idx


def _merge_kernel(oa_ref, ob_ref, g_ref, x_ref, wb0_ref, wb1_ref, wo_ref, gain_ref, wr_ref, br_ref,
                  x1_ref, h2_ref, eid_ref, wcol_ref):
    d = x_ref.shape[1]
    ya = _dot(oa_ref[...], wb0_ref[...])
    yb = _dot(ob_ref[...], wb1_ref[...])
    g = g_ref[...].astype(F32)
    m = _sigmoid(g[:, 0:d]) * ya + _sigmoid(g[:, d:2 * d]) * yb
    x1 = x_ref[...] + _dot(m.astype(BF16), wo_ref[...])
    x1_ref[...] = x1
    h2 = _rms_norm(x1, gain_ref[...])
    h2_ref[...] = h2.astype(h2_ref.dtype).reshape(h2_ref.shape)

    h_hi, h_lo = _split_bf16(h2)
    w_hi, w_lo = _split_bf16(wr_ref[...])
    lt = _dot_nt(w_hi, h_hi) + _dot_nt(w_hi, h_lo) + _dot_nt(w_lo, h_hi) + br_ref[:, 0:1]
    gl = lt[0:N_GROUPS, :]
    g_top, g_idx, _ = _first_argmax(gl, N_GROUPS)
    g_e = jnp.exp(gl - g_top)
    g_p = jnp.max(g_e / jnp.sum(g_e, axis=0, keepdims=True), axis=0, keepdims=True)
    el = jnp.zeros((EXPERTS_PER_GROUP, lt.shape[1]), F32)
    for g in range(N_GROUPS):
        r0 = 8 + g * EXPERTS_PER_GROUP
        el = jnp.where(g_idx == g, lt[r0:r0 + EXPERTS_PER_GROUP, :], el)
    e_top, i1, eidx = _first_argmax(el, EXPERTS_PER_GROUP)
    e_e = jnp.exp(el - e_top)
    e_p = e_e / jnp.sum(e_e, axis=0, keepdims=True)
    p1 = jnp.max(e_p, axis=0, keepdims=True)
    rest = jnp.where(eidx == i1, -1.0, e_p)
    p2, i2, _ = _first_argmax(rest, EXPERTS_PER_GROUP)
    norm = p1 + p2
    w1 = g_p * (p1 / norm)
    w2 = g_p * (p2 / norm)
    eid_ref[...] = jnp.concatenate([g_idx * EXPERTS_PER_GROUP + i1, g_idx * EXPERTS_PER_GROUP + i2], axis=0)
    rows = lax.broadcasted_iota(I32, (LANES, lt.shape[1]), 0)
    wrows = jnp.where(rows == 0, w1, jnp.where(rows == 1, w2, 0.0))
    wcol_ref[...] = wrows.T


def _merge(oa, ob, gbr, x, wb0, wb1, wo, gain, wr, br):
    t, d = x.shape
    tm = MERGE_TILE
    assert t % tm == 0
    row = lambda w: pl.BlockSpec((tm, w), lambda i: (i, 0))
    full = lambda a: pl.BlockSpec(a.shape, lambda i: (0,) * a.ndim)
    return pl.pallas_call(
        _merge_kernel,
        grid=(t // tm,),
        in_specs=[row(oa.shape[1]), row(ob.shape[1]), row(gbr.shape[1]), row(d),
                  full(wb0), full(wb1), full(wo), full(gain), full(wr), full(br)],
        out_specs=[row(d), pl.BlockSpec((tm, d // LANES, LANES), lambda i: (i, 0, 0)),
                   pl.BlockSpec((TOP_K, tm), lambda i: (0, i)), row(LANES)],
        out_shape=[jax.ShapeDtypeStruct((t, d), F32), jax.ShapeDtypeStruct((t, d // LANES, LANES), BF16),
                   jax.ShapeDtypeStruct((TOP_K, t), I32), jax.ShapeDtypeStruct((t, LANES), F32)],
        compiler_params=_cparams(("arbitrary",)),
        name="merge_router",
    )(oa, ob, gbr, x, wb0, wb1, wo, gain, wr, br)


def _positions_kernel(eid_ref, dest_ref, counts_ref, rank_ref):
    nblk, width = eid_ref.shape
    ji = lax.broadcasted_iota(I32, (width, width), 0)
    si = lax.broadcasted_iota(I32, (width, width), 1)
    prefix = jnp.where(ji <= si, 1.0, 0.0).astype(BF16)
    expert = lax.broadcasted_iota(I32, (N_EXPERTS, width), 0)
    group = max(g for g in (8, 4, 2, 1) if nblk % g == 0)

    def onehot(i):
        return expert == eid_ref[pl.ds(i, 1), :]

    def rank_body(ig, run):
        first = pl.multiple_of(ig * group, group)
        ohs = [onehot(first + j) for j in range(group)]
        stacked = jnp.concatenate([jnp.where(oh, 1.0, 0.0) for oh in ohs], axis=0).astype(BF16)
        cum = _dot(stacked, prefix)
        ranks = []
        for j, oh in enumerate(ohs):
            cum_j = cum[j * N_EXPERTS:(j + 1) * N_EXPERTS, :] + run
            ranks.append(jnp.sum(jnp.where(oh, cum_j, 0.0), axis=0, keepdims=True) - 1.0)
            run = cum_j[:, width - 1:width]
        rank_ref[pl.ds(first, group), :] = jnp.concatenate(ranks, axis=0)
        return run

    counts = lax.fori_loop(0, nblk // group, rank_body, jnp.zeros((N_EXPERTS, 1), F32))
    counts_ref[...] = jnp.broadcast_to(counts, counts_ref.shape).astype(I32)
    c_hi = jnp.floor(counts * (1.0 / 256.0))
    c_lo = counts - 256.0 * c_hi
    ei = lax.broadcasted_iota(I32, (N_EXPERTS, N_EXPERTS), 0)
    ej = lax.broadcasted_iota(I32, (N_EXPERTS, N_EXPERTS), 1)
    strict = jnp.where(ej < ei, 1.0, 0.0).astype(BF16)
    digits = jnp.concatenate([jnp.broadcast_to(c_hi, (N_EXPERTS, LANES)),
                              jnp.broadcast_to(c_lo, (N_EXPERTS, LANES))], axis=1).astype(BF16)
    sums = _dot(strict, digits)
    start = 256.0 * sums[:, 0:1] + sums[:, LANES:LANES + 1]

    def dest_body(ig, carry):
        first = pl.multiple_of(ig * group, group)
        offs = [jnp.sum(jnp.where(onehot(first + j), start, 0.0), axis=0, keepdims=True) for j in range(group)]
        rows = pl.ds(first, group)
        dest_ref[rows, :] = (rank_ref[rows, :] + jnp.concatenate(offs, axis=0)).astype(I32)
        return carry

    lax.fori_loop(0, nblk // group, dest_body, 0)


def _positions(eid_blocks):
    nblk, width = eid_blocks.shape
    vm = lambda shape: pl.BlockSpec(shape, lambda: (0,) * len(shape))
    return pl.pallas_call(
        _positions_kernel,
        in_specs=[vm((nblk, width))],
        out_specs=[vm((nblk, width)), vm((N_EXPERTS, LANES))],
        out_shape=[jax.ShapeDtypeStruct((nblk, width), I32), jax.ShapeDtypeStruct((N_EXPERTS, LANES), I32)],
        scratch_shapes=[pltpu.VMEM((nblk, width), F32)],
        name="positions",
    )(eid_blocks)


def _dispatch_kernel(n_prompt_tiles, dest_ref, hp_ref, hs_ref, xs_ref, sem):
    i = pl.program_id(0)
    tm = dest_ref.shape[1]

    def scatter(src_ref):
        def start(r, c):
            for k in range(TOP_K):
                pltpu.make_async_copy(src_ref.at[r], xs_ref.at[dest_ref[k, r]], sem).start(priority=k)
            return c

        lax.fori_loop(0, tm, start, 0, unroll=DMA_UNROLL)
        for k in range(TOP_K):
            pltpu.make_async_copy(src_ref, xs_ref.at[pl.ds(0, tm)], sem).wait()

    @pl.when(i < n_prompt_tiles)
    def _():
        scatter(hp_ref)

    @pl.when(i >= n_prompt_tiles)
    def _():
        scatter(hs_ref)


def _dispatch(dest, h_prompt, h_sample):
    t = dest.shape[1]
    slab = h_prompt.shape[1:]
    tm = DISPATCH_TILE
    assert h_prompt.shape[0] % tm == 0 and h_sample.shape[0] % tm == 0
    npt = h_prompt.shape[0] // tm
    return pl.pallas_call(
        functools.partial(_dispatch_kernel, npt),
        grid=(t // tm,),
        in_specs=[pl.BlockSpec((TOP_K, tm), lambda i: (0, i), memory_space=pltpu.SMEM),
                  pl.BlockSpec((tm,) + slab, lambda i: (jnp.minimum(i, npt - 1), 0, 0)),
                  pl.BlockSpec((tm,) + slab, lambda i: (jnp.maximum(i - npt, 0), 0, 0))],
        out_specs=pl.BlockSpec(memory_space=pl.ANY),
        out_shape=jax.ShapeDtypeStruct((TOP_K * t,) + slab, h_prompt.dtype),
        scratch_shapes=[pltpu.SemaphoreType.DMA(())],
        compiler_params=_cparams(("arbitrary",)),
        name="dispatch",
    )(dest, h_prompt, h_sample)


def _experts_kernel(vblk_ref, vexp_ref, vlo_ref, vhi_ref, xs_ref, wg_ref, wu_ref, wd_ref, ys_ref,
                    wg16_ref, wu16_ref, wd16_ref):
    v = pl.program_id(0)
    lo = vlo_ref[v]
    hi = vhi_ref[v]
    prev = jnp.maximum(v - 1, 0)
    first = jnp.logical_or(v == 0, vblk_ref[v] != vblk_ref[prev])
    new_expert = jnp.logical_or(v == 0, vexp_ref[v] != vexp_ref[prev])

    @pl.when(new_expert)
    def _():
        wg16_ref[...] = wg_ref[0].astype(BF16)
        wu16_ref[...] = wu_ref[0].astype(BF16)
        wd16_ref[...] = wd_ref[0].astype(BF16)

    @pl.when(hi > lo)
    def _():
        tm = xs_ref.shape[0]
        d = wg_ref.shape[1]
        x = xs_ref[...].reshape(tm, d)
        gate = _dot(x, wg16_ref[...])
        up = _dot(x, wu16_ref[...])
        hid = (gate * _sigmoid(gate) * up).astype(BF16)
        y = _dot(hid, wd16_ref[...]).astype(ys_ref.dtype)
        whole = jnp.logical_and(lo == 0, hi == tm)

        @pl.when(whole)
        def _():
            ys_ref[...] = y.reshape(ys_ref.shape)

        @pl.when(jnp.logical_not(whole))
        def _():
            rows = lax.broadcasted_iota(I32, y.shape, 0)
            mine = (rows >= lo) & (rows < hi)

            @pl.when(first)
            def _():
                ys_ref[...] = jnp.where(mine, y, jnp.zeros_like(y)).reshape(ys_ref.shape)

            @pl.when(jnp.logical_not(first))
            def _():
                ys_ref[...] = jnp.where(mine, y, ys_ref[...].reshape(tm, d)).reshape(ys_ref.shape)


def _experts(vblk, vexp, vlo, vhi, xs, wg, wu, wd):
    a = xs.shape[0]
    slab = xs.shape[1:]
    d, de = wg.shape[1:]
    tm = MOE_TILE
    grid_spec = pltpu.PrefetchScalarGridSpec(
        num_scalar_prefetch=4,
        grid=(vblk.shape[0],),
        in_specs=[pl.BlockSpec((tm,) + slab, lambda v, b, e, lo, hi: (b[v], 0, 0)),
                  pl.BlockSpec((1, d, de), lambda v, b, e, lo, hi: (e[v], 0, 0)),
                  pl.BlockSpec((1, d, de), lambda v, b, e, lo, hi: (e[v], 0, 0)),
                  pl.BlockSpec((1, de, d), lambda v, b, e, lo, hi: (e[v], 0, 0))],
        out_specs=pl.BlockSpec((tm,) + slab, lambda v, b, e, lo, hi: (b[v], 0, 0)),
        scratch_shapes=[pltpu.VMEM((d, de), BF16), pltpu.VMEM((d, de), BF16), pltpu.VMEM((de, d), BF16)],
    )
    return pl.pallas_call(
        _experts_kernel,
        grid_spec=grid_spec,
        out_shape=jax.ShapeDtypeStruct((a,) + slab, MOE_OUT_DTYPE),
        compiler_params=_cparams(("arbitrary",)),
        name="experts",
    )(vblk, vexp, vlo, vhi, xs, wg, wu, wd)


def _visit_plan(counts, n_rows):
    tm = MOE_TILE
    nblk = n_rows // tm
    n_visits = nblk + N_EXPERTS - 1
    ends = jnp.cumsum(counts)
    starts = ends - counts
    first_blk = starts // tm
    nvis = jnp.where(counts > 0, (ends + tm - 1) // tm - first_blk, 0)
    vis_end = jnp.cumsum(nvis)
    vis_start = vis_end - nvis
    v = jnp.arange(n_visits, dtype=I32)
    e = jnp.minimum(jnp.sum((vis_end[None, :] <= v[:, None]).astype(I32), axis=1), N_EXPERTS - 1)
    valid = v < vis_end[-1]
    blk = first_blk[e] + (v - vis_start[e])
    lo = jnp.clip(starts[e] - blk * tm, 0, tm)
    hi = jnp.clip(ends[e] - blk * tm, 0, tm)
    last_e = jnp.max(jnp.where(counts > 0, jnp.arange(N_EXPERTS, dtype=I32), 0))
    blk = jnp.where(valid, blk, nblk - 1).astype(I32)
    e = jnp.where(valid, e, last_e).astype(I32)
    lo = jnp.where(valid, lo, 0).astype(I32)
    hi = jnp.where(valid, hi, 0).astype(I32)
    return blk, e, lo, hi


def _combine_kernel(dest_ref, ys_ref, x1_ref, wcol_ref, gain_ref, out_ref, buf_ref, sems):
    tm, d = x1_ref.shape
    part = tm // COMBINE_PARTS

    def start(r, c, sem):
        for k in range(TOP_K):
            pltpu.make_async_copy(ys_ref.at[dest_ref[k, r]], buf_ref.at[k, r], sem).start(priority=k)
        return c

    for h in range(COMBINE_PARTS):
        lax.fori_loop(h * part, (h + 1) * part, functools.partial(start, sem=sems.at[h]), 0, unroll=DMA_UNROLL)
    for h in range(COMBINE_PARTS):
        rows = pl.ds(h * part, part)
        for k in range(TOP_K):
            pltpu.make_async_copy(ys_ref.at[rows], buf_ref.at[k, rows], sems.at[h]).wait()
        y = (wcol_ref[rows, 0:1] * buf_ref[0, rows].reshape(part, d).astype(F32)
             + wcol_ref[rows, 1:2] * buf_ref[1, rows].reshape(part, d).astype(F32))
        out_ref[rows, :] = _rms_norm(x1_ref[rows, :] + y, gain_ref[...])


def _combine(dest, first_token, ys, x1, wcol, gain):
    t, d = x1.shape
    tm = COMBINE_TILE
    assert t % tm == 0 and first_token % tm == 0
    off = first_token // tm
    row = lambda w: pl.BlockSpec((tm, w), lambda i: (i, 0))
    return pl.pallas_call(
        _combine_kernel,
        grid=(t // tm,),
        in_specs=[pl.BlockSpec((TOP_K, tm), lambda i: (0, i + off), memory_space=pltpu.SMEM),
                  pl.BlockSpec(memory_space=pl.ANY), row(d), row(LANES),
                  pl.BlockSpec(gain.shape, lambda i: (0, 0))],
        out_specs=row(d),
        out_shape=jax.ShapeDtypeStruct((t, d), F32),
        scratch_shapes=[pltpu.VMEM((TOP_K, tm) + ys.shape[1:], ys.dtype), pltpu.SemaphoreType.DMA((COMBINE_PARTS,))],
        compiler_params=_cparams(("arbitrary",)),
        name="combine",
    )(dest, ys, x1, wcol, gain)


def _pad_heads(w):
    r = w.shape[0]
    return jnp.pad(w.reshape(r, H_A, DK_A), ((0, 0), (0, 0), (0, LANES - DK_A))).reshape(r, H_A * LANES)


def _prepare_weights(w_in, w_gla_gate_up, b_gla_gate, w_branch, w_out, w_router_group, b_router_group,
                     w_router_expert, b_router_expert):
    d = w_in.shape[0]
    qk = H_A * DK_A
    mw = H_A * DV_A
    c = 0
    w_qa, c = w_in[:, c:c + qk], c + qk
    w_ka, c = w_in[:, c:c + qk], c + qk
    w_va, c = w_in[:, c:c + mw], c + mw
    w_ra, c = w_in[:, c:c + mw], c + mw
    w_lr, c = w_in[:, c:c + GATE_RANK], c + GATE_RANK
    w_b, c = w_in[:, c:c + 3 * mw], c + 3 * mw
    w_g = w_in[:, c:]
    wa = jnp.concatenate([_pad_heads(w_qa), _pad_heads(w_ka), w_va, w_ra,
                          jnp.pad(w_lr, ((0, 0), (0, LANES - GATE_RANK)))], axis=1).astype(BF16)
    wgu = jnp.pad(_pad_heads(w_gla_gate_up), ((0, LANES - GATE_RANK), (0, 0))).astype(BF16)
    bgu = _pad_heads(b_gla_gate[None, :])
    wr = jnp.zeros((LANES, d), F32)
    wr = wr.at[0:N_GROUPS].set(w_router_group.T).at[8:8 + N_EXPERTS].set(w_router_expert.T)
    br = jnp.zeros((LANES,), F32).at[0:N_GROUPS].set(b_router_group).at[8:8 + N_EXPERTS].set(b_router_expert)
    br = jnp.broadcast_to(br[:, None], (LANES, LANES))
    return dict(wa=wa, wqb=w_b[:, 0:mw].astype(BF16), wkvt=w_b[:, mw:3 * mw].T.astype(BF16),
                wg=w_g.astype(BF16), wgu=wgu, bgu=bgu,
                wb0=w_branch[0].astype(BF16), wb1=w_branch[1].astype(BF16), wo=w_out.astype(BF16),
                wr=wr, br=br)


def _mixers(x, s0, k_past, v_past, w, norm_mix_gain, gla_norm_gain, norm_ffn_gain):
    b, s, d = x.shape
    xf = x.reshape(b * s, d)
    qa, ka, va, ra, la, qb, kt, vt, kt16, vt16, gbr = _in_projection(
        xf, s, norm_mix_gain[None, :], w["wa"], w["wqb"], w["wkvt"], w["wg"], w["wgu"], w["bgu"])
    seq = lambda a: a.reshape(b, s, a.shape[-1])
    oa, s_new = _gla(seq(qa), seq(ka), seq(va), seq(ra), seq(la), s0, gla_norm_gain[None, :],
                     min(s, ROW_TILE))
    to_channel_major = lambda a: jnp.transpose(a, (0, 2, 3, 1))
    if k_past is None:
        ob = _sb_prompt(seq(qb), kt16, vt16)
    else:
        ob = _sb_sample(seq(qb), kt16, vt16, to_channel_major(k_past), to_channel_major(v_past))
    x1, h2, eid, wcol = _merge(oa.reshape(b * s, -1), ob.reshape(b * s, -1), gbr, xf, w["wb0"], w["wb1"],
                               w["wo"], norm_ffn_gain[None, :], w["wr"], w["br"])
    from_channel_major = lambda a: jnp.transpose(a.reshape(b, H_B, DH_B, s), (0, 3, 1, 2))
    return x1, h2, eid, wcol, s_new, from_channel_major(kt), from_channel_major(vt)


def kernel(x_prompt, x_sample, state_gla, cache_sb_k, cache_sb_v, norm_mix_gain, w_in, w_gla_gate_up, b_gla_gate, gla_norm_gain, w_branch, w_out, norm_ffn_gain, w_router_group, b_router_group, w_router_expert, b_router_expert, w_exp_gate, w_exp_up, w_exp_down, norm_final_gain):
    depth = w_in.shape[0]
    assert depth == 1, "one trunk layer per step"
    l = 0
    w = _prepare_weights(w_in[l], w_gla_gate_up[l], b_gla_gate[l], w_branch[l], w_out[l], w_router_group[l],
                         b_router_group[l], w_router_expert[l], b_router_expert[l])
    bp, sp, d = x_prompt.shape
    bs, ss, _ = x_sample.shape
    s0 = jnp.zeros((bp, H_A, DK_A, DV_A), x_prompt.dtype)
    x1p, h2p, eidp, wcolp, gla_p, k_p, v_p = _mixers(
        x_prompt, s0, None, None, w, norm_mix_gain[l], gla_norm_gain[l], norm_ffn_gain[l])
    x1s, h2s, eids, wcols, gla_s, k_s, v_s = _mixers(
        x_sample, state_gla[l], cache_sb_k[l], cache_sb_v[l], w, norm_mix_gain[l], gla_norm_gain[l],
        norm_ffn_gain[l])

    tp, ts = bp * sp, bs * ss
    eid = jnp.concatenate([eidp, eids], axis=1)
    dest_blocks, counts = _positions(eid.reshape(-1, SORT_WIDTH))
    dest = dest_blocks.reshape(TOP_K, tp + ts)
    xs = _dispatch(dest, h2p, h2s)
    vblk, vexp, vlo, vhi = _visit_plan(counts[:, 0], TOP_K * (tp + ts))
    ys = _experts(vblk, vexp, vlo, vhi, xs, w_exp_gate[l], w_exp_up[l], w_exp_down[l])
    gf = norm_final_gain[None, :]
    y_prompt = _combine(dest, 0, ys, x1p, wcolp, gf).reshape(bp, sp, d)
    y_sample = _combine(dest, tp, ys, x1s, wcols, gf).reshape(bs, ss, d)
    return (y_prompt, y_sample, gla_p[None], k_p[None], v_p[None], gla_s[None], k_s[None], v_s[None])
```

```python
import functools

import jax
import jax.numpy as jnp
from jax import lax
from jax.experimental import pallas as pl
from jax.experimental.pallas import tpu as pltpu

F32 = jnp.float32
BF16 = jnp.bfloat16
MOE_OUT_DTYPE = jnp.bfloat16
I32 = jnp.int32

LANES = 128
LOG2_E = 1.4426950408889634
RMS_EPS = 1e-6
GATE_TAU = 16.0
H_A = 4
DK_A = 64
DV_A = 128
GATE_RANK = 16
H_B = 8
DH_B = 64
N_GROUPS = 4
EXPERTS_PER_GROUP = 8
N_EXPERTS = N_GROUPS * EXPERTS_PER_GROUP
TOP_K = 2
GLA_CHUNK = 64
GLA_SUB = 16
GLA_EXP_CLAMP = 80.0
GLA_SEQS = 4
ROW_TILE = 256
SB_TILE = 256
SB_SAMPLE_SEQS = 2
MOE_TILE = 512
SORT_WIDTH = 256
INPROJ_TILE = 512
MERGE_TILE = 512
DISPATCH_TILE = 1024
COMBINE_TILE = 512
COMBINE_PARTS = 4
DMA_UNROLL = 8
VMEM_LIMIT = 56 * 1024 * 1024


def _cparams(sem):
    return pltpu.CompilerParams(dimension_semantics=sem, vmem_limit_bytes=VMEM_LIMIT)


def _dot(a, b):
    return jnp.dot(a, b, preferred_element_type=F32)


def _dot_nt(a, b):
    return lax.dot_general(a, b, (((1,), (1,)), ((), ())), preferred_element_type=F32)


def _dot_tn(a, b):
    return lax.dot_general(a, b, (((0,), (0,)), ((), ())), preferred_element_type=F32)


def _split_bf16(x):
    hi = x.astype(BF16)
    lo = (x - hi.astype(F32)).astype(BF16)
    return hi, lo


def _log_sigmoid(x):
    return jnp.minimum(x, 0.0) - jnp.log(1.0 + jnp.exp(-jnp.abs(x)))


def _sigmoid(x):
    return 1.0 / (1.0 + jnp.exp(-x))


def _rms_norm(x, gain):
    return x * lax.rsqrt(jnp.mean(x * x, axis=-1, keepdims=True) + RMS_EPS) * gain


def _inproj_kernel(x_ref, gain_ref, wa_ref, wqb_ref, wkvt_ref, wg_ref, wgu_ref, bgu_ref,
                   qa_ref, ka_ref, va_ref, ra_ref, la_ref, qb_ref, kt_ref, vt_ref,
                   kt16_ref, vt16_ref, gbr_ref):
    h = _rms_norm(x_ref[...], gain_ref[...]).astype(BF16)
    pa = H_A * DK_A
    mw = va_ref.shape[-1]
    kvt = _dot_nt(wkvt_ref[...], h)
    nseq, _, s = kt_ref.shape
    ntile, tile = kt16_ref.shape[1], kt16_ref.shape[3]
    for i in range(nseq):
        kt_ref[i] = kvt[0:mw, i * s:(i + 1) * s]
        vt_ref[i] = kvt[mw:2 * mw, i * s:(i + 1) * s]
        for j in range(ntile):
            cols = slice(i * s + j * tile, i * s + (j + 1) * tile)
            kt16_ref[i, j] = kvt[0:mw, cols].astype(BF16)
            vt16_ref[i, j] = kvt[mw:2 * mw, cols].astype(BF16)
    qb_ref[...] = _dot(h, wqb_ref[...]).astype(BF16)
    qa_ref[...] = _dot(h, wa_ref[:, 0:pa])
    ka_ref[...] = _dot(h, wa_ref[:, pa:2 * pa])
    va_ref[...] = _dot(h, wa_ref[:, 2 * pa:2 * pa + mw])
    ra_ref[...] = _dot(h, wa_ref[:, 2 * pa + mw:2 * pa + 2 * mw])
    lr = _dot(h, wa_ref[:, 2 * pa + 2 * mw:2 * pa + 2 * mw + LANES])
    gl = _dot(lr.astype(BF16), wgu_ref[...]) + bgu_ref[...]
    la_ref[...] = _log_sigmoid(gl) / GATE_TAU
    gbr_ref[...] = _dot(h, wg_ref[...]).astype(gbr_ref.dtype)


def _in_projection(x, seq_len, gain, wa, wqb, wkvt, wg, wgu, bgu):
    t, d = x.shape
    nb = t // seq_len
    pa = H_A * DK_A
    mw = wqb.shape[1]
    tm = INPROJ_TILE
    assert t % tm == 0
    row = lambda w: pl.BlockSpec((tm, w), lambda i: (i, 0))
    full = lambda a: pl.BlockSpec(a.shape, lambda i: (0,) * a.ndim, pipeline_mode=pl.Buffered(1))
    if seq_len >= tm:
        per_seq = seq_len // tm
        ntile = tm // SB_TILE
        assert tm % SB_TILE == 0 and seq_len % tm == 0
        kt_spec = pl.BlockSpec((1, mw, tm), lambda i: (i // per_seq, 0, i % per_seq))
        kt16_spec = pl.BlockSpec((1, ntile, mw, SB_TILE), lambda i: (i // per_seq, i % per_seq, 0, 0))
        kt16_shape = (nb, seq_len // SB_TILE, mw, SB_TILE)
    else:
        nseq = tm // seq_len
        assert tm % seq_len == 0
        kt_spec = pl.BlockSpec((nseq, mw, seq_len), lambda i: (i, 0, 0))
        kt16_spec = pl.BlockSpec((nseq, 1, mw, seq_len), lambda i: (i, 0, 0, 0))
        kt16_shape = (nb, 1, mw, seq_len)
    outs = [
        (jax.ShapeDtypeStruct((t, pa), F32), row(pa)), (jax.ShapeDtypeStruct((t, pa), F32), row(pa)),
        (jax.ShapeDtypeStruct((t, mw), F32), row(mw)), (jax.ShapeDtypeStruct((t, mw), F32), row(mw)),
        (jax.ShapeDtypeStruct((t, pa), F32), row(pa)),
        (jax.ShapeDtypeStruct((t, mw), BF16), row(mw)),
        (jax.ShapeDtypeStruct((nb, mw, seq_len), F32), kt_spec), (jax.ShapeDtypeStruct((nb, mw, seq_len), F32), kt_spec),
        (jax.ShapeDtypeStruct(kt16_shape, BF16), kt16_spec), (jax.ShapeDtypeStruct(kt16_shape, BF16), kt16_spec),
        (jax.ShapeDtypeStruct((t, wg.shape[1]), BF16), row(wg.shape[1])),
    ]
    return pl.pallas_call(
        _inproj_kernel,
        grid=(t // tm,),
        in_specs=[row(d), full(gain), full(wa), full(wqb), full(wkvt), full(wg), full(wgu), full(bgu)],
        out_specs=[spec for _, spec in outs],
        out_shape=[shape for shape, _ in outs],
        compiler_params=_cparams(("arbitrary",)),
        name="in_projection",
    )(x, gain, wa, wqb, wkvt, wg, wgu, bgu)


def _gla_chunk(q, k, v, b, st):
    c = q.shape[0]
    b_last = b[c - 1:c, :]
    rows = lax.broadcasted_iota(I32, (c, LANES), 0)
    nsub = c // GLA_SUB
    refs = [jnp.zeros((1, LANES), F32)] + [b[i * GLA_SUB - 1:i * GLA_SUB, :] for i in range(1, nsub)]
    ref_rows = refs[0]
    for i in range(1, nsub):
        ref_rows = jnp.where(rows >= i * GLA_SUB, refs[i], ref_rows)
    q_rel = q * jnp.exp(b - ref_rows)
    lhs = jnp.concatenate(
        [jnp.where((rows >= i * GLA_SUB) & (rows < (i + 1) * GLA_SUB), q_rel, 0.0) for i in range(nsub)],
        axis=1).astype(BF16)
    rhs = jnp.concatenate(
        [jnp.where(rows < (i + 1) * GLA_SUB, k * jnp.exp(jnp.minimum(refs[i] - b, GLA_EXP_CLAMP)), 0.0)
         for i in range(nsub)], axis=1).astype(BF16)
    att = _dot_nt(lhs, rhs)
    tt = lax.broadcasted_iota(I32, (c, c), 0)
    ss = lax.broadcasted_iota(I32, (c, c), 1)
    att = jnp.where(ss <= tt, att, 0.0)
    v16 = v.astype(BF16)
    inter = _dot_nt((q * jnp.exp(b)).astype(BF16), st.astype(BF16))
    intra = _dot(att.astype(BF16), v16)
    kd = (k * jnp.exp(b_last - b)).astype(BF16)
    st_new = st * jnp.exp(b_last) + _dot_tn(v16, kd)
    return inter + intra, st_new


def _gla_kernel(qa_ref, ka_ref, va_ref, ra_ref, la_ref, s0_ref, gain_ref, o_ref, sfin_ref, st_ref):
    j = pl.program_id(1)
    nj = pl.num_programs(1)
    nseq, rows_per_step, _ = qa_ref.shape
    c = GLA_CHUNK
    zpad = jnp.zeros((LANES - DK_A, DV_A), F32)

    def state_rows(h):
        return slice((h % 2) * DK_A, (h % 2 + 1) * DK_A)

    @pl.when(j == 0)
    def _():
        for si in range(nseq):
            for h in range(H_A):
                parts = [s0_ref[si, h], zpad] if h % 2 == 0 else [zpad, s0_ref[si, h]]
                st_ref[si * H_A + h] = jnp.concatenate(parts, axis=0).T

    ti = lax.broadcasted_iota(I32, (rows_per_step, rows_per_step), 0)
    si = lax.broadcasted_iota(I32, (rows_per_step, rows_per_step), 1)
    chunk_shift = c.bit_length() - 1
    same_chunk = (ti >> chunk_shift) == (si >> chunk_shift)
    tril_blocks = jnp.where(same_chunk & (si <= ti), 1.0, 0.0).astype(BF16)
    gain = gain_ref[...]
    lane = lax.broadcasted_iota(I32, (1, LANES), 1)
    half_masks = (lane < DK_A, lane >= DK_A)
    for si in range(nseq):
        la_hi, la_lo = _split_bf16(la_ref[si])
        b_all = _dot(tril_blocks, la_hi) + _dot(tril_blocks, la_lo)
        for h in range(H_A):
            hp = slice((h // 2) * LANES, (h // 2 + 1) * LANES)
            hv = slice(h * DV_A, (h + 1) * DV_A)
            mine = half_masks[h % 2]
            st = st_ref[si * H_A + h]
            for ci in range(rows_per_step // c):
                r0 = ci * c
                q = jnp.where(mine, qa_ref[si, r0:r0 + c, hp], 0.0) * (DK_A ** -0.5)
                k = jnp.where(mine, ka_ref[si, r0:r0 + c, hp], 0.0)
                o, st = _gla_chunk(q, k, va_ref[si, r0:r0 + c, hv], b_all[r0:r0 + c, hp], st)
                r = ra_ref[si, r0:r0 + c, hv]
                o = _rms_norm(o, gain) * (r * _sigmoid(r))
                o_ref[si, r0:r0 + c, hv] = o.astype(o_ref.dtype)
            st_ref[si * H_A + h] = st

    @pl.when(j == nj - 1)
    def _():
        for si in range(nseq):
            for h in range(H_A):
                sfin_ref[si, h] = st_ref[si * H_A + h].T[state_rows(h), :]


def _gla(qa, ka, va, ra, la, s0, gain, rows_per_step):
    assert 2 * DK_A == LANES and H_A % 2 == 0
    b, s, pa = qa.shape
    mw = va.shape[-1]
    ns = GLA_SEQS
    assert b % ns == 0
    seq = lambda w: pl.BlockSpec((ns, rows_per_step, w), lambda i, j: (i, j, 0))
    state = pl.BlockSpec((ns, H_A, DK_A, DV_A), lambda i, j: (i, 0, 0, 0))
    return pl.pallas_call(
        _gla_kernel,
        grid=(b // ns, s // rows_per_step),
        in_specs=[seq(pa), seq(pa), seq(mw), seq(mw), seq(pa), state,
                  pl.BlockSpec(gain.shape, lambda i, j: (0, 0))],
        out_specs=[seq(mw), state],
        out_shape=[jax.ShapeDtypeStruct((b, s, mw), BF16),
                   jax.ShapeDtypeStruct((b, H_A, DK_A, DV_A), F32)],
        scratch_shapes=[pltpu.VMEM((ns * H_A, LANES, LANES), F32)],
        compiler_params=_cparams(("arbitrary", "arbitrary")),
        name="gla",
    )(qa, ka, va, ra, la, s0, gain)


def _head_lane_masks():
    lane = lax.broadcasted_iota(I32, (1, LANES), 1)
    return lane < DH_B, lane >= DH_B


def _sb_neg_tri(tk):
    ji = lax.broadcasted_iota(I32, (tk, tk), 0)
    si = lax.broadcasted_iota(I32, (tk, tk), 1)
    return jnp.where(ji >= si, -1.0, 0.0).astype(BF16)


def _sb_stack_queries(q, qs_ref, base=0):
    m0, m1 = _head_lane_masks()
    for p in range(q.shape[1] // LANES):
        qp = (q[:, p * LANES:(p + 1) * LANES].astype(F32) * (DH_B ** -0.5 * LOG2_E)).astype(BF16)
        zero = jnp.zeros_like(qp)
        qs_ref[base + p] = jnp.concatenate([jnp.where(m0, qp, zero), jnp.where(m1, qp, zero)], axis=0)


def _pair_lanes(p):
    return slice(p * LANES, (p + 1) * LANES)


def _lane_fit(x, width):
    if width >= LANES:
        return jnp.concatenate([x] * (width // LANES), axis=1)
    return x[:, 0:width]


def _sb_tile_step(qs_ref, acc_ref, carry_ref, k_tile, v_tile, ntri, diagonal, one_suffix_matmul=False):
    npair, rows, _ = qs_ref.shape
    tq = rows // 2
    tk = ntri.shape[1]
    m0, _ = _head_lane_masks()
    if diagonal:
        t = lax.broadcasted_iota(I32, (rows, tk), 0)
        t = jnp.where(t >= tq, t - tq, t)
        visible = lax.broadcasted_iota(I32, (rows, tk), 1) < t
    def scores(p):
        z = _dot(qs_ref[p], k_tile(p))
        sp = jnp.maximum(z, 0.0) + jnp.log2(1.0 + jnp.exp2(-jnp.abs(z)))
        if diagonal:
            sp = jnp.where(visible, sp, 0.0)
        return z, sp.astype(BF16)

    if one_suffix_matmul:
        zs, sps = zip(*[scores(p) for p in range(npair)])
        stacked = _dot(jnp.concatenate(sps, axis=0), ntri)
        suffixes = [stacked[p * rows:(p + 1) * rows] for p in range(npair)]
    for p in range(npair):
        if one_suffix_matmul:
            z, suffix = zs[p], suffixes[p]
        else:
            z, sp = scores(p)
            suffix = _dot(sp, ntri)
        carry = carry_ref[p]
        w = jnp.exp2(z + suffix + _lane_fit(carry, tk))
        if diagonal:
            w = jnp.where(visible, w, 0.0)
        pv = _dot_nt(w.astype(BF16), v_tile(p))
        acc_ref[p] += jnp.where(m0, pv[0:tq], pv[tq:rows])
        carry_ref[p] = carry + jnp.broadcast_to(suffix[:, 0:1], carry.shape)


def _sb_prompt_kernel(q_ref, k_ref, v_ref, o_ref, qs_ref, acc_ref, carry_ref):
    qi = pl.program_id(1)
    tk = SB_TILE
    _sb_stack_queries(q_ref[0], qs_ref)
    acc_ref[...] = jnp.zeros_like(acc_ref)
    carry_ref[...] = jnp.zeros_like(carry_ref)
    ntri = _sb_neg_tri(tk)

    def step(jb, diagonal):
        _sb_tile_step(qs_ref, acc_ref, carry_ref, lambda p: k_ref[0, jb, _pair_lanes(p), :],
                      lambda p: v_ref[0, jb, _pair_lanes(p), :], ntri, diagonal, True)

    step(qi, True)

    def body(i, c):
        step(qi - 1 - 2 * i, False)
        step(qi - 2 - 2 * i, False)
        return c

    lax.fori_loop(0, qi // 2, body, 0)

    @pl.when(qi % 2 == 1)
    def _():
        step(0, False)

    for p in range(acc_ref.shape[0]):
        o_ref[0, :, p * LANES:(p + 1) * LANES] = acc_ref[p].astype(o_ref.dtype)


def _sb_scratch(tq, npair):
    return [pltpu.VMEM((npair, 2 * tq, LANES), BF16), pltpu.VMEM((npair, tq, LANES), F32),
            pltpu.VMEM((npair, 2 * tq, LANES), F32)]


def _sb_prompt(q, kt, vt):
    b, s, w = q.shape
    tq = SB_TILE
    assert kt.shape == (b, s // tq, w, tq)
    qspec = pl.BlockSpec((1, tq, w), lambda i, j: (i, j, 0))
    kvspec = pl.BlockSpec((1,) + kt.shape[1:], lambda i, j: (i, 0, 0, 0))
    return pl.pallas_call(
        _sb_prompt_kernel,
        grid=(b, s // tq),
        in_specs=[qspec, kvspec, kvspec],
        out_specs=qspec,
        out_shape=jax.ShapeDtypeStruct((b, s, w), BF16),
        scratch_shapes=_sb_scratch(tq, w // LANES),
        compiler_params=_cparams(("arbitrary", "arbitrary")),
        name="sb_prompt",
    )(q, kt, vt)


def _sb_sample_kernel(q_ref, kn_ref, vn_ref, kp_ref, vp_ref, o_ref, qs_ref, acc_ref, carry_ref):
    nseq, sq, w = q_ref.shape
    past = kp_ref.shape[3]
    npair = w // LANES
    tk = SB_TILE
    for si in range(nseq):
        _sb_stack_queries(q_ref[si], qs_ref, si * npair)
    acc_ref[...] = jnp.zeros_like(acc_ref)
    carry_ref[...] = jnp.zeros_like(carry_ref)
    _sb_tile_step(qs_ref, acc_ref, carry_ref, lambda e: kn_ref[e // npair, 0, _pair_lanes(e % npair), :],
                  lambda e: vn_ref[e // npair, 0, _pair_lanes(e % npair), :], _sb_neg_tri(sq), True, True)
    ntri = _sb_neg_tri(tk)

    def body(i, c):
        cols = pl.ds(pl.multiple_of(past - (i + 1) * tk, tk), tk)

        def pair(ref, e):
            p = e % npair
            return ref[e // npair, 2 * p:2 * p + 2, :, cols].reshape(LANES, tk).astype(BF16)

        _sb_tile_step(qs_ref, acc_ref, carry_ref, lambda e: pair(kp_ref, e), lambda e: pair(vp_ref, e), ntri, False,
                      True)
        return c

    lax.fori_loop(0, past // tk, body, 0)
    for e in range(acc_ref.shape[0]):
        o_ref[e // npair, :, _pair_lanes(e % npair)] = acc_ref[e].astype(o_ref.dtype)


def _sb_sample(q, kt_new, vt_new, kt_past, vt_past):
    b, sq, w = q.shape
    past = kt_past.shape[3]
    ns = SB_SAMPLE_SEQS
    assert past % SB_TILE == 0 and 2 * DH_B == LANES and b % ns == 0
    qspec = pl.BlockSpec((ns, sq, w), lambda i: (i, 0, 0))
    new = pl.BlockSpec((ns, 1, w, sq), lambda i: (i, 0, 0, 0))
    old = pl.BlockSpec((ns, H_B, DH_B, past), lambda i: (i, 0, 0, 0))
    return pl.pallas_call(
        _sb_sample_kernel,
        grid=(b // ns,),
        in_specs=[qspec, new, new, old, old],
        out_specs=qspec,
        out_shape=jax.ShapeDtypeStruct((b, sq, w), BF16),
        scratch_shapes=_sb_scratch(sq, ns * (w // LANES)),
        compiler_params=_cparams(("arbitrary",)),
        name="sb_sample",
    )(q, kt_new, vt_new, kt_past, vt_past)


def _first_argmax(vals, nrows):
    idx = lax.broadcasted_iota(I32, vals.shape, 0)
    top = jnp.max(vals, axis=0, keepdims=True)
    first = jnp.min(jnp.where(vals == top, idx, nrows), axis=0, keepdims=True)
    return top, first, idx


def _merge_kernel(oa_ref, ob_ref, g_ref, x_ref, wb0_ref, wb1_ref, wo_ref, gain_ref, wr_ref, br_ref,
                  x1_ref, h2_ref, eid_ref, wcol_ref):
    d = x_ref.shape[1]
    ya = _dot(oa_ref[...], wb0_ref[...])
    yb = _dot(ob_ref[...], wb1_ref[...])
    g = g_ref[...].astype(F32)
    m = _sigmoid(g[:, 0:d]) * ya + _sigmoid(g[:, d:2 * d]) * yb
    x1 = x_ref[...] + _dot(m.astype(BF16), wo_ref[...])
    x1_ref[...] = x1
    h2 = _rms_norm(x1, gain_ref[...])
    h2_ref[...] = h2.astype(h2_ref.dtype).reshape(h2_ref.shape)

    h_hi, h_lo = _split_bf16(h2)
    w_hi, w_lo = _split_bf16(wr_ref[...])
    lt = _dot_nt(w_hi, h_hi) + _dot_nt(w_hi, h_lo) + _dot_nt(w_lo, h_hi) + br_ref[:, 0:1]
    gl = lt[0:N_GROUPS, :]
    g_top, g_idx, _ = _first_argmax(gl, N_GROUPS)
    g_e = jnp.exp(gl - g_top)
    g_p = jnp.max(g_e / jnp.sum(g_e, axis=0, keepdims=True), axis=0, keepdims=True)
    el = jnp.zeros((EXPERTS_PER_GROUP, lt.shape[1]), F32)
    for g in range(N_GROUPS):
        r0 = 8 + g * EXPERTS_PER_GROUP
        el = jnp.where(g_idx == g, lt[r0:r0 + EXPERTS_PER_GROUP, :], el)
    e_top, i1, eidx = _first_argmax(el, EXPERTS_PER_GROUP)
    e_e = jnp.exp(el - e_top)
    e_p = e_e / jnp.sum(e_e, axis=0, keepdims=True)
    p1 = jnp.max(e_p, axis=0, keepdims=True)
    rest = jnp.where(eidx == i1, -1.0, e_p)
    p2, i2, _ = _first_argmax(rest, EXPERTS_PER_GROUP)
    norm = p1 + p2
    w1 = g_p * (p1 / norm)
    w2 = g_p * (p2 / norm)
    eid_ref[...] = jnp.concatenate([g_idx * EXPERTS_PER_GROUP + i1, g_idx * EXPERTS_PER_GROUP + i2], axis=0)
    rows = lax.broadcasted_iota(I32, (LANES, lt.shape[1]), 0)
    wrows = jnp.where(rows == 0, w1, jnp.where(rows == 1, w2, 0.0))
    wcol_ref[...] = wrows.T


def _merge(oa, ob, gbr, x, wb0, wb1, wo, gain, wr, br):
    t, d = x.shape
    tm = MERGE_TILE
    assert t % tm == 0
    row = lambda w: pl.BlockSpec((tm, w), lambda i: (i, 0))
    full = lambda a: pl.BlockSpec(a.shape, lambda i: (0,) * a.ndim)
    return pl.pallas_call(
        _merge_kernel,
        grid=(t // tm,),
        in_specs=[row(oa.shape[1]), row(ob.shape[1]), row(gbr.shape[1]), row(d),
                  full(wb0), full(wb1), full(wo), full(gain), full(wr), full(br)],
        out_specs=[row(d), pl.BlockSpec((tm, d // LANES, LANES), lambda i: (i, 0, 0)),
                   pl.BlockSpec((TOP_K, tm), lambda i: (0, i)), row(LANES)],
        out_shape=[jax.ShapeDtypeStruct((t, d), F32), jax.ShapeDtypeStruct((t, d // LANES, LANES), BF16),
                   jax.ShapeDtypeStruct((TOP_K, t), I32), jax.ShapeDtypeStruct((t, LANES), F32)],
        compiler_params=_cparams(("arbitrary",)),
        name="merge_router",
    )(oa, ob, gbr, x, wb0, wb1, wo, gain, wr, br)


def _positions_kernel(eid_ref, dest_ref, counts_ref, rank_ref):
    nblk, width = eid_ref.shape
    ji = lax.broadcasted_iota(I32, (width, width), 0)
    si = lax.broadcasted_iota(I32, (width, width), 1)
    prefix = jnp.where(ji <= si, 1.0, 0.0).astype(BF16)
    expert = lax.broadcasted_iota(I32, (N_EXPERTS, width), 0)
    group = max(g for g in (8, 4, 2, 1) if nblk % g == 0)

    def onehot(i):
        return expert == eid_ref[pl.ds(i, 1), :]

    def rank_body(ig, run):
        first = pl.multiple_of(ig * group, group)
        ohs = [onehot(first + j) for j in range(group)]
        stacked = jnp.concatenate([jnp.where(oh, 1.0, 0.0) for oh in ohs], axis=0).astype(BF16)
        cum = _dot(stacked, prefix)
        ranks = []
        for j, oh in enumerate(ohs):
            cum_j = cum[j * N_EXPERTS:(j + 1) * N_EXPERTS, :] + run
            ranks.append(jnp.sum(jnp.where(oh, cum_j, 0.0), axis=0, keepdims=True) - 1.0)
            run = cum_j[:, width - 1:width]
        rank_ref[pl.ds(first, group), :] = jnp.concatenate(ranks, axis=0)
        return run

    counts = lax.fori_loop(0, nblk // group, rank_body, jnp.zeros((N_EXPERTS, 1), F32))
    counts_ref[...] = jnp.broadcast_to(counts, counts_ref.shape).astype(I32)
    c_hi = jnp.floor(counts * (1.0 / 256.0))
    c_lo = counts - 256.0 * c_hi
    ei = lax.broadcasted_iota(I32, (N_EXPERTS, N_EXPERTS), 0)
    ej = lax.broadcasted_iota(I32, (N_EXPERTS, N_EXPERTS), 1)
    strict = jnp.where(ej < ei, 1.0, 0.0).astype(BF16)
    digits = jnp.concatenate([jnp.broadcast_to(c_hi, (N_EXPERTS, LANES)),
                              jnp.broadcast_to(c_lo, (N_EXPERTS, LANES))], axis=1).astype(BF16)
    sums = _dot(strict, digits)
    start = 256.0 * sums[:, 0:1] + sums[:, LANES:LANES + 1]

    def dest_body(ig, carry):
        first = pl.multiple_of(ig * group, group)
        offs = [jnp.sum(jnp.where(onehot(first + j), start, 0.0), axis=0, keepdims=True) for j in range(group)]
        rows = pl.ds(first, group)
        dest_ref[rows, :] = (rank_ref[rows, :] + jnp.concatenate(offs, axis=0)).astype(I32)
        return carry

    lax.fori_loop(0, nblk // group, dest_body, 0)


def _positions(eid_blocks):
    nblk, width = eid_blocks.shape
    vm = lambda shape: pl.BlockSpec(shape, lambda: (0,) * len(shape))
    return pl.pallas_call(
        _positions_kernel,
        in_specs=[vm((nblk, width))],
        out_specs=[vm((nblk, width)), vm((N_EXPERTS, LANES))],
        out_shape=[jax.ShapeDtypeStruct((nblk, width), I32), jax.ShapeDtypeStruct((N_EXPERTS, LANES), I32)],
        scratch_shapes=[pltpu.VMEM((nblk, width), F32)],
        name="positions",
    )(eid_blocks)


def _dispatch_kernel(n_prompt_tiles, dest_ref, hp_ref, hs_ref, xs_ref, sem):
    i = pl.program_id(0)
    tm = dest_ref.shape[1]

    def scatter(src_ref):
        def start(r, c):
            for k in range(TOP_K):
                pltpu.make_async_copy(src_ref.at[r], xs_ref.at[dest_ref[k, r]], sem).start(priority=k)
            return c

        lax.fori_loop(0, tm, start, 0, unroll=DMA_UNROLL)
        for k in range(TOP_K):
            pltpu.make_async_copy(src_ref, xs_ref.at[pl.ds(0, tm)], sem).wait()

    @pl.when(i < n_prompt_tiles)
    def _():
        scatter(hp_ref)

    @pl.when(i >= n_prompt_tiles)
    def _():
        scatter(hs_ref)


def _dispatch(dest, h_prompt, h_sample):
    t = dest.shape[1]
    slab = h_prompt.shape[1:]
    tm = DISPATCH_TILE
    assert h_prompt.shape[0] % tm == 0 and h_sample.shape[0] % tm == 0
    npt = h_prompt.shape[0] // tm
    return pl.pallas_call(
        functools.partial(_dispatch_kernel, npt),
        grid=(t // tm,),
        in_specs=[pl.BlockSpec((TOP_K, tm), lambda i: (0, i), memory_space=pltpu.SMEM),
                  pl.BlockSpec((tm,) + slab, lambda i: (jnp.minimum(i, npt - 1), 0, 0)),
                  pl.BlockSpec((tm,) + slab, lambda i: (jnp.maximum(i - npt, 0), 0, 0))],
        out_specs=pl.BlockSpec(memory_space=pl.ANY),
        out_shape=jax.ShapeDtypeStruct((TOP_K * t,) + slab, h_prompt.dtype),
        scratch_shapes=[pltpu.SemaphoreType.DMA(())],
        compiler_params=_cparams(("arbitrary",)),
        name="dispatch",
    )(dest, h_prompt, h_sample)


def _experts_kernel(vblk_ref, vexp_ref, vlo_ref, vhi_ref, xs_ref, wg_ref, wu_ref, wd_ref, ys_ref,
                    wg16_ref, wu16_ref, wd16_ref):
    v = pl.program_id(0)
    lo = vlo_ref[v]
    hi = vhi_ref[v]
    prev = jnp.maximum(v - 1, 0)
    first = jnp.logical_or(v == 0, vblk_ref[v] != vblk_ref[prev])
    new_expert = jnp.logical_or(v == 0, vexp_ref[v] != vexp_ref[prev])

    @pl.when(new_expert)
    def _():
        wg16_ref[...] = wg_ref[0].astype(BF16)
        wu16_ref[...] = wu_ref[0].astype(BF16)
        wd16_ref[...] = wd_ref[0].astype(BF16)

    @pl.when(hi > lo)
    def _():
        tm = xs_ref.shape[0]
        d = wg_ref.shape[1]
        x = xs_ref[...].reshape(tm, d)
        gate = _dot(x, wg16_ref[...])
        up = _dot(x, wu16_ref[...])
        hid = (gate * _sigmoid(gate) * up).astype(BF16)
        y = _dot(hid, wd16_ref[...]).astype(ys_ref.dtype)
        whole = jnp.logical_and(lo == 0, hi == tm)

        @pl.when(whole)
        def _():
            ys_ref[...] = y.reshape(ys_ref.shape)

        @pl.when(jnp.logical_not(whole))
        def _():
            rows = lax.broadcasted_iota(I32, y.shape, 0)
            mine = (rows >= lo) & (rows < hi)

            @pl.when(first)
            def _():
                ys_ref[...] = jnp.where(mine, y, jnp.zeros_like(y)).reshape(ys_ref.shape)

            @pl.when(jnp.logical_not(first))
            def _():
                ys_ref[...] = jnp.where(mine, y, ys_ref[...].reshape(tm, d)).reshape(ys_ref.shape)


def _experts(vblk, vexp, vlo, vhi, xs, wg, wu, wd):
    a = xs.shape[0]
    slab = xs.shape[1:]
    d, de = wg.shape[1:]
    tm = MOE_TILE
    grid_spec = pltpu.PrefetchScalarGridSpec(
        num_scalar_prefetch=4,
        grid=(vblk.shape[0],),
        in_specs=[pl.BlockSpec((tm,) + slab, lambda v, b, e, lo, hi: (b[v], 0, 0)),
                  pl.BlockSpec((1, d, de), lambda v, b, e, lo, hi: (e[v], 0, 0)),
                  pl.BlockSpec((1, d, de), lambda v, b, e, lo, hi: (e[v], 0, 0)),
                  pl.BlockSpec((1, de, d), lambda v, b, e, lo, hi: (e[v], 0, 0))],
        out_specs=pl.BlockSpec((tm,) + slab, lambda v, b, e, lo, hi: (b[v], 0, 0)),
        scratch_shapes=[pltpu.VMEM((d, de), BF16), pltpu.VMEM((d, de), BF16), pltpu.VMEM((de, d), BF16)],
    )
    return pl.pallas_call(
        _experts_kernel,
        grid_spec=grid_spec,
        out_shape=jax.ShapeDtypeStruct((a,) + slab, MOE_OUT_DTYPE),
        compiler_params=_cparams(("arbitrary",)),
        name="experts",
    )(vblk, vexp, vlo, vhi, xs, wg, wu, wd)


def _visit_plan(counts, n_rows):
    tm = MOE_TILE
    nblk = n_rows // tm
    n_visits = nblk + N_EXPERTS - 1
    ends = jnp.cumsum(counts)
    starts = ends - counts
    first_blk = starts // tm
    nvis = jnp.where(counts > 0, (ends + tm - 1) // tm - first_blk, 0)
    vis_end = jnp.cumsum(nvis)
    vis_start = vis_end - nvis
    v = jnp.arange(n_visits, dtype=I32)
    e = jnp.minimum(jnp.sum((vis_end[None, :] <= v[:, None]).astype(I32), axis=1), N_EXPERTS - 1)
    valid = v < vis_end[-1]
    blk = first_blk[e] + (v - vis_start[e])
    lo = jnp.clip(starts[e] - blk * tm, 0, tm)
    hi = jnp.clip(ends[e] - blk * tm, 0, tm)
    last_e = jnp.max(jnp.where(counts > 0, jnp.arange(N_EXPERTS, dtype=I32), 0))
    blk = jnp.where(valid, blk, nblk - 1).astype(I32)
    e = jnp.where(valid, e, last_e).astype(I32)
    lo = jnp.where(valid, lo, 0).astype(I32)
    hi = jnp.where(valid, hi, 0).astype(I32)
    return blk, e, lo, hi


def _combine_kernel(dest_ref, ys_ref, x1_ref, wcol_ref, gain_ref, out_ref, buf_ref, sems):
    tm, d = x1_ref.shape
    part = tm // COMBINE_PARTS

    def start(r, c, sem):
        for k in range(TOP_K):
            pltpu.make_async_copy(ys_ref.at[dest_ref[k, r]], buf_ref.at[k, r], sem).start(priority=k)
        return c

    for h in range(COMBINE_PARTS):
        lax.fori_loop(h * part, (h + 1) * part, functools.partial(start, sem=sems.at[h]), 0, unroll=DMA_UNROLL)
    for h in range(COMBINE_PARTS):
        rows = pl.ds(h * part, part)
        for k in range(TOP_K):
            pltpu.make_async_copy(ys_ref.at[rows], buf_ref.at[k, rows], sems.at[h]).wait()
        y = (wcol_ref[rows, 0:1] * buf_ref[0, rows].reshape(part, d).astype(F32)
             + wcol_ref[rows, 1:2] * buf_ref[1, rows].reshape(part, d).astype(F32))
        out_ref[rows, :] = _rms_norm(x1_ref[rows, :] + y, gain_ref[...])


def _combine(dest, first_token, ys, x1, wcol, gain):
    t, d = x1.shape
    tm = COMBINE_TILE
    assert t % tm == 0 and first_token % tm == 0
    off = first_token // tm
    row = lambda w: pl.BlockSpec((tm, w), lambda i: (i, 0))
    return pl.pallas_call(
        _combine_kernel,
        grid=(t // tm,),
        in_specs=[pl.BlockSpec((TOP_K, tm), lambda i: (0, i + off), memory_space=pltpu.SMEM),
                  pl.BlockSpec(memory_space=pl.ANY), row(d), row(LANES),
                  pl.BlockSpec(gain.shape, lambda i: (0, 0))],
        out_specs=row(d),
        out_shape=jax.ShapeDtypeStruct((t, d), F32),
        scratch_shapes=[pltpu.VMEM((TOP_K, tm) + ys.shape[1:], ys.dtype), pltpu.SemaphoreType.DMA((COMBINE_PARTS,))],
        compiler_params=_cparams(("arbitrary",)),
        name="combine",
    )(dest, ys, x1, wcol, gain)


def _prepare_weights(w_in, w_gla_gate_up, b_gla_gate, w_branch, w_out, w_router_group, b_router_group,
                     w_router_expert, b_router_expert):
    d = w_in.shape[0]
    qk = H_A * DK_A
    mw = H_A * DV_A
    c = 0
    w_qa, c = w_in[:, c:c + qk], c + qk
    w_ka, c = w_in[:, c:c + qk], c + qk
    w_va, c = w_in[:, c:c + mw], c + mw
    w_ra, c = w_in[:, c:c + mw], c + mw
    w_lr, c = w_in[:, c:c + GATE_RANK], c + GATE_RANK
    w_b, c = w_in[:, c:c + 3 * mw], c + 3 * mw
    w_g = w_in[:, c:]
    wa = jnp.concatenate([w_qa, w_ka, w_va, w_ra,
                          jnp.pad(w_lr, ((0, 0), (0, LANES - GATE_RANK)))], axis=1).astype(BF16)
    wgu = jnp.pad(w_gla_gate_up, ((0, LANES - GATE_RANK), (0, 0))).astype(BF16)
    bgu = b_gla_gate[None, :]
    wr = jnp.zeros((LANES, d), F32)
    wr = wr.at[0:N_GROUPS].set(w_router_group.T).at[8:8 + N_EXPERTS].set(w_router_expert.T)
    br = jnp.zeros((LANES,), F32).at[0:N_GROUPS].set(b_router_group).at[8:8 + N_EXPERTS].set(b_router_expert)
    br = jnp.broadcast_to(br[:, None], (LANES, LANES))
    return dict(wa=wa, wqb=w_b[:, 0:mw].astype(BF16), wkvt=w_b[:, mw:3 * mw].T.astype(BF16),
                wg=w_g.astype(BF16), wgu=wgu, bgu=bgu,
                wb0=w_branch[0].astype(BF16), wb1=w_branch[1].astype(BF16), wo=w_out.astype(BF16),
                wr=wr, br=br)


def _mixers(x, s0, k_past, v_past, w, norm_mix_gain, gla_norm_gain, norm_ffn_gain):
    b, s, d = x.shape
    xf = x.reshape(b * s, d)
    qa, ka, va, ra, la, qb, kt, vt, kt16, vt16, gbr = _in_projection(
        xf, s, norm_mix_gain[None, :], w["wa"], w["wqb"], w["wkvt"], w["wg"], w["wgu"], w["bgu"])
    seq = lambda a: a.reshape(b, s, a.shape[-1])
    oa, s_new = _gla(seq(qa), seq(ka), seq(va), seq(ra), seq(la), s0, gla_norm_gain[None, :],
                     min(s, ROW_TILE))
    to_channel_major = lambda a: jnp.transpose(a, (0, 2, 3, 1))
    if k_past is None:
        ob = _sb_prompt(seq(qb), kt16, vt16)
    else:
        ob = _sb_sample(seq(qb), kt16, vt16, to_channel_major(k_past), to_channel_major(v_past))
    x1, h2, eid, wcol = _merge(oa.reshape(b * s, -1), ob.reshape(b * s, -1), gbr, xf, w["wb0"], w["wb1"],
                               w["wo"], norm_ffn_gain[None, :], w["wr"], w["br"])
    from_channel_major = lambda a: jnp.transpose(a.reshape(b, H_B, DH_B, s), (0, 3, 1, 2))
    return x1, h2, eid, wcol, s_new, from_channel_major(kt), from_channel_major(vt)


def kernel(x_prompt, x_sample, state_gla, cache_sb_k, cache_sb_v, norm_mix_gain, w_in, w_gla_gate_up, b_gla_gate, gla_norm_gain, w_branch, w_out, norm_ffn_gain, w_router_group, b_router_group, w_router_expert, b_router_expert, w_exp_gate, w_exp_up, w_exp_down, norm_final_gain):
    depth = w_in.shape[0]
    assert depth == 1, "one trunk layer per step"
    l = 0
    w = _prepare_weights(w_in[l], w_gla_gate_up[l], b_gla_gate[l], w_branch[l], w_out[l], w_router_group[l],
                         b_router_group[l], w_router_expert[l], b_router_expert[l])
    bp, sp, d = x_prompt.shape
    bs, ss, _ = x_sample.shape
    s0 = jnp.zeros((bp, H_A, DK_A, DV_A), x_prompt.dtype)
    x1p, h2p, eidp, wcolp, gla_p, k_p, v_p = _mixers(
        x_prompt, s0, None, None, w, norm_mix_gain[l], gla_norm_gain[l], norm_ffn_gain[l])
    x1s, h2s, eids, wcols, gla_s, k_s, v_s = _mixers(
        x_sample, state_gla[l], cache_sb_k[l], cache_sb_v[l], w, norm_mix_gain[l], gla_norm_gain[l],
        norm_ffn_gain[l])

    tp, ts = bp * sp, bs * ss
    eid = jnp.concatenate([eidp, eids], axis=1)
    dest_blocks, counts = _positions(eid.reshape(-1, SORT_WIDTH))
    dest = dest_blocks.reshape(TOP_K, tp + ts)
    xs = _dispatch(dest, h2p, h2s)
    vblk, vexp, vlo, vhi = _visit_plan(counts[:, 0], TOP_K * (tp + ts))
    ys = _experts(vblk, vexp, vlo, vhi, xs, w_exp_gate[l], w_exp_up[l], w_exp_down[l])
    gf = norm_final_gain[None, :]
    y_prompt = _combine(dest, 0, ys, x1p, wcolp, gf).reshape(bp, sp, d)
    y_sample = _combine(dest, tp, ys, x1s, wcols, gf).reshape(bs, ss, d)
    return (y_prompt, y_sample, gla_p[None], k_p[None], v_p[None], gla_s[None], k_s[None], v_s[None])
```

```python
import functools

import jax
import jax.numpy as jnp
from jax import lax
from jax.experimental import pallas as pl
from jax.experimental.pallas import tpu as pltpu

F32 = jnp.float32
BF16 = jnp.bfloat16
MOE_OUT_DTYPE = jnp.bfloat16
I32 = jnp.int32

LANES = 128
LOG2_E = 1.4426950408889634
RMS_EPS = 1e-6
GATE_TAU = 16.0
H_A = 4
DK_A = 64
DV_A = 128
GATE_RANK = 16
H_B = 8
DH_B = 64
N_GROUPS = 4
EXPERTS_PER_GROUP = 8
N_EXPERTS = N_GROUPS * EXPERTS_PER_GROUP
TOP_K = 2
GLA_CHUNK = 64
GLA_SUB = 16
GLA_EXP_CLAMP = 80.0
GLA_SEQS = 4
ROW_TILE = 256
SB_TILE = 256
SB_SAMPLE_SEQS = 2
MOE_TILE = 512
SORT_WIDTH = 256
INPROJ_TILE = 512
MERGE_TILE = 512
DISPATCH_TILE = 1024
COMBINE_TILE = 512
COMBINE_PARTS = 4
DMA_UNROLL = 8
VMEM_LIMIT = 56 * 1024 * 1024


def _cparams(sem):
    return pltpu.CompilerParams(dimension_semantics=sem, vmem_limit_bytes=VMEM_LIMIT)


def _dot(a, b):
    return jnp.dot(a, b, preferred_element_type=F32)


def _dot_nt(a, b):
    return lax.dot_general(a, b, (((1,), (1,)), ((), ())), preferred_element_type=F32)


def _dot_tn(a, b):
    return lax.dot_general(a, b, (((0,), (0,)), ((), ())), preferred_element_type=F32)


def _split_bf16(x):
    hi = x.astype(BF16)
    lo = (x - hi.astype(F32)).astype(BF16)
    return hi, lo


def _log_sigmoid(x):
    return jnp.minimum(x, 0.0) - jnp.log(1.0 + jnp.exp(-jnp.abs(x)))


def _sigmoid(x):
    return 1.0 / (1.0 + jnp.exp(-x))


def _rms_norm(x, gain):
    return x * lax.rsqrt(jnp.mean(x * x, axis=-1, keepdims=True) + RMS_EPS) * gain


def _inproj_kernel(x_ref, gain_ref, wa_ref, wqb_ref, wkvt_ref, wg_ref, wgu_ref, bgu_ref,
                   qa_ref, ka_ref, va_ref, ra_ref, la_ref, qb_ref, kt_ref, vt_ref,
                   kt16_ref, vt16_ref, gbr_ref):
    h = _rms_norm(x_ref[...], gain_ref[...]).astype(BF16)
    pa = H_A * DK_A
    mw = va_ref.shape[-1]
    kvt = _dot_nt(wkvt_ref[...], h)
    nseq, _, s = kt_ref.shape
    ntile, tile = kt16_ref.shape[1], kt16_ref.shape[3]
    for i in range(nseq):
        kt_ref[i] = kvt[0:mw, i * s:(i + 1) * s]
        vt_ref[i] = kvt[mw:2 * mw, i * s:(i + 1) * s]
        for j in range(ntile):
            cols = slice(i * s + j * tile, i * s + (j + 1) * tile)
            kt16_ref[i, j] = kvt[0:mw, cols].astype(BF16)
            vt16_ref[i, j] = kvt[mw:2 * mw, cols].astype(BF16)
    qb_ref[...] = _dot(h, wqb_ref[...]).astype(BF16)
    qa_ref[...] = _dot(h, wa_ref[:, 0:pa])
    ka_ref[...] = _dot(h, wa_ref[:, pa:2 * pa])
    va_ref[...] = _dot(h, wa_ref[:, 2 * pa:2 * pa + mw])
    ra_ref[...] = _dot(h, wa_ref[:, 2 * pa + mw:2 * pa + 2 * mw])
    lr = _dot(h, wa_ref[:, 2 * pa + 2 * mw:2 * pa + 2 * mw + LANES])
    gl = _dot(lr.astype(BF16), wgu_ref[...]) + bgu_ref[...]
    la_ref[...] = _log_sigmoid(gl) / GATE_TAU
    gbr_ref[...] = _dot(h, wg_ref[...]).astype(gbr_ref.dtype)


def _in_projection(x, seq_len, gain, wa, wqb, wkvt, wg, wgu, bgu):
    t, d = x.shape
    nb = t // seq_len
    pa = H_A * DK_A
    mw = wqb.shape[1]
    tm = INPROJ_TILE
    assert t % tm == 0
    row = lambda w: pl.BlockSpec((tm, w), lambda i: (i, 0))
    full = lambda a: pl.BlockSpec(a.shape, lambda i: (0,) * a.ndim, pipeline_mode=pl.Buffered(1))
    if seq_len >= tm:
        per_seq = seq_len // tm
        ntile = tm // SB_TILE
        assert tm % SB_TILE == 0 and seq_len % tm == 0
        kt_spec = pl.BlockSpec((1, mw, tm), lambda i: (i // per_seq, 0, i % per_seq))
        kt16_spec = pl.BlockSpec((1, ntile, mw, SB_TILE), lambda i: (i // per_seq, i % per_seq, 0, 0))
        kt16_shape = (nb, seq_len // SB_TILE, mw, SB_TILE)
    else:
        nseq = tm // seq_len
        assert tm % seq_len == 0
        kt_spec = pl.BlockSpec((nseq, mw, seq_len), lambda i: (i, 0, 0))
        kt16_spec = pl.BlockSpec((nseq, 1, mw, seq_len), lambda i: (i, 0, 0, 0))
        kt16_shape = (nb, 1, mw, seq_len)
    outs = [
        (jax.ShapeDtypeStruct((t, pa), F32), row(pa)), (jax.ShapeDtypeStruct((t, pa), F32), row(pa)),
        (jax.ShapeDtypeStruct((t, mw), F32), row(mw)), (jax.ShapeDtypeStruct((t, mw), F32), row(mw)),
        (jax.ShapeDtypeStruct((t, pa), F32), row(pa)),
        (jax.ShapeDtypeStruct((t, mw), BF16), row(mw)),
        (jax.ShapeDtypeStruct((nb, mw, seq_len), F32), kt_spec), (jax.ShapeDtypeStruct((nb, mw, seq_len), F32), kt_spec),
        (jax.ShapeDtypeStruct(kt16_shape, BF16), kt16_spec), (jax.ShapeDtypeStruct(kt16_shape, BF16), kt16_spec),
        (jax.ShapeDtypeStruct((t, wg.shape[1]), BF16), row(wg.shape[1])),
    ]
    return pl.pallas_call(
        _inproj_kernel,
        grid=(t // tm,),
        in_specs=[row(d), full(gain), full(wa), full(wqb), full(wkvt), full(wg), full(wgu), full(bgu)],
        out_specs=[spec for _, spec in outs],
        out_shape=[shape for shape, _ in outs],
        compiler_params=_cparams(("arbitrary",)),
        name="in_projection",
    )(x, gain, wa, wqb, wkvt, wg, wgu, bgu)


def _gla_chunk(q, k, v, b, st):
    c = q.shape[0]
    b_last = b[c - 1:c, :]
    rows = lax.broadcasted_iota(I32, (c, LANES), 0)
    nsub = c // GLA_SUB
    refs = [jnp.zeros((1, LANES), F32)] + [b[i * GLA_SUB - 1:i * GLA_SUB, :] for i in range(1, nsub)]
    ref_rows = refs[0]
    for i in range(1, nsub):
        ref_rows = jnp.where(rows >= i * GLA_SUB, refs[i], ref_rows)
    q_rel = q * jnp.exp(b - ref_rows)
    lhs = jnp.concatenate(
        [jnp.where((rows >= i * GLA_SUB) & (rows < (i + 1) * GLA_SUB), q_rel, 0.0) for i in range(nsub)],
        axis=1).astype(BF16)
    rhs = jnp.concatenate(
        [jnp.where(rows < (i + 1) * GLA_SUB, k * jnp.exp(jnp.minimum(refs[i] - b, GLA_EXP_CLAMP)), 0.0)
         for i in range(nsub)], axis=1).astype(BF16)
    att = _dot_nt(lhs, rhs)
    tt = lax.broadcasted_iota(I32, (c, c), 0)
    ss = lax.broadcasted_iota(I32, (c, c), 1)
    att = jnp.where(ss <= tt, att, 0.0)
    v16 = v.astype(BF16)
    inter = _dot_nt((q * jnp.exp(b)).astype(BF16), st.astype(BF16))
    intra = _dot(att.astype(BF16), v16)
    kd = (k * jnp.exp(b_last - b)).astype(BF16)
    st_new = st * jnp.exp(b_last) + _dot_tn(v16, kd)
    return inter + intra, st_new


def _gla_kernel(qa_ref, ka_ref, va_ref, ra_ref, la_ref, s0_ref, gain_ref, o_ref, sfin_ref, st_ref):
    j = pl.program_id(1)
    nj = pl.num_programs(1)
    nseq, rows_per_step, _ = qa_ref.shape
    c = GLA_CHUNK
    zpad = jnp.zeros((LANES - DK_A, DV_A), F32)

    def state_rows(h):
        return slice((h % 2) * DK_A, (h % 2 + 1) * DK_A)

    @pl.when(j == 0)
    def _():
        for si in range(nseq):
            for h in range(H_A):
                parts = [s0_ref[si, h], zpad] if h % 2 == 0 else [zpad, s0_ref[si, h]]
                st_ref[si * H_A + h] = jnp.concatenate(parts, axis=0).T

    ti = lax.broadcasted_iota(I32, (rows_per_step, rows_per_step), 0)
    si = lax.broadcasted_iota(I32, (rows_per_step, rows_per_step), 1)
    chunk_shift = c.bit_length() - 1
    same_chunk = (ti >> chunk_shift) == (si >> chunk_shift)
    tril_blocks = jnp.where(same_chunk & (si <= ti), 1.0, 0.0).astype(BF16)
    gain = gain_ref[...]
    lane = lax.broadcasted_iota(I32, (1, LANES), 1)
    half_masks = (lane < DK_A, lane >= DK_A)
    for si in range(nseq):
        la_hi, la_lo = _split_bf16(la_ref[si])
        b_all = _dot(tril_blocks, la_hi) + _dot(tril_blocks, la_lo)
        for h in range(H_A):
            hp = slice((h // 2) * LANES, (h // 2 + 1) * LANES)
            hv = slice(h * DV_A, (h + 1) * DV_A)
            mine = half_masks[h % 2]
            st = st_ref[si * H_A + h]
            for ci in range(rows_per_step // c):
                r0 = ci * c
                q = jnp.where(mine, qa_ref[si, r0:r0 + c, hp], 0.0) * (DK_A ** -0.5)
                k = jnp.where(mine, ka_ref[si, r0:r0 + c, hp], 0.0)
                o, st = _gla_chunk(q, k, va_ref[si, r0:r0 + c, hv], b_all[r0:r0 + c, hp], st)
                r = ra_ref[si, r0:r0 + c, hv]
                o = _rms_norm(o, gain) * (r * _sigmoid(r))
                o_ref[si, r0:r0 + c, hv] = o.astype(o_ref.dtype)
            st_ref[si * H_A + h] = st

    @pl.when(j == nj - 1)
    def _():
        for si in range(nseq):
            for h in range(H_A):
                sfin_ref[si, h] = st_ref[si * H_A + h].T[state_rows(h), :]


def _gla(qa, ka, va, ra, la, s0, gain, rows_per_step):
    assert 2 * DK_A == LANES and H_A % 2 == 0
    b, s, pa = qa.shape
    mw = va.shape[-1]
    ns = GLA_SEQS
    assert b % ns == 0
    seq = lambda w: pl.BlockSpec((ns, rows_per_step, w), lambda i, j: (i, j, 0))
    state = pl.BlockSpec((ns, H_A, DK_A, DV_A), lambda i, j: (i, 0, 0, 0))
    return pl.pallas_call(
        _gla_kernel,
        grid=(b // ns, s // rows_per_step),
        in_specs=[seq(pa), seq(pa), seq(mw), seq(mw), seq(pa), state,
                  pl.BlockSpec(gain.shape, lambda i, j: (0, 0))],
        out_specs=[seq(mw), state],
        out_shape=[jax.ShapeDtypeStruct((b, s, mw), BF16),
                   jax.ShapeDtypeStruct((b, H_A, DK_A, DV_A), F32)],
        scratch_shapes=[pltpu.VMEM((ns * H_A, LANES, LANES), F32)],
        compiler_params=_cparams(("arbitrary", "arbitrary")),
        name="gla",
    )(qa, ka, va, ra, la, s0, gain)


def _head_lane_masks():
    lane = lax.broadcasted_iota(I32, (1, LANES), 1)
    return lane < DH_B, lane >= DH_B


def _sb_neg_tri(tk):
    ji = lax.broadcasted_iota(I32, (tk, tk), 0)
    si = lax.broadcasted_iota(I32, (tk, tk), 1)
    return jnp.where(ji >= si, -1.0, 0.0).astype(BF16)


def _sb_stack_queries(q, qs_ref, base=0):
    m0, m1 = _head_lane_masks()
    for p in range(q.shape[1] // LANES):
        qp = (q[:, p * LANES:(p + 1) * LANES].astype(F32) * (DH_B ** -0.5 * LOG2_E)).astype(BF16)
        zero = jnp.zeros_like(qp)
        qs_ref[base + p] = jnp.concatenate([jnp.where(m0, qp, zero), jnp.where(m1, qp, zero)], axis=0)


def _pair_lanes(p):
    return slice(p * LANES, (p + 1) * LANES)


def _lane_fit(x, width):
    if width >= LANES:
        return jnp.concatenate([x] * (width // LANES), axis=1)
    return x[:, 0:width]


def _sb_tile_step(qs_ref, acc_ref, carry_ref, k_tile, v_tile, ntri, diagonal, one_suffix_matmul=False):
    npair, rows, _ = qs_ref.shape
    tq = rows // 2
    tk = ntri.shape[1]
    m0, _ = _head_lane_masks()
    if diagonal:
        t = lax.broadcasted_iota(I32, (rows, tk), 0)
        t = jnp.where(t >= tq, t - tq, t)
        visible = lax.broadcasted_iota(I32, (rows, tk), 1) < t
    def scores(p):
        z = _dot(qs_ref[p], k_tile(p))
        sp = jnp.maximum(z, 0.0) + jnp.log2(1.0 + jnp.exp2(-jnp.abs(z)))
        if diagonal:
            sp = jnp.where(visible, sp, 0.0)
        return z, sp.astype(BF16)

    if one_suffix_matmul:
        zs, sps = zip(*[scores(p) for p in range(npair)])
        stacked = _dot(jnp.concatenate(sps, axis=0), ntri)
        suffixes = [stacked[p * rows:(p + 1) * rows] for p in range(npair)]
    for p in range(npair):
        if one_suffix_matmul:
            z, suffix = zs[p], suffixes[p]
        else:
            z, sp = scores(p)
            suffix = _dot(sp, ntri)
        carry = carry_ref[p]
        w = jnp.exp2(z + suffix + _lane_fit(carry, tk))
        if diagonal:
            w = jnp.where(visible, w, 0.0)
        pv = _dot_nt(w.astype(BF16), v_tile(p))
        acc_ref[p] += jnp.where(m0, pv[0:tq], pv[tq:rows])
        carry_ref[p] = carry + jnp.broadcast_to(suffix[:, 0:1], carry.shape)


def _sb_prompt_kernel(q_ref, k_ref, v_ref, o_ref, qs_ref, acc_ref, carry_ref):
    qi = pl.program_id(1)
    tk = SB_TILE
    _sb_stack_queries(q_ref[0], qs_ref)
    acc_ref[...] = jnp.zeros_like(acc_ref)
    carry_ref[...] = jnp.zeros_like(carry_ref)
    ntri = _sb_neg_tri(tk)

    def step(jb, diagonal):
        _sb_tile_step(qs_ref, acc_ref, carry_ref, lambda p: k_ref[0, jb, _pair_lanes(p), :],
                      lambda p: v_ref[0, jb, _pair_lanes(p), :], ntri, diagonal, True)

    step(qi, True)

    def body(i, c):
        step(qi - 1 - 2 * i, False)
        step(qi - 2 - 2 * i, False)
        return c

    lax.fori_loop(0, qi // 2, body, 0)

    @pl.when(qi % 2 == 1)
    def _():
        step(0, False)

    for p in range(acc_ref.shape[0]):
        o_ref[0, :, p * LANES:(p + 1) * LANES] = acc_ref[p].astype(o_ref.dtype)


def _sb_scratch(tq, npair):
    return [pltpu.VMEM((npair, 2 * tq, LANES), BF16), pltpu.VMEM((npair, tq, LANES), F32),
            pltpu.VMEM((npair, 2 * tq, LANES), F32)]


def _sb_prompt(q, kt, vt):
    b, s, w = q.shape
    tq = SB_TILE
    assert kt.shape == (b, s // tq, w, tq)
    qspec = pl.BlockSpec((1, tq, w), lambda i, j: (i, j, 0))
    kvspec = pl.BlockSpec((1,) + kt.shape[1:], lambda i, j: (i, 0, 0, 0))
    return pl.pallas_call(
        _sb_prompt_kernel,
        grid=(b, s // tq),
        in_specs=[qspec, kvspec, kvspec],
        out_specs=qspec,
        out_shape=jax.ShapeDtypeStruct((b, s, w), BF16),
        scratch_shapes=_sb_scratch(tq, w // LANES),
        compiler_params=_cparams(("arbitrary", "arbitrary")),
        name="sb_prompt",
    )(q, kt, vt)


def _sb_sample_kernel(q_ref, kn_ref, vn_ref, kp_ref, vp_ref, o_ref, qs_ref, acc_ref, carry_ref):
    nseq, sq, w = q_ref.shape
    past = kp_ref.shape[3]
    npair = w // LANES
    tk = SB_TILE
    for si in range(nseq):
        _sb_stack_queries(q_ref[si], qs_ref, si * npair)
    acc_ref[...] = jnp.zeros_like(acc_ref)
    carry_ref[...] = jnp.zeros_like(carry_ref)
    _sb_tile_step(qs_ref, acc_ref, carry_ref, lambda e: kn_ref[e // npair, 0, _pair_lanes(e % npair), :],
                  lambda e: vn_ref[e // npair, 0, _pair_lanes(e % npair), :], _sb_neg_tri(sq), True, True)
    ntri = _sb_neg_tri(tk)

    def body(i, c):
        cols = pl.ds(pl.multiple_of(past - (i + 1) * tk, tk), tk)

        def pair(ref, e):
            p = e % npair
            return ref[e // npair, 2 * p:2 * p + 2, :, cols].reshape(LANES, tk).astype(BF16)

        _sb_tile_step(qs_ref, acc_ref, carry_ref, lambda e: pair(kp_ref, e), lambda e: pair(vp_ref, e), ntri, False,
                      True)
        return c

    lax.fori_loop(0, past // tk, body, 0)
    for e in range(acc_ref.shape[0]):
        o_ref[e // npair, :, _pair_lanes(e % npair)] = acc_ref[e].astype(o_ref.dtype)


def _sb_sample(q, kt_new, vt_new, kt_past, vt_past):
    b, sq, w = q.shape
    past = kt_past.shape[3]
    ns = SB_SAMPLE_SEQS
    assert past % SB_TILE == 0 and 2 * DH_B == LANES and b % ns == 0
    qspec = pl.BlockSpec((ns, sq, w), lambda i: (i, 0, 0))
    new = pl.BlockSpec((ns, 1, w, sq), lambda i: (i, 0, 0, 0))
    old = pl.BlockSpec((ns, H_B, DH_B, past), lambda i: (i, 0, 0, 0))
    return pl.pallas_call(
        _sb_sample_kernel,
        grid=(b // ns,),
        in_specs=[qspec, new, new, old, old],
        out_specs=qspec,
        out_shape=jax.ShapeDtypeStruct((b, sq, w), BF16),
        scratch_shapes=_sb_scratch(sq, ns * (w // LANES)),
        compiler_params=_cparams(("arbitrary",)),
        name="sb_sample",
    )(q, kt_new, vt_new, kt_past, vt_past)


def _first_argmax(vals, nrows):
    idx = lax.broadcasted_iota(I32, vals.shape, 0)
    top = jnp.max(vals, axis=0, keepdims=True)
    first = jnp.min(jnp.where(vals == top, idx, nrows), axis=0, keepdims=True)
    return top, first, idx


def _merge_kernel(oa_ref, ob_ref, g_ref, x_ref, wb0_ref, wb1_ref, wo_ref, gain_ref, wr_ref, br_ref,
                  x1_ref, h2_ref, eid_ref, wcol_ref):
    d = x_ref.shape[1]
    ya = _dot(oa_ref[...], wb0_ref[...])
    yb = _dot(ob_ref[...], wb1_ref[...])
    g = g_ref[...].astype(F32)
    m = _sigmoid(g[:, 0:d]) * ya + _sigmoid(g[:, d:2 * d]) * yb
    x1 = x_ref[...] + _dot(m.astype(BF16), wo_ref[...])
    x1_ref[...] = x1
    h2 = _rms_norm(x1, gain_ref[...])
    h2_ref[...] = h2.astype(h2_ref.dtype).reshape(h2_ref.shape)

    h_hi, h_lo = _split_bf16(h2)
    w_hi, w_lo = _split_bf16(wr_ref[...])
    lt = _dot_nt(w_hi, h_hi) + _dot_nt(w_hi, h_lo) + _dot_nt(w_lo, h_hi) + br_ref[:, 0:1]
    gl = lt[0:N_GROUPS, :]
    g_top, g_idx, _ = _first_argmax(gl, N_GROUPS)
    g_e = jnp.exp(gl - g_top)
    g_p = jnp.max(g_e / jnp.sum(g_e, axis=0, keepdims=True), axis=0, keepdims=True)
    el = jnp.zeros((EXPERTS_PER_GROUP, lt.shape[1]), F32)
    for g in range(N_GROUPS):
        r0 = 8 + g * EXPERTS_PER_GROUP
        el = jnp.where(g_idx == g, lt[r0:r0 + EXPERTS_PER_GROUP, :], el)
    e_top, i1, eidx = _first_argmax(el, EXPERTS_PER_GROUP)
    e_e = jnp.exp(el - e_top)
    e_p = e_e / jnp.sum(e_e, axis=0, keepdims=True)
    p1 = jnp.max(e_p, axis=0, keepdims=True)
    rest = jnp.where(eidx == i1, -1.0, e_p)
    p2, i2, _ = _first_argmax(rest, EXPERTS_PER_GROUP)
    norm = p1 + p2
    w1 = g_p * (p1 / norm)
    w2 = g_p * (p2 / norm)
    eid_ref[...] = jnp.concatenate([g_idx * EXPERTS_PER_GROUP + i1, g_idx * EXPERTS_PER_GROUP + i2], axis=0)
    rows = lax.broadcasted_iota(I32, (LANES, lt.shape[1]), 0)
    wrows = jnp.where(rows == 0, w1, jnp.where(rows == 1, w2, 0.0))
    wcol_ref[...] = wrows.T


def _merge(oa, ob, gbr, x, wb0, wb1, wo, gain, wr, br):
    t, d = x.shape
    tm = MERGE_TILE
    assert t % tm == 0
    row = lambda w: pl.BlockSpec((tm, w), lambda i: (i, 0))
    full = lambda a: pl.BlockSpec(a.shape, lambda i: (0,) * a.ndim)
    return pl.pallas_call(
        _merge_kernel,
        grid=(t // tm,),
        in_specs=[row(oa.shape[1]), row(ob.shape[1]), row(gbr.shape[1]), row(d),
                  full(wb0), full(wb1), full(wo), full(gain), full(wr), full(br)],
        out_specs=[row(d), pl.BlockSpec((tm, d // LANES, LANES), lambda i: (i, 0, 0)),
                   pl.BlockSpec((TOP_K, tm), lambda i: (0, i)), row(LANES)],
        out_shape=[jax.ShapeDtypeStruct((t, d), F32), jax.ShapeDtypeStruct((t, d // LANES, LANES), BF16),
                   jax.ShapeDtypeStruct((TOP_K, t), I32), jax.ShapeDtypeStruct((t, LANES), F32)],
        compiler_params=_cparams(("arbitrary",)),
        name="merge_router",
    )(oa, ob, gbr, x, wb0, wb1, wo, gain, wr, br)


def _positions_kernel(eid_ref, dest_ref, counts_ref, rank_ref):
    nblk, width = eid_ref.shape
    ji = lax.broadcasted_iota(I32, (width, width), 0)
    si = lax.broadcasted_iota(I32, (width, width), 1)
    prefix = jnp.where(ji <= si, 1.0, 0.0).astype(BF16)
    expert = lax.broadcasted_iota(I32, (N_EXPERTS, width), 0)
    group = max(g for g in (8, 4, 2, 1) if nblk % g == 0)

    def onehot(i):
        return expert == eid_ref[pl.ds(i, 1), :]

    def rank_body(ig, run):
        first = pl.multiple_of(ig * group, group)
        ohs = [onehot(first + j) for j in range(group)]
        stacked = jnp.concatenate([jnp.where(oh, 1.0, 0.0) for oh in ohs], axis=0).astype(BF16)
        cum = _dot(stacked, prefix)
        ranks = []
        for j, oh in enumerate(ohs):
            cum_j = cum[j * N_EXPERTS:(j + 1) * N_EXPERTS, :] + run
            ranks.append(jnp.sum(jnp.where(oh, cum_j, 0.0), axis=0, keepdims=True) - 1.0)
            run = cum_j[:, width - 1:width]
        rank_ref[pl.ds(first, group), :] = jnp.concatenate(ranks, axis=0)
        return run

    counts = lax.fori_loop(0, nblk // group, rank_body, jnp.zeros((N_EXPERTS, 1), F32))
    counts_ref[...] = jnp.broadcast_to(counts, counts_ref.shape).astype(I32)
    c_hi = jnp.floor(counts * (1.0 / 256.0))
    c_lo = counts - 256.0 * c_hi
    ei = lax.broadcasted_iota(I32, (N_EXPERTS, N_EXPERTS), 0)
    ej = lax.broadcasted_iota(I32, (N_EXPERTS, N_EXPERTS), 1)
    strict = jnp.where(ej < ei, 1.0, 0.0).astype(BF16)
    digits = jnp.concatenate([jnp.broadcast_to(c_hi, (N_EXPERTS, LANES)),
                              jnp.broadcast_to(c_lo, (N_EXPERTS, LANES))], axis=1).astype(BF16)
    sums = _dot(strict, digits)
    start = 256.0 * sums[:, 0:1] + sums[:, LANES:LANES + 1]

    def dest_body(ig, carry):
        first = pl.multiple_of(ig * group, group)
        offs = [jnp.sum(jnp.where(onehot(first + j), start, 0.0), axis=0, keepdims=True) for j in range(group)]
        rows = pl.ds(first, group)
        dest_ref[rows, :] = (rank_ref[rows, :] + jnp.concatenate(offs, axis=0)).astype(I32)
        return carry

    lax.fori_loop(0, nblk // group, dest_body, 0)


def _positions(eid_blocks):
    nblk, width = eid_blocks.shape
    vm = lambda shape: pl.BlockSpec(shape, lambda: (0,) * len(shape))
    return pl.pallas_call(
        _positions_kernel,
        in_specs=[vm((nblk, width))],
        out_specs=[vm((nblk, width)), vm((N_EXPERTS, LANES))],
        out_shape=[jax.ShapeDtypeStruct((nblk, width), I32), jax.ShapeDtypeStruct((N_EXPERTS, LANES), I32)],
        scratch_shapes=[pltpu.VMEM((nblk, width), F32)],
        name="positions",
    )(eid_blocks)


def _dispatch_kernel(n_prompt_tiles, dest_ref, hp_ref, hs_ref, xs_ref, sem):
    i = pl.program_id(0)
    tm = dest_ref.shape[1]

    def scatter(src_ref):
        def start(r, c):
            for k in range(TOP_K):
                pltpu.make_async_copy(src_ref.at[r], xs_ref.at[dest_ref[k, r]], sem).start(priority=k)
            return c

        lax.fori_loop(0, tm, start, 0, unroll=DMA_UNROLL)
        for k in range(TOP_K):
            pltpu.make_async_copy(src_ref, xs_ref.at[pl.ds(0, tm)], sem).wait()

    @pl.when(i < n_prompt_tiles)
    def _():
        scatter(hp_ref)

    @pl.when(i >= n_prompt_tiles)
    def _():
        scatter(hs_ref)


def _dispatch(dest, h_prompt, h_sample):
    t = dest.shape[1]
    slab = h_prompt.shape[1:]
    tm = DISPATCH_TILE
    assert h_prompt.shape[0] % tm == 0 and h_sample.shape[0] % tm == 0
    npt = h_prompt.shape[0] // tm
    return pl.pallas_call(
        functools.partial(_dispatch_kernel, npt),
        grid=(t // tm,),
        in_specs=[pl.BlockSpec((TOP_K, tm), lambda i: (0, i), memory_space=pltpu.SMEM),
                  pl.BlockSpec((tm,) + slab, lambda i: (jnp.minimum(i, npt - 1), 0, 0)),
                  pl.BlockSpec((tm,) + slab, lambda i: (jnp.maximum(i - npt, 0), 0, 0))],
        out_specs=pl.BlockSpec(memory_space=pl.ANY),
        out_shape=jax.ShapeDtypeStruct((TOP_K * t,) + slab, h_prompt.dtype),
        scratch_shapes=[pltpu.SemaphoreType.DMA(())],
        compiler_params=_cparams(("arbitrary",)),
        name="dispatch",
    )(dest, h_prompt, h_sample)


def _experts_kernel(vblk_ref, vexp_ref, vlo_ref, vhi_ref, xs_ref, wg_ref, wu_ref, wd_ref, ys_ref,
                    wg16_ref, wu16_ref, wd16_ref):
    v = pl.program_id(0)
    lo = vlo_ref[v]
    hi = vhi_ref[v]
    prev = jnp.maximum(v - 1, 0)
    first = jnp.logical_or(v == 0, vblk_ref[v] != vblk_ref[prev])
    new_expert = jnp.logical_or(v == 0, vexp_ref[v] != vexp_ref[prev])

    @pl.when(new_expert)
    def _():
        wg16_ref[...] = wg_ref[0].astype(BF16)
        wu16_ref[...] = wu_ref[0].astype(BF16)
        wd16_ref[...] = wd_ref[0].astype(BF16)

    @pl.when(first)
    def _():
        ys_ref[...] = jnp.zeros_like(ys_ref)

    @pl.when(hi > lo)
    def _():
        tm = xs_ref.shape[0]
        d = wg_ref.shape[1]
        x = xs_ref[...].reshape(tm, d)
        gate = _dot(x, wg16_ref[...])
        up = _dot(x, wu16_ref[...])
        hid = (gate * _sigmoid(gate) * up).astype(BF16)
        y = _dot(hid, wd16_ref[...]).astype(ys_ref.dtype)
        rows = lax.broadcasted_iota(I32, y.shape, 0)
        mine = (rows >= lo) & (rows < hi)
        ys_ref[...] = jnp.where(mine, y, ys_ref[...].reshape(tm, d)).reshape(ys_ref.shape)


def _experts(vblk, vexp, vlo, vhi, xs, wg, wu, wd):
    a = xs.shape[0]
    slab = xs.shape[1:]
    d, de = wg.shape[1:]
    tm = MOE_TILE
    grid_spec = pltpu.PrefetchScalarGridSpec(
        num_scalar_prefetch=4,
        grid=(vblk.shape[0],),
        in_specs=[pl.BlockSpec((tm,) + slab, lambda v, b, e, lo, hi: (b[v], 0, 0)),
                  pl.BlockSpec((1, d, de), lambda v, b, e, lo, hi: (e[v], 0, 0)),
                  pl.BlockSpec((1, d, de), lambda v, b, e, lo, hi: (e[v], 0, 0)),
                  pl.BlockSpec((1, de, d), lambda v, b, e, lo, hi: (e[v], 0, 0))],
        out_specs=pl.BlockSpec((tm,) + slab, lambda v, b, e, lo, hi: (b[v], 0, 0)),
        scratch_shapes=[pltpu.VMEM((d, de), BF16), pltpu.VMEM((d, de), BF16), pltpu.VMEM((de, d), BF16)],
    )
    return pl.pallas_call(
        _experts_kernel,
        grid_spec=grid_spec,
        out_shape=jax.ShapeDtypeStruct((a,) + slab, MOE_OUT_DTYPE),
        compiler_params=_cparams(("arbitrary",)),
        name="experts",
    )(vblk, vexp, vlo, vhi, xs, wg, wu, wd)


def _visit_plan(counts, n_rows):
    tm = MOE_TILE
    nblk = n_rows // tm
    n_visits = nblk + N_EXPERTS - 1
    ends = jnp.cumsum(counts)
    starts = ends - counts
    first_blk = starts // tm
    nvis = jnp.where(counts > 0, (ends + tm - 1) // tm - first_blk, 0)
    vis_end = jnp.cumsum(nvis)
    vis_start = vis_end - nvis
    v = jnp.arange(n_visits, dtype=I32)
    e = jnp.minimum(jnp.sum((vis_end[None, :] <= v[:, None]).astype(I32), axis=1), N_EXPERTS - 1)
    valid = v < vis_end[-1]
    blk = first_blk[e] + (v - vis_start[e])
    lo = jnp.clip(starts[e] - blk * tm, 0, tm)
    hi = jnp.clip(ends[e] - blk * tm, 0, tm)
    last_e = jnp.max(jnp.where(counts > 0, jnp.arange(N_EXPERTS, dtype=I32), 0))
    blk = jnp.where(valid, blk, nblk - 1).astype(I32)
    e = jnp.where(valid, e, last_e).astype(I32)
    lo = jnp.where(valid, lo, 0).astype(I32)
    hi = jnp.where(valid, hi, 0).astype(I32)
    return blk, e, lo, hi


def _combine_kernel(dest_ref, ys_ref, x1_ref, wcol_ref, gain_ref, out_ref, buf_ref, sems):
    tm, d = x1_ref.shape
    part = tm // COMBINE_PARTS

    def start(r, c, sem):
        for k in range(TOP_K):
            pltpu.make_async_copy(ys_ref.at[dest_ref[k, r]], buf_ref.at[k, r], sem).start(priority=k)
        return c

    for h in range(COMBINE_PARTS):
        lax.fori_loop(h * part, (h + 1) * part, functools.partial(start, sem=sems.at[h]), 0, unroll=DMA_UNROLL)
    for h in range(COMBINE_PARTS):
        rows = pl.ds(h * part, part)
        for k in range(TOP_K):
            pltpu.make_async_copy(ys_ref.at[rows], buf_ref.at[k, rows], sems.at[h]).wait()
        y = (wcol_ref[rows, 0:1] * buf_ref[0, rows].reshape(part, d).astype(F32)
             + wcol_ref[rows, 1:2] * buf_ref[1, rows].reshape(part, d).astype(F32))
        out_ref[rows, :] = _rms_norm(x1_ref[rows, :] + y, gain_ref[...])


def _combine(dest, first_token, ys, x1, wcol, gain):
    t, d = x1.shape
    tm = COMBINE_TILE
    assert t % tm == 0 and first_token % tm == 0
    off = first_token // tm
    row = lambda w: pl.BlockSpec((tm, w), lambda i: (i, 0))
    return pl.pallas_call(
        _combine_kernel,
        grid=(t // tm,),
        in_specs=[pl.BlockSpec((TOP_K, tm), lambda i: (0, i + off), memory_space=pltpu.SMEM),
                  pl.BlockSpec(memory_space=pl.ANY), row(d), row(LANES),
                  pl.BlockSpec(gain.shape, lambda i: (0, 0))],
        out_specs=row(d),
        out_shape=jax.ShapeDtypeStruct((t, d), F32),
        scratch_shapes=[pltpu.VMEM((TOP_K, tm) + ys.shape[1:], ys.dtype), pltpu.SemaphoreType.DMA((COMBINE_PARTS,))],
        compiler_params=_cparams(("arbitrary",)),
        name="combine",
    )(dest, ys, x1, wcol, gain)


def _prepare_weights(w_in, w_gla_gate_up, b_gla_gate, w_branch, w_out, w_router_group, b_router_group,
                     w_router_expert, b_router_expert):
    d = w_in.shape[0]
    qk = H_A * DK_A
    mw = H_A * DV_A
    c = 0
    w_qa, c = w_in[:, c:c + qk], c + qk
    w_ka, c = w_in[:, c:c + qk], c + qk
    w_va, c = w_in[:, c:c + mw], c + mw
    w_ra, c = w_in[:, c:c + mw], c + mw
    w_lr, c = w_in[:, c:c + GATE_RANK], c + GATE_RANK
    w_b, c = w_in[:, c:c + 3 * mw], c + 3 * mw
    w_g = w_in[:, c:]
    wa = jnp.concatenate([w_qa, w_ka, w_va, w_ra,
                          jnp.pad(w_lr, ((0, 0), (0, LANES - GATE_RANK)))], axis=1).astype(BF16)
    wgu = jnp.pad(w_gla_gate_up, ((0, LANES - GATE_RANK), (0, 0))).astype(BF16)
    bgu = b_gla_gate[None, :]
    wr = jnp.zeros((LANES, d), F32)
    wr = wr.at[0:N_GROUPS].set(w_router_group.T).at[8:8 + N_EXPERTS].set(w_router_expert.T)
    br = jnp.zeros((LANES,), F32).at[0:N_GROUPS].set(b_router_group).at[8:8 + N_EXPERTS].set(b_router_expert)
    br = jnp.broadcast_to(br[:, None], (LANES, LANES))
    return dict(wa=wa, wqb=w_b[:, 0:mw].astype(BF16), wkvt=w_b[:, mw:3 * mw].T.astype(BF16),
                wg=w_g.astype(BF16), wgu=wgu, bgu=bgu,
                wb0=w_branch[0].astype(BF16), wb1=w_branch[1].astype(BF16), wo=w_out.astype(BF16),
                wr=wr, br=br)


def _mixers(x, s0, k_past, v_past, w, norm_mix_gain, gla_norm_gain, norm_ffn_gain):
    b, s, d = x.shape
    xf = x.reshape(b * s, d)
    qa, ka, va, ra, la, qb, kt, vt, kt16, vt16, gbr = _in_projection(
        xf, s, norm_mix_gain[None, :], w["wa"], w["wqb"], w["wkvt"], w["wg"], w["wgu"], w["bgu"])
    seq = lambda a: a.reshape(b, s, a.shape[-1])
    oa, s_new = _gla(seq(qa), seq(ka), seq(va), seq(ra), seq(la), s0, gla_norm_gain[None, :],
                     min(s, ROW_TILE))
    to_channel_major = lambda a: jnp.transpose(a, (0, 2, 3, 1))
    if k_past is None:
        ob = _sb_prompt(seq(qb), kt16, vt16)
    else:
        ob = _sb_sample(seq(qb), kt16, vt16, to_channel_major(k_past), to_channel_major(v_past))
    x1, h2, eid, wcol = _merge(oa.reshape(b * s, -1), ob.reshape(b * s, -1), gbr, xf, w["wb0"], w["wb1"],
                               w["wo"], norm_ffn_gain[None, :], w["wr"], w["br"])
    from_channel_major = lambda a: jnp.transpose(a.reshape(b, H_B, DH_B, s), (0, 3, 1, 2))
    return x1, h2, eid, wcol, s_new, from_channel_major(kt), from_channel_major(vt)


def kernel(x_prompt, x_sample, state_gla, cache_sb_k, cache_sb_v, norm_mix_gain, w_in, w_gla_gate_up, b_gla_gate, gla_norm_gain, w_branch, w_out, norm_ffn_gain, w_router_group, b_router_group, w_router_expert, b_router_expert, w_exp_gate, w_exp_up, w_exp_down, norm_final_gain):
    depth = w_in.shape[0]
    assert depth == 1, "one trunk layer per step"
    l = 0
    w = _prepare_weights(w_in[l], w_gla_gate_up[l], b_gla_gate[l], w_branch[l], w_out[l], w_router_group[l],
                         b_router_group[l], w_router_expert[l], b_router_expert[l])
    bp, sp, d = x_prompt.shape
    bs, ss, _ = x_sample.shape
    s0 = jnp.zeros((bp, H_A, DK_A, DV_A), x_prompt.dtype)
    x1p, h2p, eidp, wcolp, gla_p, k_p, v_p = _mixers(
        x_prompt, s0, None, None, w, norm_mix_gain[l], gla_norm_gain[l], norm_ffn_gain[l])
    x1s, h2s, eids, wcols, gla_s, k_s, v_s = _mixers(
        x_sample, state_gla[l], cache_sb_k[l], cache_sb_v[l], w, norm_mix_gain[l], gla_norm_gain[l],
        norm_ffn_gain[l])

    tp, ts = bp * sp, bs * ss
    eid = jnp.concatenate([eidp, eids], axis=1)
    dest_blocks, counts = _positions(eid.reshape(-1, SORT_WIDTH))
    dest = dest_blocks.reshape(TOP_K, tp + ts)
    xs = _dispatch(dest, h2p, h2s)
    vblk, vexp, vlo, vhi = _visit_plan(counts[:, 0], TOP_K * (tp + ts))
    ys = _experts(vblk, vexp, vlo, vhi, xs, w_exp_gate[l], w_exp_up[l], w_exp_down[l])
    gf = norm_final_gain[None, :]
    y_prompt = _combine(dest, 0, ys, x1p, wcolp, gf).reshape(bp, sp, d)
    y_sample = _combine(dest, tp, ys, x1s, wcols, gf).reshape(bs, ss, d)
    return (y_prompt, y_sample, gla_p[None], k_p[None], v_p[None], gla_s[None], k_s[None], v_s[None])
```

```python
import functools

import jax
import jax.numpy as jnp
from jax import lax
from jax.experimental import pallas as pl
from jax.experimental.pallas import tpu as pltpu

F32 = jnp.float32
BF16 = jnp.bfloat16
MOE_OUT_DTYPE = jnp.bfloat16
I32 = jnp.int32

LANES = 128
LOG2_E = 1.4426950408889634
RMS_EPS = 1e-6
GATE_TAU = 16.0
H_A = 4
DK_A = 64
DV_A = 128
GATE_RANK = 16
H_B = 8
DH_B = 64
N_GROUPS = 4
EXPERTS_PER_GROUP = 8
N_EXPERTS = N_GROUPS * EXPERTS_PER_GROUP
TOP_K = 2
GLA_CHUNK = 64
GLA_SUB = 16
GLA_EXP_CLAMP = 80.0
GLA_SEQS = 4
ROW_TILE = 256
SB_TILE = 256
SB_SAMPLE_SEQS = 2
MOE_TILE = 512
SORT_WIDTH = 256
INPROJ_TILE = 512
MERGE_TILE = 512
DISPATCH_TILE = 1024
COMBINE_TILE = 512
COMBINE_PARTS = 4
DMA_UNROLL = 8
VMEM_LIMIT = 56 * 1024 * 1024


def _cparams(sem):
    return pltpu.CompilerParams(dimension_semantics=sem, vmem_limit_bytes=VMEM_LIMIT)


def _dot(a, b):
    return jnp.dot(a, b, preferred_element_type=F32)


def _dot_nt(a, b):
    return lax.dot_general(a, b, (((1,), (1,)), ((), ())), preferred_element_type=F32)


def _dot_tn(a, b):
    return lax.dot_general(a, b, (((0,), (0,)), ((), ())), preferred_element_type=F32)


def _split_bf16(x):
    hi = x.astype(BF16)
    lo = (x - hi.astype(F32)).astype(BF16)
    return hi, lo


def _log_sigmoid(x):
    return jnp.minimum(x, 0.0) - jnp.log(1.0 + jnp.exp(-jnp.abs(x)))


def _sigmoid(x):
    return 1.0 / (1.0 + jnp.exp(-x))


def _rms_norm(x, gain):
    return x * lax.rsqrt(jnp.mean(x * x, axis=-1, keepdims=True) + RMS_EPS) * gain


def _inproj_kernel(x_ref, gain_ref, wa_ref, wqb_ref, wkvt_ref, wg_ref, wgu_ref, bgu_ref,
                   qa_ref, ka_ref, va_ref, ra_ref, la_ref, qb_ref, kt_ref, vt_ref,
                   kt16_ref, vt16_ref, gbr_ref):
    h = _rms_norm(x_ref[...], gain_ref[...]).astype(BF16)
    pa = H_A * DK_A
    mw = va_ref.shape[-1]
    kvt = _dot_nt(wkvt_ref[...], h)
    nseq, _, s = kt_ref.shape
    ntile, tile = kt16_ref.shape[1], kt16_ref.shape[3]
    for i in range(nseq):
        kt_ref[i] = kvt[0:mw, i * s:(i + 1) * s]
        vt_ref[i] = kvt[mw:2 * mw, i * s:(i + 1) * s]
        for j in range(ntile):
            cols = slice(i * s + j * tile, i * s + (j + 1) * tile)
            kt16_ref[i, j] = kvt[0:mw, cols].astype(BF16)
            vt16_ref[i, j] = kvt[mw:2 * mw, cols].astype(BF16)
    qb_ref[...] = _dot(h, wqb_ref[...]).astype(BF16)
    qa_ref[...] = _dot(h, wa_ref[:, 0:pa])
    ka_ref[...] = _dot(h, wa_ref[:, pa:2 * pa])
    va_ref[...] = _dot(h, wa_ref[:, 2 * pa:2 * pa + mw])
    ra_ref[...] = _dot(h, wa_ref[:, 2 * pa + mw:2 * pa + 2 * mw])
    lr = _dot(h, wa_ref[:, 2 * pa + 2 * mw:2 * pa + 2 * mw + LANES])
    gl = _dot(lr.astype(BF16), wgu_ref[...]) + bgu_ref[...]
    la_ref[...] = _log_sigmoid(gl) / GATE_TAU
    gbr_ref[...] = _dot(h, wg_ref[...]).astype(gbr_ref.dtype)


def _in_projection(x, seq_len, gain, wa, wqb, wkvt, wg, wgu, bgu):
    t, d = x.shape
    nb = t // seq_len
    pa = H_A * DK_A
    mw = wqb.shape[1]
    tm = INPROJ_TILE
    assert t % tm == 0
    row = lambda w: pl.BlockSpec((tm, w), lambda i: (i, 0))
    full = lambda a: pl.BlockSpec(a.shape, lambda i: (0,) * a.ndim, pipeline_mode=pl.Buffered(1))
    if seq_len >= tm:
        per_seq = seq_len // tm
        ntile = tm // SB_TILE
        assert tm % SB_TILE == 0 and seq_len % tm == 0
        kt_spec = pl.BlockSpec((1, mw, tm), lambda i: (i // per_seq, 0, i % per_seq))
        kt16_spec = pl.BlockSpec((1, ntile, mw, SB_TILE), lambda i: (i // per_seq, i % per_seq, 0, 0))
        kt16_shape = (nb, seq_len // SB_TILE, mw, SB_TILE)
    else:
        nseq = tm // seq_len
        assert tm % seq_len == 0
        kt_spec = pl.BlockSpec((nseq, mw, seq_len), lambda i: (i, 0, 0))
        kt16_spec = pl.BlockSpec((nseq, 1, mw, seq_len), lambda i: (i, 0, 0, 0))
        kt16_shape = (nb, 1, mw, seq_len)
    outs = [
        (jax.ShapeDtypeStruct((t, pa), F32), row(pa)), (jax.ShapeDtypeStruct((t, pa), F32), row(pa)),
        (jax.ShapeDtypeStruct((t, mw), F32), row(mw)), (jax.ShapeDtypeStruct((t, mw), F32), row(mw)),
        (jax.ShapeDtypeStruct((t, pa), F32), row(pa)),
        (jax.ShapeDtypeStruct((t, mw), BF16), row(mw)),
        (jax.ShapeDtypeStruct((nb, mw, seq_len), F32), kt_spec), (jax.ShapeDtypeStruct((nb, mw, seq_len), F32), kt_spec),
        (jax.ShapeDtypeStruct(kt16_shape, BF16), kt16_spec), (jax.ShapeDtypeStruct(kt16_shape, BF16), kt16_spec),
        (jax.ShapeDtypeStruct((t, wg.shape[1]), BF16), row(wg.shape[1])),
    ]
    return pl.pallas_call(
        _inproj_kernel,
        grid=(t // tm,),
        in_specs=[row(d), full(gain), full(wa), full(wqb), full(wkvt), full(wg), full(wgu), full(bgu)],
        out_specs=[spec for _, spec in outs],
        out_shape=[shape for shape, _ in outs],
        compiler_params=_cparams(("arbitrary",)),
        name="in_projection",
    )(x, gain, wa, wqb, wkvt, wg, wgu, bgu)


def _gla_chunk(q, k, v, b, st):
    c = q.shape[0]
    b_last = b[c - 1:c, :]
    rows = lax.broadcasted_iota(I32, (c, LANES), 0)
    nsub = c // GLA_SUB
    refs = [jnp.zeros((1, LANES), F32)] + [b[i * GLA_SUB - 1:i * GLA_SUB, :] for i in range(1, nsub)]
    ref_rows = refs[0]
    for i in range(1, nsub):
        ref_rows = jnp.where(rows >= i * GLA_SUB, refs[i], ref_rows)
    q_rel = q * jnp.exp(b - ref_rows)
    lhs = jnp.concatenate(
        [jnp.where((rows >= i * GLA_SUB) & (rows < (i + 1) * GLA_SUB), q_rel, 0.0) for i in range(nsub)],
        axis=1).astype(BF16)
    rhs = jnp.concatenate(
        [jnp.where(rows < (i + 1) * GLA_SUB, k * jnp.exp(jnp.minimum(refs[i] - b, GLA_EXP_CLAMP)), 0.0)
         for i in range(nsub)], axis=1).astype(BF16)
    att = _dot_nt(lhs, rhs)
    tt = lax.broadcasted_iota(I32, (c, c), 0)
    ss = lax.broadcasted_iota(I32, (c, c), 1)
    att = jnp.where(ss <= tt, att, 0.0)
    v16 = v.astype(BF16)
    inter = _dot_nt((q * jnp.exp(b)).astype(BF16), st.astype(BF16))
    intra = _dot(att.astype(BF16), v16)
    kd = (k * jnp.exp(b_last - b)).astype(BF16)
    st_new = st * jnp.exp(b_last) + _dot_tn(v16, kd)
    return inter + intra, st_new


def _gla_kernel(qa_ref, ka_ref, va_ref, ra_ref, la_ref, s0_ref, gain_ref, o_ref, sfin_ref, st_ref):
    j = pl.program_id(1)
    nj = pl.num_programs(1)
    nseq, rows_per_step, _ = qa_ref.shape
    c = GLA_CHUNK
    zpad = jnp.zeros((LANES - DK_A, DV_A), F32)

    def state_rows(h):
        return slice((h % 2) * DK_A, (h % 2 + 1) * DK_A)

    @pl.when(j == 0)
    def _():
        for si in range(nseq):
            for h in range(H_A):
                parts = [s0_ref[si, h], zpad] if h % 2 == 0 else [zpad, s0_ref[si, h]]
                st_ref[si * H_A + h] = jnp.concatenate(parts, axis=0).T

    ti = lax.broadcasted_iota(I32, (rows_per_step, rows_per_step), 0)
    si = lax.broadcasted_iota(I32, (rows_per_step, rows_per_step), 1)
    chunk_shift = c.bit_length() - 1
    same_chunk = (ti >> chunk_shift) == (si >> chunk_shift)
    tril_blocks = jnp.where(same_chunk & (si <= ti), 1.0, 0.0).astype(BF16)
    gain = gain_ref[...]
    lane = lax.broadcasted_iota(I32, (1, LANES), 1)
    half_masks = (lane < DK_A, lane >= DK_A)
    for si in range(nseq):
        la_hi, la_lo = _split_bf16(la_ref[si])
        b_all = _dot(tril_blocks, la_hi) + _dot(tril_blocks, la_lo)
        for h in range(H_A):
            hp = slice((h // 2) * LANES, (h // 2 + 1) * LANES)
            hv = slice(h * DV_A, (h + 1) * DV_A)
            mine = half_masks[h % 2]
            st = st_ref[si * H_A + h]
            for ci in range(rows_per_step // c):
                r0 = ci * c
                q = jnp.where(mine, qa_ref[si, r0:r0 + c, hp], 0.0) * (DK_A ** -0.5)
                k = jnp.where(mine, ka_ref[si, r0:r0 + c, hp], 0.0)
                o, st = _gla_chunk(q, k, va_ref[si, r0:r0 + c, hv], b_all[r0:r0 + c, hp], st)
                r = ra_ref[si, r0:r0 + c, hv]
                o = _rms_norm(o, gain) * (r * _sigmoid(r))
                o_ref[si, r0:r0 + c, hv] = o.astype(o_ref.dtype)
            st_ref[si * H_A + h] = st

    @pl.when(j == nj - 1)
    def _():
        for si in range(nseq):
            for h in range(H_A):
                sfin_ref[si, h] = st_ref[si * H_A + h].T[state_rows(h), :]


def _gla(qa, ka, va, ra, la, s0, gain, rows_per_step):
    assert 2 * DK_A == LANES and H_A % 2 == 0
    b, s, pa = qa.shape
    mw = va.shape[-1]
    ns = GLA_SEQS
    assert b % ns == 0
    seq = lambda w: pl.BlockSpec((ns, rows_per_step, w), lambda i, j: (i, j, 0))
    state = pl.BlockSpec((ns, H_A, DK_A, DV_A), lambda i, j: (i, 0, 0, 0))
    return pl.pallas_call(
        _gla_kernel,
        grid=(b // ns, s // rows_per_step),
        in_specs=[seq(pa), seq(pa), seq(mw), seq(mw), seq(pa), state,
                  pl.BlockSpec(gain.shape, lambda i, j: (0, 0))],
        out_specs=[seq(mw), state],
        out_shape=[jax.ShapeDtypeStruct((b, s, mw), BF16),
                   jax.ShapeDtypeStruct((b, H_A, DK_A, DV_A), F32)],
        scratch_shapes=[pltpu.VMEM((ns * H_A, LANES, LANES), F32)],
        compiler_params=_cparams(("arbitrary", "arbitrary")),
        name="gla",
    )(qa, ka, va, ra, la, s0, gain)


def _head_lane_masks():
    lane = lax.broadcasted_iota(I32, (1, LANES), 1)
    return lane < DH_B, lane >= DH_B


def _sb_neg_tri(tk):
    ji = lax.broadcasted_iota(I32, (tk, tk), 0)
    si = lax.broadcasted_iota(I32, (tk, tk), 1)
    return jnp.where(ji >= si, -1.0, 0.0).astype(BF16)


def _sb_stack_queries(q, qs_ref, base=0):
    m0, m1 = _head_lane_masks()
    for p in range(q.shape[1] // LANES):
        qp = (q[:, p * LANES:(p + 1) * LANES].astype(F32) * (DH_B ** -0.5 * LOG2_E)).astype(BF16)
        zero = jnp.zeros_like(qp)
        qs_ref[base + p] = jnp.concatenate([jnp.where(m0, qp, zero), jnp.where(m1, qp, zero)], axis=0)


def _pair_lanes(p):
    return slice(p * LANES, (p + 1) * LANES)


def _lane_fit(x, width):
    if width >= LANES:
        return jnp.concatenate([x] * (width // LANES), axis=1)
    return x[:, 0:width]


def _sb_tile_step(qs_ref, acc_ref, carry_ref, k_tile, v_tile, ntri, diagonal, one_suffix_matmul=False):
    npair, rows, _ = qs_ref.shape
    tq = rows // 2
    tk = ntri.shape[1]
    m0, _ = _head_lane_masks()
    if diagonal:
        t = lax.broadcasted_iota(I32, (rows, tk), 0)
        t = jnp.where(t >= tq, t - tq, t)
        visible = lax.broadcasted_iota(I32, (rows, tk), 1) < t
    def scores(p):
        z = _dot(qs_ref[p], k_tile(p))
        sp = jnp.maximum(z, 0.0) + jnp.log2(1.0 + jnp.exp2(-jnp.abs(z)))
        if diagonal:
            sp = jnp.where(visible, sp, 0.0)
        return z, sp.astype(BF16)

    if one_suffix_matmul:
        zs, sps = zip(*[scores(p) for p in range(npair)])
        stacked = _dot(jnp.concatenate(sps, axis=0), ntri)
        suffixes = [stacked[p * rows:(p + 1) * rows] for p in range(npair)]
    for p in range(npair):
        if one_suffix_matmul:
            z, suffix = zs[p], suffixes[p]
        else:
            z, sp = scores(p)
            suffix = _dot(sp, ntri)
        carry = carry_ref[p]
        w = jnp.exp2(z + suffix + _lane_fit(carry, tk))
        if diagonal:
            w = jnp.where(visible, w, 0.0)
        pv = _dot_nt(w.astype(BF16), v_tile(p))
        acc_ref[p] += jnp.where(m0, pv[0:tq], pv[tq:rows])
        carry_ref[p] = carry + jnp.broadcast_to(suffix[:, 0:1], carry.shape)


def _sb_prompt_kernel(q_ref, k_ref, v_ref, o_ref, qs_ref, acc_ref, carry_ref):
    qi = pl.program_id(1)
    tk = SB_TILE
    _sb_stack_queries(q_ref[0], qs_ref)
    acc_ref[...] = jnp.zeros_like(acc_ref)
    carry_ref[...] = jnp.zeros_like(carry_ref)
    ntri = _sb_neg_tri(tk)

    def step(jb, diagonal):
        _sb_tile_step(qs_ref, acc_ref, carry_ref, lambda p: k_ref[0, jb, _pair_lanes(p), :],
                      lambda p: v_ref[0, jb, _pair_lanes(p), :], ntri, diagonal, True)

    step(qi, True)

    def body(i, c):
        step(qi - 1 - 2 * i, False)
        step(qi - 2 - 2 * i, False)
        return c

    lax.fori_loop(0, qi // 2, body, 0)

    @pl.when(qi % 2 == 1)
    def _():
        step(0, False)

    for p in range(acc_ref.shape[0]):
        o_ref[0, :, p * LANES:(p + 1) * LANES] = acc_ref[p].astype(o_ref.dtype)


def _sb_scratch(tq, npair):
    return [pltpu.VMEM((npair, 2 * tq, LANES), BF16), pltpu.VMEM((npair, tq, LANES), F32),
            pltpu.VMEM((npair, 2 * tq, LANES), F32)]


def _sb_prompt(q, kt, vt):
    b, s, w = q.shape
    tq = SB_TILE
    assert kt.shape == (b, s // tq, w, tq)
    qspec = pl.BlockSpec((1, tq, w), lambda i, j: (i, j, 0))
    kvspec = pl.BlockSpec((1,) + kt.shape[1:], lambda i, j: (i, 0, 0, 0))
    return pl.pallas_call(
        _sb_prompt_kernel,
        grid=(b, s // tq),
        in_specs=[qspec, kvspec, kvspec],
        out_specs=qspec,
        out_shape=jax.ShapeDtypeStruct((b, s, w), BF16),
        scratch_shapes=_sb_scratch(tq, w // LANES),
        compiler_params=_cparams(("arbitrary", "arbitrary")),
        name="sb_prompt",
    )(q, kt, vt)


def _sb_sample_kernel(q_ref, kn_ref, vn_ref, kp_ref, vp_ref, o_ref, qs_ref, acc_ref, carry_ref):
    nseq, sq, w = q_ref.shape
    past = kp_ref.shape[3]
    npair = w // LANES
    tk = SB_TILE
    for si in range(nseq):
        _sb_stack_queries(q_ref[si], qs_ref, si * npair)
    acc_ref[...] = jnp.zeros_like(acc_ref)
    carry_ref[...] = jnp.zeros_like(carry_ref)
    _sb_tile_step(qs_ref, acc_ref, carry_ref, lambda e: kn_ref[e // npair, 0, _pair_lanes(e % npair), :],
                  lambda e: vn_ref[e // npair, 0, _pair_lanes(e % npair), :], _sb_neg_tri(sq), True, True)
    ntri = _sb_neg_tri(tk)

    def body(i, c):
        cols = pl.ds(pl.multiple_of(past - (i + 1) * tk, tk), tk)

        def pair(ref, e):
            p = e % npair
            return ref[e // npair, 2 * p:2 * p + 2, :, cols].reshape(LANES, tk).astype(BF16)

        _sb_tile_step(qs_ref, acc_ref, carry_ref, lambda e: pair(kp_ref, e), lambda e: pair(vp_ref, e), ntri, False,
                      True)
        return c

    lax.fori_loop(0, past // tk, body, 0)
    for e in range(acc_ref.shape[0]):
        o_ref[e // npair, :, _pair_lanes(e % npair)] = acc_ref[e].astype(o_ref.dtype)


def _sb_sample(q, kt_new, vt_new, kt_past, vt_past):
    b, sq, w = q.shape
    past = kt_past.shape[3]
    ns = SB_SAMPLE_SEQS
    assert past % SB_TILE == 0 and 2 * DH_B == LANES and b % ns == 0
    qspec = pl.BlockSpec((ns, sq, w), lambda i: (i, 0, 0))
    new = pl.BlockSpec((ns, 1, w, sq), lambda i: (i, 0, 0, 0))
    old = pl.BlockSpec((ns, H_B, DH_B, past), lambda i: (i, 0, 0, 0))
    return pl.pallas_call(
        _sb_sample_kernel,
        grid=(b // ns,),
        in_specs=[qspec, new, new, old, old],
        out_specs=qspec,
        out_shape=jax.ShapeDtypeStruct((b, sq, w), BF16),
        scratch_shapes=_sb_scratch(sq, ns * (w // LANES)),
        compiler_params=_cparams(("arbitrary",)),
        name="sb_sample",
    )(q, kt_new, vt_new, kt_past, vt_past)


def _first_argmax(vals, nrows):
    idx = lax.broadcasted_iota(I32, vals.shape, 0)
    top = jnp.max(vals, axis=0, keepdims=True)
    first = jnp.min(jnp.where(vals == top, idx, nrows), axis=0, keepdims=True)
    return top, first, idx


def _merge_kernel(oa_ref, ob_ref, g_ref, x_ref, wb0_ref, wb1_ref, wo_ref, gain_ref, wr_ref, br_ref,
                  x1_ref, h2_ref, eid_ref, wcol_ref):
    d = x_ref.shape[1]
    ya = _dot(oa_ref[...], wb0_ref[...])
    yb = _dot(ob_ref[...], wb1_ref[...])
    g = g_ref[...].astype(F32)
    m = _sigmoid(g[:, 0:d]) * ya + _sigmoid(g[:, d:2 * d]) * yb
    x1 = x_ref[...] + _dot(m.astype(BF16), wo_ref[...])
    x1_ref[...] = x1
    h2 = _rms_norm(x1, gain_ref[...])
    h2_ref[...] = h2.astype(h2_ref.dtype).reshape(h2_ref.shape)

    h_hi, h_lo = _split_bf16(h2)
    w_hi, w_lo = _split_bf16(wr_ref[...])
    lt = _dot_nt(w_hi, h_hi) + _dot_nt(w_hi, h_lo) + _dot_nt(w_lo, h_hi) + br_ref[:, 0:1]
    gl = lt[0:N_GROUPS, :]
    g_top, g_idx, _ = _first_argmax(gl, N_GROUPS)
    g_e = jnp.exp(gl - g_top)
    g_p = jnp.max(g_e / jnp.sum(g_e, axis=0, keepdims=True), axis=0, keepdims=True)
    el = jnp.zeros((EXPERTS_PER_GROUP, lt.shape[1]), F32)
    for g in range(N_GROUPS):
        r0 = 8 + g * EXPERTS_PER_GROUP
        el = jnp.where(g_idx == g, lt[r0:r0 + EXPERTS_PER_GROUP, :], el)
    e_top, i1, eidx = _first_argmax(el, EXPERTS_PER_GROUP)
    e_e = jnp.exp(el - e_top)
    e_p = e_e / jnp.sum(e_e, axis=0, keepdims=True)
    p1 = jnp.max(e_p, axis=0, keepdims=True)
    rest = jnp.where(eidx == i1, -1.0, e_p)
    p2, i2, _ = _first_argmax(rest, EXPERTS_PER_GROUP)
    norm = p1 + p2
    w1 = g_p * (p1 / norm)
    w2 = g_p * (p2 / norm)
    eid_ref[...] = jnp.concatenate([g_idx * EXPERTS_PER_GROUP + i1, g_idx * EXPERTS_PER_GROUP + i2], axis=0)
    rows = lax.broadcasted_iota(I32, (LANES, lt.shape[1]), 0)
    wrows = jnp.where(rows == 0, w1, jnp.where(rows == 1, w2, 0.0))
    wcol_ref[...] = wrows.T


def _merge(oa, ob, gbr, x, wb0, wb1, wo, gain, wr, br):
    t, d = x.shape
    tm = MERGE_TILE
    assert t % tm == 0
    row = lambda w: pl.BlockSpec((tm, w), lambda i: (i, 0))
    full = lambda a: pl.BlockSpec(a.shape, lambda i: (0,) * a.ndim)
    return pl.pallas_call(
        _merge_kernel,
        grid=(t // tm,),
        in_specs=[row(oa.shape[1]), row(ob.shape[1]), row(gbr.shape[1]), row(d),
                  full(wb0), full(wb1), full(wo), full(gain), full(wr), full(br)],
        out_specs=[row(d), pl.BlockSpec((tm, d // LANES, LANES), lambda i: (i, 0, 0)),
                   pl.BlockSpec((TOP_K, tm), lambda i: (0, i)), row(LANES)],
        out_shape=[jax.ShapeDtypeStruct((t, d), F32), jax.ShapeDtypeStruct((t, d // LANES, LANES), BF16),
                   jax.ShapeDtypeStruct((TOP_K, t), I32), jax.ShapeDtypeStruct((t, LANES), F32)],
        compiler_params=_cparams(("arbitrary",)),
        name="merge_router",
    )(oa, ob, gbr, x, wb0, wb1, wo, gain, wr, br)


def _positions_kernel(eid_ref, dest_ref, counts_ref, rank_ref):
    nblk, width = eid_ref.shape
    ji = lax.broadcasted_iota(I32, (width, width), 0)
    si = lax.broadcasted_iota(I32, (width, width), 1)
    prefix = jnp.where(ji <= si, 1.0, 0.0).astype(BF16)
    expert = lax.broadcasted_iota(I32, (N_EXPERTS, width), 0)
    group = max(g for g in (8, 4, 2, 1) if nblk % g == 0)

    def onehot(i):
        return expert == eid_ref[pl.ds(i, 1), :]

    def rank_body(ig, run):
        first = pl.multiple_of(ig * group, group)
        ohs = [onehot(first + j) for j in range(group)]
        stacked = jnp.concatenate([jnp.where(oh, 1.0, 0.0) for oh in ohs], axis=0).astype(BF16)
        cum = _dot(stacked, prefix)
        ranks = []
        for j, oh in enumerate(ohs):
            cum_j = cum[j * N_EXPERTS:(j + 1) * N_EXPERTS, :] + run
            ranks.append(jnp.sum(jnp.where(oh, cum_j, 0.0), axis=0, keepdims=True) - 1.0)
            run = cum_j[:, width - 1:width]
        rank_ref[pl.ds(first, group), :] = jnp.concatenate(ranks, axis=0)
        return run

    counts = lax.fori_loop(0, nblk // group, rank_body, jnp.zeros((N_EXPERTS, 1), F32))
    counts_ref[...] = jnp.broadcast_to(counts, counts_ref.shape).astype(I32)
    c_hi = jnp.floor(counts * (1.0 / 256.0))
    c_lo = counts - 256.0 * c_hi
    ei = lax.broadcasted_iota(I32, (N_EXPERTS, N_EXPERTS), 0)
    ej = lax.broadcasted_iota(I32, (N_EXPERTS, N_EXPERTS), 1)
    strict = jnp.where(ej < ei, 1.0, 0.0).astype(BF16)
    digits = jnp.concatenate([jnp.broadcast_to(c_hi, (N_EXPERTS, LANES)),
                              jnp.broadcast_to(c_lo, (N_EXPERTS, LANES))], axis=1).astype(BF16)
    sums = _dot(strict, digits)
    start = 256.0 * sums[:, 0:1] + sums[:, LANES:LANES + 1]

    def dest_body(ig, carry):
        first = pl.multiple_of(ig * group, group)
        offs = [jnp.sum(jnp.where(onehot(first + j), start, 0.0), axis=0, keepdims=True) for j in range(group)]
        rows = pl.ds(first, group)
        dest_ref[rows, :] = (rank_ref[rows, :] + jnp.concatenate(offs, axis=0)).astype(I32)
        return carry

    lax.fori_loop(0, nblk // group, dest_body, 0)


def _positions(eid_blocks):
    nblk, width = eid_blocks.shape
    vm = lambda shape: pl.BlockSpec(shape, lambda: (0,) * len(shape))
    return pl.pallas_call(
        _positions_kernel,
        in_specs=[vm((nblk, width))],
        out_specs=[vm((nblk, width)), vm((N_EXPERTS, LANES))],
        out_shape=[jax.ShapeDtypeStruct((nblk, width), I32), jax.ShapeDtypeStruct((N_EXPERTS, LANES), I32)],
        scratch_shapes=[pltpu.VMEM((nblk, width), F32)],
        name="positions",
    )(eid_blocks)


def _dispatch_kernel(n_prompt_tiles, dest_ref, hp_ref, hs_ref, xs_ref, sem):
    i = pl.program_id(0)
    tm = dest_ref.shape[1]

    def scatter(src_ref):
        def start(r, c):
            for k in range(TOP_K):
                pltpu.make_async_copy(src_ref.at[r], xs_ref.at[dest_ref[k, r]], sem).start(priority=k)
            return c

        lax.fori_loop(0, tm, start, 0, unroll=DMA_UNROLL)
        for k in range(TOP_K):
            pltpu.make_async_copy(src_ref, xs_ref.at[pl.ds(0, tm)], sem).wait()

    @pl.when(i < n_prompt_tiles)
    def _():
        scatter(hp_ref)

    @pl.when(i >= n_prompt_tiles)
    def _():
        scatter(hs_ref)


def _dispatch(dest, h_prompt, h_sample):
    t = dest.shape[1]
    slab = h_prompt.shape[1:]
    tm = DISPATCH_TILE
    assert h_prompt.shape[0] % tm == 0 and h_sample.shape[0] % tm == 0
    npt = h_prompt.shape[0] // tm
    return pl.pallas_call(
        functools.partial(_dispatch_kernel, npt),
        grid=(t // tm,),
        in_specs=[pl.BlockSpec((TOP_K, tm), lambda i: (0, i), memory_space=pltpu.SMEM),
                  pl.BlockSpec((tm,) + slab, lambda i: (jnp.minimum(i, npt - 1), 0, 0)),
                  pl.BlockSpec((tm,) + slab, lambda i: (jnp.maximum(i - npt, 0), 0, 0))],
        out_specs=pl.BlockSpec(memory_space=pl.ANY),
        out_shape=jax.ShapeDtypeStruct((TOP_K * t,) + slab, h_prompt.dtype),
        scratch_shapes=[pltpu.SemaphoreType.DMA(())],
        compiler_params=_cparams(("arbitrary",)),
        name="dispatch",
    )(dest, h_prompt, h_sample)


def _experts_kernel(vblk_ref, vexp_ref, vlo_ref, vhi_ref, vnext_ref, vslot_ref, xs_ref, wg_ref, wu_ref, wd_ref,
                    ys_ref, wg32_ref, wu32_ref, wd32_ref, wg16_ref, wu16_ref, wd16_ref, sems):
    v = pl.program_id(0)
    lo = vlo_ref[v]
    hi = vhi_ref[v]
    prev = jnp.maximum(v - 1, 0)
    first = jnp.logical_or(v == 0, vblk_ref[v] != vblk_ref[prev])
    new_expert = jnp.logical_or(v == 0, vexp_ref[v] != vexp_ref[prev])

    def weight_copies(e, slot):
        return [pltpu.make_async_copy(src.at[e], dst.at[slot], sems.at[slot])
                for src, dst in ((wg_ref, wg32_ref), (wu_ref, wu32_ref), (wd_ref, wd32_ref))]

    @pl.when(v == 0)
    def _():
        for cp in weight_copies(vexp_ref[0], 0):
            cp.start()

    @pl.when(new_expert)
    def _():
        slot = vslot_ref[v]
        for cp in weight_copies(vexp_ref[v], slot):
            cp.wait()
        wg16_ref[...] = wg32_ref[slot].astype(BF16)
        wu16_ref[...] = wu32_ref[slot].astype(BF16)
        wd16_ref[...] = wd32_ref[slot].astype(BF16)

        @pl.when(vnext_ref[v] >= 0)
        def _():
            for cp in weight_copies(vnext_ref[v], 1 - slot):
                cp.start()

    @pl.when(first)
    def _():
        ys_ref[...] = jnp.zeros_like(ys_ref)

    @pl.when(hi > lo)
    def _():
        tm = xs_ref.shape[0]
        d = wg_ref.shape[1]
        x = xs_ref[...].reshape(tm, d)
        gate = _dot(x, wg16_ref[...])
        up = _dot(x, wu16_ref[...])
        hid = (gate * _sigmoid(gate) * up).astype(BF16)
        y = _dot(hid, wd16_ref[...]).astype(ys_ref.dtype)
        rows = lax.broadcasted_iota(I32, y.shape, 0)
        mine = (rows >= lo) & (rows < hi)
        ys_ref[...] = jnp.where(mine, y, ys_ref[...].reshape(tm, d)).reshape(ys_ref.shape)


def _experts(plan, xs, wg, wu, wd):
    a = xs.shape[0]
    slab = xs.shape[1:]
    d, de = wg.shape[1:]
    tm = MOE_TILE
    block = lambda v, b, *_: (b[v], 0, 0)
    hbm = pl.BlockSpec(memory_space=pl.ANY)
    grid_spec = pltpu.PrefetchScalarGridSpec(
        num_scalar_prefetch=len(plan),
        grid=(plan[0].shape[0],),
        in_specs=[pl.BlockSpec((tm,) + slab, block), hbm, hbm, hbm],
        out_specs=pl.BlockSpec((tm,) + slab, block),
        scratch_shapes=[pltpu.VMEM((2, d, de), F32), pltpu.VMEM((2, d, de), F32), pltpu.VMEM((2, de, d), F32),
                        pltpu.VMEM((d, de), BF16), pltpu.VMEM((d, de), BF16), pltpu.VMEM((de, d), BF16),
                        pltpu.SemaphoreType.DMA((2,))],
    )
    return pl.pallas_call(
        _experts_kernel,
        grid_spec=grid_spec,
        out_shape=jax.ShapeDtypeStruct((a,) + slab, MOE_OUT_DTYPE),
        compiler_params=_cparams(("arbitrary",)),
        name="experts",
    )(*plan, xs, wg, wu, wd)


def _visit_plan(counts, n_rows):
    tm = MOE_TILE
    nblk = n_rows // tm
    n_visits = nblk + N_EXPERTS - 1
    ends = jnp.cumsum(counts)
    starts = ends - counts
    first_blk = starts // tm
    nvis = jnp.where(counts > 0, (ends + tm - 1) // tm - first_blk, 0)
    vis_end = jnp.cumsum(nvis)
    vis_start = vis_end - nvis
    v = jnp.arange(n_visits, dtype=I32)
    e = jnp.minimum(jnp.sum((vis_end[None, :] <= v[:, None]).astype(I32), axis=1), N_EXPERTS - 1)
    valid = v < vis_end[-1]
    blk = first_blk[e] + (v - vis_start[e])
    lo = jnp.clip(starts[e] - blk * tm, 0, tm)
    hi = jnp.clip(ends[e] - blk * tm, 0, tm)
    ids = jnp.arange(N_EXPERTS, dtype=I32)
    used = counts > 0
    last_e = jnp.max(jnp.where(used, ids, 0))
    blk = jnp.where(valid, blk, nblk - 1).astype(I32)
    e = jnp.where(valid, e, last_e).astype(I32)
    lo = jnp.where(valid, lo, 0).astype(I32)
    hi = jnp.where(valid, hi, 0).astype(I32)
    later_used = used[None, :] & (ids[None, :] > ids[:, None])
    next_used = jnp.min(jnp.where(later_used, ids[None, :], N_EXPERTS), axis=1)
    next_used = jnp.where(next_used < N_EXPERTS, next_used, -1).astype(I32)
    slot = ((jnp.cumsum(used.astype(I32)) - 1) % 2).astype(I32)
    return blk, e, lo, hi, next_used[e], slot[e]


def _combine_kernel(dest_ref, ys_ref, x1_ref, wcol_ref, gain_ref, out_ref, buf_ref, sems):
    tm, d = x1_ref.shape
    part = tm // COMBINE_PARTS

    def start(r, c, sem):
        for k in range(TOP_K):
            pltpu.make_async_copy(ys_ref.at[dest_ref[k, r]], buf_ref.at[k, r], sem).start(priority=k)
        return c

    for h in range(COMBINE_PARTS):
        lax.fori_loop(h * part, (h + 1) * part, functools.partial(start, sem=sems.at[h]), 0, unroll=DMA_UNROLL)
    for h in range(COMBINE_PARTS):
        rows = pl.ds(h * part, part)
        for k in range(TOP_K):
            pltpu.make_async_copy(ys_ref.at[rows], buf_ref.at[k, rows], sems.at[h]).wait()
        y = (wcol_ref[rows, 0:1] * buf_ref[0, rows].reshape(part, d).astype(F32)
             + wcol_ref[rows, 1:2] * buf_ref[1, rows].reshape(part, d).astype(F32))
        out_ref[rows, :] = _rms_norm(x1_ref[rows, :] + y, gain_ref[...])


def _combine(dest, first_token, ys, x1, wcol, gain):
    t, d = x1.shape
    tm = COMBINE_TILE
    assert t % tm == 0 and first_token % tm == 0
    off = first_token // tm
    row = lambda w: pl.BlockSpec((tm, w), lambda i: (i, 0))
    return pl.pallas_call(
        _combine_kernel,
        grid=(t // tm,),
        in_specs=[pl.BlockSpec((TOP_K, tm), lambda i: (0, i + off), memory_space=pltpu.SMEM),
                  pl.BlockSpec(memory_space=pl.ANY), row(d), row(LANES),
                  pl.BlockSpec(gain.shape, lambda i: (0, 0))],
        out_specs=row(d),
        out_shape=jax.ShapeDtypeStruct((t, d), F32),
        scratch_shapes=[pltpu.VMEM((TOP_K, tm) + ys.shape[1:], ys.dtype), pltpu.SemaphoreType.DMA((COMBINE_PARTS,))],
        compiler_params=_cparams(("arbitrary",)),
        name="combine",
    )(dest, ys, x1, wcol, gain)


def _prepare_weights(w_in, w_gla_gate_up, b_gla_gate, w_branch, w_out, w_router_group, b_router_group,
                     w_router_expert, b_router_expert):
    d = w_in.shape[0]
    qk = H_A * DK_A
    mw = H_A * DV_A
    c = 0
    w_qa, c = w_in[:, c:c + qk], c + qk
    w_ka, c = w_in[:, c:c + qk], c + qk
    w_va, c = w_in[:, c:c + mw], c + mw
    w_ra, c = w_in[:, c:c + mw], c + mw
    w_lr, c = w_in[:, c:c + GATE_RANK], c + GATE_RANK
    w_b, c = w_in[:, c:c + 3 * mw], c + 3 * mw
    w_g = w_in[:, c:]
    wa = jnp.concatenate([w_qa, w_ka, w_va, w_ra,
                          jnp.pad(w_lr, ((0, 0), (0, LANES - GATE_RANK)))], axis=1).astype(BF16)
    wgu = jnp.pad(w_gla_gate_up, ((0, LANES - GATE_RANK), (0, 0))).astype(BF16)
    bgu = b_gla_gate[None, :]
    wr = jnp.zeros((LANES, d), F32)
    wr = wr.at[0:N_GROUPS].set(w_router_group.T).at[8:8 + N_EXPERTS].set(w_router_expert.T)
    br = jnp.zeros((LANES,), F32).at[0:N_GROUPS].set(b_router_group).at[8:8 + N_EXPERTS].set(b_router_expert)
    br = jnp.broadcast_to(br[:, None], (LANES, LANES))
    return dict(wa=wa, wqb=w_b[:, 0:mw].astype(BF16), wkvt=w_b[:, mw:3 * mw].T.astype(BF16),
                wg=w_g.astype(BF16), wgu=wgu, bgu=bgu,
                wb0=w_branch[0].astype(BF16), wb1=w_branch[1].astype(BF16), wo=w_out.astype(BF16),
                wr=wr, br=br)


def _mixers(x, s0, k_past, v_past, w, norm_mix_gain, gla_norm_gain, norm_ffn_gain):
    b, s, d = x.shape
    xf = x.reshape(b * s, d)
    qa, ka, va, ra, la, qb, kt, vt, kt16, vt16, gbr = _in_projection(
        xf, s, norm_mix_gain[None, :], w["wa"], w["wqb"], w["wkvt"], w["wg"], w["wgu"], w["bgu"])
    seq = lambda a: a.reshape(b, s, a.shape[-1])
    oa, s_new = _gla(seq(qa), seq(ka), seq(va), seq(ra), seq(la), s0, gla_norm_gain[None, :],
                     min(s, ROW_TILE))
    to_channel_major = lambda a: jnp.transpose(a, (0, 2, 3, 1))
    if k_past is None:
        ob = _sb_prompt(seq(qb), kt16, vt16)
    else:
        ob = _sb_sample(seq(qb), kt16, vt16, to_channel_major(k_past), to_channel_major(v_past))
    x1, h2, eid, wcol = _merge(oa.reshape(b * s, -1), ob.reshape(b * s, -1), gbr, xf, w["wb0"], w["wb1"],
                               w["wo"], norm_ffn_gain[None, :], w["wr"], w["br"])
    from_channel_major = lambda a: jnp.transpose(a.reshape(b, H_B, DH_B, s), (0, 3, 1, 2))
    return x1, h2, eid, wcol, s_new, from_channel_major(kt), from_channel_major(vt)


def kernel(x_prompt, x_sample, state_gla, cache_sb_k, cache_sb_v, norm_mix_gain, w_in, w_gla_gate_up, b_gla_gate, gla_norm_gain, w_branch, w_out, norm_ffn_gain, w_router_group, b_router_group, w_router_expert, b_router_expert, w_exp_gate, w_exp_up, w_exp_down, norm_final_gain):
    depth = w_in.shape[0]
    assert depth == 1, "one trunk layer per step"
    l = 0
    w = _prepare_weights(w_in[l], w_gla_gate_up[l], b_gla_gate[l], w_branch[l], w_out[l], w_router_group[l],
                         b_router_group[l], w_router_expert[l], b_router_expert[l])
    bp, sp, d = x_prompt.shape
    bs, ss, _ = x_sample.shape
    s0 = jnp.zeros((bp, H_A, DK_A, DV_A), x_prompt.dtype)
    x1p, h2p, eidp, wcolp, gla_p, k_p, v_p = _mixers(
        x_prompt, s0, None, None, w, norm_mix_gain[l], gla_norm_gain[l], norm_ffn_gain[l])
    x1s, h2s, eids, wcols, gla_s, k_s, v_s = _mixers(
        x_sample, state_gla[l], cache_sb_k[l], cache_sb_v[l], w, norm_mix_gain[l], gla_norm_gain[l],
        norm_ffn_gain[l])

    tp, ts = bp * sp, bs * ss
    eid = jnp.concatenate([eidp, eids], axis=1)
    dest_blocks, counts = _positions(eid.reshape(-1, SORT_WIDTH))
    dest = dest_blocks.reshape(TOP_K, tp + ts)
    xs = _dispatch(dest, h2p, h2s)
    plan = _visit_plan(counts[:, 0], TOP_K * (tp + ts))
    ys = _experts(plan, xs, w_exp_gate[l], w_exp_up[l], w_exp_down[l])
    gf = norm_final_gain[None, :]
    y_prompt = _combine(dest, 0, ys, x1p, wcolp, gf).reshape(bp, sp, d)
    y_sample = _combine(dest, tp, ys, x1s, wcols, gf).reshape(bs, ss, d)
    return (y_prompt, y_sample, gla_p[None], k_p[None], v_p[None], gla_s[None], k_s[None], v_s[None])
```

```python
import functools

import jax
import jax.numpy as jnp
from jax import lax
from jax.experimental import pallas as pl
from jax.experimental.pallas import tpu as pltpu

F32 = jnp.float32
BF16 = jnp.bfloat16
MOE_OUT_DTYPE = jnp.bfloat16
I32 = jnp.int32

LANES = 128
LOG2_E = 1.4426950408889634
RMS_EPS = 1e-6
GATE_TAU = 16.0
H_A = 4
DK_A = 64
DV_A = 128
GATE_RANK = 16
H_B = 8
DH_B = 64
N_GROUPS = 4
EXPERTS_PER_GROUP = 8
N_EXPERTS = N_GROUPS * EXPERTS_PER_GROUP
TOP_K = 2
GLA_CHUNK = 64
GLA_SUB = 16
GLA_EXP_CLAMP = 80.0
GLA_SEQS = 8
GLA_ROWS = 256
SB_TILE = 256
SB_SAMPLE_SEQS = 2
MOE_TILE = 512
SORT_WIDTH = 256
INPROJ_TILE = 512
MERGE_TILE = 512
DISPATCH_TILE = 1024
COMBINE_TILE = 512
COMBINE_PARTS = 4
DMA_UNROLL = 8
VMEM_LIMIT = 56 * 1024 * 1024


def _cparams(sem):
    return pltpu.CompilerParams(dimension_semantics=sem, vmem_limit_bytes=VMEM_LIMIT)


def _dot(a, b):
    return jnp.dot(a, b, preferred_element_type=F32)


def _dot_nt(a, b):
    return lax.dot_general(a, b, (((1,), (1,)), ((), ())), preferred_element_type=F32)


def _dot_tn(a, b):
    return lax.dot_general(a, b, (((0,), (0,)), ((), ())), preferred_element_type=F32)


def _split_bf16(x):
    hi = x.astype(BF16)
    lo = (x - hi.astype(F32)).astype(BF16)
    return hi, lo


def _log_sigmoid(x):
    return jnp.minimum(x, 0.0) - jnp.log(1.0 + jnp.exp(-jnp.abs(x)))


def _sigmoid(x):
    return 1.0 / (1.0 + jnp.exp(-x))


def _rms_norm(x, gain):
    return x * lax.rsqrt(jnp.mean(x * x, axis=-1, keepdims=True) + RMS_EPS) * gain


def _inproj_kernel(x_ref, gain_ref, wa_ref, wqb_ref, wkvt_ref, wg_ref, wgu_ref, bgu_ref,
                   qa_ref, ka_ref, va_ref, ra_ref, la_ref, qb_ref, kt_ref, vt_ref,
                   kt16_ref, vt16_ref, gbr_ref):
    h = _rms_norm(x_ref[...], gain_ref[...]).astype(BF16)
    pa = H_A * DK_A
    mw = va_ref.shape[-1]
    kvt = _dot_nt(wkvt_ref[...], h)
    nseq, _, s = kt_ref.shape
    ntile, tile = kt16_ref.shape[1], kt16_ref.shape[3]
    for i in range(nseq):
        kt_ref[i] = kvt[0:mw, i * s:(i + 1) * s]
        vt_ref[i] = kvt[mw:2 * mw, i * s:(i + 1) * s]
        for j in range(ntile):
            cols = slice(i * s + j * tile, i * s + (j + 1) * tile)
            kt16_ref[i, j] = kvt[0:mw, cols].astype(BF16)
            vt16_ref[i, j] = kvt[mw:2 * mw, cols].astype(BF16)
    qb_ref[...] = _dot(h, wqb_ref[...]).astype(BF16)
    qa_ref[...] = _dot(h, wa_ref[:, 0:pa])
    ka_ref[...] = _dot(h, wa_ref[:, pa:2 * pa])
    va_ref[...] = _dot(h, wa_ref[:, 2 * pa:2 * pa + mw])
    ra_ref[...] = _dot(h, wa_ref[:, 2 * pa + mw:2 * pa + 2 * mw])
    lr = _dot(h, wa_ref[:, 2 * pa + 2 * mw:2 * pa + 2 * mw + LANES])
    gl = _dot(lr.astype(BF16), wgu_ref[...]) + bgu_ref[...]
    la_ref[...] = _log_sigmoid(gl) / GATE_TAU
    gbr_ref[...] = _dot(h, wg_ref[...]).astype(gbr_ref.dtype)


def _in_projection(x, seq_len, gain, wa, wqb, wkvt, wg, wgu, bgu):
    t, d = x.shape
    nb = t // seq_len
    pa = H_A * DK_A
    mw = wqb.shape[1]
    tm = INPROJ_TILE
    assert t % tm == 0
    row = lambda w: pl.BlockSpec((tm, w), lambda i: (i, 0))
    full = lambda a: pl.BlockSpec(a.shape, lambda i: (0,) * a.ndim, pipeline_mode=pl.Buffered(1))
    if seq_len >= tm:
        per_seq = seq_len // tm
        ntile = tm // SB_TILE
        assert tm % SB_TILE == 0 and seq_len % tm == 0
        kt_spec = pl.BlockSpec((1, mw, tm), lambda i: (i // per_seq, 0, i % per_seq))
        kt16_spec = pl.BlockSpec((1, ntile, mw, SB_TILE), lambda i: (i // per_seq, i % per_seq, 0, 0))
        kt16_shape = (nb, seq_len // SB_TILE, mw, SB_TILE)
    else:
        nseq = tm // seq_len
        assert tm % seq_len == 0
        kt_spec = pl.BlockSpec((nseq, mw, seq_len), lambda i: (i, 0, 0))
        kt16_spec = pl.BlockSpec((nseq, 1, mw, seq_len), lambda i: (i, 0, 0, 0))
        kt16_shape = (nb, 1, mw, seq_len)
    outs = [
        (jax.ShapeDtypeStruct((t, pa), F32), row(pa)), (jax.ShapeDtypeStruct((t, pa), F32), row(pa)),
        (jax.ShapeDtypeStruct((t, mw), F32), row(mw)), (jax.ShapeDtypeStruct((t, mw), F32), row(mw)),
        (jax.ShapeDtypeStruct((t, pa), F32), row(pa)),
        (jax.ShapeDtypeStruct((t, mw), BF16), row(mw)),
        (jax.ShapeDtypeStruct((nb, mw, seq_len), F32), kt_spec), (jax.ShapeDtypeStruct((nb, mw, seq_len), F32), kt_spec),
        (jax.ShapeDtypeStruct(kt16_shape, BF16), kt16_spec), (jax.ShapeDtypeStruct(kt16_shape, BF16), kt16_spec),
        (jax.ShapeDtypeStruct((t, wg.shape[1]), BF16), row(wg.shape[1])),
    ]
    return pl.pallas_call(
        _inproj_kernel,
        grid=(t // tm,),
        in_specs=[row(d), full(gain), full(wa), full(wqb), full(wkvt), full(wg), full(wgu), full(bgu)],
        out_specs=[spec for _, spec in outs],
        out_shape=[shape for shape, _ in outs],
        compiler_params=_cparams(("arbitrary",)),
        name="in_projection",
    )(x, gain, wa, wqb, wkvt, wg, wgu, bgu)


def _gla_chunk(q, k, v, b, st):
    c = q.shape[0]
    b_last = b[c - 1:c, :]
    rows = lax.broadcasted_iota(I32, (c, LANES), 0)
    nsub = c // GLA_SUB
    refs = [jnp.zeros((1, LANES), F32)] + [b[i * GLA_SUB - 1:i * GLA_SUB, :] for i in range(1, nsub)]
    ref_rows = refs[0]
    for i in range(1, nsub):
        ref_rows = jnp.where(rows >= i * GLA_SUB, refs[i], ref_rows)
    q_rel = q * jnp.exp(b - ref_rows)
    lhs = jnp.concatenate(
        [jnp.where((rows >= i * GLA_SUB) & (rows < (i + 1) * GLA_SUB), q_rel, 0.0) for i in range(nsub)],
        axis=1).astype(BF16)
    rhs = jnp.concatenate(
        [jnp.where(rows < (i + 1) * GLA_SUB, k * jnp.exp(jnp.minimum(refs[i] - b, GLA_EXP_CLAMP)), 0.0)
         for i in range(nsub)], axis=1).astype(BF16)
    att = _dot_nt(lhs, rhs)
    tt = lax.broadcasted_iota(I32, (c, c), 0)
    ss = lax.broadcasted_iota(I32, (c, c), 1)
    att = jnp.where(ss <= tt, att, 0.0)
    v16 = v.astype(BF16)
    inter = _dot_nt((q * jnp.exp(b)).astype(BF16), st.astype(BF16))
    intra = _dot(att.astype(BF16), v16)
    kd = (k * jnp.exp(b_last - b)).astype(BF16)
    st_new = st * jnp.exp(b_last) + _dot_tn(v16, kd)
    return inter + intra, st_new


def _gla_kernel(qa_ref, ka_ref, va_ref, ra_ref, la_ref, s0_ref, gain_ref, o_ref, sfin_ref, st_ref):
    j = pl.program_id(1)
    nj = pl.num_programs(1)
    nseq, rows_per_step, _ = qa_ref.shape
    c = GLA_CHUNK
    zpad = jnp.zeros((LANES - DK_A, DV_A), F32)

    def state_rows(h):
        return slice((h % 2) * DK_A, (h % 2 + 1) * DK_A)

    @pl.when(j == 0)
    def _():
        for si in range(nseq):
            for h in range(H_A):
                parts = [s0_ref[si, h], zpad] if h % 2 == 0 else [zpad, s0_ref[si, h]]
                st_ref[si * H_A + h] = jnp.concatenate(parts, axis=0).T

    ti = lax.broadcasted_iota(I32, (rows_per_step, rows_per_step), 0)
    si = lax.broadcasted_iota(I32, (rows_per_step, rows_per_step), 1)
    chunk_shift = c.bit_length() - 1
    same_chunk = (ti >> chunk_shift) == (si >> chunk_shift)
    tril_blocks = jnp.where(same_chunk & (si <= ti), 1.0, 0.0).astype(BF16)
    gain = gain_ref[...]
    lane = lax.broadcasted_iota(I32, (1, LANES), 1)
    half_masks = (lane < DK_A, lane >= DK_A)
    for si in range(nseq):
        la_hi, la_lo = _split_bf16(la_ref[si])
        b_all = _dot(tril_blocks, la_hi) + _dot(tril_blocks, la_lo)
        for h in range(H_A):
            hp = slice((h // 2) * LANES, (h // 2 + 1) * LANES)
            hv = slice(h * DV_A, (h + 1) * DV_A)
            mine = half_masks[h % 2]
            st = st_ref[si * H_A + h]
            for ci in range(rows_per_step // c):
                r0 = ci * c
                q = jnp.where(mine, qa_ref[si, r0:r0 + c, hp], 0.0) * (DK_A ** -0.5)
                k = jnp.where(mine, ka_ref[si, r0:r0 + c, hp], 0.0)
                o, st = _gla_chunk(q, k, va_ref[si, r0:r0 + c, hv], b_all[r0:r0 + c, hp], st)
                r = ra_ref[si, r0:r0 + c, hv]
                o = _rms_norm(o, gain) * (r * _sigmoid(r))
                o_ref[si, r0:r0 + c, hv] = o.astype(o_ref.dtype)
            st_ref[si * H_A + h] = st

    @pl.when(j == nj - 1)
    def _():
        for si in range(nseq):
            for h in range(H_A):
                sfin_ref[si, h] = st_ref[si * H_A + h].T[state_rows(h), :]


def _gla(qa, ka, va, ra, la, s0, gain, rows_per_step):
    assert 2 * DK_A == LANES and H_A % 2 == 0
    b, s, pa = qa.shape
    mw = va.shape[-1]
    ns = GLA_SEQS
    assert b % ns == 0
    seq = lambda w: pl.BlockSpec((ns, rows_per_step, w), lambda i, j: (i, j, 0))
    state = pl.BlockSpec((ns, H_A, DK_A, DV_A), lambda i, j: (i, 0, 0, 0))
    return pl.pallas_call(
        _gla_kernel,
        grid=(b // ns, s // rows_per_step),
        in_specs=[seq(pa), seq(pa), seq(mw), seq(mw), seq(pa), state,
                  pl.BlockSpec(gain.shape, lambda i, j: (0, 0))],
        out_specs=[seq(mw), state],
        out_shape=[jax.ShapeDtypeStruct((b, s, mw), BF16),
                   jax.ShapeDtypeStruct((b, H_A, DK_A, DV_A), F32)],
        scratch_shapes=[pltpu.VMEM((ns * H_A, LANES, LANES), F32)],
        compiler_params=_cparams(("arbitrary", "arbitrary")),
        name="gla",
    )(qa, ka, va, ra, la, s0, gain)


def _head_lane_masks():
    lane = lax.broadcasted_iota(I32, (1, LANES), 1)
    return lane < DH_B, lane >= DH_B


def _sb_neg_tri(tk):
    ji = lax.broadcasted_iota(I32, (tk, tk), 0)
    si = lax.broadcasted_iota(I32, (tk, tk), 1)
    return jnp.where(ji >= si, -1.0, 0.0).astype(BF16)


def _sb_stack_queries(q, qs_ref, base=0):
    m0, m1 = _head_lane_masks()
    for p in range(q.shape[1] // LANES):
        qp = (q[:, p * LANES:(p + 1) * LANES].astype(F32) * (DH_B ** -0.5 * LOG2_E)).astype(BF16)
        zero = jnp.zeros_like(qp)
        qs_ref[base + p] = jnp.concatenate([jnp.where(m0, qp, zero), jnp.where(m1, qp, zero)], axis=0)


def _pair_lanes(p):
    return slice(p * LANES, (p + 1) * LANES)


def _lane_fit(x, width):
    if width >= LANES:
        return jnp.concatenate([x] * (width // LANES), axis=1)
    return x[:, 0:width]


def _sb_tile_step(qs_ref, acc_ref, carry_ref, k_tile, v_tile, ntri, diagonal, one_suffix_matmul=False):
    npair, rows, _ = qs_ref.shape
    tq = rows // 2
    tk = ntri.shape[1]
    m0, _ = _head_lane_masks()
    if diagonal:
        t = lax.broadcasted_iota(I32, (rows, tk), 0)
        t = jnp.where(t >= tq, t - tq, t)
        visible = lax.broadcasted_iota(I32, (rows, tk), 1) < t
    def scores(p):
        z = _dot(qs_ref[p], k_tile(p))
        sp = jnp.maximum(z, 0.0) + jnp.log2(1.0 + jnp.exp2(-jnp.abs(z)))
        if diagonal:
            sp = jnp.where(visible, sp, 0.0)
        return z, sp.astype(BF16)

    if one_suffix_matmul:
        zs, sps = zip(*[scores(p) for p in range(npair)])
        stacked = _dot(jnp.concatenate(sps, axis=0), ntri)
        suffixes = [stacked[p * rows:(p + 1) * rows] for p in range(npair)]
    for p in range(npair):
        if one_suffix_matmul:
            z, suffix = zs[p], suffixes[p]
        else:
            z, sp = scores(p)
            suffix = _dot(sp, ntri)
        carry = carry_ref[p]
        w = jnp.exp2(z + suffix + _lane_fit(carry, tk))
        if diagonal:
            w = jnp.where(visible, w, 0.0)
        pv = _dot_nt(w.astype(BF16), v_tile(p))
        acc_ref[p] += jnp.where(m0, pv[0:tq], pv[tq:rows])
        carry_ref[p] = carry + jnp.broadcast_to(suffix[:, 0:1], carry.shape)


def _sb_prompt_kernel(q_ref, k_ref, v_ref, o_ref, qs_ref, acc_ref, carry_ref):
    qi = pl.program_id(1)
    tk = SB_TILE
    _sb_stack_queries(q_ref[0], qs_ref)
    acc_ref[...] = jnp.zeros_like(acc_ref)
    carry_ref[...] = jnp.zeros_like(carry_ref)
    ntri = _sb_neg_tri(tk)

    def step(jb, diagonal):
        _sb_tile_step(qs_ref, acc_ref, carry_ref, lambda p: k_ref[0, jb, _pair_lanes(p), :],
                      lambda p: v_ref[0, jb, _pair_lanes(p), :], ntri, diagonal, True)

    step(qi, True)

    def body(i, c):
        step(qi - 1 - 2 * i, False)
        step(qi - 2 - 2 * i, False)
        return c

    lax.fori_loop(0, qi // 2, body, 0)

    @pl.when(qi % 2 == 1)
    def _():
        step(0, False)

    for p in range(acc_ref.shape[0]):
        o_ref[0, :, p * LANES:(p + 1) * LANES] = acc_ref[p].astype(o_ref.dtype)


def _sb_scratch(tq, npair):
    return [pltpu.VMEM((npair, 2 * tq, LANES), BF16), pltpu.VMEM((npair, tq, LANES), F32),
            pltpu.VMEM((npair, 2 * tq, LANES), F32)]


def _sb_prompt(q, kt, vt):
    b, s, w = q.shape
    tq = SB_TILE
    assert kt.shape == (b, s // tq, w, tq)
    qspec = pl.BlockSpec((1, tq, w), lambda i, j: (i, j, 0))
    kvspec = pl.BlockSpec((1,) + kt.shape[1:], lambda i, j: (i, 0, 0, 0))
    return pl.pallas_call(
        _sb_prompt_kernel,
        grid=(b, s // tq),
        in_specs=[qspec, kvspec, kvspec],
        out_specs=qspec,
        out_shape=jax.ShapeDtypeStruct((b, s, w), BF16),
        scratch_shapes=_sb_scratch(tq, w // LANES),
        compiler_params=_cparams(("arbitrary", "arbitrary")),
        name="sb_prompt",
    )(q, kt, vt)


def _sb_sample_kernel(q_ref, kn_ref, vn_ref, kp_ref, vp_ref, o_ref, qs_ref, acc_ref, carry_ref):
    nseq, sq, w = q_ref.shape
    past = kp_ref.shape[3]
    npair = w // LANES
    tk = SB_TILE
    for si in range(nseq):
        _sb_stack_queries(q_ref[si], qs_ref, si * npair)
    acc_ref[...] = jnp.zeros_like(acc_ref)
    carry_ref[...] = jnp.zeros_like(carry_ref)
    _sb_tile_step(qs_ref, acc_ref, carry_ref, lambda e: kn_ref[e // npair, 0, _pair_lanes(e % npair), :],
                  lambda e: vn_ref[e // npair, 0, _pair_lanes(e % npair), :], _sb_neg_tri(sq), True, True)
    ntri = _sb_neg_tri(tk)

    def body(i, c):
        cols = pl.ds(pl.multiple_of(past - (i + 1) * tk, tk), tk)

        def pair(ref, e):
            p = e % npair
            return ref[e // npair, 2 * p:2 * p + 2, :, cols].reshape(LANES, tk).astype(BF16)

        _sb_tile_step(qs_ref, acc_ref, carry_ref, lambda e: pair(kp_ref, e), lambda e: pair(vp_ref, e), ntri, False,
                      True)
        return c

    lax.fori_loop(0, past // tk, body, 0)
    for e in range(acc_ref.shape[0]):
        o_ref[e // npair, :, _pair_lanes(e % npair)] = acc_ref[e].astype(o_ref.dtype)


def _sb_sample(q, kt_new, vt_new, kt_past, vt_past):
    b, sq, w = q.shape
    past = kt_past.shape[3]
    ns = SB_SAMPLE_SEQS
    assert past % SB_TILE == 0 and 2 * DH_B == LANES and b % ns == 0
    qspec = pl.BlockSpec((ns, sq, w), lambda i: (i, 0, 0))
    new = pl.BlockSpec((ns, 1, w, sq), lambda i: (i, 0, 0, 0))
    old = pl.BlockSpec((ns, H_B, DH_B, past), lambda i: (i, 0, 0, 0))
    return pl.pallas_call(
        _sb_sample_kernel,
        grid=(b // ns,),
        in_specs=[qspec, new, new, old, old],
        out_specs=qspec,
        out_shape=jax.ShapeDtypeStruct((b, sq, w), BF16),
        scratch_shapes=_sb_scratch(sq, ns * (w // LANES)),
        compiler_params=_cparams(("arbitrary",)),
        name="sb_sample",
    )(q, kt_new, vt_new, kt_past, vt_past)


def _first_argmax(vals, nrows):
    idx = lax.broadcasted_iota(I32, vals.shape, 0)
    top = jnp.max(vals, axis=0, keepdims=True)
    first = jnp.min(jnp.where(vals == top, idx, nrows), axis=0, keepdims=True)
    return top, first, idx


def _merge_kernel(oa_ref, ob_ref, g_ref, x_ref, wb0_ref, wb1_ref, wo_ref, gain_ref, wr_ref, br_ref,
                  x1_ref, h2_ref, eid_ref, wcol_ref):
    d = x_ref.shape[1]
    ya = _dot(oa_ref[...], wb0_ref[...])
    yb = _dot(ob_ref[...], wb1_ref[...])
    g = g_ref[...].astype(F32)
    m = _sigmoid(g[:, 0:d]) * ya + _sigmoid(g[:, d:2 * d]) * yb
    x1 = x_ref[...] + _dot(m.astype(BF16), wo_ref[...])
    x1_ref[...] = x1
    h2 = _rms_norm(x1, gain_ref[...])
    h2_ref[...] = h2.astype(h2_ref.dtype).reshape(h2_ref.shape)

    h_hi, h_lo = _split_bf16(h2)
    w_hi, w_lo = _split_bf16(wr_ref[...])
    lt = _dot_nt(w_hi, h_hi) + _dot_nt(w_hi, h_lo) + _dot_nt(w_lo, h_hi) + br_ref[:, 0:1]
    gl = lt[0:N_GROUPS, :]
    g_top, g_idx, _ = _first_argmax(gl, N_GROUPS)
    g_e = jnp.exp(gl - g_top)
    g_p = jnp.max(g_e / jnp.sum(g_e, axis=0, keepdims=True), axis=0, keepdims=True)
    el = jnp.zeros((EXPERTS_PER_GROUP, lt.shape[1]), F32)
    for g in range(N_GROUPS):
        r0 = 8 + g * EXPERTS_PER_GROUP
        el = jnp.where(g_idx == g, lt[r0:r0 + EXPERTS_PER_GROUP, :], el)
    e_top, i1, eidx = _first_argmax(el, EXPERTS_PER_GROUP)
    e_e = jnp.exp(el - e_top)
    e_p = e_e / jnp.sum(e_e, axis=0, keepdims=True)
    p1 = jnp.max(e_p, axis=0, keepdims=True)
    rest = jnp.where(eidx == i1, -1.0, e_p)
    p2, i2, _ = _first_argmax(rest, EXPERTS_PER_GROUP)
    norm = p1 + p2
    w1 = g_p * (p1 / norm)
    w2 = g_p * (p2 / norm)
    eid_ref[...] = jnp.concatenate([g_idx * EXPERTS_PER_GROUP + i1, g_idx * EXPERTS_PER_GROUP + i2], axis=0)
    rows = lax.broadcasted_iota(I32, (LANES, lt.shape[1]), 0)
    wrows = jnp.where(rows == 0, w1, jnp.where(rows == 1, w2, 0.0))
    wcol_ref[...] = wrows.T


def _merge(oa, ob, gbr, x, wb0, wb1, wo, gain, wr, br):
    t, d = x.shape
    tm = MERGE_TILE
    assert t % tm == 0
    row = lambda w: pl.BlockSpec((tm, w), lambda i: (i, 0))
    full = lambda a: pl.BlockSpec(a.shape, lambda i: (0,) * a.ndim)
    return pl.pallas_call(
        _merge_kernel,
        grid=(t // tm,),
        in_specs=[row(oa.shape[1]), row(ob.shape[1]), row(gbr.shape[1]), row(d),
                  full(wb0), full(wb1), full(wo), full(gain), full(wr), full(br)],
        out_specs=[row(d), pl.BlockSpec((tm, d // LANES, LANES), lambda i: (i, 0, 0)),
                   pl.BlockSpec((TOP_K, tm), lambda i: (0, i)), row(LANES)],
        out_shape=[jax.ShapeDtypeStruct((t, d), F32), jax.ShapeDtypeStruct((t, d // LANES, LANES), BF16),
                   jax.ShapeDtypeStruct((TOP_K, t), I32), jax.ShapeDtypeStruct((t, LANES), F32)],
        compiler_params=_cparams(("arbitrary",)),
        name="merge_router",
    )(oa, ob, gbr, x, wb0, wb1, wo, gain, wr, br)


def _positions_kernel(eid_ref, dest_ref, counts_ref, rank_ref):
    nblk, width = eid_ref.shape
    ji = lax.broadcasted_iota(I32, (width, width), 0)
    si = lax.broadcasted_iota(I32, (width, width), 1)
    prefix = jnp.where(ji <= si, 1.0, 0.0).astype(BF16)
    expert = lax.broadcasted_iota(I32, (N_EXPERTS, width), 0)
    group = max(g for g in (8, 4, 2, 1) if nblk % g == 0)

    def onehot(i):
        return expert == eid_ref[pl.ds(i, 1), :]

    def rank_body(ig, run):
        first = pl.multiple_of(ig * group, group)
        ohs = [onehot(first + j) for j in range(group)]
        stacked = jnp.concatenate([jnp.where(oh, 1.0, 0.0) for oh in ohs], axis=0).astype(BF16)
        cum = _dot(stacked, prefix)
        ranks = []
        for j, oh in enumerate(ohs):
            cum_j = cum[j * N_EXPERTS:(j + 1) * N_EXPERTS, :] + run
            ranks.append(jnp.sum(jnp.where(oh, cum_j, 0.0), axis=0, keepdims=True) - 1.0)
            run = cum_j[:, width - 1:width]
        rank_ref[pl.ds(first, group), :] = jnp.concatenate(ranks, axis=0)
        return run

    counts = lax.fori_loop(0, nblk // group, rank_body, jnp.zeros((N_EXPERTS, 1), F32))
    counts_ref[...] = jnp.broadcast_to(counts, counts_ref.shape).astype(I32)
    c_hi = jnp.floor(counts * (1.0 / 256.0))
    c_lo = counts - 256.0 * c_hi
    ei = lax.broadcasted_iota(I32, (N_EXPERTS, N_EXPERTS), 0)
    ej = lax.broadcasted_iota(I32, (N_EXPERTS, N_EXPERTS), 1)
    strict = jnp.where(ej < ei, 1.0, 0.0).astype(BF16)
    digits = jnp.concatenate([jnp.broadcast_to(c_hi, (N_EXPERTS, LANES)),
                              jnp.broadcast_to(c_lo, (N_EXPERTS, LANES))], axis=1).astype(BF16)
    sums = _dot(strict, digits)
    start = 256.0 * sums[:, 0:1] + sums[:, LANES:LANES + 1]

    def dest_body(ig, carry):
        first = pl.multiple_of(ig * group, group)
        offs = [jnp.sum(jnp.where(onehot(first + j), start, 0.0), axis=0, keepdims=True) for j in range(group)]
        rows = pl.ds(first, group)
        dest_ref[rows, :] = (rank_ref[rows, :] + jnp.concatenate(offs, axis=0)).astype(I32)
        return carry

    lax.fori_loop(0, nblk // group, dest_body, 0)


def _positions(eid_blocks):
    nblk, width = eid_blocks.shape
    vm = lambda shape: pl.BlockSpec(shape, lambda: (0,) * len(shape))
    return pl.pallas_call(
        _positions_kernel,
        in_specs=[vm((nblk, width))],
        out_specs=[vm((nblk, width)), vm((N_EXPERTS, LANES))],
        out_shape=[jax.ShapeDtypeStruct((nblk, width), I32), jax.ShapeDtypeStruct((N_EXPERTS, LANES), I32)],
        scratch_shapes=[pltpu.VMEM((nblk, width), F32)],
        name="positions",
    )(eid_blocks)


def _dispatch_kernel(n_prompt_tiles, dest_ref, hp_ref, hs_ref, xs_ref, sem):
    i = pl.program_id(0)
    tm = dest_ref.shape[1]

    def scatter(src_ref):
        def start(r, c):
            for k in range(TOP_K):
                pltpu.make_async_copy(src_ref.at[r], xs_ref.at[dest_ref[k, r]], sem).start(priority=k)
            return c

        lax.fori_loop(0, tm, start, 0, unroll=DMA_UNROLL)
        for k in range(TOP_K):
            pltpu.make_async_copy(src_ref, xs_ref.at[pl.ds(0, tm)], sem).wait()

    @pl.when(i < n_prompt_tiles)
    def _():
        scatter(hp_ref)

    @pl.when(i >= n_prompt_tiles)
    def _():
        scatter(hs_ref)


def _dispatch(dest, h_prompt, h_sample):
    t = dest.shape[1]
    slab = h_prompt.shape[1:]
    tm = DISPATCH_TILE
    assert h_prompt.shape[0] % tm == 0 and h_sample.shape[0] % tm == 0
    npt = h_prompt.shape[0] // tm
    return pl.pallas_call(
        functools.partial(_dispatch_kernel, npt),
        grid=(t // tm,),
        in_specs=[pl.BlockSpec((TOP_K, tm), lambda i: (0, i), memory_space=pltpu.SMEM),
                  pl.BlockSpec((tm,) + slab, lambda i: (jnp.minimum(i, npt - 1), 0, 0)),
                  pl.BlockSpec((tm,) + slab, lambda i: (jnp.maximum(i - npt, 0), 0, 0))],
        out_specs=pl.BlockSpec(memory_space=pl.ANY),
        out_shape=jax.ShapeDtypeStruct((TOP_K * t,) + slab, h_prompt.dtype),
        scratch_shapes=[pltpu.SemaphoreType.DMA(())],
        compiler_params=_cparams(("arbitrary",)),
        name="dispatch",
    )(dest, h_prompt, h_sample)


def _experts_kernel(vblk_ref, vexp_ref, vlo_ref, vhi_ref, vnext_ref, vslot_ref, xs_ref, wg_ref, wu_ref, wd_ref,
                    ys_ref, wg32_ref, wu32_ref, wd32_ref, wg16_ref, wu16_ref, wd16_ref, sems):
    v = pl.program_id(0)
    lo = vlo_ref[v]
    hi = vhi_ref[v]
    prev = jnp.maximum(v - 1, 0)
    first = jnp.logical_or(v == 0, vblk_ref[v] != vblk_ref[prev])
    new_expert = jnp.logical_or(v == 0, vexp_ref[v] != vexp_ref[prev])

    def weight_copies(e, slot):
        return [pltpu.make_async_copy(src.at[e], dst.at[slot], sems.at[slot])
                for src, dst in ((wg_ref, wg32_ref), (wu_ref, wu32_ref), (wd_ref, wd32_ref))]

    @pl.when(v == 0)
    def _():
        for cp in weight_copies(vexp_ref[0], 0):
            cp.start()

    @pl.when(new_expert)
    def _():
        slot = vslot_ref[v]
        for cp in weight_copies(vexp_ref[v], slot):
            cp.wait()
        wg16_ref[...] = wg32_ref[slot].astype(BF16)
        wu16_ref[...] = wu32_ref[slot].astype(BF16)
        wd16_ref[...] = wd32_ref[slot].astype(BF16)

        @pl.when(vnext_ref[v] >= 0)
        def _():
            for cp in weight_copies(vnext_ref[v], 1 - slot):
                cp.start()

    @pl.when(first)
    def _():
        ys_ref[...] = jnp.zeros_like(ys_ref)

    @pl.when(hi > lo)
    def _():
        tm = xs_ref.shape[0]
        d = wg_ref.shape[1]
        x = xs_ref[...].reshape(tm, d)
        gate = _dot(x, wg16_ref[...])
        up = _dot(x, wu16_ref[...])
        hid = (gate * _sigmoid(gate) * up).astype(BF16)
        y = _dot(hid, wd16_ref[...]).astype(ys_ref.dtype)
        rows = lax.broadcasted_iota(I32, y.shape, 0)
        mine = (rows >= lo) & (rows < hi)
        ys_ref[...] = jnp.where(mine, y, ys_ref[...].reshape(tm, d)).reshape(ys_ref.shape)


def _experts(plan, xs, wg, wu, wd):
    a = xs.shape[0]
    slab = xs.shape[1:]
    d, de = wg.shape[1:]
    tm = MOE_TILE
    block = lambda v, b, *_: (b[v], 0, 0)
    hbm = pl.BlockSpec(memory_space=pl.ANY)
    grid_spec = pltpu.PrefetchScalarGridSpec(
        num_scalar_prefetch=len(plan),
        grid=(plan[0].shape[0],),
        in_specs=[pl.BlockSpec((tm,) + slab, block), hbm, hbm, hbm],
        out_specs=pl.BlockSpec((tm,) + slab, block),
        scratch_shapes=[pltpu.VMEM((2, d, de), F32), pltpu.VMEM((2, d, de), F32), pltpu.VMEM((2, de, d), F32),
                        pltpu.VMEM((d, de), BF16), pltpu.VMEM((d, de), BF16), pltpu.VMEM((de, d), BF16),
                        pltpu.SemaphoreType.DMA((2,))],
    )
    return pl.pallas_call(
        _experts_kernel,
        grid_spec=grid_spec,
        out_shape=jax.ShapeDtypeStruct((a,) + slab, MOE_OUT_DTYPE),
        compiler_params=_cparams(("arbitrary",)),
        name="experts",
    )(*plan, xs, wg, wu, wd)


def _visit_plan(counts, n_rows):
    tm = MOE_TILE
    nblk = n_rows // tm
    n_visits = nblk + N_EXPERTS - 1
    ends = jnp.cumsum(counts)
    starts = ends - counts
    first_blk = starts // tm
    nvis = jnp.where(counts > 0, (ends + tm - 1) // tm - first_blk, 0)
    vis_end = jnp.cumsum(nvis)
    vis_start = vis_end - nvis
    v = jnp.arange(n_visits, dtype=I32)
    e = jnp.minimum(jnp.sum((vis_end[None, :] <= v[:, None]).astype(I32), axis=1), N_EXPERTS - 1)
    valid = v < vis_end[-1]
    blk = first_blk[e] + (v - vis_start[e])
    lo = jnp.clip(starts[e] - blk * tm, 0, tm)
    hi = jnp.clip(ends[e] - blk * tm, 0, tm)
    ids = jnp.arange(N_EXPERTS, dtype=I32)
    used = counts > 0
    last_e = jnp.max(jnp.where(used, ids, 0))
    blk = jnp.where(valid, blk, nblk - 1).astype(I32)
    e = jnp.where(valid, e, last_e).astype(I32)
    lo = jnp.where(valid, lo, 0).astype(I32)
    hi = jnp.where(valid, hi, 0).astype(I32)
    later_used = used[None, :] & (ids[None, :] > ids[:, None])
    next_used = jnp.min(jnp.where(later_used, ids[None, :], N_EXPERTS), axis=1)
    next_used = jnp.where(next_used < N_EXPERTS, next_used, -1).astype(I32)
    slot = ((jnp.cumsum(used.astype(I32)) - 1) % 2).astype(I32)
    return blk, e, lo, hi, next_used[e], slot[e]


def _combine_kernel(dest_ref, ys_ref, x1_ref, wcol_ref, gain_ref, out_ref, buf_ref, sems):
    tm, d = x1_ref.shape
    part = tm // COMBINE_PARTS

    def start(r, c, sem):
        for k in range(TOP_K):
            pltpu.make_async_copy(ys_ref.at[dest_ref[k, r]], buf_ref.at[k, r], sem).start(priority=k)
        return c

    for h in range(COMBINE_PARTS):
        lax.fori_loop(h * part, (h + 1) * part, functools.partial(start, sem=sems.at[h]), 0, unroll=DMA_UNROLL)
    for h in range(COMBINE_PARTS):
        rows = pl.ds(h * part, part)
        for k in range(TOP_K):
            pltpu.make_async_copy(ys_ref.at[rows], buf_ref.at[k, rows], sems.at[h]).wait()
        y = (wcol_ref[rows, 0:1] * buf_ref[0, rows].reshape(part, d).astype(F32)
             + wcol_ref[rows, 1:2] * buf_ref[1, rows].reshape(part, d).astype(F32))
        out_ref[rows, :] = _rms_norm(x1_ref[rows, :] + y, gain_ref[...])


def _combine(dest, first_token, ys, x1, wcol, gain):
    t, d = x1.shape
    tm = COMBINE_TILE
    assert t % tm == 0 and first_token % tm == 0
    off = first_token // tm
    row = lambda w: pl.BlockSpec((tm, w), lambda i: (i, 0))
    return pl.pallas_call(
        _combine_kernel,
        grid=(t // tm,),
        in_specs=[pl.BlockSpec((TOP_K, tm), lambda i: (0, i + off), memory_space=pltpu.SMEM),
                  pl.BlockSpec(memory_space=pl.ANY), row(d), row(LANES),
                  pl.BlockSpec(gain.shape, lambda i: (0, 0))],
        out_specs=row(d),
        out_shape=jax.ShapeDtypeStruct((t, d), F32),
        scratch_shapes=[pltpu.VMEM((TOP_K, tm) + ys.shape[1:], ys.dtype), pltpu.SemaphoreType.DMA((COMBINE_PARTS,))],
        compiler_params=_cparams(("arbitrary",)),
        name="combine",
    )(dest, ys, x1, wcol, gain)


def _prepare_weights(w_in, w_gla_gate_up, b_gla_gate, w_branch, w_out, w_router_group, b_router_group,
                     w_router_expert, b_router_expert):
    d = w_in.shape[0]
    qk = H_A * DK_A
    mw = H_A * DV_A
    c = 0
    w_qa, c = w_in[:, c:c + qk], c + qk
    w_ka, c = w_in[:, c:c + qk], c + qk
    w_va, c = w_in[:, c:c + mw], c + mw
    w_ra, c = w_in[:, c:c + mw], c + mw
    w_lr, c = w_in[:, c:c + GATE_RANK], c + GATE_RANK
    w_b, c = w_in[:, c:c + 3 * mw], c + 3 * mw
    w_g = w_in[:, c:]
    wa = jnp.concatenate([w_qa, w_ka, w_va, w_ra,
                          jnp.pad(w_lr, ((0, 0), (0, LANES - GATE_RANK)))], axis=1).astype(BF16)
    wgu = jnp.pad(w_gla_gate_up, ((0, LANES - GATE_RANK), (0, 0))).astype(BF16)
    bgu = b_gla_gate[None, :]
    wr = jnp.zeros((LANES, d), F32)
    wr = wr.at[0:N_GROUPS].set(w_router_group.T).at[8:8 + N_EXPERTS].set(w_router_expert.T)
    br = jnp.zeros((LANES,), F32).at[0:N_GROUPS].set(b_router_group).at[8:8 + N_EXPERTS].set(b_router_expert)
    br = jnp.broadcast_to(br[:, None], (LANES, LANES))
    return dict(wa=wa, wqb=w_b[:, 0:mw].astype(BF16), wkvt=w_b[:, mw:3 * mw].T.astype(BF16),
                wg=w_g.astype(BF16), wgu=wgu, bgu=bgu,
                wb0=w_branch[0].astype(BF16), wb1=w_branch[1].astype(BF16), wo=w_out.astype(BF16),
                wr=wr, br=br)


def _mixers(x, s0, k_past, v_past, w, norm_mix_gain, gla_norm_gain, norm_ffn_gain):
    b, s, d = x.shape
    xf = x.reshape(b * s, d)
    qa, ka, va, ra, la, qb, kt, vt, kt16, vt16, gbr = _in_projection(
        xf, s, norm_mix_gain[None, :], w["wa"], w["wqb"], w["wkvt"], w["wg"], w["wgu"], w["bgu"])
    seq = lambda a: a.reshape(b, s, a.shape[-1])
    oa, s_new = _gla(seq(qa), seq(ka), seq(va), seq(ra), seq(la), s0, gla_norm_gain[None, :],
                     min(s, GLA_ROWS))
    to_channel_major = lambda a: jnp.transpose(a, (0, 2, 3, 1))
    if k_past is None:
        ob = _sb_prompt(seq(qb), kt16, vt16)
    else:
        ob = _sb_sample(seq(qb), kt16, vt16, to_channel_major(k_past), to_channel_major(v_past))
    x1, h2, eid, wcol = _merge(oa.reshape(b * s, -1), ob.reshape(b * s, -1), gbr, xf, w["wb0"], w["wb1"],
                               w["wo"], norm_ffn_gain[None, :], w["wr"], w["br"])
    from_channel_major = lambda a: jnp.transpose(a.reshape(b, H_B, DH_B, s), (0, 3, 1, 2))
    return x1, h2, eid, wcol, s_new, from_channel_major(kt), from_channel_major(vt)


def kernel(x_prompt, x_sample, state_gla, cache_sb_k, cache_sb_v, norm_mix_gain, w_in, w_gla_gate_up, b_gla_gate, gla_norm_gain, w_branch, w_out, norm_ffn_gain, w_router_group, b_router_group, w_router_expert, b_router_expert, w_exp_gate, w_exp_up, w_exp_down, norm_final_gain):
    depth = w_in.shape[0]
    assert depth == 1, "one trunk layer per step"
    l = 0
    w = _prepare_weights(w_in[l], w_gla_gate_up[l], b_gla_gate[l], w_branch[l], w_out[l], w_router_group[l],
                         b_router_group[l], w_router_expert[l], b_router_expert[l])
    bp, sp, d = x_prompt.shape
    bs, ss, _ = x_sample.shape
    s0 = jnp.zeros((bp, H_A, DK_A, DV_A), x_prompt.dtype)
    x1p, h2p, eidp, wcolp, gla_p, k_p, v_p = _mixers(
        x_prompt, s0, None, None, w, norm_mix_gain[l], gla_norm_gain[l], norm_ffn_gain[l])
    x1s, h2s, eids, wcols, gla_s, k_s, v_s = _mixers(
        x_sample, state_gla[l], cache_sb_k[l], cache_sb_v[l], w, norm_mix_gain[l], gla_norm_gain[l],
        norm_ffn_gain[l])

    tp, ts = bp * sp, bs * ss
    eid = jnp.concatenate([eidp, eids], axis=1)
    dest_blocks, counts = _positions(eid.reshape(-1, SORT_WIDTH))
    dest = dest_blocks.reshape(TOP_K, tp + ts)
    xs = _dispatch(dest, h2p, h2s)
    plan = _visit_plan(counts[:, 0], TOP_K * (tp + ts))
    ys = _experts(plan, xs, w_exp_gate[l], w_exp_up[l], w_exp_down[l])
    gf = norm_final_gain[None, :]
    y_prompt = _combine(dest, 0, ys, x1p, wcolp, gf).reshape(bp, sp, d)
    y_sample = _combine(dest, tp, ys, x1s, wcols, gf).reshape(bs, ss, d)
    return (y_prompt, y_sample, gla_p[None], k_p[None], v_p[None], gla_s[None], k_s[None], v_s[None])
```

```python
import functools

import jax
import jax.numpy as jnp
from jax import lax
from jax.experimental import pallas as pl
from jax.experimental.pallas import tpu as pltpu

F32 = jnp.float32
BF16 = jnp.bfloat16
MOE_OUT_DTYPE = jnp.bfloat16
I32 = jnp.int32

LANES = 128
LOG2_E = 1.4426950408889634
RMS_EPS = 1e-6
GATE_TAU = 16.0
H_A = 4
DK_A = 64
DV_A = 128
GATE_RANK = 16
H_B = 8
DH_B = 64
N_GROUPS = 4
EXPERTS_PER_GROUP = 8
N_EXPERTS = N_GROUPS * EXPERTS_PER_GROUP
TOP_K = 2
GLA_CHUNK = 64
GLA_SUB = 16
GLA_EXP_CLAMP = 80.0
GLA_SEQS = 4
GLA_ROWS = 256
SB_TILE = 256
SB_SAMPLE_SEQS = 2
MOE_TILE = 512
SORT_WIDTH = 256
INPROJ_TILE = 512
MERGE_TILE = 512
DISPATCH_TILE = 1024
COMBINE_TILE = 512
COMBINE_PARTS = 4
DMA_UNROLL = 8
VMEM_LIMIT = 56 * 1024 * 1024


def _cparams(sem):
    return pltpu.CompilerParams(dimension_semantics=sem, vmem_limit_bytes=VMEM_LIMIT)


def _dot(a, b):
    return jnp.dot(a, b, preferred_element_type=F32)


def _dot_nt(a, b):
    return lax.dot_general(a, b, (((1,), (1,)), ((), ())), preferred_element_type=F32)


def _dot_tn(a, b):
    return lax.dot_general(a, b, (((0,), (0,)), ((), ())), preferred_element_type=F32)


def _split_bf16(x):
    hi = x.astype(BF16)
    lo = (x - hi.astype(F32)).astype(BF16)
    return hi, lo


def _log_sigmoid(x):
    return jnp.minimum(x, 0.0) - jnp.log(1.0 + jnp.exp(-jnp.abs(x)))


def _sigmoid(x):
    return 1.0 / (1.0 + jnp.exp(-x))


def _rms_norm(x, gain):
    return x * lax.rsqrt(jnp.mean(x * x, axis=-1, keepdims=True) + RMS_EPS) * gain


def _inproj_kernel(x_ref, gain_ref, wa_ref, wqb_ref, wkvt_ref, wg_ref, wgu_ref, bgu_ref,
                   qa_ref, ka_ref, va_ref, ra_ref, la_ref, qb_ref, kt_ref, vt_ref,
                   kt16_ref, vt16_ref, gbr_ref):
    h = _rms_norm(x_ref[...], gain_ref[...]).astype(BF16)
    pa = H_A * DK_A
    mw = va_ref.shape[-1]
    kvt = _dot_nt(wkvt_ref[...], h)
    nseq, _, s = kt_ref.shape
    ntile, tile = kt16_ref.shape[1], kt16_ref.shape[3]
    for i in range(nseq):
        kt_ref[i] = kvt[0:mw, i * s:(i + 1) * s]
        vt_ref[i] = kvt[mw:2 * mw, i * s:(i + 1) * s]
        for j in range(ntile):
            cols = slice(i * s + j * tile, i * s + (j + 1) * tile)
            kt16_ref[i, j] = kvt[0:mw, cols].astype(BF16)
            vt16_ref[i, j] = kvt[mw:2 * mw, cols].astype(BF16)
    qb_ref[...] = _dot(h, wqb_ref[...]).astype(BF16)
    qa_ref[...] = _dot(h, wa_ref[:, 0:pa])
    ka_ref[...] = _dot(h, wa_ref[:, pa:2 * pa])
    va_ref[...] = _dot(h, wa_ref[:, 2 * pa:2 * pa + mw])
    ra_ref[...] = _dot(h, wa_ref[:, 2 * pa + mw:2 * pa + 2 * mw])
    lr = _dot(h, wa_ref[:, 2 * pa + 2 * mw:2 * pa + 2 * mw + LANES])
    gl = _dot(lr.astype(BF16), wgu_ref[...]) + bgu_ref[...]
    la_ref[...] = _log_sigmoid(gl) / GATE_TAU
    gbr_ref[...] = _dot(h, wg_ref[...]).astype(gbr_ref.dtype)


def _in_projection(x, seq_len, gain, wa, wqb, wkvt, wg, wgu, bgu):
    t, d = x.shape
    nb = t // seq_len
    pa = H_A * DK_A
    mw = wqb.shape[1]
    tm = INPROJ_TILE
    assert t % tm == 0
    row = lambda w: pl.BlockSpec((tm, w), lambda i: (i, 0))
    full = lambda a: pl.BlockSpec(a.shape, lambda i: (0,) * a.ndim, pipeline_mode=pl.Buffered(1))
    if seq_len >= tm:
        per_seq = seq_len // tm
        ntile = tm // SB_TILE
        assert tm % SB_TILE == 0 and seq_len % tm == 0
        kt_spec = pl.BlockSpec((1, mw, tm), lambda i: (i // per_seq, 0, i % per_seq))
        kt16_spec = pl.BlockSpec((1, ntile, mw, SB_TILE), lambda i: (i // per_seq, i % per_seq, 0, 0))
        kt16_shape = (nb, seq_len // SB_TILE, mw, SB_TILE)
    else:
        nseq = tm // seq_len
        assert tm % seq_len == 0
        kt_spec = pl.BlockSpec((nseq, mw, seq_len), lambda i: (i, 0, 0))
        kt16_spec = pl.BlockSpec((nseq, 1, mw, seq_len), lambda i: (i, 0, 0, 0))
        kt16_shape = (nb, 1, mw, seq_len)
    outs = [
        (jax.ShapeDtypeStruct((t, pa), F32), row(pa)), (jax.ShapeDtypeStruct((t, pa), F32), row(pa)),
        (jax.ShapeDtypeStruct((t, mw), F32), row(mw)), (jax.ShapeDtypeStruct((t, mw), F32), row(mw)),
        (jax.ShapeDtypeStruct((t, pa), F32), row(pa)),
        (jax.ShapeDtypeStruct((t, mw), BF16), row(mw)),
        (jax.ShapeDtypeStruct((nb, mw, seq_len), F32), kt_spec), (jax.ShapeDtypeStruct((nb, mw, seq_len), F32), kt_spec),
        (jax.ShapeDtypeStruct(kt16_shape, BF16), kt16_spec), (jax.ShapeDtypeStruct(kt16_shape, BF16), kt16_spec),
        (jax.ShapeDtypeStruct((t, wg.shape[1]), BF16), row(wg.shape[1])),
    ]
    return pl.pallas_call(
        _inproj_kernel,
        grid=(t // tm,),
        in_specs=[row(d), full(gain), full(wa), full(wqb), full(wkvt), full(wg), full(wgu), full(bgu)],
        out_specs=[spec for _, spec in outs],
        out_shape=[shape for shape, _ in outs],
        compiler_params=_cparams(("arbitrary",)),
        name="in_projection",
    )(x, gain, wa, wqb, wkvt, wg, wgu, bgu)


def _gla_chunk(q, k, v, b, st):
    c = q.shape[0]
    b_last = b[c - 1:c, :]
    rows = lax.broadcasted_iota(I32, (c, LANES), 0)
    nsub = c // GLA_SUB
    refs = [jnp.zeros((1, LANES), F32)] + [b[i * GLA_SUB - 1:i * GLA_SUB, :] for i in range(1, nsub)]
    ref_rows = refs[0]
    for i in range(1, nsub):
        ref_rows = jnp.where(rows >= i * GLA_SUB, refs[i], ref_rows)
    q_rel = q * jnp.exp(b - ref_rows)
    lhs = jnp.concatenate(
        [jnp.where((rows >= i * GLA_SUB) & (rows < (i + 1) * GLA_SUB), q_rel, 0.0) for i in range(nsub)],
        axis=1).astype(BF16)
    rhs = jnp.concatenate(
        [jnp.where(rows < (i + 1) * GLA_SUB, k * jnp.exp(jnp.minimum(refs[i] - b, GLA_EXP_CLAMP)), 0.0)
         for i in range(nsub)], axis=1).astype(BF16)
    att = _dot_nt(lhs, rhs)
    tt = lax.broadcasted_iota(I32, (c, c), 0)
    ss = lax.broadcasted_iota(I32, (c, c), 1)
    att = jnp.where(ss <= tt, att, 0.0)
    v16 = v.astype(BF16)
    inter = _dot_nt((q * jnp.exp(b)).astype(BF16), st.astype(BF16))
    intra = _dot(att.astype(BF16), v16)
    kd = (k * jnp.exp(b_last - b)).astype(BF16)
    st_new = st * jnp.exp(b_last) + _dot_tn(v16, kd)
    return inter + intra, st_new


def _gla_kernel(qa_ref, ka_ref, va_ref, ra_ref, la_ref, s0_ref, gain_ref, o_ref, sfin_ref, st_ref):
    j = pl.program_id(1)
    nj = pl.num_programs(1)
    nseq, rows_per_step, _ = qa_ref.shape
    c = GLA_CHUNK
    zpad = jnp.zeros((LANES - DK_A, DV_A), F32)

    def state_rows(h):
        return slice((h % 2) * DK_A, (h % 2 + 1) * DK_A)

    @pl.when(j == 0)
    def _():
        for si in range(nseq):
            for h in range(H_A):
                parts = [s0_ref[si, h], zpad] if h % 2 == 0 else [zpad, s0_ref[si, h]]
                st_ref[si * H_A + h] = jnp.concatenate(parts, axis=0).T

    ti = lax.broadcasted_iota(I32, (rows_per_step, rows_per_step), 0)
    si = lax.broadcasted_iota(I32, (rows_per_step, rows_per_step), 1)
    chunk_shift = c.bit_length() - 1
    same_chunk = (ti >> chunk_shift) == (si >> chunk_shift)
    tril_blocks = jnp.where(same_chunk & (si <= ti), 1.0, 0.0).astype(BF16)
    gain = gain_ref[...]
    lane = lax.broadcasted_iota(I32, (1, LANES), 1)
    half_masks = (lane < DK_A, lane >= DK_A)
    for si in range(nseq):
        la_hi, la_lo = _split_bf16(la_ref[si])
        b_all = _dot(tril_blocks, la_hi) + _dot(tril_blocks, la_lo)
        for h in range(H_A):
            hp = slice((h // 2) * LANES, (h // 2 + 1) * LANES)
            hv = slice(h * DV_A, (h + 1) * DV_A)
            mine = half_masks[h % 2]
            st = st_ref[si * H_A + h]
            for ci in range(rows_per_step // c):
                r0 = ci * c
                q = jnp.where(mine, qa_ref[si, r0:r0 + c, hp], 0.0) * (DK_A ** -0.5)
                k = jnp.where(mine, ka_ref[si, r0:r0 + c, hp], 0.0)
                o, st = _gla_chunk(q, k, va_ref[si, r0:r0 + c, hv], b_all[r0:r0 + c, hp], st)
                r = ra_ref[si, r0:r0 + c, hv]
                o = _rms_norm(o, gain) * (r * _sigmoid(r))
                o_ref[si, r0:r0 + c, hv] = o.astype(o_ref.dtype)
            st_ref[si * H_A + h] = st

    @pl.when(j == nj - 1)
    def _():
        for si in range(nseq):
            for h in range(H_A):
                sfin_ref[si, h] = st_ref[si * H_A + h].T[state_rows(h), :]


def _gla(qa, ka, va, ra, la, s0, gain, rows_per_step):
    assert 2 * DK_A == LANES and H_A % 2 == 0
    b, s, pa = qa.shape
    mw = va.shape[-1]
    ns = GLA_SEQS
    assert b % ns == 0
    seq = lambda w: pl.BlockSpec((ns, rows_per_step, w), lambda i, j: (i, j, 0))
    state = pl.BlockSpec((ns, H_A, DK_A, DV_A), lambda i, j: (i, 0, 0, 0))
    return pl.pallas_call(
        _gla_kernel,
        grid=(b // ns, s // rows_per_step),
        in_specs=[seq(pa), seq(pa), seq(mw), seq(mw), seq(pa), state,
                  pl.BlockSpec(gain.shape, lambda i, j: (0, 0))],
        out_specs=[seq(mw), state],
        out_shape=[jax.ShapeDtypeStruct((b, s, mw), BF16),
                   jax.ShapeDtypeStruct((b, H_A, DK_A, DV_A), F32)],
        scratch_shapes=[pltpu.VMEM((ns * H_A, LANES, LANES), F32)],
        compiler_params=_cparams(("arbitrary", "arbitrary")),
        name="gla",
    )(qa, ka, va, ra, la, s0, gain)


def _head_lane_masks():
    lane = lax.broadcasted_iota(I32, (1, LANES), 1)
    return lane < DH_B, lane >= DH_B


def _sb_neg_tri(tk):
    ji = lax.broadcasted_iota(I32, (tk, tk), 0)
    si = lax.broadcasted_iota(I32, (tk, tk), 1)
    return jnp.where(ji >= si, -1.0, 0.0).astype(BF16)


def _sb_stack_queries(q, qs_ref, base=0):
    m0, m1 = _head_lane_masks()
    for p in range(q.shape[1] // LANES):
        qp = (q[:, p * LANES:(p + 1) * LANES].astype(F32) * (DH_B ** -0.5 * LOG2_E)).astype(BF16)
        zero = jnp.zeros_like(qp)
        qs_ref[base + p] = jnp.concatenate([jnp.where(m0, qp, zero), jnp.where(m1, qp, zero)], axis=0)


def _pair_lanes(p):
    return slice(p * LANES, (p + 1) * LANES)


def _lane_fit(x, width):
    if width >= LANES:
        return jnp.concatenate([x] * (width // LANES), axis=1)
    return x[:, 0:width]


def _sb_tile_step(qs_ref, acc_ref, carry_ref, k_tile, v_tile, ntri, diagonal, one_suffix_matmul=False):
    npair, rows, _ = qs_ref.shape
    tq = rows // 2
    tk = ntri.shape[1]
    m0, _ = _head_lane_masks()
    if diagonal:
        t = lax.broadcasted_iota(I32, (rows, tk), 0)
        t = jnp.where(t >= tq, t - tq, t)
        visible = lax.broadcasted_iota(I32, (rows, tk), 1) < t
    def scores(p):
        z = _dot(qs_ref[p], k_tile(p))
        sp = jnp.maximum(z, 0.0) + jnp.log2(1.0 + jnp.exp2(-jnp.abs(z)))
        if diagonal:
            sp = jnp.where(visible, sp, 0.0)
        return z, sp.astype(BF16)

    if one_suffix_matmul:
        zs, sps = zip(*[scores(p) for p in range(npair)])
        stacked = _dot(jnp.concatenate(sps, axis=0), ntri)
        suffixes = [stacked[p * rows:(p + 1) * rows] for p in range(npair)]
    for p in range(npair):
        if one_suffix_matmul:
            z, suffix = zs[p], suffixes[p]
        else:
            z, sp = scores(p)
            suffix = _dot(sp, ntri)
        carry = carry_ref[p]
        w = jnp.exp2(z + suffix + _lane_fit(carry, tk))
        if diagonal:
            w = jnp.where(visible, w, 0.0)
        pv = _dot_nt(w.astype(BF16), v_tile(p))
        acc_ref[p] += jnp.where(m0, pv[0:tq], pv[tq:rows])
        carry_ref[p] = carry + jnp.broadcast_to(suffix[:, 0:1], carry.shape)


def _sb_prompt_kernel(q_ref, k_ref, v_ref, o_ref, qs_ref, acc_ref, carry_ref):
    qi = pl.program_id(1)
    tk = SB_TILE
    _sb_stack_queries(q_ref[0], qs_ref)
    acc_ref[...] = jnp.zeros_like(acc_ref)
    carry_ref[...] = jnp.zeros_like(carry_ref)
    ntri = _sb_neg_tri(tk)

    def step(jb, diagonal):
        _sb_tile_step(qs_ref, acc_ref, carry_ref, lambda p: k_ref[0, jb, _pair_lanes(p), :],
                      lambda p: v_ref[0, jb, _pair_lanes(p), :], ntri, diagonal, True)

    step(qi, True)

    def body(i, c):
        step(qi - 1 - 2 * i, False)
        step(qi - 2 - 2 * i, False)
        return c

    lax.fori_loop(0, qi // 2, body, 0)

    @pl.when(qi % 2 == 1)
    def _():
        step(0, False)

    for p in range(acc_ref.shape[0]):
        o_ref[0, :, p * LANES:(p + 1) * LANES] = acc_ref[p].astype(o_ref.dtype)


def _sb_scratch(tq, npair):
    return [pltpu.VMEM((npair, 2 * tq, LANES), BF16), pltpu.VMEM((npair, tq, LANES), F32),
            pltpu.VMEM((npair, 2 * tq, LANES), F32)]


def _sb_prompt(q, kt, vt):
    b, s, w = q.shape
    tq = SB_TILE
    assert kt.shape == (b, s // tq, w, tq)
    qspec = pl.BlockSpec((1, tq, w), lambda i, j: (i, j, 0))
    kvspec = pl.BlockSpec((1,) + kt.shape[1:], lambda i, j: (i, 0, 0, 0))
    return pl.pallas_call(
        _sb_prompt_kernel,
        grid=(b, s // tq),
        in_specs=[qspec, kvspec, kvspec],
        out_specs=qspec,
        out_shape=jax.ShapeDtypeStruct((b, s, w), BF16),
        scratch_shapes=_sb_scratch(tq, w // LANES),
        compiler_params=_cparams(("arbitrary", "arbitrary")),
        name="sb_prompt",
    )(q, kt, vt)


def _sb_sample_kernel(q_ref, kn_ref, vn_ref, kp_ref, vp_ref, o_ref, qs_ref, acc_ref, carry_ref):
    nseq, sq, w = q_ref.shape
    past = kp_ref.shape[3]
    npair = w // LANES
    tk = SB_TILE
    for si in range(nseq):
        _sb_stack_queries(q_ref[si], qs_ref, si * npair)
    acc_ref[...] = jnp.zeros_like(acc_ref)
    carry_ref[...] = jnp.zeros_like(carry_ref)
    _sb_tile_step(qs_ref, acc_ref, carry_ref, lambda e: kn_ref[e // npair, 0, _pair_lanes(e % npair), :],
                  lambda e: vn_ref[e // npair, 0, _pair_lanes(e % npair), :], _sb_neg_tri(sq), True, True)
    ntri = _sb_neg_tri(tk)

    def body(i, c):
        cols = pl.ds(pl.multiple_of(past - (i + 1) * tk, tk), tk)

        def pair(ref, e):
            p = e % npair
            return ref[e // npair, 2 * p:2 * p + 2, :, cols].reshape(LANES, tk).astype(BF16)

        _sb_tile_step(qs_ref, acc_ref, carry_ref, lambda e: pair(kp_ref, e), lambda e: pair(vp_ref, e), ntri, False,
                      True)
        return c

    lax.fori_loop(0, past // tk, body, 0)
    for e in range(acc_ref.shape[0]):
        o_ref[e // npair, :, _pair_lanes(e % npair)] = acc_ref[e].astype(o_ref.dtype)


def _sb_sample(q, kt_new, vt_new, kt_past, vt_past):
    b, sq, w = q.shape
    past = kt_past.shape[3]
    ns = SB_SAMPLE_SEQS
    assert past % SB_TILE == 0 and 2 * DH_B == LANES and b % ns == 0
    qspec = pl.BlockSpec((ns, sq, w), lambda i: (i, 0, 0))
    new = pl.BlockSpec((ns, 1, w, sq), lambda i: (i, 0, 0, 0))
    old = pl.BlockSpec((ns, H_B, DH_B, past), lambda i: (i, 0, 0, 0))
    return pl.pallas_call(
        _sb_sample_kernel,
        grid=(b // ns,),
        in_specs=[qspec, new, new, old, old],
        out_specs=qspec,
        out_shape=jax.ShapeDtypeStruct((b, sq, w), BF16),
        scratch_shapes=_sb_scratch(sq, ns * (w // LANES)),
        compiler_params=_cparams(("arbitrary",)),
        name="sb_sample",
    )(q, kt_new, vt_new, kt_past, vt_past)


def _first_argmax(vals, nrows):
    idx = lax.broadcasted_iota(I32, vals.shape, 0)
    top = jnp.max(vals, axis=0, keepdims=True)
    first = jnp.min(jnp.where(vals == top, idx, nrows), axis=0, keepdims=True)
    return top, first, idx


def _merge_kernel(oa_ref, ob_ref, g_ref, x_ref, wb0_ref, wb1_ref, wo_ref, gain_ref, wr_ref, br_ref,
                  x1_ref, h2_ref, eid_ref, wcol_ref):
    d = x_ref.shape[1]
    ya = _dot(oa_ref[...], wb0_ref[...])
    yb = _dot(ob_ref[...], wb1_ref[...])
    g = g_ref[...].astype(F32)
    m = _sigmoid(g[:, 0:d]) * ya + _sigmoid(g[:, d:2 * d]) * yb
    x1 = x_ref[...] + _dot(m.astype(BF16), wo_ref[...])
    x1_ref[...] = x1
    h2 = _rms_norm(x1, gain_ref[...])
    h2_ref[...] = h2.astype(h2_ref.dtype).reshape(h2_ref.shape)

    h_hi, h_lo = _split_bf16(h2)
    w_hi, w_lo = _split_bf16(wr_ref[...])
    lt = _dot_nt(w_hi, h_hi) + _dot_nt(w_hi, h_lo) + _dot_nt(w_lo, h_hi) + br_ref[:, 0:1]
    gl = lt[0:N_GROUPS, :]
    g_top, g_idx, _ = _first_argmax(gl, N_GROUPS)
    g_e = jnp.exp(gl - g_top)
    g_p = jnp.max(g_e / jnp.sum(g_e, axis=0, keepdims=True), axis=0, keepdims=True)
    el = jnp.zeros((EXPERTS_PER_GROUP, lt.shape[1]), F32)
    for g in range(N_GROUPS):
        r0 = 8 + g * EXPERTS_PER_GROUP
        el = jnp.where(g_idx == g, lt[r0:r0 + EXPERTS_PER_GROUP, :], el)
    e_top, i1, eidx = _first_argmax(el, EXPERTS_PER_GROUP)
    e_e = jnp.exp(el - e_top)
    e_p = e_e / jnp.sum(e_e, axis=0, keepdims=True)
    p1 = jnp.max(e_p, axis=0, keepdims=True)
    rest = jnp.where(eidx == i1, -1.0, e_p)
    p2, i2, _ = _first_argmax(rest, EXPERTS_PER_GROUP)
    norm = p1 + p2
    w1 = g_p * (p1 / norm)
    w2 = g_p * (p2 / norm)
    eid_ref[...] = jnp.concatenate([g_idx * EXPERTS_PER_GROUP + i1, g_idx * EXPERTS_PER_GROUP + i2], axis=0)
    rows = lax.broadcasted_iota(I32, (LANES, lt.shape[1]), 0)
    wrows = jnp.where(rows == 0, w1, jnp.where(rows == 1, w2, 0.0))
    wcol_ref[...] = wrows.T


def _merge(oa, ob, gbr, x, wb0, wb1, wo, gain, wr, br):
    t, d = x.shape
    tm = MERGE_TILE
    assert t % tm == 0
    row = lambda w: pl.BlockSpec((tm, w), lambda i: (i, 0))
    full = lambda a: pl.BlockSpec(a.shape, lambda i: (0,) * a.ndim)
    return pl.pallas_call(
        _merge_kernel,
        grid=(t // tm,),
        in_specs=[row(oa.shape[1]), row(ob.shape[1]), row(gbr.shape[1]), row(d),
                  full(wb0), full(wb1), full(wo), full(gain), full(wr), full(br)],
        out_specs=[row(d), pl.BlockSpec((tm, d // LANES, LANES), lambda i: (i, 0, 0)),
                   pl.BlockSpec((TOP_K, tm), lambda i: (0, i)), row(LANES)],
        out_shape=[jax.ShapeDtypeStruct((t, d), F32), jax.ShapeDtypeStruct((t, d // LANES, LANES), BF16),
                   jax.ShapeDtypeStruct((TOP_K, t), I32), jax.ShapeDtypeStruct((t, LANES), F32)],
        compiler_params=_cparams(("arbitrary",)),
        name="merge_router",
    )(oa, ob, gbr, x, wb0, wb1, wo, gain, wr, br)


def _positions_kernel(eid_ref, dest_ref, counts_ref, rank_ref):
    nblk, width = eid_ref.shape
    ji = lax.broadcasted_iota(I32, (width, width), 0)
    si = lax.broadcasted_iota(I32, (width, width), 1)
    prefix = jnp.where(ji <= si, 1.0, 0.0).astype(BF16)
    expert = lax.broadcasted_iota(I32, (N_EXPERTS, width), 0)
    group = max(g for g in (8, 4, 2, 1) if nblk % g == 0)

    def onehot(i):
        return expert == eid_ref[pl.ds(i, 1), :]

    def rank_body(ig, run):
        first = pl.multiple_of(ig * group, group)
        ohs = [onehot(first + j) for j in range(group)]
        stacked = jnp.concatenate([jnp.where(oh, 1.0, 0.0) for oh in ohs], axis=0).astype(BF16)
        cum = _dot(stacked, prefix)
        ranks = []
        for j, oh in enumerate(ohs):
            cum_j = cum[j * N_EXPERTS:(j + 1) * N_EXPERTS, :] + run
            ranks.append(jnp.sum(jnp.where(oh, cum_j, 0.0), axis=0, keepdims=True) - 1.0)
            run = cum_j[:, width - 1:width]
        rank_ref[pl.ds(first, group), :] = jnp.concatenate(ranks, axis=0)
        return run

    counts = lax.fori_loop(0, nblk // group, rank_body, jnp.zeros((N_EXPERTS, 1), F32))
    counts_ref[...] = jnp.broadcast_to(counts, counts_ref.shape).astype(I32)
    c_hi = jnp.floor(counts * (1.0 / 256.0))
    c_lo = counts - 256.0 * c_hi
    ei = lax.broadcasted_iota(I32, (N_EXPERTS, N_EXPERTS), 0)
    ej = lax.broadcasted_iota(I32, (N_EXPERTS, N_EXPERTS), 1)
    strict = jnp.where(ej < ei, 1.0, 0.0).astype(BF16)
    digits = jnp.concatenate([jnp.broadcast_to(c_hi, (N_EXPERTS, LANES)),
                              jnp.broadcast_to(c_lo, (N_EXPERTS, LANES))], axis=1).astype(BF16)
    sums = _dot(strict, digits)
    start = 256.0 * sums[:, 0:1] + sums[:, LANES:LANES + 1]

    def dest_body(ig, carry):
        first = pl.multiple_of(ig * group, group)
        offs = [jnp.sum(jnp.where(onehot(first + j), start, 0.0), axis=0, keepdims=True) for j in range(group)]
        rows = pl.ds(first, group)
        dest_ref[rows, :] = (rank_ref[rows, :] + jnp.concatenate(offs, axis=0)).astype(I32)
        return carry

    lax.fori_loop(0, nblk // group, dest_body, 0)


def _positions(eid_blocks):
    nblk, width = eid_blocks.shape
    vm = lambda shape: pl.BlockSpec(shape, lambda: (0,) * len(shape))
    return pl.pallas_call(
        _positions_kernel,
        in_specs=[vm((nblk, width))],
        out_specs=[vm((nblk, width)), vm((N_EXPERTS, LANES))],
        out_shape=[jax.ShapeDtypeStruct((nblk, width), I32), jax.ShapeDtypeStruct((N_EXPERTS, LANES), I32)],
        scratch_shapes=[pltpu.VMEM((nblk, width), F32)],
        name="positions",
    )(eid_blocks)


def _dispatch_kernel(n_prompt_tiles, dest_ref, hp_ref, hs_ref, xs_ref, sem):
    i = pl.program_id(0)
    tm = dest_ref.shape[1]

    def scatter(src_ref):
        def start(r, c):
            for k in range(TOP_K):
                pltpu.make_async_copy(src_ref.at[r], xs_ref.at[dest_ref[k, r]], sem).start(priority=k)
            return c

        lax.fori_loop(0, tm, start, 0, unroll=DMA_UNROLL)
        for k in range(TOP_K):
            pltpu.make_async_copy(src_ref, xs_ref.at[pl.ds(0, tm)], sem).wait()

    @pl.when(i < n_prompt_tiles)
    def _():
        scatter(hp_ref)

    @pl.when(i >= n_prompt_tiles)
    def _():
        scatter(hs_ref)


def _dispatch(dest, h_prompt, h_sample):
    t = dest.shape[1]
    slab = h_prompt.shape[1:]
    tm = DISPATCH_TILE
    assert h_prompt.shape[0] % tm == 0 and h_sample.shape[0] % tm == 0
    npt = h_prompt.shape[0] // tm
    return pl.pallas_call(
        functools.partial(_dispatch_kernel, npt),
        grid=(t // tm,),
        in_specs=[pl.BlockSpec((TOP_K, tm), lambda i: (0, i), memory_space=pltpu.SMEM),
                  pl.BlockSpec((tm,) + slab, lambda i: (jnp.minimum(i, npt - 1), 0, 0)),
                  pl.BlockSpec((tm,) + slab, lambda i: (jnp.maximum(i - npt, 0), 0, 0))],
        out_specs=pl.BlockSpec(memory_space=pl.ANY),
        out_shape=jax.ShapeDtypeStruct((TOP_K * t,) + slab, h_prompt.dtype),
        scratch_shapes=[pltpu.SemaphoreType.DMA(())],
        compiler_params=_cparams(("arbitrary",)),
        name="dispatch",
    )(dest, h_prompt, h_sample)


def _experts_kernel(vblk_ref, vexp_ref, vlo_ref, vhi_ref, vnext_ref, vslot_ref, xs_ref, wg_ref, wu_ref, wd_ref,
                    ys_ref, wg32_ref, wu32_ref, wd32_ref, wg16_ref, wu16_ref, wd16_ref, sems):
    v = pl.program_id(0)
    lo = vlo_ref[v]
    hi = vhi_ref[v]
    prev = jnp.maximum(v - 1, 0)
    first = jnp.logical_or(v == 0, vblk_ref[v] != vblk_ref[prev])
    new_expert = jnp.logical_or(v == 0, vexp_ref[v] != vexp_ref[prev])

    def weight_copies(e, slot):
        return [pltpu.make_async_copy(src.at[e], dst.at[slot], sems.at[slot])
                for src, dst in ((wg_ref, wg32_ref), (wu_ref, wu32_ref), (wd_ref, wd32_ref))]

    @pl.when(v == 0)
    def _():
        for cp in weight_copies(vexp_ref[0], 0):
            cp.start()

    @pl.when(new_expert)
    def _():
        slot = vslot_ref[v]
        for cp in weight_copies(vexp_ref[v], slot):
            cp.wait()
        wg16_ref[...] = wg32_ref[slot].astype(BF16)
        wu16_ref[...] = wu32_ref[slot].astype(BF16)
        wd16_ref[...] = wd32_ref[slot].astype(BF16)

        @pl.when(vnext_ref[v] >= 0)
        def _():
            for cp in weight_copies(vnext_ref[v], 1 - slot):
                cp.start()

    @pl.when(first)
    def _():
        ys_ref[...] = jnp.zeros_like(ys_ref)

    @pl.when(hi > lo)
    def _():
        tm = xs_ref.shape[0]
        d = wg_ref.shape[1]
        x = xs_ref[...].reshape(tm, d)
        gate = _dot(x, wg16_ref[...])
        up = _dot(x, wu16_ref[...])
        hid = (gate * _sigmoid(gate) * up).astype(BF16)
        y = _dot(hid, wd16_ref[...]).astype(ys_ref.dtype)
        rows = lax.broadcasted_iota(I32, y.shape, 0)
        mine = (rows >= lo) & (rows < hi)
        ys_ref[...] = jnp.where(mine, y, ys_ref[...].reshape(tm, d)).reshape(ys_ref.shape)


def _experts(plan, xs, wg, wu, wd):
    a = xs.shape[0]
    slab = xs.shape[1:]
    d, de = wg.shape[1:]
    tm = MOE_TILE
    block = lambda v, b, *_: (b[v], 0, 0)
    hbm = pl.BlockSpec(memory_space=pl.ANY)
    grid_spec = pltpu.PrefetchScalarGridSpec(
        num_scalar_prefetch=len(plan),
        grid=(plan[0].shape[0],),
        in_specs=[pl.BlockSpec((tm,) + slab, block), hbm, hbm, hbm],
        out_specs=pl.BlockSpec((tm,) + slab, block),
        scratch_shapes=[pltpu.VMEM((2, d, de), F32), pltpu.VMEM((2, d, de), F32), pltpu.VMEM((2, de, d), F32),
                        pltpu.VMEM((d, de), BF16), pltpu.VMEM((d, de), BF16), pltpu.VMEM((de, d), BF16),
                        pltpu.SemaphoreType.DMA((2,))],
    )
    return pl.pallas_call(
        _experts_kernel,
        grid_spec=grid_spec,
        out_shape=jax.ShapeDtypeStruct((a,) + slab, MOE_OUT_DTYPE),
        compiler_params=_cparams(("arbitrary",)),
        name="experts",
    )(*plan, xs, wg, wu, wd)


def _visit_plan(counts, n_rows):
    tm = MOE_TILE
    nblk = n_rows // tm
    n_visits = nblk + N_EXPERTS - 1
    ends = jnp.cumsum(counts)
    starts = ends - counts
    first_blk = starts // tm
    nvis = jnp.where(counts > 0, (ends + tm - 1) // tm - first_blk, 0)
    vis_end = jnp.cumsum(nvis)
    vis_start = vis_end - nvis
    v = jnp.arange(n_visits, dtype=I32)
    e = jnp.minimum(jnp.sum((vis_end[None, :] <= v[:, None]).astype(I32), axis=1), N_EXPERTS - 1)
    valid = v < vis_end[-1]
    blk = first_blk[e] + (v - vis_start[e])
    lo = jnp.clip(starts[e] - blk * tm, 0, tm)
    hi = jnp.clip(ends[e] - blk * tm, 0, tm)
    ids = jnp.arange(N_EXPERTS, dtype=I32)
    used = counts > 0
    last_e = jnp.max(jnp.where(used, ids, 0))
    blk = jnp.where(valid, blk, nblk - 1).astype(I32)
    e = jnp.where(valid, e, last_e).astype(I32)
    lo = jnp.where(valid, lo, 0).astype(I32)
    hi = jnp.where(valid, hi, 0).astype(I32)
    later_used = used[None, :] & (ids[None, :] > ids[:, None])
    next_used = jnp.min(jnp.where(later_used, ids[None, :], N_EXPERTS), axis=1)
    next_used = jnp.where(next_used < N_EXPERTS, next_used, -1).astype(I32)
    slot = ((jnp.cumsum(used.astype(I32)) - 1) % 2).astype(I32)
    return blk, e, lo, hi, next_used[e], slot[e]


def _combine_kernel(dest_ref, ys_ref, x1_ref, wcol_ref, gain_ref, out_ref, buf_ref, sems):
    tm, d = x1_ref.shape
    part = tm // COMBINE_PARTS

    def start(r, c, sem):
        for k in range(TOP_K):
            pltpu.make_async_copy(ys_ref.at[dest_ref[k, r]], buf_ref.at[k, r], sem).start(priority=k)
        return c

    for h in range(COMBINE_PARTS):
        lax.fori_loop(h * part, (h + 1) * part, functools.partial(start, sem=sems.at[h]), 0, unroll=DMA_UNROLL)
    for h in range(COMBINE_PARTS):
        rows = pl.ds(h * part, part)
        for k in range(TOP_K):
            pltpu.make_async_copy(ys_ref.at[rows], buf_ref.at[k, rows], sems.at[h]).wait()
        y = (wcol_ref[rows, 0:1] * buf_ref[0, rows].reshape(part, d).astype(F32)
             + wcol_ref[rows, 1:2] * buf_ref[1, rows].reshape(part, d).astype(F32))
        out_ref[rows, :] = _rms_norm(x1_ref[rows, :] + y, gain_ref[...])


def _combine(dest, first_token, ys, x1, wcol, gain):
    t, d = x1.shape
    tm = COMBINE_TILE
    assert t % tm == 0 and first_token % tm == 0
    off = first_token // tm
    row = lambda w: pl.BlockSpec((tm, w), lambda i: (i, 0))
    return pl.pallas_call(
        _combine_kernel,
        grid=(t // tm,),
        in_specs=[pl.BlockSpec((TOP_K, tm), lambda i: (0, i + off), memory_space=pltpu.SMEM),
                  pl.BlockSpec(memory_space=pl.ANY), row(d), row(LANES),
                  pl.BlockSpec(gain.shape, lambda i: (0, 0))],
        out_specs=row(d),
        out_shape=jax.ShapeDtypeStruct((t, d), F32),
        scratch_shapes=[pltpu.VMEM((TOP_K, tm) + ys.shape[1:], ys.dtype), pltpu.SemaphoreType.DMA((COMBINE_PARTS,))],
        compiler_params=_cparams(("arbitrary",)),
        name="combine",
    )(dest, ys, x1, wcol, gain)


def _prepare_weights(w_in, w_gla_gate_up, b_gla_gate, w_branch, w_out, w_router_group, b_router_group,
                     w_router_expert, b_router_expert):
    d = w_in.shape[0]
    qk = H_A * DK_A
    mw = H_A * DV_A
    c = 0
    w_qa, c = w_in[:, c:c + qk], c + qk
    w_ka, c = w_in[:, c:c + qk], c + qk
    w_va, c = w_in[:, c:c + mw], c + mw
    w_ra, c = w_in[:, c:c + mw], c + mw
    w_lr, c = w_in[:, c:c + GATE_RANK], c + GATE_RANK
    w_b, c = w_in[:, c:c + 3 * mw], c + 3 * mw
    w_g = w_in[:, c:]
    wa = jnp.concatenate([w_qa, w_ka, w_va, w_ra,
                          jnp.pad(w_lr, ((0, 0), (0, LANES - GATE_RANK)))], axis=1).astype(BF16)
    wgu = jnp.pad(w_gla_gate_up, ((0, LANES - GATE_RANK), (0, 0))).astype(BF16)
    bgu = b_gla_gate[None, :]
    wr = jnp.zeros((LANES, d), F32)
    wr = wr.at[0:N_GROUPS].set(w_router_group.T).at[8:8 + N_EXPERTS].set(w_router_expert.T)
    br = jnp.zeros((LANES,), F32).at[0:N_GROUPS].set(b_router_group).at[8:8 + N_EXPERTS].set(b_router_expert)
    br = jnp.broadcast_to(br[:, None], (LANES, LANES))
    return dict(wa=wa, wqb=w_b[:, 0:mw].astype(BF16), wkvt=w_b[:, mw:3 * mw].T.astype(BF16),
                wg=w_g.astype(BF16), wgu=wgu, bgu=bgu,
                wb0=w_branch[0].astype(BF16), wb1=w_branch[1].astype(BF16), wo=w_out.astype(BF16),
                wr=wr, br=br)


def _mixers(x, s0, k_past, v_past, w, norm_mix_gain, gla_norm_gain, norm_ffn_gain):
    b, s, d = x.shape
    xf = x.reshape(b * s, d)
    qa, ka, va, ra, la, qb, kt, vt, kt16, vt16, gbr = _in_projection(
        xf, s, norm_mix_gain[None, :], w["wa"], w["wqb"], w["wkvt"], w["wg"], w["wgu"], w["bgu"])
    seq = lambda a: a.reshape(b, s, a.shape[-1])
    oa, s_new = _gla(seq(qa), seq(ka), seq(va), seq(ra), seq(la), s0, gla_norm_gain[None, :],
                     min(s, GLA_ROWS))
    to_channel_major = lambda a: jnp.transpose(a, (0, 2, 3, 1))
    if k_past is None:
        ob = _sb_prompt(seq(qb), kt16, vt16)
    else:
        ob = _sb_sample(seq(qb), kt16, vt16, to_channel_major(k_past), to_channel_major(v_past))
    x1, h2, eid, wcol = _merge(oa.reshape(b * s, -1), ob.reshape(b * s, -1), gbr, xf, w["wb0"], w["wb1"],
                               w["wo"], norm_ffn_gain[None, :], w["wr"], w["br"])
    from_channel_major = lambda a: jnp.transpose(a.reshape(b, H_B, DH_B, s), (0, 3, 1, 2))
    return x1, h2, eid, wcol, s_new, from_channel_major(kt), from_channel_major(vt)


def kernel(x_prompt, x_sample, state_gla, cache_sb_k, cache_sb_v, norm_mix_gain, w_in, w_gla_gate_up, b_gla_gate, gla_norm_gain, w_branch, w_out, norm_ffn_gain, w_router_group, b_router_group, w_router_expert, b_router_expert, w_exp_gate, w_exp_up, w_exp_down, norm_final_gain):
    depth = w_in.shape[0]
    assert depth == 1, "one trunk layer per step"
    l = 0
    w = _prepare_weights(w_in[l], w_gla_gate_up[l], b_gla_gate[l], w_branch[l], w_out[l], w_router_group[l],
                         b_router_group[l], w_router_expert[l], b_router_expert[l])
    bp, sp, d = x_prompt.shape
    bs, ss, _ = x_sample.shape
    s0 = jnp.zeros((bp, H_A, DK_A, DV_A), x_prompt.dtype)
    x1p, h2p, eidp, wcolp, gla_p, k_p, v_p = _mixers(
        x_prompt, s0, None, None, w, norm_mix_gain[l], gla_norm_gain[l], norm_ffn_gain[l])
    x1s, h2s, eids, wcols, gla_s, k_s, v_s = _mixers(
        x_sample, state_gla[l], cache_sb_k[l], cache_sb_v[l], w, norm_mix_gain[l], gla_norm_gain[l],
        norm_ffn_gain[l])

    tp, ts = bp * sp, bs * ss
    eid = jnp.concatenate([eidp, eids], axis=1)
    dest_blocks, counts = _positions(eid.reshape(-1, SORT_WIDTH))
    dest = dest_blocks.reshape(TOP_K, tp + ts)
    xs = _dispatch(dest, h2p, h2s)
    plan = _visit_plan(counts[:, 0], TOP_K * (tp + ts))
    ys = _experts(plan, xs, w_exp_gate[l], w_exp_up[l], w_exp_down[l])
    gf = norm_final_gain[None, :]
    y_prompt = _combine(dest, 0, ys, x1p, wcolp, gf).reshape(bp, sp, d)
    y_sample = _combine(dest, tp, ys, x1s, wcols, gf).reshape(bs, ss, d)
    return (y_prompt, y_sample, gla_p[None], k_p[None], v_p[None], gla_s[None], k_s[None], v_s[None])
```

```python
import functools

import jax
import jax.numpy as jnp
from jax import lax
from jax.experimental import pallas as pl
from jax.experimental.pallas import tpu as pltpu

F32 = jnp.float32
BF16 = jnp.bfloat16
MOE_OUT_DTYPE = jnp.bfloat16
I32 = jnp.int32

LANES = 128
LOG2_E = 1.4426950408889634
RMS_EPS = 1e-6
GATE_TAU = 16.0
H_A = 4
DK_A = 64
DV_A = 128
GATE_RANK = 16
H_B = 8
DH_B = 64
N_GROUPS = 4
EXPERTS_PER_GROUP = 8
N_EXPERTS = N_GROUPS * EXPERTS_PER_GROUP
TOP_K = 2
GLA_CHUNK = 64
GLA_SUB = 16
GLA_EXP_CLAMP = 80.0
GLA_SEQS = 4
GLA_ROWS = 256
SB_TILE = 256
SB_SAMPLE_SEQS = 2
MOE_TILE = 512
SORT_WIDTH = 256
INPROJ_TILE = 512
MERGE_TILE = 512
DISPATCH_TILE = 1024
COMBINE_TILE = 512
COMBINE_PARTS = 4
DMA_UNROLL = 8
VMEM_LIMIT = 56 * 1024 * 1024


def _cparams(sem):
    return pltpu.CompilerParams(dimension_semantics=sem, vmem_limit_bytes=VMEM_LIMIT)


def _dot(a, b):
    return jnp.dot(a, b, preferred_element_type=F32)


def _dot_nt(a, b):
    return lax.dot_general(a, b, (((1,), (1,)), ((), ())), preferred_element_type=F32)


def _dot_tn(a, b):
    return lax.dot_general(a, b, (((0,), (0,)), ((), ())), preferred_element_type=F32)


def _split_bf16(x):
    hi = x.astype(BF16)
    lo = (x - hi.astype(F32)).astype(BF16)
    return hi, lo


def _log_sigmoid(x):
    return jnp.minimum(x, 0.0) - jnp.log(1.0 + jnp.exp(-jnp.abs(x)))


def _sigmoid(x):
    return 1.0 / (1.0 + jnp.exp(-x))


def _rms_norm(x, gain):
    return x * lax.rsqrt(jnp.mean(x * x, axis=-1, keepdims=True) + RMS_EPS) * gain


def _inproj_kernel(x_ref, gain_ref, wa_ref, wqb_ref, wkvt_ref, wg_ref, wgu_ref, bgu_ref,
                   qa_ref, ka_ref, va_ref, ra_ref, la_ref, qb_ref, kt_ref, vt_ref,
                   kt16_ref, vt16_ref, gbr_ref):
    h = _rms_norm(x_ref[...], gain_ref[...]).astype(BF16)
    pa = H_A * DK_A
    mw = va_ref.shape[-1]
    kvt = _dot_nt(wkvt_ref[...], h)
    nseq, _, s = kt_ref.shape
    ntile, tile = kt16_ref.shape[1], kt16_ref.shape[3]
    for i in range(nseq):
        kt_ref[i] = kvt[0:mw, i * s:(i + 1) * s]
        vt_ref[i] = kvt[mw:2 * mw, i * s:(i + 1) * s]
        for j in range(ntile):
            cols = slice(i * s + j * tile, i * s + (j + 1) * tile)
            kt16_ref[i, j] = kvt[0:mw, cols].astype(BF16)
            vt16_ref[i, j] = kvt[mw:2 * mw, cols].astype(BF16)
    qb_ref[...] = _dot(h, wqb_ref[...]).astype(BF16)
    qa_ref[...] = _dot(h, wa_ref[:, 0:pa])
    ka_ref[...] = _dot(h, wa_ref[:, pa:2 * pa])
    va_ref[...] = _dot(h, wa_ref[:, 2 * pa:2 * pa + mw])
    ra_ref[...] = _dot(h, wa_ref[:, 2 * pa + mw:2 * pa + 2 * mw])
    lr = _dot(h, wa_ref[:, 2 * pa + 2 * mw:2 * pa + 2 * mw + LANES])
    gl = _dot(lr.astype(BF16), wgu_ref[...]) + bgu_ref[...]
    la_ref[...] = _log_sigmoid(gl) / GATE_TAU
    gbr_ref[...] = _dot(h, wg_ref[...]).astype(gbr_ref.dtype)


def _in_projection(x, seq_len, gain, wa, wqb, wkvt, wg, wgu, bgu):
    t, d = x.shape
    nb = t // seq_len
    pa = H_A * DK_A
    mw = wqb.shape[1]
    tm = INPROJ_TILE
    assert t % tm == 0
    row = lambda w: pl.BlockSpec((tm, w), lambda i: (i, 0))
    full = lambda a: pl.BlockSpec(a.shape, lambda i: (0,) * a.ndim, pipeline_mode=pl.Buffered(1))
    if seq_len >= tm:
        per_seq = seq_len // tm
        ntile = tm // SB_TILE
        assert tm % SB_TILE == 0 and seq_len % tm == 0
        kt_spec = pl.BlockSpec((1, mw, tm), lambda i: (i // per_seq, 0, i % per_seq))
        kt16_spec = pl.BlockSpec((1, ntile, mw, SB_TILE), lambda i: (i // per_seq, i % per_seq, 0, 0))
        kt16_shape = (nb, seq_len // SB_TILE, mw, SB_TILE)
    else:
        nseq = tm // seq_len
        assert tm % seq_len == 0
        kt_spec = pl.BlockSpec((nseq, mw, seq_len), lambda i: (i, 0, 0))
        kt16_spec = pl.BlockSpec((nseq, 1, mw, seq_len), lambda i: (i, 0, 0, 0))
        kt16_shape = (nb, 1, mw, seq_len)
    outs = [
        (jax.ShapeDtypeStruct((t, pa), F32), row(pa)), (jax.ShapeDtypeStruct((t, pa), F32), row(pa)),
        (jax.ShapeDtypeStruct((t, mw), F32), row(mw)), (jax.ShapeDtypeStruct((t, mw), F32), row(mw)),
        (jax.ShapeDtypeStruct((t, pa), F32), row(pa)),
        (jax.ShapeDtypeStruct((t, mw), BF16), row(mw)),
        (jax.ShapeDtypeStruct((nb, mw, seq_len), F32), kt_spec), (jax.ShapeDtypeStruct((nb, mw, seq_len), F32), kt_spec),
        (jax.ShapeDtypeStruct(kt16_shape, BF16), kt16_spec), (jax.ShapeDtypeStruct(kt16_shape, BF16), kt16_spec),
        (jax.ShapeDtypeStruct((t, wg.shape[1]), BF16), row(wg.shape[1])),
    ]
    return pl.pallas_call(
        _inproj_kernel,
        grid=(t // tm,),
        in_specs=[row(d), full(gain), full(wa), full(wqb), full(wkvt), full(wg), full(wgu), full(bgu)],
        out_specs=[spec for _, spec in outs],
        out_shape=[shape for shape, _ in outs],
        compiler_params=_cparams(("arbitrary",)),
        name="in_projection",
    )(x, gain, wa, wqb, wkvt, wg, wgu, bgu)


def _gla_chunk(q, k, v, b, st):
    c = q.shape[0]
    b_last = b[c - 1:c, :]
    rows = lax.broadcasted_iota(I32, (c, LANES), 0)
    nsub = c // GLA_SUB
    refs = [jnp.zeros((1, LANES), F32)] + [b[i * GLA_SUB - 1:i * GLA_SUB, :] for i in range(1, nsub)]
    ref_rows = refs[0]
    for i in range(1, nsub):
        ref_rows = jnp.where(rows >= i * GLA_SUB, refs[i], ref_rows)
    q_rel = q * jnp.exp(b - ref_rows)
    lhs = jnp.concatenate(
        [jnp.where((rows >= i * GLA_SUB) & (rows < (i + 1) * GLA_SUB), q_rel, 0.0) for i in range(nsub)],
        axis=1).astype(BF16)
    rhs = jnp.concatenate(
        [jnp.where(rows < (i + 1) * GLA_SUB, k * jnp.exp(jnp.minimum(refs[i] - b, GLA_EXP_CLAMP)), 0.0)
         for i in range(nsub)], axis=1).astype(BF16)
    att = _dot_nt(lhs, rhs)
    tt = lax.broadcasted_iota(I32, (c, c), 0)
    ss = lax.broadcasted_iota(I32, (c, c), 1)
    att = jnp.where(ss <= tt, att, 0.0)
    v16 = v.astype(BF16)
    inter = _dot_nt((q * jnp.exp(b)).astype(BF16), st.astype(BF16))
    intra = _dot(att.astype(BF16), v16)
    kd = (k * jnp.exp(b_last - b)).astype(BF16)
    st_new = st * jnp.exp(b_last) + _dot_tn(v16, kd)
    return inter + intra, st_new


def _gla_kernel(qa_ref, ka_ref, va_ref, ra_ref, la_ref, s0_ref, gain_ref, o_ref, sfin_ref, st_ref):
    j = pl.program_id(1)
    nj = pl.num_programs(1)
    nseq, rows_per_step, _ = qa_ref.shape
    c = GLA_CHUNK
    zpad = jnp.zeros((LANES - DK_A, DV_A), F32)

    def state_rows(h):
        return slice((h % 2) * DK_A, (h % 2 + 1) * DK_A)

    @pl.when(j == 0)
    def _():
        for si in range(nseq):
            for h in range(H_A):
                parts = [s0_ref[si, h], zpad] if h % 2 == 0 else [zpad, s0_ref[si, h]]
                st_ref[si * H_A + h] = jnp.concatenate(parts, axis=0).T

    ti = lax.broadcasted_iota(I32, (rows_per_step, rows_per_step), 0)
    si = lax.broadcasted_iota(I32, (rows_per_step, rows_per_step), 1)
    chunk_shift = c.bit_length() - 1
    same_chunk = (ti >> chunk_shift) == (si >> chunk_shift)
    tril_blocks = jnp.where(same_chunk & (si <= ti), 1.0, 0.0).astype(BF16)
    gain = gain_ref[...]
    lane = lax.broadcasted_iota(I32, (1, LANES), 1)
    half_masks = (lane < DK_A, lane >= DK_A)
    for si in range(nseq):
        la_hi, la_lo = _split_bf16(la_ref[si])
        b_all = _dot(tril_blocks, la_hi) + _dot(tril_blocks, la_lo)
        for h in range(H_A):
            hp = slice((h // 2) * LANES, (h // 2 + 1) * LANES)
            hv = slice(h * DV_A, (h + 1) * DV_A)
            mine = half_masks[h % 2]
            st = st_ref[si * H_A + h]
            for ci in range(rows_per_step // c):
                r0 = ci * c
                q = jnp.where(mine, qa_ref[si, r0:r0 + c, hp], 0.0) * (DK_A ** -0.5)
                k = jnp.where(mine, ka_ref[si, r0:r0 + c, hp], 0.0)
                o, st = _gla_chunk(q, k, va_ref[si, r0:r0 + c, hv], b_all[r0:r0 + c, hp], st)
                r = ra_ref[si, r0:r0 + c, hv]
                o = _rms_norm(o, gain) * (r * _sigmoid(r))
                o_ref[si, r0:r0 + c, hv] = o.astype(o_ref.dtype)
            st_ref[si * H_A + h] = st

    @pl.when(j == nj - 1)
    def _():
        for si in range(nseq):
            for h in range(H_A):
                sfin_ref[si, h] = st_ref[si * H_A + h].T[state_rows(h), :]


def _gla(qa, ka, va, ra, la, s0, gain, rows_per_step):
    assert 2 * DK_A == LANES and H_A % 2 == 0
    b, s, pa = qa.shape
    mw = va.shape[-1]
    ns = GLA_SEQS
    assert b % ns == 0
    seq = lambda w: pl.BlockSpec((ns, rows_per_step, w), lambda i, j: (i, j, 0))
    state = pl.BlockSpec((ns, H_A, DK_A, DV_A), lambda i, j: (i, 0, 0, 0))
    return pl.pallas_call(
        _gla_kernel,
        grid=(b // ns, s // rows_per_step),
        in_specs=[seq(pa), seq(pa), seq(mw), seq(mw), seq(pa), state,
                  pl.BlockSpec(gain.shape, lambda i, j: (0, 0))],
        out_specs=[seq(mw), state],
        out_shape=[jax.ShapeDtypeStruct((b, s, mw), BF16),
                   jax.ShapeDtypeStruct((b, H_A, DK_A, DV_A), F32)],
        scratch_shapes=[pltpu.VMEM((ns * H_A, LANES, LANES), F32)],
        compiler_params=_cparams(("arbitrary", "arbitrary")),
        name="gla",
    )(qa, ka, va, ra, la, s0, gain)


def _head_lane_masks():
    lane = lax.broadcasted_iota(I32, (1, LANES), 1)
    return lane < DH_B, lane >= DH_B


def _sb_neg_tri(tk):
    ji = lax.broadcasted_iota(I32, (tk, tk), 0)
    si = lax.broadcasted_iota(I32, (tk, tk), 1)
    return jnp.where(ji >= si, -1.0, 0.0).astype(BF16)


def _sb_stack_queries(q, qs_ref, base=0):
    m0, m1 = _head_lane_masks()
    for p in range(q.shape[1] // LANES):
        qp = (q[:, p * LANES:(p + 1) * LANES].astype(F32) * (DH_B ** -0.5 * LOG2_E)).astype(BF16)
        zero = jnp.zeros_like(qp)
        qs_ref[base + p] = jnp.concatenate([jnp.where(m0, qp, zero), jnp.where(m1, qp, zero)], axis=0)


def _pair_lanes(p):
    return slice(p * LANES, (p + 1) * LANES)


def _lane_fit(x, width):
    if width >= LANES:
        return jnp.concatenate([x] * (width // LANES), axis=1)
    return x[:, 0:width]


def _sb_tile_step(qs_ref, acc_ref, carry_ref, k_tile, v_tile, ntri, diagonal, one_suffix_matmul=False):
    npair, rows, _ = qs_ref.shape
    tq = rows // 2
    tk = ntri.shape[1]
    m0, _ = _head_lane_masks()
    if diagonal:
        t = lax.broadcasted_iota(I32, (rows, tk), 0)
        t = jnp.where(t >= tq, t - tq, t)
        visible = lax.broadcasted_iota(I32, (rows, tk), 1) < t
    def scores(p):
        z = _dot(qs_ref[p], k_tile(p))
        sp = jnp.maximum(z, 0.0) + jnp.log2(1.0 + jnp.exp2(-jnp.abs(z)))
        if diagonal:
            sp = jnp.where(visible, sp, 0.0)
        return z, sp.astype(BF16)

    if one_suffix_matmul:
        zs, sps = zip(*[scores(p) for p in range(npair)])
        stacked = _dot(jnp.concatenate(sps, axis=0), ntri)
        suffixes = [stacked[p * rows:(p + 1) * rows] for p in range(npair)]
    for p in range(npair):
        if one_suffix_matmul:
            z, suffix = zs[p], suffixes[p]
        else:
            z, sp = scores(p)
            suffix = _dot(sp, ntri)
        carry = carry_ref[p]
        w = jnp.exp2(z + suffix + _lane_fit(carry, tk))
        if diagonal:
            w = jnp.where(visible, w, 0.0)
        pv = _dot_nt(w.astype(BF16), v_tile(p))
        acc_ref[p] += jnp.where(m0, pv[0:tq], pv[tq:rows])
        carry_ref[p] = carry + jnp.broadcast_to(suffix[:, 0:1], carry.shape)


def _sb_prompt_kernel(q_ref, k_ref, v_ref, o_ref, qs_ref, acc_ref, carry_ref):
    qi = pl.program_id(1)
    tk = SB_TILE
    _sb_stack_queries(q_ref[0], qs_ref)
    acc_ref[...] = jnp.zeros_like(acc_ref)
    carry_ref[...] = jnp.zeros_like(carry_ref)
    ntri = _sb_neg_tri(tk)

    def step(jb, diagonal):
        _sb_tile_step(qs_ref, acc_ref, carry_ref, lambda p: k_ref[0, jb, _pair_lanes(p), :],
                      lambda p: v_ref[0, jb, _pair_lanes(p), :], ntri, diagonal, True)

    step(qi, True)

    def body(i, c):
        step(qi - 1 - 2 * i, False)
        step(qi - 2 - 2 * i, False)
        return c

    lax.fori_loop(0, qi // 2, body, 0)

    @pl.when(qi % 2 == 1)
    def _():
        step(0, False)

    for p in range(acc_ref.shape[0]):
        o_ref[0, :, p * LANES:(p + 1) * LANES] = acc_ref[p].astype(o_ref.dtype)


def _sb_scratch(tq, npair):
    return [pltpu.VMEM((npair, 2 * tq, LANES), BF16), pltpu.VMEM((npair, tq, LANES), F32),
            pltpu.VMEM((npair, 2 * tq, LANES), F32)]


def _sb_prompt(q, kt, vt):
    b, s, w = q.shape
    tq = SB_TILE
    assert kt.shape == (b, s // tq, w, tq)
    qspec = pl.BlockSpec((1, tq, w), lambda i, j: (i, j, 0))
    kvspec = pl.BlockSpec((1,) + kt.shape[1:], lambda i, j: (i, 0, 0, 0))
    return pl.pallas_call(
        _sb_prompt_kernel,
        grid=(b, s // tq),
        in_specs=[qspec, kvspec, kvspec],
        out_specs=qspec,
        out_shape=jax.ShapeDtypeStruct((b, s, w), BF16),
        scratch_shapes=_sb_scratch(tq, w // LANES),
        compiler_params=_cparams(("arbitrary", "arbitrary")),
        name="sb_prompt",
    )(q, kt, vt)


def _sb_sample_kernel(q_ref, kn_ref, vn_ref, kp_ref, vp_ref, o_ref, qs_ref, acc_ref, carry_ref):
    nseq, sq, w = q_ref.shape
    past = kp_ref.shape[3]
    npair = w // LANES
    tk = SB_TILE
    for si in range(nseq):
        _sb_stack_queries(q_ref[si], qs_ref, si * npair)
    acc_ref[...] = jnp.zeros_like(acc_ref)
    carry_ref[...] = jnp.zeros_like(carry_ref)
    _sb_tile_step(qs_ref, acc_ref, carry_ref, lambda e: kn_ref[e // npair, 0, _pair_lanes(e % npair), :],
                  lambda e: vn_ref[e // npair, 0, _pair_lanes(e % npair), :], _sb_neg_tri(sq), True, True)
    ntri = _sb_neg_tri(tk)

    def tile(j):
        cols = pl.ds(pl.multiple_of(past - (j + 1) * tk, tk), tk)

        def pair(ref, e):
            p = e % npair
            return ref[e // npair, 2 * p:2 * p + 2, :, cols].reshape(LANES, tk).astype(BF16)

        _sb_tile_step(qs_ref, acc_ref, carry_ref, lambda e: pair(kp_ref, e), lambda e: pair(vp_ref, e), ntri, False,
                      True)

    def body(i, c):
        tile(2 * i)
        tile(2 * i + 1)
        return c

    lax.fori_loop(0, past // tk // 2, body, 0)
    for e in range(acc_ref.shape[0]):
        o_ref[e // npair, :, _pair_lanes(e % npair)] = acc_ref[e].astype(o_ref.dtype)


def _sb_sample(q, kt_new, vt_new, kt_past, vt_past):
    b, sq, w = q.shape
    past = kt_past.shape[3]
    ns = SB_SAMPLE_SEQS
    assert past % (2 * SB_TILE) == 0 and 2 * DH_B == LANES and b % ns == 0
    qspec = pl.BlockSpec((ns, sq, w), lambda i: (i, 0, 0))
    new = pl.BlockSpec((ns, 1, w, sq), lambda i: (i, 0, 0, 0))
    old = pl.BlockSpec((ns, H_B, DH_B, past), lambda i: (i, 0, 0, 0))
    return pl.pallas_call(
        _sb_sample_kernel,
        grid=(b // ns,),
        in_specs=[qspec, new, new, old, old],
        out_specs=qspec,
        out_shape=jax.ShapeDtypeStruct((b, sq, w), BF16),
        scratch_shapes=_sb_scratch(sq, ns * (w // LANES)),
        compiler_params=_cparams(("arbitrary",)),
        name="sb_sample",
    )(q, kt_new, vt_new, kt_past, vt_past)


def _first_argmax(vals, nrows):
    idx = lax.broadcasted_iota(I32, vals.shape, 0)
    top = jnp.max(vals, axis=0, keepdims=True)
    first = jnp.min(jnp.where(vals == top, idx, nrows), axis=0, keepdims=True)
    return top, first, idx


def _merge_kernel(oa_ref, ob_ref, g_ref, x_ref, wb0_ref, wb1_ref, wo_ref, gain_ref, wr_ref, br_ref,
                  x1_ref, h2_ref, eid_ref, wcol_ref):
    d = x_ref.shape[1]
    ya = _dot(oa_ref[...], wb0_ref[...])
    yb = _dot(ob_ref[...], wb1_ref[...])
    g = g_ref[...].astype(F32)
    m = _sigmoid(g[:, 0:d]) * ya + _sigmoid(g[:, d:2 * d]) * yb
    x1 = x_ref[...] + _dot(m.astype(BF16), wo_ref[...])
    x1_ref[...] = x1
    h2 = _rms_norm(x1, gain_ref[...])
    h2_ref[...] = h2.astype(h2_ref.dtype).reshape(h2_ref.shape)

    h_hi, h_lo = _split_bf16(h2)
    w_hi, w_lo = _split_bf16(wr_ref[...])
    lt = _dot_nt(w_hi, h_hi) + _dot_nt(w_hi, h_lo) + _dot_nt(w_lo, h_hi) + br_ref[:, 0:1]
    gl = lt[0:N_GROUPS, :]
    g_top, g_idx, _ = _first_argmax(gl, N_GROUPS)
    g_e = jnp.exp(gl - g_top)
    g_p = jnp.max(g_e / jnp.sum(g_e, axis=0, keepdims=True), axis=0, keepdims=True)
    el = jnp.zeros((EXPERTS_PER_GROUP, lt.shape[1]), F32)
    for g in range(N_GROUPS):
        r0 = 8 + g * EXPERTS_PER_GROUP
        el = jnp.where(g_idx == g, lt[r0:r0 + EXPERTS_PER_GROUP, :], el)
    e_top, i1, eidx = _first_argmax(el, EXPERTS_PER_GROUP)
    e_e = jnp.exp(el - e_top)
    e_p = e_e / jnp.sum(e_e, axis=0, keepdims=True)
    p1 = jnp.max(e_p, axis=0, keepdims=True)
    rest = jnp.where(eidx == i1, -1.0, e_p)
    p2, i2, _ = _first_argmax(rest, EXPERTS_PER_GROUP)
    norm = p1 + p2
    w1 = g_p * (p1 / norm)
    w2 = g_p * (p2 / norm)
    eid_ref[...] = jnp.concatenate([g_idx * EXPERTS_PER_GROUP + i1, g_idx * EXPERTS_PER_GROUP + i2], axis=0)
    rows = lax.broadcasted_iota(I32, (LANES, lt.shape[1]), 0)
    wrows = jnp.where(rows == 0, w1, jnp.where(rows == 1, w2, 0.0))
    wcol_ref[...] = wrows.T


def _merge(oa, ob, gbr, x, wb0, wb1, wo, gain, wr, br):
    t, d = x.shape
    tm = MERGE_TILE
    assert t % tm == 0
    row = lambda w: pl.BlockSpec((tm, w), lambda i: (i, 0))
    full = lambda a: pl.BlockSpec(a.shape, lambda i: (0,) * a.ndim)
    return pl.pallas_call(
        _merge_kernel,
        grid=(t // tm,),
        in_specs=[row(oa.shape[1]), row(ob.shape[1]), row(gbr.shape[1]), row(d),
                  full(wb0), full(wb1), full(wo), full(gain), full(wr), full(br)],
        out_specs=[row(d), pl.BlockSpec((tm, d // LANES, LANES), lambda i: (i, 0, 0)),
                   pl.BlockSpec((TOP_K, tm), lambda i: (0, i)), row(LANES)],
        out_shape=[jax.ShapeDtypeStruct((t, d), F32), jax.ShapeDtypeStruct((t, d // LANES, LANES), BF16),
                   jax.ShapeDtypeStruct((TOP_K, t), I32), jax.ShapeDtypeStruct((t, LANES), F32)],
        compiler_params=_cparams(("arbitrary",)),
        name="merge_router",
    )(oa, ob, gbr, x, wb0, wb1, wo, gain, wr, br)


def _positions_kernel(eid_ref, dest_ref, counts_ref, rank_ref):
    nblk, width = eid_ref.shape
    ji = lax.broadcasted_iota(I32, (width, width), 0)
    si = lax.broadcasted_iota(I32, (width, width), 1)
    prefix = jnp.where(ji <= si, 1.0, 0.0).astype(BF16)
    expert = lax.broadcasted_iota(I32, (N_EXPERTS, width), 0)
    group = max(g for g in (8, 4, 2, 1) if nblk % g == 0)

    def onehot(i):
        return expert == eid_ref[pl.ds(i, 1), :]

    def rank_body(ig, run):
        first = pl.multiple_of(ig * group, group)
        ohs = [onehot(first + j) for j in range(group)]
        stacked = jnp.concatenate([jnp.where(oh, 1.0, 0.0) for oh in ohs], axis=0).astype(BF16)
        cum = _dot(stacked, prefix)
        ranks = []
        for j, oh in enumerate(ohs):
            cum_j = cum[j * N_EXPERTS:(j + 1) * N_EXPERTS, :] + run
            ranks.append(jnp.sum(jnp.where(oh, cum_j, 0.0), axis=0, keepdims=True) - 1.0)
            run = cum_j[:, width - 1:width]
        rank_ref[pl.ds(first, group), :] = jnp.concatenate(ranks, axis=0)
        return run

    counts = lax.fori_loop(0, nblk // group, rank_body, jnp.zeros((N_EXPERTS, 1), F32))
    counts_ref[...] = jnp.broadcast_to(counts, counts_ref.shape).astype(I32)
    c_hi = jnp.floor(counts * (1.0 / 256.0))
    c_lo = counts - 256.0 * c_hi
    ei = lax.broadcasted_iota(I32, (N_EXPERTS, N_EXPERTS), 0)
    ej = lax.broadcasted_iota(I32, (N_EXPERTS, N_EXPERTS), 1)
    strict = jnp.where(ej < ei, 1.0, 0.0).astype(BF16)
    digits = jnp.concatenate([jnp.broadcast_to(c_hi, (N_EXPERTS, LANES)),
                              jnp.broadcast_to(c_lo, (N_EXPERTS, LANES))], axis=1).astype(BF16)
    sums = _dot(strict, digits)
    start = 256.0 * sums[:, 0:1] + sums[:, LANES:LANES + 1]

    def dest_body(ig, carry):
        first = pl.multiple_of(ig * group, group)
        offs = [jnp.sum(jnp.where(onehot(first + j), start, 0.0), axis=0, keepdims=True) for j in range(group)]
        rows = pl.ds(first, group)
        dest_ref[rows, :] = (rank_ref[rows, :] + jnp.concatenate(offs, axis=0)).astype(I32)
        return carry

    lax.fori_loop(0, nblk // group, dest_body, 0)


def _positions(eid_blocks):
    nblk, width = eid_blocks.shape
    vm = lambda shape: pl.BlockSpec(shape, lambda: (0,) * len(shape))
    return pl.pallas_call(
        _positions_kernel,
        in_specs=[vm((nblk, width))],
        out_specs=[vm((nblk, width)), vm((N_EXPERTS, LANES))],
        out_shape=[jax.ShapeDtypeStruct((nblk, width), I32), jax.ShapeDtypeStruct((N_EXPERTS, LANES), I32)],
        scratch_shapes=[pltpu.VMEM((nblk, width), F32)],
        name="positions",
    )(eid_blocks)


def _dispatch_kernel(n_prompt_tiles, dest_ref, hp_ref, hs_ref, xs_ref, sem):
    i = pl.program_id(0)
    tm = dest_ref.shape[1]

    def scatter(src_ref):
        def start(r, c):
            for k in range(TOP_K):
                pltpu.make_async_copy(src_ref.at[r], xs_ref.at[dest_ref[k, r]], sem).start(priority=k)
            return c

        lax.fori_loop(0, tm, start, 0, unroll=DMA_UNROLL)
        for k in range(TOP_K):
            pltpu.make_async_copy(src_ref, xs_ref.at[pl.ds(0, tm)], sem).wait()

    @pl.when(i < n_prompt_tiles)
    def _():
        scatter(hp_ref)

    @pl.when(i >= n_prompt_tiles)
    def _():
        scatter(hs_ref)


def _dispatch(dest, h_prompt, h_sample):
    t = dest.shape[1]
    slab = h_prompt.shape[1:]
    tm = DISPATCH_TILE
    assert h_prompt.shape[0] % tm == 0 and h_sample.shape[0] % tm == 0
    npt = h_prompt.shape[0] // tm
    return pl.pallas_call(
        functools.partial(_dispatch_kernel, npt),
        grid=(t // tm,),
        in_specs=[pl.BlockSpec((TOP_K, tm), lambda i: (0, i), memory_space=pltpu.SMEM),
                  pl.BlockSpec((tm,) + slab, lambda i: (jnp.minimum(i, npt - 1), 0, 0)),
                  pl.BlockSpec((tm,) + slab, lambda i: (jnp.maximum(i - npt, 0), 0, 0))],
        out_specs=pl.BlockSpec(memory_space=pl.ANY),
        out_shape=jax.ShapeDtypeStruct((TOP_K * t,) + slab, h_prompt.dtype),
        scratch_shapes=[pltpu.SemaphoreType.DMA(())],
        compiler_params=_cparams(("arbitrary",)),
        name="dispatch",
    )(dest, h_prompt, h_sample)


def _experts_kernel(vblk_ref, vexp_ref, vlo_ref, vhi_ref, vnext_ref, vslot_ref, xs_ref, wg_ref, wu_ref, wd_ref,
                    ys_ref, wg32_ref, wu32_ref, wd32_ref, wg16_ref, wu16_ref, wd16_ref, sems):
    v = pl.program_id(0)
    lo = vlo_ref[v]
    hi = vhi_ref[v]
    prev = jnp.maximum(v - 1, 0)
    first = jnp.logical_or(v == 0, vblk_ref[v] != vblk_ref[prev])
    new_expert = jnp.logical_or(v == 0, vexp_ref[v] != vexp_ref[prev])

    def weight_copies(e, slot):
        return [pltpu.make_async_copy(src.at[e], dst.at[slot], sems.at[slot])
                for src, dst in ((wg_ref, wg32_ref), (wu_ref, wu32_ref), (wd_ref, wd32_ref))]

    @pl.when(v == 0)
    def _():
        for cp in weight_copies(vexp_ref[0], 0):
            cp.start()

    @pl.when(new_expert)
    def _():
        slot = vslot_ref[v]
        for cp in weight_copies(vexp_ref[v], slot):
            cp.wait()
        wg16_ref[...] = wg32_ref[slot].astype(BF16)
        wu16_ref[...] = wu32_ref[slot].astype(BF16)
        wd16_ref[...] = wd32_ref[slot].astype(BF16)

        @pl.when(vnext_ref[v] >= 0)
        def _():
            for cp in weight_copies(vnext_ref[v], 1 - slot):
                cp.start()

    @pl.when(first)
    def _():
        ys_ref[...] = jnp.zeros_like(ys_ref)

    @pl.when(hi > lo)
    def _():
        tm = xs_ref.shape[0]
        d = wg_ref.shape[1]
        x = xs_ref[...].reshape(tm, d)
        gate = _dot(x, wg16_ref[...])
        up = _dot(x, wu16_ref[...])
        hid = (gate * _sigmoid(gate) * up).astype(BF16)
        y = _dot(hid, wd16_ref[...]).astype(ys_ref.dtype)
        rows = lax.broadcasted_iota(I32, y.shape, 0)
        mine = (rows >= lo) & (rows < hi)
        ys_ref[...] = jnp.where(mine, y, ys_ref[...].reshape(tm, d)).reshape(ys_ref.shape)


def _experts(plan, xs, wg, wu, wd):
    a = xs.shape[0]
    slab = xs.shape[1:]
    d, de = wg.shape[1:]
    tm = MOE_TILE
    block = lambda v, b, *_: (b[v], 0, 0)
    hbm = pl.BlockSpec(memory_space=pl.ANY)
    grid_spec = pltpu.PrefetchScalarGridSpec(
        num_scalar_prefetch=len(plan),
        grid=(plan[0].shape[0],),
        in_specs=[pl.BlockSpec((tm,) + slab, block), hbm, hbm, hbm],
        out_specs=pl.BlockSpec((tm,) + slab, block),
        scratch_shapes=[pltpu.VMEM((2, d, de), F32), pltpu.VMEM((2, d, de), F32), pltpu.VMEM((2, de, d), F32),
                        pltpu.VMEM((d, de), BF16), pltpu.VMEM((d, de), BF16), pltpu.VMEM((de, d), BF16),
                        pltpu.SemaphoreType.DMA((2,))],
    )
    return pl.pallas_call(
        _experts_kernel,
        grid_spec=grid_spec,
        out_shape=jax.ShapeDtypeStruct((a,) + slab, MOE_OUT_DTYPE),
        compiler_params=_cparams(("arbitrary",)),
        name="experts",
    )(*plan, xs, wg, wu, wd)


def _visit_plan(counts, n_rows):
    tm = MOE_TILE
    nblk = n_rows // tm
    n_visits = nblk + N_EXPERTS - 1
    ends = jnp.cumsum(counts)
    starts = ends - counts
    first_blk = starts // tm
    nvis = jnp.where(counts > 0, (ends + tm - 1) // tm - first_blk, 0)
    vis_end = jnp.cumsum(nvis)
    vis_start = vis_end - nvis
    v = jnp.arange(n_visits, dtype=I32)
    e = jnp.minimum(jnp.sum((vis_end[None, :] <= v[:, None]).astype(I32), axis=1), N_EXPERTS - 1)
    valid = v < vis_end[-1]
    blk = first_blk[e] + (v - vis_start[e])
    lo = jnp.clip(starts[e] - blk * tm, 0, tm)
    hi = jnp.clip(ends[e] - blk * tm, 0, tm)
    ids = jnp.arange(N_EXPERTS, dtype=I32)
    used = counts > 0
    last_e = jnp.max(jnp.where(used, ids, 0))
    blk = jnp.where(valid, blk, nblk - 1).astype(I32)
    e = jnp.where(valid, e, last_e).astype(I32)
    lo = jnp.where(valid, lo, 0).astype(I32)
    hi = jnp.where(valid, hi, 0).astype(I32)
    later_used = used[None, :] & (ids[None, :] > ids[:, None])
    next_used = jnp.min(jnp.where(later_used, ids[None, :], N_EXPERTS), axis=1)
    next_used = jnp.where(next_used < N_EXPERTS, next_used, -1).astype(I32)
    slot = ((jnp.cumsum(used.astype(I32)) - 1) % 2).astype(I32)
    return blk, e, lo, hi, next_used[e], slot[e]


def _combine_kernel(dest_ref, ys_ref, x1_ref, wcol_ref, gain_ref, out_ref, buf_ref, sems):
    tm, d = x1_ref.shape
    part = tm // COMBINE_PARTS

    def start(r, c, sem):
        for k in range(TOP_K):
            pltpu.make_async_copy(ys_ref.at[dest_ref[k, r]], buf_ref.at[k, r], sem).start(priority=k)
        return c

    for h in range(COMBINE_PARTS):
        lax.fori_loop(h * part, (h + 1) * part, functools.partial(start, sem=sems.at[h]), 0, unroll=DMA_UNROLL)
    for h in range(COMBINE_PARTS):
        rows = pl.ds(h * part, part)
        for k in range(TOP_K):
            pltpu.make_async_copy(ys_ref.at[rows], buf_ref.at[k, rows], sems.at[h]).wait()
        y = (wcol_ref[rows, 0:1] * buf_ref[0, rows].reshape(part, d).astype(F32)
             + wcol_ref[rows, 1:2] * buf_ref[1, rows].reshape(part, d).astype(F32))
        out_ref[rows, :] = _rms_norm(x1_ref[rows, :] + y, gain_ref[...])


def _combine(dest, first_token, ys, x1, wcol, gain):
    t, d = x1.shape
    tm = COMBINE_TILE
    assert t % tm == 0 and first_token % tm == 0
    off = first_token // tm
    row = lambda w: pl.BlockSpec((tm, w), lambda i: (i, 0))
    return pl.pallas_call(
        _combine_kernel,
        grid=(t // tm,),
        in_specs=[pl.BlockSpec((TOP_K, tm), lambda i: (0, i + off), memory_space=pltpu.SMEM),
                  pl.BlockSpec(memory_space=pl.ANY), row(d), row(LANES),
                  pl.BlockSpec(gain.shape, lambda i: (0, 0))],
        out_specs=row(d),
        out_shape=jax.ShapeDtypeStruct((t, d), F32),
        scratch_shapes=[pltpu.VMEM((TOP_K, tm) + ys.shape[1:], ys.dtype), pltpu.SemaphoreType.DMA((COMBINE_PARTS,))],
        compiler_params=_cparams(("arbitrary",)),
        name="combine",
    )(dest, ys, x1, wcol, gain)


def _prepare_weights(w_in, w_gla_gate_up, b_gla_gate, w_branch, w_out, w_router_group, b_router_group,
                     w_router_expert, b_router_expert):
    d = w_in.shape[0]
    qk = H_A * DK_A
    mw = H_A * DV_A
    c = 0
    w_qa, c = w_in[:, c:c + qk], c + qk
    w_ka, c = w_in[:, c:c + qk], c + qk
    w_va, c = w_in[:, c:c + mw], c + mw
    w_ra, c = w_in[:, c:c + mw], c + mw
    w_lr, c = w_in[:, c:c + GATE_RANK], c + GATE_RANK
    w_b, c = w_in[:, c:c + 3 * mw], c + 3 * mw
    w_g = w_in[:, c:]
    wa = jnp.concatenate([w_qa, w_ka, w_va, w_ra,
                          jnp.pad(w_lr, ((0, 0), (0, LANES - GATE_RANK)))], axis=1).astype(BF16)
    wgu = jnp.pad(w_gla_gate_up, ((0, LANES - GATE_RANK), (0, 0))).astype(BF16)
    bgu = b_gla_gate[None, :]
    wr = jnp.zeros((LANES, d), F32)
    wr = wr.at[0:N_GROUPS].set(w_router_group.T).at[8:8 + N_EXPERTS].set(w_router_expert.T)
    br = jnp.zeros((LANES,), F32).at[0:N_GROUPS].set(b_router_group).at[8:8 + N_EXPERTS].set(b_router_expert)
    br = jnp.broadcast_to(br[:, None], (LANES, LANES))
    return dict(wa=wa, wqb=w_b[:, 0:mw].astype(BF16), wkvt=w_b[:, mw:3 * mw].T.astype(BF16),
                wg=w_g.astype(BF16), wgu=wgu, bgu=bgu,
                wb0=w_branch[0].astype(BF16), wb1=w_branch[1].astype(BF16), wo=w_out.astype(BF16),
                wr=wr, br=br)


def _mixers(x, s0, k_past, v_past, w, norm_mix_gain, gla_norm_gain, norm_ffn_gain):
    b, s, d = x.shape
    xf = x.reshape(b * s, d)
    qa, ka, va, ra, la, qb, kt, vt, kt16, vt16, gbr = _in_projection(
        xf, s, norm_mix_gain[None, :], w["wa"], w["wqb"], w["wkvt"], w["wg"], w["wgu"], w["bgu"])
    seq = lambda a: a.reshape(b, s, a.shape[-1])
    oa, s_new = _gla(seq(qa), seq(ka), seq(va), seq(ra), seq(la), s0, gla_norm_gain[None, :],
                     min(s, GLA_ROWS))
    to_channel_major = lambda a: jnp.transpose(a, (0, 2, 3, 1))
    if k_past is None:
        ob = _sb_prompt(seq(qb), kt16, vt16)
    else:
        ob = _sb_sample(seq(qb), kt16, vt16, to_channel_major(k_past), to_channel_major(v_past))
    x1, h2, eid, wcol = _merge(oa.reshape(b * s, -1), ob.reshape(b * s, -1), gbr, xf, w["wb0"], w["wb1"],
                               w["wo"], norm_ffn_gain[None, :], w["wr"], w["br"])
    from_channel_major = lambda a: jnp.transpose(a.reshape(b, H_B, DH_B, s), (0, 3, 1, 2))
    return x1, h2, eid, wcol, s_new, from_channel_major(kt), from_channel_major(vt)


def kernel(x_prompt, x_sample, state_gla, cache_sb_k, cache_sb_v, norm_mix_gain, w_in, w_gla_gate_up, b_gla_gate, gla_norm_gain, w_branch, w_out, norm_ffn_gain, w_router_group, b_router_group, w_router_expert, b_router_expert, w_exp_gate, w_exp_up, w_exp_down, norm_final_gain):
    depth = w_in.shape[0]
    assert depth == 1, "one trunk layer per step"
    l = 0
    w = _prepare_weights(w_in[l], w_gla_gate_up[l], b_gla_gate[l], w_branch[l], w_out[l], w_router_group[l],
                         b_router_group[l], w_router_expert[l], b_router_expert[l])
    bp, sp, d = x_prompt.shape
    bs, ss, _ = x_sample.shape
    s0 = jnp.zeros((bp, H_A, DK_A, DV_A), x_prompt.dtype)
    x1p, h2p, eidp, wcolp, gla_p, k_p, v_p = _mixers(
        x_prompt, s0, None, None, w, norm_mix_gain[l], gla_norm_gain[l], norm_ffn_gain[l])
    x1s, h2s, eids, wcols, gla_s, k_s, v_s = _mixers(
        x_sample, state_gla[l], cache_sb_k[l], cache_sb_v[l], w, norm_mix_gain[l], gla_norm_gain[l],
        norm_ffn_gain[l])

    tp, ts = bp * sp, bs * ss
    eid = jnp.concatenate([eidp, eids], axis=1)
    dest_blocks, counts = _positions(eid.reshape(-1, SORT_WIDTH))
    dest = dest_blocks.reshape(TOP_K, tp + ts)
    xs = _dispatch(dest, h2p, h2s)
    plan = _visit_plan(counts[:, 0], TOP_K * (tp + ts))
    ys = _experts(plan, xs, w_exp_gate[l], w_exp_up[l], w_exp_down[l])
    gf = norm_final_gain[None, :]
    y_prompt = _combine(dest, 0, ys, x1p, wcolp, gf).reshape(bp, sp, d)
    y_sample = _combine(dest, tp, ys, x1s, wcols, gf).reshape(bs, ss, d)
    return (y_prompt, y_sample, gla_p[None], k_p[None], v_p[None], gla_s[None], k_s[None], v_s[None])
```

```python
import functools

import jax
import jax.numpy as jnp
from jax import lax
from jax.experimental import pallas as pl
from jax.experimental.pallas import tpu as pltpu

F32 = jnp.float32
BF16 = jnp.bfloat16
MOE_OUT_DTYPE = jnp.bfloat16
I32 = jnp.int32

LANES = 128
LOG2_E = 1.4426950408889634
RMS_EPS = 1e-6
GATE_TAU = 16.0
H_A = 4
DK_A = 64
DV_A = 128
GATE_RANK = 16
H_B = 8
DH_B = 64
N_GROUPS = 4
EXPERTS_PER_GROUP = 8
N_EXPERTS = N_GROUPS * EXPERTS_PER_GROUP
TOP_K = 2
GLA_CHUNK = 64
GLA_SUB = 16
GLA_EXP_CLAMP = 80.0
GLA_SEQS = 4
GLA_ROWS = 256
SB_TILE = 256
SB_SAMPLE_SEQS = 2
MOE_TILE = 512
SORT_WIDTH = 256
INPROJ_TILE = 512
MERGE_TILE = 1024
DISPATCH_TILE = 2048
COMBINE_TILE = 512
COMBINE_PARTS = 4
DMA_UNROLL = 8
VMEM_LIMIT = 56 * 1024 * 1024


def _cparams(sem):
    return pltpu.CompilerParams(dimension_semantics=sem, vmem_limit_bytes=VMEM_LIMIT)


def _dot(a, b):
    return jnp.dot(a, b, preferred_element_type=F32)


def _dot_nt(a, b):
    return lax.dot_general(a, b, (((1,), (1,)), ((), ())), preferred_element_type=F32)


def _dot_tn(a, b):
    return lax.dot_general(a, b, (((0,), (0,)), ((), ())), preferred_element_type=F32)


def _split_bf16(x):
    hi = x.astype(BF16)
    lo = (x - hi.astype(F32)).astype(BF16)
    return hi, lo


def _log_sigmoid(x):
    return jnp.minimum(x, 0.0) - jnp.log(1.0 + jnp.exp(-jnp.abs(x)))


def _sigmoid(x):
    return 1.0 / (1.0 + jnp.exp(-x))


def _rms_norm(x, gain):
    return x * lax.rsqrt(jnp.mean(x * x, axis=-1, keepdims=True) + RMS_EPS) * gain


def _inproj_kernel(x_ref, gain_ref, wa_ref, wqb_ref, wkvt_ref, wg_ref, wgu_ref, bgu_ref,
                   qa_ref, ka_ref, va_ref, ra_ref, la_ref, qb_ref, kt_ref, vt_ref,
                   kt16_ref, vt16_ref, gbr_ref):
    h = _rms_norm(x_ref[...], gain_ref[...]).astype(BF16)
    pa = H_A * DK_A
    mw = va_ref.shape[-1]
    kvt = _dot_nt(wkvt_ref[...], h)
    nseq, _, s = kt_ref.shape
    ntile, tile = kt16_ref.shape[1], kt16_ref.shape[3]
    for i in range(nseq):
        kt_ref[i] = kvt[0:mw, i * s:(i + 1) * s]
        vt_ref[i] = kvt[mw:2 * mw, i * s:(i + 1) * s]
        for j in range(ntile):
            cols = slice(i * s + j * tile, i * s + (j + 1) * tile)
            kt16_ref[i, j] = kvt[0:mw, cols].astype(BF16)
            vt16_ref[i, j] = kvt[mw:2 * mw, cols].astype(BF16)
    qb_ref[...] = _dot(h, wqb_ref[...]).astype(BF16)
    qa_ref[...] = _dot(h, wa_ref[:, 0:pa])
    ka_ref[...] = _dot(h, wa_ref[:, pa:2 * pa])
    va_ref[...] = _dot(h, wa_ref[:, 2 * pa:2 * pa + mw])
    ra_ref[...] = _dot(h, wa_ref[:, 2 * pa + mw:2 * pa + 2 * mw])
    lr = _dot(h, wa_ref[:, 2 * pa + 2 * mw:2 * pa + 2 * mw + LANES])
    gl = _dot(lr.astype(BF16), wgu_ref[...]) + bgu_ref[...]
    la_ref[...] = _log_sigmoid(gl) / GATE_TAU
    gbr_ref[...] = _dot(h, wg_ref[...]).astype(gbr_ref.dtype)


def _in_projection(x, seq_len, gain, wa, wqb, wkvt, wg, wgu, bgu):
    t, d = x.shape
    nb = t // seq_len
    pa = H_A * DK_A
    mw = wqb.shape[1]
    tm = INPROJ_TILE
    assert t % tm == 0
    row = lambda w: pl.BlockSpec((tm, w), lambda i: (i, 0))
    full = lambda a: pl.BlockSpec(a.shape, lambda i: (0,) * a.ndim, pipeline_mode=pl.Buffered(1))
    if seq_len >= tm:
        per_seq = seq_len // tm
        ntile = tm // SB_TILE
        assert tm % SB_TILE == 0 and seq_len % tm == 0
        kt_spec = pl.BlockSpec((1, mw, tm), lambda i: (i // per_seq, 0, i % per_seq))
        kt16_spec = pl.BlockSpec((1, ntile, mw, SB_TILE), lambda i: (i // per_seq, i % per_seq, 0, 0))
        kt16_shape = (nb, seq_len // SB_TILE, mw, SB_TILE)
    else:
        nseq = tm // seq_len
        assert tm % seq_len == 0
        kt_spec = pl.BlockSpec((nseq, mw, seq_len), lambda i: (i, 0, 0))
        kt16_spec = pl.BlockSpec((nseq, 1, mw, seq_len), lambda i: (i, 0, 0, 0))
        kt16_shape = (nb, 1, mw, seq_len)
    outs = [
        (jax.ShapeDtypeStruct((t, pa), F32), row(pa)), (jax.ShapeDtypeStruct((t, pa), F32), row(pa)),
        (jax.ShapeDtypeStruct((t, mw), F32), row(mw)), (jax.ShapeDtypeStruct((t, mw), F32), row(mw)),
        (jax.ShapeDtypeStruct((t, pa), F32), row(pa)),
        (jax.ShapeDtypeStruct((t, mw), BF16), row(mw)),
        (jax.ShapeDtypeStruct((nb, mw, seq_len), F32), kt_spec), (jax.ShapeDtypeStruct((nb, mw, seq_len), F32), kt_spec),
        (jax.ShapeDtypeStruct(kt16_shape, BF16), kt16_spec), (jax.ShapeDtypeStruct(kt16_shape, BF16), kt16_spec),
        (jax.ShapeDtypeStruct((t, wg.shape[1]), BF16), row(wg.shape[1])),
    ]
    return pl.pallas_call(
        _inproj_kernel,
        grid=(t // tm,),
        in_specs=[row(d), full(gain), full(wa), full(wqb), full(wkvt), full(wg), full(wgu), full(bgu)],
        out_specs=[spec for _, spec in outs],
        out_shape=[shape for shape, _ in outs],
        compiler_params=_cparams(("arbitrary",)),
        name="in_projection",
    )(x, gain, wa, wqb, wkvt, wg, wgu, bgu)


def _gla_chunk(q, k, v, b, st):
    c = q.shape[0]
    b_last = b[c - 1:c, :]
    rows = lax.broadcasted_iota(I32, (c, LANES), 0)
    nsub = c // GLA_SUB
    refs = [jnp.zeros((1, LANES), F32)] + [b[i * GLA_SUB - 1:i * GLA_SUB, :] for i in range(1, nsub)]
    ref_rows = refs[0]
    for i in range(1, nsub):
        ref_rows = jnp.where(rows >= i * GLA_SUB, refs[i], ref_rows)
    q_rel = q * jnp.exp(b - ref_rows)
    lhs = jnp.concatenate(
        [jnp.where((rows >= i * GLA_SUB) & (rows < (i + 1) * GLA_SUB), q_rel, 0.0) for i in range(nsub)],
        axis=1).astype(BF16)
    rhs = jnp.concatenate(
        [jnp.where(rows < (i + 1) * GLA_SUB, k * jnp.exp(jnp.minimum(refs[i] - b, GLA_EXP_CLAMP)), 0.0)
         for i in range(nsub)], axis=1).astype(BF16)
    att = _dot_nt(lhs, rhs)
    tt = lax.broadcasted_iota(I32, (c, c), 0)
    ss = lax.broadcasted_iota(I32, (c, c), 1)
    att = jnp.where(ss <= tt, att, 0.0)
    v16 = v.astype(BF16)
    inter = _dot_nt((q * jnp.exp(b)).astype(BF16), st.astype(BF16))
    intra = _dot(att.astype(BF16), v16)
    kd = (k * jnp.exp(b_last - b)).astype(BF16)
    st_new = st * jnp.exp(b_last) + _dot_tn(v16, kd)
    return inter + intra, st_new


def _gla_kernel(qa_ref, ka_ref, va_ref, ra_ref, la_ref, s0_ref, gain_ref, o_ref, sfin_ref, st_ref):
    j = pl.program_id(1)
    nj = pl.num_programs(1)
    nseq, rows_per_step, _ = qa_ref.shape
    c = GLA_CHUNK
    zpad = jnp.zeros((LANES - DK_A, DV_A), F32)

    def state_rows(h):
        return slice((h % 2) * DK_A, (h % 2 + 1) * DK_A)

    @pl.when(j == 0)
    def _():
        for si in range(nseq):
            for h in range(H_A):
                parts = [s0_ref[si, h], zpad] if h % 2 == 0 else [zpad, s0_ref[si, h]]
                st_ref[si * H_A + h] = jnp.concatenate(parts, axis=0).T

    ti = lax.broadcasted_iota(I32, (rows_per_step, rows_per_step), 0)
    si = lax.broadcasted_iota(I32, (rows_per_step, rows_per_step), 1)
    chunk_shift = c.bit_length() - 1
    same_chunk = (ti >> chunk_shift) == (si >> chunk_shift)
    tril_blocks = jnp.where(same_chunk & (si <= ti), 1.0, 0.0).astype(BF16)
    gain = gain_ref[...]
    lane = lax.broadcasted_iota(I32, (1, LANES), 1)
    half_masks = (lane < DK_A, lane >= DK_A)
    for si in range(nseq):
        la_hi, la_lo = _split_bf16(la_ref[si])
        b_all = _dot(tril_blocks, la_hi) + _dot(tril_blocks, la_lo)
        for h in range(H_A):
            hp = slice((h // 2) * LANES, (h // 2 + 1) * LANES)
            hv = slice(h * DV_A, (h + 1) * DV_A)
            mine = half_masks[h % 2]
            st = st_ref[si * H_A + h]
            for ci in range(rows_per_step // c):
                r0 = ci * c
                q = jnp.where(mine, qa_ref[si, r0:r0 + c, hp], 0.0) * (DK_A ** -0.5)
                k = jnp.where(mine, ka_ref[si, r0:r0 + c, hp], 0.0)
                o, st = _gla_chunk(q, k, va_ref[si, r0:r0 + c, hv], b_all[r0:r0 + c, hp], st)
                r = ra_ref[si, r0:r0 + c, hv]
                o = _rms_norm(o, gain) * (r * _sigmoid(r))
                o_ref[si, r0:r0 + c, hv] = o.astype(o_ref.dtype)
            st_ref[si * H_A + h] = st

    @pl.when(j == nj - 1)
    def _():
        for si in range(nseq):
            for h in range(H_A):
                sfin_ref[si, h] = st_ref[si * H_A + h].T[state_rows(h), :]


def _gla(qa, ka, va, ra, la, s0, gain, rows_per_step):
    assert 2 * DK_A == LANES and H_A % 2 == 0
    b, s, pa = qa.shape
    mw = va.shape[-1]
    ns = GLA_SEQS
    assert b % ns == 0
    seq = lambda w: pl.BlockSpec((ns, rows_per_step, w), lambda i, j: (i, j, 0))
    state = pl.BlockSpec((ns, H_A, DK_A, DV_A), lambda i, j: (i, 0, 0, 0))
    return pl.pallas_call(
        _gla_kernel,
        grid=(b // ns, s // rows_per_step),
        in_specs=[seq(pa), seq(pa), seq(mw), seq(mw), seq(pa), state,
                  pl.BlockSpec(gain.shape, lambda i, j: (0, 0))],
        out_specs=[seq(mw), state],
        out_shape=[jax.ShapeDtypeStruct((b, s, mw), BF16),
                   jax.ShapeDtypeStruct((b, H_A, DK_A, DV_A), F32)],
        scratch_shapes=[pltpu.VMEM((ns * H_A, LANES, LANES), F32)],
        compiler_params=_cparams(("arbitrary", "arbitrary")),
        name="gla",
    )(qa, ka, va, ra, la, s0, gain)


def _head_lane_masks():
    lane = lax.broadcasted_iota(I32, (1, LANES), 1)
    return lane < DH_B, lane >= DH_B


def _sb_neg_tri(tk):
    ji = lax.broadcasted_iota(I32, (tk, tk), 0)
    si = lax.broadcasted_iota(I32, (tk, tk), 1)
    return jnp.where(ji >= si, -1.0, 0.0).astype(BF16)


def _sb_stack_queries(q, qs_ref, base=0):
    m0, m1 = _head_lane_masks()
    for p in range(q.shape[1] // LANES):
        qp = (q[:, p * LANES:(p + 1) * LANES].astype(F32) * (DH_B ** -0.5 * LOG2_E)).astype(BF16)
        zero = jnp.zeros_like(qp)
        qs_ref[base + p] = jnp.concatenate([jnp.where(m0, qp, zero), jnp.where(m1, qp, zero)], axis=0)


def _pair_lanes(p):
    return slice(p * LANES, (p + 1) * LANES)


def _lane_fit(x, width):
    if width >= LANES:
        return jnp.concatenate([x] * (width // LANES), axis=1)
    return x[:, 0:width]


def _sb_tile_step(qs_ref, acc_ref, carry_ref, k_tile, v_tile, ntri, diagonal, one_suffix_matmul=False):
    npair, rows, _ = qs_ref.shape
    tq = rows // 2
    tk = ntri.shape[1]
    m0, _ = _head_lane_masks()
    if diagonal:
        t = lax.broadcasted_iota(I32, (rows, tk), 0)
        t = jnp.where(t >= tq, t - tq, t)
        visible = lax.broadcasted_iota(I32, (rows, tk), 1) < t
    def scores(p):
        z = _dot(qs_ref[p], k_tile(p))
        sp = jnp.maximum(z, 0.0) + jnp.log2(1.0 + jnp.exp2(-jnp.abs(z)))
        if diagonal:
            sp = jnp.where(visible, sp, 0.0)
        return z, sp.astype(BF16)

    if one_suffix_matmul:
        zs, sps = zip(*[scores(p) for p in range(npair)])
        stacked = _dot(jnp.concatenate(sps, axis=0), ntri)
        suffixes = [stacked[p * rows:(p + 1) * rows] for p in range(npair)]
    for p in range(npair):
        if one_suffix_matmul:
            z, suffix = zs[p], suffixes[p]
        else:
            z, sp = scores(p)
            suffix = _dot(sp, ntri)
        carry = carry_ref[p]
        w = jnp.exp2(z + suffix + _lane_fit(carry, tk))
        if diagonal:
            w = jnp.where(visible, w, 0.0)
        pv = _dot_nt(w.astype(BF16), v_tile(p))
        acc_ref[p] += jnp.where(m0, pv[0:tq], pv[tq:rows])
        carry_ref[p] = carry + jnp.broadcast_to(suffix[:, 0:1], carry.shape)


def _sb_prompt_kernel(q_ref, k_ref, v_ref, o_ref, qs_ref, acc_ref, carry_ref):
    qi = pl.program_id(1)
    tk = SB_TILE
    _sb_stack_queries(q_ref[0], qs_ref)
    acc_ref[...] = jnp.zeros_like(acc_ref)
    carry_ref[...] = jnp.zeros_like(carry_ref)
    ntri = _sb_neg_tri(tk)

    def step(jb, diagonal):
        _sb_tile_step(qs_ref, acc_ref, carry_ref, lambda p: k_ref[0, jb, _pair_lanes(p), :],
                      lambda p: v_ref[0, jb, _pair_lanes(p), :], ntri, diagonal, True)

    step(qi, True)

    def body(i, c):
        step(qi - 1 - 2 * i, False)
        step(qi - 2 - 2 * i, False)
        return c

    lax.fori_loop(0, qi // 2, body, 0)

    @pl.when(qi % 2 == 1)
    def _():
        step(0, False)

    for p in range(acc_ref.shape[0]):
        o_ref[0, :, p * LANES:(p + 1) * LANES] = acc_ref[p].astype(o_ref.dtype)


def _sb_scratch(tq, npair):
    return [pltpu.VMEM((npair, 2 * tq, LANES), BF16), pltpu.VMEM((npair, tq, LANES), F32),
            pltpu.VMEM((npair, 2 * tq, LANES), F32)]


def _sb_prompt(q, kt, vt):
    b, s, w = q.shape
    tq = SB_TILE
    assert kt.shape == (b, s // tq, w, tq)
    qspec = pl.BlockSpec((1, tq, w), lambda i, j: (i, j, 0))
    kvspec = pl.BlockSpec((1,) + kt.shape[1:], lambda i, j: (i, 0, 0, 0))
    return pl.pallas_call(
        _sb_prompt_kernel,
        grid=(b, s // tq),
        in_specs=[qspec, kvspec, kvspec],
        out_specs=qspec,
        out_shape=jax.ShapeDtypeStruct((b, s, w), BF16),
        scratch_shapes=_sb_scratch(tq, w // LANES),
        compiler_params=_cparams(("arbitrary", "arbitrary")),
        name="sb_prompt",
    )(q, kt, vt)


def _sb_sample_kernel(q_ref, kn_ref, vn_ref, kp_ref, vp_ref, o_ref, qs_ref, acc_ref, carry_ref):
    nseq, sq, w = q_ref.shape
    past = kp_ref.shape[3]
    npair = w // LANES
    tk = SB_TILE
    for si in range(nseq):
        _sb_stack_queries(q_ref[si], qs_ref, si * npair)
    acc_ref[...] = jnp.zeros_like(acc_ref)
    carry_ref[...] = jnp.zeros_like(carry_ref)
    _sb_tile_step(qs_ref, acc_ref, carry_ref, lambda e: kn_ref[e // npair, 0, _pair_lanes(e % npair), :],
                  lambda e: vn_ref[e // npair, 0, _pair_lanes(e % npair), :], _sb_neg_tri(sq), True, True)
    ntri = _sb_neg_tri(tk)

    def tile(j):
        cols = pl.ds(pl.multiple_of(past - (j + 1) * tk, tk), tk)

        def pair(ref, e):
            p = e % npair
            return ref[e // npair, 2 * p:2 * p + 2, :, cols].reshape(LANES, tk).astype(BF16)

        _sb_tile_step(qs_ref, acc_ref, carry_ref, lambda e: pair(kp_ref, e), lambda e: pair(vp_ref, e), ntri, False,
                      True)

    def body(i, c):
        tile(2 * i)
        tile(2 * i + 1)
        return c

    lax.fori_loop(0, past // tk // 2, body, 0)
    for e in range(acc_ref.shape[0]):
        o_ref[e // npair, :, _pair_lanes(e % npair)] = acc_ref[e].astype(o_ref.dtype)


def _sb_sample(q, kt_new, vt_new, kt_past, vt_past):
    b, sq, w = q.shape
    past = kt_past.shape[3]
    ns = SB_SAMPLE_SEQS
    assert past % (2 * SB_TILE) == 0 and 2 * DH_B == LANES and b % ns == 0
    qspec = pl.BlockSpec((ns, sq, w), lambda i: (i, 0, 0))
    new = pl.BlockSpec((ns, 1, w, sq), lambda i: (i, 0, 0, 0))
    old = pl.BlockSpec((ns, H_B, DH_B, past), lambda i: (i, 0, 0, 0))
    return pl.pallas_call(
        _sb_sample_kernel,
        grid=(b // ns,),
        in_specs=[qspec, new, new, old, old],
        out_specs=qspec,
        out_shape=jax.ShapeDtypeStruct((b, sq, w), BF16),
        scratch_shapes=_sb_scratch(sq, ns * (w // LANES)),
        compiler_params=_cparams(("arbitrary",)),
        name="sb_sample",
    )(q, kt_new, vt_new, kt_past, vt_past)


def _first_argmax(vals, nrows):
    idx = lax.broadcasted_iota(I32, vals.shape, 0)
    top = jnp.max(vals, axis=0, keepdims=True)
    first = jnp.min(jnp.where(vals == top, idx, nrows), axis=0, keepdims=True)
    return top, first, idx


def _merge_kernel(oa_ref, ob_ref, g_ref, x_ref, wb0_ref, wb1_ref, wo_ref, gain_ref, wr_ref, br_ref,
                  x1_ref, h2_ref, eid_ref, wcol_ref):
    d = x_ref.shape[1]
    ya = _dot(oa_ref[...], wb0_ref[...])
    yb = _dot(ob_ref[...], wb1_ref[...])
    g = g_ref[...].astype(F32)
    m = _sigmoid(g[:, 0:d]) * ya + _sigmoid(g[:, d:2 * d]) * yb
    x1 = x_ref[...] + _dot(m.astype(BF16), wo_ref[...])
    x1_ref[...] = x1
    h2 = _rms_norm(x1, gain_ref[...])
    h2_ref[...] = h2.astype(h2_ref.dtype).reshape(h2_ref.shape)

    h_hi, h_lo = _split_bf16(h2)
    w_hi, w_lo = _split_bf16(wr_ref[...])
    lt = _dot_nt(w_hi, h_hi) + _dot_nt(w_hi, h_lo) + _dot_nt(w_lo, h_hi) + br_ref[:, 0:1]
    gl = lt[0:N_GROUPS, :]
    g_top, g_idx, _ = _first_argmax(gl, N_GROUPS)
    g_e = jnp.exp(gl - g_top)
    g_p = jnp.max(g_e / jnp.sum(g_e, axis=0, keepdims=True), axis=0, keepdims=True)
    el = jnp.zeros((EXPERTS_PER_GROUP, lt.shape[1]), F32)
    for g in range(N_GROUPS):
        r0 = 8 + g * EXPERTS_PER_GROUP
        el = jnp.where(g_idx == g, lt[r0:r0 + EXPERTS_PER_GROUP, :], el)
    e_top, i1, eidx = _first_argmax(el, EXPERTS_PER_GROUP)
    e_e = jnp.exp(el - e_top)
    e_p = e_e / jnp.sum(e_e, axis=0, keepdims=True)
    p1 = jnp.max(e_p, axis=0, keepdims=True)
    rest = jnp.where(eidx == i1, -1.0, e_p)
    p2, i2, _ = _first_argmax(rest, EXPERTS_PER_GROUP)
    norm = p1 + p2
    w1 = g_p * (p1 / norm)
    w2 = g_p * (p2 / norm)
    eid_ref[...] = jnp.concatenate([g_idx * EXPERTS_PER_GROUP + i1, g_idx * EXPERTS_PER_GROUP + i2], axis=0)
    rows = lax.broadcasted_iota(I32, (LANES, lt.shape[1]), 0)
    wrows = jnp.where(rows == 0, w1, jnp.where(rows == 1, w2, 0.0))
    wcol_ref[...] = wrows.T


def _merge(oa, ob, gbr, x, wb0, wb1, wo, gain, wr, br):
    t, d = x.shape
    tm = MERGE_TILE
    assert t % tm == 0
    row = lambda w: pl.BlockSpec((tm, w), lambda i: (i, 0))
    full = lambda a: pl.BlockSpec(a.shape, lambda i: (0,) * a.ndim)
    return pl.pallas_call(
        _merge_kernel,
        grid=(t // tm,),
        in_specs=[row(oa.shape[1]), row(ob.shape[1]), row(gbr.shape[1]), row(d),
                  full(wb0), full(wb1), full(wo), full(gain), full(wr), full(br)],
        out_specs=[row(d), pl.BlockSpec((tm, d // LANES, LANES), lambda i: (i, 0, 0)),
                   pl.BlockSpec((TOP_K, tm), lambda i: (0, i)), row(LANES)],
        out_shape=[jax.ShapeDtypeStruct((t, d), F32), jax.ShapeDtypeStruct((t, d // LANES, LANES), BF16),
                   jax.ShapeDtypeStruct((TOP_K, t), I32), jax.ShapeDtypeStruct((t, LANES), F32)],
        compiler_params=_cparams(("arbitrary",)),
        name="merge_router",
    )(oa, ob, gbr, x, wb0, wb1, wo, gain, wr, br)


def _positions_kernel(eid_ref, dest_ref, counts_ref, rank_ref):
    nblk, width = eid_ref.shape
    ji = lax.broadcasted_iota(I32, (width, width), 0)
    si = lax.broadcasted_iota(I32, (width, width), 1)
    prefix = jnp.where(ji <= si, 1.0, 0.0).astype(BF16)
    expert = lax.broadcasted_iota(I32, (N_EXPERTS, width), 0)
    group = max(g for g in (8, 4, 2, 1) if nblk % g == 0)

    def onehot(i):
        return expert == eid_ref[pl.ds(i, 1), :]

    def rank_body(ig, run):
        first = pl.multiple_of(ig * group, group)
        ohs = [onehot(first + j) for j in range(group)]
        stacked = jnp.concatenate([jnp.where(oh, 1.0, 0.0) for oh in ohs], axis=0).astype(BF16)
        cum = _dot(stacked, prefix)
        ranks = []
        for j, oh in enumerate(ohs):
            cum_j = cum[j * N_EXPERTS:(j + 1) * N_EXPERTS, :] + run
            ranks.append(jnp.sum(jnp.where(oh, cum_j, 0.0), axis=0, keepdims=True) - 1.0)
            run = cum_j[:, width - 1:width]
        rank_ref[pl.ds(first, group), :] = jnp.concatenate(ranks, axis=0)
        return run

    counts = lax.fori_loop(0, nblk // group, rank_body, jnp.zeros((N_EXPERTS, 1), F32))
    counts_ref[...] = jnp.broadcast_to(counts, counts_ref.shape).astype(I32)
    c_hi = jnp.floor(counts * (1.0 / 256.0))
    c_lo = counts - 256.0 * c_hi
    ei = lax.broadcasted_iota(I32, (N_EXPERTS, N_EXPERTS), 0)
    ej = lax.broadcasted_iota(I32, (N_EXPERTS, N_EXPERTS), 1)
    strict = jnp.where(ej < ei, 1.0, 0.0).astype(BF16)
    digits = jnp.concatenate([jnp.broadcast_to(c_hi, (N_EXPERTS, LANES)),
                              jnp.broadcast_to(c_lo, (N_EXPERTS, LANES))], axis=1).astype(BF16)
    sums = _dot(strict, digits)
    start = 256.0 * sums[:, 0:1] + sums[:, LANES:LANES + 1]

    def dest_body(ig, carry):
        first = pl.multiple_of(ig * group, group)
        offs = [jnp.sum(jnp.where(onehot(first + j), start, 0.0), axis=0, keepdims=True) for j in range(group)]
        rows = pl.ds(first, group)
        dest_ref[rows, :] = (rank_ref[rows, :] + jnp.concatenate(offs, axis=0)).astype(I32)
        return carry

    lax.fori_loop(0, nblk // group, dest_body, 0)


def _positions(eid_blocks):
    nblk, width = eid_blocks.shape
    vm = lambda shape: pl.BlockSpec(shape, lambda: (0,) * len(shape))
    return pl.pallas_call(
        _positions_kernel,
        in_specs=[vm((nblk, width))],
        out_specs=[vm((nblk, width)), vm((N_EXPERTS, LANES))],
        out_shape=[jax.ShapeDtypeStruct((nblk, width), I32), jax.ShapeDtypeStruct((N_EXPERTS, LANES), I32)],
        scratch_shapes=[pltpu.VMEM((nblk, width), F32)],
        name="positions",
    )(eid_blocks)


def _dispatch_kernel(n_prompt_tiles, dest_ref, hp_ref, hs_ref, xs_ref, sem):
    i = pl.program_id(0)
    tm = dest_ref.shape[1]

    def scatter(src_ref):
        def start(r, c):
            for k in range(TOP_K):
                pltpu.make_async_copy(src_ref.at[r], xs_ref.at[dest_ref[k, r]], sem).start(priority=k)
            return c

        lax.fori_loop(0, tm, start, 0, unroll=DMA_UNROLL)
        for k in range(TOP_K):
            pltpu.make_async_copy(src_ref, xs_ref.at[pl.ds(0, tm)], sem).wait()

    @pl.when(i < n_prompt_tiles)
    def _():
        scatter(hp_ref)

    @pl.when(i >= n_prompt_tiles)
    def _():
        scatter(hs_ref)


def _dispatch(dest, h_prompt, h_sample):
    t = dest.shape[1]
    slab = h_prompt.shape[1:]
    tm = DISPATCH_TILE
    assert h_prompt.shape[0] % tm == 0 and h_sample.shape[0] % tm == 0
    npt = h_prompt.shape[0] // tm
    return pl.pallas_call(
        functools.partial(_dispatch_kernel, npt),
        grid=(t // tm,),
        in_specs=[pl.BlockSpec((TOP_K, tm), lambda i: (0, i), memory_space=pltpu.SMEM),
                  pl.BlockSpec((tm,) + slab, lambda i: (jnp.minimum(i, npt - 1), 0, 0)),
                  pl.BlockSpec((tm,) + slab, lambda i: (jnp.maximum(i - npt, 0), 0, 0))],
        out_specs=pl.BlockSpec(memory_space=pl.ANY),
        out_shape=jax.ShapeDtypeStruct((TOP_K * t,) + slab, h_prompt.dtype),
        scratch_shapes=[pltpu.SemaphoreType.DMA(())],
        compiler_params=_cparams(("arbitrary",)),
        name="dispatch",
    )(dest, h_prompt, h_sample)


def _experts_kernel(vblk_ref, vexp_ref, vlo_ref, vhi_ref, vnext_ref, vslot_ref, xs_ref, wg_ref, wu_ref, wd_ref,
                    ys_ref, wg32_ref, wu32_ref, wd32_ref, wg16_ref, wu16_ref, wd16_ref, sems):
    v = pl.program_id(0)
    lo = vlo_ref[v]
    hi = vhi_ref[v]
    prev = jnp.maximum(v - 1, 0)
    first = jnp.logical_or(v == 0, vblk_ref[v] != vblk_ref[prev])
    new_expert = jnp.logical_or(v == 0, vexp_ref[v] != vexp_ref[prev])

    def weight_copies(e, slot):
        return [pltpu.make_async_copy(src.at[e], dst.at[slot], sems.at[slot])
                for src, dst in ((wg_ref, wg32_ref), (wu_ref, wu32_ref), (wd_ref, wd32_ref))]

    @pl.when(v == 0)
    def _():
        for cp in weight_copies(vexp_ref[0], 0):
            cp.start()

    @pl.when(new_expert)
    def _():
        slot = vslot_ref[v]
        for cp in weight_copies(vexp_ref[v], slot):
            cp.wait()
        wg16_ref[...] = wg32_ref[slot].astype(BF16)
        wu16_ref[...] = wu32_ref[slot].astype(BF16)
        wd16_ref[...] = wd32_ref[slot].astype(BF16)

        @pl.when(vnext_ref[v] >= 0)
        def _():
            for cp in weight_copies(vnext_ref[v], 1 - slot):
                cp.start()

    @pl.when(first)
    def _():
        ys_ref[...] = jnp.zeros_like(ys_ref)

    @pl.when(hi > lo)
    def _():
        tm = xs_ref.shape[0]
        d = wg_ref.shape[1]
        x = xs_ref[...].reshape(tm, d)
        gate = _dot(x, wg16_ref[...])
        up = _dot(x, wu16_ref[...])
        hid = (gate * _sigmoid(gate) * up).astype(BF16)
        y = _dot(hid, wd16_ref[...]).astype(ys_ref.dtype)
        rows = lax.broadcasted_iota(I32, y.shape, 0)
        mine = (rows >= lo) & (rows < hi)
        ys_ref[...] = jnp.where(mine, y, ys_ref[...].reshape(tm, d)).reshape(ys_ref.shape)


def _experts(plan, xs, wg, wu, wd):
    a = xs.shape[0]
    slab = xs.shape[1:]
    d, de = wg.shape[1:]
    tm = MOE_TILE
    block = lambda v, b, *_: (b[v], 0, 0)
    hbm = pl.BlockSpec(memory_space=pl.ANY)
    grid_spec = pltpu.PrefetchScalarGridSpec(
        num_scalar_prefetch=len(plan),
        grid=(plan[0].shape[0],),
        in_specs=[pl.BlockSpec((tm,) + slab, block), hbm, hbm, hbm],
        out_specs=pl.BlockSpec((tm,) + slab, block),
        scratch_shapes=[pltpu.VMEM((2, d, de), F32), pltpu.VMEM((2, d, de), F32), pltpu.VMEM((2, de, d), F32),
                        pltpu.VMEM((d, de), BF16), pltpu.VMEM((d, de), BF16), pltpu.VMEM((de, d), BF16),
                        pltpu.SemaphoreType.DMA((2,))],
    )
    return pl.pallas_call(
        _experts_kernel,
        grid_spec=grid_spec,
        out_shape=jax.ShapeDtypeStruct((a,) + slab, MOE_OUT_DTYPE),
        compiler_params=_cparams(("arbitrary",)),
        name="experts",
    )(*plan, xs, wg, wu, wd)


def _visit_plan(counts, n_rows):
    tm = MOE_TILE
    nblk = n_rows // tm
    n_visits = nblk + N_EXPERTS - 1
    ends = jnp.cumsum(counts)
    starts = ends - counts
    first_blk = starts // tm
    nvis = jnp.where(counts > 0, (ends + tm - 1) // tm - first_blk, 0)
    vis_end = jnp.cumsum(nvis)
    vis_start = vis_end - nvis
    v = jnp.arange(n_visits, dtype=I32)
    e = jnp.minimum(jnp.sum((vis_end[None, :] <= v[:, None]).astype(I32), axis=1), N_EXPERTS - 1)
    valid = v < vis_end[-1]
    blk = first_blk[e] + (v - vis_start[e])
    lo = jnp.clip(starts[e] - blk * tm, 0, tm)
    hi = jnp.clip(ends[e] - blk * tm, 0, tm)
    ids = jnp.arange(N_EXPERTS, dtype=I32)
    used = counts > 0
    last_e = jnp.max(jnp.where(used, ids, 0))
    blk = jnp.where(valid, blk, nblk - 1).astype(I32)
    e = jnp.where(valid, e, last_e).astype(I32)
    lo = jnp.where(valid, lo, 0).astype(I32)
    hi = jnp.where(valid, hi, 0).astype(I32)
    later_used = used[None, :] & (ids[None, :] > ids[:, None])
    next_used = jnp.min(jnp.where(later_used, ids[None, :], N_EXPERTS), axis=1)
    next_used = jnp.where(next_used < N_EXPERTS, next_used, -1).astype(I32)
    slot = ((jnp.cumsum(used.astype(I32)) - 1) % 2).astype(I32)
    return blk, e, lo, hi, next_used[e], slot[e]


def _combine_kernel(dest_ref, ys_ref, x1_ref, wcol_ref, gain_ref, out_ref, buf_ref, sems):
    tm, d = x1_ref.shape
    part = tm // COMBINE_PARTS

    def start(r, c, sem):
        for k in range(TOP_K):
            pltpu.make_async_copy(ys_ref.at[dest_ref[k, r]], buf_ref.at[k, r], sem).start(priority=k)
        return c

    for h in range(COMBINE_PARTS):
        lax.fori_loop(h * part, (h + 1) * part, functools.partial(start, sem=sems.at[h]), 0, unroll=DMA_UNROLL)
    for h in range(COMBINE_PARTS):
        rows = pl.ds(h * part, part)
        for k in range(TOP_K):
            pltpu.make_async_copy(ys_ref.at[rows], buf_ref.at[k, rows], sems.at[h]).wait()
        y = (wcol_ref[rows, 0:1] * buf_ref[0, rows].reshape(part, d).astype(F32)
             + wcol_ref[rows, 1:2] * buf_ref[1, rows].reshape(part, d).astype(F32))
        out_ref[rows, :] = _rms_norm(x1_ref[rows, :] + y, gain_ref[...])


def _combine(dest, first_token, ys, x1, wcol, gain):
    t, d = x1.shape
    tm = COMBINE_TILE
    assert t % tm == 0 and first_token % tm == 0
    off = first_token // tm
    row = lambda w: pl.BlockSpec((tm, w), lambda i: (i, 0))
    return pl.pallas_call(
        _combine_kernel,
        grid=(t // tm,),
        in_specs=[pl.BlockSpec((TOP_K, tm), lambda i: (0, i + off), memory_space=pltpu.SMEM),
                  pl.BlockSpec(memory_space=pl.ANY), row(d), row(LANES),
                  pl.BlockSpec(gain.shape, lambda i: (0, 0))],
        out_specs=row(d),
        out_shape=jax.ShapeDtypeStruct((t, d), F32),
        scratch_shapes=[pltpu.VMEM((TOP_K, tm) + ys.shape[1:], ys.dtype), pltpu.SemaphoreType.DMA((COMBINE_PARTS,))],
        compiler_params=_cparams(("arbitrary",)),
        name="combine",
    )(dest, ys, x1, wcol, gain)


def _prepare_weights(w_in, w_gla_gate_up, b_gla_gate, w_branch, w_out, w_router_group, b_router_group,
                     w_router_expert, b_router_expert):
    d = w_in.shape[0]
    qk = H_A * DK_A
    mw = H_A * DV_A
    c = 0
    w_qa, c = w_in[:, c:c + qk], c + qk
    w_ka, c = w_in[:, c:c + qk], c + qk
    w_va, c = w_in[:, c:c + mw], c + mw
    w_ra, c = w_in[:, c:c + mw], c + mw
    w_lr, c = w_in[:, c:c + GATE_RANK], c + GATE_RANK
    w_b, c = w_in[:, c:c + 3 * mw], c + 3 * mw
    w_g = w_in[:, c:]
    wa = jnp.concatenate([w_qa, w_ka, w_va, w_ra,
                          jnp.pad(w_lr, ((0, 0), (0, LANES - GATE_RANK)))], axis=1).astype(BF16)
    wgu = jnp.pad(w_gla_gate_up, ((0, LANES - GATE_RANK), (0, 0))).astype(BF16)
    bgu = b_gla_gate[None, :]
    wr = jnp.zeros((LANES, d), F32)
    wr = wr.at[0:N_GROUPS].set(w_router_group.T).at[8:8 + N_EXPERTS].set(w_router_expert.T)
    br = jnp.zeros((LANES,), F32).at[0:N_GROUPS].set(b_router_group).at[8:8 + N_EXPERTS].set(b_router_expert)
    br = jnp.broadcast_to(br[:, None], (LANES, LANES))
    return dict(wa=wa, wqb=w_b[:, 0:mw].astype(BF16), wkvt=w_b[:, mw:3 * mw].T.astype(BF16),
                wg=w_g.astype(BF16), wgu=wgu, bgu=bgu,
                wb0=w_branch[0].astype(BF16), wb1=w_branch[1].astype(BF16), wo=w_out.astype(BF16),
                wr=wr, br=br)


def _mixers(x, s0, k_past, v_past, w, norm_mix_gain, gla_norm_gain, norm_ffn_gain):
    b, s, d = x.shape
    xf = x.reshape(b * s, d)
    qa, ka, va, ra, la, qb, kt, vt, kt16, vt16, gbr = _in_projection(
        xf, s, norm_mix_gain[None, :], w["wa"], w["wqb"], w["wkvt"], w["wg"], w["wgu"], w["bgu"])
    seq = lambda a: a.reshape(b, s, a.shape[-1])
    oa, s_new = _gla(seq(qa), seq(ka), seq(va), seq(ra), seq(la), s0, gla_norm_gain[None, :],
                     min(s, GLA_ROWS))
    to_channel_major = lambda a: jnp.transpose(a, (0, 2, 3, 1))
    if k_past is None:
        ob = _sb_prompt(seq(qb), kt16, vt16)
    else:
        ob = _sb_sample(seq(qb), kt16, vt16, to_channel_major(k_past), to_channel_major(v_past))
    x1, h2, eid, wcol = _merge(oa.reshape(b * s, -1), ob.reshape(b * s, -1), gbr, xf, w["wb0"], w["wb1"],
                               w["wo"], norm_ffn_gain[None, :], w["wr"], w["br"])
    from_channel_major = lambda a: jnp.transpose(a.reshape(b, H_B, DH_B, s), (0, 3, 1, 2))
    return x1, h2, eid, wcol, s_new, from_channel_major(kt), from_channel_major(vt)


def kernel(x_prompt, x_sample, state_gla, cache_sb_k, cache_sb_v, norm_mix_gain, w_in, w_gla_gate_up, b_gla_gate, gla_norm_gain, w_branch, w_out, norm_ffn_gain, w_router_group, b_router_group, w_router_expert, b_router_expert, w_exp_gate, w_exp_up, w_exp_down, norm_final_gain):
    depth = w_in.shape[0]
    assert depth == 1, "one trunk layer per step"
    l = 0
    w = _prepare_weights(w_in[l], w_gla_gate_up[l], b_gla_gate[l], w_branch[l], w_out[l], w_router_group[l],
                         b_router_group[l], w_router_expert[l], b_router_expert[l])
    bp, sp, d = x_prompt.shape
    bs, ss, _ = x_sample.shape
    s0 = jnp.zeros((bp, H_A, DK_A, DV_A), x_prompt.dtype)
    x1p, h2p, eidp, wcolp, gla_p, k_p, v_p = _mixers(
        x_prompt, s0, None, None, w, norm_mix_gain[l], gla_norm_gain[l], norm_ffn_gain[l])
    x1s, h2s, eids, wcols, gla_s, k_s, v_s = _mixers(
        x_sample, state_gla[l], cache_sb_k[l], cache_sb_v[l], w, norm_mix_gain[l], gla_norm_gain[l],
        norm_ffn_gain[l])

    tp, ts = bp * sp, bs * ss
    eid = jnp.concatenate([eidp, eids], axis=1)
    dest_blocks, counts = _positions(eid.reshape(-1, SORT_WIDTH))
    dest = dest_blocks.reshape(TOP_K, tp + ts)
    xs = _dispatch(dest, h2p, h2s)
    plan = _visit_plan(counts[:, 0], TOP_K * (tp + ts))
    ys = _experts(plan, xs, w_exp_gate[l], w_exp_up[l], w_exp_down[l])
    gf = norm_final_gain[None, :]
    y_prompt = _combine(dest, 0, ys, x1p, wcolp, gf).reshape(bp, sp, d)
    y_sample = _combine(dest, tp, ys, x1s, wcols, gf).reshape(bs, ss, d)
    return (y_prompt, y_sample, gla_p[None], k_p[None], v_p[None], gla_s[None], k_s[None], v_s[None])
```

```python
import functools

import jax
import jax.numpy as jnp
from jax import lax
from jax.experimental import pallas as pl
from jax.experimental.pallas import tpu as pltpu

F32 = jnp.float32
BF16 = jnp.bfloat16
MOE_OUT_DTYPE = jnp.bfloat16
I32 = jnp.int32

LANES = 128
LOG2_E = 1.4426950408889634
RMS_EPS = 1e-6
GATE_TAU = 16.0
H_A = 4
DK_A = 64
DV_A = 128
GATE_RANK = 16
H_B = 8
DH_B = 64
N_GROUPS = 4
EXPERTS_PER_GROUP = 8
N_EXPERTS = N_GROUPS * EXPERTS_PER_GROUP
TOP_K = 2
GLA_CHUNK = 64
GLA_SUB = 16
GLA_EXP_CLAMP = 80.0
GLA_SEQS = 4
GLA_ROWS = 256
SB_TILE = 256
SB_SAMPLE_SEQS = 2
MOE_TILE = 512
SORT_WIDTH = 256
INPROJ_TILE = 512
MERGE_TILE = 1024
DISPATCH_TILE = 2048
COMBINE_TILE = 512
COMBINE_PARTS = 4
DMA_UNROLL = 16
VMEM_LIMIT = 56 * 1024 * 1024


def _cparams(sem):
    return pltpu.CompilerParams(dimension_semantics=sem, vmem_limit_bytes=VMEM_LIMIT)


def _dot(a, b):
    return jnp.dot(a, b, preferred_element_type=F32)


def _dot_nt(a, b):
    return lax.dot_general(a, b, (((1,), (1,)), ((), ())), preferred_element_type=F32)


def _dot_tn(a, b):
    return lax.dot_general(a, b, (((0,), (0,)), ((), ())), preferred_element_type=F32)


def _split_bf16(x):
    hi = x.astype(BF16)
    lo = (x - hi.astype(F32)).astype(BF16)
    return hi, lo


def _log_sigmoid(x):
    return jnp.minimum(x, 0.0) - jnp.log(1.0 + jnp.exp(-jnp.abs(x)))


def _sigmoid(x):
    return 1.0 / (1.0 + jnp.exp(-x))


def _rms_norm(x, gain):
    return x * lax.rsqrt(jnp.mean(x * x, axis=-1, keepdims=True) + RMS_EPS) * gain


def _inproj_kernel(x_ref, gain_ref, wa_ref, wqb_ref, wkvt_ref, wg_ref, wgu_ref, bgu_ref,
                   qa_ref, ka_ref, va_ref, ra_ref, la_ref, qb_ref, kt_ref, vt_ref,
                   kt16_ref, vt16_ref, gbr_ref):
    h = _rms_norm(x_ref[...], gain_ref[...]).astype(BF16)
    pa = H_A * DK_A
    mw = va_ref.shape[-1]
    kvt = _dot_nt(wkvt_ref[...], h)
    nseq, _, s = kt_ref.shape
    ntile, tile = kt16_ref.shape[1], kt16_ref.shape[3]
    for i in range(nseq):
        kt_ref[i] = kvt[0:mw, i * s:(i + 1) * s]
        vt_ref[i] = kvt[mw:2 * mw, i * s:(i + 1) * s]
        for j in range(ntile):
            cols = slice(i * s + j * tile, i * s + (j + 1) * tile)
            kt16_ref[i, j] = kvt[0:mw, cols].astype(BF16)
            vt16_ref[i, j] = kvt[mw:2 * mw, cols].astype(BF16)
    qb_ref[...] = _dot(h, wqb_ref[...]).astype(BF16)
    qa_ref[...] = _dot(h, wa_ref[:, 0:pa])
    ka_ref[...] = _dot(h, wa_ref[:, pa:2 * pa])
    va_ref[...] = _dot(h, wa_ref[:, 2 * pa:2 * pa + mw])
    ra_ref[...] = _dot(h, wa_ref[:, 2 * pa + mw:2 * pa + 2 * mw])
    lr = _dot(h, wa_ref[:, 2 * pa + 2 * mw:2 * pa + 2 * mw + LANES])
    gl = _dot(lr.astype(BF16), wgu_ref[...]) + bgu_ref[...]
    la_ref[...] = _log_sigmoid(gl) / GATE_TAU
    gbr_ref[...] = _dot(h, wg_ref[...]).astype(gbr_ref.dtype)


def _in_projection(x, seq_len, gain, wa, wqb, wkvt, wg, wgu, bgu):
    t, d = x.shape
    nb = t // seq_len
    pa = H_A * DK_A
    mw = wqb.shape[1]
    tm = INPROJ_TILE
    assert t % tm == 0
    row = lambda w: pl.BlockSpec((tm, w), lambda i: (i, 0))
    full = lambda a: pl.BlockSpec(a.shape, lambda i: (0,) * a.ndim, pipeline_mode=pl.Buffered(1))
    if seq_len >= tm:
        per_seq = seq_len // tm
        ntile = tm // SB_TILE
        assert tm % SB_TILE == 0 and seq_len % tm == 0
        kt_spec = pl.BlockSpec((1, mw, tm), lambda i: (i // per_seq, 0, i % per_seq))
        kt16_spec = pl.BlockSpec((1, ntile, mw, SB_TILE), lambda i: (i // per_seq, i % per_seq, 0, 0))
        kt16_shape = (nb, seq_len // SB_TILE, mw, SB_TILE)
    else:
        nseq = tm // seq_len
        assert tm % seq_len == 0
        kt_spec = pl.BlockSpec((nseq, mw, seq_len), lambda i: (i, 0, 0))
        kt16_spec = pl.BlockSpec((nseq, 1, mw, seq_len), lambda i: (i, 0, 0, 0))
        kt16_shape = (nb, 1, mw, seq_len)
    outs = [
        (jax.ShapeDtypeStruct((t, pa), F32), row(pa)), (jax.ShapeDtypeStruct((t, pa), F32), row(pa)),
        (jax.ShapeDtypeStruct((t, mw), F32), row(mw)), (jax.ShapeDtypeStruct((t, mw), F32), row(mw)),
        (jax.ShapeDtypeStruct((t, pa), F32), row(pa)),
        (jax.ShapeDtypeStruct((t, mw), BF16), row(mw)),
        (jax.ShapeDtypeStruct((nb, mw, seq_len), F32), kt_spec), (jax.ShapeDtypeStruct((nb, mw, seq_len), F32), kt_spec),
        (jax.ShapeDtypeStruct(kt16_shape, BF16), kt16_spec), (jax.ShapeDtypeStruct(kt16_shape, BF16), kt16_spec),
        (jax.ShapeDtypeStruct((t, wg.shape[1]), BF16), row(wg.shape[1])),
    ]
    return pl.pallas_call(
        _inproj_kernel,
        grid=(t // tm,),
        in_specs=[row(d), full(gain), full(wa), full(wqb), full(wkvt), full(wg), full(wgu), full(bgu)],
        out_specs=[spec for _, spec in outs],
        out_shape=[shape for shape, _ in outs],
        compiler_params=_cparams(("arbitrary",)),
        name="in_projection",
    )(x, gain, wa, wqb, wkvt, wg, wgu, bgu)


def _gla_chunk(q, k, v, b, st):
    c = q.shape[0]
    b_last = b[c - 1:c, :]
    rows = lax.broadcasted_iota(I32, (c, LANES), 0)
    nsub = c // GLA_SUB
    refs = [jnp.zeros((1, LANES), F32)] + [b[i * GLA_SUB - 1:i * GLA_SUB, :] for i in range(1, nsub)]
    ref_rows = refs[0]
    for i in range(1, nsub):
        ref_rows = jnp.where(rows >= i * GLA_SUB, refs[i], ref_rows)
    q_rel = q * jnp.exp(b - ref_rows)
    lhs = jnp.concatenate(
        [jnp.where((rows >= i * GLA_SUB) & (rows < (i + 1) * GLA_SUB), q_rel, 0.0) for i in range(nsub)],
        axis=1).astype(BF16)
    rhs = jnp.concatenate(
        [jnp.where(rows < (i + 1) * GLA_SUB, k * jnp.exp(jnp.minimum(refs[i] - b, GLA_EXP_CLAMP)), 0.0)
         for i in range(nsub)], axis=1).astype(BF16)
    att = _dot_nt(lhs, rhs)
    tt = lax.broadcasted_iota(I32, (c, c), 0)
    ss = lax.broadcasted_iota(I32, (c, c), 1)
    att = jnp.where(ss <= tt, att, 0.0)
    v16 = v.astype(BF16)
    inter = _dot_nt((q * jnp.exp(b)).astype(BF16), st.astype(BF16))
    intra = _dot(att.astype(BF16), v16)
    kd = (k * jnp.exp(b_last - b)).astype(BF16)
    st_new = st * jnp.exp(b_last) + _dot_tn(v16, kd)
    return inter + intra, st_new


def _gla_kernel(qa_ref, ka_ref, va_ref, ra_ref, la_ref, s0_ref, gain_ref, o_ref, sfin_ref, st_ref):
    j = pl.program_id(1)
    nj = pl.num_programs(1)
    nseq, rows_per_step, _ = qa_ref.shape
    c = GLA_CHUNK
    zpad = jnp.zeros((LANES - DK_A, DV_A), F32)

    def state_rows(h):
        return slice((h % 2) * DK_A, (h % 2 + 1) * DK_A)

    @pl.when(j == 0)
    def _():
        for si in range(nseq):
            for h in range(H_A):
                parts = [s0_ref[si, h], zpad] if h % 2 == 0 else [zpad, s0_ref[si, h]]
                st_ref[si * H_A + h] = jnp.concatenate(parts, axis=0).T

    ti = lax.broadcasted_iota(I32, (rows_per_step, rows_per_step), 0)
    si = lax.broadcasted_iota(I32, (rows_per_step, rows_per_step), 1)
    chunk_shift = c.bit_length() - 1
    same_chunk = (ti >> chunk_shift) == (si >> chunk_shift)
    tril_blocks = jnp.where(same_chunk & (si <= ti), 1.0, 0.0).astype(BF16)
    gain = gain_ref[...]
    lane = lax.broadcasted_iota(I32, (1, LANES), 1)
    half_masks = (lane < DK_A, lane >= DK_A)
    for si in range(nseq):
        la_hi, la_lo = _split_bf16(la_ref[si])
        b_all = _dot(tril_blocks, la_hi) + _dot(tril_blocks, la_lo)
        for h in range(H_A):
            hp = slice((h // 2) * LANES, (h // 2 + 1) * LANES)
            hv = slice(h * DV_A, (h + 1) * DV_A)
            mine = half_masks[h % 2]
            st = st_ref[si * H_A + h]
            for ci in range(rows_per_step // c):
                r0 = ci * c
                q = jnp.where(mine, qa_ref[si, r0:r0 + c, hp], 0.0) * (DK_A ** -0.5)
                k = jnp.where(mine, ka_ref[si, r0:r0 + c, hp], 0.0)
                o, st = _gla_chunk(q, k, va_ref[si, r0:r0 + c, hv], b_all[r0:r0 + c, hp], st)
                r = ra_ref[si, r0:r0 + c, hv]
                o = _rms_norm(o, gain) * (r * _sigmoid(r))
                o_ref[si, r0:r0 + c, hv] = o.astype(o_ref.dtype)
            st_ref[si * H_A + h] = st

    @pl.when(j == nj - 1)
    def _():
        for si in range(nseq):
            for h in range(H_A):
                sfin_ref[si, h] = st_ref[si * H_A + h].T[state_rows(h), :]


def _gla(qa, ka, va, ra, la, s0, gain, rows_per_step):
    b, s, pa = qa.shape
    assert 2 * DK_A == LANES and H_A % 2 == 0 and DV_A == LANES
    assert s % rows_per_step == 0 and rows_per_step % GLA_CHUNK == 0 and GLA_CHUNK % GLA_SUB == 0
    mw = va.shape[-1]
    ns = GLA_SEQS
    assert b % ns == 0
    seq = lambda w: pl.BlockSpec((ns, rows_per_step, w), lambda i, j: (i, j, 0))
    state = pl.BlockSpec((ns, H_A, DK_A, DV_A), lambda i, j: (i, 0, 0, 0))
    return pl.pallas_call(
        _gla_kernel,
        grid=(b // ns, s // rows_per_step),
        in_specs=[seq(pa), seq(pa), seq(mw), seq(mw), seq(pa), state,
                  pl.BlockSpec(gain.shape, lambda i, j: (0, 0))],
        out_specs=[seq(mw), state],
        out_shape=[jax.ShapeDtypeStruct((b, s, mw), BF16),
                   jax.ShapeDtypeStruct((b, H_A, DK_A, DV_A), F32)],
        scratch_shapes=[pltpu.VMEM((ns * H_A, LANES, LANES), F32)],
        compiler_params=_cparams(("arbitrary", "arbitrary")),
        name="gla",
    )(qa, ka, va, ra, la, s0, gain)


def _head_lane_masks():
    lane = lax.broadcasted_iota(I32, (1, LANES), 1)
    return lane < DH_B, lane >= DH_B


def _sb_neg_tri(tk):
    ji = lax.broadcasted_iota(I32, (tk, tk), 0)
    si = lax.broadcasted_iota(I32, (tk, tk), 1)
    return jnp.where(ji >= si, -1.0, 0.0).astype(BF16)


def _sb_stack_queries(q, qs_ref, base=0):
    m0, m1 = _head_lane_masks()
    for p in range(q.shape[1] // LANES):
        qp = (q[:, p * LANES:(p + 1) * LANES].astype(F32) * (DH_B ** -0.5 * LOG2_E)).astype(BF16)
        zero = jnp.zeros_like(qp)
        qs_ref[base + p] = jnp.concatenate([jnp.where(m0, qp, zero), jnp.where(m1, qp, zero)], axis=0)


def _pair_lanes(p):
    return slice(p * LANES, (p + 1) * LANES)


def _lane_fit(x, width):
    if width >= LANES:
        return jnp.concatenate([x] * (width // LANES), axis=1)
    return x[:, 0:width]


def _sb_tile_step(qs_ref, acc_ref, carry_ref, k_tile, v_tile, ntri, diagonal, one_suffix_matmul=False):
    npair, rows, _ = qs_ref.shape
    tq = rows // 2
    tk = ntri.shape[1]
    m0, _ = _head_lane_masks()
    if diagonal:
        t = lax.broadcasted_iota(I32, (rows, tk), 0)
        t = jnp.where(t >= tq, t - tq, t)
        visible = lax.broadcasted_iota(I32, (rows, tk), 1) < t
    def scores(p):
        z = _dot(qs_ref[p], k_tile(p))
        sp = jnp.maximum(z, 0.0) + jnp.log2(1.0 + jnp.exp2(-jnp.abs(z)))
        if diagonal:
            sp = jnp.where(visible, sp, 0.0)
        return z, sp.astype(BF16)

    if one_suffix_matmul:
        zs, sps = zip(*[scores(p) for p in range(npair)])
        stacked = _dot(jnp.concatenate(sps, axis=0), ntri)
        suffixes = [stacked[p * rows:(p + 1) * rows] for p in range(npair)]
    for p in range(npair):
        if one_suffix_matmul:
            z, suffix = zs[p], suffixes[p]
        else:
            z, sp = scores(p)
            suffix = _dot(sp, ntri)
        carry = carry_ref[p]
        w = jnp.exp2(z + suffix + _lane_fit(carry, tk))
        if diagonal:
            w = jnp.where(visible, w, 0.0)
        pv = _dot_nt(w.astype(BF16), v_tile(p))
        acc_ref[p] += jnp.where(m0, pv[0:tq], pv[tq:rows])
        carry_ref[p] = carry + jnp.broadcast_to(suffix[:, 0:1], carry.shape)


def _sb_prompt_kernel(q_ref, k_ref, v_ref, o_ref, qs_ref, acc_ref, carry_ref):
    qi = pl.program_id(1)
    tk = SB_TILE
    _sb_stack_queries(q_ref[0], qs_ref)
    acc_ref[...] = jnp.zeros_like(acc_ref)
    carry_ref[...] = jnp.zeros_like(carry_ref)
    ntri = _sb_neg_tri(tk)

    def step(jb, diagonal):
        _sb_tile_step(qs_ref, acc_ref, carry_ref, lambda p: k_ref[0, jb, _pair_lanes(p), :],
                      lambda p: v_ref[0, jb, _pair_lanes(p), :], ntri, diagonal, True)

    step(qi, True)

    def body(i, c):
        step(qi - 1 - 2 * i, False)
        step(qi - 2 - 2 * i, False)
        return c

    lax.fori_loop(0, qi // 2, body, 0)

    @pl.when(qi % 2 == 1)
    def _():
        step(0, False)

    for p in range(acc_ref.shape[0]):
        o_ref[0, :, p * LANES:(p + 1) * LANES] = acc_ref[p].astype(o_ref.dtype)


def _sb_scratch(tq, npair):
    return [pltpu.VMEM((npair, 2 * tq, LANES), BF16), pltpu.VMEM((npair, tq, LANES), F32),
            pltpu.VMEM((npair, 2 * tq, LANES), F32)]


def _sb_prompt(q, kt, vt):
    b, s, w = q.shape
    tq = SB_TILE
    assert s % tq == 0 and w % LANES == 0 and 2 * DH_B == LANES and kt.shape == (b, s // tq, w, tq)
    qspec = pl.BlockSpec((1, tq, w), lambda i, j: (i, j, 0))
    kvspec = pl.BlockSpec((1,) + kt.shape[1:], lambda i, j: (i, 0, 0, 0))
    return pl.pallas_call(
        _sb_prompt_kernel,
        grid=(b, s // tq),
        in_specs=[qspec, kvspec, kvspec],
        out_specs=qspec,
        out_shape=jax.ShapeDtypeStruct((b, s, w), BF16),
        scratch_shapes=_sb_scratch(tq, w // LANES),
        compiler_params=_cparams(("arbitrary", "arbitrary")),
        name="sb_prompt",
    )(q, kt, vt)


def _sb_sample_kernel(q_ref, kn_ref, vn_ref, kp_ref, vp_ref, o_ref, qs_ref, acc_ref, carry_ref):
    nseq, sq, w = q_ref.shape
    past = kp_ref.shape[3]
    npair = w // LANES
    tk = SB_TILE
    for si in range(nseq):
        _sb_stack_queries(q_ref[si], qs_ref, si * npair)
    acc_ref[...] = jnp.zeros_like(acc_ref)
    carry_ref[...] = jnp.zeros_like(carry_ref)
    _sb_tile_step(qs_ref, acc_ref, carry_ref, lambda e: kn_ref[e // npair, 0, _pair_lanes(e % npair), :],
                  lambda e: vn_ref[e // npair, 0, _pair_lanes(e % npair), :], _sb_neg_tri(sq), True, True)
    ntri = _sb_neg_tri(tk)

    def tile(j):
        cols = pl.ds(pl.multiple_of(past - (j + 1) * tk, tk), tk)

        def pair(ref, e):
            p = e % npair
            return ref[e // npair, 2 * p:2 * p + 2, :, cols].reshape(LANES, tk).astype(BF16)

        _sb_tile_step(qs_ref, acc_ref, carry_ref, lambda e: pair(kp_ref, e), lambda e: pair(vp_ref, e), ntri, False,
                      True)

    def body(i, c):
        tile(2 * i)
        tile(2 * i + 1)
        return c

    lax.fori_loop(0, past // tk // 2, body, 0)
    for e in range(acc_ref.shape[0]):
        o_ref[e // npair, :, _pair_lanes(e % npair)] = acc_ref[e].astype(o_ref.dtype)


def _sb_sample(q, kt_new, vt_new, kt_past, vt_past):
    b, sq, w = q.shape
    past = kt_past.shape[3]
    ns = SB_SAMPLE_SEQS
    assert past % (2 * SB_TILE) == 0 and 2 * DH_B == LANES and b % ns == 0
    qspec = pl.BlockSpec((ns, sq, w), lambda i: (i, 0, 0))
    new = pl.BlockSpec((ns, 1, w, sq), lambda i: (i, 0, 0, 0))
    old = pl.BlockSpec((ns, H_B, DH_B, past), lambda i: (i, 0, 0, 0))
    return pl.pallas_call(
        _sb_sample_kernel,
        grid=(b // ns,),
        in_specs=[qspec, new, new, old, old],
        out_specs=qspec,
        out_shape=jax.ShapeDtypeStruct((b, sq, w), BF16),
        scratch_shapes=_sb_scratch(sq, ns * (w // LANES)),
        compiler_params=_cparams(("arbitrary",)),
        name="sb_sample",
    )(q, kt_new, vt_new, kt_past, vt_past)


def _first_argmax(vals, nrows):
    idx = lax.broadcasted_iota(I32, vals.shape, 0)
    top = jnp.max(vals, axis=0, keepdims=True)
    first = jnp.min(jnp.where(vals == top, idx, nrows), axis=0, keepdims=True)
    return top, first, idx


def _merge_kernel(oa_ref, ob_ref, g_ref, x_ref, wb0_ref, wb1_ref, wo_ref, gain_ref, wr_ref, br_ref,
                  x1_ref, h2_ref, eid_ref, wcol_ref):
    d = x_ref.shape[1]
    ya = _dot(oa_ref[...], wb0_ref[...])
    yb = _dot(ob_ref[...], wb1_ref[...])
    g = g_ref[...].astype(F32)
    m = _sigmoid(g[:, 0:d]) * ya + _sigmoid(g[:, d:2 * d]) * yb
    x1 = x_ref[...] + _dot(m.astype(BF16), wo_ref[...])
    x1_ref[...] = x1
    h2 = _rms_norm(x1, gain_ref[...])
    h2_ref[...] = h2.astype(h2_ref.dtype).reshape(h2_ref.shape)

    h_hi, h_lo = _split_bf16(h2)
    w_hi, w_lo = _split_bf16(wr_ref[...])
    lt = _dot_nt(w_hi, h_hi) + _dot_nt(w_hi, h_lo) + _dot_nt(w_lo, h_hi) + br_ref[:, 0:1]
    gl = lt[0:N_GROUPS, :]
    g_top, g_idx, _ = _first_argmax(gl, N_GROUPS)
    g_e = jnp.exp(gl - g_top)
    g_p = jnp.max(g_e / jnp.sum(g_e, axis=0, keepdims=True), axis=0, keepdims=True)
    el = jnp.zeros((EXPERTS_PER_GROUP, lt.shape[1]), F32)
    for g in range(N_GROUPS):
        r0 = 8 + g * EXPERTS_PER_GROUP
        el = jnp.where(g_idx == g, lt[r0:r0 + EXPERTS_PER_GROUP, :], el)
    e_top, i1, eidx = _first_argmax(el, EXPERTS_PER_GROUP)
    e_e = jnp.exp(el - e_top)
    e_p = e_e / jnp.sum(e_e, axis=0, keepdims=True)
    p1 = jnp.max(e_p, axis=0, keepdims=True)
    rest = jnp.where(eidx == i1, -1.0, e_p)
    p2, i2, _ = _first_argmax(rest, EXPERTS_PER_GROUP)
    norm = p1 + p2
    w1 = g_p * (p1 / norm)
    w2 = g_p * (p2 / norm)
    eid_ref[...] = jnp.concatenate([g_idx * EXPERTS_PER_GROUP + i1, g_idx * EXPERTS_PER_GROUP + i2], axis=0)
    rows = lax.broadcasted_iota(I32, (LANES, lt.shape[1]), 0)
    wrows = jnp.where(rows == 0, w1, jnp.where(rows == 1, w2, 0.0))
    wcol_ref[...] = wrows.T


def _merge(oa, ob, gbr, x, wb0, wb1, wo, gain, wr, br):
    t, d = x.shape
    tm = MERGE_TILE
    assert t % tm == 0
    row = lambda w: pl.BlockSpec((tm, w), lambda i: (i, 0))
    full = lambda a: pl.BlockSpec(a.shape, lambda i: (0,) * a.ndim)
    return pl.pallas_call(
        _merge_kernel,
        grid=(t // tm,),
        in_specs=[row(oa.shape[1]), row(ob.shape[1]), row(gbr.shape[1]), row(d),
                  full(wb0), full(wb1), full(wo), full(gain), full(wr), full(br)],
        out_specs=[row(d), pl.BlockSpec((tm, d // LANES, LANES), lambda i: (i, 0, 0)),
                   pl.BlockSpec((TOP_K, tm), lambda i: (0, i)), row(LANES)],
        out_shape=[jax.ShapeDtypeStruct((t, d), F32), jax.ShapeDtypeStruct((t, d // LANES, LANES), BF16),
                   jax.ShapeDtypeStruct((TOP_K, t), I32), jax.ShapeDtypeStruct((t, LANES), F32)],
        compiler_params=_cparams(("arbitrary",)),
        name="merge_router",
    )(oa, ob, gbr, x, wb0, wb1, wo, gain, wr, br)


def _positions_kernel(eid_ref, dest_ref, counts_ref, rank_ref):
    nblk, width = eid_ref.shape
    ji = lax.broadcasted_iota(I32, (width, width), 0)
    si = lax.broadcasted_iota(I32, (width, width), 1)
    prefix = jnp.where(ji <= si, 1.0, 0.0).astype(BF16)
    expert = lax.broadcasted_iota(I32, (N_EXPERTS, width), 0)
    group = max(g for g in (8, 4, 2, 1) if nblk % g == 0)

    def onehot(i):
        return expert == eid_ref[pl.ds(i, 1), :]

    def rank_body(ig, run):
        first = pl.multiple_of(ig * group, group)
        ohs = [onehot(first + j) for j in range(group)]
        stacked = jnp.concatenate([jnp.where(oh, 1.0, 0.0) for oh in ohs], axis=0).astype(BF16)
        cum = _dot(stacked, prefix)
        ranks = []
        for j, oh in enumerate(ohs):
            cum_j = cum[j * N_EXPERTS:(j + 1) * N_EXPERTS, :] + run
            ranks.append(jnp.sum(jnp.where(oh, cum_j, 0.0), axis=0, keepdims=True) - 1.0)
            run = cum_j[:, width - 1:width]
        rank_ref[pl.ds(first, group), :] = jnp.concatenate(ranks, axis=0)
        return run

    counts = lax.fori_loop(0, nblk // group, rank_body, jnp.zeros((N_EXPERTS, 1), F32))
    counts_ref[...] = jnp.broadcast_to(counts, counts_ref.shape).astype(I32)
    c_hi = jnp.floor(counts * (1.0 / 256.0))
    c_lo = counts - 256.0 * c_hi
    ei = lax.broadcasted_iota(I32, (N_EXPERTS, N_EXPERTS), 0)
    ej = lax.broadcasted_iota(I32, (N_EXPERTS, N_EXPERTS), 1)
    strict = jnp.where(ej < ei, 1.0, 0.0).astype(BF16)
    digits = jnp.concatenate([jnp.broadcast_to(c_hi, (N_EXPERTS, LANES)),
                              jnp.broadcast_to(c_lo, (N_EXPERTS, LANES))], axis=1).astype(BF16)
    sums = _dot(strict, digits)
    start = 256.0 * sums[:, 0:1] + sums[:, LANES:LANES + 1]

    def dest_body(ig, carry):
        first = pl.multiple_of(ig * group, group)
        offs = [jnp.sum(jnp.where(onehot(first + j), start, 0.0), axis=0, keepdims=True) for j in range(group)]
        rows = pl.ds(first, group)
        dest_ref[rows, :] = (rank_ref[rows, :] + jnp.concatenate(offs, axis=0)).astype(I32)
        return carry

    lax.fori_loop(0, nblk // group, dest_body, 0)


def _positions(eid_blocks):
    nblk, width = eid_blocks.shape
    vm = lambda shape: pl.BlockSpec(shape, lambda: (0,) * len(shape))
    return pl.pallas_call(
        _positions_kernel,
        in_specs=[vm((nblk, width))],
        out_specs=[vm((nblk, width)), vm((N_EXPERTS, LANES))],
        out_shape=[jax.ShapeDtypeStruct((nblk, width), I32), jax.ShapeDtypeStruct((N_EXPERTS, LANES), I32)],
        scratch_shapes=[pltpu.VMEM((nblk, width), F32)],
        name="positions",
    )(eid_blocks)


def _dispatch_kernel(n_prompt_tiles, dest_ref, hp_ref, hs_ref, xs_ref, sem):
    i = pl.program_id(0)
    tm = dest_ref.shape[1]

    def scatter(src_ref):
        def start(r, c):
            for k in range(TOP_K):
                pltpu.make_async_copy(src_ref.at[r], xs_ref.at[dest_ref[k, r]], sem).start(priority=k)
            return c

        lax.fori_loop(0, tm, start, 0, unroll=DMA_UNROLL)
        for k in range(TOP_K):
            pltpu.make_async_copy(src_ref, xs_ref.at[pl.ds(0, tm)], sem).wait()

    @pl.when(i < n_prompt_tiles)
    def _():
        scatter(hp_ref)

    @pl.when(i >= n_prompt_tiles)
    def _():
        scatter(hs_ref)


def _dispatch(dest, h_prompt, h_sample):
    t = dest.shape[1]
    slab = h_prompt.shape[1:]
    tm = DISPATCH_TILE
    assert h_prompt.shape[0] % tm == 0 and h_sample.shape[0] % tm == 0
    npt = h_prompt.shape[0] // tm
    return pl.pallas_call(
        functools.partial(_dispatch_kernel, npt),
        grid=(t // tm,),
        in_specs=[pl.BlockSpec((TOP_K, tm), lambda i: (0, i), memory_space=pltpu.SMEM),
                  pl.BlockSpec((tm,) + slab, lambda i: (jnp.minimum(i, npt - 1), 0, 0)),
                  pl.BlockSpec((tm,) + slab, lambda i: (jnp.maximum(i - npt, 0), 0, 0))],
        out_specs=pl.BlockSpec(memory_space=pl.ANY),
        out_shape=jax.ShapeDtypeStruct((TOP_K * t,) + slab, h_prompt.dtype),
        scratch_shapes=[pltpu.SemaphoreType.DMA(())],
        compiler_params=_cparams(("arbitrary",)),
        name="dispatch",
    )(dest, h_prompt, h_sample)


def _experts_kernel(vblk_ref, vexp_ref, vlo_ref, vhi_ref, vnext_ref, vslot_ref, xs_ref, wg_ref, wu_ref, wd_ref,
                    ys_ref, wg32_ref, wu32_ref, wd32_ref, wg16_ref, wu16_ref, wd16_ref, sems):
    v = pl.program_id(0)
    lo = vlo_ref[v]
    hi = vhi_ref[v]
    prev = jnp.maximum(v - 1, 0)
    first = jnp.logical_or(v == 0, vblk_ref[v] != vblk_ref[prev])
    new_expert = jnp.logical_or(v == 0, vexp_ref[v] != vexp_ref[prev])

    def weight_copies(e, slot):
        return [pltpu.make_async_copy(src.at[e], dst.at[slot], sems.at[slot])
                for src, dst in ((wg_ref, wg32_ref), (wu_ref, wu32_ref), (wd_ref, wd32_ref))]

    @pl.when(v == 0)
    def _():
        for cp in weight_copies(vexp_ref[0], 0):
            cp.start()

    @pl.when(new_expert)
    def _():
        slot = vslot_ref[v]
        for cp in weight_copies(vexp_ref[v], slot):
            cp.wait()
        wg16_ref[...] = wg32_ref[slot].astype(BF16)
        wu16_ref[...] = wu32_ref[slot].astype(BF16)
        wd16_ref[...] = wd32_ref[slot].astype(BF16)

        @pl.when(vnext_ref[v] >= 0)
        def _():
            for cp in weight_copies(vnext_ref[v], 1 - slot):
                cp.start()

    @pl.when(first)
    def _():
        ys_ref[...] = jnp.zeros_like(ys_ref)

    @pl.when(hi > lo)
    def _():
        tm = xs_ref.shape[0]
        d = wg_ref.shape[1]
        x = xs_ref[...].reshape(tm, d)
        gate = _dot(x, wg16_ref[...])
        up = _dot(x, wu16_ref[...])
        hid = (gate * _sigmoid(gate) * up).astype(BF16)
        y = _dot(hid, wd16_ref[...]).astype(ys_ref.dtype)
        rows = lax.broadcasted_iota(I32, y.shape, 0)
        mine = (rows >= lo) & (rows < hi)
        ys_ref[...] = jnp.where(mine, y, ys_ref[...].reshape(tm, d)).reshape(ys_ref.shape)


def _experts(plan, xs, wg, wu, wd):
    a = xs.shape[0]
    slab = xs.shape[1:]
    d, de = wg.shape[1:]
    tm = MOE_TILE
    assert a % tm == 0 and slab == (d // LANES, LANES) and plan[0].shape[0] == a // tm + N_EXPERTS - 1
    block = lambda v, b, *_: (b[v], 0, 0)
    hbm = pl.BlockSpec(memory_space=pl.ANY)
    grid_spec = pltpu.PrefetchScalarGridSpec(
        num_scalar_prefetch=len(plan),
        grid=(plan[0].shape[0],),
        in_specs=[pl.BlockSpec((tm,) + slab, block), hbm, hbm, hbm],
        out_specs=pl.BlockSpec((tm,) + slab, block),
        scratch_shapes=[pltpu.VMEM((2, d, de), F32), pltpu.VMEM((2, d, de), F32), pltpu.VMEM((2, de, d), F32),
                        pltpu.VMEM((d, de), BF16), pltpu.VMEM((d, de), BF16), pltpu.VMEM((de, d), BF16),
                        pltpu.SemaphoreType.DMA((2,))],
    )
    return pl.pallas_call(
        _experts_kernel,
        grid_spec=grid_spec,
        out_shape=jax.ShapeDtypeStruct((a,) + slab, MOE_OUT_DTYPE),
        compiler_params=_cparams(("arbitrary",)),
        name="experts",
    )(*plan, xs, wg, wu, wd)


def _visit_plan(counts, n_rows):
    tm = MOE_TILE
    nblk = n_rows // tm
    n_visits = nblk + N_EXPERTS - 1
    ends = jnp.cumsum(counts)
    starts = ends - counts
    first_blk = starts // tm
    nvis = jnp.where(counts > 0, (ends + tm - 1) // tm - first_blk, 0)
    vis_end = jnp.cumsum(nvis)
    vis_start = vis_end - nvis
    v = jnp.arange(n_visits, dtype=I32)
    e = jnp.minimum(jnp.sum((vis_end[None, :] <= v[:, None]).astype(I32), axis=1), N_EXPERTS - 1)
    valid = v < vis_end[-1]
    blk = first_blk[e] + (v - vis_start[e])
    lo = jnp.clip(starts[e] - blk * tm, 0, tm)
    hi = jnp.clip(ends[e] - blk * tm, 0, tm)
    ids = jnp.arange(N_EXPERTS, dtype=I32)
    used = counts > 0
    last_e = jnp.max(jnp.where(used, ids, 0))
    blk = jnp.where(valid, blk, nblk - 1).astype(I32)
    e = jnp.where(valid, e, last_e).astype(I32)
    lo = jnp.where(valid, lo, 0).astype(I32)
    hi = jnp.where(valid, hi, 0).astype(I32)
    later_used = used[None, :] & (ids[None, :] > ids[:, None])
    next_used = jnp.min(jnp.where(later_used, ids[None, :], N_EXPERTS), axis=1)
    next_used = jnp.where(next_used < N_EXPERTS, next_used, -1).astype(I32)
    slot = ((jnp.cumsum(used.astype(I32)) - 1) % 2).astype(I32)
    return blk, e, lo, hi, next_used[e], slot[e]


def _combine_kernel(dest_ref, ys_ref, x1_ref, wcol_ref, gain_ref, out_ref, buf_ref, sems):
    tm, d = x1_ref.shape
    part = tm // COMBINE_PARTS

    def start(r, c, sem):
        for k in range(TOP_K):
            pltpu.make_async_copy(ys_ref.at[dest_ref[k, r]], buf_ref.at[k, r], sem).start(priority=k)
        return c

    for h in range(COMBINE_PARTS):
        lax.fori_loop(h * part, (h + 1) * part, functools.partial(start, sem=sems.at[h]), 0, unroll=DMA_UNROLL)
    for h in range(COMBINE_PARTS):
        rows = pl.ds(h * part, part)
        for k in range(TOP_K):
            pltpu.make_async_copy(ys_ref.at[rows], buf_ref.at[k, rows], sems.at[h]).wait()
        y = (wcol_ref[rows, 0:1] * buf_ref[0, rows].reshape(part, d).astype(F32)
             + wcol_ref[rows, 1:2] * buf_ref[1, rows].reshape(part, d).astype(F32))
        out_ref[rows, :] = _rms_norm(x1_ref[rows, :] + y, gain_ref[...])


def _combine(dest, first_token, ys, x1, wcol, gain):
    t, d = x1.shape
    tm = COMBINE_TILE
    assert t % tm == 0 and first_token % tm == 0
    off = first_token // tm
    row = lambda w: pl.BlockSpec((tm, w), lambda i: (i, 0))
    return pl.pallas_call(
        _combine_kernel,
        grid=(t // tm,),
        in_specs=[pl.BlockSpec((TOP_K, tm), lambda i: (0, i + off), memory_space=pltpu.SMEM),
                  pl.BlockSpec(memory_space=pl.ANY), row(d), row(LANES),
                  pl.BlockSpec(gain.shape, lambda i: (0, 0))],
        out_specs=row(d),
        out_shape=jax.ShapeDtypeStruct((t, d), F32),
        scratch_shapes=[pltpu.VMEM((TOP_K, tm) + ys.shape[1:], ys.dtype), pltpu.SemaphoreType.DMA((COMBINE_PARTS,))],
        compiler_params=_cparams(("arbitrary",)),
        name="combine",
    )(dest, ys, x1, wcol, gain)


def _prepare_weights(w_in, w_gla_gate_up, b_gla_gate, w_branch, w_out, w_router_group, b_router_group,
                     w_router_expert, b_router_expert):
    d = w_in.shape[0]
    qk = H_A * DK_A
    mw = H_A * DV_A
    c = 0
    w_qa, c = w_in[:, c:c + qk], c + qk
    w_ka, c = w_in[:, c:c + qk], c + qk
    w_va, c = w_in[:, c:c + mw], c + mw
    w_ra, c = w_in[:, c:c + mw], c + mw
    w_lr, c = w_in[:, c:c + GATE_RANK], c + GATE_RANK
    w_b, c = w_in[:, c:c + 3 * mw], c + 3 * mw
    w_g = w_in[:, c:]
    wa = jnp.concatenate([w_qa, w_ka, w_va, w_ra,
                          jnp.pad(w_lr, ((0, 0), (0, LANES - GATE_RANK)))], axis=1).astype(BF16)
    wgu = jnp.pad(w_gla_gate_up, ((0, LANES - GATE_RANK), (0, 0))).astype(BF16)
    bgu = b_gla_gate[None, :]
    wr = jnp.zeros((LANES, d), F32)
    wr = wr.at[0:N_GROUPS].set(w_router_group.T).at[8:8 + N_EXPERTS].set(w_router_expert.T)
    br = jnp.zeros((LANES,), F32).at[0:N_GROUPS].set(b_router_group).at[8:8 + N_EXPERTS].set(b_router_expert)
    br = jnp.broadcast_to(br[:, None], (LANES, LANES))
    return dict(wa=wa, wqb=w_b[:, 0:mw].astype(BF16), wkvt=w_b[:, mw:3 * mw].T.astype(BF16),
                wg=w_g.astype(BF16), wgu=wgu, bgu=bgu,
                wb0=w_branch[0].astype(BF16), wb1=w_branch[1].astype(BF16), wo=w_out.astype(BF16),
                wr=wr, br=br)


def _mixers(x, s0, k_past, v_past, w, norm_mix_gain, gla_norm_gain, norm_ffn_gain):
    b, s, d = x.shape
    xf = x.reshape(b * s, d)
    qa, ka, va, ra, la, qb, kt, vt, kt16, vt16, gbr = _in_projection(
        xf, s, norm_mix_gain[None, :], w["wa"], w["wqb"], w["wkvt"], w["wg"], w["wgu"], w["bgu"])
    seq = lambda a: a.reshape(b, s, a.shape[-1])
    oa, s_new = _gla(seq(qa), seq(ka), seq(va), seq(ra), seq(la), s0, gla_norm_gain[None, :],
                     min(s, GLA_ROWS))
    to_channel_major = lambda a: jnp.transpose(a, (0, 2, 3, 1))
    if k_past is None:
        ob = _sb_prompt(seq(qb), kt16, vt16)
    else:
        ob = _sb_sample(seq(qb), kt16, vt16, to_channel_major(k_past), to_channel_major(v_past))
    x1, h2, eid, wcol = _merge(oa.reshape(b * s, -1), ob.reshape(b * s, -1), gbr, xf, w["wb0"], w["wb1"],
                               w["wo"], norm_ffn_gain[None, :], w["wr"], w["br"])
    from_channel_major = lambda a: jnp.transpose(a.reshape(b, H_B, DH_B, s), (0, 3, 1, 2))
    return x1, h2, eid, wcol, s_new, from_channel_major(kt), from_channel_major(vt)


def kernel(x_prompt, x_sample, state_gla, cache_sb_k, cache_sb_v, norm_mix_gain, w_in, w_gla_gate_up, b_gla_gate, gla_norm_gain, w_branch, w_out, norm_ffn_gain, w_router_group, b_router_group, w_router_expert, b_router_expert, w_exp_gate, w_exp_up, w_exp_down, norm_final_gain):
    depth = w_in.shape[0]
    assert depth == 1, "one trunk layer per step"
    l = 0
    w = _prepare_weights(w_in[l], w_gla_gate_up[l], b_gla_gate[l], w_branch[l], w_out[l], w_router_group[l],
                         b_router_group[l], w_router_expert[l], b_router_expert[l])
    bp, sp, d = x_prompt.shape
    bs, ss, _ = x_sample.shape
    s0 = jnp.zeros((bp, H_A, DK_A, DV_A), x_prompt.dtype)
    x1p, h2p, eidp, wcolp, gla_p, k_p, v_p = _mixers(
        x_prompt, s0, None, None, w, norm_mix_gain[l], gla_norm_gain[l], norm_ffn_gain[l])
    x1s, h2s, eids, wcols, gla_s, k_s, v_s = _mixers(
        x_sample, state_gla[l], cache_sb_k[l], cache_sb_v[l], w, norm_mix_gain[l], gla_norm_gain[l],
        norm_ffn_gain[l])

    tp, ts = bp * sp, bs * ss
    eid = jnp.concatenate([eidp, eids], axis=1)
    dest_blocks, counts = _positions(eid.reshape(-1, SORT_WIDTH))
    dest = dest_blocks.reshape(TOP_K, tp + ts)
    xs = _dispatch(dest, h2p, h2s)
    plan = _visit_plan(counts[:, 0], TOP_K * (tp + ts))
    ys = _experts(plan, xs, w_exp_gate[l], w_exp_up[l], w_exp_down[l])
    gf = norm_final_gain[None, :]
    y_prompt = _combine(dest, 0, ys, x1p, wcolp, gf).reshape(bp, sp, d)
    y_sample = _combine(dest, tp, ys, x1s, wcols, gf).reshape(bs, ss, d)
    return (y_prompt, y_sample, gla_p[None], k_p[None], v_p[None], gla_s[None], k_s[None], v_s[None])
```

```python
import functools

import jax
import jax.numpy as jnp
from jax import lax
from jax.experimental import pallas as pl
from jax.experimental.pallas import tpu as pltpu

F32 = jnp.float32
BF16 = jnp.bfloat16
MOE_OUT_DTYPE = jnp.bfloat16
I32 = jnp.int32

LANES = 128
LOG2_E = 1.4426950408889634
RMS_EPS = 1e-6
GATE_TAU = 16.0
H_A = 4
DK_A = 64
DV_A = 128
GATE_RANK = 16
H_B = 8
DH_B = 64
N_GROUPS = 4
EXPERTS_PER_GROUP = 8
N_EXPERTS = N_GROUPS * EXPERTS_PER_GROUP
TOP_K = 2
GLA_CHUNK = 64
GLA_SUB = 16
GLA_EXP_CLAMP = 80.0
GLA_SEQS = 4
GLA_ROWS = 256
SB_TILE = 256
SB_SAMPLE_SEQS = 2
SB_SAMPLE_TILES_PER_TRIP = 4
MOE_TILE = 512
SORT_WIDTH = 256
INPROJ_TILE = 512
MERGE_TILE = 1024
DISPATCH_TILE = 2048
COMBINE_TILE = 512
COMBINE_PARTS = 4
DMA_UNROLL = 16
VMEM_LIMIT = 56 * 1024 * 1024


def _cparams(sem):
    return pltpu.CompilerParams(dimension_semantics=sem, vmem_limit_bytes=VMEM_LIMIT)


def _dot(a, b):
    return jnp.dot(a, b, preferred_element_type=F32)


def _dot_nt(a, b):
    return lax.dot_general(a, b, (((1,), (1,)), ((), ())), preferred_element_type=F32)


def _dot_tn(a, b):
    return lax.dot_general(a, b, (((0,), (0,)), ((), ())), preferred_element_type=F32)


def _split_bf16(x):
    hi = x.astype(BF16)
    lo = (x - hi.astype(F32)).astype(BF16)
    return hi, lo


def _log_sigmoid(x):
    return jnp.minimum(x, 0.0) - jnp.log(1.0 + jnp.exp(-jnp.abs(x)))


def _sigmoid(x):
    return 1.0 / (1.0 + jnp.exp(-x))


def _rms_norm(x, gain):
    return x * lax.rsqrt(jnp.mean(x * x, axis=-1, keepdims=True) + RMS_EPS) * gain


def _inproj_kernel(x_ref, gain_ref, wa_ref, wqb_ref, wkvt_ref, wg_ref, wgu_ref, bgu_ref,
                   qa_ref, ka_ref, va_ref, ra_ref, la_ref, qb_ref, kt_ref, vt_ref,
                   kt16_ref, vt16_ref, gbr_ref):
    h = _rms_norm(x_ref[...], gain_ref[...]).astype(BF16)
    pa = H_A * DK_A
    mw = va_ref.shape[-1]
    kvt = _dot_nt(wkvt_ref[...], h)
    nseq, _, s = kt_ref.shape
    ntile, tile = kt16_ref.shape[1], kt16_ref.shape[3]
    for i in range(nseq):
        kt_ref[i] = kvt[0:mw, i * s:(i + 1) * s]
        vt_ref[i] = kvt[mw:2 * mw, i * s:(i + 1) * s]
        for j in range(ntile):
            cols = slice(i * s + j * tile, i * s + (j + 1) * tile)
            kt16_ref[i, j] = kvt[0:mw, cols].astype(BF16)
            vt16_ref[i, j] = kvt[mw:2 * mw, cols].astype(BF16)
    qb_ref[...] = _dot(h, wqb_ref[...]).astype(BF16)
    qa_ref[...] = _dot(h, wa_ref[:, 0:pa])
    ka_ref[...] = _dot(h, wa_ref[:, pa:2 * pa])
    va_ref[...] = _dot(h, wa_ref[:, 2 * pa:2 * pa + mw])
    ra_ref[...] = _dot(h, wa_ref[:, 2 * pa + mw:2 * pa + 2 * mw])
    lr = _dot(h, wa_ref[:, 2 * pa + 2 * mw:2 * pa + 2 * mw + LANES])
    gl = _dot(lr.astype(BF16), wgu_ref[...]) + bgu_ref[...]
    la_ref[...] = _log_sigmoid(gl) / GATE_TAU
    gbr_ref[...] = _dot(h, wg_ref[...]).astype(gbr_ref.dtype)


def _in_projection(x, seq_len, gain, wa, wqb, wkvt, wg, wgu, bgu):
    t, d = x.shape
    nb = t // seq_len
    pa = H_A * DK_A
    mw = wqb.shape[1]
    tm = INPROJ_TILE
    assert t % tm == 0
    row = lambda w: pl.BlockSpec((tm, w), lambda i: (i, 0))
    full = lambda a: pl.BlockSpec(a.shape, lambda i: (0,) * a.ndim, pipeline_mode=pl.Buffered(1))
    if seq_len >= tm:
        per_seq = seq_len // tm
        ntile = tm // SB_TILE
        assert tm % SB_TILE == 0 and seq_len % tm == 0
        kt_spec = pl.BlockSpec((1, mw, tm), lambda i: (i // per_seq, 0, i % per_seq))
        kt16_spec = pl.BlockSpec((1, ntile, mw, SB_TILE), lambda i: (i // per_seq, i % per_seq, 0, 0))
        kt16_shape = (nb, seq_len // SB_TILE, mw, SB_TILE)
    else:
        nseq = tm // seq_len
        assert tm % seq_len == 0
        kt_spec = pl.BlockSpec((nseq, mw, seq_len), lambda i: (i, 0, 0))
        kt16_spec = pl.BlockSpec((nseq, 1, mw, seq_len), lambda i: (i, 0, 0, 0))
        kt16_shape = (nb, 1, mw, seq_len)
    outs = [
        (jax.ShapeDtypeStruct((t, pa), F32), row(pa)), (jax.ShapeDtypeStruct((t, pa), F32), row(pa)),
        (jax.ShapeDtypeStruct((t, mw), F32), row(mw)), (jax.ShapeDtypeStruct((t, mw), F32), row(mw)),
        (jax.ShapeDtypeStruct((t, pa), F32), row(pa)),
        (jax.ShapeDtypeStruct((t, mw), BF16), row(mw)),
        (jax.ShapeDtypeStruct((nb, mw, seq_len), F32), kt_spec), (jax.ShapeDtypeStruct((nb, mw, seq_len), F32), kt_spec),
        (jax.ShapeDtypeStruct(kt16_shape, BF16), kt16_spec), (jax.ShapeDtypeStruct(kt16_shape, BF16), kt16_spec),
        (jax.ShapeDtypeStruct((t, wg.shape[1]), BF16), row(wg.shape[1])),
    ]
    return pl.pallas_call(
        _inproj_kernel,
        grid=(t // tm,),
        in_specs=[row(d), full(gain), full(wa), full(wqb), full(wkvt), full(wg), full(wgu), full(bgu)],
        out_specs=[spec for _, spec in outs],
        out_shape=[shape for shape, _ in outs],
        compiler_params=_cparams(("arbitrary",)),
        name="in_projection",
    )(x, gain, wa, wqb, wkvt, wg, wgu, bgu)


def _gla_chunk(q, k, v, b, st):
    c = q.shape[0]
    b_last = b[c - 1:c, :]
    rows = lax.broadcasted_iota(I32, (c, LANES), 0)
    nsub = c // GLA_SUB
    refs = [jnp.zeros((1, LANES), F32)] + [b[i * GLA_SUB - 1:i * GLA_SUB, :] for i in range(1, nsub)]
    ref_rows = refs[0]
    for i in range(1, nsub):
        ref_rows = jnp.where(rows >= i * GLA_SUB, refs[i], ref_rows)
    q_rel = q * jnp.exp(b - ref_rows)
    lhs = jnp.concatenate(
        [jnp.where((rows >= i * GLA_SUB) & (rows < (i + 1) * GLA_SUB), q_rel, 0.0) for i in range(nsub)],
        axis=1).astype(BF16)
    rhs = jnp.concatenate(
        [jnp.where(rows < (i + 1) * GLA_SUB, k * jnp.exp(jnp.minimum(refs[i] - b, GLA_EXP_CLAMP)), 0.0)
         for i in range(nsub)], axis=1).astype(BF16)
    att = _dot_nt(lhs, rhs)
    tt = lax.broadcasted_iota(I32, (c, c), 0)
    ss = lax.broadcasted_iota(I32, (c, c), 1)
    att = jnp.where(ss <= tt, att, 0.0)
    v16 = v.astype(BF16)
    inter = _dot_nt((q * jnp.exp(b)).astype(BF16), st.astype(BF16))
    intra = _dot(att.astype(BF16), v16)
    kd = (k * jnp.exp(b_last - b)).astype(BF16)
    st_new = st * jnp.exp(b_last) + _dot_tn(v16, kd)
    return inter + intra, st_new


def _gla_kernel(qa_ref, ka_ref, va_ref, ra_ref, la_ref, s0_ref, gain_ref, o_ref, sfin_ref, st_ref):
    j = pl.program_id(1)
    nj = pl.num_programs(1)
    nseq, rows_per_step, _ = qa_ref.shape
    c = GLA_CHUNK
    zpad = jnp.zeros((LANES - DK_A, DV_A), F32)

    def state_rows(h):
        return slice((h % 2) * DK_A, (h % 2 + 1) * DK_A)

    @pl.when(j == 0)
    def _():
        for si in range(nseq):
            for h in range(H_A):
                parts = [s0_ref[si, h], zpad] if h % 2 == 0 else [zpad, s0_ref[si, h]]
                st_ref[si * H_A + h] = jnp.concatenate(parts, axis=0).T

    ti = lax.broadcasted_iota(I32, (rows_per_step, rows_per_step), 0)
    si = lax.broadcasted_iota(I32, (rows_per_step, rows_per_step), 1)
    chunk_shift = c.bit_length() - 1
    same_chunk = (ti >> chunk_shift) == (si >> chunk_shift)
    tril_blocks = jnp.where(same_chunk & (si <= ti), 1.0, 0.0).astype(BF16)
    gain = gain_ref[...]
    lane = lax.broadcasted_iota(I32, (1, LANES), 1)
    half_masks = (lane < DK_A, lane >= DK_A)
    for si in range(nseq):
        la_hi, la_lo = _split_bf16(la_ref[si])
        b_all = _dot(tril_blocks, la_hi) + _dot(tril_blocks, la_lo)
        for h in range(H_A):
            hp = slice((h // 2) * LANES, (h // 2 + 1) * LANES)
            hv = slice(h * DV_A, (h + 1) * DV_A)
            mine = half_masks[h % 2]
            st = st_ref[si * H_A + h]
            for ci in range(rows_per_step // c):
                r0 = ci * c
                q = jnp.where(mine, qa_ref[si, r0:r0 + c, hp], 0.0) * (DK_A ** -0.5)
                k = jnp.where(mine, ka_ref[si, r0:r0 + c, hp], 0.0)
                o, st = _gla_chunk(q, k, va_ref[si, r0:r0 + c, hv], b_all[r0:r0 + c, hp], st)
                r = ra_ref[si, r0:r0 + c, hv]
                o = _rms_norm(o, gain) * (r * _sigmoid(r))
                o_ref[si, r0:r0 + c, hv] = o.astype(o_ref.dtype)
            st_ref[si * H_A + h] = st

    @pl.when(j == nj - 1)
    def _():
        for si in range(nseq):
            for h in range(H_A):
                sfin_ref[si, h] = st_ref[si * H_A + h].T[state_rows(h), :]


def _gla(qa, ka, va, ra, la, s0, gain, rows_per_step):
    b, s, pa = qa.shape
    assert 2 * DK_A == LANES and H_A % 2 == 0 and DV_A == LANES
    assert s % rows_per_step == 0 and rows_per_step % GLA_CHUNK == 0 and GLA_CHUNK % GLA_SUB == 0
    mw = va.shape[-1]
    ns = GLA_SEQS
    assert b % ns == 0
    seq = lambda w: pl.BlockSpec((ns, rows_per_step, w), lambda i, j: (i, j, 0))
    state = pl.BlockSpec((ns, H_A, DK_A, DV_A), lambda i, j: (i, 0, 0, 0))
    return pl.pallas_call(
        _gla_kernel,
        grid=(b // ns, s // rows_per_step),
        in_specs=[seq(pa), seq(pa), seq(mw), seq(mw), seq(pa), state,
                  pl.BlockSpec(gain.shape, lambda i, j: (0, 0))],
        out_specs=[seq(mw), state],
        out_shape=[jax.ShapeDtypeStruct((b, s, mw), BF16),
                   jax.ShapeDtypeStruct((b, H_A, DK_A, DV_A), F32)],
        scratch_shapes=[pltpu.VMEM((ns * H_A, LANES, LANES), F32)],
        compiler_params=_cparams(("arbitrary", "arbitrary")),
        name="gla",
    )(qa, ka, va, ra, la, s0, gain)


def _head_lane_masks():
    lane = lax.broadcasted_iota(I32, (1, LANES), 1)
    return lane < DH_B, lane >= DH_B


def _sb_neg_tri(tk):
    ji = lax.broadcasted_iota(I32, (tk, tk), 0)
    si = lax.broadcasted_iota(I32, (tk, tk), 1)
    return jnp.where(ji >= si, -1.0, 0.0).astype(BF16)


def _sb_stack_queries(q, qs_ref, base=0):
    m0, m1 = _head_lane_masks()
    for p in range(q.shape[1] // LANES):
        qp = (q[:, p * LANES:(p + 1) * LANES].astype(F32) * (DH_B ** -0.5 * LOG2_E)).astype(BF16)
        zero = jnp.zeros_like(qp)
        qs_ref[base + p] = jnp.concatenate([jnp.where(m0, qp, zero), jnp.where(m1, qp, zero)], axis=0)


def _pair_lanes(p):
    return slice(p * LANES, (p + 1) * LANES)


def _lane_fit(x, width):
    if width >= LANES:
        return jnp.concatenate([x] * (width // LANES), axis=1)
    return x[:, 0:width]


def _sb_tile_step(qs_ref, acc_ref, carry_ref, k_tile, v_tile, ntri, diagonal, one_suffix_matmul=False):
    npair, rows, _ = qs_ref.shape
    tq = rows // 2
    tk = ntri.shape[1]
    m0, _ = _head_lane_masks()
    if diagonal:
        t = lax.broadcasted_iota(I32, (rows, tk), 0)
        t = jnp.where(t >= tq, t - tq, t)
        visible = lax.broadcasted_iota(I32, (rows, tk), 1) < t
    def scores(p):
        z = _dot(qs_ref[p], k_tile(p))
        sp = jnp.maximum(z, 0.0) + jnp.log2(1.0 + jnp.exp2(-jnp.abs(z)))
        if diagonal:
            sp = jnp.where(visible, sp, 0.0)
        return z, sp.astype(BF16)

    if one_suffix_matmul:
        zs, sps = zip(*[scores(p) for p in range(npair)])
        stacked = _dot(jnp.concatenate(sps, axis=0), ntri)
        suffixes = [stacked[p * rows:(p + 1) * rows] for p in range(npair)]
    for p in range(npair):
        if one_suffix_matmul:
            z, suffix = zs[p], suffixes[p]
        else:
            z, sp = scores(p)
            suffix = _dot(sp, ntri)
        carry = carry_ref[p]
        w = jnp.exp2(z + suffix + _lane_fit(carry, tk))
        if diagonal:
            w = jnp.where(visible, w, 0.0)
        pv = _dot_nt(w.astype(BF16), v_tile(p))
        acc_ref[p] += jnp.where(m0, pv[0:tq], pv[tq:rows])
        carry_ref[p] = carry + jnp.broadcast_to(suffix[:, 0:1], carry.shape)


def _sb_prompt_kernel(q_ref, k_ref, v_ref, o_ref, qs_ref, acc_ref, carry_ref):
    qi = pl.program_id(1)
    tk = SB_TILE
    _sb_stack_queries(q_ref[0], qs_ref)
    acc_ref[...] = jnp.zeros_like(acc_ref)
    carry_ref[...] = jnp.zeros_like(carry_ref)
    ntri = _sb_neg_tri(tk)

    def step(jb, diagonal):
        _sb_tile_step(qs_ref, acc_ref, carry_ref, lambda p: k_ref[0, jb, _pair_lanes(p), :],
                      lambda p: v_ref[0, jb, _pair_lanes(p), :], ntri, diagonal, True)

    step(qi, True)

    def body(i, c):
        step(qi - 1 - 2 * i, False)
        step(qi - 2 - 2 * i, False)
        return c

    lax.fori_loop(0, qi // 2, body, 0)

    @pl.when(qi % 2 == 1)
    def _():
        step(0, False)

    for p in range(acc_ref.shape[0]):
        o_ref[0, :, p * LANES:(p + 1) * LANES] = acc_ref[p].astype(o_ref.dtype)


def _sb_scratch(tq, npair):
    return [pltpu.VMEM((npair, 2 * tq, LANES), BF16), pltpu.VMEM((npair, tq, LANES), F32),
            pltpu.VMEM((npair, 2 * tq, LANES), F32)]


def _sb_prompt(q, kt, vt):
    b, s, w = q.shape
    tq = SB_TILE
    assert s % tq == 0 and w % LANES == 0 and 2 * DH_B == LANES and kt.shape == (b, s // tq, w, tq)
    qspec = pl.BlockSpec((1, tq, w), lambda i, j: (i, j, 0))
    kvspec = pl.BlockSpec((1,) + kt.shape[1:], lambda i, j: (i, 0, 0, 0))
    return pl.pallas_call(
        _sb_prompt_kernel,
        grid=(b, s // tq),
        in_specs=[qspec, kvspec, kvspec],
        out_specs=qspec,
        out_shape=jax.ShapeDtypeStruct((b, s, w), BF16),
        scratch_shapes=_sb_scratch(tq, w // LANES),
        compiler_params=_cparams(("arbitrary", "arbitrary")),
        name="sb_prompt",
    )(q, kt, vt)


def _sb_sample_kernel(q_ref, kn_ref, vn_ref, kp_ref, vp_ref, o_ref, qs_ref, acc_ref, carry_ref):
    nseq, sq, w = q_ref.shape
    past = kp_ref.shape[3]
    npair = w // LANES
    tk = SB_TILE
    for si in range(nseq):
        _sb_stack_queries(q_ref[si], qs_ref, si * npair)
    acc_ref[...] = jnp.zeros_like(acc_ref)
    carry_ref[...] = jnp.zeros_like(carry_ref)
    _sb_tile_step(qs_ref, acc_ref, carry_ref, lambda e: kn_ref[e // npair, 0, _pair_lanes(e % npair), :],
                  lambda e: vn_ref[e // npair, 0, _pair_lanes(e % npair), :], _sb_neg_tri(sq), True, True)
    ntri = _sb_neg_tri(tk)

    def tile(j):
        cols = pl.ds(pl.multiple_of(past - (j + 1) * tk, tk), tk)

        def pair(ref, e):
            p = e % npair
            return ref[e // npair, 2 * p:2 * p + 2, :, cols].reshape(LANES, tk).astype(BF16)

        _sb_tile_step(qs_ref, acc_ref, carry_ref, lambda e: pair(kp_ref, e), lambda e: pair(vp_ref, e), ntri, False,
                      True)

    def body(i, c):
        for j in range(SB_SAMPLE_TILES_PER_TRIP):
            tile(SB_SAMPLE_TILES_PER_TRIP * i + j)
        return c

    lax.fori_loop(0, past // tk // SB_SAMPLE_TILES_PER_TRIP, body, 0)
    for e in range(acc_ref.shape[0]):
        o_ref[e // npair, :, _pair_lanes(e % npair)] = acc_ref[e].astype(o_ref.dtype)


def _sb_sample(q, kt_new, vt_new, kt_past, vt_past):
    b, sq, w = q.shape
    past = kt_past.shape[3]
    ns = SB_SAMPLE_SEQS
    assert past % (SB_SAMPLE_TILES_PER_TRIP * SB_TILE) == 0 and 2 * DH_B == LANES and b % ns == 0
    qspec = pl.BlockSpec((ns, sq, w), lambda i: (i, 0, 0))
    new = pl.BlockSpec((ns, 1, w, sq), lambda i: (i, 0, 0, 0))
    old = pl.BlockSpec((ns, H_B, DH_B, past), lambda i: (i, 0, 0, 0))
    return pl.pallas_call(
        _sb_sample_kernel,
        grid=(b // ns,),
        in_specs=[qspec, new, new, old, old],
        out_specs=qspec,
        out_shape=jax.ShapeDtypeStruct((b, sq, w), BF16),
        scratch_shapes=_sb_scratch(sq, ns * (w // LANES)),
        compiler_params=_cparams(("arbitrary",)),
        name="sb_sample",
    )(q, kt_new, vt_new, kt_past, vt_past)


def _first_argmax(vals, nrows):
    idx = lax.broadcasted_iota(I32, vals.shape, 0)
    top = jnp.max(vals, axis=0, keepdims=True)
    first = jnp.min(jnp.where(vals == top, idx, nrows), axis=0, keepdims=True)
    return top, first, idx


def _merge_kernel(oa_ref, ob_ref, g_ref, x_ref, wb0_ref, wb1_ref, wo_ref, gain_ref, wr_ref, br_ref,
                  x1_ref, h2_ref, eid_ref, wcol_ref):
    d = x_ref.shape[1]
    ya = _dot(oa_ref[...], wb0_ref[...])
    yb = _dot(ob_ref[...], wb1_ref[...])
    g = g_ref[...].astype(F32)
    m = _sigmoid(g[:, 0:d]) * ya + _sigmoid(g[:, d:2 * d]) * yb
    x1 = x_ref[...] + _dot(m.astype(BF16), wo_ref[...])
    x1_ref[...] = x1
    h2 = _rms_norm(x1, gain_ref[...])
    h2_ref[...] = h2.astype(h2_ref.dtype).reshape(h2_ref.shape)

    h_hi, h_lo = _split_bf16(h2)
    w_hi, w_lo = _split_bf16(wr_ref[...])
    lt = _dot_nt(w_hi, h_hi) + _dot_nt(w_hi, h_lo) + _dot_nt(w_lo, h_hi) + br_ref[:, 0:1]
    gl = lt[0:N_GROUPS, :]
    g_top, g_idx, _ = _first_argmax(gl, N_GROUPS)
    g_e = jnp.exp(gl - g_top)
    g_p = jnp.max(g_e / jnp.sum(g_e, axis=0, keepdims=True), axis=0, keepdims=True)
    el = jnp.zeros((EXPERTS_PER_GROUP, lt.shape[1]), F32)
    for g in range(N_GROUPS):
        r0 = 8 + g * EXPERTS_PER_GROUP
        el = jnp.where(g_idx == g, lt[r0:r0 + EXPERTS_PER_GROUP, :], el)
    e_top, i1, eidx = _first_argmax(el, EXPERTS_PER_GROUP)
    e_e = jnp.exp(el - e_top)
    e_p = e_e / jnp.sum(e_e, axis=0, keepdims=True)
    p1 = jnp.max(e_p, axis=0, keepdims=True)
    rest = jnp.where(eidx == i1, -1.0, e_p)
    p2, i2, _ = _first_argmax(rest, EXPERTS_PER_GROUP)
    norm = p1 + p2
    w1 = g_p * (p1 / norm)
    w2 = g_p * (p2 / norm)
    eid_ref[...] = jnp.concatenate([g_idx * EXPERTS_PER_GROUP + i1, g_idx * EXPERTS_PER_GROUP + i2], axis=0)
    rows = lax.broadcasted_iota(I32, (LANES, lt.shape[1]), 0)
    wrows = jnp.where(rows == 0, w1, jnp.where(rows == 1, w2, 0.0))
    wcol_ref[...] = wrows.T


def _merge(oa, ob, gbr, x, wb0, wb1, wo, gain, wr, br):
    t, d = x.shape
    tm = MERGE_TILE
    assert t % tm == 0
    row = lambda w: pl.BlockSpec((tm, w), lambda i: (i, 0))
    full = lambda a: pl.BlockSpec(a.shape, lambda i: (0,) * a.ndim)
    return pl.pallas_call(
        _merge_kernel,
        grid=(t // tm,),
        in_specs=[row(oa.shape[1]), row(ob.shape[1]), row(gbr.shape[1]), row(d),
                  full(wb0), full(wb1), full(wo), full(gain), full(wr), full(br)],
        out_specs=[row(d), pl.BlockSpec((tm, d // LANES, LANES), lambda i: (i, 0, 0)),
                   pl.BlockSpec((TOP_K, tm), lambda i: (0, i)), row(LANES)],
        out_shape=[jax.ShapeDtypeStruct((t, d), F32), jax.ShapeDtypeStruct((t, d // LANES, LANES), BF16),
                   jax.ShapeDtypeStruct((TOP_K, t), I32), jax.ShapeDtypeStruct((t, LANES), F32)],
        compiler_params=_cparams(("arbitrary",)),
        name="merge_router",
    )(oa, ob, gbr, x, wb0, wb1, wo, gain, wr, br)


def _positions_kernel(eid_ref, dest_ref, counts_ref, rank_ref):
    nblk, width = eid_ref.shape
    ji = lax.broadcasted_iota(I32, (width, width), 0)
    si = lax.broadcasted_iota(I32, (width, width), 1)
    prefix = jnp.where(ji <= si, 1.0, 0.0).astype(BF16)
    expert = lax.broadcasted_iota(I32, (N_EXPERTS, width), 0)
    group = max(g for g in (8, 4, 2, 1) if nblk % g == 0)

    def onehot(i):
        return expert == eid_ref[pl.ds(i, 1), :]

    def rank_body(ig, run):
        first = pl.multiple_of(ig * group, group)
        ohs = [onehot(first + j) for j in range(group)]
        stacked = jnp.concatenate([jnp.where(oh, 1.0, 0.0) for oh in ohs], axis=0).astype(BF16)
        cum = _dot(stacked, prefix)
        ranks = []
        for j, oh in enumerate(ohs):
            cum_j = cum[j * N_EXPERTS:(j + 1) * N_EXPERTS, :] + run
            ranks.append(jnp.sum(jnp.where(oh, cum_j, 0.0), axis=0, keepdims=True) - 1.0)
            run = cum_j[:, width - 1:width]
        rank_ref[pl.ds(first, group), :] = jnp.concatenate(ranks, axis=0)
        return run

    counts = lax.fori_loop(0, nblk // group, rank_body, jnp.zeros((N_EXPERTS, 1), F32))
    counts_ref[...] = jnp.broadcast_to(counts, counts_ref.shape).astype(I32)
    c_hi = jnp.floor(counts * (1.0 / 256.0))
    c_lo = counts - 256.0 * c_hi
    ei = lax.broadcasted_iota(I32, (N_EXPERTS, N_EXPERTS), 0)
    ej = lax.broadcasted_iota(I32, (N_EXPERTS, N_EXPERTS), 1)
    strict = jnp.where(ej < ei, 1.0, 0.0).astype(BF16)
    digits = jnp.concatenate([jnp.broadcast_to(c_hi, (N_EXPERTS, LANES)),
                              jnp.broadcast_to(c_lo, (N_EXPERTS, LANES))], axis=1).astype(BF16)
    sums = _dot(strict, digits)
    start = 256.0 * sums[:, 0:1] + sums[:, LANES:LANES + 1]

    def dest_body(ig, carry):
        first = pl.multiple_of(ig * group, group)
        offs = [jnp.sum(jnp.where(onehot(first + j), start, 0.0), axis=0, keepdims=True) for j in range(group)]
        rows = pl.ds(first, group)
        dest_ref[rows, :] = (rank_ref[rows, :] + jnp.concatenate(offs, axis=0)).astype(I32)
        return carry

    lax.fori_loop(0, nblk // group, dest_body, 0)


def _positions(eid_blocks):
    nblk, width = eid_blocks.shape
    vm = lambda shape: pl.BlockSpec(shape, lambda: (0,) * len(shape))
    return pl.pallas_call(
        _positions_kernel,
        in_specs=[vm((nblk, width))],
        out_specs=[vm((nblk, width)), vm((N_EXPERTS, LANES))],
        out_shape=[jax.ShapeDtypeStruct((nblk, width), I32), jax.ShapeDtypeStruct((N_EXPERTS, LANES), I32)],
        scratch_shapes=[pltpu.VMEM((nblk, width), F32)],
        name="positions",
    )(eid_blocks)


def _dispatch_kernel(n_prompt_tiles, dest_ref, hp_ref, hs_ref, xs_ref, sem):
    i = pl.program_id(0)
    tm = dest_ref.shape[1]

    def scatter(src_ref):
        def start(r, c):
            for k in range(TOP_K):
                pltpu.make_async_copy(src_ref.at[r], xs_ref.at[dest_ref[k, r]], sem).start(priority=k)
            return c

        lax.fori_loop(0, tm, start, 0, unroll=DMA_UNROLL)
        for k in range(TOP_K):
            pltpu.make_async_copy(src_ref, xs_ref.at[pl.ds(0, tm)], sem).wait()

    @pl.when(i < n_prompt_tiles)
    def _():
        scatter(hp_ref)

    @pl.when(i >= n_prompt_tiles)
    def _():
        scatter(hs_ref)


def _dispatch(dest, h_prompt, h_sample):
    t = dest.shape[1]
    slab = h_prompt.shape[1:]
    tm = DISPATCH_TILE
    assert h_prompt.shape[0] % tm == 0 and h_sample.shape[0] % tm == 0
    npt = h_prompt.shape[0] // tm
    return pl.pallas_call(
        functools.partial(_dispatch_kernel, npt),
        grid=(t // tm,),
        in_specs=[pl.BlockSpec((TOP_K, tm), lambda i: (0, i), memory_space=pltpu.SMEM),
                  pl.BlockSpec((tm,) + slab, lambda i: (jnp.minimum(i, npt - 1), 0, 0)),
                  pl.BlockSpec((tm,) + slab, lambda i: (jnp.maximum(i - npt, 0), 0, 0))],
        out_specs=pl.BlockSpec(memory_space=pl.ANY),
        out_shape=jax.ShapeDtypeStruct((TOP_K * t,) + slab, h_prompt.dtype),
        scratch_shapes=[pltpu.SemaphoreType.DMA(())],
        compiler_params=_cparams(("arbitrary",)),
        name="dispatch",
    )(dest, h_prompt, h_sample)


def _experts_kernel(vblk_ref, vexp_ref, vlo_ref, vhi_ref, vnext_ref, vslot_ref, xs_ref, wg_ref, wu_ref, wd_ref,
                    ys_ref, wg32_ref, wu32_ref, wd32_ref, wg16_ref, wu16_ref, wd16_ref, sems):
    v = pl.program_id(0)
    lo = vlo_ref[v]
    hi = vhi_ref[v]
    prev = jnp.maximum(v - 1, 0)
    first = jnp.logical_or(v == 0, vblk_ref[v] != vblk_ref[prev])
    new_expert = jnp.logical_or(v == 0, vexp_ref[v] != vexp_ref[prev])

    def weight_copies(e, slot):
        return [pltpu.make_async_copy(src.at[e], dst.at[slot], sems.at[slot])
                for src, dst in ((wg_ref, wg32_ref), (wu_ref, wu32_ref), (wd_ref, wd32_ref))]

    @pl.when(v == 0)
    def _():
        for cp in weight_copies(vexp_ref[0], 0):
            cp.start()

    @pl.when(new_expert)
    def _():
        slot = vslot_ref[v]
        for cp in weight_copies(vexp_ref[v], slot):
            cp.wait()
        wg16_ref[...] = wg32_ref[slot].astype(BF16)
        wu16_ref[...] = wu32_ref[slot].astype(BF16)
        wd16_ref[...] = wd32_ref[slot].astype(BF16)

        @pl.when(vnext_ref[v] >= 0)
        def _():
            for cp in weight_copies(vnext_ref[v], 1 - slot):
                cp.start()

    @pl.when(first)
    def _():
        ys_ref[...] = jnp.zeros_like(ys_ref)

    @pl.when(hi > lo)
    def _():
        tm = xs_ref.shape[0]
        d = wg_ref.shape[1]
        x = xs_ref[...].reshape(tm, d)
        gate = _dot(x, wg16_ref[...])
        up = _dot(x, wu16_ref[...])
        hid = (gate * _sigmoid(gate) * up).astype(BF16)
        y = _dot(hid, wd16_ref[...]).astype(ys_ref.dtype)
        rows = lax.broadcasted_iota(I32, y.shape, 0)
        mine = (rows >= lo) & (rows < hi)
        ys_ref[...] = jnp.where(mine, y, ys_ref[...].reshape(tm, d)).reshape(ys_ref.shape)


def _experts(plan, xs, wg, wu, wd):
    a = xs.shape[0]
    slab = xs.shape[1:]
    d, de = wg.shape[1:]
    tm = MOE_TILE
    assert a % tm == 0 and slab == (d // LANES, LANES) and plan[0].shape[0] == a // tm + N_EXPERTS - 1
    block = lambda v, b, *_: (b[v], 0, 0)
    hbm = pl.BlockSpec(memory_space=pl.ANY)
    grid_spec = pltpu.PrefetchScalarGridSpec(
        num_scalar_prefetch=len(plan),
        grid=(plan[0].shape[0],),
        in_specs=[pl.BlockSpec((tm,) + slab, block), hbm, hbm, hbm],
        out_specs=pl.BlockSpec((tm,) + slab, block),
        scratch_shapes=[pltpu.VMEM((2, d, de), F32), pltpu.VMEM((2, d, de), F32), pltpu.VMEM((2, de, d), F32),
                        pltpu.VMEM((d, de), BF16), pltpu.VMEM((d, de), BF16), pltpu.VMEM((de, d), BF16),
                        pltpu.SemaphoreType.DMA((2,))],
    )
    return pl.pallas_call(
        _experts_kernel,
        grid_spec=grid_spec,
        out_shape=jax.ShapeDtypeStruct((a,) + slab, MOE_OUT_DTYPE),
        compiler_params=_cparams(("arbitrary",)),
        name="experts",
    )(*plan, xs, wg, wu, wd)


def _visit_plan(counts, n_rows):
    tm = MOE_TILE
    nblk = n_rows // tm
    n_visits = nblk + N_EXPERTS - 1
    ends = jnp.cumsum(counts)
    starts = ends - counts
    first_blk = starts // tm
    nvis = jnp.where(counts > 0, (ends + tm - 1) // tm - first_blk, 0)
    vis_end = jnp.cumsum(nvis)
    vis_start = vis_end - nvis
    v = jnp.arange(n_visits, dtype=I32)
    e = jnp.minimum(jnp.sum((vis_end[None, :] <= v[:, None]).astype(I32), axis=1), N_EXPERTS - 1)
    valid = v < vis_end[-1]
    blk = first_blk[e] + (v - vis_start[e])
    lo = jnp.clip(starts[e] - blk * tm, 0, tm)
    hi = jnp.clip(ends[e] - blk * tm, 0, tm)
    ids = jnp.arange(N_EXPERTS, dtype=I32)
    used = counts > 0
    last_e = jnp.max(jnp.where(used, ids, 0))
    blk = jnp.where(valid, blk, nblk - 1).astype(I32)
    e = jnp.where(valid, e, last_e).astype(I32)
    lo = jnp.where(valid, lo, 0).astype(I32)
    hi = jnp.where(valid, hi, 0).astype(I32)
    later_used = used[None, :] & (ids[None, :] > ids[:, None])
    next_used = jnp.min(jnp.where(later_used, ids[None, :], N_EXPERTS), axis=1)
    next_used = jnp.where(next_used < N_EXPERTS, next_used, -1).astype(I32)
    slot = ((jnp.cumsum(used.astype(I32)) - 1) % 2).astype(I32)
    return blk, e, lo, hi, next_used[e], slot[e]


def _combine_kernel(dest_ref, ys_ref, x1_ref, wcol_ref, gain_ref, out_ref, buf_ref, sems):
    tm, d = x1_ref.shape
    part = tm // COMBINE_PARTS

    def start(r, c, sem):
        for k in range(TOP_K):
            pltpu.make_async_copy(ys_ref.at[dest_ref[k, r]], buf_ref.at[k, r], sem).start(priority=k)
        return c

    for h in range(COMBINE_PARTS):
        lax.fori_loop(h * part, (h + 1) * part, functools.partial(start, sem=sems.at[h]), 0, unroll=DMA_UNROLL)
    for h in range(COMBINE_PARTS):
        rows = pl.ds(h * part, part)
        for k in range(TOP_K):
            pltpu.make_async_copy(ys_ref.at[rows], buf_ref.at[k, rows], sems.at[h]).wait()
        y = (wcol_ref[rows, 0:1] * buf_ref[0, rows].reshape(part, d).astype(F32)
             + wcol_ref[rows, 1:2] * buf_ref[1, rows].reshape(part, d).astype(F32))
        out_ref[rows, :] = _rms_norm(x1_ref[rows, :] + y, gain_ref[...])


def _combine(dest, first_token, ys, x1, wcol, gain):
    t, d = x1.shape
    tm = COMBINE_TILE
    assert t % tm == 0 and first_token % tm == 0
    off = first_token // tm
    row = lambda w: pl.BlockSpec((tm, w), lambda i: (i, 0))
    return pl.pallas_call(
        _combine_kernel,
        grid=(t // tm,),
        in_specs=[pl.BlockSpec((TOP_K, tm), lambda i: (0, i + off), memory_space=pltpu.SMEM),
                  pl.BlockSpec(memory_space=pl.ANY), row(d), row(LANES),
                  pl.BlockSpec(gain.shape, lambda i: (0, 0))],
        out_specs=row(d),
        out_shape=jax.ShapeDtypeStruct((t, d), F32),
        scratch_shapes=[pltpu.VMEM((TOP_K, tm) + ys.shape[1:], ys.dtype), pltpu.SemaphoreType.DMA((COMBINE_PARTS,))],
        compiler_params=_cparams(("arbitrary",)),
        name="combine",
    )(dest, ys, x1, wcol, gain)


def _prepare_weights(w_in, w_gla_gate_up, b_gla_gate, w_branch, w_out, w_router_group, b_router_group,
                     w_router_expert, b_router_expert):
    d = w_in.shape[0]
    qk = H_A * DK_A
    mw = H_A * DV_A
    c = 0
    w_qa, c = w_in[:, c:c + qk], c + qk
    w_ka, c = w_in[:, c:c + qk], c + qk
    w_va, c = w_in[:, c:c + mw], c + mw
    w_ra, c = w_in[:, c:c + mw], c + mw
    w_lr, c = w_in[:, c:c + GATE_RANK], c + GATE_RANK
    w_b, c = w_in[:, c:c + 3 * mw], c + 3 * mw
    w_g = w_in[:, c:]
    wa = jnp.concatenate([w_qa, w_ka, w_va, w_ra,
                          jnp.pad(w_lr, ((0, 0), (0, LANES - GATE_RANK)))], axis=1).astype(BF16)
    wgu = jnp.pad(w_gla_gate_up, ((0, LANES - GATE_RANK), (0, 0))).astype(BF16)
    bgu = b_gla_gate[None, :]
    wr = jnp.zeros((LANES, d), F32)
    wr = wr.at[0:N_GROUPS].set(w_router_group.T).at[8:8 + N_EXPERTS].set(w_router_expert.T)
    br = jnp.zeros((LANES,), F32).at[0:N_GROUPS].set(b_router_group).at[8:8 + N_EXPERTS].set(b_router_expert)
    br = jnp.broadcast_to(br[:, None], (LANES, LANES))
    return dict(wa=wa, wqb=w_b[:, 0:mw].astype(BF16), wkvt=w_b[:, mw:3 * mw].T.astype(BF16),
                wg=w_g.astype(BF16), wgu=wgu, bgu=bgu,
                wb0=w_branch[0].astype(BF16), wb1=w_branch[1].astype(BF16), wo=w_out.astype(BF16),
                wr=wr, br=br)


def _mixers(x, s0, k_past, v_past, w, norm_mix_gain, gla_norm_gain, norm_ffn_gain):
    b, s, d = x.shape
    xf = x.reshape(b * s, d)
    qa, ka, va, ra, la, qb, kt, vt, kt16, vt16, gbr = _in_projection(
        xf, s, norm_mix_gain[None, :], w["wa"], w["wqb"], w["wkvt"], w["wg"], w["wgu"], w["bgu"])
    seq = lambda a: a.reshape(b, s, a.shape[-1])
    oa, s_new = _gla(seq(qa), seq(ka), seq(va), seq(ra), seq(la), s0, gla_norm_gain[None, :],
                     min(s, GLA_ROWS))
    to_channel_major = lambda a: jnp.transpose(a, (0, 2, 3, 1))
    if k_past is None:
        ob = _sb_prompt(seq(qb), kt16, vt16)
    else:
        ob = _sb_sample(seq(qb), kt16, vt16, to_channel_major(k_past), to_channel_major(v_past))
    x1, h2, eid, wcol = _merge(oa.reshape(b * s, -1), ob.reshape(b * s, -1), gbr, xf, w["wb0"], w["wb1"],
                               w["wo"], norm_ffn_gain[None, :], w["wr"], w["br"])
    from_channel_major = lambda a: jnp.transpose(a.reshape(b, H_B, DH_B, s), (0, 3, 1, 2))
    return x1, h2, eid, wcol, s_new, from_channel_major(kt), from_channel_major(vt)


def kernel(x_prompt, x_sample, state_gla, cache_sb_k, cache_sb_v, norm_mix_gain, w_in, w_gla_gate_up, b_gla_gate, gla_norm_gain, w_branch, w_out, norm_ffn_gain, w_router_group, b_router_group, w_router_expert, b_router_expert, w_exp_gate, w_exp_up, w_exp_down, norm_final_gain):
    depth = w_in.shape[0]
    assert depth == 1, "one trunk layer per step"
    l = 0
    w = _prepare_weights(w_in[l], w_gla_gate_up[l], b_gla_gate[l], w_branch[l], w_out[l], w_router_group[l],
                         b_router_group[l], w_router_expert[l], b_router_expert[l])
    bp, sp, d = x_prompt.shape
    bs, ss, _ = x_sample.shape
    s0 = jnp.zeros((bp, H_A, DK_A, DV_A), x_prompt.dtype)
    x1p, h2p, eidp, wcolp, gla_p, k_p, v_p = _mixers(
        x_prompt, s0, None, None, w, norm_mix_gain[l], gla_norm_gain[l], norm_ffn_gain[l])
    x1s, h2s, eids, wcols, gla_s, k_s, v_s = _mixers(
        x_sample, state_gla[l], cache_sb_k[l], cache_sb_v[l], w, norm_mix_gain[l], gla_norm_gain[l],
        norm_ffn_gain[l])

    tp, ts = bp * sp, bs * ss
    eid = jnp.concatenate([eidp, eids], axis=1)
    dest_blocks, counts = _positions(eid.reshape(-1, SORT_WIDTH))
    dest = dest_blocks.reshape(TOP_K, tp + ts)
    xs = _dispatch(dest, h2p, h2s)
    plan = _visit_plan(counts[:, 0], TOP_K * (tp + ts))
    ys = _experts(plan, xs, w_exp_gate[l], w_exp_up[l], w_exp_down[l])
    gf = norm_final_gain[None, :]
    y_prompt = _combine(dest, 0, ys, x1p, wcolp, gf).reshape(bp, sp, d)
    y_sample = _combine(dest, tp, ys, x1s, wcols, gf).reshape(bs, ss, d)
    return (y_prompt, y_sample, gla_p[None], k_p[None], v_p[None], gla_s[None], k_s[None], v_s[None])
```

```python
import functools

import jax
import jax.numpy as jnp
from jax import lax
from jax.experimental import pallas as pl
from jax.experimental.pallas import tpu as pltpu

F32 = jnp.float32
BF16 = jnp.bfloat16
MOE_OUT_DTYPE = jnp.bfloat16
I32 = jnp.int32

LANES = 128
LOG2_E = 1.4426950408889634
RMS_EPS = 1e-6
GATE_TAU = 16.0
H_A = 4
DK_A = 64
DV_A = 128
GATE_RANK = 16
H_B = 8
DH_B = 64
N_GROUPS = 4
EXPERTS_PER_GROUP = 8
N_EXPERTS = N_GROUPS * EXPERTS_PER_GROUP
TOP_K = 2
GLA_CHUNK = 64
GLA_SUB = 16
GLA_EXP_CLAMP = 80.0
GLA_SEQS = 4
GLA_ROWS = 256
SB_TILE = 256
SB_SAMPLE_SEQS = 2
SB_SAMPLE_TILES_PER_TRIP = 4
MOE_TILE = 512
SORT_WIDTH = 256
INPROJ_TILE = 512
MERGE_TILE = 1024
DISPATCH_TILE = 2048
COMBINE_TILE = 512
COMBINE_PARTS = 4
DMA_UNROLL = 16
VMEM_LIMIT = 56 * 1024 * 1024


def _cparams(sem):
    return pltpu.CompilerParams(dimension_semantics=sem, vmem_limit_bytes=VMEM_LIMIT)


def _dot(a, b):
    return jnp.dot(a, b, preferred_element_type=F32)


def _dot_nt(a, b):
    return lax.dot_general(a, b, (((1,), (1,)), ((), ())), preferred_element_type=F32)


def _dot_tn(a, b):
    return lax.dot_general(a, b, (((0,), (0,)), ((), ())), preferred_element_type=F32)


def _split_bf16(x):
    hi = x.astype(BF16)
    lo = (x - hi.astype(F32)).astype(BF16)
    return hi, lo


def _log_sigmoid(x):
    return jnp.minimum(x, 0.0) - jnp.log(1.0 + jnp.exp(-jnp.abs(x)))


def _sigmoid(x):
    return 1.0 / (1.0 + jnp.exp(-x))


def _rms_norm(x, gain):
    return x * lax.rsqrt(jnp.mean(x * x, axis=-1, keepdims=True) + RMS_EPS) * gain


def _inproj_kernel(x_ref, gain_ref, wa_ref, wqb_ref, wkvt_ref, wg_ref, wgu_ref, bgu_ref,
                   qa_ref, ka_ref, va_ref, ra_ref, la_ref, qb_ref, kt_ref, vt_ref,
                   kt16_ref, vt16_ref, gbr_ref):
    h = _rms_norm(x_ref[...], gain_ref[...]).astype(BF16)
    pa = H_A * DK_A
    mw = va_ref.shape[-1]
    kvt = _dot_nt(wkvt_ref[...], h)
    nseq, _, s = kt_ref.shape
    ntile, tile = kt16_ref.shape[1], kt16_ref.shape[3]
    for i in range(nseq):
        kt_ref[i] = kvt[0:mw, i * s:(i + 1) * s]
        vt_ref[i] = kvt[mw:2 * mw, i * s:(i + 1) * s]
        for j in range(ntile):
            cols = slice(i * s + j * tile, i * s + (j + 1) * tile)
            kt16_ref[i, j] = kvt[0:mw, cols].astype(BF16)
            vt16_ref[i, j] = kvt[mw:2 * mw, cols].astype(BF16)
    qb_ref[...] = _dot(h, wqb_ref[...]).astype(BF16)
    qa_ref[...] = _dot(h, wa_ref[:, 0:pa])
    ka_ref[...] = _dot(h, wa_ref[:, pa:2 * pa])
    va_ref[...] = _dot(h, wa_ref[:, 2 * pa:2 * pa + mw])
    ra_ref[...] = _dot(h, wa_ref[:, 2 * pa + mw:2 * pa + 2 * mw])
    lr = _dot(h, wa_ref[:, 2 * pa + 2 * mw:2 * pa + 2 * mw + LANES])
    gl = _dot(lr.astype(BF16), wgu_ref[...]) + bgu_ref[...]
    la_ref[...] = _log_sigmoid(gl) / GATE_TAU
    gbr_ref[...] = _dot(h, wg_ref[...]).astype(gbr_ref.dtype)


def _in_projection(x, seq_len, gain, wa, wqb, wkvt, wg, wgu, bgu):
    t, d = x.shape
    nb = t // seq_len
    pa = H_A * DK_A
    mw = wqb.shape[1]
    tm = INPROJ_TILE
    assert t % tm == 0
    row = lambda w: pl.BlockSpec((tm, w), lambda i: (i, 0))
    full = lambda a: pl.BlockSpec(a.shape, lambda i: (0,) * a.ndim, pipeline_mode=pl.Buffered(1))
    if seq_len >= tm:
        per_seq = seq_len // tm
        ntile = tm // SB_TILE
        assert tm % SB_TILE == 0 and seq_len % tm == 0
        kt_spec = pl.BlockSpec((1, mw, tm), lambda i: (i // per_seq, 0, i % per_seq))
        kt16_spec = pl.BlockSpec((1, ntile, mw, SB_TILE), lambda i: (i // per_seq, i % per_seq, 0, 0))
        kt16_shape = (nb, seq_len // SB_TILE, mw, SB_TILE)
    else:
        nseq = tm // seq_len
        assert tm % seq_len == 0
        kt_spec = pl.BlockSpec((nseq, mw, seq_len), lambda i: (i, 0, 0))
        kt16_spec = pl.BlockSpec((nseq, 1, mw, seq_len), lambda i: (i, 0, 0, 0))
        kt16_shape = (nb, 1, mw, seq_len)
    outs = [
        (jax.ShapeDtypeStruct((t, pa), F32), row(pa)), (jax.ShapeDtypeStruct((t, pa), F32), row(pa)),
        (jax.ShapeDtypeStruct((t, mw), F32), row(mw)), (jax.ShapeDtypeStruct((t, mw), F32), row(mw)),
        (jax.ShapeDtypeStruct((t, pa), F32), row(pa)),
        (jax.ShapeDtypeStruct((t, mw), BF16), row(mw)),
        (jax.ShapeDtypeStruct((nb, mw, seq_len), F32), kt_spec), (jax.ShapeDtypeStruct((nb, mw, seq_len), F32), kt_spec),
        (jax.ShapeDtypeStruct(kt16_shape, BF16), kt16_spec), (jax.ShapeDtypeStruct(kt16_shape, BF16), kt16_spec),
        (jax.ShapeDtypeStruct((t, wg.shape[1]), BF16), row(wg.shape[1])),
    ]
    return pl.pallas_call(
        _inproj_kernel,
        grid=(t // tm,),
        in_specs=[row(d), full(gain), full(wa), full(wqb), full(wkvt), full(wg), full(wgu), full(bgu)],
        out_specs=[spec for _, spec in outs],
        out_shape=[shape for shape, _ in outs],
        compiler_params=_cparams(("arbitrary",)),
        name="in_projection",
    )(x, gain, wa, wqb, wkvt, wg, wgu, bgu)


def _gla_chunk(q, k, v, b, st):
    c = q.shape[0]
    b_last = b[c - 1:c, :]
    rows = lax.broadcasted_iota(I32, (c, LANES), 0)
    nsub = c // GLA_SUB
    refs = [jnp.zeros((1, LANES), F32)] + [b[i * GLA_SUB - 1:i * GLA_SUB, :] for i in range(1, nsub)]
    ref_rows = refs[0]
    for i in range(1, nsub):
        ref_rows = jnp.where(rows >= i * GLA_SUB, refs[i], ref_rows)
    q_rel = q * jnp.exp(b - ref_rows)
    lhs = jnp.concatenate(
        [jnp.where((rows >= i * GLA_SUB) & (rows < (i + 1) * GLA_SUB), q_rel, 0.0) for i in range(nsub)],
        axis=1).astype(BF16)
    rhs = jnp.concatenate(
        [jnp.where(rows < (i + 1) * GLA_SUB, k * jnp.exp(jnp.minimum(refs[i] - b, GLA_EXP_CLAMP)), 0.0)
         for i in range(nsub)], axis=1).astype(BF16)
    att = _dot_nt(lhs, rhs)
    tt = lax.broadcasted_iota(I32, (c, c), 0)
    ss = lax.broadcasted_iota(I32, (c, c), 1)
    att = jnp.where(ss <= tt, att, 0.0)
    v16 = v.astype(BF16)
    inter = _dot_nt((q * jnp.exp(b)).astype(BF16), st.astype(BF16))
    intra = _dot(att.astype(BF16), v16)
    kd = (k * jnp.exp(b_last - b)).astype(BF16)
    st_new = st * jnp.exp(b_last) + _dot_tn(v16, kd)
    return inter + intra, st_new


def _gla_kernel(qa_ref, ka_ref, va_ref, ra_ref, la_ref, s0_ref, gain_ref, o_ref, sfin_ref, st_ref):
    j = pl.program_id(1)
    nj = pl.num_programs(1)
    nseq, rows_per_step, _ = qa_ref.shape
    c = GLA_CHUNK
    zpad = jnp.zeros((LANES - DK_A, DV_A), F32)

    def state_rows(h):
        return slice((h % 2) * DK_A, (h % 2 + 1) * DK_A)

    @pl.when(j == 0)
    def _():
        for si in range(nseq):
            for h in range(H_A):
                parts = [s0_ref[si, h], zpad] if h % 2 == 0 else [zpad, s0_ref[si, h]]
                st_ref[si * H_A + h] = jnp.concatenate(parts, axis=0).T

    ti = lax.broadcasted_iota(I32, (rows_per_step, rows_per_step), 0)
    si = lax.broadcasted_iota(I32, (rows_per_step, rows_per_step), 1)
    chunk_shift = c.bit_length() - 1
    same_chunk = (ti >> chunk_shift) == (si >> chunk_shift)
    tril_blocks = jnp.where(same_chunk & (si <= ti), 1.0, 0.0).astype(BF16)
    gain = gain_ref[...]
    lane = lax.broadcasted_iota(I32, (1, LANES), 1)
    half_masks = (lane < DK_A, lane >= DK_A)
    for si in range(nseq):
        la_hi, la_lo = _split_bf16(la_ref[si])
        b_all = _dot(tril_blocks, la_hi) + _dot(tril_blocks, la_lo)
        for h in range(H_A):
            hp = slice((h // 2) * LANES, (h // 2 + 1) * LANES)
            hv = slice(h * DV_A, (h + 1) * DV_A)
            mine = half_masks[h % 2]
            st = st_ref[si * H_A + h]
            for ci in range(rows_per_step // c):
                r0 = ci * c
                q = jnp.where(mine, qa_ref[si, r0:r0 + c, hp], 0.0) * (DK_A ** -0.5)
                k = jnp.where(mine, ka_ref[si, r0:r0 + c, hp], 0.0)
                o, st = _gla_chunk(q, k, va_ref[si, r0:r0 + c, hv], b_all[r0:r0 + c, hp], st)
                r = ra_ref[si, r0:r0 + c, hv]
                o = _rms_norm(o, gain) * (r * _sigmoid(r))
                o_ref[si, r0:r0 + c, hv] = o.astype(o_ref.dtype)
            st_ref[si * H_A + h] = st

    @pl.when(j == nj - 1)
    def _():
        for si in range(nseq):
            for h in range(H_A):
                sfin_ref[si, h] = st_ref[si * H_A + h].T[state_rows(h), :]


def _gla(qa, ka, va, ra, la, s0, gain, rows_per_step):
    b, s, pa = qa.shape
    assert 2 * DK_A == LANES and H_A % 2 == 0 and DV_A == LANES
    assert s % rows_per_step == 0 and rows_per_step % GLA_CHUNK == 0 and GLA_CHUNK % GLA_SUB == 0
    mw = va.shape[-1]
    ns = GLA_SEQS
    assert b % ns == 0
    seq = lambda w: pl.BlockSpec((ns, rows_per_step, w), lambda i, j: (i, j, 0))
    state = pl.BlockSpec((ns, H_A, DK_A, DV_A), lambda i, j: (i, 0, 0, 0))
    return pl.pallas_call(
        _gla_kernel,
        grid=(b // ns, s // rows_per_step),
        in_specs=[seq(pa), seq(pa), seq(mw), seq(mw), seq(pa), state,
                  pl.BlockSpec(gain.shape, lambda i, j: (0, 0))],
        out_specs=[seq(mw), state],
        out_shape=[jax.ShapeDtypeStruct((b, s, mw), BF16),
                   jax.ShapeDtypeStruct((b, H_A, DK_A, DV_A), F32)],
        scratch_shapes=[pltpu.VMEM((ns * H_A, LANES, LANES), F32)],
        compiler_params=_cparams(("arbitrary", "arbitrary")),
        name="gla",
    )(qa, ka, va, ra, la, s0, gain)


def _head_lane_masks():
    lane = lax.broadcasted_iota(I32, (1, LANES), 1)
    return lane < DH_B, lane >= DH_B


def _sb_neg_tri(tk):
    ji = lax.broadcasted_iota(I32, (tk, tk), 0)
    si = lax.broadcasted_iota(I32, (tk, tk), 1)
    return jnp.where(ji >= si, -1.0, 0.0).astype(BF16)


def _sb_stack_queries(q, qs_ref, base=0):
    m0, m1 = _head_lane_masks()
    for p in range(q.shape[1] // LANES):
        qp = (q[:, p * LANES:(p + 1) * LANES].astype(F32) * (DH_B ** -0.5 * LOG2_E)).astype(BF16)
        zero = jnp.zeros_like(qp)
        qs_ref[base + p] = jnp.concatenate([jnp.where(m0, qp, zero), jnp.where(m1, qp, zero)], axis=0)


def _pair_lanes(p):
    return slice(p * LANES, (p + 1) * LANES)


def _lane_fit(x, width):
    if width >= LANES:
        return jnp.concatenate([x] * (width // LANES), axis=1)
    return x[:, 0:width]


def _sb_tile_step(qs_ref, acc_ref, carry_ref, k_tile, v_tile, ntri, diagonal, one_suffix_matmul=False):
    npair, rows, _ = qs_ref.shape
    tq = rows // 2
    tk = ntri.shape[1]
    m0, _ = _head_lane_masks()
    if diagonal:
        t = lax.broadcasted_iota(I32, (rows, tk), 0)
        t = jnp.where(t >= tq, t - tq, t)
        visible = lax.broadcasted_iota(I32, (rows, tk), 1) < t
    def scores(p):
        z = _dot(qs_ref[p], k_tile(p))
        sp = jnp.maximum(z, 0.0) + jnp.log2(1.0 + jnp.exp2(-jnp.abs(z)))
        if diagonal:
            sp = jnp.where(visible, sp, 0.0)
        return z, sp.astype(BF16)

    if one_suffix_matmul:
        zs, sps = zip(*[scores(p) for p in range(npair)])
        stacked = _dot(jnp.concatenate(sps, axis=0), ntri)
        suffixes = [stacked[p * rows:(p + 1) * rows] for p in range(npair)]
    for p in range(npair):
        if one_suffix_matmul:
            z, suffix = zs[p], suffixes[p]
        else:
            z, sp = scores(p)
            suffix = _dot(sp, ntri)
        carry = carry_ref[p]
        w = jnp.exp2(z + suffix + _lane_fit(carry, tk))
        if diagonal:
            w = jnp.where(visible, w, 0.0)
        pv = _dot_nt(w.astype(BF16), v_tile(p))
        acc_ref[p] += jnp.where(m0, pv[0:tq], pv[tq:rows])
        carry_ref[p] = carry + jnp.broadcast_to(suffix[:, 0:1], carry.shape)


def _sb_prompt_kernel(q_ref, k_ref, v_ref, o_ref, qs_ref, acc_ref, carry_ref):
    qi = pl.program_id(1)
    tk = SB_TILE
    _sb_stack_queries(q_ref[0], qs_ref)
    acc_ref[...] = jnp.zeros_like(acc_ref)
    carry_ref[...] = jnp.zeros_like(carry_ref)
    ntri = _sb_neg_tri(tk)

    def step(jb, diagonal):
        _sb_tile_step(qs_ref, acc_ref, carry_ref, lambda p: k_ref[0, jb, _pair_lanes(p), :],
                      lambda p: v_ref[0, jb, _pair_lanes(p), :], ntri, diagonal, True)

    step(qi, True)

    def body(i, c):
        for j in range(4):
            step(qi - 1 - 4 * i - j, False)
        return c

    lax.fori_loop(0, qi // 4, body, 0)
    left = qi % 4

    @pl.when(left >= 2)
    def _():
        step(left - 1, False)
        step(left - 2, False)

    @pl.when(left % 2 == 1)
    def _():
        step(0, False)

    for p in range(acc_ref.shape[0]):
        o_ref[0, :, p * LANES:(p + 1) * LANES] = acc_ref[p].astype(o_ref.dtype)


def _sb_scratch(tq, npair):
    return [pltpu.VMEM((npair, 2 * tq, LANES), BF16), pltpu.VMEM((npair, tq, LANES), F32),
            pltpu.VMEM((npair, 2 * tq, LANES), F32)]


def _sb_prompt(q, kt, vt):
    b, s, w = q.shape
    tq = SB_TILE
    assert s % tq == 0 and w % LANES == 0 and 2 * DH_B == LANES and kt.shape == (b, s // tq, w, tq)
    qspec = pl.BlockSpec((1, tq, w), lambda i, j: (i, j, 0))
    kvspec = pl.BlockSpec((1,) + kt.shape[1:], lambda i, j: (i, 0, 0, 0))
    return pl.pallas_call(
        _sb_prompt_kernel,
        grid=(b, s // tq),
        in_specs=[qspec, kvspec, kvspec],
        out_specs=qspec,
        out_shape=jax.ShapeDtypeStruct((b, s, w), BF16),
        scratch_shapes=_sb_scratch(tq, w // LANES),
        compiler_params=_cparams(("arbitrary", "arbitrary")),
        name="sb_prompt",
    )(q, kt, vt)


def _sb_sample_kernel(q_ref, kn_ref, vn_ref, kp_ref, vp_ref, o_ref, qs_ref, acc_ref, carry_ref):
    nseq, sq, w = q_ref.shape
    past = kp_ref.shape[3]
    npair = w // LANES
    tk = SB_TILE
    for si in range(nseq):
        _sb_stack_queries(q_ref[si], qs_ref, si * npair)
    acc_ref[...] = jnp.zeros_like(acc_ref)
    carry_ref[...] = jnp.zeros_like(carry_ref)
    _sb_tile_step(qs_ref, acc_ref, carry_ref, lambda e: kn_ref[e // npair, 0, _pair_lanes(e % npair), :],
                  lambda e: vn_ref[e // npair, 0, _pair_lanes(e % npair), :], _sb_neg_tri(sq), True, True)
    ntri = _sb_neg_tri(tk)

    def tile(j):
        cols = pl.ds(pl.multiple_of(past - (j + 1) * tk, tk), tk)

        def pair(ref, e):
            p = e % npair
            return ref[e // npair, 2 * p:2 * p + 2, :, cols].reshape(LANES, tk).astype(BF16)

        _sb_tile_step(qs_ref, acc_ref, carry_ref, lambda e: pair(kp_ref, e), lambda e: pair(vp_ref, e), ntri, False,
                      True)

    def body(i, c):
        for j in range(SB_SAMPLE_TILES_PER_TRIP):
            tile(SB_SAMPLE_TILES_PER_TRIP * i + j)
        return c

    lax.fori_loop(0, past // tk // SB_SAMPLE_TILES_PER_TRIP, body, 0)
    for e in range(acc_ref.shape[0]):
        o_ref[e // npair, :, _pair_lanes(e % npair)] = acc_ref[e].astype(o_ref.dtype)


def _sb_sample(q, kt_new, vt_new, kt_past, vt_past):
    b, sq, w = q.shape
    past = kt_past.shape[3]
    ns = SB_SAMPLE_SEQS
    assert past % (SB_SAMPLE_TILES_PER_TRIP * SB_TILE) == 0 and 2 * DH_B == LANES and b % ns == 0
    qspec = pl.BlockSpec((ns, sq, w), lambda i: (i, 0, 0))
    new = pl.BlockSpec((ns, 1, w, sq), lambda i: (i, 0, 0, 0))
    old = pl.BlockSpec((ns, H_B, DH_B, past), lambda i: (i, 0, 0, 0))
    return pl.pallas_call(
        _sb_sample_kernel,
        grid=(b // ns,),
        in_specs=[qspec, new, new, old, old],
        out_specs=qspec,
        out_shape=jax.ShapeDtypeStruct((b, sq, w), BF16),
        scratch_shapes=_sb_scratch(sq, ns * (w // LANES)),
        compiler_params=_cparams(("arbitrary",)),
        name="sb_sample",
    )(q, kt_new, vt_new, kt_past, vt_past)


def _first_argmax(vals, nrows):
    idx = lax.broadcasted_iota(I32, vals.shape, 0)
    top = jnp.max(vals, axis=0, keepdims=True)
    first = jnp.min(jnp.where(vals == top, idx, nrows), axis=0, keepdims=True)
    return top, first, idx


def _merge_kernel(oa_ref, ob_ref, g_ref, x_ref, wb0_ref, wb1_ref, wo_ref, gain_ref, wr_ref, br_ref,
                  x1_ref, h2_ref, eid_ref, wcol_ref):
    d = x_ref.shape[1]
    ya = _dot(oa_ref[...], wb0_ref[...])
    yb = _dot(ob_ref[...], wb1_ref[...])
    g = g_ref[...].astype(F32)
    m = _sigmoid(g[:, 0:d]) * ya + _sigmoid(g[:, d:2 * d]) * yb
    x1 = x_ref[...] + _dot(m.astype(BF16), wo_ref[...])
    x1_ref[...] = x1
    h2 = _rms_norm(x1, gain_ref[...])
    h2_ref[...] = h2.astype(h2_ref.dtype).reshape(h2_ref.shape)

    h_hi, h_lo = _split_bf16(h2)
    w_hi, w_lo = _split_bf16(wr_ref[...])
    lt = _dot_nt(w_hi, h_hi) + _dot_nt(w_hi, h_lo) + _dot_nt(w_lo, h_hi) + br_ref[:, 0:1]
    gl = lt[0:N_GROUPS, :]
    g_top, g_idx, _ = _first_argmax(gl, N_GROUPS)
    g_e = jnp.exp(gl - g_top)
    g_p = jnp.max(g_e / jnp.sum(g_e, axis=0, keepdims=True), axis=0, keepdims=True)
    el = jnp.zeros((EXPERTS_PER_GROUP, lt.shape[1]), F32)
    for g in range(N_GROUPS):
        r0 = 8 + g * EXPERTS_PER_GROUP
        el = jnp.where(g_idx == g, lt[r0:r0 + EXPERTS_PER_GROUP, :], el)
    e_top, i1, eidx = _first_argmax(el, EXPERTS_PER_GROUP)
    e_e = jnp.exp(el - e_top)
    e_p = e_e / jnp.sum(e_e, axis=0, keepdims=True)
    p1 = jnp.max(e_p, axis=0, keepdims=True)
    rest = jnp.where(eidx == i1, -1.0, e_p)
    p2, i2, _ = _first_argmax(rest, EXPERTS_PER_GROUP)
    norm = p1 + p2
    w1 = g_p * (p1 / norm)
    w2 = g_p * (p2 / norm)
    eid_ref[...] = jnp.concatenate([g_idx * EXPERTS_PER_GROUP + i1, g_idx * EXPERTS_PER_GROUP + i2], axis=0)
    rows = lax.broadcasted_iota(I32, (LANES, lt.shape[1]), 0)
    wrows = jnp.where(rows == 0, w1, jnp.where(rows == 1, w2, 0.0))
    wcol_ref[...] = wrows.T


def _merge(oa, ob, gbr, x, wb0, wb1, wo, gain, wr, br):
    t, d = x.shape
    tm = MERGE_TILE
    assert t % tm == 0
    row = lambda w: pl.BlockSpec((tm, w), lambda i: (i, 0))
    full = lambda a: pl.BlockSpec(a.shape, lambda i: (0,) * a.ndim)
    return pl.pallas_call(
        _merge_kernel,
        grid=(t // tm,),
        in_specs=[row(oa.shape[1]), row(ob.shape[1]), row(gbr.shape[1]), row(d),
                  full(wb0), full(wb1), full(wo), full(gain), full(wr), full(br)],
        out_specs=[row(d), pl.BlockSpec((tm, d // LANES, LANES), lambda i: (i, 0, 0)),
                   pl.BlockSpec((TOP_K, tm), lambda i: (0, i)), row(LANES)],
        out_shape=[jax.ShapeDtypeStruct((t, d), F32), jax.ShapeDtypeStruct((t, d // LANES, LANES), BF16),
                   jax.ShapeDtypeStruct((TOP_K, t), I32), jax.ShapeDtypeStruct((t, LANES), F32)],
        compiler_params=_cparams(("arbitrary",)),
        name="merge_router",
    )(oa, ob, gbr, x, wb0, wb1, wo, gain, wr, br)


def _positions_kernel(eid_ref, dest_ref, counts_ref, rank_ref):
    nblk, width = eid_ref.shape
    ji = lax.broadcasted_iota(I32, (width, width), 0)
    si = lax.broadcasted_iota(I32, (width, width), 1)
    prefix = jnp.where(ji <= si, 1.0, 0.0).astype(BF16)
    expert = lax.broadcasted_iota(I32, (N_EXPERTS, width), 0)
    group = max(g for g in (8, 4, 2, 1) if nblk % g == 0)

    def onehot(i):
        return expert == eid_ref[pl.ds(i, 1), :]

    def rank_body(ig, run):
        first = pl.multiple_of(ig * group, group)
        ohs = [onehot(first + j) for j in range(group)]
        stacked = jnp.concatenate([jnp.where(oh, 1.0, 0.0) for oh in ohs], axis=0).astype(BF16)
        cum = _dot(stacked, prefix)
        ranks = []
        for j, oh in enumerate(ohs):
            cum_j = cum[j * N_EXPERTS:(j + 1) * N_EXPERTS, :] + run
            ranks.append(jnp.sum(jnp.where(oh, cum_j, 0.0), axis=0, keepdims=True) - 1.0)
            run = cum_j[:, width - 1:width]
        rank_ref[pl.ds(first, group), :] = jnp.concatenate(ranks, axis=0)
        return run

    counts = lax.fori_loop(0, nblk // group, rank_body, jnp.zeros((N_EXPERTS, 1), F32))
    counts_ref[...] = jnp.broadcast_to(counts, counts_ref.shape).astype(I32)
    c_hi = jnp.floor(counts * (1.0 / 256.0))
    c_lo = counts - 256.0 * c_hi
    ei = lax.broadcasted_iota(I32, (N_EXPERTS, N_EXPERTS), 0)
    ej = lax.broadcasted_iota(I32, (N_EXPERTS, N_EXPERTS), 1)
    strict = jnp.where(ej < ei, 1.0, 0.0).astype(BF16)
    digits = jnp.concatenate([jnp.broadcast_to(c_hi, (N_EXPERTS, LANES)),
                              jnp.broadcast_to(c_lo, (N_EXPERTS, LANES))], axis=1).astype(BF16)
    sums = _dot(strict, digits)
    start = 256.0 * sums[:, 0:1] + sums[:, LANES:LANES + 1]

    def dest_body(ig, carry):
        first = pl.multiple_of(ig * group, group)
        offs = [jnp.sum(jnp.where(onehot(first + j), start, 0.0), axis=0, keepdims=True) for j in range(group)]
        rows = pl.ds(first, group)
        dest_ref[rows, :] = (rank_ref[rows, :] + jnp.concatenate(offs, axis=0)).astype(I32)
        return carry

    lax.fori_loop(0, nblk // group, dest_body, 0)


def _positions(eid_blocks):
    nblk, width = eid_blocks.shape
    vm = lambda shape: pl.BlockSpec(shape, lambda: (0,) * len(shape))
    return pl.pallas_call(
        _positions_kernel,
        in_specs=[vm((nblk, width))],
        out_specs=[vm((nblk, width)), vm((N_EXPERTS, LANES))],
        out_shape=[jax.ShapeDtypeStruct((nblk, width), I32), jax.ShapeDtypeStruct((N_EXPERTS, LANES), I32)],
        scratch_shapes=[pltpu.VMEM((nblk, width), F32)],
        name="positions",
    )(eid_blocks)


def _dispatch_kernel(n_prompt_tiles, dest_ref, hp_ref, hs_ref, xs_ref, sem):
    i = pl.program_id(0)
    tm = dest_ref.shape[1]

    def scatter(src_ref):
        def start(r, c):
            for k in range(TOP_K):
                pltpu.make_async_copy(src_ref.at[r], xs_ref.at[dest_ref[k, r]], sem).start(priority=k)
            return c

        lax.fori_loop(0, tm, start, 0, unroll=DMA_UNROLL)
        for k in range(TOP_K):
            pltpu.make_async_copy(src_ref, xs_ref.at[pl.ds(0, tm)], sem).wait()

    @pl.when(i < n_prompt_tiles)
    def _():
        scatter(hp_ref)

    @pl.when(i >= n_prompt_tiles)
    def _():
        scatter(hs_ref)


def _dispatch(dest, h_prompt, h_sample):
    t = dest.shape[1]
    slab = h_prompt.shape[1:]
    tm = DISPATCH_TILE
    assert h_prompt.shape[0] % tm == 0 and h_sample.shape[0] % tm == 0
    npt = h_prompt.shape[0] // tm
    return pl.pallas_call(
        functools.partial(_dispatch_kernel, npt),
        grid=(t // tm,),
        in_specs=[pl.BlockSpec((TOP_K, tm), lambda i: (0, i), memory_space=pltpu.SMEM),
                  pl.BlockSpec((tm,) + slab, lambda i: (jnp.minimum(i, npt - 1), 0, 0)),
                  pl.BlockSpec((tm,) + slab, lambda i: (jnp.maximum(i - npt, 0), 0, 0))],
        out_specs=pl.BlockSpec(memory_space=pl.ANY),
        out_shape=jax.ShapeDtypeStruct((TOP_K * t,) + slab, h_prompt.dtype),
        scratch_shapes=[pltpu.SemaphoreType.DMA(())],
        compiler_params=_cparams(("arbitrary",)),
        name="dispatch",
    )(dest, h_prompt, h_sample)


def _experts_kernel(vblk_ref, vexp_ref, vlo_ref, vhi_ref, vnext_ref, vslot_ref, xs_ref, wg_ref, wu_ref, wd_ref,
                    ys_ref, wg32_ref, wu32_ref, wd32_ref, wg16_ref, wu16_ref, wd16_ref, sems):
    v = pl.program_id(0)
    lo = vlo_ref[v]
    hi = vhi_ref[v]
    prev = jnp.maximum(v - 1, 0)
    first = jnp.logical_or(v == 0, vblk_ref[v] != vblk_ref[prev])
    new_expert = jnp.logical_or(v == 0, vexp_ref[v] != vexp_ref[prev])

    def weight_copies(e, slot):
        return [pltpu.make_async_copy(src.at[e], dst.at[slot], sems.at[slot])
                for src, dst in ((wg_ref, wg32_ref), (wu_ref, wu32_ref), (wd_ref, wd32_ref))]

    @pl.when(v == 0)
    def _():
        for cp in weight_copies(vexp_ref[0], 0):
            cp.start()

    @pl.when(new_expert)
    def _():
        slot = vslot_ref[v]
        for cp in weight_copies(vexp_ref[v], slot):
            cp.wait()
        wg16_ref[...] = wg32_ref[slot].astype(BF16)
        wu16_ref[...] = wu32_ref[slot].astype(BF16)
        wd16_ref[...] = wd32_ref[slot].astype(BF16)

        @pl.when(vnext_ref[v] >= 0)
        def _():
            for cp in weight_copies(vnext_ref[v], 1 - slot):
                cp.start()

    @pl.when(first)
    def _():
        ys_ref[...] = jnp.zeros_like(ys_ref)

    @pl.when(hi > lo)
    def _():
        tm = xs_ref.shape[0]
        d = wg_ref.shape[1]
        x = xs_ref[...].reshape(tm, d)
        gate = _dot(x, wg16_ref[...])
        up = _dot(x, wu16_ref[...])
        hid = (gate * _sigmoid(gate) * up).astype(BF16)
        y = _dot(hid, wd16_ref[...]).astype(ys_ref.dtype)
        rows = lax.broadcasted_iota(I32, y.shape, 0)
        mine = (rows >= lo) & (rows < hi)
        ys_ref[...] = jnp.where(mine, y, ys_ref[...].reshape(tm, d)).reshape(ys_ref.shape)


def _experts(plan, xs, wg, wu, wd):
    a = xs.shape[0]
    slab = xs.shape[1:]
    d, de = wg.shape[1:]
    tm = MOE_TILE
    assert a % tm == 0 and slab == (d // LANES, LANES) and plan[0].shape[0] == a // tm + N_EXPERTS - 1
    block = lambda v, b, *_: (b[v], 0, 0)
    hbm = pl.BlockSpec(memory_space=pl.ANY)
    grid_spec = pltpu.PrefetchScalarGridSpec(
        num_scalar_prefetch=len(plan),
        grid=(plan[0].shape[0],),
        in_specs=[pl.BlockSpec((tm,) + slab, block), hbm, hbm, hbm],
        out_specs=pl.BlockSpec((tm,) + slab, block),
        scratch_shapes=[pltpu.VMEM((2, d, de), F32), pltpu.VMEM((2, d, de), F32), pltpu.VMEM((2, de, d), F32),
                        pltpu.VMEM((d, de), BF16), pltpu.VMEM((d, de), BF16), pltpu.VMEM((de, d), BF16),
                        pltpu.SemaphoreType.DMA((2,))],
    )
    return pl.pallas_call(
        _experts_kernel,
        grid_spec=grid_spec,
        out_shape=jax.ShapeDtypeStruct((a,) + slab, MOE_OUT_DTYPE),
        compiler_params=_cparams(("arbitrary",)),
        name="experts",
    )(*plan, xs, wg, wu, wd)


def _visit_plan(counts, n_rows):
    tm = MOE_TILE
    nblk = n_rows // tm
    n_visits = nblk + N_EXPERTS - 1
    ends = jnp.cumsum(counts)
    starts = ends - counts
    first_blk = starts // tm
    nvis = jnp.where(counts > 0, (ends + tm - 1) // tm - first_blk, 0)
    vis_end = jnp.cumsum(nvis)
    vis_start = vis_end - nvis
    v = jnp.arange(n_visits, dtype=I32)
    e = jnp.minimum(jnp.sum((vis_end[None, :] <= v[:, None]).astype(I32), axis=1), N_EXPERTS - 1)
    valid = v < vis_end[-1]
    blk = first_blk[e] + (v - vis_start[e])
    lo = jnp.clip(starts[e] - blk * tm, 0, tm)
    hi = jnp.clip(ends[e] - blk * tm, 0, tm)
    ids = jnp.arange(N_EXPERTS, dtype=I32)
    used = counts > 0
    last_e = jnp.max(jnp.where(used, ids, 0))
    blk = jnp.where(valid, blk, nblk - 1).astype(I32)
    e = jnp.where(valid, e, last_e).astype(I32)
    lo = jnp.where(valid, lo, 0).astype(I32)
    hi = jnp.where(valid, hi, 0).astype(I32)
    later_used = used[None, :] & (ids[None, :] > ids[:, None])
    next_used = jnp.min(jnp.where(later_used, ids[None, :], N_EXPERTS), axis=1)
    next_used = jnp.where(next_used < N_EXPERTS, next_used, -1).astype(I32)
    slot = ((jnp.cumsum(used.astype(I32)) - 1) % 2).astype(I32)
    return blk, e, lo, hi, next_used[e], slot[e]


def _combine_kernel(dest_ref, ys_ref, x1_ref, wcol_ref, gain_ref, out_ref, buf_ref, sems):
    tm, d = x1_ref.shape
    part = tm // COMBINE_PARTS

    def start(r, c, sem):
        for k in range(TOP_K):
            pltpu.make_async_copy(ys_ref.at[dest_ref[k, r]], buf_ref.at[k, r], sem).start(priority=k)
        return c

    for h in range(COMBINE_PARTS):
        lax.fori_loop(h * part, (h + 1) * part, functools.partial(start, sem=sems.at[h]), 0, unroll=DMA_UNROLL)
    for h in range(COMBINE_PARTS):
        rows = pl.ds(h * part, part)
        for k in range(TOP_K):
            pltpu.make_async_copy(ys_ref.at[rows], buf_ref.at[k, rows], sems.at[h]).wait()
        y = (wcol_ref[rows, 0:1] * buf_ref[0, rows].reshape(part, d).astype(F32)
             + wcol_ref[rows, 1:2] * buf_ref[1, rows].reshape(part, d).astype(F32))
        out_ref[rows, :] = _rms_norm(x1_ref[rows, :] + y, gain_ref[...])


def _combine(dest, first_token, ys, x1, wcol, gain):
    t, d = x1.shape
    tm = COMBINE_TILE
    assert t % tm == 0 and first_token % tm == 0
    off = first_token // tm
    row = lambda w: pl.BlockSpec((tm, w), lambda i: (i, 0))
    return pl.pallas_call(
        _combine_kernel,
        grid=(t // tm,),
        in_specs=[pl.BlockSpec((TOP_K, tm), lambda i: (0, i + off), memory_space=pltpu.SMEM),
                  pl.BlockSpec(memory_space=pl.ANY), row(d), row(LANES),
                  pl.BlockSpec(gain.shape, lambda i: (0, 0))],
        out_specs=row(d),
        out_shape=jax.ShapeDtypeStruct((t, d), F32),
        scratch_shapes=[pltpu.VMEM((TOP_K, tm) + ys.shape[1:], ys.dtype), pltpu.SemaphoreType.DMA((COMBINE_PARTS,))],
        compiler_params=_cparams(("arbitrary",)),
        name="combine",
    )(dest, ys, x1, wcol, gain)


def _prepare_weights(w_in, w_gla_gate_up, b_gla_gate, w_branch, w_out, w_router_group, b_router_group,
                     w_router_expert, b_router_expert):
    d = w_in.shape[0]
    qk = H_A * DK_A
    mw = H_A * DV_A
    c = 0
    w_qa, c = w_in[:, c:c + qk], c + qk
    w_ka, c = w_in[:, c:c + qk], c + qk
    w_va, c = w_in[:, c:c + mw], c + mw
    w_ra, c = w_in[:, c:c + mw], c + mw
    w_lr, c = w_in[:, c:c + GATE_RANK], c + GATE_RANK
    w_b, c = w_in[:, c:c + 3 * mw], c + 3 * mw
    w_g = w_in[:, c:]
    wa = jnp.concatenate([w_qa, w_ka, w_va, w_ra,
                          jnp.pad(w_lr, ((0, 0), (0, LANES - GATE_RANK)))], axis=1).astype(BF16)
    wgu = jnp.pad(w_gla_gate_up, ((0, LANES - GATE_RANK), (0, 0))).astype(BF16)
    bgu = b_gla_gate[None, :]
    wr = jnp.zeros((LANES, d), F32)
    wr = wr.at[0:N_GROUPS].set(w_router_group.T).at[8:8 + N_EXPERTS].set(w_router_expert.T)
    br = jnp.zeros((LANES,), F32).at[0:N_GROUPS].set(b_router_group).at[8:8 + N_EXPERTS].set(b_router_expert)
    br = jnp.broadcast_to(br[:, None], (LANES, LANES))
    return dict(wa=wa, wqb=w_b[:, 0:mw].astype(BF16), wkvt=w_b[:, mw:3 * mw].T.astype(BF16),
                wg=w_g.astype(BF16), wgu=wgu, bgu=bgu,
                wb0=w_branch[0].astype(BF16), wb1=w_branch[1].astype(BF16), wo=w_out.astype(BF16),
                wr=wr, br=br)


def _mixers(x, s0, k_past, v_past, w, norm_mix_gain, gla_norm_gain, norm_ffn_gain):
    b, s, d = x.shape
    xf = x.reshape(b * s, d)
    qa, ka, va, ra, la, qb, kt, vt, kt16, vt16, gbr = _in_projection(
        xf, s, norm_mix_gain[None, :], w["wa"], w["wqb"], w["wkvt"], w["wg"], w["wgu"], w["bgu"])
    seq = lambda a: a.reshape(b, s, a.shape[-1])
    oa, s_new = _gla(seq(qa), seq(ka), seq(va), seq(ra), seq(la), s0, gla_norm_gain[None, :],
                     min(s, GLA_ROWS))
    to_channel_major = lambda a: jnp.transpose(a, (0, 2, 3, 1))
    if k_past is None:
        ob = _sb_prompt(seq(qb), kt16, vt16)
    else:
        ob = _sb_sample(seq(qb), kt16, vt16, to_channel_major(k_past), to_channel_major(v_past))
    x1, h2, eid, wcol = _merge(oa.reshape(b * s, -1), ob.reshape(b * s, -1), gbr, xf, w["wb0"], w["wb1"],
                               w["wo"], norm_ffn_gain[None, :], w["wr"], w["br"])
    from_channel_major = lambda a: jnp.transpose(a.reshape(b, H_B, DH_B, s), (0, 3, 1, 2))
    return x1, h2, eid, wcol, s_new, from_channel_major(kt), from_channel_major(vt)


def kernel(x_prompt, x_sample, state_gla, cache_sb_k, cache_sb_v, norm_mix_gain, w_in, w_gla_gate_up, b_gla_gate, gla_norm_gain, w_branch, w_out, norm_ffn_gain, w_router_group, b_router_group, w_router_expert, b_router_expert, w_exp_gate, w_exp_up, w_exp_down, norm_final_gain):
    depth = w_in.shape[0]
    assert depth == 1, "one trunk layer per step"
    l = 0
    w = _prepare_weights(w_in[l], w_gla_gate_up[l], b_gla_gate[l], w_branch[l], w_out[l], w_router_group[l],
                         b_router_group[l], w_router_expert[l], b_router_expert[l])
    bp, sp, d = x_prompt.shape
    bs, ss, _ = x_sample.shape
    s0 = jnp.zeros((bp, H_A, DK_A, DV_A), x_prompt.dtype)
    x1p, h2p, eidp, wcolp, gla_p, k_p, v_p = _mixers(
        x_prompt, s0, None, None, w, norm_mix_gain[l], gla_norm_gain[l], norm_ffn_gain[l])
    x1s, h2s, eids, wcols, gla_s, k_s, v_s = _mixers(
        x_sample, state_gla[l], cache_sb_k[l], cache_sb_v[l], w, norm_mix_gain[l], gla_norm_gain[l],
        norm_ffn_gain[l])

    tp, ts = bp * sp, bs * ss
    eid = jnp.concatenate([eidp, eids], axis=1)
    dest_blocks, counts = _positions(eid.reshape(-1, SORT_WIDTH))
    dest = dest_blocks.reshape(TOP_K, tp + ts)
    xs = _dispatch(dest, h2p, h2s)
    plan = _visit_plan(counts[:, 0], TOP_K * (tp + ts))
    ys = _experts(plan, xs, w_exp_gate[l], w_exp_up[l], w_exp_down[l])
    gf = norm_final_gain[None, :]
    y_prompt = _combine(dest, 0, ys, x1p, wcolp, gf).reshape(bp, sp, d)
    y_sample = _combine(dest, tp, ys, x1s, wcols, gf).reshape(bs, ss, d)
    return (y_prompt, y_sample, gla_p[None], k_p[None], v_p[None], gla_s[None], k_s[None], v_s[None])
```

```python
import functools

import jax
import jax.numpy as jnp
from jax import lax
from jax.experimental import pallas as pl
from jax.experimental.pallas import tpu as pltpu

F32 = jnp.float32
BF16 = jnp.bfloat16
MOE_OUT_DTYPE = jnp.bfloat16
I32 = jnp.int32

LANES = 128
LOG2_E = 1.4426950408889634
RMS_EPS = 1e-6
GATE_TAU = 16.0
H_A = 4
DK_A = 64
DV_A = 128
GATE_RANK = 16
H_B = 8
DH_B = 64
N_GROUPS = 4
EXPERTS_PER_GROUP = 8
N_EXPERTS = N_GROUPS * EXPERTS_PER_GROUP
TOP_K = 2
GLA_CHUNK = 64
GLA_SUB = 16
GLA_EXP_CLAMP = 80.0
GLA_SEQS = 4
GLA_ROWS = 256
SB_TILE = 256
SB_SAMPLE_SEQS = 2
SB_SAMPLE_TILES_PER_TRIP = 8
MOE_TILE = 512
SORT_WIDTH = 256
INPROJ_TILE = 512
MERGE_TILE = 1024
DISPATCH_TILE = 2048
COMBINE_TILE = 512
COMBINE_PARTS = 4
DMA_UNROLL = 32
VMEM_LIMIT = 56 * 1024 * 1024


def _cparams(sem):
    return pltpu.CompilerParams(dimension_semantics=sem, vmem_limit_bytes=VMEM_LIMIT)


def _dot(a, b):
    return jnp.dot(a, b, preferred_element_type=F32)


def _dot_nt(a, b):
    return lax.dot_general(a, b, (((1,), (1,)), ((), ())), preferred_element_type=F32)


def _dot_tn(a, b):
    return lax.dot_general(a, b, (((0,), (0,)), ((), ())), preferred_element_type=F32)


def _split_bf16(x):
    hi = x.astype(BF16)
    lo = (x - hi.astype(F32)).astype(BF16)
    return hi, lo


def _log_sigmoid(x):
    return jnp.minimum(x, 0.0) - jnp.log(1.0 + jnp.exp(-jnp.abs(x)))


def _sigmoid(x):
    return 1.0 / (1.0 + jnp.exp(-x))


def _rms_norm(x, gain):
    return x * lax.rsqrt(jnp.mean(x * x, axis=-1, keepdims=True) + RMS_EPS) * gain


def _inproj_kernel(x_ref, gain_ref, wa_ref, wqb_ref, wkvt_ref, wg_ref, wgu_ref, bgu_ref,
                   qa_ref, ka_ref, va_ref, ra_ref, la_ref, qb_ref, kt_ref, vt_ref,
                   kt16_ref, vt16_ref, gbr_ref):
    h = _rms_norm(x_ref[...], gain_ref[...]).astype(BF16)
    pa = H_A * DK_A
    mw = va_ref.shape[-1]
    kvt = _dot_nt(wkvt_ref[...], h)
    nseq, _, s = kt_ref.shape
    ntile, tile = kt16_ref.shape[1], kt16_ref.shape[3]
    for i in range(nseq):
        kt_ref[i] = kvt[0:mw, i * s:(i + 1) * s]
        vt_ref[i] = kvt[mw:2 * mw, i * s:(i + 1) * s]
        for j in range(ntile):
            cols = slice(i * s + j * tile, i * s + (j + 1) * tile)
            kt16_ref[i, j] = kvt[0:mw, cols].astype(BF16)
            vt16_ref[i, j] = kvt[mw:2 * mw, cols].astype(BF16)
    qb_ref[...] = _dot(h, wqb_ref[...]).astype(BF16)
    qa_ref[...] = _dot(h, wa_ref[:, 0:pa])
    ka_ref[...] = _dot(h, wa_ref[:, pa:2 * pa])
    va_ref[...] = _dot(h, wa_ref[:, 2 * pa:2 * pa + mw])
    ra_ref[...] = _dot(h, wa_ref[:, 2 * pa + mw:2 * pa + 2 * mw])
    lr = _dot(h, wa_ref[:, 2 * pa + 2 * mw:2 * pa + 2 * mw + LANES])
    gl = _dot(lr.astype(BF16), wgu_ref[...]) + bgu_ref[...]
    la_ref[...] = _log_sigmoid(gl) / GATE_TAU
    gbr_ref[...] = _dot(h, wg_ref[...]).astype(gbr_ref.dtype)


def _in_projection(x, seq_len, gain, wa, wqb, wkvt, wg, wgu, bgu):
    t, d = x.shape
    nb = t // seq_len
    pa = H_A * DK_A
    mw = wqb.shape[1]
    tm = INPROJ_TILE
    assert t % tm == 0
    row = lambda w: pl.BlockSpec((tm, w), lambda i: (i, 0))
    full = lambda a: pl.BlockSpec(a.shape, lambda i: (0,) * a.ndim, pipeline_mode=pl.Buffered(1))
    if seq_len >= tm:
        per_seq = seq_len // tm
        ntile = tm // SB_TILE
        assert tm % SB_TILE == 0 and seq_len % tm == 0
        kt_spec = pl.BlockSpec((1, mw, tm), lambda i: (i // per_seq, 0, i % per_seq))
        kt16_spec = pl.BlockSpec((1, ntile, mw, SB_TILE), lambda i: (i // per_seq, i % per_seq, 0, 0))
        kt16_shape = (nb, seq_len // SB_TILE, mw, SB_TILE)
    else:
        nseq = tm // seq_len
        assert tm % seq_len == 0
        kt_spec = pl.BlockSpec((nseq, mw, seq_len), lambda i: (i, 0, 0))
        kt16_spec = pl.BlockSpec((nseq, 1, mw, seq_len), lambda i: (i, 0, 0, 0))
        kt16_shape = (nb, 1, mw, seq_len)
    outs = [
        (jax.ShapeDtypeStruct((t, pa), F32), row(pa)), (jax.ShapeDtypeStruct((t, pa), F32), row(pa)),
        (jax.ShapeDtypeStruct((t, mw), F32), row(mw)), (jax.ShapeDtypeStruct((t, mw), F32), row(mw)),
        (jax.ShapeDtypeStruct((t, pa), F32), row(pa)),
        (jax.ShapeDtypeStruct((t, mw), BF16), row(mw)),
        (jax.ShapeDtypeStruct((nb, mw, seq_len), F32), kt_spec), (jax.ShapeDtypeStruct((nb, mw, seq_len), F32), kt_spec),
        (jax.ShapeDtypeStruct(kt16_shape, BF16), kt16_spec), (jax.ShapeDtypeStruct(kt16_shape, BF16), kt16_spec),
        (jax.ShapeDtypeStruct((t, wg.shape[1]), BF16), row(wg.shape[1])),
    ]
    return pl.pallas_call(
        _inproj_kernel,
        grid=(t // tm,),
        in_specs=[row(d), full(gain), full(wa), full(wqb), full(wkvt), full(wg), full(wgu), full(bgu)],
        out_specs=[spec for _, spec in outs],
        out_shape=[shape for shape, _ in outs],
        compiler_params=_cparams(("arbitrary",)),
        name="in_projection",
    )(x, gain, wa, wqb, wkvt, wg, wgu, bgu)


def _gla_chunk(q, k, v, b, st):
    c = q.shape[0]
    b_last = b[c - 1:c, :]
    rows = lax.broadcasted_iota(I32, (c, LANES), 0)
    nsub = c // GLA_SUB
    refs = [jnp.zeros((1, LANES), F32)] + [b[i * GLA_SUB - 1:i * GLA_SUB, :] for i in range(1, nsub)]
    ref_rows = refs[0]
    for i in range(1, nsub):
        ref_rows = jnp.where(rows >= i * GLA_SUB, refs[i], ref_rows)
    q_rel = q * jnp.exp(b - ref_rows)
    lhs = jnp.concatenate(
        [jnp.where((rows >= i * GLA_SUB) & (rows < (i + 1) * GLA_SUB), q_rel, 0.0) for i in range(nsub)],
        axis=1).astype(BF16)
    rhs = jnp.concatenate(
        [jnp.where(rows < (i + 1) * GLA_SUB, k * jnp.exp(jnp.minimum(refs[i] - b, GLA_EXP_CLAMP)), 0.0)
         for i in range(nsub)], axis=1).astype(BF16)
    att = _dot_nt(lhs, rhs)
    tt = lax.broadcasted_iota(I32, (c, c), 0)
    ss = lax.broadcasted_iota(I32, (c, c), 1)
    att = jnp.where(ss <= tt, att, 0.0)
    v16 = v.astype(BF16)
    inter = _dot_nt((q * jnp.exp(b)).astype(BF16), st.astype(BF16))
    intra = _dot(att.astype(BF16), v16)
    kd = (k * jnp.exp(b_last - b)).astype(BF16)
    st_new = st * jnp.exp(b_last) + _dot_tn(v16, kd)
    return inter + intra, st_new


def _gla_kernel(qa_ref, ka_ref, va_ref, ra_ref, la_ref, s0_ref, gain_ref, o_ref, sfin_ref, st_ref):
    j = pl.program_id(1)
    nj = pl.num_programs(1)
    nseq, rows_per_step, _ = qa_ref.shape
    c = GLA_CHUNK
    zpad = jnp.zeros((LANES - DK_A, DV_A), F32)

    def state_rows(h):
        return slice((h % 2) * DK_A, (h % 2 + 1) * DK_A)

    @pl.when(j == 0)
    def _():
        for si in range(nseq):
            for h in range(H_A):
                parts = [s0_ref[si, h], zpad] if h % 2 == 0 else [zpad, s0_ref[si, h]]
                st_ref[si * H_A + h] = jnp.concatenate(parts, axis=0).T

    ti = lax.broadcasted_iota(I32, (rows_per_step, rows_per_step), 0)
    si = lax.broadcasted_iota(I32, (rows_per_step, rows_per_step), 1)
    chunk_shift = c.bit_length() - 1
    same_chunk = (ti >> chunk_shift) == (si >> chunk_shift)
    tril_blocks = jnp.where(same_chunk & (si <= ti), 1.0, 0.0).astype(BF16)
    gain = gain_ref[...]
    lane = lax.broadcasted_iota(I32, (1, LANES), 1)
    half_masks = (lane < DK_A, lane >= DK_A)
    for si in range(nseq):
        la_hi, la_lo = _split_bf16(la_ref[si])
        b_all = _dot(tril_blocks, la_hi) + _dot(tril_blocks, la_lo)
        for h in range(H_A):
            hp = slice((h // 2) * LANES, (h // 2 + 1) * LANES)
            hv = slice(h * DV_A, (h + 1) * DV_A)
            mine = half_masks[h % 2]
            st = st_ref[si * H_A + h]
            for ci in range(rows_per_step // c):
                r0 = ci * c
                q = jnp.where(mine, qa_ref[si, r0:r0 + c, hp], 0.0) * (DK_A ** -0.5)
                k = jnp.where(mine, ka_ref[si, r0:r0 + c, hp], 0.0)
                o, st = _gla_chunk(q, k, va_ref[si, r0:r0 + c, hv], b_all[r0:r0 + c, hp], st)
                r = ra_ref[si, r0:r0 + c, hv]
                o = _rms_norm(o, gain) * (r * _sigmoid(r))
                o_ref[si, r0:r0 + c, hv] = o.astype(o_ref.dtype)
            st_ref[si * H_A + h] = st

    @pl.when(j == nj - 1)
    def _():
        for si in range(nseq):
            for h in range(H_A):
                sfin_ref[si, h] = st_ref[si * H_A + h].T[state_rows(h), :]


def _gla(qa, ka, va, ra, la, s0, gain, rows_per_step):
    b, s, pa = qa.shape
    assert 2 * DK_A == LANES and H_A % 2 == 0 and DV_A == LANES
    assert s % rows_per_step == 0 and rows_per_step % GLA_CHUNK == 0 and GLA_CHUNK % GLA_SUB == 0
    mw = va.shape[-1]
    ns = GLA_SEQS
    assert b % ns == 0
    seq = lambda w: pl.BlockSpec((ns, rows_per_step, w), lambda i, j: (i, j, 0))
    state = pl.BlockSpec((ns, H_A, DK_A, DV_A), lambda i, j: (i, 0, 0, 0))
    return pl.pallas_call(
        _gla_kernel,
        grid=(b // ns, s // rows_per_step),
        in_specs=[seq(pa), seq(pa), seq(mw), seq(mw), seq(pa), state,
                  pl.BlockSpec(gain.shape, lambda i, j: (0, 0))],
        out_specs=[seq(mw), state],
        out_shape=[jax.ShapeDtypeStruct((b, s, mw), BF16),
                   jax.ShapeDtypeStruct((b, H_A, DK_A, DV_A), F32)],
        scratch_shapes=[pltpu.VMEM((ns * H_A, LANES, LANES), F32)],
        compiler_params=_cparams(("arbitrary", "arbitrary")),
        name="gla",
    )(qa, ka, va, ra, la, s0, gain)


def _head_lane_masks():
    lane = lax.broadcasted_iota(I32, (1, LANES), 1)
    return lane < DH_B, lane >= DH_B


def _sb_neg_tri(tk):
    ji = lax.broadcasted_iota(I32, (tk, tk), 0)
    si = lax.broadcasted_iota(I32, (tk, tk), 1)
    return jnp.where(ji >= si, -1.0, 0.0).astype(BF16)


def _sb_stack_queries(q, qs_ref, base=0):
    m0, m1 = _head_lane_masks()
    for p in range(q.shape[1] // LANES):
        qp = (q[:, p * LANES:(p + 1) * LANES].astype(F32) * (DH_B ** -0.5 * LOG2_E)).astype(BF16)
        zero = jnp.zeros_like(qp)
        qs_ref[base + p] = jnp.concatenate([jnp.where(m0, qp, zero), jnp.where(m1, qp, zero)], axis=0)


def _pair_lanes(p):
    return slice(p * LANES, (p + 1) * LANES)


def _lane_fit(x, width):
    if width >= LANES:
        return jnp.concatenate([x] * (width // LANES), axis=1)
    return x[:, 0:width]


def _sb_tile_step(qs_ref, acc_ref, carry_ref, k_tile, v_tile, ntri, diagonal, one_suffix_matmul=False):
    npair, rows, _ = qs_ref.shape
    tq = rows // 2
    tk = ntri.shape[1]
    m0, _ = _head_lane_masks()
    if diagonal:
        t = lax.broadcasted_iota(I32, (rows, tk), 0)
        t = jnp.where(t >= tq, t - tq, t)
        visible = lax.broadcasted_iota(I32, (rows, tk), 1) < t
    def scores(p):
        z = _dot(qs_ref[p], k_tile(p))
        sp = jnp.maximum(z, 0.0) + jnp.log2(1.0 + jnp.exp2(-jnp.abs(z)))
        if diagonal:
            sp = jnp.where(visible, sp, 0.0)
        return z, sp.astype(BF16)

    if one_suffix_matmul:
        zs, sps = zip(*[scores(p) for p in range(npair)])
        stacked = _dot(jnp.concatenate(sps, axis=0), ntri)
        suffixes = [stacked[p * rows:(p + 1) * rows] for p in range(npair)]
    for p in range(npair):
        if one_suffix_matmul:
            z, suffix = zs[p], suffixes[p]
        else:
            z, sp = scores(p)
            suffix = _dot(sp, ntri)
        carry = carry_ref[p]
        w = jnp.exp2(z + suffix + _lane_fit(carry, tk))
        if diagonal:
            w = jnp.where(visible, w, 0.0)
        pv = _dot_nt(w.astype(BF16), v_tile(p))
        acc_ref[p] += jnp.where(m0, pv[0:tq], pv[tq:rows])
        carry_ref[p] = carry + jnp.broadcast_to(suffix[:, 0:1], carry.shape)


def _sb_prompt_kernel(q_ref, k_ref, v_ref, o_ref, qs_ref, acc_ref, carry_ref):
    qi = pl.program_id(1)
    tk = SB_TILE
    _sb_stack_queries(q_ref[0], qs_ref)
    acc_ref[...] = jnp.zeros_like(acc_ref)
    carry_ref[...] = jnp.zeros_like(carry_ref)
    ntri = _sb_neg_tri(tk)

    def step(jb, diagonal):
        _sb_tile_step(qs_ref, acc_ref, carry_ref, lambda p: k_ref[0, jb, _pair_lanes(p), :],
                      lambda p: v_ref[0, jb, _pair_lanes(p), :], ntri, diagonal, True)

    step(qi, True)

    def body(i, c):
        for j in range(4):
            step(qi - 1 - 4 * i - j, False)
        return c

    lax.fori_loop(0, qi // 4, body, 0)
    left = qi % 4

    @pl.when(left >= 2)
    def _():
        step(left - 1, False)
        step(left - 2, False)

    @pl.when(left % 2 == 1)
    def _():
        step(0, False)

    for p in range(acc_ref.shape[0]):
        o_ref[0, :, p * LANES:(p + 1) * LANES] = acc_ref[p].astype(o_ref.dtype)


def _sb_scratch(tq, npair):
    return [pltpu.VMEM((npair, 2 * tq, LANES), BF16), pltpu.VMEM((npair, tq, LANES), F32),
            pltpu.VMEM((npair, 2 * tq, LANES), F32)]


def _sb_prompt(q, kt, vt):
    b, s, w = q.shape
    tq = SB_TILE
    assert s % tq == 0 and w % LANES == 0 and 2 * DH_B == LANES and kt.shape == (b, s // tq, w, tq)
    qspec = pl.BlockSpec((1, tq, w), lambda i, j: (i, j, 0))
    kvspec = pl.BlockSpec((1,) + kt.shape[1:], lambda i, j: (i, 0, 0, 0))
    return pl.pallas_call(
        _sb_prompt_kernel,
        grid=(b, s // tq),
        in_specs=[qspec, kvspec, kvspec],
        out_specs=qspec,
        out_shape=jax.ShapeDtypeStruct((b, s, w), BF16),
        scratch_shapes=_sb_scratch(tq, w // LANES),
        compiler_params=_cparams(("arbitrary", "arbitrary")),
        name="sb_prompt",
    )(q, kt, vt)


def _sb_sample_kernel(q_ref, kn_ref, vn_ref, kp_ref, vp_ref, o_ref, qs_ref, acc_ref, carry_ref):
    nseq, sq, w = q_ref.shape
    past = kp_ref.shape[3]
    npair = w // LANES
    tk = SB_TILE
    for si in range(nseq):
        _sb_stack_queries(q_ref[si], qs_ref, si * npair)
    acc_ref[...] = jnp.zeros_like(acc_ref)
    carry_ref[...] = jnp.zeros_like(carry_ref)
    _sb_tile_step(qs_ref, acc_ref, carry_ref, lambda e: kn_ref[e // npair, 0, _pair_lanes(e % npair), :],
                  lambda e: vn_ref[e // npair, 0, _pair_lanes(e % npair), :], _sb_neg_tri(sq), True, True)
    ntri = _sb_neg_tri(tk)

    def tile(j):
        cols = pl.ds(pl.multiple_of(past - (j + 1) * tk, tk), tk)

        def pair(ref, e):
            p = e % npair
            return ref[e // npair, 2 * p:2 * p + 2, :, cols].reshape(LANES, tk).astype(BF16)

        _sb_tile_step(qs_ref, acc_ref, carry_ref, lambda e: pair(kp_ref, e), lambda e: pair(vp_ref, e), ntri, False,
                      True)

    def body(i, c):
        for j in range(SB_SAMPLE_TILES_PER_TRIP):
            tile(SB_SAMPLE_TILES_PER_TRIP * i + j)
        return c

    lax.fori_loop(0, past // tk // SB_SAMPLE_TILES_PER_TRIP, body, 0)
    for e in range(acc_ref.shape[0]):
        o_ref[e // npair, :, _pair_lanes(e % npair)] = acc_ref[e].astype(o_ref.dtype)


def _sb_sample(q, kt_new, vt_new, kt_past, vt_past):
    b, sq, w = q.shape
    past = kt_past.shape[3]
    ns = SB_SAMPLE_SEQS
    assert past % (SB_SAMPLE_TILES_PER_TRIP * SB_TILE) == 0 and 2 * DH_B == LANES and b % ns == 0
    qspec = pl.BlockSpec((ns, sq, w), lambda i: (i, 0, 0))
    new = pl.BlockSpec((ns, 1, w, sq), lambda i: (i, 0, 0, 0))
    old = pl.BlockSpec((ns, H_B, DH_B, past), lambda i: (i, 0, 0, 0))
    return pl.pallas_call(
        _sb_sample_kernel,
        grid=(b // ns,),
        in_specs=[qspec, new, new, old, old],
        out_specs=qspec,
        out_shape=jax.ShapeDtypeStruct((b, sq, w), BF16),
        scratch_shapes=_sb_scratch(sq, ns * (w // LANES)),
        compiler_params=_cparams(("arbitrary",)),
        name="sb_sample",
    )(q, kt_new, vt_new, kt_past, vt_past)


def _first_argmax(vals, nrows):
    idx = lax.broadcasted_iota(I32, vals.shape, 0)
    top = jnp.max(vals, axis=0, keepdims=True)
    first = jnp.min(jnp.where(vals == top, idx, nrows), axis=0, keepdims=True)
    return top, first, idx


def _merge_kernel(oa_ref, ob_ref, g_ref, x_ref, wb0_ref, wb1_ref, wo_ref, gain_ref, wr_ref, br_ref,
                  x1_ref, h2_ref, eid_ref, wcol_ref):
    d = x_ref.shape[1]
    ya = _dot(oa_ref[...], wb0_ref[...])
    yb = _dot(ob_ref[...], wb1_ref[...])
    g = g_ref[...].astype(F32)
    m = _sigmoid(g[:, 0:d]) * ya + _sigmoid(g[:, d:2 * d]) * yb
    x1 = x_ref[...] + _dot(m.astype(BF16), wo_ref[...])
    x1_ref[...] = x1
    h2 = _rms_norm(x1, gain_ref[...])
    h2_ref[...] = h2.astype(h2_ref.dtype).reshape(h2_ref.shape)

    h_hi, h_lo = _split_bf16(h2)
    w_hi, w_lo = _split_bf16(wr_ref[...])
    lt = _dot_nt(w_hi, h_hi) + _dot_nt(w_hi, h_lo) + _dot_nt(w_lo, h_hi) + br_ref[:, 0:1]
    gl = lt[0:N_GROUPS, :]
    g_top, g_idx, _ = _first_argmax(gl, N_GROUPS)
    g_e = jnp.exp(gl - g_top)
    g_p = jnp.max(g_e / jnp.sum(g_e, axis=0, keepdims=True), axis=0, keepdims=True)
    el = jnp.zeros((EXPERTS_PER_GROUP, lt.shape[1]), F32)
    for g in range(N_GROUPS):
        r0 = 8 + g * EXPERTS_PER_GROUP
        el = jnp.where(g_idx == g, lt[r0:r0 + EXPERTS_PER_GROUP, :], el)
    e_top, i1, eidx = _first_argmax(el, EXPERTS_PER_GROUP)
    e_e = jnp.exp(el - e_top)
    e_p = e_e / jnp.sum(e_e, axis=0, keepdims=True)
    p1 = jnp.max(e_p, axis=0, keepdims=True)
    rest = jnp.where(eidx == i1, -1.0, e_p)
    p2, i2, _ = _first_argmax(rest, EXPERTS_PER_GROUP)
    norm = p1 + p2
    w1 = g_p * (p1 / norm)
    w2 = g_p * (p2 / norm)
    eid_ref[...] = jnp.concatenate([g_idx * EXPERTS_PER_GROUP + i1, g_idx * EXPERTS_PER_GROUP + i2], axis=0)
    rows = lax.broadcasted_iota(I32, (LANES, lt.shape[1]), 0)
    wrows = jnp.where(rows == 0, w1, jnp.where(rows == 1, w2, 0.0))
    wcol_ref[...] = wrows.T


def _merge(oa, ob, gbr, x, wb0, wb1, wo, gain, wr, br):
    t, d = x.shape
    tm = MERGE_TILE
    assert t % tm == 0
    row = lambda w: pl.BlockSpec((tm, w), lambda i: (i, 0))
    full = lambda a: pl.BlockSpec(a.shape, lambda i: (0,) * a.ndim)
    return pl.pallas_call(
        _merge_kernel,
        grid=(t // tm,),
        in_specs=[row(oa.shape[1]), row(ob.shape[1]), row(gbr.shape[1]), row(d),
                  full(wb0), full(wb1), full(wo), full(gain), full(wr), full(br)],
        out_specs=[row(d), pl.BlockSpec((tm, d // LANES, LANES), lambda i: (i, 0, 0)),
                   pl.BlockSpec((TOP_K, tm), lambda i: (0, i)), row(LANES)],
        out_shape=[jax.ShapeDtypeStruct((t, d), F32), jax.ShapeDtypeStruct((t, d // LANES, LANES), BF16),
                   jax.ShapeDtypeStruct((TOP_K, t), I32), jax.ShapeDtypeStruct((t, LANES), F32)],
        compiler_params=_cparams(("arbitrary",)),
        name="merge_router",
    )(oa, ob, gbr, x, wb0, wb1, wo, gain, wr, br)


def _positions_kernel(eid_ref, dest_ref, counts_ref, rank_ref):
    nblk, width = eid_ref.shape
    ji = lax.broadcasted_iota(I32, (width, width), 0)
    si = lax.broadcasted_iota(I32, (width, width), 1)
    prefix = jnp.where(ji <= si, 1.0, 0.0).astype(BF16)
    expert = lax.broadcasted_iota(I32, (N_EXPERTS, width), 0)
    group = max(g for g in (8, 4, 2, 1) if nblk % g == 0)

    def onehot(i):
        return expert == eid_ref[pl.ds(i, 1), :]

    def rank_body(ig, run):
        first = pl.multiple_of(ig * group, group)
        ohs = [onehot(first + j) for j in range(group)]
        stacked = jnp.concatenate([jnp.where(oh, 1.0, 0.0) for oh in ohs], axis=0).astype(BF16)
        cum = _dot(stacked, prefix)
        ranks = []
        for j, oh in enumerate(ohs):
            cum_j = cum[j * N_EXPERTS:(j + 1) * N_EXPERTS, :] + run
            ranks.append(jnp.sum(jnp.where(oh, cum_j, 0.0), axis=0, keepdims=True) - 1.0)
            run = cum_j[:, width - 1:width]
        rank_ref[pl.ds(first, group), :] = jnp.concatenate(ranks, axis=0)
        return run

    counts = lax.fori_loop(0, nblk // group, rank_body, jnp.zeros((N_EXPERTS, 1), F32))
    counts_ref[...] = jnp.broadcast_to(counts, counts_ref.shape).astype(I32)
    c_hi = jnp.floor(counts * (1.0 / 256.0))
    c_lo = counts - 256.0 * c_hi
    ei = lax.broadcasted_iota(I32, (N_EXPERTS, N_EXPERTS), 0)
    ej = lax.broadcasted_iota(I32, (N_EXPERTS, N_EXPERTS), 1)
    strict = jnp.where(ej < ei, 1.0, 0.0).astype(BF16)
    digits = jnp.concatenate([jnp.broadcast_to(c_hi, (N_EXPERTS, LANES)),
                              jnp.broadcast_to(c_lo, (N_EXPERTS, LANES))], axis=1).astype(BF16)
    sums = _dot(strict, digits)
    start = 256.0 * sums[:, 0:1] + sums[:, LANES:LANES + 1]

    def dest_body(ig, carry):
        first = pl.multiple_of(ig * group, group)
        offs = [jnp.sum(jnp.where(onehot(first + j), start, 0.0), axis=0, keepdims=True) for j in range(group)]
        rows = pl.ds(first, group)
        dest_ref[rows, :] = (rank_ref[rows, :] + jnp.concatenate(offs, axis=0)).astype(I32)
        return carry

    lax.fori_loop(0, nblk // group, dest_body, 0)


def _positions(eid_blocks):
    nblk, width = eid_blocks.shape
    vm = lambda shape: pl.BlockSpec(shape, lambda: (0,) * len(shape))
    return pl.pallas_call(
        _positions_kernel,
        in_specs=[vm((nblk, width))],
        out_specs=[vm((nblk, width)), vm((N_EXPERTS, LANES))],
        out_shape=[jax.ShapeDtypeStruct((nblk, width), I32), jax.ShapeDtypeStruct((N_EXPERTS, LANES), I32)],
        scratch_shapes=[pltpu.VMEM((nblk, width), F32)],
        name="positions",
    )(eid_blocks)


def _dispatch_kernel(n_prompt_tiles, dest_ref, hp_ref, hs_ref, xs_ref, sem):
    i = pl.program_id(0)
    tm = dest_ref.shape[1]

    def scatter(src_ref):
        def start(r, c):
            for k in range(TOP_K):
                pltpu.make_async_copy(src_ref.at[r], xs_ref.at[dest_ref[k, r]], sem).start(priority=k)
            return c

        lax.fori_loop(0, tm, start, 0, unroll=DMA_UNROLL)
        for k in range(TOP_K):
            pltpu.make_async_copy(src_ref, xs_ref.at[pl.ds(0, tm)], sem).wait()

    @pl.when(i < n_prompt_tiles)
    def _():
        scatter(hp_ref)

    @pl.when(i >= n_prompt_tiles)
    def _():
        scatter(hs_ref)


def _dispatch(dest, h_prompt, h_sample):
    t = dest.shape[1]
    slab = h_prompt.shape[1:]
    tm = DISPATCH_TILE
    assert h_prompt.shape[0] % tm == 0 and h_sample.shape[0] % tm == 0
    npt = h_prompt.shape[0] // tm
    return pl.pallas_call(
        functools.partial(_dispatch_kernel, npt),
        grid=(t // tm,),
        in_specs=[pl.BlockSpec((TOP_K, tm), lambda i: (0, i), memory_space=pltpu.SMEM),
                  pl.BlockSpec((tm,) + slab, lambda i: (jnp.minimum(i, npt - 1), 0, 0)),
                  pl.BlockSpec((tm,) + slab, lambda i: (jnp.maximum(i - npt, 0), 0, 0))],
        out_specs=pl.BlockSpec(memory_space=pl.ANY),
        out_shape=jax.ShapeDtypeStruct((TOP_K * t,) + slab, h_prompt.dtype),
        scratch_shapes=[pltpu.SemaphoreType.DMA(())],
        compiler_params=_cparams(("arbitrary",)),
        name="dispatch",
    )(dest, h_prompt, h_sample)


def _experts_kernel(vblk_ref, vexp_ref, vlo_ref, vhi_ref, vnext_ref, vslot_ref, xs_ref, wg_ref, wu_ref, wd_ref,
                    ys_ref, wg32_ref, wu32_ref, wd32_ref, wg16_ref, wu16_ref, wd16_ref, sems):
    v = pl.program_id(0)
    lo = vlo_ref[v]
    hi = vhi_ref[v]
    prev = jnp.maximum(v - 1, 0)
    first = jnp.logical_or(v == 0, vblk_ref[v] != vblk_ref[prev])
    new_expert = jnp.logical_or(v == 0, vexp_ref[v] != vexp_ref[prev])

    def weight_copies(e, slot):
        return [pltpu.make_async_copy(src.at[e], dst.at[slot], sems.at[slot])
                for src, dst in ((wg_ref, wg32_ref), (wu_ref, wu32_ref), (wd_ref, wd32_ref))]

    @pl.when(v == 0)
    def _():
        for cp in weight_copies(vexp_ref[0], 0):
            cp.start()

    @pl.when(new_expert)
    def _():
        slot = vslot_ref[v]
        for cp in weight_copies(vexp_ref[v], slot):
            cp.wait()
        wg16_ref[...] = wg32_ref[slot].astype(BF16)
        wu16_ref[...] = wu32_ref[slot].astype(BF16)
        wd16_ref[...] = wd32_ref[slot].astype(BF16)

        @pl.when(vnext_ref[v] >= 0)
        def _():
            for cp in weight_copies(vnext_ref[v], 1 - slot):
                cp.start()

    @pl.when(first)
    def _():
        ys_ref[...] = jnp.zeros_like(ys_ref)

    @pl.when(hi > lo)
    def _():
        tm = xs_ref.shape[0]
        d = wg_ref.shape[1]
        x = xs_ref[...].reshape(tm, d)
        gate = _dot(x, wg16_ref[...])
        up = _dot(x, wu16_ref[...])
        hid = (gate * _sigmoid(gate) * up).astype(BF16)
        y = _dot(hid, wd16_ref[...]).astype(ys_ref.dtype)
        rows = lax.broadcasted_iota(I32, y.shape, 0)
        mine = (rows >= lo) & (rows < hi)
        ys_ref[...] = jnp.where(mine, y, ys_ref[...].reshape(tm, d)).reshape(ys_ref.shape)


def _experts(plan, xs, wg, wu, wd):
    a = xs.shape[0]
    slab = xs.shape[1:]
    d, de = wg.shape[1:]
    tm = MOE_TILE
    assert a % tm == 0 and slab == (d // LANES, LANES) and plan[0].shape[0] == a // tm + N_EXPERTS - 1
    block = lambda v, b, *_: (b[v], 0, 0)
    hbm = pl.BlockSpec(memory_space=pl.ANY)
    grid_spec = pltpu.PrefetchScalarGridSpec(
        num_scalar_prefetch=len(plan),
        grid=(plan[0].shape[0],),
        in_specs=[pl.BlockSpec((tm,) + slab, block), hbm, hbm, hbm],
        out_specs=pl.BlockSpec((tm,) + slab, block),
        scratch_shapes=[pltpu.VMEM((2, d, de), F32), pltpu.VMEM((2, d, de), F32), pltpu.VMEM((2, de, d), F32),
                        pltpu.VMEM((d, de), BF16), pltpu.VMEM((d, de), BF16), pltpu.VMEM((de, d), BF16),
                        pltpu.SemaphoreType.DMA((2,))],
    )
    return pl.pallas_call(
        _experts_kernel,
        grid_spec=grid_spec,
        out_shape=jax.ShapeDtypeStruct((a,) + slab, MOE_OUT_DTYPE),
        compiler_params=_cparams(("arbitrary",)),
        name="experts",
    )(*plan, xs, wg, wu, wd)


def _visit_plan(counts, n_rows):
    tm = MOE_TILE
    nblk = n_rows // tm
    n_visits = nblk + N_EXPERTS - 1
    ends = jnp.cumsum(counts)
    starts = ends - counts
    first_blk = starts // tm
    nvis = jnp.where(counts > 0, (ends + tm - 1) // tm - first_blk, 0)
    vis_end = jnp.cumsum(nvis)
    vis_start = vis_end - nvis
    v = jnp.arange(n_visits, dtype=I32)
    e = jnp.minimum(jnp.sum((vis_end[None, :] <= v[:, None]).astype(I32), axis=1), N_EXPERTS - 1)
    valid = v < vis_end[-1]
    blk = first_blk[e] + (v - vis_start[e])
    lo = jnp.clip(starts[e] - blk * tm, 0, tm)
    hi = jnp.clip(ends[e] - blk * tm, 0, tm)
    ids = jnp.arange(N_EXPERTS, dtype=I32)
    used = counts > 0
    last_e = jnp.max(jnp.where(used, ids, 0))
    blk = jnp.where(valid, blk, nblk - 1).astype(I32)
    e = jnp.where(valid, e, last_e).astype(I32)
    lo = jnp.where(valid, lo, 0).astype(I32)
    hi = jnp.where(valid, hi, 0).astype(I32)
    later_used = used[None, :] & (ids[None, :] > ids[:, None])
    next_used = jnp.min(jnp.where(later_used, ids[None, :], N_EXPERTS), axis=1)
    next_used = jnp.where(next_used < N_EXPERTS, next_used, -1).astype(I32)
    slot = ((jnp.cumsum(used.astype(I32)) - 1) % 2).astype(I32)
    return blk, e, lo, hi, next_used[e], slot[e]


def _combine_kernel(dest_ref, ys_ref, x1_ref, wcol_ref, gain_ref, out_ref, buf_ref, sems):
    tm, d = x1_ref.shape
    part = tm // COMBINE_PARTS

    def start(r, c, sem):
        for k in range(TOP_K):
            pltpu.make_async_copy(ys_ref.at[dest_ref[k, r]], buf_ref.at[k, r], sem).start(priority=k)
        return c

    for h in range(COMBINE_PARTS):
        lax.fori_loop(h * part, (h + 1) * part, functools.partial(start, sem=sems.at[h]), 0, unroll=DMA_UNROLL)
    for h in range(COMBINE_PARTS):
        rows = pl.ds(h * part, part)
        for k in range(TOP_K):
            pltpu.make_async_copy(ys_ref.at[rows], buf_ref.at[k, rows], sems.at[h]).wait()
        y = (wcol_ref[rows, 0:1] * buf_ref[0, rows].reshape(part, d).astype(F32)
             + wcol_ref[rows, 1:2] * buf_ref[1, rows].reshape(part, d).astype(F32))
        out_ref[rows, :] = _rms_norm(x1_ref[rows, :] + y, gain_ref[...])


def _combine(dest, first_token, ys, x1, wcol, gain):
    t, d = x1.shape
    tm = COMBINE_TILE
    assert t % tm == 0 and first_token % tm == 0
    off = first_token // tm
    row = lambda w: pl.BlockSpec((tm, w), lambda i: (i, 0))
    return pl.pallas_call(
        _combine_kernel,
        grid=(t // tm,),
        in_specs=[pl.BlockSpec((TOP_K, tm), lambda i: (0, i + off), memory_space=pltpu.SMEM),
                  pl.BlockSpec(memory_space=pl.ANY), row(d), row(LANES),
                  pl.BlockSpec(gain.shape, lambda i: (0, 0))],
        out_specs=row(d),
        out_shape=jax.ShapeDtypeStruct((t, d), F32),
        scratch_shapes=[pltpu.VMEM((TOP_K, tm) + ys.shape[1:], ys.dtype), pltpu.SemaphoreType.DMA((COMBINE_PARTS,))],
        compiler_params=_cparams(("arbitrary",)),
        name="combine",
    )(dest, ys, x1, wcol, gain)


def _prepare_weights(w_in, w_gla_gate_up, b_gla_gate, w_branch, w_out, w_router_group, b_router_group,
                     w_router_expert, b_router_expert):
    d = w_in.shape[0]
    qk = H_A * DK_A
    mw = H_A * DV_A
    c = 0
    w_qa, c = w_in[:, c:c + qk], c + qk
    w_ka, c = w_in[:, c:c + qk], c + qk
    w_va, c = w_in[:, c:c + mw], c + mw
    w_ra, c = w_in[:, c:c + mw], c + mw
    w_lr, c = w_in[:, c:c + GATE_RANK], c + GATE_RANK
    w_b, c = w_in[:, c:c + 3 * mw], c + 3 * mw
    w_g = w_in[:, c:]
    wa = jnp.concatenate([w_qa, w_ka, w_va, w_ra,
                          jnp.pad(w_lr, ((0, 0), (0, LANES - GATE_RANK)))], axis=1).astype(BF16)
    wgu = jnp.pad(w_gla_gate_up, ((0, LANES - GATE_RANK), (0, 0))).astype(BF16)
    bgu = b_gla_gate[None, :]
    wr = jnp.zeros((LANES, d), F32)
    wr = wr.at[0:N_GROUPS].set(w_router_group.T).at[8:8 + N_EXPERTS].set(w_router_expert.T)
    br = jnp.zeros((LANES,), F32).at[0:N_GROUPS].set(b_router_group).at[8:8 + N_EXPERTS].set(b_router_expert)
    br = jnp.broadcast_to(br[:, None], (LANES, LANES))
    return dict(wa=wa, wqb=w_b[:, 0:mw].astype(BF16), wkvt=w_b[:, mw:3 * mw].T.astype(BF16),
                wg=w_g.astype(BF16), wgu=wgu, bgu=bgu,
                wb0=w_branch[0].astype(BF16), wb1=w_branch[1].astype(BF16), wo=w_out.astype(BF16),
                wr=wr, br=br)


def _mixers(x, s0, k_past, v_past, w, norm_mix_gain, gla_norm_gain, norm_ffn_gain):
    b, s, d = x.shape
    xf = x.reshape(b * s, d)
    qa, ka, va, ra, la, qb, kt, vt, kt16, vt16, gbr = _in_projection(
        xf, s, norm_mix_gain[None, :], w["wa"], w["wqb"], w["wkvt"], w["wg"], w["wgu"], w["bgu"])
    seq = lambda a: a.reshape(b, s, a.shape[-1])
    oa, s_new = _gla(seq(qa), seq(ka), seq(va), seq(ra), seq(la), s0, gla_norm_gain[None, :],
                     min(s, GLA_ROWS))
    to_channel_major = lambda a: jnp.transpose(a, (0, 2, 3, 1))
    if k_past is None:
        ob = _sb_prompt(seq(qb), kt16, vt16)
    else:
        ob = _sb_sample(seq(qb), kt16, vt16, to_channel_major(k_past), to_channel_major(v_past))
    x1, h2, eid, wcol = _merge(oa.reshape(b * s, -1), ob.reshape(b * s, -1), gbr, xf, w["wb0"], w["wb1"],
                               w["wo"], norm_ffn_gain[None, :], w["wr"], w["br"])
    from_channel_major = lambda a: jnp.transpose(a.reshape(b, H_B, DH_B, s), (0, 3, 1, 2))
    return x1, h2, eid, wcol, s_new, from_channel_major(kt), from_channel_major(vt)


def kernel(x_prompt, x_sample, state_gla, cache_sb_k, cache_sb_v, norm_mix_gain, w_in, w_gla_gate_up, b_gla_gate, gla_norm_gain, w_branch, w_out, norm_ffn_gain, w_router_group, b_router_group, w_router_expert, b_router_expert, w_exp_gate, w_exp_up, w_exp_down, norm_final_gain):
    depth = w_in.shape[0]
    assert depth == 1, "one trunk layer per step"
    l = 0
    w = _prepare_weights(w_in[l], w_gla_gate_up[l], b_gla_gate[l], w_branch[l], w_out[l], w_router_group[l],
                         b_router_group[l], w_router_expert[l], b_router_expert[l])
    bp, sp, d = x_prompt.shape
    bs, ss, _ = x_sample.shape
    s0 = jnp.zeros((bp, H_A, DK_A, DV_A), x_prompt.dtype)
    x1p, h2p, eidp, wcolp, gla_p, k_p, v_p = _mixers(
        x_prompt, s0, None, None, w, norm_mix_gain[l], gla_norm_gain[l], norm_ffn_gain[l])
    x1s, h2s, eids, wcols, gla_s, k_s, v_s = _mixers(
        x_sample, state_gla[l], cache_sb_k[l], cache_sb_v[l], w, norm_mix_gain[l], gla_norm_gain[l],
        norm_ffn_gain[l])

    tp, ts = bp * sp, bs * ss
    eid = jnp.concatenate([eidp, eids], axis=1)
    dest_blocks, counts = _positions(eid.reshape(-1, SORT_WIDTH))
    dest = dest_blocks.reshape(TOP_K, tp + ts)
    xs = _dispatch(dest, h2p, h2s)
    plan = _visit_plan(counts[:, 0], TOP_K * (tp + ts))
    ys = _experts(plan, xs, w_exp_gate[l], w_exp_up[l], w_exp_down[l])
    gf = norm_final_gain[None, :]
    y_prompt = _combine(dest, 0, ys, x1p, wcolp, gf).reshape(bp, sp, d)
    y_sample = _combine(dest, tp, ys, x1s, wcols, gf).reshape(bs, ss, d)
    return (y_prompt, y_sample, gla_p[None], k_p[None], v_p[None], gla_s[None], k_s[None], v_s[None])
```

```python
import functools

import jax
import jax.numpy as jnp
from jax import lax
from jax.experimental import pallas as pl
from jax.experimental.pallas import tpu as pltpu

F32 = jnp.float32
BF16 = jnp.bfloat16
MOE_OUT_DTYPE = jnp.bfloat16
I32 = jnp.int32

LANES = 128
LOG2_E = 1.4426950408889634
RMS_EPS = 1e-6
GATE_TAU = 16.0
H_A = 4
DK_A = 64
DV_A = 128
GATE_RANK = 16
H_B = 8
DH_B = 64
N_GROUPS = 4
EXPERTS_PER_GROUP = 8
N_EXPERTS = N_GROUPS * EXPERTS_PER_GROUP
TOP_K = 2
GLA_CHUNK = 64
GLA_SUB = 16
GLA_EXP_CLAMP = 80.0
GLA_SEQS = 4
GLA_ROWS = 256
SB_TILE = 256
SB_SAMPLE_SEQS = 2
SB_SAMPLE_TILES_PER_TRIP = 8
MOE_TILE = 512
SORT_WIDTH = 256
INPROJ_TILE = 512
MERGE_TILE = 1024
DISPATCH_TILE = 2048
COMBINE_TILE = 512
COMBINE_PARTS = 4
DMA_UNROLL = 32
VMEM_LIMIT = 56 * 1024 * 1024


def _cparams(sem):
    return pltpu.CompilerParams(dimension_semantics=sem, vmem_limit_bytes=VMEM_LIMIT)


def _dot(a, b):
    return jnp.dot(a, b, preferred_element_type=F32)


def _dot_nt(a, b):
    return lax.dot_general(a, b, (((1,), (1,)), ((), ())), preferred_element_type=F32)


def _dot_tn(a, b):
    return lax.dot_general(a, b, (((0,), (0,)), ((), ())), preferred_element_type=F32)


def _split_bf16(x):
    hi = x.astype(BF16)
    lo = (x - hi.astype(F32)).astype(BF16)
    return hi, lo


def _log_sigmoid(x):
    return jnp.minimum(x, 0.0) - jnp.log(1.0 + jnp.exp(-jnp.abs(x)))


def _sigmoid(x):
    return 0.5 * jnp.tanh(0.5 * x) + 0.5


def _rms_norm(x, gain):
    return x * lax.rsqrt(jnp.mean(x * x, axis=-1, keepdims=True) + RMS_EPS) * gain


def _inproj_kernel(x_ref, gain_ref, wa_ref, wqb_ref, wkvt_ref, wg_ref, wgu_ref, bgu_ref,
                   qa_ref, ka_ref, va_ref, ra_ref, la_ref, qb_ref, kt_ref, vt_ref,
                   kt16_ref, vt16_ref, gbr_ref):
    h = _rms_norm(x_ref[...], gain_ref[...]).astype(BF16)
    pa = H_A * DK_A
    mw = va_ref.shape[-1]
    kvt = _dot_nt(wkvt_ref[...], h)
    nseq, _, s = kt_ref.shape
    ntile, tile = kt16_ref.shape[1], kt16_ref.shape[3]
    for i in range(nseq):
        kt_ref[i] = kvt[0:mw, i * s:(i + 1) * s]
        vt_ref[i] = kvt[mw:2 * mw, i * s:(i + 1) * s]
        for j in range(ntile):
            cols = slice(i * s + j * tile, i * s + (j + 1) * tile)
            kt16_ref[i, j] = kvt[0:mw, cols].astype(BF16)
            vt16_ref[i, j] = kvt[mw:2 * mw, cols].astype(BF16)
    qb_ref[...] = _dot(h, wqb_ref[...]).astype(BF16)
    qa_ref[...] = _dot(h, wa_ref[:, 0:pa])
    ka_ref[...] = _dot(h, wa_ref[:, pa:2 * pa])
    va_ref[...] = _dot(h, wa_ref[:, 2 * pa:2 * pa + mw])
    ra_ref[...] = _dot(h, wa_ref[:, 2 * pa + mw:2 * pa + 2 * mw])
    lr = _dot(h, wa_ref[:, 2 * pa + 2 * mw:2 * pa + 2 * mw + LANES])
    gl = _dot(lr.astype(BF16), wgu_ref[...]) + bgu_ref[...]
    la_ref[...] = _log_sigmoid(gl) / GATE_TAU
    gbr_ref[...] = _dot(h, wg_ref[...]).astype(gbr_ref.dtype)


def _in_projection(x, seq_len, gain, wa, wqb, wkvt, wg, wgu, bgu):
    t, d = x.shape
    nb = t // seq_len
    pa = H_A * DK_A
    mw = wqb.shape[1]
    tm = INPROJ_TILE
    assert t % tm == 0
    row = lambda w: pl.BlockSpec((tm, w), lambda i: (i, 0))
    full = lambda a: pl.BlockSpec(a.shape, lambda i: (0,) * a.ndim, pipeline_mode=pl.Buffered(1))
    if seq_len >= tm:
        per_seq = seq_len // tm
        ntile = tm // SB_TILE
        assert tm % SB_TILE == 0 and seq_len % tm == 0
        kt_spec = pl.BlockSpec((1, mw, tm), lambda i: (i // per_seq, 0, i % per_seq))
        kt16_spec = pl.BlockSpec((1, ntile, mw, SB_TILE), lambda i: (i // per_seq, i % per_seq, 0, 0))
        kt16_shape = (nb, seq_len // SB_TILE, mw, SB_TILE)
    else:
        nseq = tm // seq_len
        assert tm % seq_len == 0
        kt_spec = pl.BlockSpec((nseq, mw, seq_len), lambda i: (i, 0, 0))
        kt16_spec = pl.BlockSpec((nseq, 1, mw, seq_len), lambda i: (i, 0, 0, 0))
        kt16_shape = (nb, 1, mw, seq_len)
    outs = [
        (jax.ShapeDtypeStruct((t, pa), F32), row(pa)), (jax.ShapeDtypeStruct((t, pa), F32), row(pa)),
        (jax.ShapeDtypeStruct((t, mw), F32), row(mw)), (jax.ShapeDtypeStruct((t, mw), F32), row(mw)),
        (jax.ShapeDtypeStruct((t, pa), F32), row(pa)),
        (jax.ShapeDtypeStruct((t, mw), BF16), row(mw)),
        (jax.ShapeDtypeStruct((nb, mw, seq_len), F32), kt_spec), (jax.ShapeDtypeStruct((nb, mw, seq_len), F32), kt_spec),
        (jax.ShapeDtypeStruct(kt16_shape, BF16), kt16_spec), (jax.ShapeDtypeStruct(kt16_shape, BF16), kt16_spec),
        (jax.ShapeDtypeStruct((t, wg.shape[1]), BF16), row(wg.shape[1])),
    ]
    return pl.pallas_call(
        _inproj_kernel,
        grid=(t // tm,),
        in_specs=[row(d), full(gain), full(wa), full(wqb), full(wkvt), full(wg), full(wgu), full(bgu)],
        out_specs=[spec for _, spec in outs],
        out_shape=[shape for shape, _ in outs],
        compiler_params=_cparams(("arbitrary",)),
        name="in_projection",
    )(x, gain, wa, wqb, wkvt, wg, wgu, bgu)


def _gla_chunk(q, k, v, b, st):
    c = q.shape[0]
    b_last = b[c - 1:c, :]
    rows = lax.broadcasted_iota(I32, (c, LANES), 0)
    nsub = c // GLA_SUB
    refs = [jnp.zeros((1, LANES), F32)] + [b[i * GLA_SUB - 1:i * GLA_SUB, :] for i in range(1, nsub)]
    ref_rows = refs[0]
    for i in range(1, nsub):
        ref_rows = jnp.where(rows >= i * GLA_SUB, refs[i], ref_rows)
    q_rel = q * jnp.exp(b - ref_rows)
    lhs = jnp.concatenate(
        [jnp.where((rows >= i * GLA_SUB) & (rows < (i + 1) * GLA_SUB), q_rel, 0.0) for i in range(nsub)],
        axis=1).astype(BF16)
    rhs = jnp.concatenate(
        [jnp.where(rows < (i + 1) * GLA_SUB, k * jnp.exp(jnp.minimum(refs[i] - b, GLA_EXP_CLAMP)), 0.0)
         for i in range(nsub)], axis=1).astype(BF16)
    att = _dot_nt(lhs, rhs)
    tt = lax.broadcasted_iota(I32, (c, c), 0)
    ss = lax.broadcasted_iota(I32, (c, c), 1)
    att = jnp.where(ss <= tt, att, 0.0)
    v16 = v.astype(BF16)
    inter = _dot_nt((q * jnp.exp(b)).astype(BF16), st.astype(BF16))
    intra = _dot(att.astype(BF16), v16)
    kd = (k * jnp.exp(b_last - b)).astype(BF16)
    st_new = st * jnp.exp(b_last) + _dot_tn(v16, kd)
    return inter + intra, st_new


def _gla_kernel(qa_ref, ka_ref, va_ref, ra_ref, la_ref, s0_ref, gain_ref, o_ref, sfin_ref, st_ref):
    j = pl.program_id(1)
    nj = pl.num_programs(1)
    nseq, rows_per_step, _ = qa_ref.shape
    c = GLA_CHUNK
    zpad = jnp.zeros((LANES - DK_A, DV_A), F32)

    def state_rows(h):
        return slice((h % 2) * DK_A, (h % 2 + 1) * DK_A)

    @pl.when(j == 0)
    def _():
        for si in range(nseq):
            for h in range(H_A):
                parts = [s0_ref[si, h], zpad] if h % 2 == 0 else [zpad, s0_ref[si, h]]
                st_ref[si * H_A + h] = jnp.concatenate(parts, axis=0).T

    ti = lax.broadcasted_iota(I32, (rows_per_step, rows_per_step), 0)
    si = lax.broadcasted_iota(I32, (rows_per_step, rows_per_step), 1)
    chunk_shift = c.bit_length() - 1
    same_chunk = (ti >> chunk_shift) == (si >> chunk_shift)
    tril_blocks = jnp.where(same_chunk & (si <= ti), 1.0, 0.0).astype(BF16)
    gain = gain_ref[...]
    lane = lax.broadcasted_iota(I32, (1, LANES), 1)
    half_masks = (lane < DK_A, lane >= DK_A)
    for si in range(nseq):
        la_hi, la_lo = _split_bf16(la_ref[si])
        b_all = _dot(tril_blocks, la_hi) + _dot(tril_blocks, la_lo)
        for h in range(H_A):
            hp = slice((h // 2) * LANES, (h // 2 + 1) * LANES)
            hv = slice(h * DV_A, (h + 1) * DV_A)
            mine = half_masks[h % 2]
            st = st_ref[si * H_A + h]
            for ci in range(rows_per_step // c):
                r0 = ci * c
                q = jnp.where(mine, qa_ref[si, r0:r0 + c, hp], 0.0) * (DK_A ** -0.5)
                k = jnp.where(mine, ka_ref[si, r0:r0 + c, hp], 0.0)
                o, st = _gla_chunk(q, k, va_ref[si, r0:r0 + c, hv], b_all[r0:r0 + c, hp], st)
                r = ra_ref[si, r0:r0 + c, hv]
                o = _rms_norm(o, gain) * (r * _sigmoid(r))
                o_ref[si, r0:r0 + c, hv] = o.astype(o_ref.dtype)
            st_ref[si * H_A + h] = st

    @pl.when(j == nj - 1)
    def _():
        for si in range(nseq):
            for h in range(H_A):
                sfin_ref[si, h] = st_ref[si * H_A + h].T[state_rows(h), :]


def _gla(qa, ka, va, ra, la, s0, gain, rows_per_step):
    b, s, pa = qa.shape
    assert 2 * DK_A == LANES and H_A % 2 == 0 and DV_A == LANES
    assert s % rows_per_step == 0 and rows_per_step % GLA_CHUNK == 0 and GLA_CHUNK % GLA_SUB == 0
    mw = va.shape[-1]
    ns = GLA_SEQS
    assert b % ns == 0
    seq = lambda w: pl.BlockSpec((ns, rows_per_step, w), lambda i, j: (i, j, 0))
    state = pl.BlockSpec((ns, H_A, DK_A, DV_A), lambda i, j: (i, 0, 0, 0))
    return pl.pallas_call(
        _gla_kernel,
        grid=(b // ns, s // rows_per_step),
        in_specs=[seq(pa), seq(pa), seq(mw), seq(mw), seq(pa), state,
                  pl.BlockSpec(gain.shape, lambda i, j: (0, 0))],
        out_specs=[seq(mw), state],
        out_shape=[jax.ShapeDtypeStruct((b, s, mw), BF16),
                   jax.ShapeDtypeStruct((b, H_A, DK_A, DV_A), F32)],
        scratch_shapes=[pltpu.VMEM((ns * H_A, LANES, LANES), F32)],
        compiler_params=_cparams(("arbitrary", "arbitrary")),
        name="gla",
    )(qa, ka, va, ra, la, s0, gain)


def _head_lane_masks():
    lane = lax.broadcasted_iota(I32, (1, LANES), 1)
    return lane < DH_B, lane >= DH_B


def _sb_neg_tri(tk):
    ji = lax.broadcasted_iota(I32, (tk, tk), 0)
    si = lax.broadcasted_iota(I32, (tk, tk), 1)
    return jnp.where(ji >= si, -1.0, 0.0).astype(BF16)


def _sb_stack_queries(q, qs_ref, base=0):
    m0, m1 = _head_lane_masks()
    for p in range(q.shape[1] // LANES):
        qp = (q[:, p * LANES:(p + 1) * LANES].astype(F32) * (DH_B ** -0.5 * LOG2_E)).astype(BF16)
        zero = jnp.zeros_like(qp)
        qs_ref[base + p] = jnp.concatenate([jnp.where(m0, qp, zero), jnp.where(m1, qp, zero)], axis=0)


def _pair_lanes(p):
    return slice(p * LANES, (p + 1) * LANES)


def _lane_fit(x, width):
    if width >= LANES:
        return jnp.concatenate([x] * (width // LANES), axis=1)
    return x[:, 0:width]


def _sb_tile_step(qs_ref, acc_ref, carry_ref, k_tile, v_tile, ntri, diagonal, one_suffix_matmul=False):
    npair, rows, _ = qs_ref.shape
    tq = rows // 2
    tk = ntri.shape[1]
    m0, _ = _head_lane_masks()
    if diagonal:
        t = lax.broadcasted_iota(I32, (rows, tk), 0)
        t = jnp.where(t >= tq, t - tq, t)
        visible = lax.broadcasted_iota(I32, (rows, tk), 1) < t
    def scores(p):
        z = _dot(qs_ref[p], k_tile(p))
        sp = jnp.maximum(z, 0.0) + jnp.log2(1.0 + jnp.exp2(-jnp.abs(z)))
        if diagonal:
            sp = jnp.where(visible, sp, 0.0)
        return z, sp.astype(BF16)

    if one_suffix_matmul:
        zs, sps = zip(*[scores(p) for p in range(npair)])
        stacked = _dot(jnp.concatenate(sps, axis=0), ntri)
        suffixes = [stacked[p * rows:(p + 1) * rows] for p in range(npair)]
    for p in range(npair):
        if one_suffix_matmul:
            z, suffix = zs[p], suffixes[p]
        else:
            z, sp = scores(p)
            suffix = _dot(sp, ntri)
        carry = carry_ref[p]
        w = jnp.exp2(z + suffix + _lane_fit(carry, tk))
        if diagonal:
            w = jnp.where(visible, w, 0.0)
        pv = _dot_nt(w.astype(BF16), v_tile(p))
        acc_ref[p] += jnp.where(m0, pv[0:tq], pv[tq:rows])
        carry_ref[p] = carry + jnp.broadcast_to(suffix[:, 0:1], carry.shape)


def _sb_prompt_kernel(q_ref, k_ref, v_ref, o_ref, qs_ref, acc_ref, carry_ref):
    qi = pl.program_id(1)
    tk = SB_TILE
    _sb_stack_queries(q_ref[0], qs_ref)
    acc_ref[...] = jnp.zeros_like(acc_ref)
    carry_ref[...] = jnp.zeros_like(carry_ref)
    ntri = _sb_neg_tri(tk)

    def step(jb, diagonal):
        _sb_tile_step(qs_ref, acc_ref, carry_ref, lambda p: k_ref[0, jb, _pair_lanes(p), :],
                      lambda p: v_ref[0, jb, _pair_lanes(p), :], ntri, diagonal, True)

    step(qi, True)

    def body(i, c):
        for j in range(4):
            step(qi - 1 - 4 * i - j, False)
        return c

    lax.fori_loop(0, qi // 4, body, 0)
    left = qi % 4

    @pl.when(left >= 2)
    def _():
        step(left - 1, False)
        step(left - 2, False)

    @pl.when(left % 2 == 1)
    def _():
        step(0, False)

    for p in range(acc_ref.shape[0]):
        o_ref[0, :, p * LANES:(p + 1) * LANES] = acc_ref[p].astype(o_ref.dtype)


def _sb_scratch(tq, npair):
    return [pltpu.VMEM((npair, 2 * tq, LANES), BF16), pltpu.VMEM((npair, tq, LANES), F32),
            pltpu.VMEM((npair, 2 * tq, LANES), F32)]


def _sb_prompt(q, kt, vt):
    b, s, w = q.shape
    tq = SB_TILE
    assert s % tq == 0 and w % LANES == 0 and 2 * DH_B == LANES and kt.shape == (b, s // tq, w, tq)
    qspec = pl.BlockSpec((1, tq, w), lambda i, j: (i, j, 0))
    kvspec = pl.BlockSpec((1,) + kt.shape[1:], lambda i, j: (i, 0, 0, 0))
    return pl.pallas_call(
        _sb_prompt_kernel,
        grid=(b, s // tq),
        in_specs=[qspec, kvspec, kvspec],
        out_specs=qspec,
        out_shape=jax.ShapeDtypeStruct((b, s, w), BF16),
        scratch_shapes=_sb_scratch(tq, w // LANES),
        compiler_params=_cparams(("arbitrary", "arbitrary")),
        name="sb_prompt",
    )(q, kt, vt)


def _sb_sample_kernel(q_ref, kn_ref, vn_ref, kp_ref, vp_ref, o_ref, qs_ref, acc_ref, carry_ref):
    nseq, sq, w = q_ref.shape
    past = kp_ref.shape[3]
    npair = w // LANES
    tk = SB_TILE
    for si in range(nseq):
        _sb_stack_queries(q_ref[si], qs_ref, si * npair)
    acc_ref[...] = jnp.zeros_like(acc_ref)
    carry_ref[...] = jnp.zeros_like(carry_ref)
    _sb_tile_step(qs_ref, acc_ref, carry_ref, lambda e: kn_ref[e // npair, 0, _pair_lanes(e % npair), :],
                  lambda e: vn_ref[e // npair, 0, _pair_lanes(e % npair), :], _sb_neg_tri(sq), True, True)
    ntri = _sb_neg_tri(tk)

    def tile(j):
        cols = pl.ds(pl.multiple_of(past - (j + 1) * tk, tk), tk)

        def pair(ref, e):
            p = e % npair
            return ref[e // npair, 2 * p:2 * p + 2, :, cols].reshape(LANES, tk).astype(BF16)

        _sb_tile_step(qs_ref, acc_ref, carry_ref, lambda e: pair(kp_ref, e), lambda e: pair(vp_ref, e), ntri, False,
                      True)

    def body(i, c):
        for j in range(SB_SAMPLE_TILES_PER_TRIP):
            tile(SB_SAMPLE_TILES_PER_TRIP * i + j)
        return c

    lax.fori_loop(0, past // tk // SB_SAMPLE_TILES_PER_TRIP, body, 0)
    for e in range(acc_ref.shape[0]):
        o_ref[e // npair, :, _pair_lanes(e % npair)] = acc_ref[e].astype(o_ref.dtype)


def _sb_sample(q, kt_new, vt_new, kt_past, vt_past):
    b, sq, w = q.shape
    past = kt_past.shape[3]
    ns = SB_SAMPLE_SEQS
    assert past % (SB_SAMPLE_TILES_PER_TRIP * SB_TILE) == 0 and 2 * DH_B == LANES and b % ns == 0
    qspec = pl.BlockSpec((ns, sq, w), lambda i: (i, 0, 0))
    new = pl.BlockSpec((ns, 1, w, sq), lambda i: (i, 0, 0, 0))
    old = pl.BlockSpec((ns, H_B, DH_B, past), lambda i: (i, 0, 0, 0))
    return pl.pallas_call(
        _sb_sample_kernel,
        grid=(b // ns,),
        in_specs=[qspec, new, new, old, old],
        out_specs=qspec,
        out_shape=jax.ShapeDtypeStruct((b, sq, w), BF16),
        scratch_shapes=_sb_scratch(sq, ns * (w // LANES)),
        compiler_params=_cparams(("arbitrary",)),
        name="sb_sample",
    )(q, kt_new, vt_new, kt_past, vt_past)


def _first_argmax(vals, nrows):
    idx = lax.broadcasted_iota(I32, vals.shape, 0)
    top = jnp.max(vals, axis=0, keepdims=True)
    first = jnp.min(jnp.where(vals == top, idx, nrows), axis=0, keepdims=True)
    return top, first, idx


def _merge_kernel(oa_ref, ob_ref, g_ref, x_ref, wb0_ref, wb1_ref, wo_ref, gain_ref, wr_ref, br_ref,
                  x1_ref, h2_ref, eid_ref, wcol_ref):
    d = x_ref.shape[1]
    ya = _dot(oa_ref[...], wb0_ref[...])
    yb = _dot(ob_ref[...], wb1_ref[...])
    g = g_ref[...].astype(F32)
    m = _sigmoid(g[:, 0:d]) * ya + _sigmoid(g[:, d:2 * d]) * yb
    x1 = x_ref[...] + _dot(m.astype(BF16), wo_ref[...])
    x1_ref[...] = x1
    h2 = _rms_norm(x1, gain_ref[...])
    h2_ref[...] = h2.astype(h2_ref.dtype).reshape(h2_ref.shape)

    h_hi, h_lo = _split_bf16(h2)
    w_hi, w_lo = _split_bf16(wr_ref[...])
    lt = _dot_nt(w_hi, h_hi) + _dot_nt(w_hi, h_lo) + _dot_nt(w_lo, h_hi) + br_ref[:, 0:1]
    gl = lt[0:N_GROUPS, :]
    g_top, g_idx, _ = _first_argmax(gl, N_GROUPS)
    g_e = jnp.exp(gl - g_top)
    g_p = jnp.max(g_e / jnp.sum(g_e, axis=0, keepdims=True), axis=0, keepdims=True)
    el = jnp.zeros((EXPERTS_PER_GROUP, lt.shape[1]), F32)
    for g in range(N_GROUPS):
        r0 = 8 + g * EXPERTS_PER_GROUP
        el = jnp.where(g_idx == g, lt[r0:r0 + EXPERTS_PER_GROUP, :], el)
    e_top, i1, eidx = _first_argmax(el, EXPERTS_PER_GROUP)
    e_e = jnp.exp(el - e_top)
    e_p = e_e / jnp.sum(e_e, axis=0, keepdims=True)
    p1 = jnp.max(e_p, axis=0, keepdims=True)
    rest = jnp.where(eidx == i1, -1.0, e_p)
    p2, i2, _ = _first_argmax(rest, EXPERTS_PER_GROUP)
    norm = p1 + p2
    w1 = g_p * (p1 / norm)
    w2 = g_p * (p2 / norm)
    eid_ref[...] = jnp.concatenate([g_idx * EXPERTS_PER_GROUP + i1, g_idx * EXPERTS_PER_GROUP + i2], axis=0)
    rows = lax.broadcasted_iota(I32, (LANES, lt.shape[1]), 0)
    wrows = jnp.where(rows == 0, w1, jnp.where(rows == 1, w2, 0.0))
    wcol_ref[...] = wrows.T


def _merge(oa, ob, gbr, x, wb0, wb1, wo, gain, wr, br):
    t, d = x.shape
    tm = MERGE_TILE
    assert t % tm == 0
    row = lambda w: pl.BlockSpec((tm, w), lambda i: (i, 0))
    full = lambda a: pl.BlockSpec(a.shape, lambda i: (0,) * a.ndim)
    return pl.pallas_call(
        _merge_kernel,
        grid=(t // tm,),
        in_specs=[row(oa.shape[1]), row(ob.shape[1]), row(gbr.shape[1]), row(d),
                  full(wb0), full(wb1), full(wo), full(gain), full(wr), full(br)],
        out_specs=[row(d), pl.BlockSpec((tm, d // LANES, LANES), lambda i: (i, 0, 0)),
                   pl.BlockSpec((TOP_K, tm), lambda i: (0, i)), row(LANES)],
        out_shape=[jax.ShapeDtypeStruct((t, d), F32), jax.ShapeDtypeStruct((t, d // LANES, LANES), BF16),
                   jax.ShapeDtypeStruct((TOP_K, t), I32), jax.ShapeDtypeStruct((t, LANES), F32)],
        compiler_params=_cparams(("arbitrary",)),
        name="merge_router",
    )(oa, ob, gbr, x, wb0, wb1, wo, gain, wr, br)


def _positions_kernel(eid_ref, dest_ref, counts_ref, rank_ref):
    nblk, width = eid_ref.shape
    ji = lax.broadcasted_iota(I32, (width, width), 0)
    si = lax.broadcasted_iota(I32, (width, width), 1)
    prefix = jnp.where(ji <= si, 1.0, 0.0).astype(BF16)
    expert = lax.broadcasted_iota(I32, (N_EXPERTS, width), 0)
    group = max(g for g in (8, 4, 2, 1) if nblk % g == 0)

    def onehot(i):
        return expert == eid_ref[pl.ds(i, 1), :]

    def rank_body(ig, run):
        first = pl.multiple_of(ig * group, group)
        ohs = [onehot(first + j) for j in range(group)]
        stacked = jnp.concatenate([jnp.where(oh, 1.0, 0.0) for oh in ohs], axis=0).astype(BF16)
        cum = _dot(stacked, prefix)
        ranks = []
        for j, oh in enumerate(ohs):
            cum_j = cum[j * N_EXPERTS:(j + 1) * N_EXPERTS, :] + run
            ranks.append(jnp.sum(jnp.where(oh, cum_j, 0.0), axis=0, keepdims=True) - 1.0)
            run = cum_j[:, width - 1:width]
        rank_ref[pl.ds(first, group), :] = jnp.concatenate(ranks, axis=0)
        return run

    counts = lax.fori_loop(0, nblk // group, rank_body, jnp.zeros((N_EXPERTS, 1), F32))
    counts_ref[...] = jnp.broadcast_to(counts, counts_ref.shape).astype(I32)
    c_hi = jnp.floor(counts * (1.0 / 256.0))
    c_lo = counts - 256.0 * c_hi
    ei = lax.broadcasted_iota(I32, (N_EXPERTS, N_EXPERTS), 0)
    ej = lax.broadcasted_iota(I32, (N_EXPERTS, N_EXPERTS), 1)
    strict = jnp.where(ej < ei, 1.0, 0.0).astype(BF16)
    digits = jnp.concatenate([jnp.broadcast_to(c_hi, (N_EXPERTS, LANES)),
                              jnp.broadcast_to(c_lo, (N_EXPERTS, LANES))], axis=1).astype(BF16)
    sums = _dot(strict, digits)
    start = 256.0 * sums[:, 0:1] + sums[:, LANES:LANES + 1]

    def dest_body(ig, carry):
        first = pl.multiple_of(ig * group, group)
        offs = [jnp.sum(jnp.where(onehot(first + j), start, 0.0), axis=0, keepdims=True) for j in range(group)]
        rows = pl.ds(first, group)
        dest_ref[rows, :] = (rank_ref[rows, :] + jnp.concatenate(offs, axis=0)).astype(I32)
        return carry

    lax.fori_loop(0, nblk // group, dest_body, 0)


def _positions(eid_blocks):
    nblk, width = eid_blocks.shape
    vm = lambda shape: pl.BlockSpec(shape, lambda: (0,) * len(shape))
    return pl.pallas_call(
        _positions_kernel,
        in_specs=[vm((nblk, width))],
        out_specs=[vm((nblk, width)), vm((N_EXPERTS, LANES))],
        out_shape=[jax.ShapeDtypeStruct((nblk, width), I32), jax.ShapeDtypeStruct((N_EXPERTS, LANES), I32)],
        scratch_shapes=[pltpu.VMEM((nblk, width), F32)],
        name="positions",
    )(eid_blocks)


def _dispatch_kernel(n_prompt_tiles, dest_ref, hp_ref, hs_ref, xs_ref, sem):
    i = pl.program_id(0)
    tm = dest_ref.shape[1]

    def scatter(src_ref):
        def start(r, c):
            for k in range(TOP_K):
                pltpu.make_async_copy(src_ref.at[r], xs_ref.at[dest_ref[k, r]], sem).start(priority=k)
            return c

        lax.fori_loop(0, tm, start, 0, unroll=DMA_UNROLL)
        for k in range(TOP_K):
            pltpu.make_async_copy(src_ref, xs_ref.at[pl.ds(0, tm)], sem).wait()

    @pl.when(i < n_prompt_tiles)
    def _():
        scatter(hp_ref)

    @pl.when(i >= n_prompt_tiles)
    def _():
        scatter(hs_ref)


def _dispatch(dest, h_prompt, h_sample):
    t = dest.shape[1]
    slab = h_prompt.shape[1:]
    tm = DISPATCH_TILE
    assert h_prompt.shape[0] % tm == 0 and h_sample.shape[0] % tm == 0
    npt = h_prompt.shape[0] // tm
    return pl.pallas_call(
        functools.partial(_dispatch_kernel, npt),
        grid=(t // tm,),
        in_specs=[pl.BlockSpec((TOP_K, tm), lambda i: (0, i), memory_space=pltpu.SMEM),
                  pl.BlockSpec((tm,) + slab, lambda i: (jnp.minimum(i, npt - 1), 0, 0)),
                  pl.BlockSpec((tm,) + slab, lambda i: (jnp.maximum(i - npt, 0), 0, 0))],
        out_specs=pl.BlockSpec(memory_space=pl.ANY),
        out_shape=jax.ShapeDtypeStruct((TOP_K * t,) + slab, h_prompt.dtype),
        scratch_shapes=[pltpu.SemaphoreType.DMA(())],
        compiler_params=_cparams(("arbitrary",)),
        name="dispatch",
    )(dest, h_prompt, h_sample)


def _experts_kernel(vblk_ref, vexp_ref, vlo_ref, vhi_ref, vnext_ref, vslot_ref, xs_ref, wg_ref, wu_ref, wd_ref,
                    ys_ref, wg32_ref, wu32_ref, wd32_ref, wg16_ref, wu16_ref, wd16_ref, sems):
    v = pl.program_id(0)
    lo = vlo_ref[v]
    hi = vhi_ref[v]
    prev = jnp.maximum(v - 1, 0)
    first = jnp.logical_or(v == 0, vblk_ref[v] != vblk_ref[prev])
    new_expert = jnp.logical_or(v == 0, vexp_ref[v] != vexp_ref[prev])

    def weight_copies(e, slot):
        return [pltpu.make_async_copy(src.at[e], dst.at[slot], sems.at[slot])
                for src, dst in ((wg_ref, wg32_ref), (wu_ref, wu32_ref), (wd_ref, wd32_ref))]

    @pl.when(v == 0)
    def _():
        for cp in weight_copies(vexp_ref[0], 0):
            cp.start()

    @pl.when(new_expert)
    def _():
        slot = vslot_ref[v]
        for cp in weight_copies(vexp_ref[v], slot):
            cp.wait()
        wg16_ref[...] = wg32_ref[slot].astype(BF16)
        wu16_ref[...] = wu32_ref[slot].astype(BF16)
        wd16_ref[...] = wd32_ref[slot].astype(BF16)

        @pl.when(vnext_ref[v] >= 0)
        def _():
            for cp in weight_copies(vnext_ref[v], 1 - slot):
                cp.start()

    @pl.when(first)
    def _():
        ys_ref[...] = jnp.zeros_like(ys_ref)

    @pl.when(hi > lo)
    def _():
        tm = xs_ref.shape[0]
        d = wg_ref.shape[1]
        x = xs_ref[...].reshape(tm, d)
        gate = _dot(x, wg16_ref[...])
        up = _dot(x, wu16_ref[...])
        hid = (gate * _sigmoid(gate) * up).astype(BF16)
        y = _dot(hid, wd16_ref[...]).astype(ys_ref.dtype)
        rows = lax.broadcasted_iota(I32, y.shape, 0)
        mine = (rows >= lo) & (rows < hi)
        ys_ref[...] = jnp.where(mine, y, ys_ref[...].reshape(tm, d)).reshape(ys_ref.shape)


def _experts(plan, xs, wg, wu, wd):
    a = xs.shape[0]
    slab = xs.shape[1:]
    d, de = wg.shape[1:]
    tm = MOE_TILE
    assert a % tm == 0 and slab == (d // LANES, LANES) and plan[0].shape[0] == a // tm + N_EXPERTS - 1
    block = lambda v, b, *_: (b[v], 0, 0)
    hbm = pl.BlockSpec(memory_space=pl.ANY)
    grid_spec = pltpu.PrefetchScalarGridSpec(
        num_scalar_prefetch=len(plan),
        grid=(plan[0].shape[0],),
        in_specs=[pl.BlockSpec((tm,) + slab, block), hbm, hbm, hbm],
        out_specs=pl.BlockSpec((tm,) + slab, block),
        scratch_shapes=[pltpu.VMEM((2, d, de), F32), pltpu.VMEM((2, d, de), F32), pltpu.VMEM((2, de, d), F32),
                        pltpu.VMEM((d, de), BF16), pltpu.VMEM((d, de), BF16), pltpu.VMEM((de, d), BF16),
                        pltpu.SemaphoreType.DMA((2,))],
    )
    return pl.pallas_call(
        _experts_kernel,
        grid_spec=grid_spec,
        out_shape=jax.ShapeDtypeStruct((a,) + slab, MOE_OUT_DTYPE),
        compiler_params=_cparams(("arbitrary",)),
        name="experts",
    )(*plan, xs, wg, wu, wd)


def _visit_plan(counts, n_rows):
    tm = MOE_TILE
    nblk = n_rows // tm
    n_visits = nblk + N_EXPERTS - 1
    ends = jnp.cumsum(counts)
    starts = ends - counts
    first_blk = starts // tm
    nvis = jnp.where(counts > 0, (ends + tm - 1) // tm - first_blk, 0)
    vis_end = jnp.cumsum(nvis)
    vis_start = vis_end - nvis
    v = jnp.arange(n_visits, dtype=I32)
    e = jnp.minimum(jnp.sum((vis_end[None, :] <= v[:, None]).astype(I32), axis=1), N_EXPERTS - 1)
    valid = v < vis_end[-1]
    blk = first_blk[e] + (v - vis_start[e])
    lo = jnp.clip(starts[e] - blk * tm, 0, tm)
    hi = jnp.clip(ends[e] - blk * tm, 0, tm)
    ids = jnp.arange(N_EXPERTS, dtype=I32)
    used = counts > 0
    last_e = jnp.max(jnp.where(used, ids, 0))
    blk = jnp.where(valid, blk, nblk - 1).astype(I32)
    e = jnp.where(valid, e, last_e).astype(I32)
    lo = jnp.where(valid, lo, 0).astype(I32)
    hi = jnp.where(valid, hi, 0).astype(I32)
    later_used = used[None, :] & (ids[None, :] > ids[:, None])
    next_used = jnp.min(jnp.where(later_used, ids[None, :], N_EXPERTS), axis=1)
    next_used = jnp.where(next_used < N_EXPERTS, next_used, -1).astype(I32)
    slot = ((jnp.cumsum(used.astype(I32)) - 1) % 2).astype(I32)
    return blk, e, lo, hi, next_used[e], slot[e]


def _combine_kernel(dest_ref, ys_ref, x1_ref, wcol_ref, gain_ref, out_ref, buf_ref, sems):
    tm, d = x1_ref.shape
    part = tm // COMBINE_PARTS

    def start(r, c, sem):
        for k in range(TOP_K):
            pltpu.make_async_copy(ys_ref.at[dest_ref[k, r]], buf_ref.at[k, r], sem).start(priority=k)
        return c

    for h in range(COMBINE_PARTS):
        lax.fori_loop(h * part, (h + 1) * part, functools.partial(start, sem=sems.at[h]), 0, unroll=DMA_UNROLL)
    for h in range(COMBINE_PARTS):
        rows = pl.ds(h * part, part)
        for k in range(TOP_K):
            pltpu.make_async_copy(ys_ref.at[rows], buf_ref.at[k, rows], sems.at[h]).wait()
        y = (wcol_ref[rows, 0:1] * buf_ref[0, rows].reshape(part, d).astype(F32)
             + wcol_ref[rows, 1:2] * buf_ref[1, rows].reshape(part, d).astype(F32))
        out_ref[rows, :] = _rms_norm(x1_ref[rows, :] + y, gain_ref[...])


def _combine(dest, first_token, ys, x1, wcol, gain):
    t, d = x1.shape
    tm = COMBINE_TILE
    assert t % tm == 0 and first_token % tm == 0
    off = first_token // tm
    row = lambda w: pl.BlockSpec((tm, w), lambda i: (i, 0))
    return pl.pallas_call(
        _combine_kernel,
        grid=(t // tm,),
        in_specs=[pl.BlockSpec((TOP_K, tm), lambda i: (0, i + off), memory_space=pltpu.SMEM),
                  pl.BlockSpec(memory_space=pl.ANY), row(d), row(LANES),
                  pl.BlockSpec(gain.shape, lambda i: (0, 0))],
        out_specs=row(d),
        out_shape=jax.ShapeDtypeStruct((t, d), F32),
        scratch_shapes=[pltpu.VMEM((TOP_K, tm) + ys.shape[1:], ys.dtype), pltpu.SemaphoreType.DMA((COMBINE_PARTS,))],
        compiler_params=_cparams(("arbitrary",)),
        name="combine",
    )(dest, ys, x1, wcol, gain)


def _prepare_weights(w_in, w_gla_gate_up, b_gla_gate, w_branch, w_out, w_router_group, b_router_group,
                     w_router_expert, b_router_expert):
    d = w_in.shape[0]
    qk = H_A * DK_A
    mw = H_A * DV_A
    c = 0
    w_qa, c = w_in[:, c:c + qk], c + qk
    w_ka, c = w_in[:, c:c + qk], c + qk
    w_va, c = w_in[:, c:c + mw], c + mw
    w_ra, c = w_in[:, c:c + mw], c + mw
    w_lr, c = w_in[:, c:c + GATE_RANK], c + GATE_RANK
    w_b, c = w_in[:, c:c + 3 * mw], c + 3 * mw
    w_g = w_in[:, c:]
    wa = jnp.concatenate([w_qa, w_ka, w_va, w_ra,
                          jnp.pad(w_lr, ((0, 0), (0, LANES - GATE_RANK)))], axis=1).astype(BF16)
    wgu = jnp.pad(w_gla_gate_up, ((0, LANES - GATE_RANK), (0, 0))).astype(BF16)
    bgu = b_gla_gate[None, :]
    wr = jnp.zeros((LANES, d), F32)
    wr = wr.at[0:N_GROUPS].set(w_router_group.T).at[8:8 + N_EXPERTS].set(w_router_expert.T)
    br = jnp.zeros((LANES,), F32).at[0:N_GROUPS].set(b_router_group).at[8:8 + N_EXPERTS].set(b_router_expert)
    br = jnp.broadcast_to(br[:, None], (LANES, LANES))
    return dict(wa=wa, wqb=w_b[:, 0:mw].astype(BF16), wkvt=w_b[:, mw:3 * mw].T.astype(BF16),
                wg=w_g.astype(BF16), wgu=wgu, bgu=bgu,
                wb0=w_branch[0].astype(BF16), wb1=w_branch[1].astype(BF16), wo=w_out.astype(BF16),
                wr=wr, br=br)


def _mixers(x, s0, k_past, v_past, w, norm_mix_gain, gla_norm_gain, norm_ffn_gain):
    b, s, d = x.shape
    xf = x.reshape(b * s, d)
    qa, ka, va, ra, la, qb, kt, vt, kt16, vt16, gbr = _in_projection(
        xf, s, norm_mix_gain[None, :], w["wa"], w["wqb"], w["wkvt"], w["wg"], w["wgu"], w["bgu"])
    seq = lambda a: a.reshape(b, s, a.shape[-1])
    oa, s_new = _gla(seq(qa), seq(ka), seq(va), seq(ra), seq(la), s0, gla_norm_gain[None, :],
                     min(s, GLA_ROWS))
    to_channel_major = lambda a: jnp.transpose(a, (0, 2, 3, 1))
    if k_past is None:
        ob = _sb_prompt(seq(qb), kt16, vt16)
    else:
        ob = _sb_sample(seq(qb), kt16, vt16, to_channel_major(k_past), to_channel_major(v_past))
    x1, h2, eid, wcol = _merge(oa.reshape(b * s, -1), ob.reshape(b * s, -1), gbr, xf, w["wb0"], w["wb1"],
                               w["wo"], norm_ffn_gain[None, :], w["wr"], w["br"])
    from_channel_major = lambda a: jnp.transpose(a.reshape(b, H_B, DH_B, s), (0, 3, 1, 2))
    return x1, h2, eid, wcol, s_new, from_channel_major(kt), from_channel_major(vt)


def kernel(x_prompt, x_sample, state_gla, cache_sb_k, cache_sb_v, norm_mix_gain, w_in, w_gla_gate_up, b_gla_gate, gla_norm_gain, w_branch, w_out, norm_ffn_gain, w_router_group, b_router_group, w_router_expert, b_router_expert, w_exp_gate, w_exp_up, w_exp_down, norm_final_gain):
    depth = w_in.shape[0]
    assert depth == 1, "one trunk layer per step"
    l = 0
    w = _prepare_weights(w_in[l], w_gla_gate_up[l], b_gla_gate[l], w_branch[l], w_out[l], w_router_group[l],
                         b_router_group[l], w_router_expert[l], b_router_expert[l])
    bp, sp, d = x_prompt.shape
    bs, ss, _ = x_sample.shape
    s0 = jnp.zeros((bp, H_A, DK_A, DV_A), x_prompt.dtype)
    x1p, h2p, eidp, wcolp, gla_p, k_p, v_p = _mixers(
        x_prompt, s0, None, None, w, norm_mix_gain[l], gla_norm_gain[l], norm_ffn_gain[l])
    x1s, h2s, eids, wcols, gla_s, k_s, v_s = _mixers(
        x_sample, state_gla[l], cache_sb_k[l], cache_sb_v[l], w, norm_mix_gain[l], gla_norm_gain[l],
        norm_ffn_gain[l])

    tp, ts = bp * sp, bs * ss
    eid = jnp.concatenate([eidp, eids], axis=1)
    dest_blocks, counts = _positions(eid.reshape(-1, SORT_WIDTH))
    dest = dest_blocks.reshape(TOP_K, tp + ts)
    xs = _dispatch(dest, h2p, h2s)
    plan = _visit_plan(counts[:, 0], TOP_K * (tp + ts))
    ys = _experts(plan, xs, w_exp_gate[l], w_exp_up[l], w_exp_down[l])
    gf = norm_final_gain[None, :]
    y_prompt = _combine(dest, 0, ys, x1p, wcolp, gf).reshape(bp, sp, d)
    y_sample = _combine(dest, tp, ys, x1s, wcols, gf).reshape(bs, ss, d)
    return (y_prompt, y_sample, gla_p[None], k_p[None], v_p[None], gla_s[None], k_s[None], v_s[None])
```
